```python
import jax, jax.numpy as jnp
from jax import lax
import numpy as np

D_MODEL = 2048
BATCH = 8
SEQ = 2048
DEPTH = 1

CHUNK = 64
N_META = 16
D_CONF = D_MODEL // 2
D_SHORT = D_MODEL // 2
CONF_KERNEL = 31
SHORT_KERNEL = 3
D_FF = 4 * D_MODEL
IN_COLS = 2 * D_CONF + 3 * D_SHORT + 2 * D_MODEL
RMS_EPS = 1e-6
LN_EPS = 1e-5

kernel_name = "hybrid_gated_conformer_shortconv_block"


def rms_norm(x, g):
    xf = x.astype(jnp.float32)
    y = xf * lax.rsqrt(jnp.mean(xf * xf, axis=-1, keepdims=True) + RMS_EPS)
    return (y * g.astype(jnp.float32)).astype(x.dtype)


def layer_norm(x, g, b):
    xf = x.astype(jnp.float32)
    mu = jnp.mean(xf, axis=-1, keepdims=True)
    var = jnp.mean(jnp.square(xf - mu), axis=-1, keepdims=True)
    y = (xf - mu) * lax.rsqrt(var + LN_EPS)
    return (y * g.astype(jnp.float32) + b.astype(jnp.float32)).astype(x.dtype)


def causal_depthwise_conv(x, w, b=None):
    k = w.shape[0]
    y = lax.conv_general_dilated(
        x, w[:, None, :].astype(x.dtype),
        window_strides=(1,),
        padding=[(k - 1, 0)],
        dimension_numbers=("NWC", "WIO", "NWC"),
        feature_group_count=x.shape[-1])
    if b is not None:
        y = y + b.astype(x.dtype)
    return y


def _fwd_setup_inputs(seed: int = 0) -> dict:
    key = jax.random.key(seed)
    ks = jax.random.split(key, 24)
    f32 = jnp.float32

    def nrm(k, shape, scale):
        return jax.random.normal(k, shape, f32) * scale

    def gain(k, shape):
        return 1.0 + 0.05 * jax.random.normal(k, shape, f32)

    return {
        "x": jax.random.normal(ks[0], (BATCH, SEQ, D_MODEL), f32),
        "meta": nrm(ks[1], (N_META, D_MODEL), 1.0),
        "g_pre_mix": gain(ks[2], (DEPTH, D_MODEL)),
        "w_in": nrm(ks[3], (DEPTH, D_MODEL, IN_COLS), D_MODEL ** -0.5),
        "b_gates": nrm(ks[4], (DEPTH, 2 * D_MODEL), 0.1),
        "conf_dw_w": nrm(ks[5], (DEPTH, CONF_KERNEL, D_CONF), CONF_KERNEL ** -0.5),
        "conf_dw_b": nrm(ks[6], (DEPTH, D_CONF), 0.02),
        "conf_ln_g": gain(ks[7], (DEPTH, D_CONF)),
        "conf_ln_b": nrm(ks[8], (DEPTH, D_CONF), 0.02),
        "conf_w_pw": nrm(ks[9], (DEPTH, D_CONF, D_MODEL), D_CONF ** -0.5),
        "short_dw_w": nrm(ks[10], (DEPTH, SHORT_KERNEL, D_SHORT), SHORT_KERNEL ** -0.5),
        "short_w_out": nrm(ks[11], (DEPTH, D_SHORT, D_MODEL), D_SHORT ** -0.5),
        "w_o": nrm(ks[12], (DEPTH, D_MODEL, D_MODEL), D_MODEL ** -0.5),
        "g_post_mix": gain(ks[13], (DEPTH, D_MODEL)),
        "g_pre_mlp": gain(ks[14], (DEPTH, D_MODEL)),
        "w_up": nrm(ks[15], (DEPTH, D_MODEL, D_FF), D_MODEL ** -0.5),
        "w_down": nrm(ks[16], (DEPTH, D_FF, D_MODEL), D_FF ** -0.5),
        "g_post_mlp": gain(ks[17], (DEPTH, D_MODEL)),
    }


def _fwd_reference(x, meta, g_pre_mix, w_in, b_gates, conf_dw_w, conf_dw_b, conf_ln_g,
              conf_ln_b, conf_w_pw, short_dw_w, short_w_out, w_o, g_post_mix,
              g_pre_mlp, w_up, w_down, g_post_mlp):
    bsz = x.shape[0]
    meta_b = jnp.broadcast_to(meta.astype(x.dtype)[None], (bsz, N_META, D_MODEL))
    h = jnp.concatenate([meta_b, x], axis=1)

    for l in range(DEPTH):
        n = rms_norm(h, g_pre_mix[l])
        proj = jnp.einsum("btd,dc->btc", n, w_in[l])
        o1 = 2 * D_CONF
        o2 = o1 + 3 * D_SHORT
        u_a = proj[..., :o1]
        u_b = proj[..., o1:o2]
        gates = jax.nn.sigmoid(proj[..., o2:] + b_gates[l])
        gate_a = gates[..., :D_MODEL]
        gate_b = gates[..., D_MODEL:]

        a_val, a_gate = jnp.split(u_a, 2, axis=-1)
        a = a_val * jax.nn.sigmoid(a_gate)
        a = causal_depthwise_conv(a, conf_dw_w[l], conf_dw_b[l])
        a = jax.nn.silu(layer_norm(a, conf_ln_g[l], conf_ln_b[l]))
        y_a = jnp.einsum("btc,cd->btd", a, conf_w_pw[l])

        b_g, c_g, v = jnp.split(u_b, 3, axis=-1)
        s = b_g * causal_depthwise_conv(c_g * v, short_dw_w[l])
        y_b = jnp.einsum("btc,cd->btd", s, short_w_out[l])

        m = gate_a * y_a + gate_b * y_b
        mix = jnp.einsum("btd,de->bte", m, w_o[l])
        h = h + rms_norm(mix, g_post_mix[l])

        n2 = rms_norm(h, g_pre_mlp[l])
        f = jnp.square(jax.nn.relu(jnp.einsum("btd,df->btf", n2, w_up[l])))
        f = jnp.einsum("btf,fd->btd", f, w_down[l])
        h = h + rms_norm(f, g_post_mlp[l])

    return h[:, N_META:, :]


import jax as _jax
import jax.numpy as _jnp

TWIN_FORMAT = 'train_step'
FWD_PARAMS = ['x', 'meta', 'g_pre_mix', 'w_in', 'b_gates', 'conf_dw_w', 'conf_dw_b', 'conf_ln_g', 'conf_ln_b', 'conf_w_pw', 'short_dw_w', 'short_w_out', 'w_o', 'g_post_mix', 'g_pre_mlp', 'w_up', 'w_down', 'g_post_mlp']
TWIN_WEIGHTS = ['meta', 'g_pre_mix', 'w_in', 'b_gates', 'conf_dw_w', 'conf_dw_b', 'conf_ln_g', 'conf_ln_b', 'conf_w_pw', 'short_dw_w', 'short_w_out', 'w_o', 'g_post_mix', 'g_pre_mlp', 'w_up', 'w_down', 'g_post_mlp']
TWIN_DIFF_INPUT = 'x'
TWIN_INPUTS = ['x', 'meta', 'g_pre_mix', 'w_in', 'b_gates', 'conf_dw_w', 'conf_dw_b', 'conf_ln_g', 'conf_ln_b', 'conf_w_pw', 'short_dw_w', 'short_w_out', 'w_o', 'g_post_mix', 'g_pre_mlp', 'w_up', 'w_down', 'g_post_mlp', 'loss_target', 'm_meta', 'm_g_pre_mix', 'm_w_in', 'm_b_gates', 'm_conf_dw_w', 'm_conf_dw_b', 'm_conf_ln_g', 'm_conf_ln_b', 'm_conf_w_pw', 'm_short_dw_w', 'm_short_w_out', 'm_w_o', 'm_g_post_mix', 'm_g_pre_mlp', 'm_w_up', 'm_w_down', 'm_g_post_mlp', 'v_meta', 'v_g_pre_mix', 'v_w_in', 'v_b_gates', 'v_conf_dw_w', 'v_conf_dw_b', 'v_conf_ln_g', 'v_conf_ln_b', 'v_conf_w_pw', 'v_short_dw_w', 'v_short_w_out', 'v_w_o', 'v_g_post_mix', 'v_g_pre_mlp', 'v_w_up', 'v_w_down', 'v_g_post_mlp']
TWIN_OUTPUTS = ['loss', 'grad_x', 'grad_meta', 'grad_g_pre_mix', 'grad_w_in', 'grad_b_gates', 'grad_conf_dw_w', 'grad_conf_dw_b', 'grad_conf_ln_g', 'grad_conf_ln_b', 'grad_conf_w_pw', 'grad_short_dw_w', 'grad_short_w_out', 'grad_w_o', 'grad_g_post_mix', 'grad_g_pre_mlp', 'grad_w_up', 'grad_w_down', 'grad_g_post_mlp', 'delta_meta', 'delta_g_pre_mix', 'delta_w_in', 'delta_b_gates', 'delta_conf_dw_w', 'delta_conf_dw_b', 'delta_conf_ln_g', 'delta_conf_ln_b', 'delta_conf_w_pw', 'delta_short_dw_w', 'delta_short_w_out', 'delta_w_o', 'delta_g_post_mix', 'delta_g_pre_mlp', 'delta_w_up', 'delta_w_down', 'delta_g_post_mlp', 'new_m_meta', 'new_m_g_pre_mix', 'new_m_w_in', 'new_m_b_gates', 'new_m_conf_dw_w', 'new_m_conf_dw_b', 'new_m_conf_ln_g', 'new_m_conf_ln_b', 'new_m_conf_w_pw', 'new_m_short_dw_w', 'new_m_short_w_out', 'new_m_w_o', 'new_m_g_post_mix', 'new_m_g_pre_mlp', 'new_m_w_up', 'new_m_w_down', 'new_m_g_post_mlp', 'new_v_meta', 'new_v_g_pre_mix', 'new_v_w_in', 'new_v_b_gates', 'new_v_conf_dw_w', 'new_v_conf_dw_b', 'new_v_conf_ln_g', 'new_v_conf_ln_b', 'new_v_conf_w_pw', 'new_v_short_dw_w', 'new_v_short_w_out', 'new_v_w_o', 'new_v_g_post_mix', 'new_v_g_pre_mlp', 'new_v_w_up', 'new_v_w_down', 'new_v_g_post_mlp']
TWIN_LEAF_KINDS = {'loss': 'loss', 'grad_x': 'grad_x', 'grad_meta': 'grad_w', 'grad_g_pre_mix': 'grad_w', 'grad_w_in': 'grad_w', 'grad_b_gates': 'grad_w', 'grad_conf_dw_w': 'grad_w', 'grad_conf_dw_b': 'grad_w', 'grad_conf_ln_g': 'grad_w', 'grad_conf_ln_b': 'grad_w', 'grad_conf_w_pw': 'grad_w', 'grad_short_dw_w': 'grad_w', 'grad_short_w_out': 'grad_w', 'grad_w_o': 'grad_w', 'grad_g_post_mix': 'grad_w', 'grad_g_pre_mlp': 'grad_w', 'grad_w_up': 'grad_w', 'grad_w_down': 'grad_w', 'grad_g_post_mlp': 'grad_w', 'delta_meta': 'delta_w', 'delta_g_pre_mix': 'delta_w', 'delta_w_in': 'delta_w', 'delta_b_gates': 'delta_w', 'delta_conf_dw_w': 'delta_w', 'delta_conf_dw_b': 'delta_w', 'delta_conf_ln_g': 'delta_w', 'delta_conf_ln_b': 'delta_w', 'delta_conf_w_pw': 'delta_w', 'delta_short_dw_w': 'delta_w', 'delta_short_w_out': 'delta_w', 'delta_w_o': 'delta_w', 'delta_g_post_mix': 'delta_w', 'delta_g_pre_mlp': 'delta_w', 'delta_w_up': 'delta_w', 'delta_w_down': 'delta_w', 'delta_g_post_mlp': 'delta_w', 'new_m_meta': 'new_m', 'new_m_g_pre_mix': 'new_m', 'new_m_w_in': 'new_m', 'new_m_b_gates': 'new_m', 'new_m_conf_dw_w': 'new_m', 'new_m_conf_dw_b': 'new_m', 'new_m_conf_ln_g': 'new_m', 'new_m_conf_ln_b': 'new_m', 'new_m_conf_w_pw': 'new_m', 'new_m_short_dw_w': 'new_m', 'new_m_short_w_out': 'new_m', 'new_m_w_o': 'new_m', 'new_m_g_post_mix': 'new_m', 'new_m_g_pre_mlp': 'new_m', 'new_m_w_up': 'new_m', 'new_m_w_down': 'new_m', 'new_m_g_post_mlp': 'new_m', 'new_v_meta': 'new_v', 'new_v_g_pre_mix': 'new_v', 'new_v_w_in': 'new_v', 'new_v_b_gates': 'new_v', 'new_v_conf_dw_w': 'new_v', 'new_v_conf_dw_b': 'new_v', 'new_v_conf_ln_g': 'new_v', 'new_v_conf_ln_b': 'new_v', 'new_v_conf_w_pw': 'new_v', 'new_v_short_dw_w': 'new_v', 'new_v_short_w_out': 'new_v', 'new_v_w_o': 'new_v', 'new_v_g_post_mix': 'new_v', 'new_v_g_pre_mlp': 'new_v', 'new_v_w_up': 'new_v', 'new_v_w_down': 'new_v', 'new_v_g_post_mlp': 'new_v'}


def _forward(args):
    return _fwd_reference(*[args[k] for k in FWD_PARAMS])


def _output_shape():
    out = _jax.eval_shape(lambda: _forward(_fwd_setup_inputs(0)))
    return out.shape, out.dtype

N_MICROBATCH = 1
ADAM_LR = 0.001
ADAM_B1 = 0.9
ADAM_B2 = 0.999
ADAM_EPS = 1e-08
ADAM_WD = 0.01
ADAM_STEP = 10
PER_EXAMPLE_BATCH_AXIS = {'x': 0, 'loss_target': 0}
SHARED_INPUTS = []
_WEIGHT_DTYPES = {'meta': _jnp.float32, 'g_pre_mix': _jnp.float32, 'w_in': _jnp.float32, 'b_gates': _jnp.float32, 'conf_dw_w': _jnp.float32, 'conf_dw_b': _jnp.float32, 'conf_ln_g': _jnp.float32, 'conf_ln_b': _jnp.float32, 'conf_w_pw': _jnp.float32, 'short_dw_w': _jnp.float32, 'short_w_out': _jnp.float32, 'w_o': _jnp.float32, 'g_post_mix': _jnp.float32, 'g_pre_mlp': _jnp.float32, 'w_up': _jnp.float32, 'w_down': _jnp.float32, 'g_post_mlp': _jnp.float32}
MOMENT_SCALE = {'meta': 5.653309e-03, 'g_pre_mix': 2.263104e-01, 'w_in': 1.094097e-01, 'b_gates': 1.610179e-01, 'conf_dw_w': 2.113892e-01, 'conf_dw_b': 3.580702e+00, 'conf_ln_g': 1.303354e+00, 'conf_ln_b': 1.945057e+00, 'conf_w_pw': 5.652397e-01, 'short_dw_w': 1.733086e-01, 'short_w_out': 1.528146e-01, 'w_o': 5.363187e-01, 'g_post_mix': 8.077800e+00, 'g_pre_mlp': 2.149171e-01, 'w_up': 1.090265e-01, 'w_down': 5.709094e-01, 'g_post_mlp': 8.262614e+00}


def _to_microbatches(a, axis):
    t = _jnp.moveaxis(a, axis, 0)
    t = t.reshape((N_MICROBATCH, t.shape[0] // N_MICROBATCH) + t.shape[1:])
    return _jnp.moveaxis(t, 1, axis + 1)


def setup_inputs(seed: int = 0) -> dict:
    inp = _fwd_setup_inputs(seed)
    key = _jax.random.fold_in(_jax.random.key(seed), 7919)
    shape, _ = _output_shape()
    out = dict(inp)
    out["loss_target"] = _jax.random.normal(_jax.random.fold_in(key, 0), shape, _jnp.float32)
    for i, name in enumerate(TWIN_WEIGHTS):
        w = inp[name].astype(_jnp.float32)
        if MOMENT_SCALE is None:
            s = _jnp.sqrt(_jnp.mean(_jnp.square(w)) + 1e-30)
        else:
            s = MOMENT_SCALE[name]
        km, kv = _jax.random.split(_jax.random.fold_in(key, i + 1))
        out[name] = w
        out["m_" + name] = s * _jax.random.normal(km, w.shape, _jnp.float32)
        out["v_" + name] = (s * s) * _jax.random.uniform(kv, w.shape, _jnp.float32, 0.5, 1.5)
    if N_MICROBATCH > 1:
        for name, axis in PER_EXAMPLE_BATCH_AXIS.items():
            out[name] = _to_microbatches(out[name], axis)
    return {'x': out['x'], 'meta': out['meta'], 'g_pre_mix': out['g_pre_mix'], 'w_in': out['w_in'], 'b_gates': out['b_gates'], 'conf_dw_w': out['conf_dw_w'], 'conf_dw_b': out['conf_dw_b'], 'conf_ln_g': out['conf_ln_g'], 'conf_ln_b': out['conf_ln_b'], 'conf_w_pw': out['conf_w_pw'], 'short_dw_w': out['short_dw_w'], 'short_w_out': out['short_w_out'], 'w_o': out['w_o'], 'g_post_mix': out['g_post_mix'], 'g_pre_mlp': out['g_pre_mlp'], 'w_up': out['w_up'], 'w_down': out['w_down'], 'g_post_mlp': out['g_post_mlp'], 'loss_target': out['loss_target'], 'm_meta': out['m_meta'], 'm_g_pre_mix': out['m_g_pre_mix'], 'm_w_in': out['m_w_in'], 'm_b_gates': out['m_b_gates'], 'm_conf_dw_w': out['m_conf_dw_w'], 'm_conf_dw_b': out['m_conf_dw_b'], 'm_conf_ln_g': out['m_conf_ln_g'], 'm_conf_ln_b': out['m_conf_ln_b'], 'm_conf_w_pw': out['m_conf_w_pw'], 'm_short_dw_w': out['m_short_dw_w'], 'm_short_w_out': out['m_short_w_out'], 'm_w_o': out['m_w_o'], 'm_g_post_mix': out['m_g_post_mix'], 'm_g_pre_mlp': out['m_g_pre_mlp'], 'm_w_up': out['m_w_up'], 'm_w_down': out['m_w_down'], 'm_g_post_mlp': out['m_g_post_mlp'], 'v_meta': out['v_meta'], 'v_g_pre_mix': out['v_g_pre_mix'], 'v_w_in': out['v_w_in'], 'v_b_gates': out['v_b_gates'], 'v_conf_dw_w': out['v_conf_dw_w'], 'v_conf_dw_b': out['v_conf_dw_b'], 'v_conf_ln_g': out['v_conf_ln_g'], 'v_conf_ln_b': out['v_conf_ln_b'], 'v_conf_w_pw': out['v_conf_w_pw'], 'v_short_dw_w': out['v_short_dw_w'], 'v_short_w_out': out['v_short_w_out'], 'v_w_o': out['v_w_o'], 'v_g_post_mix': out['v_g_post_mix'], 'v_g_pre_mlp': out['v_g_pre_mlp'], 'v_w_up': out['v_w_up'], 'v_w_down': out['v_w_down'], 'v_g_post_mlp': out['v_g_post_mlp']}


def _loss(weights, diff, rest, loss_target):
    with _jax.named_scope("forward"):
        args = {**rest, TWIN_DIFF_INPUT: diff, **{k: w.astype(_WEIGHT_DTYPES[k]) for k, w in weights.items()}}
        y = _forward(args)
    with _jax.named_scope("loss_head"):
        err = _jnp.square(y.astype(_jnp.float32) - loss_target)
        return 0.5 * _jnp.sum(_jnp.mean(err, axis=-1)) if err.ndim else 0.5 * err


def _adamw(w, g, m, v):
    m = ADAM_B1 * m + (1.0 - ADAM_B1) * g
    v = ADAM_B2 * v + (1.0 - ADAM_B2) * _jnp.square(g)
    m_hat = m / (1.0 - ADAM_B1 ** ADAM_STEP)
    v_hat = v / (1.0 - ADAM_B2 ** ADAM_STEP)
    delta = -ADAM_LR * (m_hat / (_jnp.sqrt(v_hat) + ADAM_EPS) + ADAM_WD * w)
    return delta, m, v


def reference(x, meta, g_pre_mix, w_in, b_gates, conf_dw_w, conf_dw_b, conf_ln_g, conf_ln_b, conf_w_pw, short_dw_w, short_w_out, w_o, g_post_mix, g_pre_mlp, w_up, w_down, g_post_mlp, loss_target, m_meta, m_g_pre_mix, m_w_in, m_b_gates, m_conf_dw_w, m_conf_dw_b, m_conf_ln_g, m_conf_ln_b, m_conf_w_pw, m_short_dw_w, m_short_w_out, m_w_o, m_g_post_mix, m_g_pre_mlp, m_w_up, m_w_down, m_g_post_mlp, v_meta, v_g_pre_mix, v_w_in, v_b_gates, v_conf_dw_w, v_conf_dw_b, v_conf_ln_g, v_conf_ln_b, v_conf_w_pw, v_short_dw_w, v_short_w_out, v_w_o, v_g_post_mix, v_g_pre_mlp, v_w_up, v_w_down, v_g_post_mlp):
    given = dict(x=x, meta=meta, g_pre_mix=g_pre_mix, w_in=w_in, b_gates=b_gates, conf_dw_w=conf_dw_w, conf_dw_b=conf_dw_b, conf_ln_g=conf_ln_g, conf_ln_b=conf_ln_b, conf_w_pw=conf_w_pw, short_dw_w=short_dw_w, short_w_out=short_w_out, w_o=w_o, g_post_mix=g_post_mix, g_pre_mlp=g_pre_mlp, w_up=w_up, w_down=w_down, g_post_mlp=g_post_mlp, loss_target=loss_target, m_meta=m_meta, m_g_pre_mix=m_g_pre_mix, m_w_in=m_w_in, m_b_gates=m_b_gates, m_conf_dw_w=m_conf_dw_w, m_conf_dw_b=m_conf_dw_b, m_conf_ln_g=m_conf_ln_g, m_conf_ln_b=m_conf_ln_b, m_conf_w_pw=m_conf_w_pw, m_short_dw_w=m_short_dw_w, m_short_w_out=m_short_w_out, m_w_o=m_w_o, m_g_post_mix=m_g_post_mix, m_g_pre_mlp=m_g_pre_mlp, m_w_up=m_w_up, m_w_down=m_w_down, m_g_post_mlp=m_g_post_mlp, v_meta=v_meta, v_g_pre_mix=v_g_pre_mix, v_w_in=v_w_in, v_b_gates=v_b_gates, v_conf_dw_w=v_conf_dw_w, v_conf_dw_b=v_conf_dw_b, v_conf_ln_g=v_conf_ln_g, v_conf_ln_b=v_conf_ln_b, v_conf_w_pw=v_conf_w_pw, v_short_dw_w=v_short_dw_w, v_short_w_out=v_short_w_out, v_w_o=v_w_o, v_g_post_mix=v_g_post_mix, v_g_pre_mlp=v_g_pre_mlp, v_w_up=v_w_up, v_w_down=v_w_down, v_g_post_mlp=v_g_post_mlp)
    weights = {n: given[n] for n in TWIN_WEIGHTS}
    shared = {n: given[n] for n in SHARED_INPUTS}
    per_example = {n: given[n] for n in ['x']}
    grad_fn = _jax.value_and_grad(_loss, argnums=(0, 1))

    def one_microbatch(ex, loss_target):
        ex = dict(ex)
        diff = ex.pop(TWIN_DIFF_INPUT)
        return grad_fn(weights, diff, {**shared, **ex}, loss_target)

    if N_MICROBATCH == 1:
        loss, (grad_w, grad_x) = one_microbatch(per_example, given["loss_target"])
    else:
        def body(carry, xs):
            loss_sum, grad_sum = carry
            l_k, (gw_k, gx_k) = one_microbatch(xs[0], xs[1])
            with _jax.named_scope("update"):
                return (loss_sum + l_k, _jax.tree.map(_jnp.add, grad_sum, gw_k)), gx_k

        init = (_jnp.zeros((), _jnp.float32), _jax.tree.map(_jnp.zeros_like, weights))
        (loss, grad_w), grad_x = _jax.lax.scan(body, init, (per_example, given["loss_target"]))
    with _jax.named_scope("update"):
        delta_w, new_m, new_v = {}, {}, {}
        for n in TWIN_WEIGHTS:
            delta_w[n], new_m[n], new_v[n] = _adamw(weights[n], grad_w[n], given["m_" + n], given["v_" + n])
    return (loss, grad_x, *[grad_w[n] for n in TWIN_WEIGHTS], *[delta_w[n] for n in TWIN_WEIGHTS],
            *[new_m[n] for n in TWIN_WEIGHTS], *[new_v[n] for n in TWIN_WEIGHTS])
```

```python
import jax
import jax.numpy as jnp
from jax import lax
from jax.experimental import pallas as pl
from jax.experimental.pallas import tpu as pltpu

N_DEV = 8
N_META = 16
CONF_K = 31
SHORT_K = 3
RMS_EPS = 1e-6
LN_EPS = 1e-5
ADAM_LR = 0.001
ADAM_B1 = 0.9
ADAM_B2 = 0.999
ADAM_EPS = 1e-08
ADAM_WD = 0.01
ADAM_STEP = 10

LANE = 128
SUB = 8
ROW_TILE = 128
CONV_PAD = 32
CONV_CHUNK = 128
VMEM_LIMIT = 56 * 1024 * 1024

F32 = jnp.float32
BF16 = jnp.bfloat16
MESH = pl.DeviceIdType.MESH
ANY = pl.BlockSpec(memory_space=pl.ANY)


def _params(n_axes):
    return pltpu.CompilerParams(dimension_semantics=("arbitrary",) * n_axes, vmem_limit_bytes=VMEM_LIMIT)


def _sigmoid(z):
    return 1.0 / (1.0 + jnp.exp(-z))


def _colsum8(v):
    r, c = v.shape
    return jnp.sum(v.reshape(r // SUB, SUB, c), axis=0)


def _position():
    x, y, c = lax.axis_index("x"), lax.axis_index("y"), lax.axis_index("c")
    return x, y, c


def _flat(p):
    return 4 * p[0] + 2 * p[1] + p[2]


def _all_gather(name, shards):
    n = len(shards)

    def body(*refs):
        ins, outs = refs[:n], refs[n:2 * n]
        send_sems, recv_sems, local_sems = refs[2 * n:]
        x, y, c = _position()
        me, sibling = (x, y, c), (x, y, 1 - c)
        chips = [(1 - x, y), (x, 1 - y), (1 - x, 1 - y)]

        def copy(q, k, block, to, src=None):
            dst = outs[q].at[_flat(block)]
            return pltpu.make_async_remote_copy(
                src_ref=dst if src is None else src, dst_ref=dst,
                send_sem=send_sems.at[q, k], recv_sem=recv_sems.at[q, k],
                device_id=to, device_id_type=MESH)

        mine = [pltpu.make_async_copy(ins[q], outs[q].at[_flat(me)], local_sems.at[q]) for q in range(n)]
        for cp in mine:
            cp.start()
        first = []
        for q in range(n):
            first.append(copy(q, 0, me, sibling, src=ins[q]))
            for j, chip in enumerate(chips):
                first.append(copy(q, 1 + j, me, (*chip, c), src=ins[q]))
        for cp in first:
            cp.start()
        passed = []
        for q in range(n):
            for j, chip in enumerate(chips):
                copy(q, 1 + j, (*chip, c), me).wait_recv()
                fwd = copy(q, 4 + j, (*chip, c), sibling)
                fwd.start()
                passed.append(fwd)
        for q in range(n):
            copy(q, 0, sibling, me).wait_recv()
            for j, chip in enumerate(chips):
                copy(q, 4 + j, (*chip, 1 - c), me).wait_recv()
        for cp in first + passed:
            cp.wait_send()
        for cp in mine:
            cp.wait()

    return pl.pallas_call(
        body, name=name,
        in_specs=[ANY] * n, out_specs=[ANY] * n,
        out_shape=[jax.ShapeDtypeStruct((N_DEV,) + s.shape, s.dtype) for s in shards],
        scratch_shapes=[pltpu.SemaphoreType.DMA((n, 7)), pltpu.SemaphoreType.DMA((n, 7)),
                        pltpu.SemaphoreType.DMA((n,))],
    )(*shards)


def _exchange_blocks(name, fulls):
    n = len(fulls)

    def body(*refs):
        ins, outs = refs[:n], refs[n:2 * n]
        send_sems, recv_sems = refs[2 * n:]
        x, y, c = _position()
        me = (x, y, c)
        copies = []
        for q in range(n):
            for k in range(1, N_DEV):
                peer = (x ^ (k >> 2), y ^ ((k >> 1) & 1), c ^ (k & 1))
                copies.append(pltpu.make_async_remote_copy(
                    src_ref=ins[q].at[_flat(peer)], dst_ref=outs[q].at[_flat(me)],
                    send_sem=send_sems.at[q, k - 1], recv_sem=recv_sems.at[q, k - 1],
                    device_id=peer, device_id_type=MESH))
        for cp in copies:
            cp.start()
        for cp in copies:
            cp.wait_recv()
        for cp in copies:
            cp.wait_send()

    return pl.pallas_call(
        body, name=name,
        in_specs=[ANY] * n, out_specs=[ANY] * n,
        out_shape=[jax.ShapeDtypeStruct(f.shape, f.dtype) for f in fulls],
        scratch_shapes=[pltpu.SemaphoreType.DMA((n, 7)), pltpu.SemaphoreType.DMA((n, 7))],
    )(*fulls)


def _mm_cols(name, a, w, *, tm, nb=1, epilogue=None, out_dtypes=(F32,)):
    t, k = a.shape
    nblk, _, cb = w.shape

    def body(a_ref, w_ref, *o_refs):
        av = a_ref[...]
        for b in range(nb):
            acc = jnp.dot(av, w_ref[b], preferred_element_type=F32)
            outs = (acc,) if epilogue is None else epilogue(acc)
            for o_ref, o in zip(o_refs, outs):
                o_ref[:, b * cb:(b + 1) * cb] = o.astype(o_ref.dtype)

    return pl.pallas_call(
        body, name=name, grid=(nblk // nb, t // tm),
        in_specs=[pl.BlockSpec((tm, k), lambda j, i: (i, 0)),
                  pl.BlockSpec((nb, k, cb), lambda j, i: (j, 0, 0))],
        out_specs=[pl.BlockSpec((tm, nb * cb), lambda j, i: (i, j)) for _ in out_dtypes],
        out_shape=[jax.ShapeDtypeStruct((t, nblk * cb), dt) for dt in out_dtypes],
        compiler_params=_params(2),
    )(a, w)


def _mm_rows(name, a, w2d, *, tm, tk):
    t = a.shape[0]
    kf, n = w2d.shape

    def body(a_ref, w_ref, o_ref):
        acc = jnp.dot(a_ref[...], w_ref[...], preferred_element_type=F32)

        @pl.when(pl.program_id(1) == 0)
        def _():
            o_ref[...] = acc

        @pl.when(pl.program_id(1) > 0)
        def _():
            o_ref[...] += acc

    return pl.pallas_call(
        body, name=name, grid=(t // tm, kf // tk),
        in_specs=[pl.BlockSpec((tm, tk), lambda i, kk: (i, kk)),
                  pl.BlockSpec((tk, n), lambda i, kk: (kk, 0))],
        out_specs=pl.BlockSpec((tm, n), lambda i, kk: (i, 0)),
        out_shape=jax.ShapeDtypeStruct((t, n), F32),
        compiler_params=_params(2),
    )(a, w2d)


def _mm_nt_acc(name, dy, w, *, tm, nb=1, col_off=0):
    t = dy.shape[0]
    nblk, k, cb = w.shape

    def body(dy_ref, w_ref, o_ref):
        acc = None
        for b in range(nb):
            d = lax.dot_general(dy_ref[:, b * cb:(b + 1) * cb], w_ref[b], (((1,), (1,)), ((), ())),
                                preferred_element_type=F32)
            acc = d if acc is None else acc + d

        @pl.when(pl.program_id(1) == 0)
        def _():
            o_ref[...] = acc

        @pl.when(pl.program_id(1) > 0)
        def _():
            o_ref[...] += acc

    return pl.pallas_call(
        body, name=name, grid=(t // tm, nblk // nb),
        in_specs=[pl.BlockSpec((tm, nb * cb), lambda i, j: (i, col_off + j)),
                  pl.BlockSpec((nb, k, cb), lambda i, j: (j, 0, 0))],
        out_specs=pl.BlockSpec((tm, k), lambda i, j: (i, 0)),
        out_shape=jax.ShapeDtypeStruct((t, k), F32),
        compiler_params=_params(2),
    )(dy, w)


def _mm_nt_blocks(name, dy, w2d, *, tm, tkb, extra=(), epilogue=None, out_dtypes=(F32,)):
    t, n = dy.shape
    kf = w2d.shape[0]
    ne = len(extra)

    def body(dy_ref, w_ref, *rest):
        acc = lax.dot_general(dy_ref[...], w_ref[...], (((1,), (1,)), ((), ())), preferred_element_type=F32)
        outs = (acc,) if epilogue is None else epilogue(acc, *[e[...] for e in rest[:ne]])
        for o_ref, o in zip(rest[ne:], outs):
            o_ref[...] = o.astype(o_ref.dtype)

    return pl.pallas_call(
        body, name=name, grid=(kf // tkb, t // tm),
        in_specs=[pl.BlockSpec((tm, n), lambda kb, i: (i, 0)),
                  pl.BlockSpec((tkb, n), lambda kb, i: (kb, 0))]
                 + [pl.BlockSpec((tm, tkb), lambda kb, i: (i, kb)) for _ in extra],
        out_specs=[pl.BlockSpec((tm, tkb), lambda kb, i: (i, kb)) for _ in out_dtypes],
        out_shape=[jax.ShapeDtypeStruct((t, kf), dt) for dt in out_dtypes],
        compiler_params=_params(2),
    )(dy, w2d, *extra)


def _mm_tn(name, a, b, me_arr, *, m, n, tma, tn, sharded, a_off=0, b_off=0):
    t = a.shape[0]
    if sharded == "cols":
        cb = n // N_DEV
        q = cb // tn
        full_shape, own_shape = (N_DEV, m, cb), (m, cb)
        full_spec = pl.BlockSpec((1, tma, tn), lambda i, j, me: (j // q, i, j % q))
    else:
        kb = m // N_DEV
        p = kb // tma
        full_shape, own_shape = (m, n), (kb, n)
        full_spec = pl.BlockSpec((tma, tn), lambda i, j, me: (i, j))

    def body(me_ref, a_ref, b_ref, full_ref, own_ref, stage, sem):
        i, j = pl.program_id(0), pl.program_id(1)
        acc = lax.dot_general(a_ref[...], b_ref[...], (((0,), (0,)), ((), ())), preferred_element_type=F32)
        if sharded == "cols":
            full_ref[0] = acc.astype(BF16)
            owner, r0, c0 = j // q, i * tma, (j % q) * tn
        else:
            full_ref[...] = acc.astype(BF16)
            owner, r0, c0 = i // p, (i % p) * tma, j * tn

        @pl.when(owner == me_ref[0])
        def _():
            stage[...] = acc
            cp = pltpu.make_async_copy(
                stage, own_ref.at[pl.ds(pl.multiple_of(r0, tma), tma), pl.ds(pl.multiple_of(c0, tn), tn)], sem)
            cp.start()
            cp.wait()

    full, own = pl.pallas_call(
        body, name=name,
        grid_spec=pltpu.PrefetchScalarGridSpec(
            num_scalar_prefetch=1, grid=(m // tma, n // tn),
            in_specs=[pl.BlockSpec((t, tma), lambda i, j, me: (0, a_off + i)),
                      pl.BlockSpec((t, tn), lambda i, j, me: (0, b_off + j))],
            out_specs=[full_spec, ANY],
            scratch_shapes=[pltpu.VMEM((tma, tn), F32), pltpu.SemaphoreType.DMA(())]),
        out_shape=[jax.ShapeDtypeStruct(full_shape, BF16), jax.ShapeDtypeStruct(own_shape, F32)],
        compiler_params=_params(2),
    )(me_arr, a, b)
    if sharded == "rows":
        full = full.reshape(N_DEV, m // N_DEV, n)
    return full, own


def _row_call(name, body, t, row_ins, full_ins, row_outs, acc_outs, scratch=()):
    tm = ROW_TILE
    return pl.pallas_call(
        body, name=name, grid=(t // tm,),
        in_specs=[pl.BlockSpec((tm, a.shape[1]), lambda i: (i, 0)) for a in row_ins]
                 + [pl.BlockSpec(a.shape, lambda i: (0, 0)) for a in full_ins],
        out_specs=[pl.BlockSpec((tm, c), lambda i: (i, 0)) for c, _ in row_outs]
                  + [pl.BlockSpec((r, c), lambda i: (0, 0)) for r, c in acc_outs],
        out_shape=[jax.ShapeDtypeStruct((t, c), dt) for c, dt in row_outs]
                  + [jax.ShapeDtypeStruct((r, c), F32) for r, c in acc_outs],
        scratch_shapes=list(scratch),
        compiler_params=_params(1),
    )(*row_ins, *full_ins)


def _accumulate(ref, v):
    @pl.when(pl.program_id(0) == 0)
    def _():
        ref[...] = v

    @pl.when(pl.program_id(0) > 0)
    def _():
        ref[...] += v


def _rms(v):
    return lax.rsqrt(jnp.mean(v * v, axis=-1, keepdims=True) + RMS_EPS)


def _rms_bwd(dout, u, r, g):
    du = dout * g
    dx = r * (du - u * jnp.mean(du * u, axis=-1, keepdims=True))
    return dx, _colsum8(dout * u)


def _pre_norm(h0, g):
    t, d = h0.shape

    def body(h_ref, g_ref, n_ref):
        h = h_ref[...]
        n_ref[...] = (h * _rms(h) * g_ref[...]).astype(BF16)

    return _row_call("pre_norm", body, t, [h0], [g], [(d, BF16)], [])[0]


def _post_mix(mix, h0, g_post, g_pre):
    t, d = h0.shape

    def body(mix_ref, h0_ref, gp_ref, gq_ref, h1_ref, n2_ref):
        mix_v = mix_ref[...]
        h1 = h0_ref[...] + mix_v * _rms(mix_v) * gp_ref[...]
        h1_ref[...] = h1
        n2_ref[...] = (h1 * _rms(h1) * gq_ref[...]).astype(BF16)

    return _row_call("post_mix", body, t, [mix, h0], [g_post, g_pre], [(d, F32), (d, BF16)], [])


def _loss_head(fo, h1, tgt, g_post_mlp, t_real):
    t, d = h1.shape

    def body(fo_ref, h1_ref, tgt_ref, g_ref, dfo_ref, dh2_ref, dg_ref, loss_ref, lacc):
        i = pl.program_id(0)
        fo_v = fo_ref[...]
        g = g_ref[...]
        r = _rms(fo_v)
        u = fo_v * r
        h2 = h1_ref[...] + u * g
        row = i * ROW_TILE + lax.broadcasted_iota(jnp.int32, (ROW_TILE, 1), 0)
        valid = jnp.logical_and(row >= N_META, row < t_real)
        diff = jnp.where(valid, h2 - tgt_ref[...], 0.0)
        dh2 = diff * (1.0 / d)
        dh2_ref[...] = dh2
        dfo, dg = _rms_bwd(dh2, u, r, g)
        dfo_ref[...] = dfo.astype(BF16)
        _accumulate(dg_ref, dg)
        _accumulate(lacc, _colsum8(diff * diff))

        @pl.when(i == pl.num_programs(0) - 1)
        def _():
            loss_ref[...] = jnp.full((SUB, LANE), (0.5 / d) * jnp.sum(lacc[...]), F32)

    return _row_call("loss_head", body, t, [fo, h1, tgt], [g_post_mlp],
                     [(d, BF16), (d, F32)], [(SUB, d), (SUB, LANE)], scratch=[pltpu.VMEM((SUB, d), F32)])


def _mid_norm_bwd(dn2, h1, dh2, mix, g_pre_mlp, g_post_mix):
    t, d = h1.shape

    def body(dn2_ref, h1_ref, dh2_ref, mix_ref, gq_ref, gp_ref, dh1_ref, dmix_ref, dgq_ref, dgp_ref):
        h1 = h1_ref[...]
        r3 = _rms(h1)
        dx, dgq = _rms_bwd(dn2_ref[...], h1 * r3, r3, gq_ref[...])
        dh1 = dh2_ref[...] + dx
        dh1_ref[...] = dh1
        mix_v = mix_ref[...]
        r2 = _rms(mix_v)
        dmix, dgp = _rms_bwd(dh1, mix_v * r2, r2, gp_ref[...])
        dmix_ref[...] = dmix.astype(BF16)
        _accumulate(dgq_ref, dgq)
        _accumulate(dgp_ref, dgp)

    return _row_call("mid_norm_bwd", body, t, [dn2, h1, dh2, mix], [g_pre_mlp, g_post_mix],
                     [(d, F32), (d, BF16)], [(SUB, d), (SUB, d)])


def _pre_norm_bwd(dn, h0, dh1, g_pre_mix):
    t, d = h0.shape

    def body(dn_ref, h0_ref, dh1_ref, g_ref, dh0_ref, dg_ref):
        h0 = h0_ref[...]
        r = _rms(h0)
        dx, dg = _rms_bwd(dn_ref[...], h0 * r, r, g_ref[...])
        dh0_ref[...] = dh1_ref[...] + dx
        _accumulate(dg_ref, dg)

    return _row_call("pre_norm_bwd", body, t, [dn, h0, dh1], [g_pre_mix], [(d, F32)], [(SUB, d)])


def _layer_norm_silu(a1, ln_g, ln_b):
    t, c = a1.shape

    def body(a1_ref, g_ref, b_ref, a3_ref):
        a = a1_ref[...]
        mu = jnp.mean(a, axis=-1, keepdims=True)
        xc = a - mu
        rstd = lax.rsqrt(jnp.mean(xc * xc, axis=-1, keepdims=True) + LN_EPS)
        z = xc * rstd * g_ref[...] + b_ref[...]
        a3_ref[...] = (z * _sigmoid(z)).astype(BF16)

    return _row_call("layer_norm_silu", body, t, [a1], [ln_g, ln_b], [(c, BF16)], [])[0]


def _layer_norm_silu_bwd(da3, a1, ln_g, ln_b):
    t, c = a1.shape

    def body(da3_ref, a1_ref, g_ref, b_ref, da1_ref, dg_ref, db_ref):
        a = a1_ref[...]
        g = g_ref[...]
        mu = jnp.mean(a, axis=-1, keepdims=True)
        xc = a - mu
        rstd = lax.rsqrt(jnp.mean(xc * xc, axis=-1, keepdims=True) + LN_EPS)
        xhat = xc * rstd
        z = xhat * g + b_ref[...]
        sg = _sigmoid(z)
        dz = da3_ref[...] * (sg * (1.0 + z * (1.0 - sg)))
        dxhat = dz * g
        da1_ref[...] = rstd * (dxhat - jnp.mean(dxhat, axis=-1, keepdims=True)
                               - xhat * jnp.mean(dxhat * xhat, axis=-1, keepdims=True))
        _accumulate(dg_ref, _colsum8(dz * xhat))
        _accumulate(db_ref, _colsum8(dz))

    return _row_call("layer_norm_silu_bwd", body, t, [da3, a1], [ln_g, ln_b], [(c, F32)], [(SUB, c), (SUB, c)])


def _gate_merge(proj, ya, yb, b_gates, d):
    t = proj.shape[0]
    w = 1024
    nh = d // w
    ga0 = (proj.shape[1] - 2 * d) // w

    def body(pa_ref, pb_ref, ya_ref, yb_ref, ba_ref, bb_ref, m_ref):
        ga = _sigmoid(pa_ref[...] + ba_ref[...])
        gb = _sigmoid(pb_ref[...] + bb_ref[...])
        m_ref[...] = (ga * ya_ref[...] + gb * yb_ref[...]).astype(BF16)

    tm = ROW_TILE
    return pl.pallas_call(
        body, name="gate_merge", grid=(nh, t // tm),
        in_specs=[pl.BlockSpec((tm, w), lambda h, i: (i, ga0 + h)),
                  pl.BlockSpec((tm, w), lambda h, i: (i, ga0 + nh + h)),
                  pl.BlockSpec((tm, w), lambda h, i: (i, h)),
                  pl.BlockSpec((tm, w), lambda h, i: (i, h)),
                  pl.BlockSpec((1, w), lambda h, i: (0, h)),
                  pl.BlockSpec((1, w), lambda h, i: (0, nh + h))],
        out_specs=pl.BlockSpec((tm, w), lambda h, i: (i, h)),
        out_shape=jax.ShapeDtypeStruct((t, d), BF16),
        compiler_params=_params(2),
    )(proj, proj, ya, yb, b_gates, b_gates)


def _gate_merge_bwd(dm, proj, ya, yb, b_gates, d):
    t, cols = proj.shape
    w = 1024
    nh = d // w
    ga0 = (cols - 2 * d) // w

    def body(dm_ref, p_ref, ya_ref, yb_ref, b_ref, dy_ref, dp_ref, db_ref):
        q = pl.program_id(0)
        g = _sigmoid(p_ref[...] + b_ref[...])
        dm_v = dm_ref[...]
        y = jnp.where(q < nh, ya_ref[...], yb_ref[...])
        dy_ref[...] = (dm_v * g).astype(BF16)
        dp = dm_v * y * g * (1.0 - g)
        dp_ref[...] = dp.astype(BF16)

        @pl.when(pl.program_id(1) == 0)
        def _():
            db_ref[...] = _colsum8(dp)

        @pl.when(pl.program_id(1) > 0)
        def _():
            db_ref[...] += _colsum8(dp)

    tm = ROW_TILE
    return pl.pallas_call(
        body, name="gate_merge_bwd", grid=(2 * nh, t // tm),
        in_specs=[pl.BlockSpec((tm, w), lambda q, i: (i, q % nh)),
                  pl.BlockSpec((tm, w), lambda q, i: (i, ga0 + q)),
                  pl.BlockSpec((tm, w), lambda q, i: (i, q % nh)),
                  pl.BlockSpec((tm, w), lambda q, i: (i, q % nh)),
                  pl.BlockSpec((1, w), lambda q, i: (0, q))],
        out_specs=[pl.BlockSpec((tm, w), lambda q, i: (i, q)),
                   pl.BlockSpec((tm, w), lambda q, i: (i, ga0 + q)),
                   pl.BlockSpec((SUB, w), lambda q, i: (0, q))],
        out_shape=[jax.ShapeDtypeStruct((t, 2 * d), BF16), jax.ShapeDtypeStruct((t, cols), BF16),
                   jax.ShapeDtypeStruct((SUB, 2 * d), F32)],
        compiler_params=_params(2),
    )(dm, proj, ya, yb, b_gates)


def _causal_conv(xp_ref, w_ref, ntap, r0):
    n = CONV_CHUNK + CONV_PAD
    win = xp_ref[pl.ds(r0, n), :]
    acc = None
    for k in range(ntap):
        back = ntap - 1 - k
        shifted = pltpu.roll(win, n - (CONV_PAD - back), 0)
        term = w_ref[k:k + 1, :] * shifted[:CONV_CHUNK]
        acc = term if acc is None else acc + term
    return acc


def _anticausal_conv(xp_ref, w_ref, ntap, r0):
    n = CONV_CHUNK + CONV_PAD
    win = xp_ref[pl.ds(pl.multiple_of(CONV_PAD + r0, CONV_PAD), n), :]
    acc = None
    for k in range(ntap):
        ahead = ntap - 1 - k
        shifted = win if ahead == 0 else pltpu.roll(win, n - ahead, 0)
        term = w_ref[k:k + 1, :] * shifted[:CONV_CHUNK]
        acc = term if acc is None else acc + term
    return acc


def _conv_weight_grad(dw_ref, d_chunk, xp_ref, ntap, r0):
    n = CONV_CHUNK + CONV_PAD
    win = xp_ref[pl.ds(r0, n), :]
    for k in range(ntap):
        back = ntap - 1 - k
        shifted = pltpu.roll(win, n - (CONV_PAD - back), 0)
        dw_ref[k * SUB:(k + 1) * SUB, :] += _colsum8(d_chunk * shifted[:CONV_CHUNK])


def _zero_pads(ref, t):
    ref[0:CONV_PAD, :] = jnp.zeros((CONV_PAD, LANE), F32)
    ref[CONV_PAD + t:CONV_PAD + t + CONV_PAD, :] = jnp.zeros((CONV_PAD, LANE), F32)


def _for_chunks(t, fn):
    def step(idx, carry):
        fn(pl.multiple_of(idx * CONV_CHUNK, CONV_CHUNK))
        return carry

    lax.fori_loop(0, t // CONV_CHUNK, step, 0)


def _conv_forward(proj, conf_w, conf_b, short_w, dc):
    t = proj.shape[0]
    nc = dc // LANE

    def body(av_ref, ag_ref, bg_ref, cg_ref, v_ref, cw_ref, cb_ref, sw_ref, a1_ref, s_ref, xa, xb):
        _zero_pads(xa, t)
        _zero_pads(xb, t)
        xa[CONV_PAD:CONV_PAD + t, :] = av_ref[...] * _sigmoid(ag_ref[...])
        xb[CONV_PAD:CONV_PAD + t, :] = cg_ref[...] * v_ref[...]

        def chunk(r0):
            rs = pl.ds(r0, CONV_CHUNK)
            a1_ref[rs, :] = _causal_conv(xa, cw_ref, CONF_K, r0) + cb_ref[...]
            s_ref[rs, :] = (bg_ref[rs, :] * _causal_conv(xb, sw_ref, SHORT_K, r0)).astype(BF16)

        _for_chunks(t, chunk)

    col = lambda g: pl.BlockSpec((t, LANE), lambda c, g=g: (0, g * nc + c))
    return pl.pallas_call(
        body, name="conv_forward", grid=(nc,),
        in_specs=[col(0), col(1), col(2), col(3), col(4),
                  pl.BlockSpec((CONF_K, LANE), lambda c: (0, c)),
                  pl.BlockSpec((1, LANE), lambda c: (0, c)),
                  pl.BlockSpec((SHORT_K, LANE), lambda c: (0, c))],
        out_specs=[pl.BlockSpec((t, LANE), lambda c: (0, c)), pl.BlockSpec((t, LANE), lambda c: (0, c))],
        out_shape=[jax.ShapeDtypeStruct((t, dc), F32), jax.ShapeDtypeStruct((t, dc), BF16)],
        scratch_shapes=[pltpu.VMEM((t + 2 * CONV_PAD, LANE), F32), pltpu.VMEM((t + 2 * CONV_PAD, LANE), F32)],
        compiler_params=_params(1),
    )(proj, proj, proj, proj, proj, conf_w, conf_b, short_w)


def _conv_backward(dproj, proj, da1, ds, conf_w, short_w, dc):
    t = proj.shape[0]
    nc = dc // LANE

    def body(dp_in, av_ref, ag_ref, bg_ref, cg_ref, v_ref, da1_ref, ds_ref, cw_ref, sw_ref,
             dp_ref, dcw_ref, dcb_ref, dsw_ref, xa, xb, da, db, stage, sems):
        del dp_in
        c = pl.program_id(0)
        for ref in (xa, xb, da, db):
            _zero_pads(ref, t)
        xa[CONV_PAD:CONV_PAD + t, :] = av_ref[...] * _sigmoid(ag_ref[...])
        xb[CONV_PAD:CONV_PAD + t, :] = cg_ref[...] * v_ref[...]
        da[CONV_PAD:CONV_PAD + t, :] = da1_ref[...]
        dcw_ref[...] = jnp.zeros(dcw_ref.shape, F32)
        dsw_ref[...] = jnp.zeros(dsw_ref.shape, F32)
        dcb_ref[...] = jnp.zeros(dcb_ref.shape, F32)

        def through_gate(r0):
            rs = pl.ds(r0, CONV_CHUNK)
            ds_c = ds_ref[rs, :]
            stage[2, rs, :] = (ds_c * _causal_conv(xb, sw_ref, SHORT_K, r0)).astype(BF16)
            db[pl.ds(pl.multiple_of(CONV_PAD + r0, CONV_PAD), CONV_CHUNK), :] = ds_c * bg_ref[rs, :]

        _for_chunks(t, through_gate)

        def through_convs(r0):
            rs = pl.ds(r0, CONV_CHUNK)
            da0 = _anticausal_conv(da, cw_ref, CONF_K, r0)
            sg = _sigmoid(ag_ref[rs, :])
            stage[0, rs, :] = (da0 * sg).astype(BF16)
            stage[1, rs, :] = (da0 * av_ref[rs, :] * sg * (1.0 - sg)).astype(BF16)
            dcv = _anticausal_conv(db, sw_ref, SHORT_K, r0)
            stage[3, rs, :] = (dcv * v_ref[rs, :]).astype(BF16)
            stage[4, rs, :] = (dcv * cg_ref[rs, :]).astype(BF16)
            da1_c = da1_ref[rs, :]
            _conv_weight_grad(dcw_ref, da1_c, xa, CONF_K, r0)
            _conv_weight_grad(dsw_ref, ds_ref[rs, :] * bg_ref[rs, :], xb, SHORT_K, r0)
            dcb_ref[...] += _colsum8(da1_c)

        _for_chunks(t, through_convs)
        copies = [pltpu.make_async_copy(
            stage.at[g], dp_ref.at[:, pl.ds(pl.multiple_of((g * nc + c) * LANE, LANE), LANE)], sems.at[g])
            for g in range(5)]
        for cp in copies:
            cp.start()
        for cp in copies:
            cp.wait()

    col = lambda g: pl.BlockSpec((t, LANE), lambda c, g=g: (0, g * nc + c))
    blk = pl.BlockSpec((t, LANE), lambda c: (0, c))
    return pl.pallas_call(
        body, name="conv_backward", grid=(nc,),
        in_specs=[ANY, col(0), col(1), col(2), col(3), col(4), blk, blk,
                  pl.BlockSpec((CONF_K, LANE), lambda c: (0, c)),
                  pl.BlockSpec((SHORT_K, LANE), lambda c: (0, c))],
        out_specs=[ANY,
                   pl.BlockSpec((CONF_K * SUB, LANE), lambda c: (0, c)),
                   pl.BlockSpec((SUB, LANE), lambda c: (0, c)),
                   pl.BlockSpec((SHORT_K * SUB, LANE), lambda c: (0, c))],
        out_shape=[jax.ShapeDtypeStruct(dproj.shape, dproj.dtype),
                   jax.ShapeDtypeStruct((CONF_K * SUB, dc), F32),
                   jax.ShapeDtypeStruct((SUB, dc), F32),
                   jax.ShapeDtypeStruct((SHORT_K * SUB, dc), F32)],
        scratch_shapes=[pltpu.VMEM((t + 2 * CONV_PAD, LANE), F32)] * 4
                       + [pltpu.VMEM((5, t, LANE), BF16), pltpu.SemaphoreType.DMA((5,))],
        input_output_aliases={0: 0},
        compiler_params=_params(1),
    )(dproj, proj, proj, proj, proj, proj, da1, ds, conf_w, short_w)


def _adamw_math(w, g, m, v):
    m = ADAM_B1 * m + (1.0 - ADAM_B1) * g
    v = ADAM_B2 * v + (1.0 - ADAM_B2) * (g * g)
    m_hat = m / (1.0 - ADAM_B1 ** ADAM_STEP)
    v_hat = v / (1.0 - ADAM_B2 ** ADAM_STEP)
    delta = -ADAM_LR * (m_hat / (jnp.sqrt(v_hat) + ADAM_EPS) + ADAM_WD * w)
    return delta, m, v


def _cast_bf16(name, w):
    r, c = w.shape
    tr = 256

    def body(w_ref, o_ref):
        o_ref[...] = w_ref[...].astype(BF16)

    return pl.pallas_call(
        body, name=name, grid=(r // tr,),
        in_specs=[pl.BlockSpec((tr, c), lambda i: (i, 0))],
        out_specs=pl.BlockSpec((tr, c), lambda i: (i, 0)),
        out_shape=jax.ShapeDtypeStruct((r, c), BF16),
        compiler_params=_params(1),
    )(w)


def _adamw_shard(name, w, m, v, g_own, landed, me_arr):
    r, c = w.shape
    tr = 128

    def body(me_ref, w_ref, m_ref, v_ref, g_ref, *rest):
        lands, (g_out, d_out, m_out, v_out) = rest[:N_DEV - 1], rest[N_DEV - 1:]
        g = g_ref[...]
        for l_ref in lands:
            g = g + l_ref[0].astype(F32)
        delta, m_new, v_new = _adamw_math(w_ref[...], g, m_ref[...], v_ref[...])
        g_out[...] = g
        d_out[...] = delta
        m_out[...] = m_new
        v_out[...] = v_new

    tile = pl.BlockSpec((tr, c), lambda i, me: (i, 0))
    land_specs = [pl.BlockSpec((1, tr, c), lambda i, me, k=k: ((me[0] + k) % N_DEV, i, 0)) for k in range(1, N_DEV)]
    return pl.pallas_call(
        body, name=name,
        grid_spec=pltpu.PrefetchScalarGridSpec(
            num_scalar_prefetch=1, grid=(r // tr,),
            in_specs=[tile] * 4 + land_specs, out_specs=[tile] * 4),
        out_shape=[jax.ShapeDtypeStruct((r, c), F32)] * 4,
        compiler_params=_params(1),
    )(me_arr, w, m, v, g_own, *([landed] * (N_DEV - 1)))


SMALL_W = 1024
VEC_ROWS = 16
META_ROW0 = 16
CONF_ROW0 = 64
SHORT_ROW0 = 96
SMALL_ROWS = 104


def _pack_small(vec_parts, dmeta, dcw, dsw):
    widths = [p.shape[1] for p in vec_parts]
    nv = len(vec_parts)

    def body(*refs):
        parts, (dmeta_ref, dcw_ref, dsw_ref, out_ref) = refs[:nv], refs[nv:]
        out_ref[...] = jnp.zeros((SMALL_ROWS, SMALL_W), F32)
        row = 0
        for p_ref, wd in zip(parts, widths):
            s = jnp.sum(p_ref[...], axis=0, keepdims=True)
            for h in range(wd // SMALL_W):
                out_ref[row:row + 1, :] = s[:, h * SMALL_W:(h + 1) * SMALL_W]
                row += 1
        for h in range(dmeta_ref.shape[1] // SMALL_W):
            out_ref[META_ROW0 + h * N_META:META_ROW0 + (h + 1) * N_META, :] = dmeta_ref[:, h * SMALL_W:(h + 1) * SMALL_W]
        for k in range(CONF_K):
            out_ref[CONF_ROW0 + k:CONF_ROW0 + k + 1, :] = jnp.sum(dcw_ref[k * SUB:(k + 1) * SUB, :], axis=0, keepdims=True)
        for k in range(SHORT_K):
            out_ref[SHORT_ROW0 + k:SHORT_ROW0 + k + 1, :] = jnp.sum(dsw_ref[k * SUB:(k + 1) * SUB, :], axis=0, keepdims=True)

    return pl.pallas_call(
        body, name="pack_small",
        out_shape=jax.ShapeDtypeStruct((SMALL_ROWS, SMALL_W), F32),
        compiler_params=pltpu.CompilerParams(vmem_limit_bytes=VMEM_LIMIT),
    )(*vec_parts, dmeta, dcw, dsw)


def _small_update(gathered, me_arr, vec_params, meta_p, conf_p, short_p):
    widths = [p[0].shape[1] for p in vec_params]
    nv = len(vec_params)
    mcols = meta_p[0].shape[1]
    per_row = SMALL_W // mcols

    def body(me_ref, gv_ref, gm_ref, gc_ref, gs_ref, *rest):
        del me_ref
        ins, outs = rest[:3 * (nv + 3)], rest[3 * (nv + 3):]

        def total(ref, r0, rows):
            s = ref[0, r0:r0 + rows, :]
            for dev in range(1, N_DEV):
                s = s + ref[dev, r0:r0 + rows, :]
            return s

        grads = []
        row = 0
        for wd in widths:
            pieces = [total(gv_ref, row + h, 1) for h in range(wd // SMALL_W)]
            grads.append(pieces[0] if len(pieces) == 1 else jnp.concatenate(pieces, axis=1))
            row += len(pieces)
        grads.append(total(gm_ref, 0, N_META))
        grads.append(total(gc_ref, 0, CONF_K))
        grads.append(total(gs_ref, 0, SHORT_K))
        for idx, g in enumerate(grads):
            w_ref, m_ref, v_ref = ins[3 * idx:3 * idx + 3]
            delta, m_new, v_new = _adamw_math(w_ref[...], g, m_ref[...], v_ref[...])
            g_out, d_out, m_out, v_out = outs[4 * idx:4 * idx + 4]
            g_out[...] = g
            d_out[...] = delta
            m_out[...] = m_new
            v_out[...] = v_new

    params = list(vec_params) + [meta_p, conf_p, short_p]
    flat = [a for p in params for a in p]
    whole = lambda a: pl.BlockSpec(a.shape, lambda i, me: (0,) * a.ndim)
    outs = pl.pallas_call(
        body, name="small_update",
        grid_spec=pltpu.PrefetchScalarGridSpec(
            num_scalar_prefetch=1, grid=(1,),
            in_specs=[pl.BlockSpec((N_DEV, VEC_ROWS, SMALL_W), lambda i, me: (0, 0, 0)),
                      pl.BlockSpec((N_DEV, N_META, mcols),
                                   lambda i, me: (0, META_ROW0 // N_META + me[0] // per_row, me[0] % per_row)),
                      pl.BlockSpec((N_DEV, 32, LANE), lambda i, me: (0, CONF_ROW0 // 32, me[0])),
                      pl.BlockSpec((N_DEV, SUB, LANE), lambda i, me: (0, SHORT_ROW0 // SUB, me[0]))]
                     + [whole(a) for a in flat],
            out_specs=[whole(p[0]) for p in params for _ in range(4)]),
        out_shape=[jax.ShapeDtypeStruct(p[0].shape, F32) for p in params for _ in range(4)],
        compiler_params=_params(1),
    )(me_arr, gathered, gathered, gathered, gathered, *flat)
    return [tuple(outs[4 * i:4 * i + 4]) for i in range(len(params))]


def kernel(x, meta, g_pre_mix, w_in, b_gates, conf_dw_w, conf_dw_b, conf_ln_g, conf_ln_b, conf_w_pw, short_dw_w, short_w_out, w_o, g_post_mix, g_pre_mlp, w_up, w_down, g_post_mlp, loss_target, m_meta, m_g_pre_mix, m_w_in, m_b_gates, m_conf_dw_w, m_conf_dw_b, m_conf_ln_g, m_conf_ln_b, m_conf_w_pw, m_short_dw_w, m_short_w_out, m_w_o, m_g_post_mix, m_g_pre_mlp, m_w_up, m_w_down, m_g_post_mlp, v_meta, v_g_pre_mix, v_w_in, v_b_gates, v_conf_dw_w, v_conf_dw_b, v_conf_ln_g, v_conf_ln_b, v_conf_w_pw, v_short_dw_w, v_short_w_out, v_w_o, v_g_post_mix, v_g_pre_mlp, v_w_up, v_w_down, v_g_post_mlp):
    seq, d = x.shape[1], x.shape[2]
    dc = conf_w_pw.shape[1]
    t_real = N_META + seq
    t = -(-t_real // ROW_TILE) * ROW_TILE
    tm = t // 2
    assert tm % 16 == 0 and d % 1024 == 0 and dc % 1024 == 0
    x_idx, y_idx, c_idx = _position()
    me_arr = jnp.reshape(4 * x_idx + 2 * y_idx + c_idx, (1,)).astype(jnp.int32)

    big = [w_in[0], conf_w_pw[0], short_w_out[0], w_o[0], w_up[0], w_down[0]]
    big_names = ["w_in", "conf_w_pw", "short_w_out", "w_o", "w_up", "w_down"]
    big_bf = [_cast_bf16("cast_" + nm, w) for nm, w in zip(big_names, big)]
    gathered = _all_gather("gather_weights", big_bf + [meta, conf_dw_w[0], short_dw_w[0]])
    win_g, wpw_g, wso_g, wo_g, wup_g, wdn_g, meta_g, cw_g, sw_g = gathered
    wo_full = wo_g.reshape(d, d)
    wdn_full = wdn_g.reshape(-1, d)
    unshard = lambda g: jnp.transpose(g, (1, 0, 2)).reshape(g.shape[1], -1)
    meta_full, cw_full, sw_full = unshard(meta_g), unshard(cw_g), unshard(sw_g)

    zrows = jnp.zeros((t - t_real, d), F32)
    h0 = jnp.concatenate([meta_full, x[0], zrows], axis=0)
    tgt = jnp.concatenate([jnp.zeros((N_META, d), F32), loss_target[0], zrows], axis=0)
    n = _pre_norm(h0, g_pre_mix)
    proj = _mm_cols("proj", n, win_g, tm=tm)[0]
    a1, s = _conv_forward(proj, cw_full, conf_dw_b, sw_full, dc)
    a3 = _layer_norm_silu(a1, conf_ln_g, conf_ln_b)
    ya = _mm_cols("y_a", a3, wpw_g, tm=tm, nb=N_DEV)[0]
    yb = _mm_cols("y_b", s, wso_g, tm=tm, nb=N_DEV)[0]
    m_mix = _gate_merge(proj, ya, yb, b_gates, d)
    mix = _mm_rows("mix", m_mix, wo_full, tm=tm // 2, tk=d)
    h1, n2 = _post_mix(mix, h0, g_post_mix, g_pre_mlp)

    def up_epilogue(acc):
        r = jnp.maximum(acc, 0.0)
        return r * r, r

    f, relu_up = _mm_cols("mlp_up", n2, wup_g, tm=tm, epilogue=up_epilogue, out_dtypes=(BF16, BF16))
    fo = _mm_rows("mlp_down", f, wdn_full, tm=tm, tk=1024)
    dfo, dh2, dg_post_mlp, loss_blk = _loss_head(fo, h1, tgt, g_post_mlp, t_real)
    loss = lax.psum(loss_blk[0, 0], ("x", "y", "c"))

    dup = _mm_nt_blocks("d_up", dfo, wdn_full, tm=tm, tkb=1024, extra=(relu_up,),
                        epilogue=lambda acc, r: (acc * (2.0 * r.astype(F32)),), out_dtypes=(BF16,))[0]
    gw_down, gw_down_own = _mm_tn("dw_down", f, dfo, me_arr, m=f.shape[1], n=d, tma=512, tn=1024, sharded="rows")
    dn2 = _mm_nt_acc("d_n2", dup, wup_g, tm=tm // 2)
    gw_up, gw_up_own = _mm_tn("dw_up", n2, dup, me_arr, m=d, n=dup.shape[1], tma=512, tn=1024, sharded="cols")
    dh1, dmix, dg_pre_mlp, dg_post_mix = _mid_norm_bwd(dn2, h1, dh2, mix, g_pre_mlp, g_post_mix)
    dm = _mm_nt_blocks("d_m", dmix, wo_full, tm=tm, tkb=1024)[0]
    gw_o, gw_o_own = _mm_tn("dw_o", m_mix, dmix, me_arr, m=d, n=d, tma=d // N_DEV, tn=1024, sharded="rows")
    dyab, dproj, db_gates = _gate_merge_bwd(dm, proj, ya, yb, b_gates, d)
    ycb = d // N_DEV
    da3 = _mm_nt_acc("d_a3", dyab, wpw_g, tm=tm, nb=N_DEV, col_off=0)
    gw_pw, gw_pw_own = _mm_tn("dw_pw", a3, dyab, me_arr, m=dc, n=d, tma=512, tn=ycb, sharded="cols")
    dsb = _mm_nt_acc("d_s", dyab, wso_g, tm=tm, nb=N_DEV, col_off=1)
    gw_so, gw_so_own = _mm_tn("dw_so", s, dyab, me_arr, m=dc, n=d, tma=512, tn=ycb, sharded="cols", b_off=d // ycb)
    da1, dln_g, dln_b = _layer_norm_silu_bwd(da3, a1, conf_ln_g, conf_ln_b)
    dproj, dcw, dcb, dsw = _conv_backward(dproj, proj, da1, dsb, cw_full, sw_full, dc)
    dn = _mm_nt_acc("d_n", dproj, win_g, tm=tm // 2)
    in_cb = w_in.shape[2]
    gw_in, gw_in_own = _mm_tn("dw_in", n, dproj, me_arr, m=d, n=proj.shape[1], tma=512, tn=in_cb, sharded="cols")
    dh0, dg_pre_mix = _pre_norm_bwd(dn, h0, dh1, g_pre_mix)
    grad_x = dh0[N_META:t_real][None]

    vec_parts = [dg_pre_mix, db_gates, dcb, dln_g, dln_b, dg_post_mix, dg_pre_mlp, dg_post_mlp]
    packed = _pack_small(vec_parts, dh0[:N_META], dcw, dsw)
    small_g = _all_gather("gather_small_grads", [packed])[0]
    vec_names = ["g_pre_mix", "b_gates", "conf_dw_b", "conf_ln_g", "conf_ln_b", "g_post_mix", "g_pre_mlp", "g_post_mlp"]
    env = locals()
    triple = lambda nm, sq: tuple(env[p + nm][0] if sq else env[p + nm] for p in ("", "m_", "v_"))
    small = _small_update(small_g, me_arr, [triple(nm, False) for nm in vec_names],
                          triple("meta", False), triple("conf_dw_w", True), triple("short_dw_w", True))
    results = {}
    for nm, res in zip(vec_names + ["meta"], small[:len(vec_names) + 1]):
        results[nm] = res
    results["conf_dw_w"] = tuple(r[None] for r in small[-2])
    results["short_dw_w"] = tuple(r[None] for r in small[-1])

    landed = _exchange_blocks("exchange_grads", [gw_in, gw_pw, gw_so, gw_o, gw_up, gw_down])
    owns = [gw_in_own, gw_pw_own, gw_so_own, gw_o_own, gw_up_own, gw_down_own]
    for nm, own, land in zip(big_names, owns, landed):
        res = _adamw_shard("adamw_" + nm, env[nm][0], env["m_" + nm][0], env["v_" + nm][0], own, land, me_arr)
        results[nm] = tuple(r[None] for r in res)

    order = ["meta", "g_pre_mix", "w_in", "b_gates", "conf_dw_w", "conf_dw_b", "conf_ln_g", "conf_ln_b", "conf_w_pw",
             "short_dw_w", "short_w_out", "w_o", "g_post_mix", "g_pre_mlp", "w_up", "w_down", "g_post_mlp"]
    return (loss, grad_x, *[results[nm][0] for nm in order], *[results[nm][1] for nm in order],
            *[results[nm][2] for nm in order], *[results[nm][3] for nm in order])
```

```python
import jax
import jax.numpy as jnp
from jax import lax
from jax.experimental import pallas as pl
from jax.experimental.pallas import tpu as pltpu

N_DEV = 8
N_META = 16
CONF_K = 31
SHORT_K = 3
RMS_EPS = 1e-6
LN_EPS = 1e-5
ADAM_LR = 0.001
ADAM_B1 = 0.9
ADAM_B2 = 0.999
ADAM_EPS = 1e-08
ADAM_WD = 0.01
ADAM_STEP = 10

LANE = 128
SUB = 8
ROW_TILE = 128
CONV_PAD = 32
CONV_CHUNK = 128
VMEM_LIMIT = 56 * 1024 * 1024

F32 = jnp.float32
BF16 = jnp.bfloat16
MESH = pl.DeviceIdType.MESH
ANY = pl.BlockSpec(memory_space=pl.ANY)
HBM_SPEC = pl.BlockSpec(memory_space=pltpu.HBM)
SEM_SPEC = pl.BlockSpec(memory_space=pltpu.SEMAPHORE)
EFFECT = pltpu.SideEffectType.DATAFLOW_SIDE_EFFECTING


def _params(n_axes):
    return pltpu.CompilerParams(dimension_semantics=("arbitrary",) * n_axes, vmem_limit_bytes=VMEM_LIMIT)


def _sigmoid(z):
    return 1.0 / (1.0 + jnp.exp(-z))


def _colsum8(v):
    r, c = v.shape
    return jnp.sum(v.reshape(r // SUB, SUB, c), axis=0)


def _position():
    x, y, c = lax.axis_index("x"), lax.axis_index("y"), lax.axis_index("c")
    return x, y, c


def _flat(p):
    return 4 * p[0] + 2 * p[1] + p[2]


def _all_gather(name, shards):
    n = len(shards)

    def body(*refs):
        ins, outs = refs[:n], refs[n:2 * n]
        send_sems, recv_sems, local_sems = refs[2 * n:]
        x, y, c = _position()
        me, sibling = (x, y, c), (x, y, 1 - c)
        chips = [(1 - x, y), (x, 1 - y), (1 - x, 1 - y)]

        def copy(q, k, block, to, src=None):
            dst = outs[q].at[_flat(block)]
            return pltpu.make_async_remote_copy(
                src_ref=dst if src is None else src, dst_ref=dst,
                send_sem=send_sems.at[q, k], recv_sem=recv_sems.at[q, k],
                device_id=to, device_id_type=MESH)

        mine = [pltpu.make_async_copy(ins[q], outs[q].at[_flat(me)], local_sems.at[q]) for q in range(n)]
        for cp in mine:
            cp.start()
        first = []
        for q in range(n):
            first.append(copy(q, 0, me, sibling, src=ins[q]))
            for j, chip in enumerate(chips):
                first.append(copy(q, 1 + j, me, (*chip, c), src=ins[q]))
        for cp in first:
            cp.start()
        passed = []
        for q in range(n):
            for j, chip in enumerate(chips):
                copy(q, 1 + j, (*chip, c), me).wait_recv()
                fwd = copy(q, 4 + j, (*chip, c), sibling)
                fwd.start()
                passed.append(fwd)
        for q in range(n):
            copy(q, 0, sibling, me).wait_recv()
            for j, chip in enumerate(chips):
                copy(q, 4 + j, (*chip, 1 - c), me).wait_recv()
        for cp in first + passed:
            cp.wait_send()
        for cp in mine:
            cp.wait()

    return pl.pallas_call(
        body, name=name,
        in_specs=[ANY] * n, out_specs=[ANY] * n,
        out_shape=[jax.ShapeDtypeStruct((N_DEV,) + s.shape, s.dtype) for s in shards],
        scratch_shapes=[pltpu.SemaphoreType.DMA((n, 7)), pltpu.SemaphoreType.DMA((n, 7)),
                        pltpu.SemaphoreType.DMA((n,))],
    )(*shards)


N_COPIES = {"gather_ici": 4, "gather_d2d": 3, "reduce_d2d": 4, "reduce_ici": 3}


def _copy_plan(kind):
    x, y, c = _position()
    me, sibling = (x, y, c), (x, y, 1 - c)
    chips = [(1 - x, y), (x, 1 - y), (1 - x, 1 - y)]
    if kind == "gather_ici":
        return [(_flat(me), _flat(me), sibling)] + [(_flat(me), _flat(me), (*ch, c)) for ch in chips]
    if kind == "gather_d2d":
        return [(_flat((*ch, c)), _flat((*ch, c)), sibling) for ch in chips]
    if kind == "reduce_d2d":
        return [(2 * chip + (1 - c), chip, sibling) for chip in range(4)]
    return [(2 * ch[0] + ch[1], 2 * x + y, (*ch, c)) for ch in chips]


def _planned_copies(kind, srcs, dsts, send_sems, recv_sems):
    plan = _copy_plan(kind)
    return [pltpu.make_async_remote_copy(
        src_ref=src.at[s_slot], dst_ref=dst.at[d_slot],
        send_sem=send_sems.at[q * len(plan) + k], recv_sem=recv_sems.at[q * len(plan) + k],
        device_id=to, device_id_type=MESH)
        for q, (src, dst) in enumerate(zip(srcs, dsts)) for k, (s_slot, d_slot, to) in enumerate(plan)]


def _remote_start(name, kind, srcs, lands=None, deps=()):
    n = len(srcs)
    bufs = list(srcs) + ([] if lands is None else list(lands))
    nb, nd = len(bufs), len(deps)
    nsem = n * N_COPIES[kind]

    def body(*refs):
        ins = refs[:nb]
        send_sems, recv_sems = refs[nb + nd], refs[nb + nd + 1]
        token = refs[-1]
        for cp in _planned_copies(kind, ins[:n], ins[:n] if lands is None else ins[n:], send_sems, recv_sems):
            cp.start()
        token[...] = jnp.zeros_like(token)

    outs = pl.pallas_call(
        body, name=name,
        out_shape=(pltpu.SemaphoreType.DMA((nsem,)), pltpu.SemaphoreType.DMA((nsem,)),
                   *[pltpu.HBM(b.shape, b.dtype) for b in bufs], jax.ShapeDtypeStruct((SUB, LANE), F32)),
        in_specs=[HBM_SPEC] * nb + [ANY] * nd,
        out_specs=(SEM_SPEC, SEM_SPEC, *[HBM_SPEC] * nb, pl.BlockSpec(memory_space=pltpu.VMEM)),
        input_output_aliases={i: 2 + i for i in range(nb)},
        compiler_params=pltpu.CompilerParams(has_side_effects=EFFECT),
    )(*[pltpu.with_memory_space_constraint(b, pltpu.HBM) for b in bufs], *deps)
    return outs[0], outs[1], list(outs[2:2 + nb]), outs[-1]


def _remote_wait(name, kind, send_sems, recv_sems, bufs, n, after):
    nb, na = len(bufs), len(after)
    same = nb == n

    def body(*refs):
        ins = refs[:nb]
        sends, recvs = refs[nb], refs[nb + 1]
        for cp in _planned_copies(kind, ins[:n], ins[:n] if same else ins[n:], sends, recvs):
            cp.wait_send()
            cp.wait_recv()

    outs = pl.pallas_call(
        body, name=name,
        out_shape=[pltpu.HBM(b.shape, b.dtype) for b in bufs],
        in_specs=[HBM_SPEC] * nb + [SEM_SPEC, SEM_SPEC] + [ANY] * na,
        out_specs=[HBM_SPEC] * nb,
        input_output_aliases={i: i for i in range(nb)},
        compiler_params=pltpu.CompilerParams(has_side_effects=EFFECT),
    )(*bufs, send_sems, recv_sems, *after)
    return list(outs)


def _mm_cols(name, a, w, *, tm, nb=1, epilogue=None, out_dtypes=(F32,)):
    t, k = a.shape
    nblk, _, cb = w.shape

    def body(a_ref, w_ref, *o_refs):
        av = a_ref[...]
        for b in range(nb):
            acc = jnp.dot(av, w_ref[b], preferred_element_type=F32)
            outs = (acc,) if epilogue is None else epilogue(acc)
            for o_ref, o in zip(o_refs, outs):
                o_ref[:, b * cb:(b + 1) * cb] = o.astype(o_ref.dtype)

    return pl.pallas_call(
        body, name=name, grid=(nblk // nb, t // tm),
        in_specs=[pl.BlockSpec((tm, k), lambda j, i: (i, 0)),
                  pl.BlockSpec((nb, k, cb), lambda j, i: (j, 0, 0))],
        out_specs=[pl.BlockSpec((tm, nb * cb), lambda j, i: (i, j)) for _ in out_dtypes],
        out_shape=[jax.ShapeDtypeStruct((t, nblk * cb), dt) for dt in out_dtypes],
        compiler_params=_params(2),
    )(a, w)


def _mm_rows(name, a, w2d, *, tm, tk):
    t = a.shape[0]
    kf, n = w2d.shape

    def body(a_ref, w_ref, o_ref):
        acc = jnp.dot(a_ref[...], w_ref[...], preferred_element_type=F32)

        @pl.when(pl.program_id(1) == 0)
        def _():
            o_ref[...] = acc

        @pl.when(pl.program_id(1) > 0)
        def _():
            o_ref[...] += acc

    return pl.pallas_call(
        body, name=name, grid=(t // tm, kf // tk),
        in_specs=[pl.BlockSpec((tm, tk), lambda i, kk: (i, kk)),
                  pl.BlockSpec((tk, n), lambda i, kk: (kk, 0))],
        out_specs=pl.BlockSpec((tm, n), lambda i, kk: (i, 0)),
        out_shape=jax.ShapeDtypeStruct((t, n), F32),
        compiler_params=_params(2),
    )(a, w2d)


def _mm_nt_acc(name, dy, w, *, tm, nb=1, col_off=0, deps=()):
    t = dy.shape[0]
    nblk, k, cb = w.shape

    def body(dy_ref, w_ref, *rest):
        o_ref = rest[-1]
        acc = None
        for b in range(nb):
            d = lax.dot_general(dy_ref[:, b * cb:(b + 1) * cb], w_ref[b], (((1,), (1,)), ((), ())),
                                preferred_element_type=F32)
            acc = d if acc is None else acc + d

        @pl.when(pl.program_id(1) == 0)
        def _():
            o_ref[...] = acc

        @pl.when(pl.program_id(1) > 0)
        def _():
            o_ref[...] += acc

    return pl.pallas_call(
        body, name=name, grid=(t // tm, nblk // nb),
        in_specs=[pl.BlockSpec((tm, nb * cb), lambda i, j: (i, col_off + j)),
                  pl.BlockSpec((nb, k, cb), lambda i, j: (j, 0, 0))] + [ANY] * len(deps),
        out_specs=pl.BlockSpec((tm, k), lambda i, j: (i, 0)),
        out_shape=jax.ShapeDtypeStruct((t, k), F32),
        compiler_params=_params(2),
    )(dy, w, *deps)


def _mm_nt_blocks(name, dy, w2d, *, tm, tkb, extra=(), epilogue=None, out_dtypes=(F32,)):
    t, n = dy.shape
    kf = w2d.shape[0]
    ne = len(extra)

    def body(dy_ref, w_ref, *rest):
        acc = lax.dot_general(dy_ref[...], w_ref[...], (((1,), (1,)), ((), ())), preferred_element_type=F32)
        outs = (acc,) if epilogue is None else epilogue(acc, *[e[...] for e in rest[:ne]])
        for o_ref, o in zip(rest[ne:], outs):
            o_ref[...] = o.astype(o_ref.dtype)

    return pl.pallas_call(
        body, name=name, grid=(kf // tkb, t // tm),
        in_specs=[pl.BlockSpec((tm, n), lambda kb, i: (i, 0)),
                  pl.BlockSpec((tkb, n), lambda kb, i: (kb, 0))]
                 + [pl.BlockSpec((tm, tkb), lambda kb, i: (i, kb)) for _ in extra],
        out_specs=[pl.BlockSpec((tm, tkb), lambda kb, i: (i, kb)) for _ in out_dtypes],
        out_shape=[jax.ShapeDtypeStruct((t, kf), dt) for dt in out_dtypes],
        compiler_params=_params(2),
    )(dy, w2d, *extra)


def _mm_tn(name, a, b, me_arr, *, m, n, tma, tn, sharded, a_off=0, b_off=0, deps=()):
    t = a.shape[0]
    if sharded == "cols":
        cb = n // N_DEV
        q = cb // tn
        full_shape, own_shape = (N_DEV, m, cb), (m, cb)
        full_spec = pl.BlockSpec((1, tma, tn), lambda i, j, me: (j // q, i, j % q))
    else:
        kb = m // N_DEV
        p = kb // tma
        full_shape, own_shape = (m, n), (kb, n)
        full_spec = pl.BlockSpec((tma, tn), lambda i, j, me: (i, j))

    def body(me_ref, a_ref, b_ref, *rest):
        full_ref, own_ref, stage, sem = rest[len(deps):]
        i, j = pl.program_id(0), pl.program_id(1)
        acc = lax.dot_general(a_ref[...], b_ref[...], (((0,), (0,)), ((), ())), preferred_element_type=F32)
        if sharded == "cols":
            full_ref[0] = acc.astype(BF16)
            owner, r0, c0 = j // q, i * tma, (j % q) * tn
        else:
            full_ref[...] = acc.astype(BF16)
            owner, r0, c0 = i // p, (i % p) * tma, j * tn

        @pl.when(owner == me_ref[0])
        def _():
            stage[...] = acc
            cp = pltpu.make_async_copy(
                stage, own_ref.at[pl.ds(pl.multiple_of(r0, tma), tma), pl.ds(pl.multiple_of(c0, tn), tn)], sem)
            cp.start()
            cp.wait()

    full, own = pl.pallas_call(
        body, name=name,
        grid_spec=pltpu.PrefetchScalarGridSpec(
            num_scalar_prefetch=1, grid=(m // tma, n // tn),
            in_specs=[pl.BlockSpec((t, tma), lambda i, j, me: (0, a_off + i)),
                      pl.BlockSpec((t, tn), lambda i, j, me: (0, b_off + j))] + [ANY] * len(deps),
            out_specs=[full_spec, ANY],
            scratch_shapes=[pltpu.VMEM((tma, tn), F32), pltpu.SemaphoreType.DMA(())]),
        out_shape=[jax.ShapeDtypeStruct(full_shape, BF16), jax.ShapeDtypeStruct(own_shape, F32)],
        compiler_params=_params(2),
    )(me_arr, a, b, *deps)
    if sharded == "rows":
        full = full.reshape(N_DEV, m // N_DEV, n)
    return full, own


def _row_call(name, body, t, row_ins, full_ins, row_outs, acc_outs, scratch=(), deps=()):
    tm = ROW_TILE
    nin = len(row_ins) + len(full_ins)

    def without_deps(*refs):
        body(*refs[:nin], *refs[nin + len(deps):])

    return pl.pallas_call(
        without_deps, name=name, grid=(t // tm,),
        in_specs=[pl.BlockSpec((tm, a.shape[1]), lambda i: (i, 0)) for a in row_ins]
                 + [pl.BlockSpec(a.shape, lambda i: (0, 0)) for a in full_ins] + [ANY] * len(deps),
        out_specs=[pl.BlockSpec((tm, c), lambda i: (i, 0)) for c, _ in row_outs]
                  + [pl.BlockSpec((r, c), lambda i: (0, 0)) for r, c in acc_outs],
        out_shape=[jax.ShapeDtypeStruct((t, c), dt) for c, dt in row_outs]
                  + [jax.ShapeDtypeStruct((r, c), F32) for r, c in acc_outs],
        scratch_shapes=list(scratch),
        compiler_params=_params(1),
    )(*row_ins, *full_ins, *deps)


def _accumulate(ref, v):
    @pl.when(pl.program_id(0) == 0)
    def _():
        ref[...] = v

    @pl.when(pl.program_id(0) > 0)
    def _():
        ref[...] += v


def _rms(v):
    return lax.rsqrt(jnp.mean(v * v, axis=-1, keepdims=True) + RMS_EPS)


def _rms_bwd(dout, u, r, g):
    du = dout * g
    dx = r * (du - u * jnp.mean(du * u, axis=-1, keepdims=True))
    return dx, _colsum8(dout * u)


def _pre_norm(h0, g):
    t, d = h0.shape

    def body(h_ref, g_ref, n_ref):
        h = h_ref[...]
        n_ref[...] = (h * _rms(h) * g_ref[...]).astype(BF16)

    return _row_call("pre_norm", body, t, [h0], [g], [(d, BF16)], [])[0]


def _post_mix(mix, h0, g_post, g_pre):
    t, d = h0.shape

    def body(mix_ref, h0_ref, gp_ref, gq_ref, h1_ref, n2_ref):
        mix_v = mix_ref[...]
        h1 = h0_ref[...] + mix_v * _rms(mix_v) * gp_ref[...]
        h1_ref[...] = h1
        n2_ref[...] = (h1 * _rms(h1) * gq_ref[...]).astype(BF16)

    return _row_call("post_mix", body, t, [mix, h0], [g_post, g_pre], [(d, F32), (d, BF16)], [])


def _loss_head(fo, h1, tgt, g_post_mlp, t_real):
    t, d = h1.shape

    def body(fo_ref, h1_ref, tgt_ref, g_ref, dfo_ref, dh2_ref, dg_ref, loss_ref, lacc):
        i = pl.program_id(0)
        fo_v = fo_ref[...]
        g = g_ref[...]
        r = _rms(fo_v)
        u = fo_v * r
        h2 = h1_ref[...] + u * g
        row = i * ROW_TILE + lax.broadcasted_iota(jnp.int32, (ROW_TILE, 1), 0)
        valid = jnp.logical_and(row >= N_META, row < t_real)
        diff = jnp.where(valid, h2 - tgt_ref[...], 0.0)
        dh2 = diff * (1.0 / d)
        dh2_ref[...] = dh2
        dfo, dg = _rms_bwd(dh2, u, r, g)
        dfo_ref[...] = dfo.astype(BF16)
        _accumulate(dg_ref, dg)
        _accumulate(lacc, _colsum8(diff * diff))

        @pl.when(i == pl.num_programs(0) - 1)
        def _():
            loss_ref[...] = jnp.full((SUB, LANE), (0.5 / d) * jnp.sum(lacc[...]), F32)

    return _row_call("loss_head", body, t, [fo, h1, tgt], [g_post_mlp],
                     [(d, BF16), (d, F32)], [(SUB, d), (SUB, LANE)], scratch=[pltpu.VMEM((SUB, d), F32)])


def _mid_norm_bwd(dn2, h1, dh2, mix, g_pre_mlp, g_post_mix, deps=()):
    t, d = h1.shape

    def body(dn2_ref, h1_ref, dh2_ref, mix_ref, gq_ref, gp_ref, dh1_ref, dmix_ref, dgq_ref, dgp_ref):
        h1 = h1_ref[...]
        r3 = _rms(h1)
        dx, dgq = _rms_bwd(dn2_ref[...], h1 * r3, r3, gq_ref[...])
        dh1 = dh2_ref[...] + dx
        dh1_ref[...] = dh1
        mix_v = mix_ref[...]
        r2 = _rms(mix_v)
        dmix, dgp = _rms_bwd(dh1, mix_v * r2, r2, gp_ref[...])
        dmix_ref[...] = dmix.astype(BF16)
        _accumulate(dgq_ref, dgq)
        _accumulate(dgp_ref, dgp)

    return _row_call("mid_norm_bwd", body, t, [dn2, h1, dh2, mix], [g_pre_mlp, g_post_mix],
                     [(d, F32), (d, BF16)], [(SUB, d), (SUB, d)], deps=deps)


def _pre_norm_bwd(dn, h0, dh1, g_pre_mix, deps=()):
    t, d = h0.shape

    def body(dn_ref, h0_ref, dh1_ref, g_ref, dh0_ref, dg_ref):
        h0 = h0_ref[...]
        r = _rms(h0)
        dx, dg = _rms_bwd(dn_ref[...], h0 * r, r, g_ref[...])
        dh0_ref[...] = dh1_ref[...] + dx
        _accumulate(dg_ref, dg)

    return _row_call("pre_norm_bwd", body, t, [dn, h0, dh1], [g_pre_mix], [(d, F32)], [(SUB, d)], deps=deps)


def _layer_norm_silu(a1, ln_g, ln_b):
    t, c = a1.shape

    def body(a1_ref, g_ref, b_ref, a3_ref):
        a = a1_ref[...]
        mu = jnp.mean(a, axis=-1, keepdims=True)
        xc = a - mu
        rstd = lax.rsqrt(jnp.mean(xc * xc, axis=-1, keepdims=True) + LN_EPS)
        z = xc * rstd * g_ref[...] + b_ref[...]
        a3_ref[...] = (z * _sigmoid(z)).astype(BF16)

    return _row_call("layer_norm_silu", body, t, [a1], [ln_g, ln_b], [(c, BF16)], [])[0]


def _layer_norm_silu_bwd(da3, a1, ln_g, ln_b, deps=()):
    t, c = a1.shape

    def body(da3_ref, a1_ref, g_ref, b_ref, da1_ref, dg_ref, db_ref):
        a = a1_ref[...]
        g = g_ref[...]
        mu = jnp.mean(a, axis=-1, keepdims=True)
        xc = a - mu
        rstd = lax.rsqrt(jnp.mean(xc * xc, axis=-1, keepdims=True) + LN_EPS)
        xhat = xc * rstd
        z = xhat * g + b_ref[...]
        sg = _sigmoid(z)
        dz = da3_ref[...] * (sg * (1.0 + z * (1.0 - sg)))
        dxhat = dz * g
        da1_ref[...] = rstd * (dxhat - jnp.mean(dxhat, axis=-1, keepdims=True)
                               - xhat * jnp.mean(dxhat * xhat, axis=-1, keepdims=True))
        _accumulate(dg_ref, _colsum8(dz * xhat))
        _accumulate(db_ref, _colsum8(dz))

    return _row_call("layer_norm_silu_bwd", body, t, [da3, a1], [ln_g, ln_b], [(c, F32)], [(SUB, c), (SUB, c)], deps=deps)


def _gate_merge(proj, ya, yb, b_gates, d):
    t = proj.shape[0]
    w = 1024
    nh = d // w
    ga0 = (proj.shape[1] - 2 * d) // w

    def body(pa_ref, pb_ref, ya_ref, yb_ref, ba_ref, bb_ref, m_ref):
        ga = _sigmoid(pa_ref[...] + ba_ref[...])
        gb = _sigmoid(pb_ref[...] + bb_ref[...])
        m_ref[...] = (ga * ya_ref[...] + gb * yb_ref[...]).astype(BF16)

    tm = ROW_TILE
    return pl.pallas_call(
        body, name="gate_merge", grid=(nh, t // tm),
        in_specs=[pl.BlockSpec((tm, w), lambda h, i: (i, ga0 + h)),
                  pl.BlockSpec((tm, w), lambda h, i: (i, ga0 + nh + h)),
                  pl.BlockSpec((tm, w), lambda h, i: (i, h)),
                  pl.BlockSpec((tm, w), lambda h, i: (i, h)),
                  pl.BlockSpec((1, w), lambda h, i: (0, h)),
                  pl.BlockSpec((1, w), lambda h, i: (0, nh + h))],
        out_specs=pl.BlockSpec((tm, w), lambda h, i: (i, h)),
        out_shape=jax.ShapeDtypeStruct((t, d), BF16),
        compiler_params=_params(2),
    )(proj, proj, ya, yb, b_gates, b_gates)


def _gate_merge_bwd(dm, proj, ya, yb, b_gates, d):
    t, cols = proj.shape
    w = 1024
    nh = d // w
    ga0 = (cols - 2 * d) // w

    def body(dm_ref, p_ref, ya_ref, yb_ref, b_ref, dy_ref, dp_ref, db_ref):
        q = pl.program_id(0)
        g = _sigmoid(p_ref[...] + b_ref[...])
        dm_v = dm_ref[...]
        y = jnp.where(q < nh, ya_ref[...], yb_ref[...])
        dy_ref[...] = (dm_v * g).astype(BF16)
        dp = dm_v * y * g * (1.0 - g)
        dp_ref[...] = dp.astype(BF16)

        @pl.when(pl.program_id(1) == 0)
        def _():
            db_ref[...] = _colsum8(dp)

        @pl.when(pl.program_id(1) > 0)
        def _():
            db_ref[...] += _colsum8(dp)

    tm = ROW_TILE
    return pl.pallas_call(
        body, name="gate_merge_bwd", grid=(2 * nh, t // tm),
        in_specs=[pl.BlockSpec((tm, w), lambda q, i: (i, q % nh)),
                  pl.BlockSpec((tm, w), lambda q, i: (i, ga0 + q)),
                  pl.BlockSpec((tm, w), lambda q, i: (i, q % nh)),
                  pl.BlockSpec((tm, w), lambda q, i: (i, q % nh)),
                  pl.BlockSpec((1, w), lambda q, i: (0, q))],
        out_specs=[pl.BlockSpec((tm, w), lambda q, i: (i, q)),
                   pl.BlockSpec((tm, w), lambda q, i: (i, ga0 + q)),
                   pl.BlockSpec((SUB, w), lambda q, i: (0, q))],
        out_shape=[jax.ShapeDtypeStruct((t, 2 * d), BF16), jax.ShapeDtypeStruct((t, cols), BF16),
                   jax.ShapeDtypeStruct((SUB, 2 * d), F32)],
        compiler_params=_params(2),
    )(dm, proj, ya, yb, b_gates)


def _causal_conv(xp_ref, w_ref, ntap, r0):
    n = CONV_CHUNK + CONV_PAD
    win = xp_ref[pl.ds(r0, n), :]
    acc = None
    for k in range(ntap):
        back = ntap - 1 - k
        shifted = pltpu.roll(win, n - (CONV_PAD - back), 0)
        term = w_ref[k:k + 1, :] * shifted[:CONV_CHUNK]
        acc = term if acc is None else acc + term
    return acc


def _anticausal_conv(xp_ref, w_ref, ntap, r0):
    n = CONV_CHUNK + CONV_PAD
    win = xp_ref[pl.ds(pl.multiple_of(CONV_PAD + r0, CONV_PAD), n), :]
    acc = None
    for k in range(ntap):
        ahead = ntap - 1 - k
        shifted = win if ahead == 0 else pltpu.roll(win, n - ahead, 0)
        term = w_ref[k:k + 1, :] * shifted[:CONV_CHUNK]
        acc = term if acc is None else acc + term
    return acc


def _conv_weight_grad(dw_ref, d_chunk, xp_ref, ntap, r0):
    n = CONV_CHUNK + CONV_PAD
    win = xp_ref[pl.ds(r0, n), :]
    for k in range(ntap):
        back = ntap - 1 - k
        shifted = pltpu.roll(win, n - (CONV_PAD - back), 0)
        dw_ref[k * SUB:(k + 1) * SUB, :] += _colsum8(d_chunk * shifted[:CONV_CHUNK])


def _zero_pads(ref, t):
    ref[0:CONV_PAD, :] = jnp.zeros((CONV_PAD, LANE), F32)
    ref[CONV_PAD + t:CONV_PAD + t + CONV_PAD, :] = jnp.zeros((CONV_PAD, LANE), F32)


def _for_chunks(t, fn):
    def step(idx, carry):
        fn(pl.multiple_of(idx * CONV_CHUNK, CONV_CHUNK))
        return carry

    lax.fori_loop(0, t // CONV_CHUNK, step, 0)


def _conv_forward(proj, conf_w, conf_b, short_w, dc):
    t = proj.shape[0]
    nc = dc // LANE

    def body(av_ref, ag_ref, bg_ref, cg_ref, v_ref, cw_ref, cb_ref, sw_ref, a1_ref, s_ref, xa, xb):
        _zero_pads(xa, t)
        _zero_pads(xb, t)
        xa[CONV_PAD:CONV_PAD + t, :] = av_ref[...] * _sigmoid(ag_ref[...])
        xb[CONV_PAD:CONV_PAD + t, :] = cg_ref[...] * v_ref[...]

        def chunk(r0):
            rs = pl.ds(r0, CONV_CHUNK)
            a1_ref[rs, :] = _causal_conv(xa, cw_ref, CONF_K, r0) + cb_ref[...]
            s_ref[rs, :] = (bg_ref[rs, :] * _causal_conv(xb, sw_ref, SHORT_K, r0)).astype(BF16)

        _for_chunks(t, chunk)

    col = lambda g: pl.BlockSpec((t, LANE), lambda c, g=g: (0, g * nc + c))
    return pl.pallas_call(
        body, name="conv_forward", grid=(nc,),
        in_specs=[col(0), col(1), col(2), col(3), col(4),
                  pl.BlockSpec((CONF_K, LANE), lambda c: (0, c)),
                  pl.BlockSpec((1, LANE), lambda c: (0, c)),
                  pl.BlockSpec((SHORT_K, LANE), lambda c: (0, c))],
        out_specs=[pl.BlockSpec((t, LANE), lambda c: (0, c)), pl.BlockSpec((t, LANE), lambda c: (0, c))],
        out_shape=[jax.ShapeDtypeStruct((t, dc), F32), jax.ShapeDtypeStruct((t, dc), BF16)],
        scratch_shapes=[pltpu.VMEM((t + 2 * CONV_PAD, LANE), F32), pltpu.VMEM((t + 2 * CONV_PAD, LANE), F32)],
        compiler_params=_params(1),
    )(proj, proj, proj, proj, proj, conf_w, conf_b, short_w)


def _conv_backward(dproj, proj, da1, ds, conf_w, short_w, dc):
    t = proj.shape[0]
    nc = dc // LANE

    def body(dp_in, av_ref, ag_ref, bg_ref, cg_ref, v_ref, da1_ref, ds_ref, cw_ref, sw_ref,
             dp_ref, dcw_ref, dcb_ref, dsw_ref, xa, xb, da, db, stage, sems):
        del dp_in
        c = pl.program_id(0)
        for ref in (xa, xb, da, db):
            _zero_pads(ref, t)
        xa[CONV_PAD:CONV_PAD + t, :] = av_ref[...] * _sigmoid(ag_ref[...])
        xb[CONV_PAD:CONV_PAD + t, :] = cg_ref[...] * v_ref[...]
        da[CONV_PAD:CONV_PAD + t, :] = da1_ref[...]
        dcw_ref[...] = jnp.zeros(dcw_ref.shape, F32)
        dsw_ref[...] = jnp.zeros(dsw_ref.shape, F32)
        dcb_ref[...] = jnp.zeros(dcb_ref.shape, F32)

        def through_gate(r0):
            rs = pl.ds(r0, CONV_CHUNK)
            ds_c = ds_ref[rs, :]
            stage[2, rs, :] = (ds_c * _causal_conv(xb, sw_ref, SHORT_K, r0)).astype(BF16)
            db[pl.ds(pl.multiple_of(CONV_PAD + r0, CONV_PAD), CONV_CHUNK), :] = ds_c * bg_ref[rs, :]

        _for_chunks(t, through_gate)

        def through_convs(r0):
            rs = pl.ds(r0, CONV_CHUNK)
            da0 = _anticausal_conv(da, cw_ref, CONF_K, r0)
            sg = _sigmoid(ag_ref[rs, :])
            stage[0, rs, :] = (da0 * sg).astype(BF16)
            stage[1, rs, :] = (da0 * av_ref[rs, :] * sg * (1.0 - sg)).astype(BF16)
            dcv = _anticausal_conv(db, sw_ref, SHORT_K, r0)
            stage[3, rs, :] = (dcv * v_ref[rs, :]).astype(BF16)
            stage[4, rs, :] = (dcv * cg_ref[rs, :]).astype(BF16)
            da1_c = da1_ref[rs, :]
            _conv_weight_grad(dcw_ref, da1_c, xa, CONF_K, r0)
            _conv_weight_grad(dsw_ref, ds_ref[rs, :] * bg_ref[rs, :], xb, SHORT_K, r0)
            dcb_ref[...] += _colsum8(da1_c)

        _for_chunks(t, through_convs)
        copies = [pltpu.make_async_copy(
            stage.at[g], dp_ref.at[:, pl.ds(pl.multiple_of((g * nc + c) * LANE, LANE), LANE)], sems.at[g])
            for g in range(5)]
        for cp in copies:
            cp.start()
        for cp in copies:
            cp.wait()

    col = lambda g: pl.BlockSpec((t, LANE), lambda c, g=g: (0, g * nc + c))
    blk = pl.BlockSpec((t, LANE), lambda c: (0, c))
    return pl.pallas_call(
        body, name="conv_backward", grid=(nc,),
        in_specs=[ANY, col(0), col(1), col(2), col(3), col(4), blk, blk,
                  pl.BlockSpec((CONF_K, LANE), lambda c: (0, c)),
                  pl.BlockSpec((SHORT_K, LANE), lambda c: (0, c))],
        out_specs=[ANY,
                   pl.BlockSpec((CONF_K * SUB, LANE), lambda c: (0, c)),
                   pl.BlockSpec((SUB, LANE), lambda c: (0, c)),
                   pl.BlockSpec((SHORT_K * SUB, LANE), lambda c: (0, c))],
        out_shape=[jax.ShapeDtypeStruct(dproj.shape, dproj.dtype),
                   jax.ShapeDtypeStruct((CONF_K * SUB, dc), F32),
                   jax.ShapeDtypeStruct((SUB, dc), F32),
                   jax.ShapeDtypeStruct((SHORT_K * SUB, dc), F32)],
        scratch_shapes=[pltpu.VMEM((t + 2 * CONV_PAD, LANE), F32)] * 4
                       + [pltpu.VMEM((5, t, LANE), BF16), pltpu.SemaphoreType.DMA((5,))],
        input_output_aliases={0: 0},
        compiler_params=_params(1),
    )(dproj, proj, proj, proj, proj, proj, da1, ds, conf_w, short_w)


def _adamw_math(w, g, m, v):
    m = ADAM_B1 * m + (1.0 - ADAM_B1) * g
    v = ADAM_B2 * v + (1.0 - ADAM_B2) * (g * g)
    m_hat = m / (1.0 - ADAM_B1 ** ADAM_STEP)
    v_hat = v / (1.0 - ADAM_B2 ** ADAM_STEP)
    delta = -ADAM_LR * (m_hat / (jnp.sqrt(v_hat) + ADAM_EPS) + ADAM_WD * w)
    return delta, m, v


def _cast_into_slot(name, w, me_arr):
    r, c = w.shape
    tr = 256

    def body(me_ref, w_ref, o_ref):
        del me_ref
        o_ref[0] = w_ref[...].astype(BF16)

    return pl.pallas_call(
        body, name=name,
        grid_spec=pltpu.PrefetchScalarGridSpec(
            num_scalar_prefetch=1, grid=(r // tr,),
            in_specs=[pl.BlockSpec((tr, c), lambda i, me: (i, 0))],
            out_specs=pl.BlockSpec((1, tr, c), lambda i, me: (me[0], i, 0))),
        out_shape=jax.ShapeDtypeStruct((N_DEV, r, c), BF16),
        compiler_params=_params(1),
    )(me_arr, w)


def _chip_sum(name, full, from_sibling, own, me_arr):
    _, r, c = full.shape
    tr = 128

    def body(me_ref, full_ref, sib_ref, own_ref, sums_ref, mine_ref):
        theirs = sib_ref[0].astype(F32)
        sums_ref[0] = (full_ref[0].astype(F32) + theirs).astype(BF16)

        @pl.when(pl.program_id(1) == me_ref[0] // 2)
        def _():
            mine_ref[...] = own_ref[...] + theirs

    return pl.pallas_call(
        body, name=name,
        grid_spec=pltpu.PrefetchScalarGridSpec(
            num_scalar_prefetch=1, grid=(r // tr, 4),
            in_specs=[pl.BlockSpec((1, tr, c), lambda i, chip, me: (2 * chip + me[0] % 2, i, 0)),
                      pl.BlockSpec((1, tr, c), lambda i, chip, me: (chip, i, 0)),
                      pl.BlockSpec((tr, c), lambda i, chip, me: (i, 0))],
            out_specs=[pl.BlockSpec((1, tr, c), lambda i, chip, me: (chip, i, 0)),
                       pl.BlockSpec((tr, c), lambda i, chip, me: (i, 0))]),
        out_shape=[jax.ShapeDtypeStruct((4, r, c), BF16), jax.ShapeDtypeStruct((r, c), F32)],
        compiler_params=_params(2),
    )(me_arr, full, from_sibling, own)


def _adamw_shard(name, w, m, v, g_chip, landed, me_arr):
    r, c = w.shape
    tr = 128

    def body(me_ref, w_ref, m_ref, v_ref, g_ref, *rest):
        lands, (g_out, d_out, m_out, v_out) = rest[:3], rest[3:]
        g = g_ref[...]
        for l_ref in lands:
            g = g + l_ref[0].astype(F32)
        delta, m_new, v_new = _adamw_math(w_ref[...], g, m_ref[...], v_ref[...])
        g_out[...] = g
        d_out[...] = delta
        m_out[...] = m_new
        v_out[...] = v_new

    tile = pl.BlockSpec((tr, c), lambda i, me: (i, 0))
    land_specs = [pl.BlockSpec((1, tr, c), lambda i, me, k=k: ((me[0] // 2 + k) % 4, i, 0)) for k in range(1, 4)]
    return pl.pallas_call(
        body, name=name,
        grid_spec=pltpu.PrefetchScalarGridSpec(
            num_scalar_prefetch=1, grid=(r // tr,),
            in_specs=[tile] * 4 + land_specs, out_specs=[tile] * 4),
        out_shape=[jax.ShapeDtypeStruct((r, c), F32)] * 4,
        compiler_params=_params(1),
    )(me_arr, w, m, v, g_chip, *([landed] * 3))


SMALL_W = 1024
VEC_ROWS = 16
META_ROW0 = 16
CONF_ROW0 = 64
SHORT_ROW0 = 96
SMALL_ROWS = 104


def _pack_small(vec_parts, dmeta, dcw, dsw):
    widths = [p.shape[1] for p in vec_parts]
    nv = len(vec_parts)

    def body(*refs):
        parts, (dmeta_ref, dcw_ref, dsw_ref, out_ref) = refs[:nv], refs[nv:]
        out_ref[...] = jnp.zeros((SMALL_ROWS, SMALL_W), F32)
        row = 0
        for p_ref, wd in zip(parts, widths):
            s = jnp.sum(p_ref[...], axis=0, keepdims=True)
            for h in range(wd // SMALL_W):
                out_ref[row:row + 1, :] = s[:, h * SMALL_W:(h + 1) * SMALL_W]
                row += 1
        for h in range(dmeta_ref.shape[1] // SMALL_W):
            out_ref[META_ROW0 + h * N_META:META_ROW0 + (h + 1) * N_META, :] = dmeta_ref[:, h * SMALL_W:(h + 1) * SMALL_W]
        for k in range(CONF_K):
            out_ref[CONF_ROW0 + k:CONF_ROW0 + k + 1, :] = jnp.sum(dcw_ref[k * SUB:(k + 1) * SUB, :], axis=0, keepdims=True)
        for k in range(SHORT_K):
            out_ref[SHORT_ROW0 + k:SHORT_ROW0 + k + 1, :] = jnp.sum(dsw_ref[k * SUB:(k + 1) * SUB, :], axis=0, keepdims=True)

    return pl.pallas_call(
        body, name="pack_small",
        out_shape=jax.ShapeDtypeStruct((SMALL_ROWS, SMALL_W), F32),
        compiler_params=pltpu.CompilerParams(vmem_limit_bytes=VMEM_LIMIT),
    )(*vec_parts, dmeta, dcw, dsw)


def _small_update(gathered, me_arr, vec_params, meta_p, conf_p, short_p):
    widths = [p[0].shape[1] for p in vec_params]
    nv = len(vec_params)
    mcols = meta_p[0].shape[1]
    per_row = SMALL_W // mcols

    def body(me_ref, gv_ref, gm_ref, gc_ref, gs_ref, *rest):
        del me_ref
        ins, outs = rest[:3 * (nv + 3)], rest[3 * (nv + 3):]

        def total(ref, r0, rows):
            s = ref[0, r0:r0 + rows, :]
            for dev in range(1, N_DEV):
                s = s + ref[dev, r0:r0 + rows, :]
            return s

        grads = []
        row = 0
        for wd in widths:
            pieces = [total(gv_ref, row + h, 1) for h in range(wd // SMALL_W)]
            grads.append(pieces[0] if len(pieces) == 1 else jnp.concatenate(pieces, axis=1))
            row += len(pieces)
        grads.append(total(gm_ref, 0, N_META))
        grads.append(total(gc_ref, 0, CONF_K))
        grads.append(total(gs_ref, 0, SHORT_K))
        for idx, g in enumerate(grads):
            w_ref, m_ref, v_ref = ins[3 * idx:3 * idx + 3]
            delta, m_new, v_new = _adamw_math(w_ref[...], g, m_ref[...], v_ref[...])
            g_out, d_out, m_out, v_out = outs[4 * idx:4 * idx + 4]
            g_out[...] = g
            d_out[...] = delta
            m_out[...] = m_new
            v_out[...] = v_new

    params = list(vec_params) + [meta_p, conf_p, short_p]
    flat = [a for p in params for a in p]
    whole = lambda a: pl.BlockSpec(a.shape, lambda i, me: (0,) * a.ndim)
    outs = pl.pallas_call(
        body, name="small_update",
        grid_spec=pltpu.PrefetchScalarGridSpec(
            num_scalar_prefetch=1, grid=(1,),
            in_specs=[pl.BlockSpec((N_DEV, VEC_ROWS, SMALL_W), lambda i, me: (0, 0, 0)),
                      pl.BlockSpec((N_DEV, N_META, mcols),
                                   lambda i, me: (0, META_ROW0 // N_META + me[0] // per_row, me[0] % per_row)),
                      pl.BlockSpec((N_DEV, 32, LANE), lambda i, me: (0, CONF_ROW0 // 32, me[0])),
                      pl.BlockSpec((N_DEV, SUB, LANE), lambda i, me: (0, SHORT_ROW0 // SUB, me[0]))]
                     + [whole(a) for a in flat],
            out_specs=[whole(p[0]) for p in params for _ in range(4)]),
        out_shape=[jax.ShapeDtypeStruct(p[0].shape, F32) for p in params for _ in range(4)],
        compiler_params=_params(1),
    )(me_arr, gathered, gathered, gathered, gathered, *flat)
    return [tuple(outs[4 * i:4 * i + 4]) for i in range(len(params))]


def kernel(x, meta, g_pre_mix, w_in, b_gates, conf_dw_w, conf_dw_b, conf_ln_g, conf_ln_b, conf_w_pw, short_dw_w, short_w_out, w_o, g_post_mix, g_pre_mlp, w_up, w_down, g_post_mlp, loss_target, m_meta, m_g_pre_mix, m_w_in, m_b_gates, m_conf_dw_w, m_conf_dw_b, m_conf_ln_g, m_conf_ln_b, m_conf_w_pw, m_short_dw_w, m_short_w_out, m_w_o, m_g_post_mix, m_g_pre_mlp, m_w_up, m_w_down, m_g_post_mlp, v_meta, v_g_pre_mix, v_w_in, v_b_gates, v_conf_dw_w, v_conf_dw_b, v_conf_ln_g, v_conf_ln_b, v_conf_w_pw, v_short_dw_w, v_short_w_out, v_w_o, v_g_post_mix, v_g_pre_mlp, v_w_up, v_w_down, v_g_post_mlp):
    seq, d = x.shape[1], x.shape[2]
    dc = conf_w_pw.shape[1]
    t_real = N_META + seq
    t = -(-t_real // ROW_TILE) * ROW_TILE
    tm = t // 2
    assert tm % 16 == 0 and d % 1024 == 0 and dc % 1024 == 0
    x_idx, y_idx, c_idx = _position()
    me_arr = jnp.reshape(4 * x_idx + 2 * y_idx + c_idx, (1,)).astype(jnp.int32)

    big = [w_in[0], conf_w_pw[0], short_w_out[0], w_o[0], w_up[0], w_down[0]]
    big_names = ["w_in", "conf_w_pw", "short_w_out", "w_o", "w_up", "w_down"]
    slots = [_cast_into_slot("cast_" + nm, w, me_arr) for nm, w in zip(big_names, big)]
    groups = [[0], [1, 2, 3], [4], [5]]
    ici = [_remote_start("gather%d_ici_start" % g, "gather_ici", [slots[i] for i in idxs])
           for g, idxs in enumerate(groups)]

    def gathered(g, after):
        send, recv, bufs, _ = ici[g]
        bufs = _remote_wait("gather%d_ici_wait" % g, "gather_ici", send, recv, bufs, len(bufs), after)
        send, recv, bufs, tok = _remote_start("gather%d_d2d_start" % g, "gather_d2d", bufs)
        return _remote_wait("gather%d_d2d_wait" % g, "gather_d2d", send, recv, bufs, len(bufs), [tok])

    meta_g, cw_g, sw_g = _all_gather("gather_small_params", [meta, conf_dw_w[0], short_dw_w[0]])
    unshard = lambda g: jnp.transpose(g, (1, 0, 2)).reshape(g.shape[1], -1)
    meta_full, cw_full, sw_full = unshard(meta_g), unshard(cw_g), unshard(sw_g)

    zrows = jnp.zeros((t - t_real, d), F32)
    h0 = jnp.concatenate([meta_full, x[0], zrows], axis=0)
    tgt = jnp.concatenate([jnp.zeros((N_META, d), F32), loss_target[0], zrows], axis=0)
    n = _pre_norm(h0, g_pre_mix)
    win_g, = gathered(0, [n] + [st[3] for st in ici[1:]])
    proj = _mm_cols("proj", n, win_g, tm=tm)[0]
    a1, s = _conv_forward(proj, cw_full, conf_dw_b, sw_full, dc)
    a3 = _layer_norm_silu(a1, conf_ln_g, conf_ln_b)
    wpw_g, wso_g, wo_g = gathered(1, [a3])
    wo_full = wo_g.reshape(d, d)
    ya = _mm_cols("y_a", a3, wpw_g, tm=tm, nb=N_DEV)[0]
    yb = _mm_cols("y_b", s, wso_g, tm=tm, nb=N_DEV)[0]
    m_mix = _gate_merge(proj, ya, yb, b_gates, d)
    mix = _mm_rows("mix", m_mix, wo_full, tm=tm // 2, tk=d)
    wup_g, = gathered(2, [mix])
    h1, n2 = _post_mix(mix, h0, g_post_mix, g_pre_mlp)

    def up_epilogue(acc):
        r = jnp.maximum(acc, 0.0)
        return r * r, r

    f, relu_up = _mm_cols("mlp_up", n2, wup_g, tm=tm, epilogue=up_epilogue, out_dtypes=(BF16, BF16))
    wdn_g, = gathered(3, [f])
    wdn_full = wdn_g.reshape(-1, d)
    fo = _mm_rows("mlp_down", f, wdn_full, tm=tm, tk=1024)
    dfo, dh2, dg_post_mlp, loss_blk = _loss_head(fo, h1, tgt, g_post_mlp, t_real)
    loss = lax.psum(loss_blk[0, 0], ("x", "y", "c"))

    def reduce_start(tag, fulls, deps):
        lands = [lax.empty((4,) + g.shape[1:], BF16) for g in fulls]
        send, recv, bufs, tok = _remote_start("reduce_%s_d2d_start" % tag, "reduce_d2d", fulls, lands, deps=deps)
        return (send, recv, bufs), tok

    def reduce_middle(tag, state, owns, after):
        send, recv, bufs = state
        k = len(owns)
        bufs = _remote_wait("reduce_%s_d2d_wait" % tag, "reduce_d2d", send, recv, bufs, k, after)
        sums = [_chip_sum("chip_sum_%s%d" % (tag, i), bufs[i], bufs[k + i], owns[i], me_arr) for i in range(k)]
        lands = [lax.empty(sm[0].shape, BF16) for sm in sums]
        send, recv, bufs, tok = _remote_start("reduce_%s_ici_start" % tag, "reduce_ici", [sm[0] for sm in sums], lands)
        return (send, recv, bufs, [sm[1] for sm in sums]), tok

    def reduce_finish(tag, state, after):
        send, recv, bufs, chip_sums = state
        k = len(chip_sums)
        bufs = _remote_wait("reduce_%s_ici_wait" % tag, "reduce_ici", send, recv, bufs, k, after)
        return list(zip(chip_sums, bufs[k:]))

    dup = _mm_nt_blocks("d_up", dfo, wdn_full, tm=tm, tkb=1024, extra=(relu_up,),
                        epilogue=lambda acc, r: (acc * (2.0 * r.astype(F32)),), out_dtypes=(BF16,))[0]
    gw_down, gw_down_own = _mm_tn("dw_down", f, dfo, me_arr, m=f.shape[1], n=d, tma=512, tn=1024, sharded="rows")
    red_down, tok = reduce_start("down", [gw_down], ())
    dn2 = _mm_nt_acc("d_n2", dup, wup_g, tm=tm // 2, deps=[tok])
    gw_up, gw_up_own = _mm_tn("dw_up", n2, dup, me_arr, m=d, n=dup.shape[1], tma=512, tn=1024, sharded="cols")
    red_down, tok = reduce_middle("down", red_down, [gw_down_own], [dn2])
    red_up, tok = reduce_start("up", [gw_up], [tok])
    dh1, dmix, dg_pre_mlp, dg_post_mix = _mid_norm_bwd(dn2, h1, dh2, mix, g_pre_mlp, g_post_mix, deps=[tok])
    dm = _mm_nt_blocks("d_m", dmix, wo_full, tm=tm, tkb=1024)[0]
    red_up, tok = reduce_middle("up", red_up, [gw_up_own], [dm])
    gw_o, gw_o_own = _mm_tn("dw_o", m_mix, dmix, me_arr, m=d, n=d, tma=d // N_DEV, tn=1024, sharded="rows", deps=[tok])
    dyab, dproj, db_gates = _gate_merge_bwd(dm, proj, ya, yb, b_gates, d)
    ycb = d // N_DEV
    da3 = _mm_nt_acc("d_a3", dyab, wpw_g, tm=tm, nb=N_DEV, col_off=0)
    gw_pw, gw_pw_own = _mm_tn("dw_pw", a3, dyab, me_arr, m=dc, n=d, tma=512, tn=ycb, sharded="cols")
    dsb = _mm_nt_acc("d_s", dyab, wso_g, tm=tm, nb=N_DEV, col_off=1)
    gw_so, gw_so_own = _mm_tn("dw_so", s, dyab, me_arr, m=dc, n=d, tma=512, tn=ycb, sharded="cols", b_off=d // ycb)
    red_mix, tok = reduce_start("mix", [gw_pw, gw_so, gw_o], ())
    da1, dln_g, dln_b = _layer_norm_silu_bwd(da3, a1, conf_ln_g, conf_ln_b, deps=[tok])
    dproj, dcw, dcb, dsw = _conv_backward(dproj, proj, da1, dsb, cw_full, sw_full, dc)
    red_mix, tok = reduce_middle("mix", red_mix, [gw_pw_own, gw_so_own, gw_o_own], [dcb])
    in_cb = w_in.shape[2]
    gw_in, gw_in_own = _mm_tn("dw_in", n, dproj, me_arr, m=d, n=proj.shape[1], tma=512, tn=in_cb, sharded="cols",
                              deps=[tok])
    red_in, tok = reduce_start("in", [gw_in], ())
    dn = _mm_nt_acc("d_n", dproj, win_g, tm=tm // 2, deps=[tok])
    red_in, tok = reduce_middle("in", red_in, [gw_in_own], [dn])
    dh0, dg_pre_mix = _pre_norm_bwd(dn, h0, dh1, g_pre_mix, deps=[tok])
    grad_x = dh0[N_META:t_real][None]

    vec_parts = [dg_pre_mix, db_gates, dcb, dln_g, dln_b, dg_post_mix, dg_pre_mlp, dg_post_mlp]
    packed = _pack_small(vec_parts, dh0[:N_META], dcw, dsw)
    small_g = _all_gather("gather_small_grads", [packed])[0]
    vec_names = ["g_pre_mix", "b_gates", "conf_dw_b", "conf_ln_g", "conf_ln_b", "g_post_mix", "g_pre_mlp", "g_post_mlp"]
    env = locals()
    triple = lambda nm, sq: tuple(env[p + nm][0] if sq else env[p + nm] for p in ("", "m_", "v_"))
    small = _small_update(small_g, me_arr, [triple(nm, False) for nm in vec_names],
                          triple("meta", False), triple("conf_dw_w", True), triple("short_dw_w", True))
    results = {}
    for nm, res in zip(vec_names + ["meta"], small[:len(vec_names) + 1]):
        results[nm] = res
    results["conf_dw_w"] = tuple(r[None] for r in small[-2])
    results["short_dw_w"] = tuple(r[None] for r in small[-1])

    def update(nm, chip_sum, landed):
        res = _adamw_shard("adamw_" + nm, env[nm][0], env["m_" + nm][0], env["v_" + nm][0], chip_sum, landed, me_arr)
        results[nm] = tuple(r[None] for r in res)
        return res[0]

    done = [small[0][0]]
    done.append(update("w_down", *reduce_finish("down", red_down, [small_g])[0]))
    done.append(update("w_up", *reduce_finish("up", red_up, [small_g])[0]))
    for nm, pair in zip(["conf_w_pw", "short_w_out", "w_o"], reduce_finish("mix", red_mix, [small_g])):
        done.append(update(nm, *pair))
    update("w_in", *reduce_finish("in", red_in, done)[0])

    order = ["meta", "g_pre_mix", "w_in", "b_gates", "conf_dw_w", "conf_dw_b", "conf_ln_g", "conf_ln_b", "conf_w_pw",
             "short_dw_w", "short_w_out", "w_o", "g_post_mix", "g_pre_mlp", "w_up", "w_down", "g_post_mlp"]
    return (loss, grad_x, *[results[nm][0] for nm in order], *[results[nm][1] for nm in order],
            *[results[nm][2] for nm in order], *[results[nm][3] for nm in order])
```

```python
import jax
import jax.numpy as jnp
from jax import lax
from jax.experimental import pallas as pl
from jax.experimental.pallas import tpu as pltpu

N_DEV = 8
N_META = 16
CONF_K = 31
SHORT_K = 3
RMS_EPS = 1e-6
LN_EPS = 1e-5
ADAM_LR = 0.001
ADAM_B1 = 0.9
ADAM_B2 = 0.999
ADAM_EPS = 1e-08
ADAM_WD = 0.01
ADAM_STEP = 10

LANE = 128
SUB = 8
ROW_TILE = 128
CONV_PAD = 32
CONV_CHUNK = 128
VMEM_LIMIT = 56 * 1024 * 1024

F32 = jnp.float32
BF16 = jnp.bfloat16
MESH = pl.DeviceIdType.MESH
ANY = pl.BlockSpec(memory_space=pl.ANY)
HBM_SPEC = pl.BlockSpec(memory_space=pltpu.HBM)
SEM_SPEC = pl.BlockSpec(memory_space=pltpu.SEMAPHORE)
EFFECT = pltpu.SideEffectType.DATAFLOW_SIDE_EFFECTING


def _params(n_axes):
    return pltpu.CompilerParams(dimension_semantics=("arbitrary",) * n_axes, vmem_limit_bytes=VMEM_LIMIT)


def _sigmoid(z):
    return 1.0 / (1.0 + jnp.exp(-z))


def _colsum8(v):
    r, c = v.shape
    return jnp.sum(v.reshape(r // SUB, SUB, c), axis=0)


def _position():
    x, y, c = lax.axis_index("x"), lax.axis_index("y"), lax.axis_index("c")
    return x, y, c


def _flat(p):
    return 4 * p[0] + 2 * p[1] + p[2]


def _all_gather(name, shards, deps=()):
    n, nd = len(shards), len(deps)

    def body(*refs):
        ins, outs = refs[:n], refs[n + nd:2 * n + nd]
        send_sems, recv_sems, local_sems = refs[2 * n + nd:]
        x, y, c = _position()
        me, sibling = (x, y, c), (x, y, 1 - c)
        chips = [(1 - x, y), (x, 1 - y), (1 - x, 1 - y)]

        def copy(q, k, block, to, src=None):
            dst = outs[q].at[_flat(block)]
            return pltpu.make_async_remote_copy(
                src_ref=dst if src is None else src, dst_ref=dst,
                send_sem=send_sems.at[q, k], recv_sem=recv_sems.at[q, k],
                device_id=to, device_id_type=MESH)

        mine = [pltpu.make_async_copy(ins[q], outs[q].at[_flat(me)], local_sems.at[q]) for q in range(n)]
        for cp in mine:
            cp.start()
        first = []
        for q in range(n):
            first.append(copy(q, 0, me, sibling, src=ins[q]))
            for j, chip in enumerate(chips):
                first.append(copy(q, 1 + j, me, (*chip, c), src=ins[q]))
        for cp in first:
            cp.start()
        passed = []
        for q in range(n):
            for j, chip in enumerate(chips):
                copy(q, 1 + j, (*chip, c), me).wait_recv()
                fwd = copy(q, 4 + j, (*chip, c), sibling)
                fwd.start()
                passed.append(fwd)
        for q in range(n):
            copy(q, 0, sibling, me).wait_recv()
            for j, chip in enumerate(chips):
                copy(q, 4 + j, (*chip, 1 - c), me).wait_recv()
        for cp in first + passed:
            cp.wait_send()
        for cp in mine:
            cp.wait()

    return pl.pallas_call(
        body, name=name,
        in_specs=[ANY] * (n + nd), out_specs=[ANY] * n,
        out_shape=[jax.ShapeDtypeStruct((N_DEV,) + s.shape, s.dtype) for s in shards],
        scratch_shapes=[pltpu.SemaphoreType.DMA((n, 7)), pltpu.SemaphoreType.DMA((n, 7)),
                        pltpu.SemaphoreType.DMA((n,))],
    )(*shards, *deps)


N_COPIES = {"gather_ici": 4, "gather_d2d": 3, "reduce_d2d": 4, "reduce_ici": 3}


def _copy_plan(kind):
    x, y, c = _position()
    me, sibling = (x, y, c), (x, y, 1 - c)
    chips = [(1 - x, y), (x, 1 - y), (1 - x, 1 - y)]
    if kind == "gather_ici":
        return [(_flat(me), _flat(me), sibling)] + [(_flat(me), _flat(me), (*ch, c)) for ch in chips]
    if kind == "gather_d2d":
        return [(_flat((*ch, c)), _flat((*ch, c)), sibling) for ch in chips]
    if kind == "reduce_d2d":
        return [(2 * chip + (1 - c), chip, sibling) for chip in range(4)]
    return [(2 * ch[0] + ch[1], 2 * x + y, (*ch, c)) for ch in chips]


def _planned_copies(kind, srcs, dsts, send_sems, recv_sems):
    plan = _copy_plan(kind)
    return [pltpu.make_async_remote_copy(
        src_ref=src.at[s_slot], dst_ref=dst.at[d_slot],
        send_sem=send_sems.at[q * len(plan) + k], recv_sem=recv_sems.at[q * len(plan) + k],
        device_id=to, device_id_type=MESH)
        for q, (src, dst) in enumerate(zip(srcs, dsts)) for k, (s_slot, d_slot, to) in enumerate(plan)]


def _remote_start(name, kind, srcs, lands=None, deps=()):
    n = len(srcs)
    bufs = list(srcs) + ([] if lands is None else list(lands))
    nb, nd = len(bufs), len(deps)
    nsem = n * N_COPIES[kind]

    def body(*refs):
        ins = refs[:nb]
        send_sems, recv_sems = refs[nb + nd], refs[nb + nd + 1]
        token = refs[-1]
        for cp in _planned_copies(kind, ins[:n], ins[:n] if lands is None else ins[n:], send_sems, recv_sems):
            cp.start()
        token[...] = jnp.zeros_like(token)

    outs = pl.pallas_call(
        body, name=name,
        out_shape=(pltpu.SemaphoreType.DMA((nsem,)), pltpu.SemaphoreType.DMA((nsem,)),
                   *[pltpu.HBM(b.shape, b.dtype) for b in bufs], jax.ShapeDtypeStruct((SUB, LANE), F32)),
        in_specs=[HBM_SPEC] * nb + [ANY] * nd,
        out_specs=(SEM_SPEC, SEM_SPEC, *[HBM_SPEC] * nb, pl.BlockSpec(memory_space=pltpu.VMEM)),
        input_output_aliases={i: 2 + i for i in range(nb)},
        compiler_params=pltpu.CompilerParams(has_side_effects=EFFECT),
    )(*[pltpu.with_memory_space_constraint(b, pltpu.HBM) for b in bufs], *deps)
    return outs[0], outs[1], list(outs[2:2 + nb]), outs[-1]


def _remote_wait(name, kind, send_sems, recv_sems, bufs, n, after):
    nb, na = len(bufs), len(after)
    same = nb == n

    def body(*refs):
        ins = refs[:nb]
        sends, recvs = refs[nb], refs[nb + 1]
        for cp in _planned_copies(kind, ins[:n], ins[:n] if same else ins[n:], sends, recvs):
            cp.wait_send()
            cp.wait_recv()

    outs = pl.pallas_call(
        body, name=name,
        out_shape=[pltpu.HBM(b.shape, b.dtype) for b in bufs],
        in_specs=[HBM_SPEC] * nb + [SEM_SPEC, SEM_SPEC] + [ANY] * na,
        out_specs=[HBM_SPEC] * nb,
        input_output_aliases={i: i for i in range(nb)},
        compiler_params=pltpu.CompilerParams(has_side_effects=EFFECT),
    )(*bufs, send_sems, recv_sems, *after)
    return list(outs)


def _mm_cols(name, a, w, *, tm, nb=1, epilogue=None, out_dtypes=(F32,)):
    t, k = a.shape
    nblk, _, cb = w.shape

    def body(a_ref, w_ref, *o_refs):
        av = a_ref[...]
        for b in range(nb):
            acc = jnp.dot(av, w_ref[b], preferred_element_type=F32)
            outs = (acc,) if epilogue is None else epilogue(acc)
            for o_ref, o in zip(o_refs, outs):
                o_ref[:, b * cb:(b + 1) * cb] = o.astype(o_ref.dtype)

    return pl.pallas_call(
        body, name=name, grid=(nblk // nb, t // tm),
        in_specs=[pl.BlockSpec((tm, k), lambda j, i: (i, 0)),
                  pl.BlockSpec((nb, k, cb), lambda j, i: (j, 0, 0))],
        out_specs=[pl.BlockSpec((tm, nb * cb), lambda j, i: (i, j)) for _ in out_dtypes],
        out_shape=[jax.ShapeDtypeStruct((t, nblk * cb), dt) for dt in out_dtypes],
        compiler_params=_params(2),
    )(a, w)


def _mm_rows(name, a, w2d, *, tm, tk):
    t = a.shape[0]
    kf, n = w2d.shape

    def body(a_ref, w_ref, o_ref):
        acc = jnp.dot(a_ref[...], w_ref[...], preferred_element_type=F32)

        @pl.when(pl.program_id(1) == 0)
        def _():
            o_ref[...] = acc

        @pl.when(pl.program_id(1) > 0)
        def _():
            o_ref[...] += acc

    return pl.pallas_call(
        body, name=name, grid=(t // tm, kf // tk),
        in_specs=[pl.BlockSpec((tm, tk), lambda i, kk: (i, kk)),
                  pl.BlockSpec((tk, n), lambda i, kk: (kk, 0))],
        out_specs=pl.BlockSpec((tm, n), lambda i, kk: (i, 0)),
        out_shape=jax.ShapeDtypeStruct((t, n), F32),
        compiler_params=_params(2),
    )(a, w2d)


def _mm_nt_acc(name, dy, w, *, tm, nb=1, col_off=0, deps=()):
    t = dy.shape[0]
    nblk, k, cb = w.shape

    def body(dy_ref, w_ref, *rest):
        o_ref = rest[-1]
        acc = None
        for b in range(nb):
            d = lax.dot_general(dy_ref[:, b * cb:(b + 1) * cb], w_ref[b], (((1,), (1,)), ((), ())),
                                preferred_element_type=F32)
            acc = d if acc is None else acc + d

        @pl.when(pl.program_id(1) == 0)
        def _():
            o_ref[...] = acc

        @pl.when(pl.program_id(1) > 0)
        def _():
            o_ref[...] += acc

    return pl.pallas_call(
        body, name=name, grid=(t // tm, nblk // nb),
        in_specs=[pl.BlockSpec((tm, nb * cb), lambda i, j: (i, col_off + j)),
                  pl.BlockSpec((nb, k, cb), lambda i, j: (j, 0, 0))] + [ANY] * len(deps),
        out_specs=pl.BlockSpec((tm, k), lambda i, j: (i, 0)),
        out_shape=jax.ShapeDtypeStruct((t, k), F32),
        compiler_params=_params(2),
    )(dy, w, *deps)


def _mm_nt_blocks(name, dy, w2d, *, tm, tkb, extra=(), epilogue=None, out_dtypes=(F32,)):
    t, n = dy.shape
    kf = w2d.shape[0]
    ne = len(extra)

    def body(dy_ref, w_ref, *rest):
        acc = lax.dot_general(dy_ref[...], w_ref[...], (((1,), (1,)), ((), ())), preferred_element_type=F32)
        outs = (acc,) if epilogue is None else epilogue(acc, *[e[...] for e in rest[:ne]])
        for o_ref, o in zip(rest[ne:], outs):
            o_ref[...] = o.astype(o_ref.dtype)

    return pl.pallas_call(
        body, name=name, grid=(kf // tkb, t // tm),
        in_specs=[pl.BlockSpec((tm, n), lambda kb, i: (i, 0)),
                  pl.BlockSpec((tkb, n), lambda kb, i: (kb, 0))]
                 + [pl.BlockSpec((tm, tkb), lambda kb, i: (i, kb)) for _ in extra],
        out_specs=[pl.BlockSpec((tm, tkb), lambda kb, i: (i, kb)) for _ in out_dtypes],
        out_shape=[jax.ShapeDtypeStruct((t, kf), dt) for dt in out_dtypes],
        compiler_params=_params(2),
    )(dy, w2d, *extra)


def _mm_tn(name, a, b, me_arr, *, m, n, tma, tn, sharded, a_off=0, b_off=0, deps=()):
    t = a.shape[0]
    if sharded == "cols":
        cb = n // N_DEV
        q = cb // tn
        full_shape, own_shape = (N_DEV, m, cb), (m, cb)
        full_spec = pl.BlockSpec((1, tma, tn), lambda i, j, me: (j // q, i, j % q))
    else:
        kb = m // N_DEV
        p = kb // tma
        full_shape, own_shape = (m, n), (kb, n)
        full_spec = pl.BlockSpec((tma, tn), lambda i, j, me: (i, j))

    def body(me_ref, a_ref, b_ref, *rest):
        full_ref, own_ref, stage, sem = rest[len(deps):]
        i, j = pl.program_id(0), pl.program_id(1)
        acc = lax.dot_general(a_ref[...], b_ref[...], (((0,), (0,)), ((), ())), preferred_element_type=F32)
        if sharded == "cols":
            full_ref[0] = acc.astype(BF16)
            owner, r0, c0 = j // q, i * tma, (j % q) * tn
        else:
            full_ref[...] = acc.astype(BF16)
            owner, r0, c0 = i // p, (i % p) * tma, j * tn

        @pl.when(owner == me_ref[0])
        def _():
            stage[...] = acc
            cp = pltpu.make_async_copy(
                stage, own_ref.at[pl.ds(pl.multiple_of(r0, tma), tma), pl.ds(pl.multiple_of(c0, tn), tn)], sem)
            cp.start()
            cp.wait()

    full, own = pl.pallas_call(
        body, name=name,
        grid_spec=pltpu.PrefetchScalarGridSpec(
            num_scalar_prefetch=1, grid=(m // tma, n // tn),
            in_specs=[pl.BlockSpec((t, tma), lambda i, j, me: (0, a_off + i)),
                      pl.BlockSpec((t, tn), lambda i, j, me: (0, b_off + j))] + [ANY] * len(deps),
            out_specs=[full_spec, ANY],
            scratch_shapes=[pltpu.VMEM((tma, tn), F32), pltpu.SemaphoreType.DMA(())]),
        out_shape=[jax.ShapeDtypeStruct(full_shape, BF16), jax.ShapeDtypeStruct(own_shape, F32)],
        compiler_params=_params(2),
    )(me_arr, a, b, *deps)
    if sharded == "rows":
        full = full.reshape(N_DEV, m // N_DEV, n)
    return full, own


def _row_call(name, body, t, row_ins, full_ins, row_outs, acc_outs, scratch=(), deps=()):
    tm = ROW_TILE
    nin = len(row_ins) + len(full_ins)

    def without_deps(*refs):
        body(*refs[:nin], *refs[nin + len(deps):])

    return pl.pallas_call(
        without_deps, name=name, grid=(t // tm,),
        in_specs=[pl.BlockSpec((tm, a.shape[1]), lambda i: (i, 0)) for a in row_ins]
                 + [pl.BlockSpec(a.shape, lambda i: (0, 0)) for a in full_ins] + [ANY] * len(deps),
        out_specs=[pl.BlockSpec((tm, c), lambda i: (i, 0)) for c, _ in row_outs]
                  + [pl.BlockSpec((r, c), lambda i: (0, 0)) for r, c in acc_outs],
        out_shape=[jax.ShapeDtypeStruct((t, c), dt) for c, dt in row_outs]
                  + [jax.ShapeDtypeStruct((r, c), F32) for r, c in acc_outs],
        scratch_shapes=list(scratch),
        compiler_params=_params(1),
    )(*row_ins, *full_ins, *deps)


def _accumulate(ref, v):
    @pl.when(pl.program_id(0) == 0)
    def _():
        ref[...] = v

    @pl.when(pl.program_id(0) > 0)
    def _():
        ref[...] += v


def _rms(v):
    return lax.rsqrt(jnp.mean(v * v, axis=-1, keepdims=True) + RMS_EPS)


def _rms_bwd(dout, u, r, g):
    du = dout * g
    dx = r * (du - u * jnp.mean(du * u, axis=-1, keepdims=True))
    return dx, _colsum8(dout * u)


def _pre_norm(h0, g):
    t, d = h0.shape

    def body(h_ref, g_ref, n_ref):
        h = h_ref[...]
        n_ref[...] = (h * _rms(h) * g_ref[...]).astype(BF16)

    return _row_call("pre_norm", body, t, [h0], [g], [(d, BF16)], [])[0]


def _post_mix(mix, h0, g_post, g_pre, deps=()):
    t, d = h0.shape

    def body(mix_ref, h0_ref, gp_ref, gq_ref, h1_ref, n2_ref):
        mix_v = mix_ref[...]
        h1 = h0_ref[...] + mix_v * _rms(mix_v) * gp_ref[...]
        h1_ref[...] = h1
        n2_ref[...] = (h1 * _rms(h1) * gq_ref[...]).astype(BF16)

    return _row_call("post_mix", body, t, [mix, h0], [g_post, g_pre], [(d, F32), (d, BF16)], [], deps=deps)


def _loss_head(fo, h1, tgt, g_post_mlp, t_real):
    t, d = h1.shape

    def body(fo_ref, h1_ref, tgt_ref, g_ref, dfo_ref, dh2_ref, dg_ref, loss_ref, lacc):
        i = pl.program_id(0)
        fo_v = fo_ref[...]
        g = g_ref[...]
        r = _rms(fo_v)
        u = fo_v * r
        h2 = h1_ref[...] + u * g
        row = i * ROW_TILE + lax.broadcasted_iota(jnp.int32, (ROW_TILE, 1), 0)
        valid = jnp.logical_and(row >= N_META, row < t_real)
        diff = jnp.where(valid, h2 - tgt_ref[...], 0.0)
        dh2 = diff * (1.0 / d)
        dh2_ref[...] = dh2
        dfo, dg = _rms_bwd(dh2, u, r, g)
        dfo_ref[...] = dfo.astype(BF16)
        _accumulate(dg_ref, dg)
        _accumulate(lacc, _colsum8(diff * diff))

        @pl.when(i == pl.num_programs(0) - 1)
        def _():
            loss_ref[...] = jnp.full((SUB, LANE), (0.5 / d) * jnp.sum(lacc[...]), F32)

    return _row_call("loss_head", body, t, [fo, h1, tgt], [g_post_mlp],
                     [(d, BF16), (d, F32)], [(SUB, d), (SUB, LANE)], scratch=[pltpu.VMEM((SUB, d), F32)])


def _mid_norm_bwd(dn2, h1, dh2, mix, g_pre_mlp, g_post_mix, deps=()):
    t, d = h1.shape

    def body(dn2_ref, h1_ref, dh2_ref, mix_ref, gq_ref, gp_ref, dh1_ref, dmix_ref, dgq_ref, dgp_ref):
        h1 = h1_ref[...]
        r3 = _rms(h1)
        dx, dgq = _rms_bwd(dn2_ref[...], h1 * r3, r3, gq_ref[...])
        dh1 = dh2_ref[...] + dx
        dh1_ref[...] = dh1
        mix_v = mix_ref[...]
        r2 = _rms(mix_v)
        dmix, dgp = _rms_bwd(dh1, mix_v * r2, r2, gp_ref[...])
        dmix_ref[...] = dmix.astype(BF16)
        _accumulate(dgq_ref, dgq)
        _accumulate(dgp_ref, dgp)

    return _row_call("mid_norm_bwd", body, t, [dn2, h1, dh2, mix], [g_pre_mlp, g_post_mix],
                     [(d, F32), (d, BF16)], [(SUB, d), (SUB, d)], deps=deps)


def _pre_norm_bwd(dn, h0, dh1, g_pre_mix, deps=()):
    t, d = h0.shape

    def body(dn_ref, h0_ref, dh1_ref, g_ref, dh0_ref, dg_ref):
        h0 = h0_ref[...]
        r = _rms(h0)
        dx, dg = _rms_bwd(dn_ref[...], h0 * r, r, g_ref[...])
        dh0_ref[...] = dh1_ref[...] + dx
        _accumulate(dg_ref, dg)

    return _row_call("pre_norm_bwd", body, t, [dn, h0, dh1], [g_pre_mix], [(d, F32)], [(SUB, d)], deps=deps)


def _layer_norm_silu(a1, ln_g, ln_b):
    t, c = a1.shape

    def body(a1_ref, g_ref, b_ref, a3_ref):
        a = a1_ref[...]
        mu = jnp.mean(a, axis=-1, keepdims=True)
        xc = a - mu
        rstd = lax.rsqrt(jnp.mean(xc * xc, axis=-1, keepdims=True) + LN_EPS)
        z = xc * rstd * g_ref[...] + b_ref[...]
        a3_ref[...] = (z * _sigmoid(z)).astype(BF16)

    return _row_call("layer_norm_silu", body, t, [a1], [ln_g, ln_b], [(c, BF16)], [])[0]


def _layer_norm_silu_bwd(da3, a1, ln_g, ln_b, deps=()):
    t, c = a1.shape

    def body(da3_ref, a1_ref, g_ref, b_ref, da1_ref, dg_ref, db_ref):
        a = a1_ref[...]
        g = g_ref[...]
        mu = jnp.mean(a, axis=-1, keepdims=True)
        xc = a - mu
        rstd = lax.rsqrt(jnp.mean(xc * xc, axis=-1, keepdims=True) + LN_EPS)
        xhat = xc * rstd
        z = xhat * g + b_ref[...]
        sg = _sigmoid(z)
        dz = da3_ref[...] * (sg * (1.0 + z * (1.0 - sg)))
        dxhat = dz * g
        da1_ref[...] = rstd * (dxhat - jnp.mean(dxhat, axis=-1, keepdims=True)
                               - xhat * jnp.mean(dxhat * xhat, axis=-1, keepdims=True))
        _accumulate(dg_ref, _colsum8(dz * xhat))
        _accumulate(db_ref, _colsum8(dz))

    return _row_call("layer_norm_silu_bwd", body, t, [da3, a1], [ln_g, ln_b], [(c, F32)], [(SUB, c), (SUB, c)], deps=deps)


def _gate_merge(proj, ya, yb, b_gates, d, deps=()):
    t = proj.shape[0]
    w = 1024
    nh = d // w
    ga0 = (proj.shape[1] - 2 * d) // w

    def body(pa_ref, pb_ref, ya_ref, yb_ref, ba_ref, bb_ref, *rest):
        m_ref = rest[-1]
        ga = _sigmoid(pa_ref[...] + ba_ref[...])
        gb = _sigmoid(pb_ref[...] + bb_ref[...])
        m_ref[...] = (ga * ya_ref[...] + gb * yb_ref[...]).astype(BF16)

    tm = ROW_TILE
    return pl.pallas_call(
        body, name="gate_merge", grid=(nh, t // tm),
        in_specs=[pl.BlockSpec((tm, w), lambda h, i: (i, ga0 + h)),
                  pl.BlockSpec((tm, w), lambda h, i: (i, ga0 + nh + h)),
                  pl.BlockSpec((tm, w), lambda h, i: (i, h)),
                  pl.BlockSpec((tm, w), lambda h, i: (i, h)),
                  pl.BlockSpec((1, w), lambda h, i: (0, h)),
                  pl.BlockSpec((1, w), lambda h, i: (0, nh + h))] + [ANY] * len(deps),
        out_specs=pl.BlockSpec((tm, w), lambda h, i: (i, h)),
        out_shape=jax.ShapeDtypeStruct((t, d), BF16),
        compiler_params=_params(2),
    )(proj, proj, ya, yb, b_gates, b_gates, *deps)


def _gate_merge_bwd(dm, proj, ya, yb, b_gates, d):
    t, cols = proj.shape
    w = 1024
    nh = d // w
    ga0 = (cols - 2 * d) // w

    def body(dm_ref, p_ref, ya_ref, yb_ref, b_ref, dy_ref, dp_ref, db_ref):
        q = pl.program_id(0)
        g = _sigmoid(p_ref[...] + b_ref[...])
        dm_v = dm_ref[...]
        y = jnp.where(q < nh, ya_ref[...], yb_ref[...])
        dy_ref[...] = (dm_v * g).astype(BF16)
        dp = dm_v * y * g * (1.0 - g)
        dp_ref[...] = dp.astype(BF16)

        @pl.when(pl.program_id(1) == 0)
        def _():
            db_ref[...] = _colsum8(dp)

        @pl.when(pl.program_id(1) > 0)
        def _():
            db_ref[...] += _colsum8(dp)

    tm = ROW_TILE
    return pl.pallas_call(
        body, name="gate_merge_bwd", grid=(2 * nh, t // tm),
        in_specs=[pl.BlockSpec((tm, w), lambda q, i: (i, q % nh)),
                  pl.BlockSpec((tm, w), lambda q, i: (i, ga0 + q)),
                  pl.BlockSpec((tm, w), lambda q, i: (i, q % nh)),
                  pl.BlockSpec((tm, w), lambda q, i: (i, q % nh)),
                  pl.BlockSpec((1, w), lambda q, i: (0, q))],
        out_specs=[pl.BlockSpec((tm, w), lambda q, i: (i, q)),
                   pl.BlockSpec((tm, w), lambda q, i: (i, ga0 + q)),
                   pl.BlockSpec((SUB, w), lambda q, i: (0, q))],
        out_shape=[jax.ShapeDtypeStruct((t, 2 * d), BF16), jax.ShapeDtypeStruct((t, cols), BF16),
                   jax.ShapeDtypeStruct((SUB, 2 * d), F32)],
        compiler_params=_params(2),
    )(dm, proj, ya, yb, b_gates)


def _causal_conv(xp_ref, w_ref, ntap, r0):
    n = CONV_CHUNK + CONV_PAD
    win = xp_ref[pl.ds(r0, n), :]
    acc = None
    for k in range(ntap):
        back = ntap - 1 - k
        shifted = pltpu.roll(win, n - (CONV_PAD - back), 0)
        term = w_ref[k:k + 1, :] * shifted[:CONV_CHUNK]
        acc = term if acc is None else acc + term
    return acc


def _anticausal_conv(xp_ref, w_ref, ntap, r0):
    n = CONV_CHUNK + CONV_PAD
    win = xp_ref[pl.ds(pl.multiple_of(CONV_PAD + r0, CONV_PAD), n), :]
    acc = None
    for k in range(ntap):
        ahead = ntap - 1 - k
        shifted = win if ahead == 0 else pltpu.roll(win, n - ahead, 0)
        term = w_ref[k:k + 1, :] * shifted[:CONV_CHUNK]
        acc = term if acc is None else acc + term
    return acc


def _conv_weight_grad(dw_ref, d_chunk, xp_ref, ntap, r0):
    n = CONV_CHUNK + CONV_PAD
    win = xp_ref[pl.ds(r0, n), :]
    for k in range(ntap):
        back = ntap - 1 - k
        shifted = pltpu.roll(win, n - (CONV_PAD - back), 0)
        dw_ref[k * SUB:(k + 1) * SUB, :] += _colsum8(d_chunk * shifted[:CONV_CHUNK])


def _zero_pads(ref, t):
    ref[0:CONV_PAD, :] = jnp.zeros((CONV_PAD, LANE), F32)
    ref[CONV_PAD + t:CONV_PAD + t + CONV_PAD, :] = jnp.zeros((CONV_PAD, LANE), F32)


def _for_chunks(t, fn):
    def step(idx, carry):
        fn(pl.multiple_of(idx * CONV_CHUNK, CONV_CHUNK))
        return carry

    lax.fori_loop(0, t // CONV_CHUNK, step, 0)


def _conv_forward(proj, conf_w, conf_b, short_w, dc, deps=()):
    t = proj.shape[0]
    nc = dc // LANE

    def body(av_ref, ag_ref, bg_ref, cg_ref, v_ref, cw_ref, cb_ref, sw_ref, *rest):
        a1_ref, s_ref, xa, xb = rest[len(deps):]
        _zero_pads(xa, t)
        _zero_pads(xb, t)
        xa[CONV_PAD:CONV_PAD + t, :] = av_ref[...] * _sigmoid(ag_ref[...])
        xb[CONV_PAD:CONV_PAD + t, :] = cg_ref[...] * v_ref[...]

        def chunk(r0):
            rs = pl.ds(r0, CONV_CHUNK)
            a1_ref[rs, :] = _causal_conv(xa, cw_ref, CONF_K, r0) + cb_ref[...]
            s_ref[rs, :] = (bg_ref[rs, :] * _causal_conv(xb, sw_ref, SHORT_K, r0)).astype(BF16)

        _for_chunks(t, chunk)

    col = lambda g: pl.BlockSpec((t, LANE), lambda c, g=g: (0, g * nc + c))
    return pl.pallas_call(
        body, name="conv_forward", grid=(nc,),
        in_specs=[col(0), col(1), col(2), col(3), col(4),
                  pl.BlockSpec((CONF_K, LANE), lambda c: (0, c)),
                  pl.BlockSpec((1, LANE), lambda c: (0, c)),
                  pl.BlockSpec((SHORT_K, LANE), lambda c: (0, c))] + [ANY] * len(deps),
        out_specs=[pl.BlockSpec((t, LANE), lambda c: (0, c)), pl.BlockSpec((t, LANE), lambda c: (0, c))],
        out_shape=[jax.ShapeDtypeStruct((t, dc), F32), jax.ShapeDtypeStruct((t, dc), BF16)],
        scratch_shapes=[pltpu.VMEM((t + 2 * CONV_PAD, LANE), F32), pltpu.VMEM((t + 2 * CONV_PAD, LANE), F32)],
        compiler_params=_params(1),
    )(proj, proj, proj, proj, proj, conf_w, conf_b, short_w, *deps)


def _conv_backward(dproj, proj, da1, ds, conf_w, short_w, dc):
    t = proj.shape[0]
    nc = dc // LANE

    def body(dp_in, av_ref, ag_ref, bg_ref, cg_ref, v_ref, da1_ref, ds_ref, cw_ref, sw_ref,
             dp_ref, dcw_ref, dcb_ref, dsw_ref, xa, xb, da, db, stage, sems):
        del dp_in
        c = pl.program_id(0)
        for ref in (xa, xb, da, db):
            _zero_pads(ref, t)
        xa[CONV_PAD:CONV_PAD + t, :] = av_ref[...] * _sigmoid(ag_ref[...])
        xb[CONV_PAD:CONV_PAD + t, :] = cg_ref[...] * v_ref[...]
        da[CONV_PAD:CONV_PAD + t, :] = da1_ref[...]
        dcw_ref[...] = jnp.zeros(dcw_ref.shape, F32)
        dsw_ref[...] = jnp.zeros(dsw_ref.shape, F32)
        dcb_ref[...] = jnp.zeros(dcb_ref.shape, F32)

        def through_gate(r0):
            rs = pl.ds(r0, CONV_CHUNK)
            ds_c = ds_ref[rs, :]
            stage[2, rs, :] = (ds_c * _causal_conv(xb, sw_ref, SHORT_K, r0)).astype(BF16)
            db[pl.ds(pl.multiple_of(CONV_PAD + r0, CONV_PAD), CONV_CHUNK), :] = ds_c * bg_ref[rs, :]

        _for_chunks(t, through_gate)

        def through_convs(r0):
            rs = pl.ds(r0, CONV_CHUNK)
            da0 = _anticausal_conv(da, cw_ref, CONF_K, r0)
            sg = _sigmoid(ag_ref[rs, :])
            stage[0, rs, :] = (da0 * sg).astype(BF16)
            stage[1, rs, :] = (da0 * av_ref[rs, :] * sg * (1.0 - sg)).astype(BF16)
            dcv = _anticausal_conv(db, sw_ref, SHORT_K, r0)
            stage[3, rs, :] = (dcv * v_ref[rs, :]).astype(BF16)
            stage[4, rs, :] = (dcv * cg_ref[rs, :]).astype(BF16)
            da1_c = da1_ref[rs, :]
            _conv_weight_grad(dcw_ref, da1_c, xa, CONF_K, r0)
            _conv_weight_grad(dsw_ref, ds_ref[rs, :] * bg_ref[rs, :], xb, SHORT_K, r0)
            dcb_ref[...] += _colsum8(da1_c)

        _for_chunks(t, through_convs)
        copies = [pltpu.make_async_copy(
            stage.at[g], dp_ref.at[:, pl.ds(pl.multiple_of((g * nc + c) * LANE, LANE), LANE)], sems.at[g])
            for g in range(5)]
        for cp in copies:
            cp.start()
        for cp in copies:
            cp.wait()

    col = lambda g: pl.BlockSpec((t, LANE), lambda c, g=g: (0, g * nc + c))
    blk = pl.BlockSpec((t, LANE), lambda c: (0, c))
    return pl.pallas_call(
        body, name="conv_backward", grid=(nc,),
        in_specs=[ANY, col(0), col(1), col(2), col(3), col(4), blk, blk,
                  pl.BlockSpec((CONF_K, LANE), lambda c: (0, c)),
                  pl.BlockSpec((SHORT_K, LANE), lambda c: (0, c))],
        out_specs=[ANY,
                   pl.BlockSpec((CONF_K * SUB, LANE), lambda c: (0, c)),
                   pl.BlockSpec((SUB, LANE), lambda c: (0, c)),
                   pl.BlockSpec((SHORT_K * SUB, LANE), lambda c: (0, c))],
        out_shape=[jax.ShapeDtypeStruct(dproj.shape, dproj.dtype),
                   jax.ShapeDtypeStruct((CONF_K * SUB, dc), F32),
                   jax.ShapeDtypeStruct((SUB, dc), F32),
                   jax.ShapeDtypeStruct((SHORT_K * SUB, dc), F32)],
        scratch_shapes=[pltpu.VMEM((t + 2 * CONV_PAD, LANE), F32)] * 4
                       + [pltpu.VMEM((5, t, LANE), BF16), pltpu.SemaphoreType.DMA((5,))],
        input_output_aliases={0: 0},
        compiler_params=_params(1),
    )(dproj, proj, proj, proj, proj, proj, da1, ds, conf_w, short_w)


def _adamw_math(w, g, m, v):
    m = ADAM_B1 * m + (1.0 - ADAM_B1) * g
    v = ADAM_B2 * v + (1.0 - ADAM_B2) * (g * g)
    m_hat = m / (1.0 - ADAM_B1 ** ADAM_STEP)
    v_hat = v / (1.0 - ADAM_B2 ** ADAM_STEP)
    delta = -ADAM_LR * (m_hat / (jnp.sqrt(v_hat) + ADAM_EPS) + ADAM_WD * w)
    return delta, m, v


def _cast_into_slot(name, w, me_arr):
    r, c = w.shape
    tr = 256

    def body(me_ref, w_ref, o_ref):
        del me_ref
        o_ref[0] = w_ref[...].astype(BF16)

    return pl.pallas_call(
        body, name=name,
        grid_spec=pltpu.PrefetchScalarGridSpec(
            num_scalar_prefetch=1, grid=(r // tr,),
            in_specs=[pl.BlockSpec((tr, c), lambda i, me: (i, 0))],
            out_specs=pl.BlockSpec((1, tr, c), lambda i, me: (me[0], i, 0))),
        out_shape=jax.ShapeDtypeStruct((N_DEV, r, c), BF16),
        compiler_params=_params(1),
    )(me_arr, w)


def _chip_sum(name, full, from_sibling, own, me_arr):
    _, r, c = full.shape
    tr = min(r, 512)

    def body(me_ref, full_ref, sib_ref, own_ref, sums_ref, mine_ref):
        theirs = sib_ref[0].astype(F32)
        sums_ref[0] = (full_ref[0].astype(F32) + theirs).astype(BF16)

        @pl.when(pl.program_id(1) == me_ref[0] // 2)
        def _():
            mine_ref[...] = own_ref[...] + theirs

    return pl.pallas_call(
        body, name=name,
        grid_spec=pltpu.PrefetchScalarGridSpec(
            num_scalar_prefetch=1, grid=(r // tr, 4),
            in_specs=[pl.BlockSpec((1, tr, c), lambda i, chip, me: (2 * chip + me[0] % 2, i, 0)),
                      pl.BlockSpec((1, tr, c), lambda i, chip, me: (chip, i, 0)),
                      pl.BlockSpec((tr, c), lambda i, chip, me: (i, 0))],
            out_specs=[pl.BlockSpec((1, tr, c), lambda i, chip, me: (chip, i, 0)),
                       pl.BlockSpec((tr, c), lambda i, chip, me: (i, 0))]),
        out_shape=[jax.ShapeDtypeStruct((4, r, c), BF16), jax.ShapeDtypeStruct((r, c), F32)],
        compiler_params=_params(2),
    )(me_arr, full, from_sibling, own)


def _adamw_shard(name, w, m, v, parts, me_arr):
    r, c = w.shape
    tr = 128
    np_ = len(parts)
    per = r // np_ // tr

    def body(me_ref, w_ref, m_ref, v_ref, *rest):
        g_out, d_out, m_out, v_out = rest[4 * np_:]
        g = None
        for p in range(np_):
            gp = rest[4 * p][...]
            for l_ref in rest[4 * p + 1:4 * p + 4]:
                gp = gp + l_ref[0].astype(F32)
            g = gp if g is None else jnp.where(pl.program_id(0) // per == p, gp, g)
        delta, m_new, v_new = _adamw_math(w_ref[...], g, m_ref[...], v_ref[...])
        g_out[...] = g
        d_out[...] = delta
        m_out[...] = m_new
        v_out[...] = v_new

    tile = pl.BlockSpec((tr, c), lambda i, me: (i, 0))
    part_specs, part_args = [], []
    for p, (g_chip, landed) in enumerate(parts):
        row = lambda i, p=p: jnp.clip(i - p * per, 0, per - 1)
        part_specs.append(pl.BlockSpec((tr, c), lambda i, me, row=row: (row(i), 0)))
        part_specs += [pl.BlockSpec((1, tr, c), lambda i, me, k=k, row=row: ((me[0] // 2 + k) % 4, row(i), 0))
                       for k in range(1, 4)]
        part_args += [g_chip, landed, landed, landed]
    return pl.pallas_call(
        body, name=name,
        grid_spec=pltpu.PrefetchScalarGridSpec(
            num_scalar_prefetch=1, grid=(r // tr,),
            in_specs=[tile] * 3 + part_specs, out_specs=[tile] * 4),
        out_shape=[jax.ShapeDtypeStruct((r, c), F32)] * 4,
        compiler_params=_params(1),
    )(me_arr, w, m, v, *part_args)


SMALL_W = 1024
VEC_ROWS = 16
META_ROW0 = 16
CONF_ROW0 = 64
SHORT_ROW0 = 96
SMALL_ROWS = 104


def _pack_small(vec_parts, dmeta, dcw, dsw):
    widths = [p.shape[1] for p in vec_parts]
    nv = len(vec_parts)

    def body(*refs):
        parts, (dmeta_ref, dcw_ref, dsw_ref, out_ref) = refs[:nv], refs[nv:]
        out_ref[...] = jnp.zeros((SMALL_ROWS, SMALL_W), F32)
        row = 0
        for p_ref, wd in zip(parts, widths):
            s = jnp.sum(p_ref[...], axis=0, keepdims=True)
            for h in range(wd // SMALL_W):
                out_ref[row:row + 1, :] = s[:, h * SMALL_W:(h + 1) * SMALL_W]
                row += 1
        for h in range(dmeta_ref.shape[1] // SMALL_W):
            out_ref[META_ROW0 + h * N_META:META_ROW0 + (h + 1) * N_META, :] = dmeta_ref[:, h * SMALL_W:(h + 1) * SMALL_W]
        for k in range(CONF_K):
            out_ref[CONF_ROW0 + k:CONF_ROW0 + k + 1, :] = jnp.sum(dcw_ref[k * SUB:(k + 1) * SUB, :], axis=0, keepdims=True)
        for k in range(SHORT_K):
            out_ref[SHORT_ROW0 + k:SHORT_ROW0 + k + 1, :] = jnp.sum(dsw_ref[k * SUB:(k + 1) * SUB, :], axis=0, keepdims=True)

    return pl.pallas_call(
        body, name="pack_small",
        out_shape=jax.ShapeDtypeStruct((SMALL_ROWS, SMALL_W), F32),
        compiler_params=pltpu.CompilerParams(vmem_limit_bytes=VMEM_LIMIT),
    )(*vec_parts, dmeta, dcw, dsw)


def _small_update(gathered, me_arr, vec_params, meta_p, conf_p, short_p):
    widths = [p[0].shape[1] for p in vec_params]
    nv = len(vec_params)
    mcols = meta_p[0].shape[1]
    per_row = SMALL_W // mcols

    def body(me_ref, gv_ref, gm_ref, gc_ref, gs_ref, *rest):
        del me_ref
        ins, outs = rest[:3 * (nv + 3)], rest[3 * (nv + 3):]

        def total(ref, r0, rows):
            s = ref[0, r0:r0 + rows, :]
            for dev in range(1, N_DEV):
                s = s + ref[dev, r0:r0 + rows, :]
            return s

        grads = []
        row = 0
        for wd in widths:
            pieces = [total(gv_ref, row + h, 1) for h in range(wd // SMALL_W)]
            grads.append(pieces[0] if len(pieces) == 1 else jnp.concatenate(pieces, axis=1))
            row += len(pieces)
        grads.append(total(gm_ref, 0, N_META))
        grads.append(total(gc_ref, 0, CONF_K))
        grads.append(total(gs_ref, 0, SHORT_K))
        for idx, g in enumerate(grads):
            w_ref, m_ref, v_ref = ins[3 * idx:3 * idx + 3]
            delta, m_new, v_new = _adamw_math(w_ref[...], g, m_ref[...], v_ref[...])
            g_out, d_out, m_out, v_out = outs[4 * idx:4 * idx + 4]
            g_out[...] = g
            d_out[...] = delta
            m_out[...] = m_new
            v_out[...] = v_new

    params = list(vec_params) + [meta_p, conf_p, short_p]
    flat = [a for p in params for a in p]
    whole = lambda a: pl.BlockSpec(a.shape, lambda i, me: (0,) * a.ndim)
    outs = pl.pallas_call(
        body, name="small_update",
        grid_spec=pltpu.PrefetchScalarGridSpec(
            num_scalar_prefetch=1, grid=(1,),
            in_specs=[pl.BlockSpec((N_DEV, VEC_ROWS, SMALL_W), lambda i, me: (0, 0, 0)),
                      pl.BlockSpec((N_DEV, N_META, mcols),
                                   lambda i, me: (0, META_ROW0 // N_META + me[0] // per_row, me[0] % per_row)),
                      pl.BlockSpec((N_DEV, 32, LANE), lambda i, me: (0, CONF_ROW0 // 32, me[0])),
                      pl.BlockSpec((N_DEV, SUB, LANE), lambda i, me: (0, SHORT_ROW0 // SUB, me[0]))]
                     + [whole(a) for a in flat],
            out_specs=[whole(p[0]) for p in params for _ in range(4)]),
        out_shape=[jax.ShapeDtypeStruct(p[0].shape, F32) for p in params for _ in range(4)],
        compiler_params=_params(1),
    )(me_arr, gathered, gathered, gathered, gathered, *flat)
    return [tuple(outs[4 * i:4 * i + 4]) for i in range(len(params))]


def kernel(x, meta, g_pre_mix, w_in, b_gates, conf_dw_w, conf_dw_b, conf_ln_g, conf_ln_b, conf_w_pw, short_dw_w, short_w_out, w_o, g_post_mix, g_pre_mlp, w_up, w_down, g_post_mlp, loss_target, m_meta, m_g_pre_mix, m_w_in, m_b_gates, m_conf_dw_w, m_conf_dw_b, m_conf_ln_g, m_conf_ln_b, m_conf_w_pw, m_short_dw_w, m_short_w_out, m_w_o, m_g_post_mix, m_g_pre_mlp, m_w_up, m_w_down, m_g_post_mlp, v_meta, v_g_pre_mix, v_w_in, v_b_gates, v_conf_dw_w, v_conf_dw_b, v_conf_ln_g, v_conf_ln_b, v_conf_w_pw, v_short_dw_w, v_short_w_out, v_w_o, v_g_post_mix, v_g_pre_mlp, v_w_up, v_w_down, v_g_post_mlp):
    seq, d = x.shape[1], x.shape[2]
    dc = conf_w_pw.shape[1]
    t_real = N_META + seq
    t = -(-t_real // ROW_TILE) * ROW_TILE
    tm = t // 2
    assert tm % 16 == 0 and d % 1024 == 0 and dc % 1024 == 0
    x_idx, y_idx, c_idx = _position()
    me_arr = jnp.reshape(4 * x_idx + 2 * y_idx + c_idx, (1,)).astype(jnp.int32)

    big = [w_in[0], conf_w_pw[0], short_w_out[0], w_o[0], w_up[0], w_down[0]]
    big_names = ["w_in", "conf_w_pw", "short_w_out", "w_o", "w_up", "w_down"]
    slots = [_cast_into_slot("cast_" + nm, w, me_arr) for nm, w in zip(big_names, big)]
    groups = [[0], [1, 2, 3], [4], [5]]
    ici = [_remote_start("gather%d_ici_start" % g, "gather_ici", [slots[i] for i in idxs])
           for g, idxs in enumerate(groups)]

    def forward_on(g, after):
        send, recv, bufs, _ = ici[g]
        bufs = _remote_wait("gather%d_ici_wait" % g, "gather_ici", send, recv, bufs, len(bufs), after)
        send, recv, bufs, tok = _remote_start("gather%d_d2d_start" % g, "gather_d2d", bufs)
        return (send, recv, bufs), tok

    def gathered(g, state, after):
        send, recv, bufs = state
        return _remote_wait("gather%d_d2d_wait" % g, "gather_d2d", send, recv, bufs, len(bufs), after)

    meta_g, cw_g, sw_g = _all_gather("gather_small_params", [meta, conf_dw_w[0], short_dw_w[0]],
                                     deps=[st[3] for st in ici])
    unshard = lambda g: jnp.transpose(g, (1, 0, 2)).reshape(g.shape[1], -1)
    meta_full, cw_full, sw_full = unshard(meta_g), unshard(cw_g), unshard(sw_g)

    zrows = jnp.zeros((t - t_real, d), F32)
    h0 = jnp.concatenate([meta_full, x[0], zrows], axis=0)
    tgt = jnp.concatenate([jnp.zeros((N_META, d), F32), loss_target[0], zrows], axis=0)
    n = _pre_norm(h0, g_pre_mix)
    fwd0, tok = forward_on(0, [n])
    win_g, = gathered(0, fwd0, [tok])
    proj = _mm_cols("proj", n, win_g, tm=tm)[0]
    fwd1, tok = forward_on(1, [proj])
    a1, s = _conv_forward(proj, cw_full, conf_dw_b, sw_full, dc, deps=[tok])
    a3 = _layer_norm_silu(a1, conf_ln_g, conf_ln_b)
    wpw_g, wso_g, wo_g = gathered(1, fwd1, [a3])
    wo_full = wo_g.reshape(d, d)
    ya = _mm_cols("y_a", a3, wpw_g, tm=tm, nb=N_DEV)[0]
    yb = _mm_cols("y_b", s, wso_g, tm=tm, nb=N_DEV)[0]
    fwd2, tok = forward_on(2, [yb])
    m_mix = _gate_merge(proj, ya, yb, b_gates, d, deps=[tok])
    mix = _mm_rows("mix", m_mix, wo_full, tm=tm // 2, tk=d)
    wup_g, = gathered(2, fwd2, [mix])
    fwd3, tok = forward_on(3, [mix])
    h1, n2 = _post_mix(mix, h0, g_post_mix, g_pre_mlp, deps=[tok])

    def up_epilogue(acc):
        r = jnp.maximum(acc, 0.0)
        return r * r, r

    f, relu_up = _mm_cols("mlp_up", n2, wup_g, tm=tm, epilogue=up_epilogue, out_dtypes=(BF16, BF16))
    wdn_g, = gathered(3, fwd3, [f])
    wdn_full = wdn_g.reshape(-1, d)
    fo = _mm_rows("mlp_down", f, wdn_full, tm=tm, tk=1024)
    dfo, dh2, dg_post_mlp, loss_blk = _loss_head(fo, h1, tgt, g_post_mlp, t_real)
    loss = lax.psum(loss_blk[0, 0], ("x", "y", "c"))

    def reduce_start(tag, fulls, deps):
        lands = [lax.empty((4,) + g.shape[1:], BF16) for g in fulls]
        send, recv, bufs, tok = _remote_start("reduce_%s_d2d_start" % tag, "reduce_d2d", fulls, lands, deps=deps)
        return (send, recv, bufs), tok

    def reduce_middle(tag, state, owns, after):
        send, recv, bufs = state
        k = len(owns)
        bufs = _remote_wait("reduce_%s_d2d_wait" % tag, "reduce_d2d", send, recv, bufs, k, after)
        sums = [_chip_sum("chip_sum_%s%d" % (tag, i), bufs[i], bufs[k + i], owns[i], me_arr) for i in range(k)]
        lands = [lax.empty(sm[0].shape, BF16) for sm in sums]
        send, recv, bufs, tok = _remote_start("reduce_%s_ici_start" % tag, "reduce_ici", [sm[0] for sm in sums], lands)
        return (send, recv, bufs, [sm[1] for sm in sums]), tok

    def reduce_finish(tag, state, after):
        send, recv, bufs, chip_sums = state
        k = len(chip_sums)
        bufs = _remote_wait("reduce_%s_ici_wait" % tag, "reduce_ici", send, recv, bufs, k, after)
        return list(zip(chip_sums, bufs[k:]))

    dup = _mm_nt_blocks("d_up", dfo, wdn_full, tm=tm, tkb=1024, extra=(relu_up,),
                        epilogue=lambda acc, r: (acc * (2.0 * r.astype(F32)),), out_dtypes=(BF16,))[0]
    gw_down, gw_down_own = _mm_tn("dw_down", f, dfo, me_arr, m=f.shape[1], n=d, tma=512, tn=1024, sharded="rows")
    red_down, tok = reduce_start("down", [gw_down], ())
    dn2 = _mm_nt_acc("d_n2", dup, wup_g, tm=tm // 2, deps=[tok])
    gw_up, gw_up_own = _mm_tn("dw_up", n2, dup, me_arr, m=d, n=dup.shape[1], tma=512, tn=1024, sharded="cols")
    red_down, tok = reduce_middle("down", red_down, [gw_down_own], [dn2])
    red_up, tok = reduce_start("up", [gw_up], [tok])
    dh1, dmix, dg_pre_mlp, dg_post_mix = _mid_norm_bwd(dn2, h1, dh2, mix, g_pre_mlp, g_post_mix, deps=[tok])
    dm = _mm_nt_blocks("d_m", dmix, wo_full, tm=tm, tkb=1024)[0]
    red_up, tok = reduce_middle("up", red_up, [gw_up_own], [dm])
    gw_o, gw_o_own = _mm_tn("dw_o", m_mix, dmix, me_arr, m=d, n=d, tma=d // N_DEV, tn=1024, sharded="rows", deps=[tok])
    dyab, dproj, db_gates = _gate_merge_bwd(dm, proj, ya, yb, b_gates, d)
    ycb = d // N_DEV
    da3 = _mm_nt_acc("d_a3", dyab, wpw_g, tm=tm, nb=N_DEV, col_off=0)
    gw_pw, gw_pw_own = _mm_tn("dw_pw", a3, dyab, me_arr, m=dc, n=d, tma=512, tn=ycb, sharded="cols")
    dsb = _mm_nt_acc("d_s", dyab, wso_g, tm=tm, nb=N_DEV, col_off=1)
    gw_so, gw_so_own = _mm_tn("dw_so", s, dyab, me_arr, m=dc, n=d, tma=512, tn=ycb, sharded="cols", b_off=d // ycb)
    red_mix, tok = reduce_start("mix", [gw_pw, gw_so, gw_o], ())
    da1, dln_g, dln_b = _layer_norm_silu_bwd(da3, a1, conf_ln_g, conf_ln_b, deps=[tok])
    dproj, dcw, dcb, dsw = _conv_backward(dproj, proj, da1, dsb, cw_full, sw_full, dc)
    red_mix, tok = reduce_middle("mix", red_mix, [gw_pw_own, gw_so_own, gw_o_own], [dcb])
    in_cb = w_in.shape[2]
    half = d // 2
    red_in = []
    for part in range(2):
        gw, own = _mm_tn("dw_in%d" % part, n, dproj, me_arr, m=half, n=proj.shape[1], tma=512, tn=in_cb,
                         sharded="cols", a_off=part * (half // 512), deps=[tok])
        state, tok = reduce_start("in%d" % part, [gw], ())
        red_in.append((state, own))
    for part in range(2):
        state, own = red_in[part]
        red_in[part], tok = reduce_middle("in%d" % part, state, [own], [tok])
    dn = _mm_nt_acc("d_n", dproj, win_g, tm=tm // 2, deps=[tok])
    dh0, dg_pre_mix = _pre_norm_bwd(dn, h0, dh1, g_pre_mix)
    grad_x = dh0[N_META:t_real][None]

    vec_parts = [dg_pre_mix, db_gates, dcb, dln_g, dln_b, dg_post_mix, dg_pre_mlp, dg_post_mlp]
    packed = _pack_small(vec_parts, dh0[:N_META], dcw, dsw)
    small_g = _all_gather("gather_small_grads", [packed])[0]
    vec_names = ["g_pre_mix", "b_gates", "conf_dw_b", "conf_ln_g", "conf_ln_b", "g_post_mix", "g_pre_mlp", "g_post_mlp"]
    env = locals()
    triple = lambda nm, sq: tuple(env[p + nm][0] if sq else env[p + nm] for p in ("", "m_", "v_"))
    small = _small_update(small_g, me_arr, [triple(nm, False) for nm in vec_names],
                          triple("meta", False), triple("conf_dw_w", True), triple("short_dw_w", True))
    results = {}
    for nm, res in zip(vec_names + ["meta"], small[:len(vec_names) + 1]):
        results[nm] = res
    results["conf_dw_w"] = tuple(r[None] for r in small[-2])
    results["short_dw_w"] = tuple(r[None] for r in small[-1])

    def update(nm, parts):
        res = _adamw_shard("adamw_" + nm, env[nm][0], env["m_" + nm][0], env["v_" + nm][0], parts, me_arr)
        results[nm] = tuple(r[None] for r in res)
        return res[0]

    done = [small[0][0]]
    done.append(update("w_down", reduce_finish("down", red_down, [small_g])))
    done.append(update("w_up", reduce_finish("up", red_up, [small_g])))
    for nm, pair in zip(["conf_w_pw", "short_w_out", "w_o"], reduce_finish("mix", red_mix, [small_g])):
        done.append(update(nm, [pair]))
    update("w_in", [reduce_finish("in%d" % part, red_in[part], done)[0] for part in range(2)])

    order = ["meta", "g_pre_mix", "w_in", "b_gates", "conf_dw_w", "conf_dw_b", "conf_ln_g", "conf_ln_b", "conf_w_pw",
             "short_dw_w", "short_w_out", "w_o", "g_post_mix", "g_pre_mlp", "w_up", "w_down", "g_post_mlp"]
    return (loss, grad_x, *[results[nm][0] for nm in order], *[results[nm][1] for nm in order],
            *[results[nm][2] for nm in order], *[results[nm][3] for nm in order])
```

```python
import jax
import jax.numpy as jnp
from jax import lax
from jax.experimental import pallas as pl
from jax.experimental.pallas import tpu as pltpu

N_DEV = 8
N_META = 16
CONF_K = 31
SHORT_K = 3
RMS_EPS = 1e-6
LN_EPS = 1e-5
ADAM_LR = 0.001
ADAM_B1 = 0.9
ADAM_B2 = 0.999
ADAM_EPS = 1e-08
ADAM_WD = 0.01
ADAM_STEP = 10

LANE = 128
SUB = 8
ROW_TILE = 128
CONV_PAD = 32
CONV_CHUNK = 128
VMEM_LIMIT = 56 * 1024 * 1024

F32 = jnp.float32
BF16 = jnp.bfloat16
MESH = pl.DeviceIdType.MESH
ANY = pl.BlockSpec(memory_space=pl.ANY)
HBM_SPEC = pl.BlockSpec(memory_space=pltpu.HBM)
SEM_SPEC = pl.BlockSpec(memory_space=pltpu.SEMAPHORE)
EFFECT = pltpu.SideEffectType.DATAFLOW_SIDE_EFFECTING


def _params(n_axes):
    return pltpu.CompilerParams(dimension_semantics=("arbitrary",) * n_axes, vmem_limit_bytes=VMEM_LIMIT)


def _sigmoid(z):
    return 1.0 / (1.0 + jnp.exp(-z))


def _colsum8(v):
    r, c = v.shape
    return jnp.sum(v.reshape(r // SUB, SUB, c), axis=0)


def _position():
    x, y, c = lax.axis_index("x"), lax.axis_index("y"), lax.axis_index("c")
    return x, y, c


def _flat(p):
    return 4 * p[0] + 2 * p[1] + p[2]


def _all_gather(name, shards, deps=()):
    n, nd = len(shards), len(deps)

    def body(*refs):
        ins, outs = refs[:n], refs[n + nd:2 * n + nd]
        send_sems, recv_sems, local_sems = refs[2 * n + nd:]
        x, y, c = _position()
        me, sibling = (x, y, c), (x, y, 1 - c)
        chips = [(1 - x, y), (x, 1 - y), (1 - x, 1 - y)]

        def copy(q, k, block, to, src=None):
            dst = outs[q].at[_flat(block)]
            return pltpu.make_async_remote_copy(
                src_ref=dst if src is None else src, dst_ref=dst,
                send_sem=send_sems.at[q, k], recv_sem=recv_sems.at[q, k],
                device_id=to, device_id_type=MESH)

        mine = [pltpu.make_async_copy(ins[q], outs[q].at[_flat(me)], local_sems.at[q]) for q in range(n)]
        for cp in mine:
            cp.start()
        first = []
        for q in range(n):
            first.append(copy(q, 0, me, sibling, src=ins[q]))
            for j, chip in enumerate(chips):
                first.append(copy(q, 1 + j, me, (*chip, c), src=ins[q]))
        for cp in first:
            cp.start()
        passed = []
        for q in range(n):
            for j, chip in enumerate(chips):
                copy(q, 1 + j, (*chip, c), me).wait_recv()
                fwd = copy(q, 4 + j, (*chip, c), sibling)
                fwd.start()
                passed.append(fwd)
        for q in range(n):
            copy(q, 0, sibling, me).wait_recv()
            for j, chip in enumerate(chips):
                copy(q, 4 + j, (*chip, 1 - c), me).wait_recv()
        for cp in first + passed:
            cp.wait_send()
        for cp in mine:
            cp.wait()

    return pl.pallas_call(
        body, name=name,
        in_specs=[ANY] * (n + nd), out_specs=[ANY] * n,
        out_shape=[jax.ShapeDtypeStruct((N_DEV,) + s.shape, s.dtype) for s in shards],
        scratch_shapes=[pltpu.SemaphoreType.DMA((n, 7)), pltpu.SemaphoreType.DMA((n, 7)),
                        pltpu.SemaphoreType.DMA((n,))],
    )(*shards, *deps)


N_COPIES = {"gather_ici": 4, "gather_d2d": 3, "reduce_d2d": 4, "reduce_ici": 3}


def _copy_plan(kind):
    x, y, c = _position()
    me, sibling = (x, y, c), (x, y, 1 - c)
    chips = [(1 - x, y), (x, 1 - y), (1 - x, 1 - y)]
    if kind == "gather_ici":
        return [(_flat(me), _flat(me), sibling)] + [(_flat(me), _flat(me), (*ch, c)) for ch in chips]
    if kind == "gather_d2d":
        return [(_flat((*ch, c)), _flat((*ch, c)), sibling) for ch in chips]
    if kind == "reduce_d2d":
        return [(2 * chip + (1 - c), chip, sibling) for chip in range(4)]
    return [(2 * ch[0] + ch[1], 2 * x + y, (*ch, c)) for ch in chips]


def _planned_copies(kind, srcs, dsts, send_sems, recv_sems):
    plan = _copy_plan(kind)
    return [pltpu.make_async_remote_copy(
        src_ref=src.at[s_slot], dst_ref=dst.at[d_slot],
        send_sem=send_sems.at[q * len(plan) + k], recv_sem=recv_sems.at[q * len(plan) + k],
        device_id=to, device_id_type=MESH)
        for q, (src, dst) in enumerate(zip(srcs, dsts)) for k, (s_slot, d_slot, to) in enumerate(plan)]


def _remote_start(name, kind, srcs, lands=None, deps=()):
    n = len(srcs)
    bufs = list(srcs) + ([] if lands is None else list(lands))
    nb, nd = len(bufs), len(deps)
    nsem = n * N_COPIES[kind]

    def body(*refs):
        ins = refs[:nb]
        send_sems, recv_sems = refs[nb + nd], refs[nb + nd + 1]
        token = refs[-1]
        for cp in _planned_copies(kind, ins[:n], ins[:n] if lands is None else ins[n:], send_sems, recv_sems):
            cp.start()
        token[...] = jnp.zeros_like(token)

    outs = pl.pallas_call(
        body, name=name,
        out_shape=(pltpu.SemaphoreType.DMA((nsem,)), pltpu.SemaphoreType.DMA((nsem,)),
                   *[pltpu.HBM(b.shape, b.dtype) for b in bufs], jax.ShapeDtypeStruct((SUB, LANE), F32)),
        in_specs=[HBM_SPEC] * nb + [ANY] * nd,
        out_specs=(SEM_SPEC, SEM_SPEC, *[HBM_SPEC] * nb, pl.BlockSpec(memory_space=pltpu.VMEM)),
        input_output_aliases={i: 2 + i for i in range(nb)},
        compiler_params=pltpu.CompilerParams(has_side_effects=EFFECT),
    )(*[pltpu.with_memory_space_constraint(b, pltpu.HBM) for b in bufs], *deps)
    return outs[0], outs[1], list(outs[2:2 + nb]), outs[-1]


def _remote_wait(name, kind, send_sems, recv_sems, bufs, n, after):
    nb, na = len(bufs), len(after)
    same = nb == n

    def body(*refs):
        ins = refs[:nb]
        sends, recvs = refs[nb], refs[nb + 1]
        for cp in _planned_copies(kind, ins[:n], ins[:n] if same else ins[n:], sends, recvs):
            cp.wait_send()
            cp.wait_recv()

    outs = pl.pallas_call(
        body, name=name,
        out_shape=[pltpu.HBM(b.shape, b.dtype) for b in bufs],
        in_specs=[HBM_SPEC] * nb + [SEM_SPEC, SEM_SPEC] + [ANY] * na,
        out_specs=[HBM_SPEC] * nb,
        input_output_aliases={i: i for i in range(nb)},
        compiler_params=pltpu.CompilerParams(has_side_effects=EFFECT),
    )(*bufs, send_sems, recv_sems, *after)
    return list(outs)


def _mm_cols(name, a, w, *, tm, nb=1, epilogue=None, out_dtypes=(F32,)):
    t, k = a.shape
    nblk, _, cb = w.shape

    def body(a_ref, w_ref, *o_refs):
        av = a_ref[...]
        for b in range(nb):
            acc = jnp.dot(av, w_ref[b], preferred_element_type=F32)
            outs = (acc,) if epilogue is None else epilogue(acc)
            for o_ref, o in zip(o_refs, outs):
                o_ref[:, b * cb:(b + 1) * cb] = o.astype(o_ref.dtype)

    return pl.pallas_call(
        body, name=name, grid=(nblk // nb, t // tm),
        in_specs=[pl.BlockSpec((tm, k), lambda j, i: (i, 0)),
                  pl.BlockSpec((nb, k, cb), lambda j, i: (j, 0, 0))],
        out_specs=[pl.BlockSpec((tm, nb * cb), lambda j, i: (i, j)) for _ in out_dtypes],
        out_shape=[jax.ShapeDtypeStruct((t, nblk * cb), dt) for dt in out_dtypes],
        compiler_params=_params(2),
    )(a, w)


def _mm_rows(name, a, w2d, *, tm, tk):
    t = a.shape[0]
    kf, n = w2d.shape

    def body(a_ref, w_ref, o_ref):
        acc = jnp.dot(a_ref[...], w_ref[...], preferred_element_type=F32)

        @pl.when(pl.program_id(1) == 0)
        def _():
            o_ref[...] = acc

        @pl.when(pl.program_id(1) > 0)
        def _():
            o_ref[...] += acc

    return pl.pallas_call(
        body, name=name, grid=(t // tm, kf // tk),
        in_specs=[pl.BlockSpec((tm, tk), lambda i, kk: (i, kk)),
                  pl.BlockSpec((tk, n), lambda i, kk: (kk, 0))],
        out_specs=pl.BlockSpec((tm, n), lambda i, kk: (i, 0)),
        out_shape=jax.ShapeDtypeStruct((t, n), F32),
        compiler_params=_params(2),
    )(a, w2d)


def _mm_nt_acc(name, dy, w, *, tm, nb=1, col_off=0, deps=()):
    t = dy.shape[0]
    nblk, k, cb = w.shape

    def body(dy_ref, w_ref, *rest):
        o_ref = rest[-1]
        acc = None
        for b in range(nb):
            d = lax.dot_general(dy_ref[:, b * cb:(b + 1) * cb], w_ref[b], (((1,), (1,)), ((), ())),
                                preferred_element_type=F32)
            acc = d if acc is None else acc + d

        @pl.when(pl.program_id(1) == 0)
        def _():
            o_ref[...] = acc

        @pl.when(pl.program_id(1) > 0)
        def _():
            o_ref[...] += acc

    return pl.pallas_call(
        body, name=name, grid=(t // tm, nblk // nb),
        in_specs=[pl.BlockSpec((tm, nb * cb), lambda i, j: (i, col_off + j)),
                  pl.BlockSpec((nb, k, cb), lambda i, j: (j, 0, 0))] + [ANY] * len(deps),
        out_specs=pl.BlockSpec((tm, k), lambda i, j: (i, 0)),
        out_shape=jax.ShapeDtypeStruct((t, k), F32),
        compiler_params=_params(2),
    )(dy, w, *deps)


def _mm_nt_blocks(name, dy, w2d, *, tm, tkb, extra=(), epilogue=None, out_dtypes=(F32,)):
    t, n = dy.shape
    kf = w2d.shape[0]
    ne = len(extra)

    def body(dy_ref, w_ref, *rest):
        acc = lax.dot_general(dy_ref[...], w_ref[...], (((1,), (1,)), ((), ())), preferred_element_type=F32)
        outs = (acc,) if epilogue is None else epilogue(acc, *[e[...] for e in rest[:ne]])
        for o_ref, o in zip(rest[ne:], outs):
            o_ref[...] = o.astype(o_ref.dtype)

    return pl.pallas_call(
        body, name=name, grid=(kf // tkb, t // tm),
        in_specs=[pl.BlockSpec((tm, n), lambda kb, i: (i, 0)),
                  pl.BlockSpec((tkb, n), lambda kb, i: (kb, 0))]
                 + [pl.BlockSpec((tm, tkb), lambda kb, i: (i, kb)) for _ in extra],
        out_specs=[pl.BlockSpec((tm, tkb), lambda kb, i: (i, kb)) for _ in out_dtypes],
        out_shape=[jax.ShapeDtypeStruct((t, kf), dt) for dt in out_dtypes],
        compiler_params=_params(2),
    )(dy, w2d, *extra)


def _mm_tn(name, a, b, me_arr, *, m, n, tma, tn, sharded, a_off=0, b_off=0, deps=()):
    t = a.shape[0]
    if sharded == "cols":
        cb = n // N_DEV
        q = cb // tn
        full_shape, own_shape = (N_DEV, m, cb), (m, cb)
        full_spec = pl.BlockSpec((1, tma, tn), lambda i, j, me: (j // q, i, j % q))
    else:
        kb = m // N_DEV
        p = kb // tma
        full_shape, own_shape = (m, n), (kb, n)
        full_spec = pl.BlockSpec((tma, tn), lambda i, j, me: (i, j))

    def body(me_ref, a_ref, b_ref, *rest):
        full_ref, own_ref, stage, sem = rest[len(deps):]
        i, j = pl.program_id(0), pl.program_id(1)
        acc = lax.dot_general(a_ref[...], b_ref[...], (((0,), (0,)), ((), ())), preferred_element_type=F32)
        if sharded == "cols":
            full_ref[0] = acc.astype(BF16)
            owner, r0, c0 = j // q, i * tma, (j % q) * tn
        else:
            full_ref[...] = acc.astype(BF16)
            owner, r0, c0 = i // p, (i % p) * tma, j * tn

        @pl.when(owner == me_ref[0])
        def _():
            stage[...] = acc
            cp = pltpu.make_async_copy(
                stage, own_ref.at[pl.ds(pl.multiple_of(r0, tma), tma), pl.ds(pl.multiple_of(c0, tn), tn)], sem)
            cp.start()
            cp.wait()

    full, own = pl.pallas_call(
        body, name=name,
        grid_spec=pltpu.PrefetchScalarGridSpec(
            num_scalar_prefetch=1, grid=(m // tma, n // tn),
            in_specs=[pl.BlockSpec((t, tma), lambda i, j, me: (0, a_off + i)),
                      pl.BlockSpec((t, tn), lambda i, j, me: (0, b_off + j))] + [ANY] * len(deps),
            out_specs=[full_spec, ANY],
            scratch_shapes=[pltpu.VMEM((tma, tn), F32), pltpu.SemaphoreType.DMA(())]),
        out_shape=[jax.ShapeDtypeStruct(full_shape, BF16), jax.ShapeDtypeStruct(own_shape, F32)],
        compiler_params=_params(2),
    )(me_arr, a, b, *deps)
    if sharded == "rows":
        full = full.reshape(N_DEV, m // N_DEV, n)
    return full, own


def _row_call(name, body, t, row_ins, full_ins, row_outs, acc_outs, scratch=(), deps=()):
    tm = ROW_TILE
    nin = len(row_ins) + len(full_ins)

    def without_deps(*refs):
        body(*refs[:nin], *refs[nin + len(deps):])

    return pl.pallas_call(
        without_deps, name=name, grid=(t // tm,),
        in_specs=[pl.BlockSpec((tm, a.shape[1]), lambda i: (i, 0)) for a in row_ins]
                 + [pl.BlockSpec(a.shape, lambda i: (0, 0)) for a in full_ins] + [ANY] * len(deps),
        out_specs=[pl.BlockSpec((tm, c), lambda i: (i, 0)) for c, _ in row_outs]
                  + [pl.BlockSpec((r, c), lambda i: (0, 0)) for r, c in acc_outs],
        out_shape=[jax.ShapeDtypeStruct((t, c), dt) for c, dt in row_outs]
                  + [jax.ShapeDtypeStruct((r, c), F32) for r, c in acc_outs],
        scratch_shapes=list(scratch),
        compiler_params=_params(1),
    )(*row_ins, *full_ins, *deps)


def _accumulate(ref, v):
    @pl.when(pl.program_id(0) == 0)
    def _():
        ref[...] = v

    @pl.when(pl.program_id(0) > 0)
    def _():
        ref[...] += v


def _rms(v):
    return lax.rsqrt(jnp.mean(v * v, axis=-1, keepdims=True) + RMS_EPS)


def _rms_bwd(dout, u, r, g):
    du = dout * g
    dx = r * (du - u * jnp.mean(du * u, axis=-1, keepdims=True))
    return dx, _colsum8(dout * u)


def _pre_norm(h0, g):
    t, d = h0.shape

    def body(h_ref, g_ref, n_ref):
        h = h_ref[...]
        n_ref[...] = (h * _rms(h) * g_ref[...]).astype(BF16)

    return _row_call("pre_norm", body, t, [h0], [g], [(d, BF16)], [])[0]


def _post_mix(mix, h0, g_post, g_pre, deps=()):
    t, d = h0.shape

    def body(mix_ref, h0_ref, gp_ref, gq_ref, h1_ref, n2_ref):
        mix_v = mix_ref[...]
        h1 = h0_ref[...] + mix_v * _rms(mix_v) * gp_ref[...]
        h1_ref[...] = h1
        n2_ref[...] = (h1 * _rms(h1) * gq_ref[...]).astype(BF16)

    return _row_call("post_mix", body, t, [mix, h0], [g_post, g_pre], [(d, F32), (d, BF16)], [], deps=deps)


def _loss_head(fo, h1, tgt, g_post_mlp, t_real):
    t, d = h1.shape

    def body(fo_ref, h1_ref, tgt_ref, g_ref, dfo_ref, dh2_ref, dg_ref, loss_ref, lacc):
        i = pl.program_id(0)
        fo_v = fo_ref[...]
        g = g_ref[...]
        r = _rms(fo_v)
        u = fo_v * r
        h2 = h1_ref[...] + u * g
        row = i * ROW_TILE + lax.broadcasted_iota(jnp.int32, (ROW_TILE, 1), 0)
        valid = jnp.logical_and(row >= N_META, row < t_real)
        diff = jnp.where(valid, h2 - tgt_ref[...], 0.0)
        dh2 = diff * (1.0 / d)
        dh2_ref[...] = dh2
        dfo, dg = _rms_bwd(dh2, u, r, g)
        dfo_ref[...] = dfo.astype(BF16)
        _accumulate(dg_ref, dg)
        _accumulate(lacc, _colsum8(diff * diff))

        @pl.when(i == pl.num_programs(0) - 1)
        def _():
            loss_ref[...] = jnp.full((SUB, LANE), (0.5 / d) * jnp.sum(lacc[...]), F32)

    return _row_call("loss_head", body, t, [fo, h1, tgt], [g_post_mlp],
                     [(d, BF16), (d, F32)], [(SUB, d), (SUB, LANE)], scratch=[pltpu.VMEM((SUB, d), F32)])


def _mid_norm_bwd(dn2, h1, dh2, mix, g_pre_mlp, g_post_mix, deps=()):
    t, d = h1.shape

    def body(dn2_ref, h1_ref, dh2_ref, mix_ref, gq_ref, gp_ref, dh1_ref, dmix_ref, dgq_ref, dgp_ref):
        h1 = h1_ref[...]
        r3 = _rms(h1)
        dx, dgq = _rms_bwd(dn2_ref[...], h1 * r3, r3, gq_ref[...])
        dh1 = dh2_ref[...] + dx
        dh1_ref[...] = dh1
        mix_v = mix_ref[...]
        r2 = _rms(mix_v)
        dmix, dgp = _rms_bwd(dh1, mix_v * r2, r2, gp_ref[...])
        dmix_ref[...] = dmix.astype(BF16)
        _accumulate(dgq_ref, dgq)
        _accumulate(dgp_ref, dgp)

    return _row_call("mid_norm_bwd", body, t, [dn2, h1, dh2, mix], [g_pre_mlp, g_post_mix],
                     [(d, F32), (d, BF16)], [(SUB, d), (SUB, d)], deps=deps)


def _pre_norm_bwd(dn, h0, dh1, g_pre_mix, deps=()):
    t, d = h0.shape

    def body(dn_ref, h0_ref, dh1_ref, g_ref, dh0_ref, dg_ref):
        h0 = h0_ref[...]
        r = _rms(h0)
        dx, dg = _rms_bwd(dn_ref[...], h0 * r, r, g_ref[...])
        dh0_ref[...] = dh1_ref[...] + dx
        _accumulate(dg_ref, dg)

    return _row_call("pre_norm_bwd", body, t, [dn, h0, dh1], [g_pre_mix], [(d, F32)], [(SUB, d)], deps=deps)


def _layer_norm_silu(a1, ln_g, ln_b):
    t, c = a1.shape

    def body(a1_ref, g_ref, b_ref, a3_ref):
        a = a1_ref[...]
        mu = jnp.mean(a, axis=-1, keepdims=True)
        xc = a - mu
        rstd = lax.rsqrt(jnp.mean(xc * xc, axis=-1, keepdims=True) + LN_EPS)
        z = xc * rstd * g_ref[...] + b_ref[...]
        a3_ref[...] = (z * _sigmoid(z)).astype(BF16)

    return _row_call("layer_norm_silu", body, t, [a1], [ln_g, ln_b], [(c, BF16)], [])[0]


def _layer_norm_silu_bwd(da3, a1, ln_g, ln_b, deps=()):
    t, c = a1.shape

    def body(da3_ref, a1_ref, g_ref, b_ref, da1_ref, dg_ref, db_ref):
        a = a1_ref[...]
        g = g_ref[...]
        mu = jnp.mean(a, axis=-1, keepdims=True)
        xc = a - mu
        rstd = lax.rsqrt(jnp.mean(xc * xc, axis=-1, keepdims=True) + LN_EPS)
        xhat = xc * rstd
        z = xhat * g + b_ref[...]
        sg = _sigmoid(z)
        dz = da3_ref[...] * (sg * (1.0 + z * (1.0 - sg)))
        dxhat = dz * g
        da1_ref[...] = rstd * (dxhat - jnp.mean(dxhat, axis=-1, keepdims=True)
                               - xhat * jnp.mean(dxhat * xhat, axis=-1, keepdims=True))
        _accumulate(dg_ref, _colsum8(dz * xhat))
        _accumulate(db_ref, _colsum8(dz))

    return _row_call("layer_norm_silu_bwd", body, t, [da3, a1], [ln_g, ln_b], [(c, F32)], [(SUB, c), (SUB, c)], deps=deps)


def _gate_merge(proj, ya, yb, b_gates, d, deps=()):
    t = proj.shape[0]
    w = 1024
    nh = d // w
    ga0 = (proj.shape[1] - 2 * d) // w

    def body(pa_ref, pb_ref, ya_ref, yb_ref, ba_ref, bb_ref, *rest):
        m_ref = rest[-1]
        ga = _sigmoid(pa_ref[...] + ba_ref[...])
        gb = _sigmoid(pb_ref[...] + bb_ref[...])
        m_ref[...] = (ga * ya_ref[...] + gb * yb_ref[...]).astype(BF16)

    tm = ROW_TILE
    return pl.pallas_call(
        body, name="gate_merge", grid=(nh, t // tm),
        in_specs=[pl.BlockSpec((tm, w), lambda h, i: (i, ga0 + h)),
                  pl.BlockSpec((tm, w), lambda h, i: (i, ga0 + nh + h)),
                  pl.BlockSpec((tm, w), lambda h, i: (i, h)),
                  pl.BlockSpec((tm, w), lambda h, i: (i, h)),
                  pl.BlockSpec((1, w), lambda h, i: (0, h)),
                  pl.BlockSpec((1, w), lambda h, i: (0, nh + h))] + [ANY] * len(deps),
        out_specs=pl.BlockSpec((tm, w), lambda h, i: (i, h)),
        out_shape=jax.ShapeDtypeStruct((t, d), BF16),
        compiler_params=_params(2),
    )(proj, proj, ya, yb, b_gates, b_gates, *deps)


def _gate_merge_bwd(dm, proj, ya, yb, b_gates, d):
    t, cols = proj.shape
    w = 1024
    nh = d // w
    ga0 = (cols - 2 * d) // w

    def body(dm_ref, p_ref, ya_ref, yb_ref, b_ref, dy_ref, dp_ref, db_ref):
        q = pl.program_id(0)
        g = _sigmoid(p_ref[...] + b_ref[...])
        dm_v = dm_ref[...]
        y = jnp.where(q < nh, ya_ref[...], yb_ref[...])
        dy_ref[...] = (dm_v * g).astype(BF16)
        dp = dm_v * y * g * (1.0 - g)
        dp_ref[...] = dp.astype(BF16)

        @pl.when(pl.program_id(1) == 0)
        def _():
            db_ref[...] = _colsum8(dp)

        @pl.when(pl.program_id(1) > 0)
        def _():
            db_ref[...] += _colsum8(dp)

    tm = ROW_TILE
    return pl.pallas_call(
        body, name="gate_merge_bwd", grid=(2 * nh, t // tm),
        in_specs=[pl.BlockSpec((tm, w), lambda q, i: (i, q % nh)),
                  pl.BlockSpec((tm, w), lambda q, i: (i, ga0 + q)),
                  pl.BlockSpec((tm, w), lambda q, i: (i, q % nh)),
                  pl.BlockSpec((tm, w), lambda q, i: (i, q % nh)),
                  pl.BlockSpec((1, w), lambda q, i: (0, q))],
        out_specs=[pl.BlockSpec((tm, w), lambda q, i: (i, q)),
                   pl.BlockSpec((tm, w), lambda q, i: (i, ga0 + q)),
                   pl.BlockSpec((SUB, w), lambda q, i: (0, q))],
        out_shape=[jax.ShapeDtypeStruct((t, 2 * d), BF16), jax.ShapeDtypeStruct((t, cols), BF16),
                   jax.ShapeDtypeStruct((SUB, 2 * d), F32)],
        compiler_params=_params(2),
    )(dm, proj, ya, yb, b_gates)


def _causal_conv(xp_ref, w_ref, ntap, r0):
    n = CONV_CHUNK + CONV_PAD
    win = xp_ref[pl.ds(r0, n), :]
    acc = None
    for k in range(ntap):
        back = ntap - 1 - k
        shifted = pltpu.roll(win, n - (CONV_PAD - back), 0)
        term = w_ref[k:k + 1, :] * shifted[:CONV_CHUNK]
        acc = term if acc is None else acc + term
    return acc


def _anticausal_conv(xp_ref, w_ref, ntap, r0):
    n = CONV_CHUNK + CONV_PAD
    win = xp_ref[pl.ds(pl.multiple_of(CONV_PAD + r0, CONV_PAD), n), :]
    acc = None
    for k in range(ntap):
        ahead = ntap - 1 - k
        shifted = win if ahead == 0 else pltpu.roll(win, n - ahead, 0)
        term = w_ref[k:k + 1, :] * shifted[:CONV_CHUNK]
        acc = term if acc is None else acc + term
    return acc


def _conv_weight_grad(dw_ref, d_chunk, xp_ref, ntap, r0):
    n = CONV_CHUNK + CONV_PAD
    win = xp_ref[pl.ds(r0, n), :]
    for k in range(ntap):
        back = ntap - 1 - k
        shifted = pltpu.roll(win, n - (CONV_PAD - back), 0)
        dw_ref[k * SUB:(k + 1) * SUB, :] += _colsum8(d_chunk * shifted[:CONV_CHUNK])


def _zero_pads(ref, t):
    ref[0:CONV_PAD, :] = jnp.zeros((CONV_PAD, LANE), F32)
    ref[CONV_PAD + t:CONV_PAD + t + CONV_PAD, :] = jnp.zeros((CONV_PAD, LANE), F32)


def _for_chunks(t, fn):
    def step(idx, carry):
        fn(pl.multiple_of(idx * CONV_CHUNK, CONV_CHUNK))
        return carry

    lax.fori_loop(0, t // CONV_CHUNK, step, 0)


def _conv_forward(proj, conf_w, conf_b, short_w, dc, deps=()):
    t = proj.shape[0]
    nc = dc // LANE

    def body(av_ref, ag_ref, bg_ref, cg_ref, v_ref, cw_ref, cb_ref, sw_ref, *rest):
        a1_ref, s_ref, xa, xb = rest[len(deps):]
        _zero_pads(xa, t)
        _zero_pads(xb, t)
        xa[CONV_PAD:CONV_PAD + t, :] = av_ref[...] * _sigmoid(ag_ref[...])
        xb[CONV_PAD:CONV_PAD + t, :] = cg_ref[...] * v_ref[...]

        def chunk(r0):
            rs = pl.ds(r0, CONV_CHUNK)
            a1_ref[rs, :] = _causal_conv(xa, cw_ref, CONF_K, r0) + cb_ref[...]
            s_ref[rs, :] = (bg_ref[rs, :] * _causal_conv(xb, sw_ref, SHORT_K, r0)).astype(BF16)

        _for_chunks(t, chunk)

    col = lambda g: pl.BlockSpec((t, LANE), lambda c, g=g: (0, g * nc + c))
    return pl.pallas_call(
        body, name="conv_forward", grid=(nc,),
        in_specs=[col(0), col(1), col(2), col(3), col(4),
                  pl.BlockSpec((CONF_K, LANE), lambda c: (0, c)),
                  pl.BlockSpec((1, LANE), lambda c: (0, c)),
                  pl.BlockSpec((SHORT_K, LANE), lambda c: (0, c))] + [ANY] * len(deps),
        out_specs=[pl.BlockSpec((t, LANE), lambda c: (0, c)), pl.BlockSpec((t, LANE), lambda c: (0, c))],
        out_shape=[jax.ShapeDtypeStruct((t, dc), F32), jax.ShapeDtypeStruct((t, dc), BF16)],
        scratch_shapes=[pltpu.VMEM((t + 2 * CONV_PAD, LANE), F32), pltpu.VMEM((t + 2 * CONV_PAD, LANE), F32)],
        compiler_params=_params(1),
    )(proj, proj, proj, proj, proj, conf_w, conf_b, short_w, *deps)


def _conv_backward(dproj, proj, da1, ds, conf_w, short_w, dc):
    t = proj.shape[0]
    nc = dc // LANE

    def body(dp_in, av_ref, ag_ref, bg_ref, cg_ref, v_ref, da1_ref, ds_ref, cw_ref, sw_ref,
             dp_ref, dcw_ref, dcb_ref, dsw_ref, xa, xb, da, db, stage, sems):
        del dp_in
        c = pl.program_id(0)
        for ref in (xa, xb, da, db):
            _zero_pads(ref, t)
        xa[CONV_PAD:CONV_PAD + t, :] = av_ref[...] * _sigmoid(ag_ref[...])
        xb[CONV_PAD:CONV_PAD + t, :] = cg_ref[...] * v_ref[...]
        da[CONV_PAD:CONV_PAD + t, :] = da1_ref[...]
        dcw_ref[...] = jnp.zeros(dcw_ref.shape, F32)
        dsw_ref[...] = jnp.zeros(dsw_ref.shape, F32)
        dcb_ref[...] = jnp.zeros(dcb_ref.shape, F32)

        def through_gate(r0):
            rs = pl.ds(r0, CONV_CHUNK)
            ds_c = ds_ref[rs, :]
            stage[2, rs, :] = (ds_c * _causal_conv(xb, sw_ref, SHORT_K, r0)).astype(BF16)
            db[pl.ds(pl.multiple_of(CONV_PAD + r0, CONV_PAD), CONV_CHUNK), :] = ds_c * bg_ref[rs, :]

        _for_chunks(t, through_gate)

        def through_convs(r0):
            rs = pl.ds(r0, CONV_CHUNK)
            da0 = _anticausal_conv(da, cw_ref, CONF_K, r0)
            sg = _sigmoid(ag_ref[rs, :])
            stage[0, rs, :] = (da0 * sg).astype(BF16)
            stage[1, rs, :] = (da0 * av_ref[rs, :] * sg * (1.0 - sg)).astype(BF16)
            dcv = _anticausal_conv(db, sw_ref, SHORT_K, r0)
            stage[3, rs, :] = (dcv * v_ref[rs, :]).astype(BF16)
            stage[4, rs, :] = (dcv * cg_ref[rs, :]).astype(BF16)
            da1_c = da1_ref[rs, :]
            _conv_weight_grad(dcw_ref, da1_c, xa, CONF_K, r0)
            _conv_weight_grad(dsw_ref, ds_ref[rs, :] * bg_ref[rs, :], xb, SHORT_K, r0)
            dcb_ref[...] += _colsum8(da1_c)

        _for_chunks(t, through_convs)
        copies = [pltpu.make_async_copy(
            stage.at[g], dp_ref.at[:, pl.ds(pl.multiple_of((g * nc + c) * LANE, LANE), LANE)], sems.at[g])
            for g in range(5)]
        for cp in copies:
            cp.start()
        for cp in copies:
            cp.wait()

    col = lambda g: pl.BlockSpec((t, LANE), lambda c, g=g: (0, g * nc + c))
    blk = pl.BlockSpec((t, LANE), lambda c: (0, c))
    return pl.pallas_call(
        body, name="conv_backward", grid=(nc,),
        in_specs=[ANY, col(0), col(1), col(2), col(3), col(4), blk, blk,
                  pl.BlockSpec((CONF_K, LANE), lambda c: (0, c)),
                  pl.BlockSpec((SHORT_K, LANE), lambda c: (0, c))],
        out_specs=[ANY,
                   pl.BlockSpec((CONF_K * SUB, LANE), lambda c: (0, c)),
                   pl.BlockSpec((SUB, LANE), lambda c: (0, c)),
                   pl.BlockSpec((SHORT_K * SUB, LANE), lambda c: (0, c))],
        out_shape=[jax.ShapeDtypeStruct(dproj.shape, dproj.dtype),
                   jax.ShapeDtypeStruct((CONF_K * SUB, dc), F32),
                   jax.ShapeDtypeStruct((SUB, dc), F32),
                   jax.ShapeDtypeStruct((SHORT_K * SUB, dc), F32)],
        scratch_shapes=[pltpu.VMEM((t + 2 * CONV_PAD, LANE), F32)] * 4
                       + [pltpu.VMEM((5, t, LANE), BF16), pltpu.SemaphoreType.DMA((5,))],
        input_output_aliases={0: 0},
        compiler_params=_params(1),
    )(dproj, proj, proj, proj, proj, proj, da1, ds, conf_w, short_w)


def _adamw_math(w, g, m, v):
    m = ADAM_B1 * m + (1.0 - ADAM_B1) * g
    v = ADAM_B2 * v + (1.0 - ADAM_B2) * (g * g)
    m_hat = m / (1.0 - ADAM_B1 ** ADAM_STEP)
    v_hat = v / (1.0 - ADAM_B2 ** ADAM_STEP)
    delta = -ADAM_LR * (m_hat / (jnp.sqrt(v_hat) + ADAM_EPS) + ADAM_WD * w)
    return delta, m, v


def _cast_into_slot(name, w, me_arr):
    r, c = w.shape
    tr = 256

    def body(me_ref, w_ref, o_ref):
        del me_ref
        o_ref[0] = w_ref[...].astype(BF16)

    return pl.pallas_call(
        body, name=name,
        grid_spec=pltpu.PrefetchScalarGridSpec(
            num_scalar_prefetch=1, grid=(r // tr,),
            in_specs=[pl.BlockSpec((tr, c), lambda i, me: (i, 0))],
            out_specs=pl.BlockSpec((1, tr, c), lambda i, me: (me[0], i, 0))),
        out_shape=jax.ShapeDtypeStruct((N_DEV, r, c), BF16),
        compiler_params=_params(1),
    )(me_arr, w)


def _chip_sum(name, full, from_sibling, own, me_arr):
    _, r, c = full.shape
    tr = min(r, 512)

    def body(me_ref, full_ref, sib_ref, own_ref, sums_ref, mine_ref):
        theirs = sib_ref[0].astype(F32)
        sums_ref[0] = (full_ref[0].astype(F32) + theirs).astype(BF16)

        @pl.when(pl.program_id(1) == me_ref[0] // 2)
        def _():
            mine_ref[...] = own_ref[...] + theirs

    return pl.pallas_call(
        body, name=name,
        grid_spec=pltpu.PrefetchScalarGridSpec(
            num_scalar_prefetch=1, grid=(r // tr, 4),
            in_specs=[pl.BlockSpec((1, tr, c), lambda i, chip, me: (2 * chip + me[0] % 2, i, 0)),
                      pl.BlockSpec((1, tr, c), lambda i, chip, me: (chip, i, 0)),
                      pl.BlockSpec((tr, c), lambda i, chip, me: (i, 0))],
            out_specs=[pl.BlockSpec((1, tr, c), lambda i, chip, me: (chip, i, 0)),
                       pl.BlockSpec((tr, c), lambda i, chip, me: (i, 0))]),
        out_shape=[jax.ShapeDtypeStruct((4, r, c), BF16), jax.ShapeDtypeStruct((r, c), F32)],
        compiler_params=_params(2),
    )(me_arr, full, from_sibling, own)


def _adamw_shard(name, w, m, v, parts, me_arr):
    r, c = w.shape
    tr = 128
    np_ = len(parts)
    per = r // np_ // tr

    def body(me_ref, w_ref, m_ref, v_ref, *rest):
        g_out, d_out, m_out, v_out = rest[4 * np_:]
        g = None
        for p in range(np_):
            gp = rest[4 * p][...]
            for l_ref in rest[4 * p + 1:4 * p + 4]:
                gp = gp + l_ref[0].astype(F32)
            g = gp if g is None else jnp.where(pl.program_id(0) // per == p, gp, g)
        delta, m_new, v_new = _adamw_math(w_ref[...], g, m_ref[...], v_ref[...])
        g_out[...] = g
        d_out[...] = delta
        m_out[...] = m_new
        v_out[...] = v_new

    tile = pl.BlockSpec((tr, c), lambda i, me: (i, 0))
    part_specs, part_args = [], []
    for p, (g_chip, landed) in enumerate(parts):
        row = lambda i, p=p: jnp.clip(i - p * per, 0, per - 1)
        part_specs.append(pl.BlockSpec((tr, c), lambda i, me, row=row: (row(i), 0)))
        part_specs += [pl.BlockSpec((1, tr, c), lambda i, me, k=k, row=row: ((me[0] // 2 + k) % 4, row(i), 0))
                       for k in range(1, 4)]
        part_args += [g_chip, landed, landed, landed]
    return pl.pallas_call(
        body, name=name,
        grid_spec=pltpu.PrefetchScalarGridSpec(
            num_scalar_prefetch=1, grid=(r // tr,),
            in_specs=[tile] * 3 + part_specs, out_specs=[tile] * 4),
        out_shape=[jax.ShapeDtypeStruct((r, c), F32)] * 4,
        compiler_params=_params(1),
    )(me_arr, w, m, v, *part_args)


SMALL_W = 1024
VEC_ROWS = 16
META_ROW0 = 16
CONF_ROW0 = 64
SHORT_ROW0 = 96
SMALL_ROWS = 104


def _pack_small(vec_parts, dmeta, dcw, dsw):
    widths = [p.shape[1] for p in vec_parts]
    nv = len(vec_parts)

    def body(*refs):
        parts, (dmeta_ref, dcw_ref, dsw_ref, out_ref) = refs[:nv], refs[nv:]
        out_ref[...] = jnp.zeros((SMALL_ROWS, SMALL_W), F32)
        row = 0
        for p_ref, wd in zip(parts, widths):
            s = jnp.sum(p_ref[...], axis=0, keepdims=True)
            for h in range(wd // SMALL_W):
                out_ref[row:row + 1, :] = s[:, h * SMALL_W:(h + 1) * SMALL_W]
                row += 1
        for h in range(dmeta_ref.shape[1] // SMALL_W):
            out_ref[META_ROW0 + h * N_META:META_ROW0 + (h + 1) * N_META, :] = dmeta_ref[:, h * SMALL_W:(h + 1) * SMALL_W]
        for k in range(CONF_K):
            out_ref[CONF_ROW0 + k:CONF_ROW0 + k + 1, :] = jnp.sum(dcw_ref[k * SUB:(k + 1) * SUB, :], axis=0, keepdims=True)
        for k in range(SHORT_K):
            out_ref[SHORT_ROW0 + k:SHORT_ROW0 + k + 1, :] = jnp.sum(dsw_ref[k * SUB:(k + 1) * SUB, :], axis=0, keepdims=True)

    return pl.pallas_call(
        body, name="pack_small",
        out_shape=jax.ShapeDtypeStruct((SMALL_ROWS, SMALL_W), F32),
        compiler_params=pltpu.CompilerParams(vmem_limit_bytes=VMEM_LIMIT),
    )(*vec_parts, dmeta, dcw, dsw)


def _small_update(gathered, me_arr, vec_params, meta_p, conf_p, short_p):
    widths = [p[0].shape[1] for p in vec_params]
    nv = len(vec_params)
    mcols = meta_p[0].shape[1]
    per_row = SMALL_W // mcols

    def body(me_ref, gv_ref, gm_ref, gc_ref, gs_ref, *rest):
        del me_ref
        ins, outs = rest[:3 * (nv + 3)], rest[3 * (nv + 3):]

        def total(ref, r0, rows):
            s = ref[0, r0:r0 + rows, :]
            for dev in range(1, N_DEV):
                s = s + ref[dev, r0:r0 + rows, :]
            return s

        grads = []
        row = 0
        for wd in widths:
            pieces = [total(gv_ref, row + h, 1) for h in range(wd // SMALL_W)]
            grads.append(pieces[0] if len(pieces) == 1 else jnp.concatenate(pieces, axis=1))
            row += len(pieces)
        grads.append(total(gm_ref, 0, N_META))
        grads.append(total(gc_ref, 0, CONF_K))
        grads.append(total(gs_ref, 0, SHORT_K))
        for idx, g in enumerate(grads):
            w_ref, m_ref, v_ref = ins[3 * idx:3 * idx + 3]
            delta, m_new, v_new = _adamw_math(w_ref[...], g, m_ref[...], v_ref[...])
            g_out, d_out, m_out, v_out = outs[4 * idx:4 * idx + 4]
            g_out[...] = g
            d_out[...] = delta
            m_out[...] = m_new
            v_out[...] = v_new

    params = list(vec_params) + [meta_p, conf_p, short_p]
    flat = [a for p in params for a in p]
    whole = lambda a: pl.BlockSpec(a.shape, lambda i, me: (0,) * a.ndim)
    outs = pl.pallas_call(
        body, name="small_update",
        grid_spec=pltpu.PrefetchScalarGridSpec(
            num_scalar_prefetch=1, grid=(1,),
            in_specs=[pl.BlockSpec((N_DEV, VEC_ROWS, SMALL_W), lambda i, me: (0, 0, 0)),
                      pl.BlockSpec((N_DEV, N_META, mcols),
                                   lambda i, me: (0, META_ROW0 // N_META + me[0] // per_row, me[0] % per_row)),
                      pl.BlockSpec((N_DEV, 32, LANE), lambda i, me: (0, CONF_ROW0 // 32, me[0])),
                      pl.BlockSpec((N_DEV, SUB, LANE), lambda i, me: (0, SHORT_ROW0 // SUB, me[0]))]
                     + [whole(a) for a in flat],
            out_specs=[whole(p[0]) for p in params for _ in range(4)]),
        out_shape=[jax.ShapeDtypeStruct(p[0].shape, F32) for p in params for _ in range(4)],
        compiler_params=_params(1),
    )(me_arr, gathered, gathered, gathered, gathered, *flat)
    return [tuple(outs[4 * i:4 * i + 4]) for i in range(len(params))]


def kernel(x, meta, g_pre_mix, w_in, b_gates, conf_dw_w, conf_dw_b, conf_ln_g, conf_ln_b, conf_w_pw, short_dw_w, short_w_out, w_o, g_post_mix, g_pre_mlp, w_up, w_down, g_post_mlp, loss_target, m_meta, m_g_pre_mix, m_w_in, m_b_gates, m_conf_dw_w, m_conf_dw_b, m_conf_ln_g, m_conf_ln_b, m_conf_w_pw, m_short_dw_w, m_short_w_out, m_w_o, m_g_post_mix, m_g_pre_mlp, m_w_up, m_w_down, m_g_post_mlp, v_meta, v_g_pre_mix, v_w_in, v_b_gates, v_conf_dw_w, v_conf_dw_b, v_conf_ln_g, v_conf_ln_b, v_conf_w_pw, v_short_dw_w, v_short_w_out, v_w_o, v_g_post_mix, v_g_pre_mlp, v_w_up, v_w_down, v_g_post_mlp):
    seq, d = x.shape[1], x.shape[2]
    dc = conf_w_pw.shape[1]
    t_real = N_META + seq
    t = -(-t_real // ROW_TILE) * ROW_TILE
    tm = t // 2
    assert tm % 16 == 0 and d % 1024 == 0 and dc % 1024 == 0
    x_idx, y_idx, c_idx = _position()
    me_arr = jnp.reshape(4 * x_idx + 2 * y_idx + c_idx, (1,)).astype(jnp.int32)

    big = [w_in[0], conf_w_pw[0], short_w_out[0], w_o[0], w_up[0], w_down[0]]
    big_names = ["w_in", "conf_w_pw", "short_w_out", "w_o", "w_up", "w_down"]
    slots = [_cast_into_slot("cast_" + nm, w, me_arr) for nm, w in zip(big_names, big)]
    groups = [[0], [1, 2, 3], [4], [5]]
    meta_g, cw_g, sw_g = _all_gather("gather_small_params", [meta, conf_dw_w[0], short_dw_w[0]])
    ici = []
    for g, idxs in enumerate(groups):
        ici.append(_remote_start("gather%d_ici_start" % g, "gather_ici", [slots[i] for i in idxs],
                                 deps=[meta_g] + [st[3] for st in ici[-1:]]))
    started = ici[-1][3][0, 0] * 0.0

    def forward_on(g, after):
        send, recv, bufs, _ = ici[g]
        bufs = _remote_wait("gather%d_ici_wait" % g, "gather_ici", send, recv, bufs, len(bufs), after)
        send, recv, bufs, tok = _remote_start("gather%d_d2d_start" % g, "gather_d2d", bufs)
        return (send, recv, bufs), tok

    def gathered(g, state, after):
        send, recv, bufs = state
        return _remote_wait("gather%d_d2d_wait" % g, "gather_d2d", send, recv, bufs, len(bufs), after)

    unshard =lambda g: jnp.transpose(g, (1, 0, 2)).reshape(g.shape[1], -1)
    meta_full, cw_full, sw_full = unshard(meta_g), unshard(cw_g), unshard(sw_g)

    zrows = jnp.zeros((t - t_real, d), F32) + started
    h0 = jnp.concatenate([meta_full, x[0], zrows], axis=0)
    tgt = jnp.concatenate([jnp.zeros((N_META, d), F32), loss_target[0], zrows], axis=0)
    n = _pre_norm(h0, g_pre_mix)
    fwd0, tok = forward_on(0, [n])
    win_g, = gathered(0, fwd0, [tok])
    proj = _mm_cols("proj", n, win_g, tm=tm)[0]
    fwd1, tok = forward_on(1, [proj])
    a1, s = _conv_forward(proj, cw_full, conf_dw_b, sw_full, dc, deps=[tok])
    a3 = _layer_norm_silu(a1, conf_ln_g, conf_ln_b)
    wpw_g, wso_g, wo_g = gathered(1, fwd1, [a3])
    wo_full = wo_g.reshape(d, d)
    ya = _mm_cols("y_a", a3, wpw_g, tm=tm, nb=N_DEV)[0]
    yb = _mm_cols("y_b", s, wso_g, tm=tm, nb=N_DEV)[0]
    fwd2, tok = forward_on(2, [yb])
    m_mix = _gate_merge(proj, ya, yb, b_gates, d, deps=[tok])
    mix = _mm_rows("mix", m_mix, wo_full, tm=tm // 2, tk=d)
    wup_g, = gathered(2, fwd2, [mix])
    fwd3, tok = forward_on(3, [mix])
    h1, n2 = _post_mix(mix, h0, g_post_mix, g_pre_mlp, deps=[tok])

    def up_epilogue(acc):
        r = jnp.maximum(acc, 0.0)
        return r * r, r

    f, relu_up = _mm_cols("mlp_up", n2, wup_g, tm=tm, epilogue=up_epilogue, out_dtypes=(BF16, BF16))
    wdn_g, = gathered(3, fwd3, [f])
    wdn_full = wdn_g.reshape(-1, d)
    fo = _mm_rows("mlp_down", f, wdn_full, tm=tm, tk=1024)
    dfo, dh2, dg_post_mlp, loss_blk = _loss_head(fo, h1, tgt, g_post_mlp, t_real)
    loss = lax.psum(loss_blk[0, 0], ("x", "y", "c"))

    def reduce_start(tag, fulls, deps):
        lands = [lax.empty((4,) + g.shape[1:], BF16) for g in fulls]
        send, recv, bufs, tok = _remote_start("reduce_%s_d2d_start" % tag, "reduce_d2d", fulls, lands, deps=deps)
        return (send, recv, bufs), tok

    def reduce_middle(tag, state, owns, after):
        send, recv, bufs = state
        k = len(owns)
        bufs = _remote_wait("reduce_%s_d2d_wait" % tag, "reduce_d2d", send, recv, bufs, k, after)
        sums = [_chip_sum("chip_sum_%s%d" % (tag, i), bufs[i], bufs[k + i], owns[i], me_arr) for i in range(k)]
        lands = [lax.empty(sm[0].shape, BF16) for sm in sums]
        send, recv, bufs, tok = _remote_start("reduce_%s_ici_start" % tag, "reduce_ici", [sm[0] for sm in sums], lands)
        return (send, recv, bufs, [sm[1] for sm in sums]), tok

    def reduce_finish(tag, state, after):
        send, recv, bufs, chip_sums = state
        k = len(chip_sums)
        bufs = _remote_wait("reduce_%s_ici_wait" % tag, "reduce_ici", send, recv, bufs, k, after)
        return list(zip(chip_sums, bufs[k:]))

    dup = _mm_nt_blocks("d_up", dfo, wdn_full, tm=tm, tkb=1024, extra=(relu_up,),
                        epilogue=lambda acc, r: (acc * (2.0 * r.astype(F32)),), out_dtypes=(BF16,))[0]
    gw_down, gw_down_own = _mm_tn("dw_down", f, dfo, me_arr, m=f.shape[1], n=d, tma=512, tn=1024, sharded="rows")
    red_down, tok = reduce_start("down", [gw_down], ())
    dn2 = _mm_nt_acc("d_n2", dup, wup_g, tm=tm // 2, deps=[tok])
    gw_up, gw_up_own = _mm_tn("dw_up", n2, dup, me_arr, m=d, n=dup.shape[1], tma=512, tn=1024, sharded="cols")
    red_down, tok = reduce_middle("down", red_down, [gw_down_own], [dn2])
    red_up, tok = reduce_start("up", [gw_up], [tok])
    dh1, dmix, dg_pre_mlp, dg_post_mix = _mid_norm_bwd(dn2, h1, dh2, mix, g_pre_mlp, g_post_mix, deps=[tok])
    dm = _mm_nt_blocks("d_m", dmix, wo_full, tm=tm, tkb=1024)[0]
    red_up, tok = reduce_middle("up", red_up, [gw_up_own], [dm])
    gw_o, gw_o_own = _mm_tn("dw_o", m_mix, dmix, me_arr, m=d, n=d, tma=d // N_DEV, tn=1024, sharded="rows", deps=[tok])
    dyab, dproj, db_gates = _gate_merge_bwd(dm, proj, ya, yb, b_gates, d)
    ycb = d // N_DEV
    da3 = _mm_nt_acc("d_a3", dyab, wpw_g, tm=tm, nb=N_DEV, col_off=0)
    gw_pw, gw_pw_own = _mm_tn("dw_pw", a3, dyab, me_arr, m=dc, n=d, tma=512, tn=ycb, sharded="cols")
    dsb = _mm_nt_acc("d_s", dyab, wso_g, tm=tm, nb=N_DEV, col_off=1)
    gw_so, gw_so_own = _mm_tn("dw_so", s, dyab, me_arr, m=dc, n=d, tma=512, tn=ycb, sharded="cols", b_off=d // ycb)
    red_mix, tok = reduce_start("mix", [gw_pw, gw_so, gw_o], ())
    da1, dln_g, dln_b = _layer_norm_silu_bwd(da3, a1, conf_ln_g, conf_ln_b, deps=[tok])
    dproj, dcw, dcb, dsw = _conv_backward(dproj, proj, da1, dsb, cw_full, sw_full, dc)
    red_mix, tok = reduce_middle("mix", red_mix, [gw_pw_own, gw_so_own, gw_o_own], [dcb])
    in_cb = w_in.shape[2]
    half = d // 2
    red_in = []
    for part in range(2):
        gw, own = _mm_tn("dw_in%d" % part, n, dproj, me_arr, m=half, n=proj.shape[1], tma=512, tn=in_cb,
                         sharded="cols", a_off=part * (half // 512), deps=[tok])
        state, tok = reduce_start("in%d" % part, [gw], ())
        red_in.append((state, own))
    for part in range(2):
        state, own = red_in[part]
        red_in[part], tok = reduce_middle("in%d" % part, state, [own], [tok])
    dn = _mm_nt_acc("d_n", dproj, win_g, tm=tm // 2, deps=[tok])
    dh0, dg_pre_mix = _pre_norm_bwd(dn, h0, dh1, g_pre_mix)
    grad_x = dh0[N_META:t_real][None]

    vec_parts = [dg_pre_mix, db_gates, dcb, dln_g, dln_b, dg_post_mix, dg_pre_mlp, dg_post_mlp]
    packed = _pack_small(vec_parts, dh0[:N_META], dcw, dsw)
    small_g = _all_gather("gather_small_grads", [packed])[0]
    vec_names = ["g_pre_mix", "b_gates", "conf_dw_b", "conf_ln_g", "conf_ln_b", "g_post_mix", "g_pre_mlp", "g_post_mlp"]
    env = locals()
    triple = lambda nm, sq: tuple(env[p + nm][0] if sq else env[p + nm] for p in ("", "m_", "v_"))
    small = _small_update(small_g, me_arr, [triple(nm, False) for nm in vec_names],
                          triple("meta", False), triple("conf_dw_w", True), triple("short_dw_w", True))
    results = {}
    for nm, res in zip(vec_names + ["meta"], small[:len(vec_names) + 1]):
        results[nm] = res
    results["conf_dw_w"] = tuple(r[None] for r in small[-2])
    results["short_dw_w"] = tuple(r[None] for r in small[-1])

    def update(nm, parts):
        res = _adamw_shard("adamw_" + nm, env[nm][0], env["m_" + nm][0], env["v_" + nm][0], parts, me_arr)
        results[nm] = tuple(r[None] for r in res)
        return res[0]

    done = [small[0][0]]
    done.append(update("w_down", reduce_finish("down", red_down, [small_g])))
    done.append(update("w_up", reduce_finish("up", red_up, [small_g])))
    for nm, pair in zip(["conf_w_pw", "short_w_out", "w_o"], reduce_finish("mix", red_mix, [small_g])):
        done.append(update(nm, [pair]))
    update("w_in", [reduce_finish("in%d" % part, red_in[part], done)[0] for part in range(2)])

    order = ["meta", "g_pre_mix", "w_in", "b_gates", "conf_dw_w", "conf_dw_b", "conf_ln_g", "conf_ln_b", "conf_w_pw",
             "short_dw_w", "short_w_out", "w_o", "g_post_mix", "g_pre_mlp", "w_up", "w_down", "g_post_mlp"]
    return (loss, grad_x, *[results[nm][0] for nm in order], *[results[nm][1] for nm in order],
            *[results[nm][2] for nm in order], *[results[nm][3] for nm in order])
```

```python
import jax
import jax.numpy as jnp
from jax import lax
from jax.experimental import pallas as pl
from jax.experimental.pallas import tpu as pltpu

N_DEV = 8
N_META = 16
CONF_K = 31
SHORT_K = 3
RMS_EPS = 1e-6
LN_EPS = 1e-5
ADAM_LR = 0.001
ADAM_B1 = 0.9
ADAM_B2 = 0.999
ADAM_EPS = 1e-08
ADAM_WD = 0.01
ADAM_STEP = 10

LANE = 128
SUB = 8
ROW_TILE = 128
CONV_PAD = 32
CONV_CHUNK = 128
VMEM_LIMIT = 56 * 1024 * 1024

F32 = jnp.float32
BF16 = jnp.bfloat16
MESH = pl.DeviceIdType.MESH
ANY = pl.BlockSpec(memory_space=pl.ANY)
HBM_SPEC = pl.BlockSpec(memory_space=pltpu.HBM)
SEM_SPEC = pl.BlockSpec(memory_space=pltpu.SEMAPHORE)
EFFECT = pltpu.SideEffectType.DATAFLOW_SIDE_EFFECTING


def _params(n_axes):
    return pltpu.CompilerParams(dimension_semantics=("arbitrary",) * n_axes, vmem_limit_bytes=VMEM_LIMIT)


def _sigmoid(z):
    return 1.0 / (1.0 + jnp.exp(-z))


def _colsum8(v):
    r, c = v.shape
    return jnp.sum(v.reshape(r // SUB, SUB, c), axis=0)


def _position():
    x, y, c = lax.axis_index("x"), lax.axis_index("y"), lax.axis_index("c")
    return x, y, c


def _flat(p):
    return 4 * p[0] + 2 * p[1] + p[2]


def _all_gather(name, shards, deps=()):
    n, nd = len(shards), len(deps)

    def body(*refs):
        ins, outs = refs[:n], refs[n + nd:2 * n + nd]
        send_sems, recv_sems, local_sems = refs[2 * n + nd:]
        x, y, c = _position()
        me, sibling = (x, y, c), (x, y, 1 - c)
        chips = [(1 - x, y), (x, 1 - y), (1 - x, 1 - y)]

        def copy(q, k, block, to, src=None):
            dst = outs[q].at[_flat(block)]
            return pltpu.make_async_remote_copy(
                src_ref=dst if src is None else src, dst_ref=dst,
                send_sem=send_sems.at[q, k], recv_sem=recv_sems.at[q, k],
                device_id=to, device_id_type=MESH)

        mine = [pltpu.make_async_copy(ins[q], outs[q].at[_flat(me)], local_sems.at[q]) for q in range(n)]
        for cp in mine:
            cp.start()
        first = []
        for q in range(n):
            first.append(copy(q, 0, me, sibling, src=ins[q]))
            for j, chip in enumerate(chips):
                first.append(copy(q, 1 + j, me, (*chip, c), src=ins[q]))
        for cp in first:
            cp.start()
        passed = []
        for q in range(n):
            for j, chip in enumerate(chips):
                copy(q, 1 + j, (*chip, c), me).wait_recv()
                fwd = copy(q, 4 + j, (*chip, c), sibling)
                fwd.start()
                passed.append(fwd)
        for q in range(n):
            copy(q, 0, sibling, me).wait_recv()
            for j, chip in enumerate(chips):
                copy(q, 4 + j, (*chip, 1 - c), me).wait_recv()
        for cp in first + passed:
            cp.wait_send()
        for cp in mine:
            cp.wait()

    return pl.pallas_call(
        body, name=name,
        in_specs=[ANY] * (n + nd), out_specs=[ANY] * n,
        out_shape=[jax.ShapeDtypeStruct((N_DEV,) + s.shape, s.dtype) for s in shards],
        scratch_shapes=[pltpu.SemaphoreType.DMA((n, 7)), pltpu.SemaphoreType.DMA((n, 7)),
                        pltpu.SemaphoreType.DMA((n,))],
    )(*shards, *deps)


N_COPIES = {"gather_ici": 4, "gather_d2d": 3, "reduce_d2d": 4, "reduce_ici": 3}


def _copy_plan(kind):
    x, y, c = _position()
    me, sibling = (x, y, c), (x, y, 1 - c)
    chips = [(1 - x, y), (x, 1 - y), (1 - x, 1 - y)]
    if kind == "gather_ici":
        return [(_flat(me), _flat(me), sibling)] + [(_flat(me), _flat(me), (*ch, c)) for ch in chips]
    if kind == "gather_d2d":
        return [(_flat((*ch, c)), _flat((*ch, c)), sibling) for ch in chips]
    if kind == "reduce_d2d":
        return [(2 * chip + (1 - c), chip, sibling) for chip in range(4)]
    return [(2 * ch[0] + ch[1], 2 * x + y, (*ch, c)) for ch in chips]


def _planned_copies(kind, srcs, dsts, send_sems, recv_sems):
    plan = _copy_plan(kind)
    return [pltpu.make_async_remote_copy(
        src_ref=src.at[s_slot], dst_ref=dst.at[d_slot],
        send_sem=send_sems.at[q * len(plan) + k], recv_sem=recv_sems.at[q * len(plan) + k],
        device_id=to, device_id_type=MESH)
        for q, (src, dst) in enumerate(zip(srcs, dsts)) for k, (s_slot, d_slot, to) in enumerate(plan)]


def _remote_start(name, kind, srcs, lands=None, deps=()):
    n = len(srcs)
    bufs = list(srcs) + ([] if lands is None else list(lands))
    nb, nd = len(bufs), len(deps)
    nsem = n * N_COPIES[kind]

    def body(*refs):
        ins = refs[:nb]
        send_sems, recv_sems = refs[nb + nd], refs[nb + nd + 1]
        token = refs[-1]
        for cp in _planned_copies(kind, ins[:n], ins[:n] if lands is None else ins[n:], send_sems, recv_sems):
            cp.start()
        token[...] = jnp.zeros_like(token)

    outs = pl.pallas_call(
        body, name=name,
        out_shape=(pltpu.SemaphoreType.DMA((nsem,)), pltpu.SemaphoreType.DMA((nsem,)),
                   *[pltpu.HBM(b.shape, b.dtype) for b in bufs], jax.ShapeDtypeStruct((SUB, LANE), F32)),
        in_specs=[HBM_SPEC] * nb + [ANY] * nd,
        out_specs=(SEM_SPEC, SEM_SPEC, *[HBM_SPEC] * nb, pl.BlockSpec(memory_space=pltpu.VMEM)),
        input_output_aliases={i: 2 + i for i in range(nb)},
        compiler_params=pltpu.CompilerParams(has_side_effects=EFFECT),
    )(*[pltpu.with_memory_space_constraint(b, pltpu.HBM) for b in bufs], *deps)
    return outs[0], outs[1], list(outs[2:2 + nb]), outs[-1]


def _remote_wait(name, kind, send_sems, recv_sems, bufs, n, after):
    nb, na = len(bufs), len(after)
    same = nb == n

    def body(*refs):
        ins = refs[:nb]
        sends, recvs = refs[nb], refs[nb + 1]
        for cp in _planned_copies(kind, ins[:n], ins[:n] if same else ins[n:], sends, recvs):
            cp.wait_send()
            cp.wait_recv()

    outs = pl.pallas_call(
        body, name=name,
        out_shape=[pltpu.HBM(b.shape, b.dtype) for b in bufs],
        in_specs=[HBM_SPEC] * nb + [SEM_SPEC, SEM_SPEC] + [ANY] * na,
        out_specs=[HBM_SPEC] * nb,
        input_output_aliases={i: i for i in range(nb)},
        compiler_params=pltpu.CompilerParams(has_side_effects=EFFECT),
    )(*bufs, send_sems, recv_sems, *after)
    return list(outs)


def _mm_cols(name, a, w, *, tm, nb=1, epilogue=None, out_dtypes=(F32,)):
    t, k = a.shape
    nblk, _, cb = w.shape

    def body(a_ref, w_ref, *o_refs):
        av = a_ref[...]
        for b in range(nb):
            acc = jnp.dot(av, w_ref[b], preferred_element_type=F32)
            outs = (acc,) if epilogue is None else epilogue(acc)
            for o_ref, o in zip(o_refs, outs):
                o_ref[:, b * cb:(b + 1) * cb] = o.astype(o_ref.dtype)

    return pl.pallas_call(
        body, name=name, grid=(nblk // nb, t // tm),
        in_specs=[pl.BlockSpec((tm, k), lambda j, i: (i, 0)),
                  pl.BlockSpec((nb, k, cb), lambda j, i: (j, 0, 0))],
        out_specs=[pl.BlockSpec((tm, nb * cb), lambda j, i: (i, j)) for _ in out_dtypes],
        out_shape=[jax.ShapeDtypeStruct((t, nblk * cb), dt) for dt in out_dtypes],
        compiler_params=_params(2),
    )(a, w)


def _mm_rows(name, a, w2d, *, tm, tn):
    t, kf = a.shape
    n = w2d.shape[1]

    def body(a_ref, w_ref, o_ref):
        o_ref[...] = jnp.dot(a_ref[...], w_ref[...], preferred_element_type=F32)

    return pl.pallas_call(
        body, name=name, grid=(t // tm, n // tn),
        in_specs=[pl.BlockSpec((tm, kf), lambda i, j: (i, 0)),
                  pl.BlockSpec((kf, tn), lambda i, j: (0, j))],
        out_specs=pl.BlockSpec((tm, tn), lambda i, j: (i, j)),
        out_shape=jax.ShapeDtypeStruct((t, n), F32),
        compiler_params=_params(2),
    )(a, w2d)


def _mm_nt_acc(name, dy, w, *, tm, tn, col_off=0, deps=()):
    t = dy.shape[0]
    nblk, k, cb = w.shape

    def body(dy_ref, w_ref, *rest):
        acc = None
        for b in range(nblk):
            d = lax.dot_general(dy_ref[:, b * cb:(b + 1) * cb], w_ref[b], (((1,), (1,)), ((), ())),
                                preferred_element_type=F32)
            acc = d if acc is None else acc + d
        rest[-1][...] = acc

    return pl.pallas_call(
        body, name=name, grid=(t // tm, k // tn),
        in_specs=[pl.BlockSpec((tm, nblk * cb), lambda i, j: (i, col_off)),
                  pl.BlockSpec((nblk, tn, cb), lambda i, j: (0, j, 0))] + [ANY] * len(deps),
        out_specs=pl.BlockSpec((tm, tn), lambda i, j: (i, j)),
        out_shape=jax.ShapeDtypeStruct((t, k), F32),
        compiler_params=_params(2),
    )(dy, w, *deps)


def _mm_nt_blocks(name, dy, w2d, *, tm, tkb, extra=(), epilogue=None, out_dtypes=(F32,)):
    t, n = dy.shape
    kf = w2d.shape[0]
    ne = len(extra)

    def body(dy_ref, w_ref, *rest):
        acc = lax.dot_general(dy_ref[...], w_ref[...], (((1,), (1,)), ((), ())), preferred_element_type=F32)
        outs = (acc,) if epilogue is None else epilogue(acc, *[e[...] for e in rest[:ne]])
        for o_ref, o in zip(rest[ne:], outs):
            o_ref[...] = o.astype(o_ref.dtype)

    return pl.pallas_call(
        body, name=name, grid=(kf // tkb, t // tm),
        in_specs=[pl.BlockSpec((tm, n), lambda kb, i: (i, 0)),
                  pl.BlockSpec((tkb, n), lambda kb, i: (kb, 0))]
                 + [pl.BlockSpec((tm, tkb), lambda kb, i: (i, kb)) for _ in extra],
        out_specs=[pl.BlockSpec((tm, tkb), lambda kb, i: (i, kb)) for _ in out_dtypes],
        out_shape=[jax.ShapeDtypeStruct((t, kf), dt) for dt in out_dtypes],
        compiler_params=_params(2),
    )(dy, w2d, *extra)


def _mm_tn(name, a, b, me_arr, *, m, n, tma, tn, sharded, a_off=0, b_off=0, deps=()):
    t = a.shape[0]
    if sharded == "cols":
        cb = n // N_DEV
        q = cb // tn
        full_shape, own_shape = (N_DEV, m, cb), (m, cb)
        full_spec = pl.BlockSpec((1, tma, tn), lambda i, j, me: (j // q, i, j % q))
    else:
        kb = m // N_DEV
        p = kb // tma
        full_shape, own_shape = (m, n), (kb, n)
        full_spec = pl.BlockSpec((tma, tn), lambda i, j, me: (i, j))

    def body(me_ref, a_ref, b_ref, *rest):
        full_ref, own_ref, stage, sem = rest[len(deps):]
        i, j = pl.program_id(0), pl.program_id(1)
        acc = lax.dot_general(a_ref[...], b_ref[...], (((0,), (0,)), ((), ())), preferred_element_type=F32)
        if sharded == "cols":
            full_ref[0] = acc.astype(BF16)
            owner, r0, c0 = j // q, i * tma, (j % q) * tn
        else:
            full_ref[...] = acc.astype(BF16)
            owner, r0, c0 = i // p, (i % p) * tma, j * tn

        @pl.when(owner == me_ref[0])
        def _():
            stage[...] = acc
            cp = pltpu.make_async_copy(
                stage, own_ref.at[pl.ds(pl.multiple_of(r0, tma), tma), pl.ds(pl.multiple_of(c0, tn), tn)], sem)
            cp.start()
            cp.wait()

    full, own = pl.pallas_call(
        body, name=name,
        grid_spec=pltpu.PrefetchScalarGridSpec(
            num_scalar_prefetch=1, grid=(m // tma, n // tn),
            in_specs=[pl.BlockSpec((t, tma), lambda i, j, me: (0, a_off + i)),
                      pl.BlockSpec((t, tn), lambda i, j, me: (0, b_off + j))] + [ANY] * len(deps),
            out_specs=[full_spec, ANY],
            scratch_shapes=[pltpu.VMEM((tma, tn), F32), pltpu.SemaphoreType.DMA(())]),
        out_shape=[jax.ShapeDtypeStruct(full_shape, BF16), jax.ShapeDtypeStruct(own_shape, F32)],
        compiler_params=_params(2),
    )(me_arr, a, b, *deps)
    if sharded == "rows":
        full = full.reshape(N_DEV, m // N_DEV, n)
    return full, own


def _row_call(name, body, t, row_ins, full_ins, row_outs, acc_outs, scratch=(), deps=()):
    tm = ROW_TILE
    nin = len(row_ins) + len(full_ins)

    def without_deps(*refs):
        body(*refs[:nin], *refs[nin + len(deps):])

    return pl.pallas_call(
        without_deps, name=name, grid=(t // tm,),
        in_specs=[pl.BlockSpec((tm, a.shape[1]), lambda i: (i, 0)) for a in row_ins]
                 + [pl.BlockSpec(a.shape, lambda i: (0, 0)) for a in full_ins] + [ANY] * len(deps),
        out_specs=[pl.BlockSpec((tm, c), lambda i: (i, 0)) for c, _ in row_outs]
                  + [pl.BlockSpec((r, c), lambda i: (0, 0)) for r, c in acc_outs],
        out_shape=[jax.ShapeDtypeStruct((t, c), dt) for c, dt in row_outs]
                  + [jax.ShapeDtypeStruct((r, c), F32) for r, c in acc_outs],
        scratch_shapes=list(scratch),
        compiler_params=_params(1),
    )(*row_ins, *full_ins, *deps)


def _accumulate(ref, v):
    @pl.when(pl.program_id(0) == 0)
    def _():
        ref[...] = v

    @pl.when(pl.program_id(0) > 0)
    def _():
        ref[...] += v


def _rms(v):
    return lax.rsqrt(jnp.mean(v * v, axis=-1, keepdims=True) + RMS_EPS)


def _rms_bwd(dout, u, r, g):
    du = dout * g
    dx = r * (du - u * jnp.mean(du * u, axis=-1, keepdims=True))
    return dx, _colsum8(dout * u)


def _pre_norm(h0, g):
    t, d = h0.shape

    def body(h_ref, g_ref, n_ref):
        h = h_ref[...]
        n_ref[...] = (h * _rms(h) * g_ref[...]).astype(BF16)

    return _row_call("pre_norm", body, t, [h0], [g], [(d, BF16)], [])[0]


def _post_mix(mix, h0, g_post, g_pre, deps=()):
    t, d = h0.shape

    def body(mix_ref, h0_ref, gp_ref, gq_ref, h1_ref, n2_ref):
        mix_v = mix_ref[...]
        h1 = h0_ref[...] + mix_v * _rms(mix_v) * gp_ref[...]
        h1_ref[...] = h1
        n2_ref[...] = (h1 * _rms(h1) * gq_ref[...]).astype(BF16)

    return _row_call("post_mix", body, t, [mix, h0], [g_post, g_pre], [(d, F32), (d, BF16)], [], deps=deps)


def _loss_head(fo, h1, tgt, g_post_mlp, t_real):
    t, d = h1.shape

    def body(fo_ref, h1_ref, tgt_ref, g_ref, dfo_ref, dh2_ref, dg_ref, loss_ref, lacc):
        i = pl.program_id(0)
        fo_v = fo_ref[...]
        g = g_ref[...]
        r = _rms(fo_v)
        u = fo_v * r
        h2 = h1_ref[...] + u * g
        row = i * ROW_TILE + lax.broadcasted_iota(jnp.int32, (ROW_TILE, 1), 0)
        valid = jnp.logical_and(row >= N_META, row < t_real)
        diff = jnp.where(valid, h2 - tgt_ref[...], 0.0)
        dh2 = diff * (1.0 / d)
        dh2_ref[...] = dh2
        dfo, dg = _rms_bwd(dh2, u, r, g)
        dfo_ref[...] = dfo.astype(BF16)
        _accumulate(dg_ref, dg)
        _accumulate(lacc, _colsum8(diff * diff))

        @pl.when(i == pl.num_programs(0) - 1)
        def _():
            loss_ref[...] = jnp.full((SUB, LANE), (0.5 / d) * jnp.sum(lacc[...]), F32)

    return _row_call("loss_head", body, t, [fo, h1, tgt], [g_post_mlp],
                     [(d, BF16), (d, F32)], [(SUB, d), (SUB, LANE)], scratch=[pltpu.VMEM((SUB, d), F32)])


def _mid_norm_bwd(dn2, h1, dh2, mix, g_pre_mlp, g_post_mix, deps=()):
    t, d = h1.shape

    def body(dn2_ref, h1_ref, dh2_ref, mix_ref, gq_ref, gp_ref, dh1_ref, dmix_ref, dgq_ref, dgp_ref):
        h1 = h1_ref[...]
        r3 = _rms(h1)
        dx, dgq = _rms_bwd(dn2_ref[...], h1 * r3, r3, gq_ref[...])
        dh1 = dh2_ref[...] + dx
        dh1_ref[...] = dh1
        mix_v = mix_ref[...]
        r2 = _rms(mix_v)
        dmix, dgp = _rms_bwd(dh1, mix_v * r2, r2, gp_ref[...])
        dmix_ref[...] = dmix.astype(BF16)
        _accumulate(dgq_ref, dgq)
        _accumulate(dgp_ref, dgp)

    return _row_call("mid_norm_bwd", body, t, [dn2, h1, dh2, mix], [g_pre_mlp, g_post_mix],
                     [(d, F32), (d, BF16)], [(SUB, d), (SUB, d)], deps=deps)


def _pre_norm_bwd(dn, h0, dh1, g_pre_mix, deps=()):
    t, d = h0.shape

    def body(dn_ref, h0_ref, dh1_ref, g_ref, dh0_ref, dg_ref):
        h0 = h0_ref[...]
        r = _rms(h0)
        dx, dg = _rms_bwd(dn_ref[...], h0 * r, r, g_ref[...])
        dh0_ref[...] = dh1_ref[...] + dx
        _accumulate(dg_ref, dg)

    return _row_call("pre_norm_bwd", body, t, [dn, h0, dh1], [g_pre_mix], [(d, F32)], [(SUB, d)], deps=deps)


def _layer_norm_silu(a1, ln_g, ln_b):
    t, c = a1.shape

    def body(a1_ref, g_ref, b_ref, a3_ref):
        a = a1_ref[...]
        mu = jnp.mean(a, axis=-1, keepdims=True)
        xc = a - mu
        rstd = lax.rsqrt(jnp.mean(xc * xc, axis=-1, keepdims=True) + LN_EPS)
        z = xc * rstd * g_ref[...] + b_ref[...]
        a3_ref[...] = (z * _sigmoid(z)).astype(BF16)

    return _row_call("layer_norm_silu", body, t, [a1], [ln_g, ln_b], [(c, BF16)], [])[0]


def _layer_norm_silu_bwd(da3, a1, ln_g, ln_b, deps=()):
    t, c = a1.shape

    def body(da3_ref, a1_ref, g_ref, b_ref, da1_ref, dg_ref, db_ref):
        a = a1_ref[...]
        g = g_ref[...]
        mu = jnp.mean(a, axis=-1, keepdims=True)
        xc = a - mu
        rstd = lax.rsqrt(jnp.mean(xc * xc, axis=-1, keepdims=True) + LN_EPS)
        xhat = xc * rstd
        z = xhat * g + b_ref[...]
        sg = _sigmoid(z)
        dz = da3_ref[...] * (sg * (1.0 + z * (1.0 - sg)))
        dxhat = dz * g
        da1_ref[...] = rstd * (dxhat - jnp.mean(dxhat, axis=-1, keepdims=True)
                               - xhat * jnp.mean(dxhat * xhat, axis=-1, keepdims=True))
        _accumulate(dg_ref, _colsum8(dz * xhat))
        _accumulate(db_ref, _colsum8(dz))

    return _row_call("layer_norm_silu_bwd", body, t, [da3, a1], [ln_g, ln_b], [(c, F32)], [(SUB, c), (SUB, c)], deps=deps)


def _gate_merge(proj, ya, yb, b_gates, d, deps=()):
    t = proj.shape[0]
    w = 1024
    nh = d // w
    ga0 = (proj.shape[1] - 2 * d) // w

    def body(pa_ref, pb_ref, ya_ref, yb_ref, ba_ref, bb_ref, *rest):
        m_ref = rest[-1]
        ga = _sigmoid(pa_ref[...] + ba_ref[...])
        gb = _sigmoid(pb_ref[...] + bb_ref[...])
        m_ref[...] = (ga * ya_ref[...] + gb * yb_ref[...]).astype(BF16)

    tm = ROW_TILE
    return pl.pallas_call(
        body, name="gate_merge", grid=(nh, t // tm),
        in_specs=[pl.BlockSpec((tm, w), lambda h, i: (i, ga0 + h)),
                  pl.BlockSpec((tm, w), lambda h, i: (i, ga0 + nh + h)),
                  pl.BlockSpec((tm, w), lambda h, i: (i, h)),
                  pl.BlockSpec((tm, w), lambda h, i: (i, h)),
                  pl.BlockSpec((1, w), lambda h, i: (0, h)),
                  pl.BlockSpec((1, w), lambda h, i: (0, nh + h))] + [ANY] * len(deps),
        out_specs=pl.BlockSpec((tm, w), lambda h, i: (i, h)),
        out_shape=jax.ShapeDtypeStruct((t, d), BF16),
        compiler_params=_params(2),
    )(proj, proj, ya, yb, b_gates, b_gates, *deps)


def _gate_backward(dmix, wo_full, proj, ya, yb, b_gates, d, tm, deps=()):
    t, cols = proj.shape
    w = 1024
    nh = d // w
    ga0 = (cols - 2 * d) // w

    def body(dmix_ref, wo_ref, pa_ref, pb_ref, ya_ref, yb_ref, ba_ref, bb_ref, *rest):
        dya_ref, dyb_ref, dp_ref, dba_ref, dbb_ref, stage, sems = rest[len(deps):]
        h, i = pl.program_id(0), pl.program_id(1)
        dm = lax.dot_general(dmix_ref[...], wo_ref[...], (((1,), (1,)), ((), ())), preferred_element_type=F32)
        ga = _sigmoid(pa_ref[...] + ba_ref[...])
        gb = _sigmoid(pb_ref[...] + bb_ref[...])
        dya_ref[...] = (dm * ga).astype(BF16)
        dyb_ref[...] = (dm * gb).astype(BF16)
        dpa = dm * ya_ref[...] * ga * (1.0 - ga)
        dpb = dm * yb_ref[...] * gb * (1.0 - gb)
        stage[0] = dpa.astype(BF16)
        stage[1] = dpb.astype(BF16)
        rows = pl.ds(pl.multiple_of(i * tm, tm), tm)
        copies = [pltpu.make_async_copy(
            stage.at[g], dp_ref.at[rows, pl.ds(pl.multiple_of((ga0 + g * nh + h) * w, w), w)], sems.at[g])
            for g in range(2)]
        for cp in copies:
            cp.start()

        @pl.when(i == 0)
        def _():
            dba_ref[...] = _colsum8(dpa)
            dbb_ref[...] = _colsum8(dpb)

        @pl.when(i > 0)
        def _():
            dba_ref[...] += _colsum8(dpa)
            dbb_ref[...] += _colsum8(dpb)

        for cp in copies:
            cp.wait()

    tile = pl.BlockSpec((tm, w), lambda h, i: (i, h))
    return pl.pallas_call(
        body, name="gate_backward", grid=(nh, t // tm),
        in_specs=[pl.BlockSpec((tm, d), lambda h, i: (i, 0)),
                  pl.BlockSpec((w, d), lambda h, i: (h, 0)),
                  pl.BlockSpec((tm, w), lambda h, i: (i, ga0 + h)),
                  pl.BlockSpec((tm, w), lambda h, i: (i, ga0 + nh + h)),
                  tile, tile,
                  pl.BlockSpec((1, w), lambda h, i: (0, h)),
                  pl.BlockSpec((1, w), lambda h, i: (0, nh + h))] + [ANY] * len(deps),
        out_specs=[tile, tile, ANY,
                   pl.BlockSpec((SUB, w), lambda h, i: (0, h)),
                   pl.BlockSpec((SUB, w), lambda h, i: (0, h))],
        out_shape=[jax.ShapeDtypeStruct((t, d), BF16), jax.ShapeDtypeStruct((t, d), BF16),
                   jax.ShapeDtypeStruct((t, cols), BF16),
                   jax.ShapeDtypeStruct((SUB, d), F32), jax.ShapeDtypeStruct((SUB, d), F32)],
        scratch_shapes=[pltpu.VMEM((2, tm, w), BF16), pltpu.SemaphoreType.DMA((2,))],
        compiler_params=_params(2),
    )(dmix, wo_full, proj, proj, ya, yb, b_gates, b_gates, *deps)


def _causal_conv(xp_ref, w_ref, ntap, r0):
    n = CONV_CHUNK + CONV_PAD
    win = xp_ref[pl.ds(r0, n), :]
    acc = None
    for k in range(ntap):
        back = ntap - 1 - k
        shifted = pltpu.roll(win, n - (CONV_PAD - back), 0)
        term = w_ref[k:k + 1, :] * shifted[:CONV_CHUNK]
        acc = term if acc is None else acc + term
    return acc


def _anticausal_conv(xp_ref, w_ref, ntap, r0):
    n = CONV_CHUNK + CONV_PAD
    win = xp_ref[pl.ds(pl.multiple_of(CONV_PAD + r0, CONV_PAD), n), :]
    acc = None
    for k in range(ntap):
        ahead = ntap - 1 - k
        shifted = win if ahead == 0 else pltpu.roll(win, n - ahead, 0)
        term = w_ref[k:k + 1, :] * shifted[:CONV_CHUNK]
        acc = term if acc is None else acc + term
    return acc


def _conv_weight_grad(dw_ref, d_chunk, xp_ref, ntap, r0):
    n = CONV_CHUNK + CONV_PAD
    win = xp_ref[pl.ds(r0, n), :]
    for k in range(ntap):
        back = ntap - 1 - k
        shifted = pltpu.roll(win, n - (CONV_PAD - back), 0)
        dw_ref[k * SUB:(k + 1) * SUB, :] += _colsum8(d_chunk * shifted[:CONV_CHUNK])


def _zero_pads(ref, t):
    ref[0:CONV_PAD, :] = jnp.zeros((CONV_PAD, LANE), F32)
    ref[CONV_PAD + t:CONV_PAD + t + CONV_PAD, :] = jnp.zeros((CONV_PAD, LANE), F32)


def _for_chunks(t, fn):
    def step(idx, carry):
        fn(pl.multiple_of(idx * CONV_CHUNK, CONV_CHUNK))
        return carry

    lax.fori_loop(0, t // CONV_CHUNK, step, 0)


def _conv_forward(proj, conf_w, conf_b, short_w, dc, deps=()):
    t = proj.shape[0]
    nc = dc // LANE

    def body(av_ref, ag_ref, bg_ref, cg_ref, v_ref, cw_ref, cb_ref, sw_ref, *rest):
        a1_ref, s_ref, xa, xb = rest[len(deps):]
        _zero_pads(xa, t)
        _zero_pads(xb, t)
        xa[CONV_PAD:CONV_PAD + t, :] = av_ref[...] * _sigmoid(ag_ref[...])
        xb[CONV_PAD:CONV_PAD + t, :] = cg_ref[...] * v_ref[...]

        def chunk(r0):
            rs = pl.ds(r0, CONV_CHUNK)
            a1_ref[rs, :] = _causal_conv(xa, cw_ref, CONF_K, r0) + cb_ref[...]
            s_ref[rs, :] = (bg_ref[rs, :] * _causal_conv(xb, sw_ref, SHORT_K, r0)).astype(BF16)

        _for_chunks(t, chunk)

    col = lambda g: pl.BlockSpec((t, LANE), lambda c, g=g: (0, g * nc + c))
    return pl.pallas_call(
        body, name="conv_forward", grid=(nc,),
        in_specs=[col(0), col(1), col(2), col(3), col(4),
                  pl.BlockSpec((CONF_K, LANE), lambda c: (0, c)),
                  pl.BlockSpec((1, LANE), lambda c: (0, c)),
                  pl.BlockSpec((SHORT_K, LANE), lambda c: (0, c))] + [ANY] * len(deps),
        out_specs=[pl.BlockSpec((t, LANE), lambda c: (0, c)), pl.BlockSpec((t, LANE), lambda c: (0, c))],
        out_shape=[jax.ShapeDtypeStruct((t, dc), F32), jax.ShapeDtypeStruct((t, dc), BF16)],
        scratch_shapes=[pltpu.VMEM((t + 2 * CONV_PAD, LANE), F32), pltpu.VMEM((t + 2 * CONV_PAD, LANE), F32)],
        compiler_params=_params(1),
    )(proj, proj, proj, proj, proj, conf_w, conf_b, short_w, *deps)


def _conv_backward(dproj, proj, da1, ds, conf_w, short_w, dc):
    t = proj.shape[0]
    nc = dc // LANE

    def body(dp_in, av_ref, ag_ref, bg_ref, cg_ref, v_ref, da1_ref, ds_ref, cw_ref, sw_ref,
             dp_ref, dcw_ref, dcb_ref, dsw_ref, xa, xb, da, db, stage, sems):
        del dp_in
        c = pl.program_id(0)
        for ref in (xa, xb, da, db):
            _zero_pads(ref, t)
        xa[CONV_PAD:CONV_PAD + t, :] = av_ref[...] * _sigmoid(ag_ref[...])
        xb[CONV_PAD:CONV_PAD + t, :] = cg_ref[...] * v_ref[...]
        da[CONV_PAD:CONV_PAD + t, :] = da1_ref[...]
        dcw_ref[...] = jnp.zeros(dcw_ref.shape, F32)
        dsw_ref[...] = jnp.zeros(dsw_ref.shape, F32)
        dcb_ref[...] = jnp.zeros(dcb_ref.shape, F32)

        def through_gate(r0):
            rs = pl.ds(r0, CONV_CHUNK)
            ds_c = ds_ref[rs, :]
            stage[2, rs, :] = (ds_c * _causal_conv(xb, sw_ref, SHORT_K, r0)).astype(BF16)
            db[pl.ds(pl.multiple_of(CONV_PAD + r0, CONV_PAD), CONV_CHUNK), :] = ds_c * bg_ref[rs, :]

        _for_chunks(t, through_gate)

        def through_convs(r0):
            rs = pl.ds(r0, CONV_CHUNK)
            da0 = _anticausal_conv(da, cw_ref, CONF_K, r0)
            sg = _sigmoid(ag_ref[rs, :])
            stage[0, rs, :] = (da0 * sg).astype(BF16)
            stage[1, rs, :] = (da0 * av_ref[rs, :] * sg * (1.0 - sg)).astype(BF16)
            dcv = _anticausal_conv(db, sw_ref, SHORT_K, r0)
            stage[3, rs, :] = (dcv * v_ref[rs, :]).astype(BF16)
            stage[4, rs, :] = (dcv * cg_ref[rs, :]).astype(BF16)
            da1_c = da1_ref[rs, :]
            _conv_weight_grad(dcw_ref, da1_c, xa, CONF_K, r0)
            _conv_weight_grad(dsw_ref, ds_ref[rs, :] * bg_ref[rs, :], xb, SHORT_K, r0)
            dcb_ref[...] += _colsum8(da1_c)

        _for_chunks(t, through_convs)
        copies = [pltpu.make_async_copy(
            stage.at[g], dp_ref.at[:, pl.ds(pl.multiple_of((g * nc + c) * LANE, LANE), LANE)], sems.at[g])
            for g in range(5)]
        for cp in copies:
            cp.start()
        for cp in copies:
            cp.wait()

    col = lambda g: pl.BlockSpec((t, LANE), lambda c, g=g: (0, g * nc + c))
    blk = pl.BlockSpec((t, LANE), lambda c: (0, c))
    return pl.pallas_call(
        body, name="conv_backward", grid=(nc,),
        in_specs=[ANY, col(0), col(1), col(2), col(3), col(4), blk, blk,
                  pl.BlockSpec((CONF_K, LANE), lambda c: (0, c)),
                  pl.BlockSpec((SHORT_K, LANE), lambda c: (0, c))],
        out_specs=[ANY,
                   pl.BlockSpec((CONF_K * SUB, LANE), lambda c: (0, c)),
                   pl.BlockSpec((SUB, LANE), lambda c: (0, c)),
                   pl.BlockSpec((SHORT_K * SUB, LANE), lambda c: (0, c))],
        out_shape=[jax.ShapeDtypeStruct(dproj.shape, dproj.dtype),
                   jax.ShapeDtypeStruct((CONF_K * SUB, dc), F32),
                   jax.ShapeDtypeStruct((SUB, dc), F32),
                   jax.ShapeDtypeStruct((SHORT_K * SUB, dc), F32)],
        scratch_shapes=[pltpu.VMEM((t + 2 * CONV_PAD, LANE), F32)] * 4
                       + [pltpu.VMEM((5, t, LANE), BF16), pltpu.SemaphoreType.DMA((5,))],
        input_output_aliases={0: 0},
        compiler_params=_params(1),
    )(dproj, proj, proj, proj, proj, proj, da1, ds, conf_w, short_w)


def _adamw_math(w, g, m, v):
    m = ADAM_B1 * m + (1.0 - ADAM_B1) * g
    v = ADAM_B2 * v + (1.0 - ADAM_B2) * (g * g)
    m_hat = m / (1.0 - ADAM_B1 ** ADAM_STEP)
    v_hat = v / (1.0 - ADAM_B2 ** ADAM_STEP)
    delta = -ADAM_LR * (m_hat / (jnp.sqrt(v_hat) + ADAM_EPS) + ADAM_WD * w)
    return delta, m, v


def _cast_into_slot(name, w, me_arr, deps=()):
    r, c = w.shape
    tr = 256

    def body(me_ref, w_ref, *rest):
        del me_ref
        rest[-1][0] = w_ref[...].astype(BF16)

    return pl.pallas_call(
        body, name=name,
        grid_spec=pltpu.PrefetchScalarGridSpec(
            num_scalar_prefetch=1, grid=(r // tr,),
            in_specs=[pl.BlockSpec((tr, c), lambda i, me: (i, 0))] + [ANY] * len(deps),
            out_specs=pl.BlockSpec((1, tr, c), lambda i, me: (me[0], i, 0))),
        out_shape=jax.ShapeDtypeStruct((N_DEV, r, c), BF16),
        compiler_params=_params(1),
    )(me_arr, w, *deps)


def _chip_sum(name, full, from_sibling, own, me_arr):
    _, r, c = full.shape
    tr = min(r, 512)

    def body(me_ref, full_ref, sib_ref, own_ref, sums_ref, mine_ref):
        theirs = sib_ref[0].astype(F32)
        sums_ref[0] = (full_ref[0].astype(F32) + theirs).astype(BF16)

        @pl.when(pl.program_id(1) == me_ref[0] // 2)
        def _():
            mine_ref[...] = own_ref[...] + theirs

    return pl.pallas_call(
        body, name=name,
        grid_spec=pltpu.PrefetchScalarGridSpec(
            num_scalar_prefetch=1, grid=(r // tr, 4),
            in_specs=[pl.BlockSpec((1, tr, c), lambda i, chip, me: (2 * chip + me[0] % 2, i, 0)),
                      pl.BlockSpec((1, tr, c), lambda i, chip, me: (chip, i, 0)),
                      pl.BlockSpec((tr, c), lambda i, chip, me: (i, 0))],
            out_specs=[pl.BlockSpec((1, tr, c), lambda i, chip, me: (chip, i, 0)),
                       pl.BlockSpec((tr, c), lambda i, chip, me: (i, 0))]),
        out_shape=[jax.ShapeDtypeStruct((4, r, c), BF16), jax.ShapeDtypeStruct((r, c), F32)],
        compiler_params=_params(2),
    )(me_arr, full, from_sibling, own)


def _adamw_shard(name, w, m, v, parts, me_arr):
    r, c = w.shape
    tr = 128
    np_ = len(parts)
    per = r // np_ // tr

    def body(me_ref, w_ref, m_ref, v_ref, *rest):
        g_out, d_out, m_out, v_out = rest[4 * np_:]
        g = None
        for p in range(np_):
            gp = rest[4 * p][...]
            for l_ref in rest[4 * p + 1:4 * p + 4]:
                gp = gp + l_ref[0].astype(F32)
            g = gp if g is None else jnp.where(pl.program_id(0) // per == p, gp, g)
        delta, m_new, v_new = _adamw_math(w_ref[...], g, m_ref[...], v_ref[...])
        g_out[...] = g
        d_out[...] = delta
        m_out[...] = m_new
        v_out[...] = v_new

    tile = pl.BlockSpec((tr, c), lambda i, me: (i, 0))
    part_specs, part_args = [], []
    for p, (g_chip, landed) in enumerate(parts):
        row = lambda i, p=p: jnp.clip(i - p * per, 0, per - 1)
        part_specs.append(pl.BlockSpec((tr, c), lambda i, me, row=row: (row(i), 0)))
        part_specs += [pl.BlockSpec((1, tr, c), lambda i, me, k=k, row=row: ((me[0] // 2 + k) % 4, row(i), 0))
                       for k in range(1, 4)]
        part_args += [g_chip, landed, landed, landed]
    return pl.pallas_call(
        body, name=name,
        grid_spec=pltpu.PrefetchScalarGridSpec(
            num_scalar_prefetch=1, grid=(r // tr,),
            in_specs=[tile] * 3 + part_specs, out_specs=[tile] * 4),
        out_shape=[jax.ShapeDtypeStruct((r, c), F32)] * 4,
        compiler_params=_params(1),
    )(me_arr, w, m, v, *part_args)


SMALL_W = 1024
VEC_ROWS = 16
META_ROW0 = 16
CONF_ROW0 = 64
SHORT_ROW0 = 96
SMALL_ROWS = 104


def _pack_small(vec_parts, dmeta, dcw, dsw):
    widths = [p.shape[1] for p in vec_parts]
    nv = len(vec_parts)

    def body(*refs):
        parts, (dmeta_ref, dcw_ref, dsw_ref, out_ref) = refs[:nv], refs[nv:]
        out_ref[...] = jnp.zeros((SMALL_ROWS, SMALL_W), F32)
        row = 0
        for p_ref, wd in zip(parts, widths):
            s = jnp.sum(p_ref[...], axis=0, keepdims=True)
            for h in range(wd // SMALL_W):
                out_ref[row:row + 1, :] = s[:, h * SMALL_W:(h + 1) * SMALL_W]
                row += 1
        for h in range(dmeta_ref.shape[1] // SMALL_W):
            out_ref[META_ROW0 + h * N_META:META_ROW0 + (h + 1) * N_META, :] = dmeta_ref[:, h * SMALL_W:(h + 1) * SMALL_W]
        for k in range(CONF_K):
            out_ref[CONF_ROW0 + k:CONF_ROW0 + k + 1, :] = jnp.sum(dcw_ref[k * SUB:(k + 1) * SUB, :], axis=0, keepdims=True)
        for k in range(SHORT_K):
            out_ref[SHORT_ROW0 + k:SHORT_ROW0 + k + 1, :] = jnp.sum(dsw_ref[k * SUB:(k + 1) * SUB, :], axis=0, keepdims=True)

    return pl.pallas_call(
        body, name="pack_small",
        out_shape=jax.ShapeDtypeStruct((SMALL_ROWS, SMALL_W), F32),
        compiler_params=pltpu.CompilerParams(vmem_limit_bytes=VMEM_LIMIT),
    )(*vec_parts, dmeta, dcw, dsw)


def _small_update(gathered, me_arr, vec_params, meta_p, conf_p, short_p):
    widths = [p[0].shape[1] for p in vec_params]
    nv = len(vec_params)
    mcols = meta_p[0].shape[1]
    per_row = SMALL_W // mcols

    def body(me_ref, gv_ref, gm_ref, gc_ref, gs_ref, *rest):
        del me_ref
        ins, outs = rest[:3 * (nv + 3)], rest[3 * (nv + 3):]

        def total(ref, r0, rows):
            s = ref[0, r0:r0 + rows, :]
            for dev in range(1, N_DEV):
                s = s + ref[dev, r0:r0 + rows, :]
            return s

        grads = []
        row = 0
        for wd in widths:
            pieces = [total(gv_ref, row + h, 1) for h in range(wd // SMALL_W)]
            grads.append(pieces[0] if len(pieces) == 1 else jnp.concatenate(pieces, axis=1))
            row += len(pieces)
        grads.append(total(gm_ref, 0, N_META))
        grads.append(total(gc_ref, 0, CONF_K))
        grads.append(total(gs_ref, 0, SHORT_K))
        for idx, g in enumerate(grads):
            w_ref, m_ref, v_ref = ins[3 * idx:3 * idx + 3]
            delta, m_new, v_new = _adamw_math(w_ref[...], g, m_ref[...], v_ref[...])
            g_out, d_out, m_out, v_out = outs[4 * idx:4 * idx + 4]
            g_out[...] = g
            d_out[...] = delta
            m_out[...] = m_new
            v_out[...] = v_new

    params = list(vec_params) + [meta_p, conf_p, short_p]
    flat = [a for p in params for a in p]
    whole = lambda a: pl.BlockSpec(a.shape, lambda i, me: (0,) * a.ndim)
    outs = pl.pallas_call(
        body, name="small_update",
        grid_spec=pltpu.PrefetchScalarGridSpec(
            num_scalar_prefetch=1, grid=(1,),
            in_specs=[pl.BlockSpec((N_DEV, VEC_ROWS, SMALL_W), lambda i, me: (0, 0, 0)),
                      pl.BlockSpec((N_DEV, N_META, mcols),
                                   lambda i, me: (0, META_ROW0 // N_META + me[0] // per_row, me[0] % per_row)),
                      pl.BlockSpec((N_DEV, 32, LANE), lambda i, me: (0, CONF_ROW0 // 32, me[0])),
                      pl.BlockSpec((N_DEV, SUB, LANE), lambda i, me: (0, SHORT_ROW0 // SUB, me[0]))]
                     + [whole(a) for a in flat],
            out_specs=[whole(p[0]) for p in params for _ in range(4)]),
        out_shape=[jax.ShapeDtypeStruct(p[0].shape, F32) for p in params for _ in range(4)],
        compiler_params=_params(1),
    )(me_arr, gathered, gathered, gathered, gathered, *flat)
    return [tuple(outs[4 * i:4 * i + 4]) for i in range(len(params))]


def kernel(x, meta, g_pre_mix, w_in, b_gates, conf_dw_w, conf_dw_b, conf_ln_g, conf_ln_b, conf_w_pw, short_dw_w, short_w_out, w_o, g_post_mix, g_pre_mlp, w_up, w_down, g_post_mlp, loss_target, m_meta, m_g_pre_mix, m_w_in, m_b_gates, m_conf_dw_w, m_conf_dw_b, m_conf_ln_g, m_conf_ln_b, m_conf_w_pw, m_short_dw_w, m_short_w_out, m_w_o, m_g_post_mix, m_g_pre_mlp, m_w_up, m_w_down, m_g_post_mlp, v_meta, v_g_pre_mix, v_w_in, v_b_gates, v_conf_dw_w, v_conf_dw_b, v_conf_ln_g, v_conf_ln_b, v_conf_w_pw, v_short_dw_w, v_short_w_out, v_w_o, v_g_post_mix, v_g_pre_mlp, v_w_up, v_w_down, v_g_post_mlp):
    seq, d = x.shape[1], x.shape[2]
    dc = conf_w_pw.shape[1]
    t_real = N_META + seq
    t = -(-t_real // ROW_TILE) * ROW_TILE
    tm = t // 2
    assert tm % 16 == 0 and d % 1024 == 0 and dc % 1024 == 0
    x_idx, y_idx, c_idx = _position()
    me_arr = jnp.reshape(4 * x_idx + 2 * y_idx + c_idx, (1,)).astype(jnp.int32)

    big = [w_in[0], conf_w_pw[0], short_w_out[0], w_o[0], w_up[0], w_down[0]]
    big_names = ["w_in", "conf_w_pw", "short_w_out", "w_o", "w_up", "w_down"]
    groups = [[0], [1, 2, 3], [4], [5]]
    meta_g, cw_g, sw_g = _all_gather("gather_small_params", [meta, conf_dw_w[0], short_dw_w[0]])
    ici = []
    for g, idxs in enumerate(groups):
        order = [meta_g] + [st[3] for st in ici[-1:]]
        slots = [_cast_into_slot("cast_" + big_names[i], big[i], me_arr, deps=order) for i in idxs]
        ici.append(_remote_start("gather%d_ici_start" % g, "gather_ici", slots, deps=order))
    started = ici[-1][3][0, 0] * 0.0

    def forward_on(g, after):
        send, recv, bufs, _ = ici[g]
        bufs = _remote_wait("gather%d_ici_wait" % g, "gather_ici", send, recv, bufs, len(bufs), after)
        send, recv, bufs, tok = _remote_start("gather%d_d2d_start" % g, "gather_d2d", bufs)
        return (send, recv, bufs), tok

    def gathered(g, state, after):
        send, recv, bufs = state
        return _remote_wait("gather%d_d2d_wait" % g, "gather_d2d", send, recv, bufs, len(bufs), after)

    unshard =lambda g: jnp.transpose(g, (1, 0, 2)).reshape(g.shape[1], -1)
    meta_full, cw_full, sw_full = unshard(meta_g), unshard(cw_g), unshard(sw_g)

    zrows = jnp.zeros((t - t_real, d), F32) + started
    h0 = jnp.concatenate([meta_full, x[0], zrows], axis=0)
    tgt = jnp.concatenate([jnp.zeros((N_META, d), F32), loss_target[0], zrows], axis=0)
    n = _pre_norm(h0, g_pre_mix)
    fwd0, tok = forward_on(0, [n])
    win_g, = gathered(0, fwd0, [tok])
    proj = _mm_cols("proj", n, win_g, tm=tm)[0]
    fwd1, tok = forward_on(1, [proj])
    a1, s = _conv_forward(proj, cw_full, conf_dw_b, sw_full, dc, deps=[tok])
    a3 = _layer_norm_silu(a1, conf_ln_g, conf_ln_b)
    wpw_g, wso_g, wo_g = gathered(1, fwd1, [a3])
    wo_full = wo_g.reshape(d, d)
    ya = _mm_cols("y_a", a3, wpw_g, tm=tm, nb=N_DEV)[0]
    yb = _mm_cols("y_b", s, wso_g, tm=tm, nb=N_DEV)[0]
    m_mix = _gate_merge(proj, ya, yb, b_gates, d)
    mix = _mm_rows("mix", m_mix, wo_full, tm=tm // 2, tn=d)
    fwd2, tok = forward_on(2, [mix])
    h1, n2 = _post_mix(mix, h0, g_post_mix, g_pre_mlp, deps=[tok])
    wup_g, = gathered(2, fwd2, [n2])

    def up_epilogue(acc):
        r = jnp.maximum(acc, 0.0)
        return r * r, r

    f, relu_up = _mm_cols("mlp_up", n2, wup_g, tm=tm, epilogue=up_epilogue, out_dtypes=(BF16, BF16))
    fwd3, tok = forward_on(3, [f])
    wdn_g, = gathered(3, fwd3, [tok])
    wdn_full = wdn_g.reshape(-1, d)
    fo = _mm_rows("mlp_down", f, wdn_full, tm=tm // 2, tn=512)
    dfo, dh2, dg_post_mlp, loss_blk = _loss_head(fo, h1, tgt, g_post_mlp, t_real)
    loss = lax.psum(loss_blk[0, 0], ("x", "y", "c"))

    def reduce_start(tag, fulls, deps):
        lands = [lax.empty((4,) + g.shape[1:], BF16) for g in fulls]
        send, recv, bufs, tok = _remote_start("reduce_%s_d2d_start" % tag, "reduce_d2d", fulls, lands, deps=deps)
        return (send, recv, bufs), tok

    def reduce_middle(tag, state, owns, after):
        send, recv, bufs = state
        k = len(owns)
        bufs = _remote_wait("reduce_%s_d2d_wait" % tag, "reduce_d2d", send, recv, bufs, k, after)
        sums = [_chip_sum("chip_sum_%s%d" % (tag, i), bufs[i], bufs[k + i], owns[i], me_arr) for i in range(k)]
        lands = [lax.empty(sm[0].shape, BF16) for sm in sums]
        send, recv, bufs, tok = _remote_start("reduce_%s_ici_start" % tag, "reduce_ici", [sm[0] for sm in sums], lands)
        return (send, recv, bufs, [sm[1] for sm in sums]), tok

    def reduce_finish(tag, state, after):
        send, recv, bufs, chip_sums = state
        k = len(chip_sums)
        bufs = _remote_wait("reduce_%s_ici_wait" % tag, "reduce_ici", send, recv, bufs, k, after)
        return list(zip(chip_sums, bufs[k:]))

    dup = _mm_nt_blocks("d_up", dfo, wdn_full, tm=tm, tkb=1024, extra=(relu_up,),
                        epilogue=lambda acc, r: (acc * (2.0 * r.astype(F32)),), out_dtypes=(BF16,))[0]
    gw_down, gw_down_own = _mm_tn("dw_down", f, dfo, me_arr, m=f.shape[1], n=d, tma=512, tn=1024, sharded="rows")
    red_down, tok = reduce_start("down", [gw_down], ())
    dn2 = _mm_nt_acc("d_n2", dup, wup_g, tm=tm // 2, tn=512, deps=[tok])
    gw_up, gw_up_own = _mm_tn("dw_up", n2, dup, me_arr, m=d, n=dup.shape[1], tma=512, tn=1024, sharded="cols")
    red_down, tok = reduce_middle("down", red_down, [gw_down_own], [dn2])
    red_up, tok = reduce_start("up", [gw_up], [tok])
    dh1, dmix, dg_pre_mlp, dg_post_mix = _mid_norm_bwd(dn2, h1, dh2, mix, g_pre_mlp, g_post_mix, deps=[tok])
    dya, dyb, dproj, db_a, db_b = _gate_backward(dmix, wo_full, proj, ya, yb, b_gates, d, tm // 2)
    db_gates = jnp.concatenate([db_a, db_b], axis=1)
    red_up, tok = reduce_middle("up", red_up, [gw_up_own], [dya])
    gw_o, gw_o_own = _mm_tn("dw_o", m_mix, dmix, me_arr, m=d, n=d, tma=d // N_DEV, tn=1024, sharded="rows", deps=[tok])
    ycb = d // N_DEV
    da3 = _mm_nt_acc("d_a3", dya, wpw_g, tm=tm, tn=512)
    gw_pw, gw_pw_own = _mm_tn("dw_pw", a3, dya, me_arr, m=dc, n=d, tma=512, tn=ycb, sharded="cols")
    dsb = _mm_nt_acc("d_s", dyb, wso_g, tm=tm, tn=512)
    gw_so, gw_so_own = _mm_tn("dw_so", s, dyb, me_arr, m=dc, n=d, tma=512, tn=ycb, sharded="cols")
    red_mix, tok = reduce_start("mix", [gw_pw, gw_so, gw_o], ())
    da1, dln_g, dln_b = _layer_norm_silu_bwd(da3, a1, conf_ln_g, conf_ln_b, deps=[tok])
    dproj, dcw, dcb, dsw = _conv_backward(dproj, proj, da1, dsb, cw_full, sw_full, dc)
    red_mix, tok = reduce_middle("mix", red_mix, [gw_pw_own, gw_so_own, gw_o_own], [dcb])
    in_cb = w_in.shape[2]
    half = d // 2
    red_in = []
    for part in range(2):
        gw, own = _mm_tn("dw_in%d" % part, n, dproj, me_arr, m=half, n=proj.shape[1], tma=512, tn=in_cb,
                         sharded="cols", a_off=part * (half // 512), deps=[tok])
        state, tok = reduce_start("in%d" % part, [gw], ())
        red_in.append((state, own))
    for part in range(2):
        state, own = red_in[part]
        red_in[part], tok = reduce_middle("in%d" % part, state, [own], [tok])
    dn = _mm_nt_acc("d_n", dproj, win_g, tm=tm // 2, tn=512, deps=[tok])
    dh0, dg_pre_mix = _pre_norm_bwd(dn, h0, dh1, g_pre_mix)
    grad_x = dh0[N_META:t_real][None]

    vec_parts = [dg_pre_mix, db_gates, dcb, dln_g, dln_b, dg_post_mix, dg_pre_mlp, dg_post_mlp]
    packed = _pack_small(vec_parts, dh0[:N_META], dcw, dsw)
    small_g = _all_gather("gather_small_grads", [packed])[0]
    vec_names = ["g_pre_mix", "b_gates", "conf_dw_b", "conf_ln_g", "conf_ln_b", "g_post_mix", "g_pre_mlp", "g_post_mlp"]
    env = locals()
    triple = lambda nm, sq: tuple(env[p + nm][0] if sq else env[p + nm] for p in ("", "m_", "v_"))
    small = _small_update(small_g, me_arr, [triple(nm, False) for nm in vec_names],
                          triple("meta", False), triple("conf_dw_w", True), triple("short_dw_w", True))
    results = {}
    for nm, res in zip(vec_names + ["meta"], small[:len(vec_names) + 1]):
        results[nm] = res
    results["conf_dw_w"] = tuple(r[None] for r in small[-2])
    results["short_dw_w"] = tuple(r[None] for r in small[-1])

    def update(nm, parts):
        res = _adamw_shard("adamw_" + nm, env[nm][0], env["m_" + nm][0], env["v_" + nm][0], parts, me_arr)
        results[nm] = tuple(r[None] for r in res)
        return res[0]

    done = [small[0][0]]
    done.append(update("w_down", reduce_finish("down", red_down, [small_g])))
    done.append(update("w_up", reduce_finish("up", red_up, [small_g])))
    for nm, pair in zip(["conf_w_pw", "short_w_out", "w_o"], reduce_finish("mix", red_mix, [small_g])):
        done.append(update(nm, [pair]))
    update("w_in", [reduce_finish("in%d" % part, red_in[part], done)[0] for part in range(2)])

    order = ["meta", "g_pre_mix", "w_in", "b_gates", "conf_dw_w", "conf_dw_b", "conf_ln_g", "conf_ln_b", "conf_w_pw",
             "short_dw_w", "short_w_out", "w_o", "g_post_mix", "g_pre_mlp", "w_up", "w_down", "g_post_mlp"]
    return (loss, grad_x, *[results[nm][0] for nm in order], *[results[nm][1] for nm in order],
            *[results[nm][2] for nm in order], *[results[nm][3] for nm in order])
```

```python
import jax
import jax.numpy as jnp
from jax import lax
from jax.experimental import pallas as pl
from jax.experimental.pallas import tpu as pltpu

N_DEV = 8
N_META = 16
CONF_K = 31
SHORT_K = 3
RMS_EPS = 1e-6
LN_EPS = 1e-5
ADAM_LR = 0.001
ADAM_B1 = 0.9
ADAM_B2 = 0.999
ADAM_EPS = 1e-08
ADAM_WD = 0.01
ADAM_STEP = 10

LANE = 128
SUB = 8
ROW_TILE = 128
CONV_PAD = 32
CONV_CHUNK = 128
VMEM_LIMIT = 56 * 1024 * 1024

F32 = jnp.float32
BF16 = jnp.bfloat16
MESH = pl.DeviceIdType.MESH
ANY = pl.BlockSpec(memory_space=pl.ANY)
HBM_SPEC = pl.BlockSpec(memory_space=pltpu.HBM)
SEM_SPEC = pl.BlockSpec(memory_space=pltpu.SEMAPHORE)
EFFECT = pltpu.SideEffectType.DATAFLOW_SIDE_EFFECTING


def _params(n_axes):
    return pltpu.CompilerParams(dimension_semantics=("arbitrary",) * n_axes, vmem_limit_bytes=VMEM_LIMIT)


def _sigmoid(z):
    return 1.0 / (1.0 + jnp.exp(-z))


def _colsum8(v):
    r, c = v.shape
    return jnp.sum(v.reshape(r // SUB, SUB, c), axis=0)


def _position():
    x, y, c = lax.axis_index("x"), lax.axis_index("y"), lax.axis_index("c")
    return x, y, c


def _flat(p):
    return 4 * p[0] + 2 * p[1] + p[2]


def _all_gather(name, shards, deps=()):
    n, nd = len(shards), len(deps)

    def body(*refs):
        ins, outs = refs[:n], refs[n + nd:2 * n + nd]
        send_sems, recv_sems, local_sems = refs[2 * n + nd:]
        x, y, c = _position()
        me, sibling = (x, y, c), (x, y, 1 - c)
        chips = [(1 - x, y), (x, 1 - y), (1 - x, 1 - y)]

        def copy(q, k, block, to, src=None):
            dst = outs[q].at[_flat(block)]
            return pltpu.make_async_remote_copy(
                src_ref=dst if src is None else src, dst_ref=dst,
                send_sem=send_sems.at[q, k], recv_sem=recv_sems.at[q, k],
                device_id=to, device_id_type=MESH)

        mine = [pltpu.make_async_copy(ins[q], outs[q].at[_flat(me)], local_sems.at[q]) for q in range(n)]
        for cp in mine:
            cp.start()
        first = []
        for q in range(n):
            first.append(copy(q, 0, me, sibling, src=ins[q]))
            for j, chip in enumerate(chips):
                first.append(copy(q, 1 + j, me, (*chip, c), src=ins[q]))
        for cp in first:
            cp.start()
        passed = []
        for q in range(n):
            for j, chip in enumerate(chips):
                copy(q, 1 + j, (*chip, c), me).wait_recv()
                fwd = copy(q, 4 + j, (*chip, c), sibling)
                fwd.start()
                passed.append(fwd)
        for q in range(n):
            copy(q, 0, sibling, me).wait_recv()
            for j, chip in enumerate(chips):
                copy(q, 4 + j, (*chip, 1 - c), me).wait_recv()
        for cp in first + passed:
            cp.wait_send()
        for cp in mine:
            cp.wait()

    return pl.pallas_call(
        body, name=name,
        in_specs=[ANY] * (n + nd), out_specs=[ANY] * n,
        out_shape=[jax.ShapeDtypeStruct((N_DEV,) + s.shape, s.dtype) for s in shards],
        scratch_shapes=[pltpu.SemaphoreType.DMA((n, 7)), pltpu.SemaphoreType.DMA((n, 7)),
                        pltpu.SemaphoreType.DMA((n,))],
    )(*shards, *deps)


N_COPIES = {"gather_ici": 4, "gather_d2d": 3, "reduce_d2d": 4, "reduce_ici": 3}


def _copy_plan(kind):
    x, y, c = _position()
    me, sibling = (x, y, c), (x, y, 1 - c)
    chips = [(1 - x, y), (x, 1 - y), (1 - x, 1 - y)]
    if kind == "gather_ici":
        return [(_flat(me), _flat(me), sibling)] + [(_flat(me), _flat(me), (*ch, c)) for ch in chips]
    if kind == "gather_d2d":
        return [(_flat((*ch, c)), _flat((*ch, c)), sibling) for ch in chips]
    if kind == "reduce_d2d":
        return [(2 * chip + (1 - c), chip, sibling) for chip in range(4)]
    return [(2 * ch[0] + ch[1], 2 * x + y, (*ch, c)) for ch in chips]


def _planned_copies(kind, srcs, dsts, send_sems, recv_sems):
    plan = _copy_plan(kind)
    return [pltpu.make_async_remote_copy(
        src_ref=src.at[s_slot], dst_ref=dst.at[d_slot],
        send_sem=send_sems.at[q * len(plan) + k], recv_sem=recv_sems.at[q * len(plan) + k],
        device_id=to, device_id_type=MESH)
        for q, (src, dst) in enumerate(zip(srcs, dsts)) for k, (s_slot, d_slot, to) in enumerate(plan)]


def _remote_start(name, kind, srcs, lands=None, deps=()):
    n = len(srcs)
    bufs = list(srcs) + ([] if lands is None else list(lands))
    nb, nd = len(bufs), len(deps)
    nsem = n * N_COPIES[kind]

    def body(*refs):
        ins = refs[:nb]
        send_sems, recv_sems = refs[nb + nd], refs[nb + nd + 1]
        token = refs[-1]
        for cp in _planned_copies(kind, ins[:n], ins[:n] if lands is None else ins[n:], send_sems, recv_sems):
            cp.start()
        token[...] = jnp.zeros_like(token)

    outs = pl.pallas_call(
        body, name=name,
        out_shape=(pltpu.SemaphoreType.DMA((nsem,)), pltpu.SemaphoreType.DMA((nsem,)),
                   *[pltpu.HBM(b.shape, b.dtype) for b in bufs], jax.ShapeDtypeStruct((SUB, LANE), F32)),
        in_specs=[HBM_SPEC] * nb + [ANY] * nd,
        out_specs=(SEM_SPEC, SEM_SPEC, *[HBM_SPEC] * nb, pl.BlockSpec(memory_space=pltpu.VMEM)),
        input_output_aliases={i: 2 + i for i in range(nb)},
        compiler_params=pltpu.CompilerParams(has_side_effects=EFFECT),
    )(*[pltpu.with_memory_space_constraint(b, pltpu.HBM) for b in bufs], *deps)
    return outs[0], outs[1], list(outs[2:2 + nb]), outs[-1]


def _remote_wait(name, kind, send_sems, recv_sems, bufs, n, after):
    nb, na = len(bufs), len(after)
    same = nb == n

    def body(*refs):
        ins = refs[:nb]
        sends, recvs = refs[nb], refs[nb + 1]
        for cp in _planned_copies(kind, ins[:n], ins[:n] if same else ins[n:], sends, recvs):
            cp.wait_send()
            cp.wait_recv()

    outs = pl.pallas_call(
        body, name=name,
        out_shape=[pltpu.HBM(b.shape, b.dtype) for b in bufs],
        in_specs=[HBM_SPEC] * nb + [SEM_SPEC, SEM_SPEC] + [ANY] * na,
        out_specs=[HBM_SPEC] * nb,
        input_output_aliases={i: i for i in range(nb)},
        compiler_params=pltpu.CompilerParams(has_side_effects=EFFECT),
    )(*bufs, send_sems, recv_sems, *after)
    return list(outs)


def _mm_cols(name, a, w, *, tm, nb=1, epilogue=None, out_dtypes=(F32,)):
    t, k = a.shape
    nblk, _, cb = w.shape

    def body(a_ref, w_ref, *o_refs):
        av = a_ref[...]
        for b in range(nb):
            acc = jnp.dot(av, w_ref[b], preferred_element_type=F32)
            outs = (acc,) if epilogue is None else epilogue(acc)
            for o_ref, o in zip(o_refs, outs):
                o_ref[:, b * cb:(b + 1) * cb] = o.astype(o_ref.dtype)

    return pl.pallas_call(
        body, name=name, grid=(nblk // nb, t // tm),
        in_specs=[pl.BlockSpec((tm, k), lambda j, i: (i, 0)),
                  pl.BlockSpec((nb, k, cb), lambda j, i: (j, 0, 0))],
        out_specs=[pl.BlockSpec((tm, nb * cb), lambda j, i: (i, j)) for _ in out_dtypes],
        out_shape=[jax.ShapeDtypeStruct((t, nblk * cb), dt) for dt in out_dtypes],
        compiler_params=_params(2),
    )(a, w)


def _mm_rows(name, a, w2d, *, tm, tn):
    t, kf = a.shape
    n = w2d.shape[1]

    def body(a_ref, w_ref, o_ref):
        o_ref[...] = jnp.dot(a_ref[...], w_ref[...], preferred_element_type=F32)

    return pl.pallas_call(
        body, name=name, grid=(t // tm, n // tn),
        in_specs=[pl.BlockSpec((tm, kf), lambda i, j: (i, 0)),
                  pl.BlockSpec((kf, tn), lambda i, j: (0, j))],
        out_specs=pl.BlockSpec((tm, tn), lambda i, j: (i, j)),
        out_shape=jax.ShapeDtypeStruct((t, n), F32),
        compiler_params=_params(2),
    )(a, w2d)


def _mm_nt_acc(name, dy, w, *, tm, tn, col_off=0, deps=()):
    t = dy.shape[0]
    nblk, k, cb = w.shape

    def body(dy_ref, w_ref, *rest):
        acc = None
        for b in range(nblk):
            d = lax.dot_general(dy_ref[:, b * cb:(b + 1) * cb], w_ref[b], (((1,), (1,)), ((), ())),
                                preferred_element_type=F32)
            acc = d if acc is None else acc + d
        rest[-1][...] = acc

    return pl.pallas_call(
        body, name=name, grid=(t // tm, k // tn),
        in_specs=[pl.BlockSpec((tm, nblk * cb), lambda i, j: (i, col_off)),
                  pl.BlockSpec((nblk, tn, cb), lambda i, j: (0, j, 0))] + [ANY] * len(deps),
        out_specs=pl.BlockSpec((tm, tn), lambda i, j: (i, j)),
        out_shape=jax.ShapeDtypeStruct((t, k), F32),
        compiler_params=_params(2),
    )(dy, w, *deps)


def _mm_nt_blocks(name, dy, w2d, *, tm, tkb, extra=(), epilogue=None, out_dtypes=(F32,)):
    t, n = dy.shape
    kf = w2d.shape[0]
    ne = len(extra)

    def body(dy_ref, w_ref, *rest):
        acc = lax.dot_general(dy_ref[...], w_ref[...], (((1,), (1,)), ((), ())), preferred_element_type=F32)
        outs = (acc,) if epilogue is None else epilogue(acc, *[e[...] for e in rest[:ne]])
        for o_ref, o in zip(rest[ne:], outs):
            o_ref[...] = o.astype(o_ref.dtype)

    return pl.pallas_call(
        body, name=name, grid=(kf // tkb, t // tm),
        in_specs=[pl.BlockSpec((tm, n), lambda kb, i: (i, 0)),
                  pl.BlockSpec((tkb, n), lambda kb, i: (kb, 0))]
                 + [pl.BlockSpec((tm, tkb), lambda kb, i: (i, kb)) for _ in extra],
        out_specs=[pl.BlockSpec((tm, tkb), lambda kb, i: (i, kb)) for _ in out_dtypes],
        out_shape=[jax.ShapeDtypeStruct((t, kf), dt) for dt in out_dtypes],
        compiler_params=_params(2),
    )(dy, w2d, *extra)


def _mm_tn(name, a, b, me_arr, *, m, n, tma, tn, sharded, a_off=0, b_off=0, deps=()):
    t = a.shape[0]
    if sharded == "cols":
        cb = n // N_DEV
        nb, q = max(tn // cb, 1), max(cb // tn, 1)
        tw = tn // nb
        full_shape, own_shape = (N_DEV, m, cb), (m, cb)
        full_spec = pl.BlockSpec((nb, tma, tw), lambda i, j, me: (j // q, i, j % q))
    else:
        kb = m // N_DEV
        p = kb // tma
        nb, tw = 1, tn
        full_shape, own_shape = (m, n), (kb, n)
        full_spec = pl.BlockSpec((tma, tn), lambda i, j, me: (i, j))

    def body(me_ref, a_ref, b_ref, *rest):
        full_ref, own_ref, stage, sem = rest[len(deps):]
        i, j = pl.program_id(0), pl.program_id(1)
        acc = lax.dot_general(a_ref[...], b_ref[...], (((0,), (0,)), ((), ())), preferred_element_type=F32)
        for blk in range(nb):
            part = acc[:, blk * tw:(blk + 1) * tw]
            if sharded == "cols":
                full_ref[blk] = part.astype(BF16)
                owner, r0, c0 = (j // q) * nb + blk, i * tma, (j % q) * tw
            else:
                full_ref[...] = part.astype(BF16)
                owner, r0, c0 = i // p, (i % p) * tma, j * tn

            @pl.when(owner == me_ref[0])
            def _():
                stage[...] = part
                cp = pltpu.make_async_copy(
                    stage, own_ref.at[pl.ds(pl.multiple_of(r0, tma), tma), pl.ds(pl.multiple_of(c0, tw), tw)], sem)
                cp.start()
                cp.wait()

    full, own = pl.pallas_call(
        body, name=name,
        grid_spec=pltpu.PrefetchScalarGridSpec(
            num_scalar_prefetch=1, grid=(m // tma, n // tn),
            in_specs=[pl.BlockSpec((t, tma), lambda i, j, me: (0, a_off + i)),
                      pl.BlockSpec((t, tn), lambda i, j, me: (0, b_off + j))] + [ANY] * len(deps),
            out_specs=[full_spec, ANY],
            scratch_shapes=[pltpu.VMEM((tma, tw), F32), pltpu.SemaphoreType.DMA(())]),
        out_shape=[jax.ShapeDtypeStruct(full_shape, BF16), jax.ShapeDtypeStruct(own_shape, F32)],
        compiler_params=_params(2),
    )(me_arr, a, b, *deps)
    if sharded == "rows":
        full = full.reshape(N_DEV, m // N_DEV, n)
    return full, own


def _row_call(name, body, t, row_ins, full_ins, row_outs, acc_outs, scratch=(), deps=()):
    tm = ROW_TILE
    nin = len(row_ins) + len(full_ins)

    def without_deps(*refs):
        body(*refs[:nin], *refs[nin + len(deps):])

    return pl.pallas_call(
        without_deps, name=name, grid=(t // tm,),
        in_specs=[pl.BlockSpec((tm, a.shape[1]), lambda i: (i, 0)) for a in row_ins]
                 + [pl.BlockSpec(a.shape, lambda i: (0, 0)) for a in full_ins] + [ANY] * len(deps),
        out_specs=[pl.BlockSpec((tm, c), lambda i: (i, 0)) for c, _ in row_outs]
                  + [pl.BlockSpec((r, c), lambda i: (0, 0)) for r, c in acc_outs],
        out_shape=[jax.ShapeDtypeStruct((t, c), dt) for c, dt in row_outs]
                  + [jax.ShapeDtypeStruct((r, c), F32) for r, c in acc_outs],
        scratch_shapes=list(scratch),
        compiler_params=_params(1),
    )(*row_ins, *full_ins, *deps)


def _accumulate(ref, v):
    @pl.when(pl.program_id(0) == 0)
    def _():
        ref[...] = v

    @pl.when(pl.program_id(0) > 0)
    def _():
        ref[...] += v


def _rms(v):
    return lax.rsqrt(jnp.mean(v * v, axis=-1, keepdims=True) + RMS_EPS)


def _rms_bwd(dout, u, r, g):
    du = dout * g
    dx = r * (du - u * jnp.mean(du * u, axis=-1, keepdims=True))
    return dx, _colsum8(dout * u)


def _pre_norm(h0, g):
    t, d = h0.shape

    def body(h_ref, g_ref, n_ref):
        h = h_ref[...]
        n_ref[...] = (h * _rms(h) * g_ref[...]).astype(BF16)

    return _row_call("pre_norm", body, t, [h0], [g], [(d, BF16)], [])[0]


def _post_mix(mix, h0, g_post, g_pre, deps=()):
    t, d = h0.shape

    def body(mix_ref, h0_ref, gp_ref, gq_ref, h1_ref, n2_ref):
        mix_v = mix_ref[...]
        h1 = h0_ref[...] + mix_v * _rms(mix_v) * gp_ref[...]
        h1_ref[...] = h1
        n2_ref[...] = (h1 * _rms(h1) * gq_ref[...]).astype(BF16)

    return _row_call("post_mix", body, t, [mix, h0], [g_post, g_pre], [(d, F32), (d, BF16)], [], deps=deps)


def _loss_head(fo, h1, tgt, g_post_mlp, t_real):
    t, d = h1.shape

    def body(fo_ref, h1_ref, tgt_ref, g_ref, dfo_ref, dh2_ref, dg_ref, loss_ref, lacc):
        i = pl.program_id(0)
        fo_v = fo_ref[...]
        g = g_ref[...]
        r = _rms(fo_v)
        u = fo_v * r
        h2 = h1_ref[...] + u * g
        row = i * ROW_TILE + lax.broadcasted_iota(jnp.int32, (ROW_TILE, 1), 0)
        valid = jnp.logical_and(row >= N_META, row < t_real)
        diff = jnp.where(valid, h2 - tgt_ref[...], 0.0)
        dh2 = diff * (1.0 / d)
        dh2_ref[...] = dh2
        dfo, dg = _rms_bwd(dh2, u, r, g)
        dfo_ref[...] = dfo.astype(BF16)
        _accumulate(dg_ref, dg)
        _accumulate(lacc, _colsum8(diff * diff))

        @pl.when(i == pl.num_programs(0) - 1)
        def _():
            loss_ref[...] = jnp.full((SUB, LANE), (0.5 / d) * jnp.sum(lacc[...]), F32)

    return _row_call("loss_head", body, t, [fo, h1, tgt], [g_post_mlp],
                     [(d, BF16), (d, F32)], [(SUB, d), (SUB, LANE)], scratch=[pltpu.VMEM((SUB, d), F32)])


def _mid_norm_bwd(dn2, h1, dh2, mix, g_pre_mlp, g_post_mix, deps=()):
    t, d = h1.shape

    def body(dn2_ref, h1_ref, dh2_ref, mix_ref, gq_ref, gp_ref, dh1_ref, dmix_ref, dgq_ref, dgp_ref):
        h1 = h1_ref[...]
        r3 = _rms(h1)
        dx, dgq = _rms_bwd(dn2_ref[...], h1 * r3, r3, gq_ref[...])
        dh1 = dh2_ref[...] + dx
        dh1_ref[...] = dh1
        mix_v = mix_ref[...]
        r2 = _rms(mix_v)
        dmix, dgp = _rms_bwd(dh1, mix_v * r2, r2, gp_ref[...])
        dmix_ref[...] = dmix.astype(BF16)
        _accumulate(dgq_ref, dgq)
        _accumulate(dgp_ref, dgp)

    return _row_call("mid_norm_bwd", body, t, [dn2, h1, dh2, mix], [g_pre_mlp, g_post_mix],
                     [(d, F32), (d, BF16)], [(SUB, d), (SUB, d)], deps=deps)


def _pre_norm_bwd(dn, h0, dh1, g_pre_mix, deps=()):
    t, d = h0.shape

    def body(dn_ref, h0_ref, dh1_ref, g_ref, dh0_ref, dg_ref):
        h0 = h0_ref[...]
        r = _rms(h0)
        dx, dg = _rms_bwd(dn_ref[...], h0 * r, r, g_ref[...])
        dh0_ref[...] = dh1_ref[...] + dx
        _accumulate(dg_ref, dg)

    return _row_call("pre_norm_bwd", body, t, [dn, h0, dh1], [g_pre_mix], [(d, F32)], [(SUB, d)], deps=deps)


def _layer_norm_silu(a1, ln_g, ln_b):
    t, c = a1.shape

    def body(a1_ref, g_ref, b_ref, a3_ref):
        a = a1_ref[...]
        mu = jnp.mean(a, axis=-1, keepdims=True)
        xc = a - mu
        rstd = lax.rsqrt(jnp.mean(xc * xc, axis=-1, keepdims=True) + LN_EPS)
        z = xc * rstd * g_ref[...] + b_ref[...]
        a3_ref[...] = (z * _sigmoid(z)).astype(BF16)

    return _row_call("layer_norm_silu", body, t, [a1], [ln_g, ln_b], [(c, BF16)], [])[0]


def _layer_norm_silu_bwd(da3, a1, ln_g, ln_b, deps=()):
    t, c = a1.shape

    def body(da3_ref, a1_ref, g_ref, b_ref, da1_ref, dg_ref, db_ref):
        a = a1_ref[...]
        g = g_ref[...]
        mu = jnp.mean(a, axis=-1, keepdims=True)
        xc = a - mu
        rstd = lax.rsqrt(jnp.mean(xc * xc, axis=-1, keepdims=True) + LN_EPS)
        xhat = xc * rstd
        z = xhat * g + b_ref[...]
        sg = _sigmoid(z)
        dz = da3_ref[...] * (sg * (1.0 + z * (1.0 - sg)))
        dxhat = dz * g
        da1_ref[...] = rstd * (dxhat - jnp.mean(dxhat, axis=-1, keepdims=True)
                               - xhat * jnp.mean(dxhat * xhat, axis=-1, keepdims=True))
        _accumulate(dg_ref, _colsum8(dz * xhat))
        _accumulate(db_ref, _colsum8(dz))

    return _row_call("layer_norm_silu_bwd", body, t, [da3, a1], [ln_g, ln_b], [(c, F32)], [(SUB, c), (SUB, c)], deps=deps)


def _gate_merge(proj, ya, yb, b_gates, d, deps=()):
    t = proj.shape[0]
    w = 1024
    nh = d // w
    ga0 = (proj.shape[1] - 2 * d) // w

    def body(pa_ref, pb_ref, ya_ref, yb_ref, ba_ref, bb_ref, *rest):
        m_ref = rest[-1]
        ga = _sigmoid(pa_ref[...] + ba_ref[...])
        gb = _sigmoid(pb_ref[...] + bb_ref[...])
        m_ref[...] = (ga * ya_ref[...] + gb * yb_ref[...]).astype(BF16)

    tm = ROW_TILE
    return pl.pallas_call(
        body, name="gate_merge", grid=(nh, t // tm),
        in_specs=[pl.BlockSpec((tm, w), lambda h, i: (i, ga0 + h)),
                  pl.BlockSpec((tm, w), lambda h, i: (i, ga0 + nh + h)),
                  pl.BlockSpec((tm, w), lambda h, i: (i, h)),
                  pl.BlockSpec((tm, w), lambda h, i: (i, h)),
                  pl.BlockSpec((1, w), lambda h, i: (0, h)),
                  pl.BlockSpec((1, w), lambda h, i: (0, nh + h))] + [ANY] * len(deps),
        out_specs=pl.BlockSpec((tm, w), lambda h, i: (i, h)),
        out_shape=jax.ShapeDtypeStruct((t, d), BF16),
        compiler_params=_params(2),
    )(proj, proj, ya, yb, b_gates, b_gates, *deps)


def _gate_backward(dmix, wo_full, proj, ya, yb, b_gates, d, tm, deps=()):
    t, cols = proj.shape
    w = 1024
    nh = d // w
    ga0 = (cols - 2 * d) // w

    def body(dmix_ref, wo_ref, pa_ref, pb_ref, ya_ref, yb_ref, ba_ref, bb_ref, *rest):
        dya_ref, dyb_ref, dp_ref, dba_ref, dbb_ref, stage, sems = rest[len(deps):]
        h, i = pl.program_id(0), pl.program_id(1)
        dm = lax.dot_general(dmix_ref[...], wo_ref[...], (((1,), (1,)), ((), ())), preferred_element_type=F32)
        ga = _sigmoid(pa_ref[...] + ba_ref[...])
        gb = _sigmoid(pb_ref[...] + bb_ref[...])
        dya_ref[...] = (dm * ga).astype(BF16)
        dyb_ref[...] = (dm * gb).astype(BF16)
        dpa = dm * ya_ref[...] * ga * (1.0 - ga)
        dpb = dm * yb_ref[...] * gb * (1.0 - gb)
        stage[0] = dpa.astype(BF16)
        stage[1] = dpb.astype(BF16)
        rows = pl.ds(pl.multiple_of(i * tm, tm), tm)
        copies = [pltpu.make_async_copy(
            stage.at[g], dp_ref.at[rows, pl.ds(pl.multiple_of((ga0 + g * nh + h) * w, w), w)], sems.at[g])
            for g in range(2)]
        for cp in copies:
            cp.start()

        @pl.when(i == 0)
        def _():
            dba_ref[...] = _colsum8(dpa)
            dbb_ref[...] = _colsum8(dpb)

        @pl.when(i > 0)
        def _():
            dba_ref[...] += _colsum8(dpa)
            dbb_ref[...] += _colsum8(dpb)

        for cp in copies:
            cp.wait()

    tile = pl.BlockSpec((tm, w), lambda h, i: (i, h))
    return pl.pallas_call(
        body, name="gate_backward", grid=(nh, t // tm),
        in_specs=[pl.BlockSpec((tm, d), lambda h, i: (i, 0)),
                  pl.BlockSpec((w, d), lambda h, i: (h, 0)),
                  pl.BlockSpec((tm, w), lambda h, i: (i, ga0 + h)),
                  pl.BlockSpec((tm, w), lambda h, i: (i, ga0 + nh + h)),
                  tile, tile,
                  pl.BlockSpec((1, w), lambda h, i: (0, h)),
                  pl.BlockSpec((1, w), lambda h, i: (0, nh + h))] + [ANY] * len(deps),
        out_specs=[tile, tile, ANY,
                   pl.BlockSpec((SUB, w), lambda h, i: (0, h)),
                   pl.BlockSpec((SUB, w), lambda h, i: (0, h))],
        out_shape=[jax.ShapeDtypeStruct((t, d), BF16), jax.ShapeDtypeStruct((t, d), BF16),
                   jax.ShapeDtypeStruct((t, cols), BF16),
                   jax.ShapeDtypeStruct((SUB, d), F32), jax.ShapeDtypeStruct((SUB, d), F32)],
        scratch_shapes=[pltpu.VMEM((2, tm, w), BF16), pltpu.SemaphoreType.DMA((2,))],
        compiler_params=_params(2),
    )(dmix, wo_full, proj, proj, ya, yb, b_gates, b_gates, *deps)


def _causal_conv(xp_ref, w_ref, ntap, r0):
    n = CONV_CHUNK + CONV_PAD
    win = xp_ref[pl.ds(r0, n), :]
    acc = None
    for k in range(ntap):
        back = ntap - 1 - k
        shifted = pltpu.roll(win, n - (CONV_PAD - back), 0)
        term = w_ref[k:k + 1, :] * shifted[:CONV_CHUNK]
        acc = term if acc is None else acc + term
    return acc


def _anticausal_conv(xp_ref, w_ref, ntap, r0):
    n = CONV_CHUNK + CONV_PAD
    win = xp_ref[pl.ds(pl.multiple_of(CONV_PAD + r0, CONV_PAD), n), :]
    acc = None
    for k in range(ntap):
        ahead = ntap - 1 - k
        shifted = win if ahead == 0 else pltpu.roll(win, n - ahead, 0)
        term = w_ref[k:k + 1, :] * shifted[:CONV_CHUNK]
        acc = term if acc is None else acc + term
    return acc


def _conv_weight_grad(dw_ref, d_chunk, xp_ref, ntap, r0):
    n = CONV_CHUNK + CONV_PAD
    win = xp_ref[pl.ds(r0, n), :]
    for k in range(ntap):
        back = ntap - 1 - k
        shifted = pltpu.roll(win, n - (CONV_PAD - back), 0)
        dw_ref[k * SUB:(k + 1) * SUB, :] += _colsum8(d_chunk * shifted[:CONV_CHUNK])


def _zero_pads(ref, t):
    ref[0:CONV_PAD, :] = jnp.zeros((CONV_PAD, LANE), F32)
    ref[CONV_PAD + t:CONV_PAD + t + CONV_PAD, :] = jnp.zeros((CONV_PAD, LANE), F32)


def _for_chunks(t, fn):
    def step(idx, carry):
        fn(pl.multiple_of(idx * CONV_CHUNK, CONV_CHUNK))
        return carry

    lax.fori_loop(0, t // CONV_CHUNK, step, 0)


def _conv_forward(proj, conf_w, conf_b, short_w, dc, deps=()):
    t = proj.shape[0]
    nc = dc // LANE

    def body(av_ref, ag_ref, bg_ref, cg_ref, v_ref, cw_ref, cb_ref, sw_ref, *rest):
        a1_ref, s_ref, xa, xb = rest[len(deps):]
        _zero_pads(xa, t)
        _zero_pads(xb, t)
        xa[CONV_PAD:CONV_PAD + t, :] = av_ref[...] * _sigmoid(ag_ref[...])
        xb[CONV_PAD:CONV_PAD + t, :] = cg_ref[...] * v_ref[...]

        def chunk(r0):
            rs = pl.ds(r0, CONV_CHUNK)
            a1_ref[rs, :] = _causal_conv(xa, cw_ref, CONF_K, r0) + cb_ref[...]
            s_ref[rs, :] = (bg_ref[rs, :] * _causal_conv(xb, sw_ref, SHORT_K, r0)).astype(BF16)

        _for_chunks(t, chunk)

    col = lambda g: pl.BlockSpec((t, LANE), lambda c, g=g: (0, g * nc + c))
    return pl.pallas_call(
        body, name="conv_forward", grid=(nc,),
        in_specs=[col(0), col(1), col(2), col(3), col(4),
                  pl.BlockSpec((CONF_K, LANE), lambda c: (0, c)),
                  pl.BlockSpec((1, LANE), lambda c: (0, c)),
                  pl.BlockSpec((SHORT_K, LANE), lambda c: (0, c))] + [ANY] * len(deps),
        out_specs=[pl.BlockSpec((t, LANE), lambda c: (0, c)), pl.BlockSpec((t, LANE), lambda c: (0, c))],
        out_shape=[jax.ShapeDtypeStruct((t, dc), F32), jax.ShapeDtypeStruct((t, dc), BF16)],
        scratch_shapes=[pltpu.VMEM((t + 2 * CONV_PAD, LANE), F32), pltpu.VMEM((t + 2 * CONV_PAD, LANE), F32)],
        compiler_params=_params(1),
    )(proj, proj, proj, proj, proj, conf_w, conf_b, short_w, *deps)


def _conv_backward(dproj, proj, da1, ds, conf_w, short_w, dc):
    t = proj.shape[0]
    nc = dc // LANE

    def body(dp_in, av_ref, ag_ref, bg_ref, cg_ref, v_ref, da1_ref, ds_ref, cw_ref, sw_ref,
             dp_ref, dcw_ref, dcb_ref, dsw_ref, xa, xb, da, db, stage, sems):
        del dp_in
        c = pl.program_id(0)
        for ref in (xa, xb, da, db):
            _zero_pads(ref, t)
        xa[CONV_PAD:CONV_PAD + t, :] = av_ref[...] * _sigmoid(ag_ref[...])
        xb[CONV_PAD:CONV_PAD + t, :] = cg_ref[...] * v_ref[...]
        da[CONV_PAD:CONV_PAD + t, :] = da1_ref[...]
        dcw_ref[...] = jnp.zeros(dcw_ref.shape, F32)
        dsw_ref[...] = jnp.zeros(dsw_ref.shape, F32)
        dcb_ref[...] = jnp.zeros(dcb_ref.shape, F32)

        def through_gate(r0):
            rs = pl.ds(r0, CONV_CHUNK)
            ds_c = ds_ref[rs, :]
            stage[2, rs, :] = (ds_c * _causal_conv(xb, sw_ref, SHORT_K, r0)).astype(BF16)
            db[pl.ds(pl.multiple_of(CONV_PAD + r0, CONV_PAD), CONV_CHUNK), :] = ds_c * bg_ref[rs, :]

        _for_chunks(t, through_gate)

        def through_convs(r0):
            rs = pl.ds(r0, CONV_CHUNK)
            da0 = _anticausal_conv(da, cw_ref, CONF_K, r0)
            sg = _sigmoid(ag_ref[rs, :])
            stage[0, rs, :] = (da0 * sg).astype(BF16)
            stage[1, rs, :] = (da0 * av_ref[rs, :] * sg * (1.0 - sg)).astype(BF16)
            dcv = _anticausal_conv(db, sw_ref, SHORT_K, r0)
            stage[3, rs, :] = (dcv * v_ref[rs, :]).astype(BF16)
            stage[4, rs, :] = (dcv * cg_ref[rs, :]).astype(BF16)
            da1_c = da1_ref[rs, :]
            _conv_weight_grad(dcw_ref, da1_c, xa, CONF_K, r0)
            _conv_weight_grad(dsw_ref, ds_ref[rs, :] * bg_ref[rs, :], xb, SHORT_K, r0)
            dcb_ref[...] += _colsum8(da1_c)

        _for_chunks(t, through_convs)
        copies = [pltpu.make_async_copy(
            stage.at[g], dp_ref.at[:, pl.ds(pl.multiple_of((g * nc + c) * LANE, LANE), LANE)], sems.at[g])
            for g in range(5)]
        for cp in copies:
            cp.start()
        for cp in copies:
            cp.wait()

    col = lambda g: pl.BlockSpec((t, LANE), lambda c, g=g: (0, g * nc + c))
    blk = pl.BlockSpec((t, LANE), lambda c: (0, c))
    return pl.pallas_call(
        body, name="conv_backward", grid=(nc,),
        in_specs=[ANY, col(0), col(1), col(2), col(3), col(4), blk, blk,
                  pl.BlockSpec((CONF_K, LANE), lambda c: (0, c)),
                  pl.BlockSpec((SHORT_K, LANE), lambda c: (0, c))],
        out_specs=[ANY,
                   pl.BlockSpec((CONF_K * SUB, LANE), lambda c: (0, c)),
                   pl.BlockSpec((SUB, LANE), lambda c: (0, c)),
                   pl.BlockSpec((SHORT_K * SUB, LANE), lambda c: (0, c))],
        out_shape=[jax.ShapeDtypeStruct(dproj.shape, dproj.dtype),
                   jax.ShapeDtypeStruct((CONF_K * SUB, dc), F32),
                   jax.ShapeDtypeStruct((SUB, dc), F32),
                   jax.ShapeDtypeStruct((SHORT_K * SUB, dc), F32)],
        scratch_shapes=[pltpu.VMEM((t + 2 * CONV_PAD, LANE), F32)] * 4
                       + [pltpu.VMEM((5, t, LANE), BF16), pltpu.SemaphoreType.DMA((5,))],
        input_output_aliases={0: 0},
        compiler_params=_params(1),
    )(dproj, proj, proj, proj, proj, proj, da1, ds, conf_w, short_w)


def _adamw_math(w, g, m, v):
    m = ADAM_B1 * m + (1.0 - ADAM_B1) * g
    v = ADAM_B2 * v + (1.0 - ADAM_B2) * (g * g)
    m_hat = m / (1.0 - ADAM_B1 ** ADAM_STEP)
    v_hat = v / (1.0 - ADAM_B2 ** ADAM_STEP)
    delta = -ADAM_LR * (m_hat / (jnp.sqrt(v_hat) + ADAM_EPS) + ADAM_WD * w)
    return delta, m, v


def _cast_into_slot(name, w, me_arr, deps=()):
    r, c = w.shape
    tr = 256

    def body(me_ref, w_ref, *rest):
        del me_ref
        rest[-1][0] = w_ref[...].astype(BF16)

    return pl.pallas_call(
        body, name=name,
        grid_spec=pltpu.PrefetchScalarGridSpec(
            num_scalar_prefetch=1, grid=(r // tr,),
            in_specs=[pl.BlockSpec((tr, c), lambda i, me: (i, 0))] + [ANY] * len(deps),
            out_specs=pl.BlockSpec((1, tr, c), lambda i, me: (me[0], i, 0))),
        out_shape=jax.ShapeDtypeStruct((N_DEV, r, c), BF16),
        compiler_params=_params(1),
    )(me_arr, w, *deps)


def _chip_sum(name, full, from_sibling, own, me_arr):
    _, r, c = full.shape
    tr = min(r, 512)

    def body(me_ref, full_ref, sib_ref, own_ref, sums_ref, mine_ref):
        theirs = sib_ref[0].astype(F32)
        sums_ref[0] = (full_ref[0].astype(F32) + theirs).astype(BF16)

        @pl.when(pl.program_id(1) == me_ref[0] // 2)
        def _():
            mine_ref[...] = own_ref[...] + theirs

    return pl.pallas_call(
        body, name=name,
        grid_spec=pltpu.PrefetchScalarGridSpec(
            num_scalar_prefetch=1, grid=(r // tr, 4),
            in_specs=[pl.BlockSpec((1, tr, c), lambda i, chip, me: (2 * chip + me[0] % 2, i, 0)),
                      pl.BlockSpec((1, tr, c), lambda i, chip, me: (chip, i, 0)),
                      pl.BlockSpec((tr, c), lambda i, chip, me: (i, 0))],
            out_specs=[pl.BlockSpec((1, tr, c), lambda i, chip, me: (chip, i, 0)),
                       pl.BlockSpec((tr, c), lambda i, chip, me: (i, 0))]),
        out_shape=[jax.ShapeDtypeStruct((4, r, c), BF16), jax.ShapeDtypeStruct((r, c), F32)],
        compiler_params=_params(2),
    )(me_arr, full, from_sibling, own)


def _adamw_shard(name, w, m, v, parts, me_arr, deps=()):
    r, c = w.shape
    tr = 128
    np_ = len(parts)
    per = r // np_ // tr

    def body(me_ref, w_ref, m_ref, v_ref, *rest):
        g_out, d_out, m_out, v_out = rest[4 * np_ + len(deps):]
        g = None
        for p in range(np_):
            gp = rest[4 * p][...]
            for l_ref in rest[4 * p + 1:4 * p + 4]:
                gp = gp + l_ref[0].astype(F32)
            g = gp if g is None else jnp.where(pl.program_id(0) // per == p, gp, g)
        delta, m_new, v_new = _adamw_math(w_ref[...], g, m_ref[...], v_ref[...])
        g_out[...] = g
        d_out[...] = delta
        m_out[...] = m_new
        v_out[...] = v_new

    tile = pl.BlockSpec((tr, c), lambda i, me: (i, 0))
    part_specs, part_args = [], []
    for p, (g_chip, landed) in enumerate(parts):
        row = lambda i, p=p: jnp.clip(i - p * per, 0, per - 1)
        part_specs.append(pl.BlockSpec((tr, c), lambda i, me, row=row: (row(i), 0)))
        part_specs += [pl.BlockSpec((1, tr, c), lambda i, me, k=k, row=row: ((me[0] // 2 + k) % 4, row(i), 0))
                       for k in range(1, 4)]
        part_args += [g_chip, landed, landed, landed]
    return pl.pallas_call(
        body, name=name,
        grid_spec=pltpu.PrefetchScalarGridSpec(
            num_scalar_prefetch=1, grid=(r // tr,),
            in_specs=[tile] * 3 + part_specs + [ANY] * len(deps), out_specs=[tile] * 4),
        out_shape=[jax.ShapeDtypeStruct((r, c), F32)] * 4,
        compiler_params=_params(1),
    )(me_arr, w, m, v, *part_args, *deps)


SMALL_W = 1024
VEC_ROWS = 16
META_ROW0 = 16
CONF_ROW0 = 64
SHORT_ROW0 = 96
SMALL_ROWS = 104


def _pack_small(vec_parts, dmeta, dcw, dsw, me_arr):
    widths = [p.shape[1] for p in vec_parts]
    nv = len(vec_parts)

    def body(me_ref, *refs):
        del me_ref
        parts, (dmeta_ref, dcw_ref, dsw_ref, out_ref) = refs[:nv], refs[nv:]
        out_ref[0] = jnp.zeros((SMALL_ROWS, SMALL_W), F32)
        row = 0
        for p_ref, wd in zip(parts, widths):
            s = jnp.sum(p_ref[...], axis=0, keepdims=True)
            for h in range(wd // SMALL_W):
                out_ref[0, row:row + 1, :] = s[:, h * SMALL_W:(h + 1) * SMALL_W]
                row += 1
        for h in range(dmeta_ref.shape[1] // SMALL_W):
            out_ref[0, META_ROW0 + h * N_META:META_ROW0 + (h + 1) * N_META, :] = dmeta_ref[:, h * SMALL_W:(h + 1) * SMALL_W]
        for k in range(CONF_K):
            out_ref[0, CONF_ROW0 + k:CONF_ROW0 + k + 1, :] = jnp.sum(dcw_ref[k * SUB:(k + 1) * SUB, :], axis=0, keepdims=True)
        for k in range(SHORT_K):
            out_ref[0, SHORT_ROW0 + k:SHORT_ROW0 + k + 1, :] = jnp.sum(dsw_ref[k * SUB:(k + 1) * SUB, :], axis=0, keepdims=True)

    ins = [*vec_parts, dmeta, dcw, dsw]
    return pl.pallas_call(
        body, name="pack_small",
        grid_spec=pltpu.PrefetchScalarGridSpec(
            num_scalar_prefetch=1, grid=(1,),
            in_specs=[pl.BlockSpec(a.shape, lambda i, me: (0, 0)) for a in ins],
            out_specs=pl.BlockSpec((1, SMALL_ROWS, SMALL_W), lambda i, me: (me[0], 0, 0))),
        out_shape=jax.ShapeDtypeStruct((N_DEV, SMALL_ROWS, SMALL_W), F32),
        compiler_params=_params(1),
    )(me_arr, *ins)


def _small_update(gathered, me_arr, vec_params, meta_p, conf_p, short_p):
    widths = [p[0].shape[1] for p in vec_params]
    nv = len(vec_params)
    mcols = meta_p[0].shape[1]
    per_row = SMALL_W // mcols

    def body(me_ref, gv_ref, gm_ref, gc_ref, gs_ref, *rest):
        del me_ref
        ins, outs = rest[:3 * (nv + 3)], rest[3 * (nv + 3):]

        def total(ref, r0, rows):
            s = ref[0, r0:r0 + rows, :]
            for dev in range(1, N_DEV):
                s = s + ref[dev, r0:r0 + rows, :]
            return s

        grads = []
        row = 0
        for wd in widths:
            pieces = [total(gv_ref, row + h, 1) for h in range(wd // SMALL_W)]
            grads.append(pieces[0] if len(pieces) == 1 else jnp.concatenate(pieces, axis=1))
            row += len(pieces)
        grads.append(total(gm_ref, 0, N_META))
        grads.append(total(gc_ref, 0, CONF_K))
        grads.append(total(gs_ref, 0, SHORT_K))
        for idx, g in enumerate(grads):
            w_ref, m_ref, v_ref = ins[3 * idx:3 * idx + 3]
            delta, m_new, v_new = _adamw_math(w_ref[...], g, m_ref[...], v_ref[...])
            g_out, d_out, m_out, v_out = outs[4 * idx:4 * idx + 4]
            g_out[...] = g
            d_out[...] = delta
            m_out[...] = m_new
            v_out[...] = v_new

    params = list(vec_params) + [meta_p, conf_p, short_p]
    flat = [a for p in params for a in p]
    whole = lambda a: pl.BlockSpec(a.shape, lambda i, me: (0,) * a.ndim)
    outs = pl.pallas_call(
        body, name="small_update",
        grid_spec=pltpu.PrefetchScalarGridSpec(
            num_scalar_prefetch=1, grid=(1,),
            in_specs=[pl.BlockSpec((N_DEV, VEC_ROWS, SMALL_W), lambda i, me: (0, 0, 0)),
                      pl.BlockSpec((N_DEV, N_META, mcols),
                                   lambda i, me: (0, META_ROW0 // N_META + me[0] // per_row, me[0] % per_row)),
                      pl.BlockSpec((N_DEV, 32, LANE), lambda i, me: (0, CONF_ROW0 // 32, me[0])),
                      pl.BlockSpec((N_DEV, SUB, LANE), lambda i, me: (0, SHORT_ROW0 // SUB, me[0]))]
                     + [whole(a) for a in flat],
            out_specs=[whole(p[0]) for p in params for _ in range(4)]),
        out_shape=[jax.ShapeDtypeStruct(p[0].shape, F32) for p in params for _ in range(4)],
        compiler_params=_params(1),
    )(me_arr, gathered, gathered, gathered, gathered, *flat)
    return [tuple(outs[4 * i:4 * i + 4]) for i in range(len(params))]


def kernel(x, meta, g_pre_mix, w_in, b_gates, conf_dw_w, conf_dw_b, conf_ln_g, conf_ln_b, conf_w_pw, short_dw_w, short_w_out, w_o, g_post_mix, g_pre_mlp, w_up, w_down, g_post_mlp, loss_target, m_meta, m_g_pre_mix, m_w_in, m_b_gates, m_conf_dw_w, m_conf_dw_b, m_conf_ln_g, m_conf_ln_b, m_conf_w_pw, m_short_dw_w, m_short_w_out, m_w_o, m_g_post_mix, m_g_pre_mlp, m_w_up, m_w_down, m_g_post_mlp, v_meta, v_g_pre_mix, v_w_in, v_b_gates, v_conf_dw_w, v_conf_dw_b, v_conf_ln_g, v_conf_ln_b, v_conf_w_pw, v_short_dw_w, v_short_w_out, v_w_o, v_g_post_mix, v_g_pre_mlp, v_w_up, v_w_down, v_g_post_mlp):
    seq, d = x.shape[1], x.shape[2]
    dc = conf_w_pw.shape[1]
    t_real = N_META + seq
    t = -(-t_real // ROW_TILE) * ROW_TILE
    tm = t // 2
    assert tm % 16 == 0 and d % 1024 == 0 and dc % 1024 == 0
    x_idx, y_idx, c_idx = _position()
    me_arr = jnp.reshape(4 * x_idx + 2 * y_idx + c_idx, (1,)).astype(jnp.int32)

    big = [w_in[0], conf_w_pw[0], short_w_out[0], w_o[0], w_up[0], w_down[0]]
    big_names = ["w_in", "conf_w_pw", "short_w_out", "w_o", "w_up", "w_down"]
    groups = [[0], [1, 2, 3], [4], [5]]
    meta_g, cw_g, sw_g = _all_gather("gather_small_params", [meta, conf_dw_w[0], short_dw_w[0]])
    ici = []
    for g, idxs in enumerate(groups):
        order = [meta_g] + [st[3] for st in ici[-1:]]
        slots = [_cast_into_slot("cast_" + big_names[i], big[i], me_arr, deps=order) for i in idxs]
        ici.append(_remote_start("gather%d_ici_start" % g, "gather_ici", slots, deps=order))
    started = ici[-1][3][0, 0] * 0.0

    def forward_on(g, after):
        send, recv, bufs, _ = ici[g]
        bufs = _remote_wait("gather%d_ici_wait" % g, "gather_ici", send, recv, bufs, len(bufs), after)
        send, recv, bufs, tok = _remote_start("gather%d_d2d_start" % g, "gather_d2d", bufs)
        return (send, recv, bufs), tok

    def gathered(g, state, after):
        send, recv, bufs = state
        return _remote_wait("gather%d_d2d_wait" % g, "gather_d2d", send, recv, bufs, len(bufs), after)

    unshard =lambda g: jnp.transpose(g, (1, 0, 2)).reshape(g.shape[1], -1)
    meta_full, cw_full, sw_full = unshard(meta_g), unshard(cw_g), unshard(sw_g)

    zrows = jnp.zeros((t - t_real, d), F32) + started
    h0 = jnp.concatenate([meta_full, x[0], zrows], axis=0)
    tgt = jnp.concatenate([jnp.zeros((N_META, d), F32), loss_target[0], zrows], axis=0)
    n = _pre_norm(h0, g_pre_mix)
    fwd0, tok = forward_on(0, [n])
    win_g, = gathered(0, fwd0, [tok])
    proj = _mm_cols("proj", n, win_g, tm=tm)[0]
    fwd1, tok = forward_on(1, [proj])
    a1, s = _conv_forward(proj, cw_full, conf_dw_b, sw_full, dc, deps=[tok])
    a3 = _layer_norm_silu(a1, conf_ln_g, conf_ln_b)
    wpw_g, wso_g, wo_g = gathered(1, fwd1, [a3])
    wo_full = wo_g.reshape(d, d)
    ya = _mm_cols("y_a", a3, wpw_g, tm=tm, nb=N_DEV)[0]
    yb = _mm_cols("y_b", s, wso_g, tm=tm, nb=N_DEV)[0]
    m_mix = _gate_merge(proj, ya, yb, b_gates, d)
    mix = _mm_rows("mix", m_mix, wo_full, tm=tm // 2, tn=d)
    fwd2, tok = forward_on(2, [mix])
    h1, n2 = _post_mix(mix, h0, g_post_mix, g_pre_mlp, deps=[tok])
    wup_g, = gathered(2, fwd2, [n2])

    def up_epilogue(acc):
        r = jnp.maximum(acc, 0.0)
        return r * r, r

    f, relu_up = _mm_cols("mlp_up", n2, wup_g, tm=tm, epilogue=up_epilogue, out_dtypes=(BF16, BF16))
    fwd3, tok = forward_on(3, [f])
    wdn_g, = gathered(3, fwd3, [tok])
    wdn_full = wdn_g.reshape(-1, d)
    fo = _mm_rows("mlp_down", f, wdn_full, tm=tm // 2, tn=512)
    dfo, dh2, dg_post_mlp, loss_blk = _loss_head(fo, h1, tgt, g_post_mlp, t_real)
    loss = lax.psum(loss_blk[0, 0], ("x", "y", "c"))

    def reduce_start(tag, fulls, deps):
        lands = [lax.empty((4,) + g.shape[1:], BF16) for g in fulls]
        send, recv, bufs, tok = _remote_start("reduce_%s_d2d_start" % tag, "reduce_d2d", fulls, lands, deps=deps)
        return (send, recv, bufs), tok

    def reduce_middle(tag, state, owns, after):
        send, recv, bufs = state
        k = len(owns)
        bufs = _remote_wait("reduce_%s_d2d_wait" % tag, "reduce_d2d", send, recv, bufs, k, after)
        sums = [_chip_sum("chip_sum_%s%d" % (tag, i), bufs[i], bufs[k + i], owns[i], me_arr) for i in range(k)]
        lands = [lax.empty(sm[0].shape, BF16) for sm in sums]
        send, recv, bufs, tok = _remote_start("reduce_%s_ici_start" % tag, "reduce_ici", [sm[0] for sm in sums], lands)
        return (send, recv, bufs, [sm[1] for sm in sums]), tok

    def reduce_finish(tag, state, after):
        send, recv, bufs, chip_sums = state
        k = len(chip_sums)
        bufs = _remote_wait("reduce_%s_ici_wait" % tag, "reduce_ici", send, recv, bufs, k, after)
        return list(zip(chip_sums, bufs[k:]))

    dup = _mm_nt_blocks("d_up", dfo, wdn_full, tm=tm, tkb=1024, extra=(relu_up,),
                        epilogue=lambda acc, r: (acc * (2.0 * r.astype(F32)),), out_dtypes=(BF16,))[0]
    gw_down, gw_down_own = _mm_tn("dw_down", f, dfo, me_arr, m=f.shape[1], n=d, tma=512, tn=d, sharded="rows")
    red_down, tok = reduce_start("down", [gw_down], ())
    dn2 = _mm_nt_acc("d_n2", dup, wup_g, tm=tm // 2, tn=512, deps=[tok])
    gw_up, gw_up_own = _mm_tn("dw_up", n2, dup, me_arr, m=d, n=dup.shape[1], tma=512, tn=1024, sharded="cols")
    red_down, tok = reduce_middle("down", red_down, [gw_down_own], [dn2])
    red_up, tok = reduce_start("up", [gw_up], [tok])
    dh1, dmix, dg_pre_mlp, dg_post_mix = _mid_norm_bwd(dn2, h1, dh2, mix, g_pre_mlp, g_post_mix, deps=[tok])
    dya, dyb, dproj, db_a, db_b = _gate_backward(dmix, wo_full, proj, ya, yb, b_gates, d, tm // 2)
    db_gates = jnp.concatenate([db_a, db_b], axis=1)
    red_up, tok = reduce_middle("up", red_up, [gw_up_own], [dya])
    gw_o, gw_o_own = _mm_tn("dw_o", m_mix, dmix, me_arr, m=d, n=d, tma=d // N_DEV, tn=d, sharded="rows", deps=[tok])
    da3 = _mm_nt_acc("d_a3", dya, wpw_g, tm=tm, tn=512)
    gw_pw, gw_pw_own = _mm_tn("dw_pw", a3, dya, me_arr, m=dc, n=d, tma=512, tn=d, sharded="cols")
    dsb = _mm_nt_acc("d_s", dyb, wso_g, tm=tm, tn=512)
    gw_so, gw_so_own = _mm_tn("dw_so", s, dyb, me_arr, m=dc, n=d, tma=512, tn=d, sharded="cols")
    red_mix, tok = reduce_start("mix", [gw_pw, gw_so, gw_o], ())
    da1, dln_g, dln_b = _layer_norm_silu_bwd(da3, a1, conf_ln_g, conf_ln_b, deps=[tok])
    dproj, dcw, dcb, dsw = _conv_backward(dproj, proj, da1, dsb, cw_full, sw_full, dc)
    red_mix, tok = reduce_middle("mix", red_mix, [gw_pw_own, gw_so_own, gw_o_own], [dcb])
    in_cb = w_in.shape[2]
    half = d // 2
    red_in = []
    for part in range(2):
        gw, own = _mm_tn("dw_in%d" % part, n, dproj, me_arr, m=half, n=proj.shape[1], tma=512, tn=2 * in_cb,
                         sharded="cols", a_off=part * (half // 512), deps=[tok])
        state, tok = reduce_start("in%d" % part, [gw], ())
        red_in.append((state, own))
    for part in range(2):
        state, own = red_in[part]
        red_in[part], tok = reduce_middle("in%d" % part, state, [own], [tok])
    dn = _mm_nt_acc("d_n", dproj, win_g, tm=tm // 2, tn=512, deps=[tok])
    dh0, dg_pre_mix = _pre_norm_bwd(dn, h0, dh1, g_pre_mix)
    grad_x = dh0[N_META:t_real][None]

    vec_parts = [dg_pre_mix, db_gates, dcb, dln_g, dln_b, dg_post_mix, dg_pre_mlp, dg_post_mlp]
    packed = _pack_small(vec_parts, dh0[:N_META], dcw, dsw, me_arr)
    send, recv, bufs, tok = _remote_start("small_grads_ici_start", "gather_ici", [packed])
    vec_names = ["g_pre_mix", "b_gates", "conf_dw_b", "conf_ln_g", "conf_ln_b", "g_post_mix", "g_pre_mlp", "g_post_mlp"]
    env = locals()
    results = {}

    def update(nm, parts, deps=()):
        res = _adamw_shard("adamw_" + nm, env[nm][0], env["m_" + nm][0], env["v_" + nm][0], parts, me_arr, deps=deps)
        results[nm] = tuple(r[None] for r in res)
        return res[0]

    done = [update("w_down", reduce_finish("down", red_down, [tok]), deps=[tok])]
    done.append(update("w_up", reduce_finish("up", red_up, done)))
    bufs = _remote_wait("small_grads_ici_wait", "gather_ici", send, recv, bufs, 1, done)
    send, recv, bufs, tok = _remote_start("small_grads_d2d_start", "gather_d2d", bufs)
    for nm, pair in zip(["conf_w_pw", "short_w_out", "w_o"], reduce_finish("mix", red_mix, [tok])):
        done.append(update(nm, [pair], deps=[tok]))
    small_g, = _remote_wait("small_grads_d2d_wait", "gather_d2d", send, recv, bufs, 1, done)
    triple = lambda nm, sq: tuple(env[p + nm][0] if sq else env[p + nm] for p in ("", "m_", "v_"))
    small = _small_update(small_g, me_arr, [triple(nm, False) for nm in vec_names],
                          triple("meta", False), triple("conf_dw_w", True), triple("short_dw_w", True))
    for nm, res in zip(vec_names + ["meta"], small[:len(vec_names) + 1]):
        results[nm] = res
    results["conf_dw_w"] = tuple(r[None] for r in small[-2])
    results["short_dw_w"] = tuple(r[None] for r in small[-1])
    update("w_in", [reduce_finish("in%d" % part, red_in[part], [small[0][0]])[0] for part in range(2)])

    order = ["meta", "g_pre_mix", "w_in", "b_gates", "conf_dw_w", "conf_dw_b", "conf_ln_g", "conf_ln_b", "conf_w_pw",
             "short_dw_w", "short_w_out", "w_o", "g_post_mix", "g_pre_mlp", "w_up", "w_down", "g_post_mlp"]
    return (loss, grad_x, *[results[nm][0] for nm in order], *[results[nm][1] for nm in order],
            *[results[nm][2] for nm in order], *[results[nm][3] for nm in order])
```

```python
import jax
import jax.numpy as jnp
from jax import lax
from jax.experimental import pallas as pl
from jax.experimental.pallas import tpu as pltpu

N_DEV = 8
N_META = 16
CONF_K = 31
SHORT_K = 3
RMS_EPS = 1e-6
LN_EPS = 1e-5
ADAM_LR = 0.001
ADAM_B1 = 0.9
ADAM_B2 = 0.999
ADAM_EPS = 1e-08
ADAM_WD = 0.01
ADAM_STEP = 10

LANE = 128
SUB = 8
ROW_TILE = 128
CONV_PAD = 32
CONV_CHUNK = 128
VMEM_LIMIT = 56 * 1024 * 1024

F32 = jnp.float32
BF16 = jnp.bfloat16
MESH = pl.DeviceIdType.MESH
ANY = pl.BlockSpec(memory_space=pl.ANY)
HBM_SPEC = pl.BlockSpec(memory_space=pltpu.HBM)
SEM_SPEC = pl.BlockSpec(memory_space=pltpu.SEMAPHORE)
EFFECT = pltpu.SideEffectType.DATAFLOW_SIDE_EFFECTING


def _params(n_axes):
    return pltpu.CompilerParams(dimension_semantics=("arbitrary",) * n_axes, vmem_limit_bytes=VMEM_LIMIT)


def _sigmoid(z):
    return 1.0 / (1.0 + jnp.exp(-z))


def _colsum8(v):
    r, c = v.shape
    return jnp.sum(v.reshape(r // SUB, SUB, c), axis=0)


def _position():
    x, y, c = lax.axis_index("x"), lax.axis_index("y"), lax.axis_index("c")
    return x, y, c


def _flat(p):
    return 4 * p[0] + 2 * p[1] + p[2]


def _all_gather(name, shards, deps=()):
    n, nd = len(shards), len(deps)

    def body(*refs):
        ins, outs = refs[:n], refs[n + nd:2 * n + nd]
        send_sems, recv_sems, local_sems = refs[2 * n + nd:]
        x, y, c = _position()
        me, sibling = (x, y, c), (x, y, 1 - c)
        chips = [(1 - x, y), (x, 1 - y), (1 - x, 1 - y)]

        def copy(q, k, block, to, src=None):
            dst = outs[q].at[_flat(block)]
            return pltpu.make_async_remote_copy(
                src_ref=dst if src is None else src, dst_ref=dst,
                send_sem=send_sems.at[q, k], recv_sem=recv_sems.at[q, k],
                device_id=to, device_id_type=MESH)

        mine = [pltpu.make_async_copy(ins[q], outs[q].at[_flat(me)], local_sems.at[q]) for q in range(n)]
        for cp in mine:
            cp.start()
        first = []
        for q in range(n):
            first.append(copy(q, 0, me, sibling, src=ins[q]))
            for j, chip in enumerate(chips):
                first.append(copy(q, 1 + j, me, (*chip, c), src=ins[q]))
        for cp in first:
            cp.start()
        passed = []
        for q in range(n):
            for j, chip in enumerate(chips):
                copy(q, 1 + j, (*chip, c), me).wait_recv()
                fwd = copy(q, 4 + j, (*chip, c), sibling)
                fwd.start()
                passed.append(fwd)
        for q in range(n):
            copy(q, 0, sibling, me).wait_recv()
            for j, chip in enumerate(chips):
                copy(q, 4 + j, (*chip, 1 - c), me).wait_recv()
        for cp in first + passed:
            cp.wait_send()
        for cp in mine:
            cp.wait()

    return pl.pallas_call(
        body, name=name,
        in_specs=[ANY] * (n + nd), out_specs=[ANY] * n,
        out_shape=[jax.ShapeDtypeStruct((N_DEV,) + s.shape, s.dtype) for s in shards],
        scratch_shapes=[pltpu.SemaphoreType.DMA((n, 7)), pltpu.SemaphoreType.DMA((n, 7)),
                        pltpu.SemaphoreType.DMA((n,))],
    )(*shards, *deps)


N_COPIES = {"gather_ici": 4, "gather_d2d": 3, "reduce_d2d": 4, "reduce_ici": 3}


def _copy_plan(kind):
    x, y, c = _position()
    me, sibling = (x, y, c), (x, y, 1 - c)
    chips = [(1 - x, y), (x, 1 - y), (1 - x, 1 - y)]
    if kind == "gather_ici":
        return [(_flat(me), _flat(me), sibling)] + [(_flat(me), _flat(me), (*ch, c)) for ch in chips]
    if kind == "gather_d2d":
        return [(_flat((*ch, c)), _flat((*ch, c)), sibling) for ch in chips]
    if kind == "reduce_d2d":
        return [(2 * chip + (1 - c), chip, sibling) for chip in range(4)]
    return [(2 * ch[0] + ch[1], 2 * x + y, (*ch, c)) for ch in chips]


def _planned_copies(kind, srcs, dsts, send_sems, recv_sems):
    plan = _copy_plan(kind)
    return [pltpu.make_async_remote_copy(
        src_ref=src.at[s_slot], dst_ref=dst.at[d_slot],
        send_sem=send_sems.at[q * len(plan) + k], recv_sem=recv_sems.at[q * len(plan) + k],
        device_id=to, device_id_type=MESH)
        for q, (src, dst) in enumerate(zip(srcs, dsts)) for k, (s_slot, d_slot, to) in enumerate(plan)]


def _remote_start(name, kind, srcs, lands=None, deps=()):
    n = len(srcs)
    bufs = list(srcs) + ([] if lands is None else list(lands))
    nb, nd = len(bufs), len(deps)
    nsem = n * N_COPIES[kind]

    def body(*refs):
        ins = refs[:nb]
        send_sems, recv_sems = refs[nb + nd], refs[nb + nd + 1]
        token = refs[-1]
        for cp in _planned_copies(kind, ins[:n], ins[:n] if lands is None else ins[n:], send_sems, recv_sems):
            cp.start()
        token[...] = jnp.zeros_like(token)

    outs = pl.pallas_call(
        body, name=name,
        out_shape=(pltpu.SemaphoreType.DMA((nsem,)), pltpu.SemaphoreType.DMA((nsem,)),
                   *[pltpu.HBM(b.shape, b.dtype) for b in bufs], jax.ShapeDtypeStruct((SUB, LANE), F32)),
        in_specs=[HBM_SPEC] * nb + [ANY] * nd,
        out_specs=(SEM_SPEC, SEM_SPEC, *[HBM_SPEC] * nb, pl.BlockSpec(memory_space=pltpu.VMEM)),
        input_output_aliases={i: 2 + i for i in range(nb)},
        compiler_params=pltpu.CompilerParams(has_side_effects=EFFECT),
    )(*[pltpu.with_memory_space_constraint(b, pltpu.HBM) for b in bufs], *deps)
    return outs[0], outs[1], list(outs[2:2 + nb]), outs[-1]


def _remote_wait(name, kind, send_sems, recv_sems, bufs, n, after):
    nb, na = len(bufs), len(after)
    same = nb == n

    def body(*refs):
        ins = refs[:nb]
        sends, recvs = refs[nb], refs[nb + 1]
        for cp in _planned_copies(kind, ins[:n], ins[:n] if same else ins[n:], sends, recvs):
            cp.wait_send()
            cp.wait_recv()

    outs = pl.pallas_call(
        body, name=name,
        out_shape=[pltpu.HBM(b.shape, b.dtype) for b in bufs],
        in_specs=[HBM_SPEC] * nb + [SEM_SPEC, SEM_SPEC] + [ANY] * na,
        out_specs=[HBM_SPEC] * nb,
        input_output_aliases={i: i for i in range(nb)},
        compiler_params=pltpu.CompilerParams(has_side_effects=EFFECT),
    )(*bufs, send_sems, recv_sems, *after)
    return list(outs)


def _mm_cols(name, a, w, *, tm, nb=1, epilogue=None, out_dtypes=(F32,)):
    t, k = a.shape
    nblk, _, cb = w.shape

    def body(a_ref, w_ref, *o_refs):
        av = a_ref[...]
        for b in range(nb):
            acc = jnp.dot(av, w_ref[b], preferred_element_type=F32)
            outs = (acc,) if epilogue is None else epilogue(acc)
            for o_ref, o in zip(o_refs, outs):
                o_ref[:, b * cb:(b + 1) * cb] = o.astype(o_ref.dtype)

    return pl.pallas_call(
        body, name=name, grid=(nblk // nb, t // tm),
        in_specs=[pl.BlockSpec((tm, k), lambda j, i: (i, 0)),
                  pl.BlockSpec((nb, k, cb), lambda j, i: (j, 0, 0))],
        out_specs=[pl.BlockSpec((tm, nb * cb), lambda j, i: (i, j)) for _ in out_dtypes],
        out_shape=[jax.ShapeDtypeStruct((t, nblk * cb), dt) for dt in out_dtypes],
        compiler_params=_params(2),
    )(a, w)


def _mm_rows(name, a, w2d, *, tm, tn):
    t, kf = a.shape
    n = w2d.shape[1]

    def body(a_ref, w_ref, o_ref):
        o_ref[...] = jnp.dot(a_ref[...], w_ref[...], preferred_element_type=F32)

    return pl.pallas_call(
        body, name=name, grid=(t // tm, n // tn),
        in_specs=[pl.BlockSpec((tm, kf), lambda i, j: (i, 0)),
                  pl.BlockSpec((kf, tn), lambda i, j: (0, j))],
        out_specs=pl.BlockSpec((tm, tn), lambda i, j: (i, j)),
        out_shape=jax.ShapeDtypeStruct((t, n), F32),
        compiler_params=_params(2),
    )(a, w2d)


def _mm_nt_acc(name, dy, w, *, tm, tn, col_off=0, deps=()):
    t = dy.shape[0]
    nblk, k, cb = w.shape

    def body(dy_ref, w_ref, *rest):
        acc = None
        for b in range(nblk):
            d = lax.dot_general(dy_ref[:, b * cb:(b + 1) * cb], w_ref[b], (((1,), (1,)), ((), ())),
                                preferred_element_type=F32)
            acc = d if acc is None else acc + d
        rest[-1][...] = acc

    return pl.pallas_call(
        body, name=name, grid=(t // tm, k // tn),
        in_specs=[pl.BlockSpec((tm, nblk * cb), lambda i, j: (i, col_off)),
                  pl.BlockSpec((nblk, tn, cb), lambda i, j: (0, j, 0))] + [ANY] * len(deps),
        out_specs=pl.BlockSpec((tm, tn), lambda i, j: (i, j)),
        out_shape=jax.ShapeDtypeStruct((t, k), F32),
        compiler_params=_params(2),
    )(dy, w, *deps)


def _mm_nt_blocks(name, dy, w2d, *, tm, tkb, extra=(), epilogue=None, out_dtypes=(F32,)):
    t, n = dy.shape
    kf = w2d.shape[0]
    ne = len(extra)

    def body(dy_ref, w_ref, *rest):
        acc = lax.dot_general(dy_ref[...], w_ref[...], (((1,), (1,)), ((), ())), preferred_element_type=F32)
        outs = (acc,) if epilogue is None else epilogue(acc, *[e[...] for e in rest[:ne]])
        for o_ref, o in zip(rest[ne:], outs):
            o_ref[...] = o.astype(o_ref.dtype)

    return pl.pallas_call(
        body, name=name, grid=(kf // tkb, t // tm),
        in_specs=[pl.BlockSpec((tm, n), lambda kb, i: (i, 0)),
                  pl.BlockSpec((tkb, n), lambda kb, i: (kb, 0))]
                 + [pl.BlockSpec((tm, tkb), lambda kb, i: (i, kb)) for _ in extra],
        out_specs=[pl.BlockSpec((tm, tkb), lambda kb, i: (i, kb)) for _ in out_dtypes],
        out_shape=[jax.ShapeDtypeStruct((t, kf), dt) for dt in out_dtypes],
        compiler_params=_params(2),
    )(dy, w2d, *extra)


def _mm_tn(name, a, b, me_arr, *, m, n, tma, tn, sharded, a_off=0, b_off=0, deps=()):
    t = a.shape[0]
    if sharded == "cols":
        cb = n // N_DEV
        nb, q = max(tn // cb, 1), max(cb // tn, 1)
        tw = tn // nb
        full_shape, own_shape = (N_DEV, m, cb), (m, cb)
        full_spec = pl.BlockSpec((nb, tma, tw), lambda i, j, me: (j // q, i, j % q))
    else:
        kb = m // N_DEV
        p = kb // tma
        nb, tw = 1, tn
        full_shape, own_shape = (m, n), (kb, n)
        full_spec = pl.BlockSpec((tma, tn), lambda i, j, me: (i, j))

    def body(me_ref, a_ref, b_ref, *rest):
        full_ref, own_ref, stage, sem = rest[len(deps):]
        i, j = pl.program_id(0), pl.program_id(1)
        acc = lax.dot_general(a_ref[...], b_ref[...], (((0,), (0,)), ((), ())), preferred_element_type=F32)
        for blk in range(nb):
            part = acc[:, blk * tw:(blk + 1) * tw]
            if sharded == "cols":
                full_ref[blk] = part.astype(BF16)
                owner, r0, c0 = (j // q) * nb + blk, i * tma, (j % q) * tw
            else:
                full_ref[...] = part.astype(BF16)
                owner, r0, c0 = i // p, (i % p) * tma, j * tn

            @pl.when(owner == me_ref[0])
            def _():
                stage[...] = part
                cp = pltpu.make_async_copy(
                    stage, own_ref.at[pl.ds(pl.multiple_of(r0, tma), tma), pl.ds(pl.multiple_of(c0, tw), tw)], sem)
                cp.start()
                cp.wait()

    full, own = pl.pallas_call(
        body, name=name,
        grid_spec=pltpu.PrefetchScalarGridSpec(
            num_scalar_prefetch=1, grid=(m // tma, n // tn),
            in_specs=[pl.BlockSpec((t, tma), lambda i, j, me: (0, a_off + i)),
                      pl.BlockSpec((t, tn), lambda i, j, me: (0, b_off + j))] + [ANY] * len(deps),
            out_specs=[full_spec, ANY],
            scratch_shapes=[pltpu.VMEM((tma, tw), F32), pltpu.SemaphoreType.DMA(())]),
        out_shape=[jax.ShapeDtypeStruct(full_shape, BF16), jax.ShapeDtypeStruct(own_shape, F32)],
        compiler_params=_params(2),
    )(me_arr, a, b, *deps)
    if sharded == "rows":
        full = full.reshape(N_DEV, m // N_DEV, n)
    return full, own


def _row_tile(t):
    return t // 8 if (t // 8) % 16 == 0 else ROW_TILE


def _row_call(name, body, t, row_ins, full_ins, row_outs, acc_outs, scratch=(), deps=()):
    tm = _row_tile(t)
    nin = len(row_ins) + len(full_ins)

    def without_deps(*refs):
        body(*refs[:nin], *refs[nin + len(deps):])

    return pl.pallas_call(
        without_deps, name=name, grid=(t // tm,),
        in_specs=[pl.BlockSpec((tm, a.shape[1]), lambda i: (i, 0)) for a in row_ins]
                 + [pl.BlockSpec(a.shape, lambda i: (0, 0)) for a in full_ins] + [ANY] * len(deps),
        out_specs=[pl.BlockSpec((tm, c), lambda i: (i, 0)) for c, _ in row_outs]
                  + [pl.BlockSpec((r, c), lambda i: (0, 0)) for r, c in acc_outs],
        out_shape=[jax.ShapeDtypeStruct((t, c), dt) for c, dt in row_outs]
                  + [jax.ShapeDtypeStruct((r, c), F32) for r, c in acc_outs],
        scratch_shapes=list(scratch),
        compiler_params=_params(1),
    )(*row_ins, *full_ins, *deps)


def _accumulate(ref, v):
    @pl.when(pl.program_id(0) == 0)
    def _():
        ref[...] = v

    @pl.when(pl.program_id(0) > 0)
    def _():
        ref[...] += v


def _rms(v):
    return lax.rsqrt(jnp.mean(v * v, axis=-1, keepdims=True) + RMS_EPS)


def _rms_bwd(dout, u, r, g):
    du = dout * g
    dx = r * (du - u * jnp.mean(du * u, axis=-1, keepdims=True))
    return dx, _colsum8(dout * u)


def _pre_norm(h0, g):
    t, d = h0.shape

    def body(h_ref, g_ref, n_ref):
        h = h_ref[...]
        n_ref[...] = (h * _rms(h) * g_ref[...]).astype(BF16)

    return _row_call("pre_norm", body, t, [h0], [g], [(d, BF16)], [])[0]


def _post_mix(mix, h0, g_post, g_pre, deps=()):
    t, d = h0.shape

    def body(mix_ref, h0_ref, gp_ref, gq_ref, h1_ref, n2_ref):
        mix_v = mix_ref[...]
        h1 = h0_ref[...] + mix_v * _rms(mix_v) * gp_ref[...]
        h1_ref[...] = h1
        n2_ref[...] = (h1 * _rms(h1) * gq_ref[...]).astype(BF16)

    return _row_call("post_mix", body, t, [mix, h0], [g_post, g_pre], [(d, F32), (d, BF16)], [], deps=deps)


def _loss_head(fo, h1, tgt, g_post_mlp, t_real):
    t, d = h1.shape
    tile = _row_tile(t)

    def body(fo_ref, h1_ref, tgt_ref, g_ref, dfo_ref, dh2_ref, dg_ref, loss_ref, lacc):
        i = pl.program_id(0)
        fo_v = fo_ref[...]
        g = g_ref[...]
        r = _rms(fo_v)
        u = fo_v * r
        h2 = h1_ref[...] + u * g
        row = i * tile + lax.broadcasted_iota(jnp.int32, (tile, 1), 0)
        valid = jnp.logical_and(row >= N_META, row < t_real)
        diff = jnp.where(valid, h2 - tgt_ref[...], 0.0)
        dh2 = diff * (1.0 / d)
        dh2_ref[...] = dh2
        dfo, dg = _rms_bwd(dh2, u, r, g)
        dfo_ref[...] = dfo.astype(BF16)
        _accumulate(dg_ref, dg)
        _accumulate(lacc, _colsum8(diff * diff))

        @pl.when(i == pl.num_programs(0) - 1)
        def _():
            loss_ref[...] = jnp.full((SUB, LANE), (0.5 / d) * jnp.sum(lacc[...]), F32)

    return _row_call("loss_head", body, t, [fo, h1, tgt], [g_post_mlp],
                     [(d, BF16), (d, F32)], [(SUB, d), (SUB, LANE)], scratch=[pltpu.VMEM((SUB, d), F32)])


def _mid_norm_bwd(dn2, h1, dh2, mix, g_pre_mlp, g_post_mix, deps=()):
    t, d = h1.shape

    def body(dn2_ref, h1_ref, dh2_ref, mix_ref, gq_ref, gp_ref, dh1_ref, dmix_ref, dgq_ref, dgp_ref):
        h1 = h1_ref[...]
        r3 = _rms(h1)
        dx, dgq = _rms_bwd(dn2_ref[...], h1 * r3, r3, gq_ref[...])
        dh1 = dh2_ref[...] + dx
        dh1_ref[...] = dh1
        mix_v = mix_ref[...]
        r2 = _rms(mix_v)
        dmix, dgp = _rms_bwd(dh1, mix_v * r2, r2, gp_ref[...])
        dmix_ref[...] = dmix.astype(BF16)
        _accumulate(dgq_ref, dgq)
        _accumulate(dgp_ref, dgp)

    return _row_call("mid_norm_bwd", body, t, [dn2, h1, dh2, mix], [g_pre_mlp, g_post_mix],
                     [(d, F32), (d, BF16)], [(SUB, d), (SUB, d)], deps=deps)


def _pre_norm_bwd(dn, h0, dh1, g_pre_mix, deps=()):
    t, d = h0.shape

    def body(dn_ref, h0_ref, dh1_ref, g_ref, dh0_ref, dg_ref):
        h0 = h0_ref[...]
        r = _rms(h0)
        dx, dg = _rms_bwd(dn_ref[...], h0 * r, r, g_ref[...])
        dh0_ref[...] = dh1_ref[...] + dx
        _accumulate(dg_ref, dg)

    return _row_call("pre_norm_bwd", body, t, [dn, h0, dh1], [g_pre_mix], [(d, F32)], [(SUB, d)], deps=deps)


def _layer_norm_silu(a1, ln_g, ln_b):
    t, c = a1.shape

    def body(a1_ref, g_ref, b_ref, a3_ref):
        a = a1_ref[...]
        mu = jnp.mean(a, axis=-1, keepdims=True)
        xc = a - mu
        rstd = lax.rsqrt(jnp.mean(xc * xc, axis=-1, keepdims=True) + LN_EPS)
        z = xc * rstd * g_ref[...] + b_ref[...]
        a3_ref[...] = (z * _sigmoid(z)).astype(BF16)

    return _row_call("layer_norm_silu", body, t, [a1], [ln_g, ln_b], [(c, BF16)], [])[0]


def _layer_norm_silu_bwd(da3, a1, ln_g, ln_b, deps=()):
    t, c = a1.shape

    def body(da3_ref, a1_ref, g_ref, b_ref, da1_ref, dg_ref, db_ref):
        a = a1_ref[...]
        g = g_ref[...]
        mu = jnp.mean(a, axis=-1, keepdims=True)
        xc = a - mu
        rstd = lax.rsqrt(jnp.mean(xc * xc, axis=-1, keepdims=True) + LN_EPS)
        xhat = xc * rstd
        z = xhat * g + b_ref[...]
        sg = _sigmoid(z)
        dz = da3_ref[...] * (sg * (1.0 + z * (1.0 - sg)))
        dxhat = dz * g
        da1_ref[...] = rstd * (dxhat - jnp.mean(dxhat, axis=-1, keepdims=True)
                               - xhat * jnp.mean(dxhat * xhat, axis=-1, keepdims=True))
        _accumulate(dg_ref, _colsum8(dz * xhat))
        _accumulate(db_ref, _colsum8(dz))

    return _row_call("layer_norm_silu_bwd", body, t, [da3, a1], [ln_g, ln_b], [(c, F32)], [(SUB, c), (SUB, c)], deps=deps)


def _gate_merge(proj, ya, yb, b_gates, d, deps=()):
    t = proj.shape[0]
    w = 1024
    nh = d // w
    ga0 = (proj.shape[1] - 2 * d) // w

    def body(pa_ref, pb_ref, ya_ref, yb_ref, ba_ref, bb_ref, *rest):
        m_ref = rest[-1]
        ga = _sigmoid(pa_ref[...] + ba_ref[...])
        gb = _sigmoid(pb_ref[...] + bb_ref[...])
        m_ref[...] = (ga * ya_ref[...] + gb * yb_ref[...]).astype(BF16)

    tm = _row_tile(t)
    return pl.pallas_call(
        body, name="gate_merge", grid=(nh, t // tm),
        in_specs=[pl.BlockSpec((tm, w), lambda h, i: (i, ga0 + h)),
                  pl.BlockSpec((tm, w), lambda h, i: (i, ga0 + nh + h)),
                  pl.BlockSpec((tm, w), lambda h, i: (i, h)),
                  pl.BlockSpec((tm, w), lambda h, i: (i, h)),
                  pl.BlockSpec((1, w), lambda h, i: (0, h)),
                  pl.BlockSpec((1, w), lambda h, i: (0, nh + h))] + [ANY] * len(deps),
        out_specs=pl.BlockSpec((tm, w), lambda h, i: (i, h)),
        out_shape=jax.ShapeDtypeStruct((t, d), BF16),
        compiler_params=_params(2),
    )(proj, proj, ya, yb, b_gates, b_gates, *deps)


def _gate_backward(dmix, wo_full, proj, ya, yb, b_gates, d, tm, deps=()):
    t, cols = proj.shape
    w = 1024
    nh = d // w
    ga0 = (cols - 2 * d) // w

    def body(dmix_ref, wo_ref, pa_ref, pb_ref, ya_ref, yb_ref, ba_ref, bb_ref, *rest):
        dya_ref, dyb_ref, dp_ref, dba_ref, dbb_ref, stage, sems = rest[len(deps):]
        h, i = pl.program_id(0), pl.program_id(1)
        dm = lax.dot_general(dmix_ref[...], wo_ref[...], (((1,), (1,)), ((), ())), preferred_element_type=F32)
        ga = _sigmoid(pa_ref[...] + ba_ref[...])
        gb = _sigmoid(pb_ref[...] + bb_ref[...])
        dya_ref[...] = (dm * ga).astype(BF16)
        dyb_ref[...] = (dm * gb).astype(BF16)
        dpa = dm * ya_ref[...] * ga * (1.0 - ga)
        dpb = dm * yb_ref[...] * gb * (1.0 - gb)
        stage[0] = dpa.astype(BF16)
        stage[1] = dpb.astype(BF16)
        rows = pl.ds(pl.multiple_of(i * tm, tm), tm)
        copies = [pltpu.make_async_copy(
            stage.at[g], dp_ref.at[rows, pl.ds(pl.multiple_of((ga0 + g * nh + h) * w, w), w)], sems.at[g])
            for g in range(2)]
        for cp in copies:
            cp.start()

        @pl.when(i == 0)
        def _():
            dba_ref[...] = _colsum8(dpa)
            dbb_ref[...] = _colsum8(dpb)

        @pl.when(i > 0)
        def _():
            dba_ref[...] += _colsum8(dpa)
            dbb_ref[...] += _colsum8(dpb)

        for cp in copies:
            cp.wait()

    tile = pl.BlockSpec((tm, w), lambda h, i: (i, h))
    return pl.pallas_call(
        body, name="gate_backward", grid=(nh, t // tm),
        in_specs=[pl.BlockSpec((tm, d), lambda h, i: (i, 0)),
                  pl.BlockSpec((w, d), lambda h, i: (h, 0)),
                  pl.BlockSpec((tm, w), lambda h, i: (i, ga0 + h)),
                  pl.BlockSpec((tm, w), lambda h, i: (i, ga0 + nh + h)),
                  tile, tile,
                  pl.BlockSpec((1, w), lambda h, i: (0, h)),
                  pl.BlockSpec((1, w), lambda h, i: (0, nh + h))] + [ANY] * len(deps),
        out_specs=[tile, tile, ANY,
                   pl.BlockSpec((SUB, w), lambda h, i: (0, h)),
                   pl.BlockSpec((SUB, w), lambda h, i: (0, h))],
        out_shape=[jax.ShapeDtypeStruct((t, d), BF16), jax.ShapeDtypeStruct((t, d), BF16),
                   jax.ShapeDtypeStruct((t, cols), BF16),
                   jax.ShapeDtypeStruct((SUB, d), F32), jax.ShapeDtypeStruct((SUB, d), F32)],
        scratch_shapes=[pltpu.VMEM((2, tm, w), BF16), pltpu.SemaphoreType.DMA((2,))],
        compiler_params=_params(2),
    )(dmix, wo_full, proj, proj, ya, yb, b_gates, b_gates, *deps)


def _causal_conv(xp_ref, w_ref, ntap, r0):
    n = CONV_CHUNK + CONV_PAD
    win = xp_ref[pl.ds(r0, n), :]
    acc = None
    for k in range(ntap):
        back = ntap - 1 - k
        shifted = pltpu.roll(win, n - (CONV_PAD - back), 0)
        term = w_ref[k:k + 1, :] * shifted[:CONV_CHUNK]
        acc = term if acc is None else acc + term
    return acc


def _anticausal_conv(xp_ref, w_ref, ntap, r0):
    n = CONV_CHUNK + CONV_PAD
    win = xp_ref[pl.ds(pl.multiple_of(CONV_PAD + r0, CONV_PAD), n), :]
    acc = None
    for k in range(ntap):
        ahead = ntap - 1 - k
        shifted = win if ahead == 0 else pltpu.roll(win, n - ahead, 0)
        term = w_ref[k:k + 1, :] * shifted[:CONV_CHUNK]
        acc = term if acc is None else acc + term
    return acc


def _conv_weight_grad(dw_ref, d_chunk, xp_ref, ntap, r0):
    n = CONV_CHUNK + CONV_PAD
    win = xp_ref[pl.ds(r0, n), :]
    for k in range(ntap):
        back = ntap - 1 - k
        shifted = pltpu.roll(win, n - (CONV_PAD - back), 0)
        dw_ref[k * SUB:(k + 1) * SUB, :] += _colsum8(d_chunk * shifted[:CONV_CHUNK])


def _zero_pads(ref, t):
    ref[0:CONV_PAD, :] = jnp.zeros((CONV_PAD, LANE), F32)
    ref[CONV_PAD + t:CONV_PAD + t + CONV_PAD, :] = jnp.zeros((CONV_PAD, LANE), F32)


def _for_chunks(t, fn):
    def step(idx, carry):
        fn(pl.multiple_of(idx * CONV_CHUNK, CONV_CHUNK))
        return carry

    lax.fori_loop(0, t // CONV_CHUNK, step, 0)


def _conv_forward(proj, conf_w, conf_b, short_w, dc, deps=()):
    t = proj.shape[0]
    nc = dc // LANE

    def body(av_ref, ag_ref, bg_ref, cg_ref, v_ref, cw_ref, cb_ref, sw_ref, *rest):
        a1_ref, s_ref, xa, xb = rest[len(deps):]
        _zero_pads(xa, t)
        _zero_pads(xb, t)
        xa[CONV_PAD:CONV_PAD + t, :] = av_ref[...] * _sigmoid(ag_ref[...])
        xb[CONV_PAD:CONV_PAD + t, :] = cg_ref[...] * v_ref[...]

        def chunk(r0):
            rs = pl.ds(r0, CONV_CHUNK)
            a1_ref[rs, :] = _causal_conv(xa, cw_ref, CONF_K, r0) + cb_ref[...]
            s_ref[rs, :] = (bg_ref[rs, :] * _causal_conv(xb, sw_ref, SHORT_K, r0)).astype(BF16)

        _for_chunks(t, chunk)

    col = lambda g: pl.BlockSpec((t, LANE), lambda c, g=g: (0, g * nc + c))
    return pl.pallas_call(
        body, name="conv_forward", grid=(nc,),
        in_specs=[col(0), col(1), col(2), col(3), col(4),
                  pl.BlockSpec((CONF_K, LANE), lambda c: (0, c)),
                  pl.BlockSpec((1, LANE), lambda c: (0, c)),
                  pl.BlockSpec((SHORT_K, LANE), lambda c: (0, c))] + [ANY] * len(deps),
        out_specs=[pl.BlockSpec((t, LANE), lambda c: (0, c)), pl.BlockSpec((t, LANE), lambda c: (0, c))],
        out_shape=[jax.ShapeDtypeStruct((t, dc), F32), jax.ShapeDtypeStruct((t, dc), BF16)],
        scratch_shapes=[pltpu.VMEM((t + 2 * CONV_PAD, LANE), F32), pltpu.VMEM((t + 2 * CONV_PAD, LANE), F32)],
        compiler_params=_params(1),
    )(proj, proj, proj, proj, proj, conf_w, conf_b, short_w, *deps)


def _conv_backward(dproj, proj, da1, ds, conf_w, short_w, dc):
    t = proj.shape[0]
    nc = dc // LANE

    def body(dp_in, av_ref, ag_ref, bg_ref, cg_ref, v_ref, da1_ref, ds_ref, cw_ref, sw_ref,
             dp_ref, dcw_ref, dcb_ref, dsw_ref, xa, xb, da, db, stage, sems):
        del dp_in
        c = pl.program_id(0)
        for ref in (xa, xb, da, db):
            _zero_pads(ref, t)
        xa[CONV_PAD:CONV_PAD + t, :] = av_ref[...] * _sigmoid(ag_ref[...])
        xb[CONV_PAD:CONV_PAD + t, :] = cg_ref[...] * v_ref[...]
        da[CONV_PAD:CONV_PAD + t, :] = da1_ref[...]
        dcw_ref[...] = jnp.zeros(dcw_ref.shape, F32)
        dsw_ref[...] = jnp.zeros(dsw_ref.shape, F32)
        dcb_ref[...] = jnp.zeros(dcb_ref.shape, F32)

        def through_gate(r0):
            rs = pl.ds(r0, CONV_CHUNK)
            ds_c = ds_ref[rs, :]
            stage[2, rs, :] = (ds_c * _causal_conv(xb, sw_ref, SHORT_K, r0)).astype(BF16)
            db[pl.ds(pl.multiple_of(CONV_PAD + r0, CONV_PAD), CONV_CHUNK), :] = ds_c * bg_ref[rs, :]

        _for_chunks(t, through_gate)

        def through_convs(r0):
            rs = pl.ds(r0, CONV_CHUNK)
            da0 = _anticausal_conv(da, cw_ref, CONF_K, r0)
            sg = _sigmoid(ag_ref[rs, :])
            stage[0, rs, :] = (da0 * sg).astype(BF16)
            stage[1, rs, :] = (da0 * av_ref[rs, :] * sg * (1.0 - sg)).astype(BF16)
            dcv = _anticausal_conv(db, sw_ref, SHORT_K, r0)
            stage[3, rs, :] = (dcv * v_ref[rs, :]).astype(BF16)
            stage[4, rs, :] = (dcv * cg_ref[rs, :]).astype(BF16)
            da1_c = da1_ref[rs, :]
            _conv_weight_grad(dcw_ref, da1_c, xa, CONF_K, r0)
            _conv_weight_grad(dsw_ref, ds_ref[rs, :] * bg_ref[rs, :], xb, SHORT_K, r0)
            dcb_ref[...] += _colsum8(da1_c)

        _for_chunks(t, through_convs)
        copies = [pltpu.make_async_copy(
            stage.at[g], dp_ref.at[:, pl.ds(pl.multiple_of((g * nc + c) * LANE, LANE), LANE)], sems.at[g])
            for g in range(5)]
        for cp in copies:
            cp.start()
        for cp in copies:
            cp.wait()

    col = lambda g: pl.BlockSpec((t, LANE), lambda c, g=g: (0, g * nc + c))
    blk = pl.BlockSpec((t, LANE), lambda c: (0, c))
    return pl.pallas_call(
        body, name="conv_backward", grid=(nc,),
        in_specs=[ANY, col(0), col(1), col(2), col(3), col(4), blk, blk,
                  pl.BlockSpec((CONF_K, LANE), lambda c: (0, c)),
                  pl.BlockSpec((SHORT_K, LANE), lambda c: (0, c))],
        out_specs=[ANY,
                   pl.BlockSpec((CONF_K * SUB, LANE), lambda c: (0, c)),
                   pl.BlockSpec((SUB, LANE), lambda c: (0, c)),
                   pl.BlockSpec((SHORT_K * SUB, LANE), lambda c: (0, c))],
        out_shape=[jax.ShapeDtypeStruct(dproj.shape, dproj.dtype),
                   jax.ShapeDtypeStruct((CONF_K * SUB, dc), F32),
                   jax.ShapeDtypeStruct((SUB, dc), F32),
                   jax.ShapeDtypeStruct((SHORT_K * SUB, dc), F32)],
        scratch_shapes=[pltpu.VMEM((t + 2 * CONV_PAD, LANE), F32)] * 4
                       + [pltpu.VMEM((5, t, LANE), BF16), pltpu.SemaphoreType.DMA((5,))],
        input_output_aliases={0: 0},
        compiler_params=_params(1),
    )(dproj, proj, proj, proj, proj, proj, da1, ds, conf_w, short_w)


def _adamw_math(w, g, m, v):
    m = ADAM_B1 * m + (1.0 - ADAM_B1) * g
    v = ADAM_B2 * v + (1.0 - ADAM_B2) * (g * g)
    m_hat = m / (1.0 - ADAM_B1 ** ADAM_STEP)
    v_hat = v / (1.0 - ADAM_B2 ** ADAM_STEP)
    delta = -ADAM_LR * (m_hat / (jnp.sqrt(v_hat) + ADAM_EPS) + ADAM_WD * w)
    return delta, m, v


def _cast_into_slot(name, w, me_arr, deps=()):
    r, c = w.shape
    tr = 256

    def body(me_ref, w_ref, *rest):
        del me_ref
        rest[-1][0] = w_ref[...].astype(BF16)

    return pl.pallas_call(
        body, name=name,
        grid_spec=pltpu.PrefetchScalarGridSpec(
            num_scalar_prefetch=1, grid=(r // tr,),
            in_specs=[pl.BlockSpec((tr, c), lambda i, me: (i, 0))] + [ANY] * len(deps),
            out_specs=pl.BlockSpec((1, tr, c), lambda i, me: (me[0], i, 0))),
        out_shape=jax.ShapeDtypeStruct((N_DEV, r, c), BF16),
        compiler_params=_params(1),
    )(me_arr, w, *deps)


def _chip_sum(name, full, from_sibling, me_arr):
    _, r, c = full.shape
    tr = min(r, 512)

    def body(me_ref, full_ref, sib_ref, sums_ref):
        del me_ref
        sums_ref[0] = (full_ref[0].astype(F32) + sib_ref[0].astype(F32)).astype(BF16)

    other = lambda k, me: (me[0] // 2 + 1 + k) % 4
    return pl.pallas_call(
        body, name=name,
        grid_spec=pltpu.PrefetchScalarGridSpec(
            num_scalar_prefetch=1, grid=(r // tr, 3),
            in_specs=[pl.BlockSpec((1, tr, c), lambda i, k, me: (2 * other(k, me) + me[0] % 2, i, 0)),
                      pl.BlockSpec((1, tr, c), lambda i, k, me: (other(k, me), i, 0))],
            out_specs=pl.BlockSpec((1, tr, c), lambda i, k, me: (other(k, me), i, 0))),
        out_shape=jax.ShapeDtypeStruct((4, r, c), BF16),
        compiler_params=_params(2),
    )(me_arr, full, from_sibling)


def _adamw_shard(name, w, m, v, parts, me_arr, deps=()):
    r, c = w.shape
    tr = min(256, r // len(parts))
    np_ = len(parts)
    per = r // np_ // tr

    def body(me_ref, w_ref, m_ref, v_ref, *rest):
        g_out, d_out, m_out, v_out = rest[5 * np_ + len(deps):]
        g = None
        for p in range(np_):
            gp = rest[5 * p][...]
            for l_ref in rest[5 * p + 1:5 * p + 5]:
                gp = gp + l_ref[0].astype(F32)
            g = gp if g is None else jnp.where(pl.program_id(0) // per == p, gp, g)
        delta, m_new, v_new = _adamw_math(w_ref[...], g, m_ref[...], v_ref[...])
        g_out[...] = g
        d_out[...] = delta
        m_out[...] = m_new
        v_out[...] = v_new

    tile = pl.BlockSpec((tr, c), lambda i, me: (i, 0))
    part_specs, part_args = [], []
    for p, (g_own, from_sibling, landed) in enumerate(parts):
        row = lambda i, p=p: jnp.clip(i - p * per, 0, per - 1)
        part_specs.append(pl.BlockSpec((tr, c), lambda i, me, row=row: (row(i), 0)))
        part_specs += [pl.BlockSpec((1, tr, c), lambda i, me, k=k, row=row: ((me[0] // 2 + k) % 4, row(i), 0))
                       for k in range(4)]
        part_args += [g_own, from_sibling, landed, landed, landed]
    return pl.pallas_call(
        body, name=name,
        grid_spec=pltpu.PrefetchScalarGridSpec(
            num_scalar_prefetch=1, grid=(r // tr,),
            in_specs=[tile] * 3 + part_specs + [ANY] * len(deps), out_specs=[tile] * 4),
        out_shape=[jax.ShapeDtypeStruct((r, c), F32)] * 4,
        compiler_params=_params(1),
    )(me_arr, w, m, v, *part_args, *deps)


SMALL_W = 1024
VEC_ROWS = 16
META_ROW0 = 16
CONF_ROW0 = 64
SHORT_ROW0 = 96
SMALL_ROWS = 104


def _pack_small(vec_parts, dmeta, dcw, dsw, me_arr):
    widths = [p.shape[1] for p in vec_parts]
    nv = len(vec_parts)

    def body(me_ref, *refs):
        del me_ref
        parts, (dmeta_ref, dcw_ref, dsw_ref, out_ref) = refs[:nv], refs[nv:]
        out_ref[0] = jnp.zeros((SMALL_ROWS, SMALL_W), F32)
        row = 0
        for p_ref, wd in zip(parts, widths):
            s = jnp.sum(p_ref[...], axis=0, keepdims=True)
            for h in range(wd // SMALL_W):
                out_ref[0, row:row + 1, :] = s[:, h * SMALL_W:(h + 1) * SMALL_W]
                row += 1
        for h in range(dmeta_ref.shape[1] // SMALL_W):
            out_ref[0, META_ROW0 + h * N_META:META_ROW0 + (h + 1) * N_META, :] = dmeta_ref[:, h * SMALL_W:(h + 1) * SMALL_W]
        for k in range(CONF_K):
            out_ref[0, CONF_ROW0 + k:CONF_ROW0 + k + 1, :] = jnp.sum(dcw_ref[k * SUB:(k + 1) * SUB, :], axis=0, keepdims=True)
        for k in range(SHORT_K):
            out_ref[0, SHORT_ROW0 + k:SHORT_ROW0 + k + 1, :] = jnp.sum(dsw_ref[k * SUB:(k + 1) * SUB, :], axis=0, keepdims=True)

    ins = [*vec_parts, dmeta, dcw, dsw]
    return pl.pallas_call(
        body, name="pack_small",
        grid_spec=pltpu.PrefetchScalarGridSpec(
            num_scalar_prefetch=1, grid=(1,),
            in_specs=[pl.BlockSpec(a.shape, lambda i, me: (0, 0)) for a in ins],
            out_specs=pl.BlockSpec((1, SMALL_ROWS, SMALL_W), lambda i, me: (me[0], 0, 0))),
        out_shape=jax.ShapeDtypeStruct((N_DEV, SMALL_ROWS, SMALL_W), F32),
        compiler_params=_params(1),
    )(me_arr, *ins)


def _small_update(gathered, me_arr, vec_params, meta_p, conf_p, short_p):
    widths = [p[0].shape[1] for p in vec_params]
    nv = len(vec_params)
    mcols = meta_p[0].shape[1]
    per_row = SMALL_W // mcols

    def body(me_ref, gv_ref, gm_ref, gc_ref, gs_ref, *rest):
        del me_ref
        ins, outs = rest[:3 * (nv + 3)], rest[3 * (nv + 3):]

        def total(ref, r0, rows):
            s = ref[0, r0:r0 + rows, :]
            for dev in range(1, N_DEV):
                s = s + ref[dev, r0:r0 + rows, :]
            return s

        grads = []
        row = 0
        for wd in widths:
            pieces = [total(gv_ref, row + h, 1) for h in range(wd // SMALL_W)]
            grads.append(pieces[0] if len(pieces) == 1 else jnp.concatenate(pieces, axis=1))
            row += len(pieces)
        grads.append(total(gm_ref, 0, N_META))
        grads.append(total(gc_ref, 0, CONF_K))
        grads.append(total(gs_ref, 0, SHORT_K))
        for idx, g in enumerate(grads):
            w_ref, m_ref, v_ref = ins[3 * idx:3 * idx + 3]
            delta, m_new, v_new = _adamw_math(w_ref[...], g, m_ref[...], v_ref[...])
            g_out, d_out, m_out, v_out = outs[4 * idx:4 * idx + 4]
            g_out[...] = g
            d_out[...] = delta
            m_out[...] = m_new
            v_out[...] = v_new

    params = list(vec_params) + [meta_p, conf_p, short_p]
    flat = [a for p in params for a in p]
    whole = lambda a: pl.BlockSpec(a.shape, lambda i, me: (0,) * a.ndim)
    outs = pl.pallas_call(
        body, name="small_update",
        grid_spec=pltpu.PrefetchScalarGridSpec(
            num_scalar_prefetch=1, grid=(1,),
            in_specs=[pl.BlockSpec((N_DEV, VEC_ROWS, SMALL_W), lambda i, me: (0, 0, 0)),
                      pl.BlockSpec((N_DEV, N_META, mcols),
                                   lambda i, me: (0, META_ROW0 // N_META + me[0] // per_row, me[0] % per_row)),
                      pl.BlockSpec((N_DEV, 32, LANE), lambda i, me: (0, CONF_ROW0 // 32, me[0])),
                      pl.BlockSpec((N_DEV, SUB, LANE), lambda i, me: (0, SHORT_ROW0 // SUB, me[0]))]
                     + [whole(a) for a in flat],
            out_specs=[whole(p[0]) for p in params for _ in range(4)]),
        out_shape=[jax.ShapeDtypeStruct(p[0].shape, F32) for p in params for _ in range(4)],
        compiler_params=_params(1),
    )(me_arr, gathered, gathered, gathered, gathered, *flat)
    return [tuple(outs[4 * i:4 * i + 4]) for i in range(len(params))]


def kernel(x, meta, g_pre_mix, w_in, b_gates, conf_dw_w, conf_dw_b, conf_ln_g, conf_ln_b, conf_w_pw, short_dw_w, short_w_out, w_o, g_post_mix, g_pre_mlp, w_up, w_down, g_post_mlp, loss_target, m_meta, m_g_pre_mix, m_w_in, m_b_gates, m_conf_dw_w, m_conf_dw_b, m_conf_ln_g, m_conf_ln_b, m_conf_w_pw, m_short_dw_w, m_short_w_out, m_w_o, m_g_post_mix, m_g_pre_mlp, m_w_up, m_w_down, m_g_post_mlp, v_meta, v_g_pre_mix, v_w_in, v_b_gates, v_conf_dw_w, v_conf_dw_b, v_conf_ln_g, v_conf_ln_b, v_conf_w_pw, v_short_dw_w, v_short_w_out, v_w_o, v_g_post_mix, v_g_pre_mlp, v_w_up, v_w_down, v_g_post_mlp):
    seq, d = x.shape[1], x.shape[2]
    dc = conf_w_pw.shape[1]
    t_real = N_META + seq
    t = -(-t_real // ROW_TILE) * ROW_TILE
    tm = t // 2
    assert tm % 16 == 0 and d % 1024 == 0 and dc % 1024 == 0
    x_idx, y_idx, c_idx = _position()
    me_arr = jnp.reshape(4 * x_idx + 2 * y_idx + c_idx, (1,)).astype(jnp.int32)

    big = [w_in[0], conf_w_pw[0], short_w_out[0], w_o[0], w_up[0], w_down[0]]
    big_names = ["w_in", "conf_w_pw", "short_w_out", "w_o", "w_up", "w_down"]
    groups = [[0], [1, 2, 3], [4], [5]]
    meta_g, cw_g, sw_g = _all_gather("gather_small_params", [meta, conf_dw_w[0], short_dw_w[0]])
    ici = []
    for g, idxs in enumerate(groups):
        order = [meta_g] + [st[3] for st in ici[-1:]]
        slots = [_cast_into_slot("cast_" + big_names[i], big[i], me_arr, deps=order) for i in idxs]
        ici.append(_remote_start("gather%d_ici_start" % g, "gather_ici", slots, deps=order))
    started = ici[-1][3][0, 0] * 0.0

    def forward_on(g, after):
        send, recv, bufs, _ = ici[g]
        bufs = _remote_wait("gather%d_ici_wait" % g, "gather_ici", send, recv, bufs, len(bufs), after)
        send, recv, bufs, tok = _remote_start("gather%d_d2d_start" % g, "gather_d2d", bufs)
        return (send, recv, bufs), tok

    def gathered(g, state, after):
        send, recv, bufs = state
        return _remote_wait("gather%d_d2d_wait" % g, "gather_d2d", send, recv, bufs, len(bufs), after)

    unshard =lambda g: jnp.transpose(g, (1, 0, 2)).reshape(g.shape[1], -1)
    meta_full, cw_full, sw_full = unshard(meta_g), unshard(cw_g), unshard(sw_g)

    zrows = jnp.zeros((t - t_real, d), F32) + started
    h0 = jnp.concatenate([meta_full, x[0], zrows], axis=0)
    tgt = jnp.concatenate([jnp.zeros((N_META, d), F32), loss_target[0], zrows], axis=0)
    n = _pre_norm(h0, g_pre_mix)
    fwd0, tok = forward_on(0, [n])
    win_g, = gathered(0, fwd0, [tok])
    proj = _mm_cols("proj", n, win_g, tm=tm)[0]
    fwd1, tok = forward_on(1, [proj])
    a1, s = _conv_forward(proj, cw_full, conf_dw_b, sw_full, dc, deps=[tok])
    a3 = _layer_norm_silu(a1, conf_ln_g, conf_ln_b)
    wpw_g, wso_g, wo_g = gathered(1, fwd1, [a3])
    wo_full = wo_g.reshape(d, d)
    ya = _mm_cols("y_a", a3, wpw_g, tm=tm, nb=N_DEV)[0]
    yb = _mm_cols("y_b", s, wso_g, tm=tm, nb=N_DEV)[0]
    m_mix = _gate_merge(proj, ya, yb, b_gates, d)
    mix = _mm_rows("mix", m_mix, wo_full, tm=tm // 2, tn=d)
    fwd2, tok = forward_on(2, [mix])
    h1, n2 = _post_mix(mix, h0, g_post_mix, g_pre_mlp, deps=[tok])
    wup_g, = gathered(2, fwd2, [n2])

    def up_epilogue(acc):
        r = jnp.maximum(acc, 0.0)
        return r * r, r

    f, relu_up = _mm_cols("mlp_up", n2, wup_g, tm=tm, epilogue=up_epilogue, out_dtypes=(BF16, BF16))
    fwd3, tok = forward_on(3, [f])
    wdn_g, = gathered(3, fwd3, [tok])
    wdn_full = wdn_g.reshape(-1, d)
    fo = _mm_rows("mlp_down", f, wdn_full, tm=tm // 2, tn=512)
    dfo, dh2, dg_post_mlp, loss_blk = _loss_head(fo, h1, tgt, g_post_mlp, t_real)
    loss = lax.psum(loss_blk[0, 0], ("x", "y", "c"))

    def reduce_start(tag, fulls, deps):
        lands = [lax.empty((4,) + g.shape[1:], BF16) for g in fulls]
        send, recv, bufs, tok = _remote_start("reduce_%s_d2d_start" % tag, "reduce_d2d", fulls, lands, deps=deps)
        return (send, recv, bufs), tok

    def reduce_middle(tag, state, owns, after):
        send, recv, bufs = state
        k = len(owns)
        bufs = _remote_wait("reduce_%s_d2d_wait" % tag, "reduce_d2d", send, recv, bufs, k, after)
        from_sibling = bufs[k:]
        sums = [_chip_sum("chip_sum_%s%d" % (tag, i), bufs[i], from_sibling[i], me_arr) for i in range(k)]
        lands = [lax.empty(sm.shape, BF16) for sm in sums]
        send, recv, bufs, tok = _remote_start("reduce_%s_ici_start" % tag, "reduce_ici", sums, lands)
        return (send, recv, bufs, list(zip(owns, from_sibling))), tok

    def reduce_finish(tag, state, after):
        send, recv, bufs, local = state
        k = len(local)
        bufs = _remote_wait("reduce_%s_ici_wait" % tag, "reduce_ici", send, recv, bufs, k, after)
        return [(own, sib, landed) for (own, sib), landed in zip(local, bufs[k:])]

    dup = _mm_nt_blocks("d_up", dfo, wdn_full, tm=tm, tkb=1024, extra=(relu_up,),
                        epilogue=lambda acc, r: (acc * (2.0 * r.astype(F32)),), out_dtypes=(BF16,))[0]
    gw_down, gw_down_own = _mm_tn("dw_down", f, dfo, me_arr, m=f.shape[1], n=d, tma=512, tn=d, sharded="rows")
    red_down, tok = reduce_start("down", [gw_down], ())
    dn2 = _mm_nt_acc("d_n2", dup, wup_g, tm=tm // 2, tn=512, deps=[tok])
    gw_up, gw_up_own = _mm_tn("dw_up", n2, dup, me_arr, m=d, n=dup.shape[1], tma=512, tn=1024, sharded="cols")
    red_down, tok = reduce_middle("down", red_down, [gw_down_own], [dn2])
    red_up, tok = reduce_start("up", [gw_up], [tok])
    dh1, dmix, dg_pre_mlp, dg_post_mix = _mid_norm_bwd(dn2, h1, dh2, mix, g_pre_mlp, g_post_mix, deps=[tok])
    dya, dyb, dproj, db_a, db_b = _gate_backward(dmix, wo_full, proj, ya, yb, b_gates, d, tm // 2)
    db_gates = jnp.concatenate([db_a, db_b], axis=1)
    red_up, tok = reduce_middle("up", red_up, [gw_up_own], [dya])
    gw_o, gw_o_own = _mm_tn("dw_o", m_mix, dmix, me_arr, m=d, n=d, tma=d // N_DEV, tn=d, sharded="rows", deps=[tok])
    da3 = _mm_nt_acc("d_a3", dya, wpw_g, tm=tm, tn=512)
    gw_pw, gw_pw_own = _mm_tn("dw_pw", a3, dya, me_arr, m=dc, n=d, tma=512, tn=d, sharded="cols")
    dsb = _mm_nt_acc("d_s", dyb, wso_g, tm=tm, tn=512)
    gw_so, gw_so_own = _mm_tn("dw_so", s, dyb, me_arr, m=dc, n=d, tma=512, tn=d, sharded="cols")
    red_mix, tok = reduce_start("mix", [gw_pw, gw_so, gw_o], ())
    da1, dln_g, dln_b = _layer_norm_silu_bwd(da3, a1, conf_ln_g, conf_ln_b, deps=[tok])
    dproj, dcw, dcb, dsw = _conv_backward(dproj, proj, da1, dsb, cw_full, sw_full, dc)
    red_mix, tok = reduce_middle("mix", red_mix, [gw_pw_own, gw_so_own, gw_o_own], [dcb])
    in_cb = w_in.shape[2]
    half = d // 2
    red_in = []
    for part in range(2):
        gw, own = _mm_tn("dw_in%d" % part, n, dproj, me_arr, m=half, n=proj.shape[1], tma=512, tn=2 * in_cb,
                         sharded="cols", a_off=part * (half // 512), deps=[tok])
        state, tok = reduce_start("in%d" % part, [gw], ())
        red_in.append((state, own))
    for part in range(2):
        state, own = red_in[part]
        red_in[part], tok = reduce_middle("in%d" % part, state, [own], [tok])
    dn = _mm_nt_acc("d_n", dproj, win_g, tm=tm // 2, tn=512, deps=[tok])
    dh0, dg_pre_mix = _pre_norm_bwd(dn, h0, dh1, g_pre_mix)
    grad_x = dh0[N_META:t_real][None]

    vec_parts = [dg_pre_mix, db_gates, dcb, dln_g, dln_b, dg_post_mix, dg_pre_mlp, dg_post_mlp]
    packed = _pack_small(vec_parts, dh0[:N_META], dcw, dsw, me_arr)
    send, recv, bufs, tok = _remote_start("small_grads_ici_start", "gather_ici", [packed])
    vec_names = ["g_pre_mix", "b_gates", "conf_dw_b", "conf_ln_g", "conf_ln_b", "g_post_mix", "g_pre_mlp", "g_post_mlp"]
    env = locals()
    results = {}

    def update(nm, parts, deps=()):
        res = _adamw_shard("adamw_" + nm, env[nm][0], env["m_" + nm][0], env["v_" + nm][0], parts, me_arr, deps=deps)
        results[nm] = tuple(r[None] for r in res)
        return res[0]

    done = [update("w_down", reduce_finish("down", red_down, [tok]), deps=[tok])]
    done.append(update("w_up", reduce_finish("up", red_up, done)))
    bufs = _remote_wait("small_grads_ici_wait", "gather_ici", send, recv, bufs, 1, done)
    send, recv, bufs, tok = _remote_start("small_grads_d2d_start", "gather_d2d", bufs)
    for nm, pair in zip(["conf_w_pw", "short_w_out", "w_o"], reduce_finish("mix", red_mix, [tok])):
        done.append(update(nm, [pair], deps=[tok]))
    small_g, = _remote_wait("small_grads_d2d_wait", "gather_d2d", send, recv, bufs, 1, done)
    triple = lambda nm, sq: tuple(env[p + nm][0] if sq else env[p + nm] for p in ("", "m_", "v_"))
    small = _small_update(small_g, me_arr, [triple(nm, False) for nm in vec_names],
                          triple("meta", False), triple("conf_dw_w", True), triple("short_dw_w", True))
    for nm, res in zip(vec_names + ["meta"], small[:len(vec_names) + 1]):
        results[nm] = res
    results["conf_dw_w"] = tuple(r[None] for r in small[-2])
    results["short_dw_w"] = tuple(r[None] for r in small[-1])
    update("w_in", [reduce_finish("in%d" % part, red_in[part], [small[0][0]])[0] for part in range(2)])

    order = ["meta", "g_pre_mix", "w_in", "b_gates", "conf_dw_w", "conf_dw_b", "conf_ln_g", "conf_ln_b", "conf_w_pw",
             "short_dw_w", "short_w_out", "w_o", "g_post_mix", "g_pre_mlp", "w_up", "w_down", "g_post_mlp"]
    return (loss, grad_x, *[results[nm][0] for nm in order], *[results[nm][1] for nm in order],
            *[results[nm][2] for nm in order], *[results[nm][3] for nm in order])
```

```python
import jax
import jax.numpy as jnp
from jax import lax
from jax.experimental import pallas as pl
from jax.experimental.pallas import tpu as pltpu

N_DEV = 8
N_META = 16
CONF_K = 31
SHORT_K = 3
RMS_EPS = 1e-6
LN_EPS = 1e-5
ADAM_LR = 0.001
ADAM_B1 = 0.9
ADAM_B2 = 0.999
ADAM_EPS = 1e-08
ADAM_WD = 0.01
ADAM_STEP = 10

LANE = 128
SUB = 8
ROW_TILE = 128
CONV_PAD = 32
CONV_CHUNK = 128
VMEM_LIMIT = 56 * 1024 * 1024

F32 = jnp.float32
BF16 = jnp.bfloat16
MESH = pl.DeviceIdType.MESH
ANY = pl.BlockSpec(memory_space=pl.ANY)
HBM_SPEC = pl.BlockSpec(memory_space=pltpu.HBM)
SEM_SPEC = pl.BlockSpec(memory_space=pltpu.SEMAPHORE)
EFFECT = pltpu.SideEffectType.DATAFLOW_SIDE_EFFECTING


def _params(n_axes):
    return pltpu.CompilerParams(dimension_semantics=("arbitrary",) * n_axes, vmem_limit_bytes=VMEM_LIMIT)


def _sigmoid(z):
    return 1.0 / (1.0 + jnp.exp(-z))


def _colsum8(v):
    r, c = v.shape
    return jnp.sum(v.reshape(r // SUB, SUB, c), axis=0)


def _position():
    x, y, c = lax.axis_index("x"), lax.axis_index("y"), lax.axis_index("c")
    return x, y, c


def _flat(p):
    return 4 * p[0] + 2 * p[1] + p[2]


def _all_gather(name, shards, deps=()):
    n, nd = len(shards), len(deps)

    def body(*refs):
        ins, outs = refs[:n], refs[n + nd:2 * n + nd]
        send_sems, recv_sems, local_sems = refs[2 * n + nd:]
        x, y, c = _position()
        me, sibling = (x, y, c), (x, y, 1 - c)
        chips = [(1 - x, y), (x, 1 - y), (1 - x, 1 - y)]

        def copy(q, k, block, to, src=None):
            dst = outs[q].at[_flat(block)]
            return pltpu.make_async_remote_copy(
                src_ref=dst if src is None else src, dst_ref=dst,
                send_sem=send_sems.at[q, k], recv_sem=recv_sems.at[q, k],
                device_id=to, device_id_type=MESH)

        mine = [pltpu.make_async_copy(ins[q], outs[q].at[_flat(me)], local_sems.at[q]) for q in range(n)]
        for cp in mine:
            cp.start()
        first = []
        for q in range(n):
            first.append(copy(q, 0, me, sibling, src=ins[q]))
            for j, chip in enumerate(chips):
                first.append(copy(q, 1 + j, me, (*chip, c), src=ins[q]))
        for cp in first:
            cp.start()
        passed = []
        for q in range(n):
            for j, chip in enumerate(chips):
                copy(q, 1 + j, (*chip, c), me).wait_recv()
                fwd = copy(q, 4 + j, (*chip, c), sibling)
                fwd.start()
                passed.append(fwd)
        for q in range(n):
            copy(q, 0, sibling, me).wait_recv()
            for j, chip in enumerate(chips):
                copy(q, 4 + j, (*chip, 1 - c), me).wait_recv()
        for cp in first + passed:
            cp.wait_send()
        for cp in mine:
            cp.wait()

    return pl.pallas_call(
        body, name=name,
        in_specs=[ANY] * (n + nd), out_specs=[ANY] * n,
        out_shape=[jax.ShapeDtypeStruct((N_DEV,) + s.shape, s.dtype) for s in shards],
        scratch_shapes=[pltpu.SemaphoreType.DMA((n, 7)), pltpu.SemaphoreType.DMA((n, 7)),
                        pltpu.SemaphoreType.DMA((n,))],
    )(*shards, *deps)


N_COPIES = {"gather_ici": 4, "gather_d2d": 3, "gather_direct": 3, "gather_relay": 3, "gather_diag": 1,
            "reduce_d2d": 4, "reduce_ici": 3}


def _copy_plan(kind):
    x, y, c = _position()
    me, sibling = (x, y, c), (x, y, 1 - c)
    chips = [(1 - x, y), (x, 1 - y), (1 - x, 1 - y)]
    if kind == "gather_ici":
        return [(_flat(me), _flat(me), sibling)] + [(_flat(me), _flat(me), (*ch, c)) for ch in chips]
    if kind == "gather_d2d":
        return [(_flat((*ch, c)), _flat((*ch, c)), sibling) for ch in chips]
    if kind == "gather_direct":
        return [(_flat(me), _flat(me), sibling)] + [(_flat(me), _flat(me), (*ch, c)) for ch in chips[:2]]
    if kind == "gather_relay":
        held, to = (x ^ (1 - c), y ^ c, c), (x ^ c, y ^ (1 - c), c)
        return [(_flat(held), _flat(held), to)] + [(_flat((*ch, c)), _flat((*ch, c)), sibling) for ch in chips[:2]]
    if kind == "gather_diag":
        return [(_flat((*chips[2], c)), _flat((*chips[2], c)), sibling)]
    if kind == "reduce_d2d":
        return [(2 * chip + (1 - c), chip, sibling) for chip in range(4)]
    return [(2 * ch[0] + ch[1], 2 * x + y, (*ch, c)) for ch in chips]


def _planned_copies(kind, srcs, dsts, send_sems, recv_sems):
    plan = _copy_plan(kind)
    return [pltpu.make_async_remote_copy(
        src_ref=src.at[s_slot], dst_ref=dst.at[d_slot],
        send_sem=send_sems.at[q * len(plan) + k], recv_sem=recv_sems.at[q * len(plan) + k],
        device_id=to, device_id_type=MESH)
        for q, (src, dst) in enumerate(zip(srcs, dsts)) for k, (s_slot, d_slot, to) in enumerate(plan)]


def _remote_start(name, kind, srcs, lands=None, deps=()):
    n = len(srcs)
    bufs = list(srcs) + ([] if lands is None else list(lands))
    nb, nd = len(bufs), len(deps)
    nsem = n * N_COPIES[kind]

    def body(*refs):
        ins = refs[:nb]
        send_sems, recv_sems = refs[nb + nd], refs[nb + nd + 1]
        token = refs[-1]
        for cp in _planned_copies(kind, ins[:n], ins[:n] if lands is None else ins[n:], send_sems, recv_sems):
            cp.start()
        token[...] = jnp.zeros_like(token)

    outs = pl.pallas_call(
        body, name=name,
        out_shape=(pltpu.SemaphoreType.DMA((nsem,)), pltpu.SemaphoreType.DMA((nsem,)),
                   *[pltpu.HBM(b.shape, b.dtype) for b in bufs], jax.ShapeDtypeStruct((SUB, LANE), F32)),
        in_specs=[HBM_SPEC] * nb + [ANY] * nd,
        out_specs=(SEM_SPEC, SEM_SPEC, *[HBM_SPEC] * nb, pl.BlockSpec(memory_space=pltpu.VMEM)),
        input_output_aliases={i: 2 + i for i in range(nb)},
        compiler_params=pltpu.CompilerParams(has_side_effects=EFFECT),
    )(*[pltpu.with_memory_space_constraint(b, pltpu.HBM) for b in bufs], *deps)
    return outs[0], outs[1], list(outs[2:2 + nb]), outs[-1]


def _remote_wait(name, kind, send_sems, recv_sems, bufs, n, after):
    nb, na = len(bufs), len(after)
    same = nb == n

    def body(*refs):
        ins = refs[:nb]
        sends, recvs = refs[nb], refs[nb + 1]
        for cp in _planned_copies(kind, ins[:n], ins[:n] if same else ins[n:], sends, recvs):
            cp.wait_send()
            cp.wait_recv()

    outs = pl.pallas_call(
        body, name=name,
        out_shape=[pltpu.HBM(b.shape, b.dtype) for b in bufs],
        in_specs=[HBM_SPEC] * nb + [SEM_SPEC, SEM_SPEC] + [ANY] * na,
        out_specs=[HBM_SPEC] * nb,
        input_output_aliases={i: i for i in range(nb)},
        compiler_params=pltpu.CompilerParams(has_side_effects=EFFECT),
    )(*bufs, send_sems, recv_sems, *after)
    return list(outs)


def _mm_cols(name, a, w, *, tm, nb=1, epilogue=None, out_dtypes=(F32,)):
    t, k = a.shape
    nblk, _, cb = w.shape

    def body(a_ref, w_ref, *o_refs):
        av = a_ref[...]
        for b in range(nb):
            acc = jnp.dot(av, w_ref[b], preferred_element_type=F32)
            outs = (acc,) if epilogue is None else epilogue(acc)
            for o_ref, o in zip(o_refs, outs):
                o_ref[:, b * cb:(b + 1) * cb] = o.astype(o_ref.dtype)

    return pl.pallas_call(
        body, name=name, grid=(nblk // nb, t // tm),
        in_specs=[pl.BlockSpec((tm, k), lambda j, i: (i, 0)),
                  pl.BlockSpec((nb, k, cb), lambda j, i: (j, 0, 0))],
        out_specs=[pl.BlockSpec((tm, nb * cb), lambda j, i: (i, j)) for _ in out_dtypes],
        out_shape=[jax.ShapeDtypeStruct((t, nblk * cb), dt) for dt in out_dtypes],
        compiler_params=_params(2),
    )(a, w)


def _mm_rows(name, a, w2d, *, tm, tn):
    t, kf = a.shape
    n = w2d.shape[1]

    def body(a_ref, w_ref, o_ref):
        o_ref[...] = jnp.dot(a_ref[...], w_ref[...], preferred_element_type=F32)

    return pl.pallas_call(
        body, name=name, grid=(t // tm, n // tn),
        in_specs=[pl.BlockSpec((tm, kf), lambda i, j: (i, 0)),
                  pl.BlockSpec((kf, tn), lambda i, j: (0, j))],
        out_specs=pl.BlockSpec((tm, tn), lambda i, j: (i, j)),
        out_shape=jax.ShapeDtypeStruct((t, n), F32),
        compiler_params=_params(2),
    )(a, w2d)


def _mm_nt_acc(name, dy, w, *, tm, tn, col_off=0, deps=()):
    t = dy.shape[0]
    nblk, k, cb = w.shape

    def body(dy_ref, w_ref, *rest):
        acc = None
        for b in range(nblk):
            d = lax.dot_general(dy_ref[:, b * cb:(b + 1) * cb], w_ref[b], (((1,), (1,)), ((), ())),
                                preferred_element_type=F32)
            acc = d if acc is None else acc + d
        rest[-1][...] = acc

    return pl.pallas_call(
        body, name=name, grid=(t // tm, k // tn),
        in_specs=[pl.BlockSpec((tm, nblk * cb), lambda i, j: (i, col_off)),
                  pl.BlockSpec((nblk, tn, cb), lambda i, j: (0, j, 0))] + [ANY] * len(deps),
        out_specs=pl.BlockSpec((tm, tn), lambda i, j: (i, j)),
        out_shape=jax.ShapeDtypeStruct((t, k), F32),
        compiler_params=_params(2),
    )(dy, w, *deps)


def _mm_nt_blocks(name, dy, w2d, *, tm, tkb, extra=(), epilogue=None, out_dtypes=(F32,)):
    t, n = dy.shape
    kf = w2d.shape[0]
    ne = len(extra)

    def body(dy_ref, w_ref, *rest):
        acc = lax.dot_general(dy_ref[...], w_ref[...], (((1,), (1,)), ((), ())), preferred_element_type=F32)
        outs = (acc,) if epilogue is None else epilogue(acc, *[e[...] for e in rest[:ne]])
        for o_ref, o in zip(rest[ne:], outs):
            o_ref[...] = o.astype(o_ref.dtype)

    return pl.pallas_call(
        body, name=name, grid=(kf // tkb, t // tm),
        in_specs=[pl.BlockSpec((tm, n), lambda kb, i: (i, 0)),
                  pl.BlockSpec((tkb, n), lambda kb, i: (kb, 0))]
                 + [pl.BlockSpec((tm, tkb), lambda kb, i: (i, kb)) for _ in extra],
        out_specs=[pl.BlockSpec((tm, tkb), lambda kb, i: (i, kb)) for _ in out_dtypes],
        out_shape=[jax.ShapeDtypeStruct((t, kf), dt) for dt in out_dtypes],
        compiler_params=_params(2),
    )(dy, w2d, *extra)


def _mm_tn(name, a, b, me_arr, *, m, n, tma, tn, sharded, a_off=0, b_off=0, deps=()):
    t = a.shape[0]
    if sharded == "cols":
        cb = n // N_DEV
        nb, q = max(tn // cb, 1), max(cb // tn, 1)
        tw = tn // nb
        full_shape, own_shape = (N_DEV, m, cb), (m, cb)
        full_spec = pl.BlockSpec((nb, tma, tw), lambda i, j, me: (j // q, i, j % q))
    else:
        kb = m // N_DEV
        p = kb // tma
        nb, tw = 1, tn
        full_shape, own_shape = (m, n), (kb, n)
        full_spec = pl.BlockSpec((tma, tn), lambda i, j, me: (i, j))

    def body(me_ref, a_ref, b_ref, *rest):
        full_ref, own_ref, stage, sem = rest[len(deps):]
        i, j = pl.program_id(0), pl.program_id(1)
        acc = lax.dot_general(a_ref[...], b_ref[...], (((0,), (0,)), ((), ())), preferred_element_type=F32)
        for blk in range(nb):
            part = acc[:, blk * tw:(blk + 1) * tw]
            if sharded == "cols":
                full_ref[blk] = part.astype(BF16)
                owner, r0, c0 = (j // q) * nb + blk, i * tma, (j % q) * tw
            else:
                full_ref[...] = part.astype(BF16)
                owner, r0, c0 = i // p, (i % p) * tma, j * tn

            @pl.when(owner == me_ref[0])
            def _():
                stage[...] = part
                cp = pltpu.make_async_copy(
                    stage, own_ref.at[pl.ds(pl.multiple_of(r0, tma), tma), pl.ds(pl.multiple_of(c0, tw), tw)], sem)
                cp.start()
                cp.wait()

    full, own = pl.pallas_call(
        body, name=name,
        grid_spec=pltpu.PrefetchScalarGridSpec(
            num_scalar_prefetch=1, grid=(m // tma, n // tn),
            in_specs=[pl.BlockSpec((t, tma), lambda i, j, me: (0, a_off + i)),
                      pl.BlockSpec((t, tn), lambda i, j, me: (0, b_off + j))] + [ANY] * len(deps),
            out_specs=[full_spec, ANY],
            scratch_shapes=[pltpu.VMEM((tma, tw), F32), pltpu.SemaphoreType.DMA(())]),
        out_shape=[jax.ShapeDtypeStruct(full_shape, BF16), jax.ShapeDtypeStruct(own_shape, F32)],
        compiler_params=_params(2),
    )(me_arr, a, b, *deps)
    if sharded == "rows":
        full = full.reshape(N_DEV, m // N_DEV, n)
    return full, own


def _row_tile(t):
    return t // 8 if (t // 8) % 16 == 0 else ROW_TILE


def _row_call(name, body, t, row_ins, full_ins, row_outs, acc_outs, scratch=(), deps=()):
    tm = _row_tile(t)
    nin = len(row_ins) + len(full_ins)

    def without_deps(*refs):
        body(*refs[:nin], *refs[nin + len(deps):])

    return pl.pallas_call(
        without_deps, name=name, grid=(t // tm,),
        in_specs=[pl.BlockSpec((tm, a.shape[1]), lambda i: (i, 0)) for a in row_ins]
                 + [pl.BlockSpec(a.shape, lambda i: (0, 0)) for a in full_ins] + [ANY] * len(deps),
        out_specs=[pl.BlockSpec((tm, c), lambda i: (i, 0)) for c, _ in row_outs]
                  + [pl.BlockSpec((r, c), lambda i: (0, 0)) for r, c in acc_outs],
        out_shape=[jax.ShapeDtypeStruct((t, c), dt) for c, dt in row_outs]
                  + [jax.ShapeDtypeStruct((r, c), F32) for r, c in acc_outs],
        scratch_shapes=list(scratch),
        compiler_params=_params(1),
    )(*row_ins, *full_ins, *deps)


def _accumulate(ref, v):
    @pl.when(pl.program_id(0) == 0)
    def _():
        ref[...] = v

    @pl.when(pl.program_id(0) > 0)
    def _():
        ref[...] += v


def _rms(v):
    return lax.rsqrt(jnp.mean(v * v, axis=-1, keepdims=True) + RMS_EPS)


def _rms_bwd(dout, u, r, g):
    du = dout * g
    dx = r * (du - u * jnp.mean(du * u, axis=-1, keepdims=True))
    return dx, _colsum8(dout * u)


def _pre_norm(h0, g):
    t, d = h0.shape

    def body(h_ref, g_ref, n_ref):
        h = h_ref[...]
        n_ref[...] = (h * _rms(h) * g_ref[...]).astype(BF16)

    return _row_call("pre_norm", body, t, [h0], [g], [(d, BF16)], [])[0]


def _post_mix(mix, h0, g_post, g_pre, deps=()):
    t, d = h0.shape

    def body(mix_ref, h0_ref, gp_ref, gq_ref, h1_ref, n2_ref):
        mix_v = mix_ref[...]
        h1 = h0_ref[...] + mix_v * _rms(mix_v) * gp_ref[...]
        h1_ref[...] = h1
        n2_ref[...] = (h1 * _rms(h1) * gq_ref[...]).astype(BF16)

    return _row_call("post_mix", body, t, [mix, h0], [g_post, g_pre], [(d, F32), (d, BF16)], [], deps=deps)


def _loss_head(fo, h1, tgt, g_post_mlp, t_real):
    t, d = h1.shape
    tile = _row_tile(t)

    def body(fo_ref, h1_ref, tgt_ref, g_ref, dfo_ref, dh2_ref, dg_ref, loss_ref, lacc):
        i = pl.program_id(0)
        fo_v = fo_ref[...]
        g = g_ref[...]
        r = _rms(fo_v)
        u = fo_v * r
        h2 = h1_ref[...] + u * g
        row = i * tile + lax.broadcasted_iota(jnp.int32, (tile, 1), 0)
        valid = jnp.logical_and(row >= N_META, row < t_real)
        diff = jnp.where(valid, h2 - tgt_ref[...], 0.0)
        dh2 = diff * (1.0 / d)
        dh2_ref[...] = dh2
        dfo, dg = _rms_bwd(dh2, u, r, g)
        dfo_ref[...] = dfo.astype(BF16)
        _accumulate(dg_ref, dg)
        _accumulate(lacc, _colsum8(diff * diff))

        @pl.when(i == pl.num_programs(0) - 1)
        def _():
            loss_ref[...] = jnp.full((SUB, LANE), (0.5 / d) * jnp.sum(lacc[...]), F32)

    return _row_call("loss_head", body, t, [fo, h1, tgt], [g_post_mlp],
                     [(d, BF16), (d, F32)], [(SUB, d), (SUB, LANE)], scratch=[pltpu.VMEM((SUB, d), F32)])


def _mid_norm_bwd(dn2, h1, dh2, mix, g_pre_mlp, g_post_mix, deps=()):
    t, d = h1.shape

    def body(dn2_ref, h1_ref, dh2_ref, mix_ref, gq_ref, gp_ref, dh1_ref, dmix_ref, dgq_ref, dgp_ref):
        h1 = h1_ref[...]
        r3 = _rms(h1)
        dx, dgq = _rms_bwd(dn2_ref[...], h1 * r3, r3, gq_ref[...])
        dh1 = dh2_ref[...] + dx
        dh1_ref[...] = dh1
        mix_v = mix_ref[...]
        r2 = _rms(mix_v)
        dmix, dgp = _rms_bwd(dh1, mix_v * r2, r2, gp_ref[...])
        dmix_ref[...] = dmix.astype(BF16)
        _accumulate(dgq_ref, dgq)
        _accumulate(dgp_ref, dgp)

    return _row_call("mid_norm_bwd", body, t, [dn2, h1, dh2, mix], [g_pre_mlp, g_post_mix],
                     [(d, F32), (d, BF16)], [(SUB, d), (SUB, d)], deps=deps)


def _pre_norm_bwd(dn, h0, dh1, g_pre_mix, deps=()):
    t, d = h0.shape

    def body(dn_ref, h0_ref, dh1_ref, g_ref, dh0_ref, dg_ref):
        h0 = h0_ref[...]
        r = _rms(h0)
        dx, dg = _rms_bwd(dn_ref[...], h0 * r, r, g_ref[...])
        dh0_ref[...] = dh1_ref[...] + dx
        _accumulate(dg_ref, dg)

    return _row_call("pre_norm_bwd", body, t, [dn, h0, dh1], [g_pre_mix], [(d, F32)], [(SUB, d)], deps=deps)


def _layer_norm_silu(a1, ln_g, ln_b):
    t, c = a1.shape

    def body(a1_ref, g_ref, b_ref, a3_ref):
        a = a1_ref[...]
        mu = jnp.mean(a, axis=-1, keepdims=True)
        xc = a - mu
        rstd = lax.rsqrt(jnp.mean(xc * xc, axis=-1, keepdims=True) + LN_EPS)
        z = xc * rstd * g_ref[...] + b_ref[...]
        a3_ref[...] = (z * _sigmoid(z)).astype(BF16)

    return _row_call("layer_norm_silu", body, t, [a1], [ln_g, ln_b], [(c, BF16)], [])[0]


def _layer_norm_silu_bwd(da3, a1, ln_g, ln_b, deps=()):
    t, c = a1.shape

    def body(da3_ref, a1_ref, g_ref, b_ref, da1_ref, dg_ref, db_ref):
        a = a1_ref[...]
        g = g_ref[...]
        mu = jnp.mean(a, axis=-1, keepdims=True)
        xc = a - mu
        rstd = lax.rsqrt(jnp.mean(xc * xc, axis=-1, keepdims=True) + LN_EPS)
        xhat = xc * rstd
        z = xhat * g + b_ref[...]
        sg = _sigmoid(z)
        dz = da3_ref[...] * (sg * (1.0 + z * (1.0 - sg)))
        dxhat = dz * g
        da1_ref[...] = rstd * (dxhat - jnp.mean(dxhat, axis=-1, keepdims=True)
                               - xhat * jnp.mean(dxhat * xhat, axis=-1, keepdims=True))
        _accumulate(dg_ref, _colsum8(dz * xhat))
        _accumulate(db_ref, _colsum8(dz))

    return _row_call("layer_norm_silu_bwd", body, t, [da3, a1], [ln_g, ln_b], [(c, F32)], [(SUB, c), (SUB, c)], deps=deps)


def _gate_merge(proj, ya, yb, b_gates, d, deps=()):
    t = proj.shape[0]
    w = 1024
    nh = d // w
    ga0 = (proj.shape[1] - 2 * d) // w

    def body(pa_ref, pb_ref, ya_ref, yb_ref, ba_ref, bb_ref, *rest):
        m_ref = rest[-1]
        ga = _sigmoid(pa_ref[...] + ba_ref[...])
        gb = _sigmoid(pb_ref[...] + bb_ref[...])
        m_ref[...] = (ga * ya_ref[...] + gb * yb_ref[...]).astype(BF16)

    tm = _row_tile(t)
    return pl.pallas_call(
        body, name="gate_merge", grid=(nh, t // tm),
        in_specs=[pl.BlockSpec((tm, w), lambda h, i: (i, ga0 + h)),
                  pl.BlockSpec((tm, w), lambda h, i: (i, ga0 + nh + h)),
                  pl.BlockSpec((tm, w), lambda h, i: (i, h)),
                  pl.BlockSpec((tm, w), lambda h, i: (i, h)),
                  pl.BlockSpec((1, w), lambda h, i: (0, h)),
                  pl.BlockSpec((1, w), lambda h, i: (0, nh + h))] + [ANY] * len(deps),
        out_specs=pl.BlockSpec((tm, w), lambda h, i: (i, h)),
        out_shape=jax.ShapeDtypeStruct((t, d), BF16),
        compiler_params=_params(2),
    )(proj, proj, ya, yb, b_gates, b_gates, *deps)


def _gate_backward(dmix, wo_full, proj, ya, yb, b_gates, d, tm, deps=()):
    t, cols = proj.shape
    w = 1024
    nh = d // w
    ga0 = (cols - 2 * d) // w

    def body(dmix_ref, wo_ref, pa_ref, pb_ref, ya_ref, yb_ref, ba_ref, bb_ref, *rest):
        dya_ref, dyb_ref, dp_ref, dba_ref, dbb_ref, stage, sems = rest[len(deps):]
        h, i = pl.program_id(0), pl.program_id(1)
        dm = lax.dot_general(dmix_ref[...], wo_ref[...], (((1,), (1,)), ((), ())), preferred_element_type=F32)
        ga = _sigmoid(pa_ref[...] + ba_ref[...])
        gb = _sigmoid(pb_ref[...] + bb_ref[...])
        dya_ref[...] = (dm * ga).astype(BF16)
        dyb_ref[...] = (dm * gb).astype(BF16)
        dpa = dm * ya_ref[...] * ga * (1.0 - ga)
        dpb = dm * yb_ref[...] * gb * (1.0 - gb)
        stage[0] = dpa.astype(BF16)
        stage[1] = dpb.astype(BF16)
        rows = pl.ds(pl.multiple_of(i * tm, tm), tm)
        copies = [pltpu.make_async_copy(
            stage.at[g], dp_ref.at[rows, pl.ds(pl.multiple_of((ga0 + g * nh + h) * w, w), w)], sems.at[g])
            for g in range(2)]
        for cp in copies:
            cp.start()

        @pl.when(i == 0)
        def _():
            dba_ref[...] = _colsum8(dpa)
            dbb_ref[...] = _colsum8(dpb)

        @pl.when(i > 0)
        def _():
            dba_ref[...] += _colsum8(dpa)
            dbb_ref[...] += _colsum8(dpb)

        for cp in copies:
            cp.wait()

    tile = pl.BlockSpec((tm, w), lambda h, i: (i, h))
    return pl.pallas_call(
        body, name="gate_backward", grid=(nh, t // tm),
        in_specs=[pl.BlockSpec((tm, d), lambda h, i: (i, 0)),
                  pl.BlockSpec((w, d), lambda h, i: (h, 0)),
                  pl.BlockSpec((tm, w), lambda h, i: (i, ga0 + h)),
                  pl.BlockSpec((tm, w), lambda h, i: (i, ga0 + nh + h)),
                  tile, tile,
                  pl.BlockSpec((1, w), lambda h, i: (0, h)),
                  pl.BlockSpec((1, w), lambda h, i: (0, nh + h))] + [ANY] * len(deps),
        out_specs=[tile, tile, ANY,
                   pl.BlockSpec((SUB, w), lambda h, i: (0, h)),
                   pl.BlockSpec((SUB, w), lambda h, i: (0, h))],
        out_shape=[jax.ShapeDtypeStruct((t, d), BF16), jax.ShapeDtypeStruct((t, d), BF16),
                   jax.ShapeDtypeStruct((t, cols), BF16),
                   jax.ShapeDtypeStruct((SUB, d), F32), jax.ShapeDtypeStruct((SUB, d), F32)],
        scratch_shapes=[pltpu.VMEM((2, tm, w), BF16), pltpu.SemaphoreType.DMA((2,))],
        compiler_params=_params(2),
    )(dmix, wo_full, proj, proj, ya, yb, b_gates, b_gates, *deps)


def _causal_conv(xp_ref, w_ref, ntap, r0):
    n = CONV_CHUNK + CONV_PAD
    win = xp_ref[pl.ds(r0, n), :]
    acc = None
    for k in range(ntap):
        back = ntap - 1 - k
        shifted = pltpu.roll(win, n - (CONV_PAD - back), 0)
        term = w_ref[k:k + 1, :] * shifted[:CONV_CHUNK]
        acc = term if acc is None else acc + term
    return acc


def _anticausal_conv(xp_ref, w_ref, ntap, r0):
    n = CONV_CHUNK + CONV_PAD
    win = xp_ref[pl.ds(pl.multiple_of(CONV_PAD + r0, CONV_PAD), n), :]
    acc = None
    for k in range(ntap):
        ahead = ntap - 1 - k
        shifted = win if ahead == 0 else pltpu.roll(win, n - ahead, 0)
        term = w_ref[k:k + 1, :] * shifted[:CONV_CHUNK]
        acc = term if acc is None else acc + term
    return acc


def _conv_weight_grad(dw_ref, d_chunk, xp_ref, ntap, r0):
    n = CONV_CHUNK + CONV_PAD
    win = xp_ref[pl.ds(r0, n), :]
    for k in range(ntap):
        back = ntap - 1 - k
        shifted = pltpu.roll(win, n - (CONV_PAD - back), 0)
        dw_ref[k * SUB:(k + 1) * SUB, :] += _colsum8(d_chunk * shifted[:CONV_CHUNK])


def _zero_pads(ref, t):
    ref[0:CONV_PAD, :] = jnp.zeros((CONV_PAD, LANE), F32)
    ref[CONV_PAD + t:CONV_PAD + t + CONV_PAD, :] = jnp.zeros((CONV_PAD, LANE), F32)


def _for_chunks(t, fn):
    def step(idx, carry):
        fn(pl.multiple_of(idx * CONV_CHUNK, CONV_CHUNK))
        return carry

    lax.fori_loop(0, t // CONV_CHUNK, step, 0)


def _conv_forward(proj, conf_w, conf_b, short_w, dc, deps=()):
    t = proj.shape[0]
    nc = dc // LANE

    def body(av_ref, ag_ref, bg_ref, cg_ref, v_ref, cw_ref, cb_ref, sw_ref, *rest):
        a1_ref, s_ref, xa, xb = rest[len(deps):]
        _zero_pads(xa, t)
        _zero_pads(xb, t)
        xa[CONV_PAD:CONV_PAD + t, :] = av_ref[...] * _sigmoid(ag_ref[...])
        xb[CONV_PAD:CONV_PAD + t, :] = cg_ref[...] * v_ref[...]

        def chunk(r0):
            rs = pl.ds(r0, CONV_CHUNK)
            a1_ref[rs, :] = _causal_conv(xa, cw_ref, CONF_K, r0) + cb_ref[...]
            s_ref[rs, :] = (bg_ref[rs, :] * _causal_conv(xb, sw_ref, SHORT_K, r0)).astype(BF16)

        _for_chunks(t, chunk)

    col = lambda g: pl.BlockSpec((t, LANE), lambda c, g=g: (0, g * nc + c))
    return pl.pallas_call(
        body, name="conv_forward", grid=(nc,),
        in_specs=[col(0), col(1), col(2), col(3), col(4),
                  pl.BlockSpec((CONF_K, LANE), lambda c: (0, c)),
                  pl.BlockSpec((1, LANE), lambda c: (0, c)),
                  pl.BlockSpec((SHORT_K, LANE), lambda c: (0, c))] + [ANY] * len(deps),
        out_specs=[pl.BlockSpec((t, LANE), lambda c: (0, c)), pl.BlockSpec((t, LANE), lambda c: (0, c))],
        out_shape=[jax.ShapeDtypeStruct((t, dc), F32), jax.ShapeDtypeStruct((t, dc), BF16)],
        scratch_shapes=[pltpu.VMEM((t + 2 * CONV_PAD, LANE), F32), pltpu.VMEM((t + 2 * CONV_PAD, LANE), F32)],
        compiler_params=_params(1),
    )(proj, proj, proj, proj, proj, conf_w, conf_b, short_w, *deps)


def _conv_backward(dproj, proj, da1, ds, conf_w, short_w, dc):
    t = proj.shape[0]
    nc = dc // LANE

    def body(dp_in, av_ref, ag_ref, bg_ref, cg_ref, v_ref, da1_ref, ds_ref, cw_ref, sw_ref,
             dp_ref, dcw_ref, dcb_ref, dsw_ref, xa, xb, da, db, stage, sems):
        del dp_in
        c = pl.program_id(0)
        for ref in (xa, xb, da, db):
            _zero_pads(ref, t)
        xa[CONV_PAD:CONV_PAD + t, :] = av_ref[...] * _sigmoid(ag_ref[...])
        xb[CONV_PAD:CONV_PAD + t, :] = cg_ref[...] * v_ref[...]
        da[CONV_PAD:CONV_PAD + t, :] = da1_ref[...]
        dcw_ref[...] = jnp.zeros(dcw_ref.shape, F32)
        dsw_ref[...] = jnp.zeros(dsw_ref.shape, F32)
        dcb_ref[...] = jnp.zeros(dcb_ref.shape, F32)

        def through_gate(r0):
            rs = pl.ds(r0, CONV_CHUNK)
            ds_c = ds_ref[rs, :]
            stage[2, rs, :] = (ds_c * _causal_conv(xb, sw_ref, SHORT_K, r0)).astype(BF16)
            db[pl.ds(pl.multiple_of(CONV_PAD + r0, CONV_PAD), CONV_CHUNK), :] = ds_c * bg_ref[rs, :]

        _for_chunks(t, through_gate)

        def through_convs(r0):
            rs = pl.ds(r0, CONV_CHUNK)
            da0 = _anticausal_conv(da, cw_ref, CONF_K, r0)
            sg = _sigmoid(ag_ref[rs, :])
            stage[0, rs, :] = (da0 * sg).astype(BF16)
            stage[1, rs, :] = (da0 * av_ref[rs, :] * sg * (1.0 - sg)).astype(BF16)
            dcv = _anticausal_conv(db, sw_ref, SHORT_K, r0)
            stage[3, rs, :] = (dcv * v_ref[rs, :]).astype(BF16)
            stage[4, rs, :] = (dcv * cg_ref[rs, :]).astype(BF16)
            da1_c = da1_ref[rs, :]
            _conv_weight_grad(dcw_ref, da1_c, xa, CONF_K, r0)
            _conv_weight_grad(dsw_ref, ds_ref[rs, :] * bg_ref[rs, :], xb, SHORT_K, r0)
            dcb_ref[...] += _colsum8(da1_c)

        _for_chunks(t, through_convs)
        copies = [pltpu.make_async_copy(
            stage.at[g], dp_ref.at[:, pl.ds(pl.multiple_of((g * nc + c) * LANE, LANE), LANE)], sems.at[g])
            for g in range(5)]
        for cp in copies:
            cp.start()
        for cp in copies:
            cp.wait()

    col = lambda g: pl.BlockSpec((t, LANE), lambda c, g=g: (0, g * nc + c))
    blk = pl.BlockSpec((t, LANE), lambda c: (0, c))
    return pl.pallas_call(
        body, name="conv_backward", grid=(nc,),
        in_specs=[ANY, col(0), col(1), col(2), col(3), col(4), blk, blk,
                  pl.BlockSpec((CONF_K, LANE), lambda c: (0, c)),
                  pl.BlockSpec((SHORT_K, LANE), lambda c: (0, c))],
        out_specs=[ANY,
                   pl.BlockSpec((CONF_K * SUB, LANE), lambda c: (0, c)),
                   pl.BlockSpec((SUB, LANE), lambda c: (0, c)),
                   pl.BlockSpec((SHORT_K * SUB, LANE), lambda c: (0, c))],
        out_shape=[jax.ShapeDtypeStruct(dproj.shape, dproj.dtype),
                   jax.ShapeDtypeStruct((CONF_K * SUB, dc), F32),
                   jax.ShapeDtypeStruct((SUB, dc), F32),
                   jax.ShapeDtypeStruct((SHORT_K * SUB, dc), F32)],
        scratch_shapes=[pltpu.VMEM((t + 2 * CONV_PAD, LANE), F32)] * 4
                       + [pltpu.VMEM((5, t, LANE), BF16), pltpu.SemaphoreType.DMA((5,))],
        input_output_aliases={0: 0},
        compiler_params=_params(1),
    )(dproj, proj, proj, proj, proj, proj, da1, ds, conf_w, short_w)


def _adamw_math(w, g, m, v):
    m = ADAM_B1 * m + (1.0 - ADAM_B1) * g
    v = ADAM_B2 * v + (1.0 - ADAM_B2) * (g * g)
    m_hat = m / (1.0 - ADAM_B1 ** ADAM_STEP)
    v_hat = v / (1.0 - ADAM_B2 ** ADAM_STEP)
    delta = -ADAM_LR * (m_hat / (jnp.sqrt(v_hat) + ADAM_EPS) + ADAM_WD * w)
    return delta, m, v


def _cast_into_slot(name, w, me_arr, deps=()):
    r, c = w.shape
    tr = 256

    def body(me_ref, w_ref, *rest):
        del me_ref
        rest[-1][0] = w_ref[...].astype(BF16)

    return pl.pallas_call(
        body, name=name,
        grid_spec=pltpu.PrefetchScalarGridSpec(
            num_scalar_prefetch=1, grid=(r // tr,),
            in_specs=[pl.BlockSpec((tr, c), lambda i, me: (i, 0))] + [ANY] * len(deps),
            out_specs=pl.BlockSpec((1, tr, c), lambda i, me: (me[0], i, 0))),
        out_shape=jax.ShapeDtypeStruct((N_DEV, r, c), BF16),
        compiler_params=_params(1),
    )(me_arr, w, *deps)


def _chip_sum(name, full, from_sibling, me_arr):
    _, r, c = full.shape
    tr = min(r, 512)

    def body(me_ref, full_ref, sib_ref, sums_ref):
        del me_ref
        sums_ref[0] = (full_ref[0].astype(F32) + sib_ref[0].astype(F32)).astype(BF16)

    other = lambda k, me: (me[0] // 2 + 1 + k) % 4
    return pl.pallas_call(
        body, name=name,
        grid_spec=pltpu.PrefetchScalarGridSpec(
            num_scalar_prefetch=1, grid=(r // tr, 3),
            in_specs=[pl.BlockSpec((1, tr, c), lambda i, k, me: (2 * other(k, me) + me[0] % 2, i, 0)),
                      pl.BlockSpec((1, tr, c), lambda i, k, me: (other(k, me), i, 0))],
            out_specs=pl.BlockSpec((1, tr, c), lambda i, k, me: (other(k, me), i, 0))),
        out_shape=jax.ShapeDtypeStruct((4, r, c), BF16),
        compiler_params=_params(2),
    )(me_arr, full, from_sibling)


def _adamw_shard(name, w, m, v, parts, me_arr, deps=()):
    r, c = w.shape
    tr = min(256, r // len(parts))
    np_ = len(parts)
    per = r // np_ // tr

    def body(me_ref, w_ref, m_ref, v_ref, *rest):
        g_out, d_out, m_out, v_out = rest[5 * np_ + len(deps):]
        g = None
        for p in range(np_):
            gp = rest[5 * p][...]
            for l_ref in rest[5 * p + 1:5 * p + 5]:
                gp = gp + l_ref[0].astype(F32)
            g = gp if g is None else jnp.where(pl.program_id(0) // per == p, gp, g)
        delta, m_new, v_new = _adamw_math(w_ref[...], g, m_ref[...], v_ref[...])
        g_out[...] = g
        d_out[...] = delta
        m_out[...] = m_new
        v_out[...] = v_new

    tile = pl.BlockSpec((tr, c), lambda i, me: (i, 0))
    part_specs, part_args = [], []
    for p, (g_own, from_sibling, landed) in enumerate(parts):
        row = lambda i, p=p: jnp.clip(i - p * per, 0, per - 1)
        part_specs.append(pl.BlockSpec((tr, c), lambda i, me, row=row: (row(i), 0)))
        part_specs += [pl.BlockSpec((1, tr, c), lambda i, me, k=k, row=row: ((me[0] // 2 + k) % 4, row(i), 0))
                       for k in range(4)]
        part_args += [g_own, from_sibling, landed, landed, landed]
    return pl.pallas_call(
        body, name=name,
        grid_spec=pltpu.PrefetchScalarGridSpec(
            num_scalar_prefetch=1, grid=(r // tr,),
            in_specs=[tile] * 3 + part_specs + [ANY] * len(deps), out_specs=[tile] * 4),
        out_shape=[jax.ShapeDtypeStruct((r, c), F32)] * 4,
        compiler_params=_params(1),
    )(me_arr, w, m, v, *part_args, *deps)


SMALL_W = 1024
VEC_ROWS = 16
META_ROW0 = 16
CONF_ROW0 = 64
SHORT_ROW0 = 96
SMALL_ROWS = 104


def _pack_small(vec_parts, dmeta, dcw, dsw, me_arr):
    widths = [p.shape[1] for p in vec_parts]
    nv = len(vec_parts)

    def body(me_ref, *refs):
        del me_ref
        parts, (dmeta_ref, dcw_ref, dsw_ref, out_ref) = refs[:nv], refs[nv:]
        out_ref[0] = jnp.zeros((SMALL_ROWS, SMALL_W), F32)
        row = 0
        for p_ref, wd in zip(parts, widths):
            s = jnp.sum(p_ref[...], axis=0, keepdims=True)
            for h in range(wd // SMALL_W):
                out_ref[0, row:row + 1, :] = s[:, h * SMALL_W:(h + 1) * SMALL_W]
                row += 1
        for h in range(dmeta_ref.shape[1] // SMALL_W):
            out_ref[0, META_ROW0 + h * N_META:META_ROW0 + (h + 1) * N_META, :] = dmeta_ref[:, h * SMALL_W:(h + 1) * SMALL_W]
        for k in range(CONF_K):
            out_ref[0, CONF_ROW0 + k:CONF_ROW0 + k + 1, :] = jnp.sum(dcw_ref[k * SUB:(k + 1) * SUB, :], axis=0, keepdims=True)
        for k in range(SHORT_K):
            out_ref[0, SHORT_ROW0 + k:SHORT_ROW0 + k + 1, :] = jnp.sum(dsw_ref[k * SUB:(k + 1) * SUB, :], axis=0, keepdims=True)

    ins = [*vec_parts, dmeta, dcw, dsw]
    return pl.pallas_call(
        body, name="pack_small",
        grid_spec=pltpu.PrefetchScalarGridSpec(
            num_scalar_prefetch=1, grid=(1,),
            in_specs=[pl.BlockSpec(a.shape, lambda i, me: (0, 0)) for a in ins],
            out_specs=pl.BlockSpec((1, SMALL_ROWS, SMALL_W), lambda i, me: (me[0], 0, 0))),
        out_shape=jax.ShapeDtypeStruct((N_DEV, SMALL_ROWS, SMALL_W), F32),
        compiler_params=_params(1),
    )(me_arr, *ins)


def _small_update(gathered, me_arr, vec_params, meta_p, conf_p, short_p):
    widths = [p[0].shape[1] for p in vec_params]
    nv = len(vec_params)
    mcols = meta_p[0].shape[1]
    per_row = SMALL_W // mcols

    def body(me_ref, gv_ref, gm_ref, gc_ref, gs_ref, *rest):
        del me_ref
        ins, outs = rest[:3 * (nv + 3)], rest[3 * (nv + 3):]

        def total(ref, r0, rows):
            s = ref[0, r0:r0 + rows, :]
            for dev in range(1, N_DEV):
                s = s + ref[dev, r0:r0 + rows, :]
            return s

        grads = []
        row = 0
        for wd in widths:
            pieces = [total(gv_ref, row + h, 1) for h in range(wd // SMALL_W)]
            grads.append(pieces[0] if len(pieces) == 1 else jnp.concatenate(pieces, axis=1))
            row += len(pieces)
        grads.append(total(gm_ref, 0, N_META))
        grads.append(total(gc_ref, 0, CONF_K))
        grads.append(total(gs_ref, 0, SHORT_K))
        for idx, g in enumerate(grads):
            w_ref, m_ref, v_ref = ins[3 * idx:3 * idx + 3]
            delta, m_new, v_new = _adamw_math(w_ref[...], g, m_ref[...], v_ref[...])
            g_out, d_out, m_out, v_out = outs[4 * idx:4 * idx + 4]
            g_out[...] = g
            d_out[...] = delta
            m_out[...] = m_new
            v_out[...] = v_new

    params = list(vec_params) + [meta_p, conf_p, short_p]
    flat = [a for p in params for a in p]
    whole = lambda a: pl.BlockSpec(a.shape, lambda i, me: (0,) * a.ndim)
    outs = pl.pallas_call(
        body, name="small_update",
        grid_spec=pltpu.PrefetchScalarGridSpec(
            num_scalar_prefetch=1, grid=(1,),
            in_specs=[pl.BlockSpec((N_DEV, VEC_ROWS, SMALL_W), lambda i, me: (0, 0, 0)),
                      pl.BlockSpec((N_DEV, N_META, mcols),
                                   lambda i, me: (0, META_ROW0 // N_META + me[0] // per_row, me[0] % per_row)),
                      pl.BlockSpec((N_DEV, 32, LANE), lambda i, me: (0, CONF_ROW0 // 32, me[0])),
                      pl.BlockSpec((N_DEV, SUB, LANE), lambda i, me: (0, SHORT_ROW0 // SUB, me[0]))]
                     + [whole(a) for a in flat],
            out_specs=[whole(p[0]) for p in params for _ in range(4)]),
        out_shape=[jax.ShapeDtypeStruct(p[0].shape, F32) for p in params for _ in range(4)],
        compiler_params=_params(1),
    )(me_arr, gathered, gathered, gathered, gathered, *flat)
    return [tuple(outs[4 * i:4 * i + 4]) for i in range(len(params))]


def kernel(x, meta, g_pre_mix, w_in, b_gates, conf_dw_w, conf_dw_b, conf_ln_g, conf_ln_b, conf_w_pw, short_dw_w, short_w_out, w_o, g_post_mix, g_pre_mlp, w_up, w_down, g_post_mlp, loss_target, m_meta, m_g_pre_mix, m_w_in, m_b_gates, m_conf_dw_w, m_conf_dw_b, m_conf_ln_g, m_conf_ln_b, m_conf_w_pw, m_short_dw_w, m_short_w_out, m_w_o, m_g_post_mix, m_g_pre_mlp, m_w_up, m_w_down, m_g_post_mlp, v_meta, v_g_pre_mix, v_w_in, v_b_gates, v_conf_dw_w, v_conf_dw_b, v_conf_ln_g, v_conf_ln_b, v_conf_w_pw, v_short_dw_w, v_short_w_out, v_w_o, v_g_post_mix, v_g_pre_mlp, v_w_up, v_w_down, v_g_post_mlp):
    seq, d = x.shape[1], x.shape[2]
    dc = conf_w_pw.shape[1]
    t_real = N_META + seq
    t = -(-t_real // ROW_TILE) * ROW_TILE
    tm = t // 2
    assert tm % 16 == 0 and d % 1024 == 0 and dc % 1024 == 0
    x_idx, y_idx, c_idx = _position()
    me_arr = jnp.reshape(4 * x_idx + 2 * y_idx + c_idx, (1,)).astype(jnp.int32)

    big = [w_in[0], conf_w_pw[0], short_w_out[0], w_o[0], w_up[0], w_down[0]]
    big_names = ["w_in", "conf_w_pw", "short_w_out", "w_o", "w_up", "w_down"]
    groups = [[0], [1, 2, 3], [4], [5]]
    meta_g, cw_g, sw_g = _all_gather("gather_small_params", [meta, conf_dw_w[0], short_dw_w[0]])
    slots, tok = [], meta_g
    for g, idxs in enumerate(groups):
        slots.append([_cast_into_slot("cast_" + big_names[i], big[i], me_arr, deps=[tok]) for i in idxs])
        if g == 0:
            direct0 = _remote_start("gather0_direct_start", "gather_direct", slots[0], deps=[tok])
            tok = direct0[3]
    started = tok[0, 0] * 0.0

    def start_direct(g, deps):
        send, recv, bufs, tok = _remote_start("gather%d_direct_start" % g, "gather_direct", slots[g], deps=deps)
        return (send, recv, bufs), tok

    def relay(g, state, after):
        send, recv, bufs = state
        bufs = _remote_wait("gather%d_direct_wait" % g, "gather_direct", send, recv, bufs, len(bufs), after)
        send, recv, bufs, tok = _remote_start("gather%d_relay_start" % g, "gather_relay", bufs)
        return (send, recv, bufs), tok

    def gathered(g, state, after):
        send, recv, bufs = state
        bufs = _remote_wait("gather%d_relay_wait" % g, "gather_relay", send, recv, bufs, len(bufs), after)
        send, recv, bufs, tok = _remote_start("gather%d_diag_start" % g, "gather_diag", bufs)
        return _remote_wait("gather%d_diag_wait" % g, "gather_diag", send, recv, bufs, len(bufs), [tok])

    unshard =lambda g: jnp.transpose(g, (1, 0, 2)).reshape(g.shape[1], -1)
    meta_full, cw_full, sw_full = unshard(meta_g), unshard(cw_g), unshard(sw_g)

    zrows = jnp.zeros((t - t_real, d), F32) + started
    h0 = jnp.concatenate([meta_full, x[0], zrows], axis=0)
    tgt = jnp.concatenate([jnp.zeros((N_META, d), F32), loss_target[0], zrows], axis=0)
    n = _pre_norm(h0, g_pre_mix)
    relay0, tok = relay(0, direct0[:3], [n])
    direct1, tok = start_direct(1, [tok])
    direct2, tok = start_direct(2, [tok])
    win_g, = gathered(0, relay0, [tok])
    proj = _mm_cols("proj", n, win_g, tm=tm)[0]
    relay1, tok = relay(1, direct1, [proj])
    direct3, tok = start_direct(3, [tok])
    a1, s = _conv_forward(proj, cw_full, conf_dw_b, sw_full, dc, deps=[tok])
    a3 = _layer_norm_silu(a1, conf_ln_g, conf_ln_b)
    wpw_g, wso_g, wo_g = gathered(1, relay1, [a3])
    wo_full = wo_g.reshape(d, d)
    ya = _mm_cols("y_a", a3, wpw_g, tm=tm, nb=N_DEV)[0]
    yb = _mm_cols("y_b", s, wso_g, tm=tm, nb=N_DEV)[0]
    m_mix = _gate_merge(proj, ya, yb, b_gates, d)
    mix = _mm_rows("mix", m_mix, wo_full, tm=tm // 2, tn=d)
    relay2, tok = relay(2, direct2, [mix])
    relay3, tok = relay(3, direct3, [tok])
    h1, n2 = _post_mix(mix, h0, g_post_mix, g_pre_mlp, deps=[tok])
    wup_g, = gathered(2, relay2, [n2])

    def up_epilogue(acc):
        r = jnp.maximum(acc, 0.0)
        return r * r, r

    f, relu_up = _mm_cols("mlp_up", n2, wup_g, tm=tm, epilogue=up_epilogue, out_dtypes=(BF16, BF16))
    wdn_g, = gathered(3, relay3, [f])
    wdn_full = wdn_g.reshape(-1, d)
    fo = _mm_rows("mlp_down", f, wdn_full, tm=tm // 2, tn=512)
    dfo, dh2, dg_post_mlp, loss_blk = _loss_head(fo, h1, tgt, g_post_mlp, t_real)
    loss = lax.psum(loss_blk[0, 0], ("x", "y", "c"))

    def reduce_start(tag, fulls, deps):
        lands = [lax.empty((4,) + g.shape[1:], BF16) for g in fulls]
        send, recv, bufs, tok = _remote_start("reduce_%s_d2d_start" % tag, "reduce_d2d", fulls, lands, deps=deps)
        return (send, recv, bufs), tok

    def reduce_middle(tag, state, owns, after):
        send, recv, bufs = state
        k = len(owns)
        bufs = _remote_wait("reduce_%s_d2d_wait" % tag, "reduce_d2d", send, recv, bufs, k, after)
        from_sibling = bufs[k:]
        sums = [_chip_sum("chip_sum_%s%d" % (tag, i), bufs[i], from_sibling[i], me_arr) for i in range(k)]
        lands = [lax.empty(sm.shape, BF16) for sm in sums]
        send, recv, bufs, tok = _remote_start("reduce_%s_ici_start" % tag, "reduce_ici", sums, lands)
        return (send, recv, bufs, list(zip(owns, from_sibling))), tok

    def reduce_finish(tag, state, after):
        send, recv, bufs, local = state
        k = len(local)
        bufs = _remote_wait("reduce_%s_ici_wait" % tag, "reduce_ici", send, recv, bufs, k, after)
        return [(own, sib, landed) for (own, sib), landed in zip(local, bufs[k:])]

    dup = _mm_nt_blocks("d_up", dfo, wdn_full, tm=tm, tkb=1024, extra=(relu_up,),
                        epilogue=lambda acc, r: (acc * (2.0 * r.astype(F32)),), out_dtypes=(BF16,))[0]
    gw_down, gw_down_own = _mm_tn("dw_down", f, dfo, me_arr, m=f.shape[1], n=d, tma=512, tn=d, sharded="rows")
    red_down, tok = reduce_start("down", [gw_down], ())
    dn2 = _mm_nt_acc("d_n2", dup, wup_g, tm=tm // 2, tn=512, deps=[tok])
    gw_up, gw_up_own = _mm_tn("dw_up", n2, dup, me_arr, m=d, n=dup.shape[1], tma=512, tn=1024, sharded="cols")
    red_down, tok = reduce_middle("down", red_down, [gw_down_own], [dn2])
    red_up, tok = reduce_start("up", [gw_up], [tok])
    dh1, dmix, dg_pre_mlp, dg_post_mix = _mid_norm_bwd(dn2, h1, dh2, mix, g_pre_mlp, g_post_mix, deps=[tok])
    dya, dyb, dproj, db_a, db_b = _gate_backward(dmix, wo_full, proj, ya, yb, b_gates, d, tm // 2)
    db_gates = jnp.concatenate([db_a, db_b], axis=1)
    red_up, tok = reduce_middle("up", red_up, [gw_up_own], [dya])
    gw_o, gw_o_own = _mm_tn("dw_o", m_mix, dmix, me_arr, m=d, n=d, tma=d // N_DEV, tn=d, sharded="rows", deps=[tok])
    da3 = _mm_nt_acc("d_a3", dya, wpw_g, tm=tm, tn=512)
    gw_pw, gw_pw_own = _mm_tn("dw_pw", a3, dya, me_arr, m=dc, n=d, tma=512, tn=d, sharded="cols")
    dsb = _mm_nt_acc("d_s", dyb, wso_g, tm=tm, tn=512)
    gw_so, gw_so_own = _mm_tn("dw_so", s, dyb, me_arr, m=dc, n=d, tma=512, tn=d, sharded="cols")
    red_mix, tok = reduce_start("mix", [gw_pw, gw_so, gw_o], ())
    da1, dln_g, dln_b = _layer_norm_silu_bwd(da3, a1, conf_ln_g, conf_ln_b, deps=[tok])
    dproj, dcw, dcb, dsw = _conv_backward(dproj, proj, da1, dsb, cw_full, sw_full, dc)
    red_mix, tok = reduce_middle("mix", red_mix, [gw_pw_own, gw_so_own, gw_o_own], [dcb])
    in_cb = w_in.shape[2]
    half = d // 2
    red_in = []
    for part in range(2):
        gw, own = _mm_tn("dw_in%d" % part, n, dproj, me_arr, m=half, n=proj.shape[1], tma=512, tn=2 * in_cb,
                         sharded="cols", a_off=part * (half // 512), deps=[tok])
        state, tok = reduce_start("in%d" % part, [gw], ())
        red_in.append((state, own))
    for part in range(2):
        state, own = red_in[part]
        red_in[part], tok = reduce_middle("in%d" % part, state, [own], [tok])
    dn = _mm_nt_acc("d_n", dproj, win_g, tm=tm // 2, tn=512, deps=[tok])
    dh0, dg_pre_mix = _pre_norm_bwd(dn, h0, dh1, g_pre_mix)
    grad_x = dh0[N_META:t_real][None]

    vec_parts = [dg_pre_mix, db_gates, dcb, dln_g, dln_b, dg_post_mix, dg_pre_mlp, dg_post_mlp]
    packed = _pack_small(vec_parts, dh0[:N_META], dcw, dsw, me_arr)
    send, recv, bufs, tok = _remote_start("small_grads_ici_start", "gather_ici", [packed])
    vec_names = ["g_pre_mix", "b_gates", "conf_dw_b", "conf_ln_g", "conf_ln_b", "g_post_mix", "g_pre_mlp", "g_post_mlp"]
    env = locals()
    results = {}

    def update(nm, parts, deps=()):
        res = _adamw_shard("adamw_" + nm, env[nm][0], env["m_" + nm][0], env["v_" + nm][0], parts, me_arr, deps=deps)
        results[nm] = tuple(r[None] for r in res)
        return res[0]

    done = [update("w_down", reduce_finish("down", red_down, [tok]), deps=[tok])]
    done.append(update("w_up", reduce_finish("up", red_up, done)))
    bufs = _remote_wait("small_grads_ici_wait", "gather_ici", send, recv, bufs, 1, done)
    send, recv, bufs, tok = _remote_start("small_grads_d2d_start", "gather_d2d", bufs)
    for nm, pair in zip(["conf_w_pw", "short_w_out", "w_o"], reduce_finish("mix", red_mix, [tok])):
        done.append(update(nm, [pair], deps=[tok]))
    small_g, = _remote_wait("small_grads_d2d_wait", "gather_d2d", send, recv, bufs, 1, done)
    triple = lambda nm, sq: tuple(env[p + nm][0] if sq else env[p + nm] for p in ("", "m_", "v_"))
    small = _small_update(small_g, me_arr, [triple(nm, False) for nm in vec_names],
                          triple("meta", False), triple("conf_dw_w", True), triple("short_dw_w", True))
    for nm, res in zip(vec_names + ["meta"], small[:len(vec_names) + 1]):
        results[nm] = res
    results["conf_dw_w"] = tuple(r[None] for r in small[-2])
    results["short_dw_w"] = tuple(r[None] for r in small[-1])
    update("w_in", [reduce_finish("in%d" % part, red_in[part], [small[0][0]])[0] for part in range(2)])

    order = ["meta", "g_pre_mix", "w_in", "b_gates", "conf_dw_w", "conf_dw_b", "conf_ln_g", "conf_ln_b", "conf_w_pw",
             "short_dw_w", "short_w_out", "w_o", "g_post_mix", "g_pre_mlp", "w_up", "w_down", "g_post_mlp"]
    return (loss, grad_x, *[results[nm][0] for nm in order], *[results[nm][1] for nm in order],
            *[results[nm][2] for nm in order], *[results[nm][3] for nm in order])
```

```python
import jax
import jax.numpy as jnp
from jax import lax
from jax.experimental import pallas as pl
from jax.experimental.pallas import tpu as pltpu

N_DEV = 8
N_META = 16
CONF_K = 31
SHORT_K = 3
RMS_EPS = 1e-6
LN_EPS = 1e-5
ADAM_LR = 0.001
ADAM_B1 = 0.9
ADAM_B2 = 0.999
ADAM_EPS = 1e-08
ADAM_WD = 0.01
ADAM_STEP = 10

LANE = 128
SUB = 8
ROW_TILE = 128
CONV_PAD = 32
CONV_CHUNK = 128
VMEM_LIMIT = 56 * 1024 * 1024

F32 = jnp.float32
BF16 = jnp.bfloat16
MESH = pl.DeviceIdType.MESH
ANY = pl.BlockSpec(memory_space=pl.ANY)
HBM_SPEC = pl.BlockSpec(memory_space=pltpu.HBM)
SEM_SPEC = pl.BlockSpec(memory_space=pltpu.SEMAPHORE)
EFFECT = pltpu.SideEffectType.DATAFLOW_SIDE_EFFECTING


def _params(n_axes):
    return pltpu.CompilerParams(dimension_semantics=("arbitrary",) * n_axes, vmem_limit_bytes=VMEM_LIMIT)


def _sigmoid(z):
    return 1.0 / (1.0 + jnp.exp(-z))


def _colsum8(v):
    r, c = v.shape
    return jnp.sum(v.reshape(r // SUB, SUB, c), axis=0)


def _position():
    x, y, c = lax.axis_index("x"), lax.axis_index("y"), lax.axis_index("c")
    return x, y, c


def _flat(p):
    return 4 * p[0] + 2 * p[1] + p[2]


def _all_gather(name, shards, deps=()):
    n, nd = len(shards), len(deps)

    def body(*refs):
        ins, outs = refs[:n], refs[n + nd:2 * n + nd]
        send_sems, recv_sems, local_sems = refs[2 * n + nd:]
        x, y, c = _position()
        me, sibling = (x, y, c), (x, y, 1 - c)
        chips = [(1 - x, y), (x, 1 - y), (1 - x, 1 - y)]

        def copy(q, k, block, to, src=None):
            dst = outs[q].at[_flat(block)]
            return pltpu.make_async_remote_copy(
                src_ref=dst if src is None else src, dst_ref=dst,
                send_sem=send_sems.at[q, k], recv_sem=recv_sems.at[q, k],
                device_id=to, device_id_type=MESH)

        mine = [pltpu.make_async_copy(ins[q], outs[q].at[_flat(me)], local_sems.at[q]) for q in range(n)]
        for cp in mine:
            cp.start()
        first = []
        for q in range(n):
            first.append(copy(q, 0, me, sibling, src=ins[q]))
            for j, chip in enumerate(chips):
                first.append(copy(q, 1 + j, me, (*chip, c), src=ins[q]))
        for cp in first:
            cp.start()
        passed = []
        for q in range(n):
            for j, chip in enumerate(chips):
                copy(q, 1 + j, (*chip, c), me).wait_recv()
                fwd = copy(q, 4 + j, (*chip, c), sibling)
                fwd.start()
                passed.append(fwd)
        for q in range(n):
            copy(q, 0, sibling, me).wait_recv()
            for j, chip in enumerate(chips):
                copy(q, 4 + j, (*chip, 1 - c), me).wait_recv()
        for cp in first + passed:
            cp.wait_send()
        for cp in mine:
            cp.wait()

    return pl.pallas_call(
        body, name=name,
        in_specs=[ANY] * (n + nd), out_specs=[ANY] * n,
        out_shape=[jax.ShapeDtypeStruct((N_DEV,) + s.shape, s.dtype) for s in shards],
        scratch_shapes=[pltpu.SemaphoreType.DMA((n, 7)), pltpu.SemaphoreType.DMA((n, 7)),
                        pltpu.SemaphoreType.DMA((n,))],
    )(*shards, *deps)


N_COPIES = {"gather_ici": 4, "gather_d2d": 3, "gather_direct": 3, "gather_relay": 3, "gather_diag": 1,
            "reduce_d2d": 4, "reduce_ici": 3}


def _copy_plan(kind):
    x, y, c = _position()
    me, sibling = (x, y, c), (x, y, 1 - c)
    chips = [(1 - x, y), (x, 1 - y), (1 - x, 1 - y)]
    if kind == "gather_ici":
        return [(_flat(me), _flat(me), sibling)] + [(_flat(me), _flat(me), (*ch, c)) for ch in chips]
    if kind == "gather_d2d":
        return [(_flat((*ch, c)), _flat((*ch, c)), sibling) for ch in chips]
    if kind == "gather_direct":
        return [(_flat(me), _flat(me), sibling)] + [(_flat(me), _flat(me), (*ch, c)) for ch in chips[:2]]
    if kind == "gather_relay":
        held, to = (x ^ (1 - c), y ^ c, c), (x ^ c, y ^ (1 - c), c)
        return [(_flat(held), _flat(held), to)] + [(_flat((*ch, c)), _flat((*ch, c)), sibling) for ch in chips[:2]]
    if kind == "gather_diag":
        return [(_flat((*chips[2], c)), _flat((*chips[2], c)), sibling)]
    if kind == "reduce_d2d":
        return [(2 * chip + (1 - c), chip, sibling) for chip in range(4)]
    return [(2 * ch[0] + ch[1], 2 * x + y, (*ch, c)) for ch in chips]


def _planned_copies(kind, srcs, dsts, send_sems, recv_sems):
    plan = _copy_plan(kind)
    return [pltpu.make_async_remote_copy(
        src_ref=src.at[s_slot], dst_ref=dst.at[d_slot],
        send_sem=send_sems.at[q * len(plan) + k], recv_sem=recv_sems.at[q * len(plan) + k],
        device_id=to, device_id_type=MESH)
        for q, (src, dst) in enumerate(zip(srcs, dsts)) for k, (s_slot, d_slot, to) in enumerate(plan)]


def _remote_start(name, kind, srcs, lands=None, deps=()):
    n = len(srcs)
    bufs = list(srcs) + ([] if lands is None else list(lands))
    nb, nd = len(bufs), len(deps)
    nsem = n * N_COPIES[kind]

    def body(*refs):
        ins = refs[:nb]
        send_sems, recv_sems = refs[nb + nd], refs[nb + nd + 1]
        token = refs[-1]
        for cp in _planned_copies(kind, ins[:n], ins[:n] if lands is None else ins[n:], send_sems, recv_sems):
            cp.start()
        token[...] = jnp.zeros_like(token)

    outs = pl.pallas_call(
        body, name=name,
        out_shape=(pltpu.SemaphoreType.DMA((nsem,)), pltpu.SemaphoreType.DMA((nsem,)),
                   *[pltpu.HBM(b.shape, b.dtype) for b in bufs], jax.ShapeDtypeStruct((SUB, LANE), F32)),
        in_specs=[HBM_SPEC] * nb + [ANY] * nd,
        out_specs=(SEM_SPEC, SEM_SPEC, *[HBM_SPEC] * nb, pl.BlockSpec(memory_space=pltpu.VMEM)),
        input_output_aliases={i: 2 + i for i in range(nb)},
        compiler_params=pltpu.CompilerParams(has_side_effects=EFFECT),
    )(*[pltpu.with_memory_space_constraint(b, pltpu.HBM) for b in bufs], *deps)
    return outs[0], outs[1], list(outs[2:2 + nb]), outs[-1]


def _remote_wait(name, kind, send_sems, recv_sems, bufs, n, after):
    nb, na = len(bufs), len(after)
    same = nb == n

    def body(*refs):
        ins = refs[:nb]
        sends, recvs = refs[nb], refs[nb + 1]
        for cp in _planned_copies(kind, ins[:n], ins[:n] if same else ins[n:], sends, recvs):
            cp.wait_send()
            cp.wait_recv()

    outs = pl.pallas_call(
        body, name=name,
        out_shape=[pltpu.HBM(b.shape, b.dtype) for b in bufs],
        in_specs=[HBM_SPEC] * nb + [SEM_SPEC, SEM_SPEC] + [ANY] * na,
        out_specs=[HBM_SPEC] * nb,
        input_output_aliases={i: i for i in range(nb)},
        compiler_params=pltpu.CompilerParams(has_side_effects=EFFECT),
    )(*bufs, send_sems, recv_sems, *after)
    return list(outs)


def _mm_cols(name, a, w, *, tm, nb=1, epilogue=None, out_dtypes=(F32,)):
    t, k = a.shape
    nblk, _, cb = w.shape

    def body(a_ref, w_ref, *o_refs):
        av = a_ref[...]
        for b in range(nb):
            acc = jnp.dot(av, w_ref[b], preferred_element_type=F32)
            outs = (acc,) if epilogue is None else epilogue(acc)
            for o_ref, o in zip(o_refs, outs):
                o_ref[:, b * cb:(b + 1) * cb] = o.astype(o_ref.dtype)

    return pl.pallas_call(
        body, name=name, grid=(nblk // nb, t // tm),
        in_specs=[pl.BlockSpec((tm, k), lambda j, i: (i, 0)),
                  pl.BlockSpec((nb, k, cb), lambda j, i: (j, 0, 0))],
        out_specs=[pl.BlockSpec((tm, nb * cb), lambda j, i: (i, j)) for _ in out_dtypes],
        out_shape=[jax.ShapeDtypeStruct((t, nblk * cb), dt) for dt in out_dtypes],
        compiler_params=_params(2),
    )(a, w)


def _mm_rows(name, a, w2d, *, tm, tn):
    t, kf = a.shape
    n = w2d.shape[1]

    def body(a_ref, w_ref, o_ref):
        o_ref[...] = jnp.dot(a_ref[...], w_ref[...], preferred_element_type=F32)

    return pl.pallas_call(
        body, name=name, grid=(t // tm, n // tn),
        in_specs=[pl.BlockSpec((tm, kf), lambda i, j: (i, 0)),
                  pl.BlockSpec((kf, tn), lambda i, j: (0, j))],
        out_specs=pl.BlockSpec((tm, tn), lambda i, j: (i, j)),
        out_shape=jax.ShapeDtypeStruct((t, n), F32),
        compiler_params=_params(2),
    )(a, w2d)


def _mm_nt_acc(name, dy, w, *, tm, tn, col_off=0, deps=()):
    t = dy.shape[0]
    nblk, k, cb = w.shape

    def body(dy_ref, w_ref, *rest):
        acc = None
        for b in range(nblk):
            d = lax.dot_general(dy_ref[:, b * cb:(b + 1) * cb], w_ref[b], (((1,), (1,)), ((), ())),
                                preferred_element_type=F32)
            acc = d if acc is None else acc + d
        rest[-1][...] = acc

    return pl.pallas_call(
        body, name=name, grid=(t // tm, k // tn),
        in_specs=[pl.BlockSpec((tm, nblk * cb), lambda i, j: (i, col_off)),
                  pl.BlockSpec((nblk, tn, cb), lambda i, j: (0, j, 0))] + [ANY] * len(deps),
        out_specs=pl.BlockSpec((tm, tn), lambda i, j: (i, j)),
        out_shape=jax.ShapeDtypeStruct((t, k), F32),
        compiler_params=_params(2),
    )(dy, w, *deps)


def _mm_nt_blocks(name, dy, w2d, *, tm, tkb, extra=(), epilogue=None, out_dtypes=(F32,)):
    t, n = dy.shape
    kf = w2d.shape[0]
    ne = len(extra)

    def body(dy_ref, w_ref, *rest):
        acc = lax.dot_general(dy_ref[...], w_ref[...], (((1,), (1,)), ((), ())), preferred_element_type=F32)
        outs = (acc,) if epilogue is None else epilogue(acc, *[e[...] for e in rest[:ne]])
        for o_ref, o in zip(rest[ne:], outs):
            o_ref[...] = o.astype(o_ref.dtype)

    return pl.pallas_call(
        body, name=name, grid=(kf // tkb, t // tm),
        in_specs=[pl.BlockSpec((tm, n), lambda kb, i: (i, 0)),
                  pl.BlockSpec((tkb, n), lambda kb, i: (kb, 0))]
                 + [pl.BlockSpec((tm, tkb), lambda kb, i: (i, kb)) for _ in extra],
        out_specs=[pl.BlockSpec((tm, tkb), lambda kb, i: (i, kb)) for _ in out_dtypes],
        out_shape=[jax.ShapeDtypeStruct((t, kf), dt) for dt in out_dtypes],
        compiler_params=_params(2),
    )(dy, w2d, *extra)


def _mm_tn(name, a, b, me_arr, *, m, n, tma, tn, sharded, a_off=0, b_off=0, deps=()):
    t = a.shape[0]
    if sharded == "cols":
        cb = n // N_DEV
        nb, q = max(tn // cb, 1), max(cb // tn, 1)
        tw = tn // nb
        full_shape, own_shape = (N_DEV, m, cb), (m, cb)
        full_spec = pl.BlockSpec((nb, tma, tw), lambda i, j, me: (j // q, i, j % q))
    else:
        kb = m // N_DEV
        p = kb // tma
        nb, tw = 1, tn
        full_shape, own_shape = (m, n), (kb, n)
        full_spec = pl.BlockSpec((tma, tn), lambda i, j, me: (i, j))

    def body(me_ref, a_ref, b_ref, *rest):
        full_ref, own_ref, stage, sem = rest[len(deps):]
        i, j = pl.program_id(0), pl.program_id(1)
        acc = lax.dot_general(a_ref[...], b_ref[...], (((0,), (0,)), ((), ())), preferred_element_type=F32)
        for blk in range(nb):
            part = acc[:, blk * tw:(blk + 1) * tw]
            if sharded == "cols":
                full_ref[blk] = part.astype(BF16)
                owner, r0, c0 = (j // q) * nb + blk, i * tma, (j % q) * tw
            else:
                full_ref[...] = part.astype(BF16)
                owner, r0, c0 = i // p, (i % p) * tma, j * tn

            @pl.when(owner == me_ref[0])
            def _():
                stage[...] = part
                cp = pltpu.make_async_copy(
                    stage, own_ref.at[pl.ds(pl.multiple_of(r0, tma), tma), pl.ds(pl.multiple_of(c0, tw), tw)], sem)
                cp.start()
                cp.wait()

    full, own = pl.pallas_call(
        body, name=name,
        grid_spec=pltpu.PrefetchScalarGridSpec(
            num_scalar_prefetch=1, grid=(m // tma, n // tn),
            in_specs=[pl.BlockSpec((t, tma), lambda i, j, me: (0, a_off + i)),
                      pl.BlockSpec((t, tn), lambda i, j, me: (0, b_off + j))] + [ANY] * len(deps),
            out_specs=[full_spec, ANY],
            scratch_shapes=[pltpu.VMEM((tma, tw), F32), pltpu.SemaphoreType.DMA(())]),
        out_shape=[jax.ShapeDtypeStruct(full_shape, BF16), jax.ShapeDtypeStruct(own_shape, F32)],
        compiler_params=_params(2),
    )(me_arr, a, b, *deps)
    if sharded == "rows":
        full = full.reshape(N_DEV, m // N_DEV, n)
    return full, own


def _row_tile(t):
    return t // 8 if (t // 8) % 16 == 0 else ROW_TILE


def _row_call(name, body, t, row_ins, full_ins, row_outs, acc_outs, scratch=(), deps=()):
    tm = _row_tile(t)
    nin = len(row_ins) + len(full_ins)

    def without_deps(*refs):
        body(*refs[:nin], *refs[nin + len(deps):])

    return pl.pallas_call(
        without_deps, name=name, grid=(t // tm,),
        in_specs=[pl.BlockSpec((tm, a.shape[1]), lambda i: (i, 0)) for a in row_ins]
                 + [pl.BlockSpec(a.shape, lambda i: (0, 0)) for a in full_ins] + [ANY] * len(deps),
        out_specs=[pl.BlockSpec((tm, c), lambda i: (i, 0)) for c, _ in row_outs]
                  + [pl.BlockSpec((r, c), lambda i: (0, 0)) for r, c in acc_outs],
        out_shape=[jax.ShapeDtypeStruct((t, c), dt) for c, dt in row_outs]
                  + [jax.ShapeDtypeStruct((r, c), F32) for r, c in acc_outs],
        scratch_shapes=list(scratch),
        compiler_params=_params(1),
    )(*row_ins, *full_ins, *deps)


def _accumulate(ref, v):
    @pl.when(pl.program_id(0) == 0)
    def _():
        ref[...] = v

    @pl.when(pl.program_id(0) > 0)
    def _():
        ref[...] += v


def _rms(v):
    return lax.rsqrt(jnp.mean(v * v, axis=-1, keepdims=True) + RMS_EPS)


def _rms_bwd(dout, u, r, g):
    du = dout * g
    dx = r * (du - u * jnp.mean(du * u, axis=-1, keepdims=True))
    return dx, _colsum8(dout * u)


def _pre_norm(h0, g):
    t, d = h0.shape

    def body(h_ref, g_ref, n_ref):
        h = h_ref[...]
        n_ref[...] = (h * _rms(h) * g_ref[...]).astype(BF16)

    return _row_call("pre_norm", body, t, [h0], [g], [(d, BF16)], [])[0]


def _post_mix(mix, h0, g_post, g_pre, deps=()):
    t, d = h0.shape

    def body(mix_ref, h0_ref, gp_ref, gq_ref, h1_ref, n2_ref):
        mix_v = mix_ref[...]
        h1 = h0_ref[...] + mix_v * _rms(mix_v) * gp_ref[...]
        h1_ref[...] = h1
        n2_ref[...] = (h1 * _rms(h1) * gq_ref[...]).astype(BF16)

    return _row_call("post_mix", body, t, [mix, h0], [g_post, g_pre], [(d, F32), (d, BF16)], [], deps=deps)


def _loss_head(fo, h1, tgt, g_post_mlp, t_real):
    t, d = h1.shape
    tile = _row_tile(t)

    def body(fo_ref, h1_ref, tgt_ref, g_ref, dfo_ref, dh2_ref, dg_ref, loss_ref, lacc):
        i = pl.program_id(0)
        fo_v = fo_ref[...]
        g = g_ref[...]
        r = _rms(fo_v)
        u = fo_v * r
        h2 = h1_ref[...] + u * g
        row = i * tile + lax.broadcasted_iota(jnp.int32, (tile, 1), 0)
        valid = jnp.logical_and(row >= N_META, row < t_real)
        diff = jnp.where(valid, h2 - tgt_ref[...], 0.0)
        dh2 = diff * (1.0 / d)
        dh2_ref[...] = dh2
        dfo, dg = _rms_bwd(dh2, u, r, g)
        dfo_ref[...] = dfo.astype(BF16)
        _accumulate(dg_ref, dg)
        _accumulate(lacc, _colsum8(diff * diff))

        @pl.when(i == pl.num_programs(0) - 1)
        def _():
            loss_ref[...] = jnp.full((SUB, LANE), (0.5 / d) * jnp.sum(lacc[...]), F32)

    return _row_call("loss_head", body, t, [fo, h1, tgt], [g_post_mlp],
                     [(d, BF16), (d, F32)], [(SUB, d), (SUB, LANE)], scratch=[pltpu.VMEM((SUB, d), F32)])


def _mid_norm_bwd(dn2, h1, dh2, mix, g_pre_mlp, g_post_mix, deps=()):
    t, d = h1.shape

    def body(dn2_ref, h1_ref, dh2_ref, mix_ref, gq_ref, gp_ref, dh1_ref, dmix_ref, dgq_ref, dgp_ref):
        h1 = h1_ref[...]
        r3 = _rms(h1)
        dx, dgq = _rms_bwd(dn2_ref[...], h1 * r3, r3, gq_ref[...])
        dh1 = dh2_ref[...] + dx
        dh1_ref[...] = dh1
        mix_v = mix_ref[...]
        r2 = _rms(mix_v)
        dmix, dgp = _rms_bwd(dh1, mix_v * r2, r2, gp_ref[...])
        dmix_ref[...] = dmix.astype(BF16)
        _accumulate(dgq_ref, dgq)
        _accumulate(dgp_ref, dgp)

    return _row_call("mid_norm_bwd", body, t, [dn2, h1, dh2, mix], [g_pre_mlp, g_post_mix],
                     [(d, F32), (d, BF16)], [(SUB, d), (SUB, d)], deps=deps)


def _pre_norm_bwd(dn, h0, dh1, g_pre_mix, deps=()):
    t, d = h0.shape

    def body(dn_ref, h0_ref, dh1_ref, g_ref, dh0_ref, dg_ref):
        h0 = h0_ref[...]
        r = _rms(h0)
        dx, dg = _rms_bwd(dn_ref[...], h0 * r, r, g_ref[...])
        dh0_ref[...] = dh1_ref[...] + dx
        _accumulate(dg_ref, dg)

    return _row_call("pre_norm_bwd", body, t, [dn, h0, dh1], [g_pre_mix], [(d, F32)], [(SUB, d)], deps=deps)


def _layer_norm_silu(a1, ln_g, ln_b):
    t, c = a1.shape

    def body(a1_ref, g_ref, b_ref, a3_ref):
        a = a1_ref[...]
        mu = jnp.mean(a, axis=-1, keepdims=True)
        xc = a - mu
        rstd = lax.rsqrt(jnp.mean(xc * xc, axis=-1, keepdims=True) + LN_EPS)
        z = xc * rstd * g_ref[...] + b_ref[...]
        a3_ref[...] = (z * _sigmoid(z)).astype(BF16)

    return _row_call("layer_norm_silu", body, t, [a1], [ln_g, ln_b], [(c, BF16)], [])[0]


def _layer_norm_silu_bwd(da3, a1, ln_g, ln_b, deps=()):
    t, c = a1.shape

    def body(da3_ref, a1_ref, g_ref, b_ref, da1_ref, dg_ref, db_ref):
        a = a1_ref[...]
        g = g_ref[...]
        mu = jnp.mean(a, axis=-1, keepdims=True)
        xc = a - mu
        rstd = lax.rsqrt(jnp.mean(xc * xc, axis=-1, keepdims=True) + LN_EPS)
        xhat = xc * rstd
        z = xhat * g + b_ref[...]
        sg = _sigmoid(z)
        dz = da3_ref[...] * (sg * (1.0 + z * (1.0 - sg)))
        dxhat = dz * g
        da1_ref[...] = rstd * (dxhat - jnp.mean(dxhat, axis=-1, keepdims=True)
                               - xhat * jnp.mean(dxhat * xhat, axis=-1, keepdims=True))
        _accumulate(dg_ref, _colsum8(dz * xhat))
        _accumulate(db_ref, _colsum8(dz))

    return _row_call("layer_norm_silu_bwd", body, t, [da3, a1], [ln_g, ln_b], [(c, F32)], [(SUB, c), (SUB, c)], deps=deps)


def _branch_merge(a3, s, wpw, wso, proj, b_gates, d, deps=()):
    t, cols = proj.shape
    nblk, k, cb = wpw.shape
    w = 1024
    nh = d // w
    per = w // cb
    ga0 = (cols - 2 * d) // w
    tm = _row_tile(t)

    def body(a3_ref, s_ref, wpw_ref, wso_ref, *rest):
        pa_refs, pb_refs, bg_ref = rest[:nh], rest[nh:2 * nh], rest[2 * nh]
        ya_ref, yb_ref, ga_ref, gb_ref, m_ref = rest[2 * nh + 1 + len(deps):]
        a3v, sv = a3_ref[...], s_ref[...]
        for b in range(nblk):
            here = slice(b * cb, (b + 1) * cb)
            local = slice((b % per) * cb, (b % per + 1) * cb)
            ya = jnp.dot(a3v, wpw_ref[b], preferred_element_type=F32)
            yb = jnp.dot(sv, wso_ref[b], preferred_element_type=F32)
            ga = _sigmoid(pa_refs[b // per][:, local] + bg_ref[:, here])
            gb = _sigmoid(pb_refs[b // per][:, local] + bg_ref[:, d + b * cb:d + (b + 1) * cb])
            ya_ref[:, here] = ya.astype(BF16)
            yb_ref[:, here] = yb.astype(BF16)
            ga_ref[:, here] = ga.astype(BF16)
            gb_ref[:, here] = gb.astype(BF16)
            m_ref[:, here] = (ga * ya + gb * yb).astype(BF16)

    tile = pl.BlockSpec((tm, d), lambda i: (i, 0))
    return pl.pallas_call(
        body, name="branch_merge", grid=(t // tm,),
        in_specs=[pl.BlockSpec((tm, k), lambda i: (i, 0)), pl.BlockSpec((tm, k), lambda i: (i, 0)),
                  pl.BlockSpec((nblk, k, cb), lambda i: (0, 0, 0)), pl.BlockSpec((nblk, k, cb), lambda i: (0, 0, 0))]
                 + [pl.BlockSpec((tm, w), lambda i, h=h: (i, ga0 + h)) for h in range(2 * nh)]
                 + [pl.BlockSpec((1, 2 * d), lambda i: (0, 0))] + [ANY] * len(deps),
        out_specs=[tile] * 5,
        out_shape=[jax.ShapeDtypeStruct((t, d), BF16)] * 5,
        compiler_params=_params(1),
    )(a3, s, wpw, wso, *([proj] * (2 * nh)), b_gates, *deps)


def _gate_backward(dmix, wo_full, ga, gb, ya, yb, cols, tm, deps=()):
    t, d = ya.shape
    w = 1024
    nh = d // w
    ga0 = (cols - 2 * d) // w

    def body(dmix_ref, wo_ref, ga_ref, gb_ref, ya_ref, yb_ref, *rest):
        dya_ref, dyb_ref, dp_ref, dba_ref, dbb_ref, stage, sems = rest[len(deps):]
        h, i = pl.program_id(0), pl.program_id(1)
        dm = lax.dot_general(dmix_ref[...], wo_ref[...], (((1,), (1,)), ((), ())), preferred_element_type=F32)
        ga = ga_ref[...].astype(F32)
        gb = gb_ref[...].astype(F32)
        dya_ref[...] = (dm * ga).astype(BF16)
        dyb_ref[...] = (dm * gb).astype(BF16)
        dpa = dm * ya_ref[...].astype(F32) * ga * (1.0 - ga)
        dpb = dm * yb_ref[...].astype(F32) * gb * (1.0 - gb)
        stage[0] = dpa.astype(BF16)
        stage[1] = dpb.astype(BF16)
        rows = pl.ds(pl.multiple_of(i * tm, tm), tm)
        copies = [pltpu.make_async_copy(
            stage.at[g], dp_ref.at[rows, pl.ds(pl.multiple_of((ga0 + g * nh + h) * w, w), w)], sems.at[g])
            for g in range(2)]
        for cp in copies:
            cp.start()

        @pl.when(i == 0)
        def _():
            dba_ref[...] = _colsum8(dpa)
            dbb_ref[...] = _colsum8(dpb)

        @pl.when(i > 0)
        def _():
            dba_ref[...] += _colsum8(dpa)
            dbb_ref[...] += _colsum8(dpb)

        for cp in copies:
            cp.wait()

    tile = pl.BlockSpec((tm, w), lambda h, i: (i, h))
    return pl.pallas_call(
        body, name="gate_backward", grid=(nh, t // tm),
        in_specs=[pl.BlockSpec((tm, d), lambda h, i: (i, 0)),
                  pl.BlockSpec((w, d), lambda h, i: (h, 0)),
                  tile, tile, tile, tile] + [ANY] * len(deps),
        out_specs=[tile, tile, ANY,
                   pl.BlockSpec((SUB, w), lambda h, i: (0, h)),
                   pl.BlockSpec((SUB, w), lambda h, i: (0, h))],
        out_shape=[jax.ShapeDtypeStruct((t, d), BF16), jax.ShapeDtypeStruct((t, d), BF16),
                   jax.ShapeDtypeStruct((t, cols), BF16),
                   jax.ShapeDtypeStruct((SUB, d), F32), jax.ShapeDtypeStruct((SUB, d), F32)],
        scratch_shapes=[pltpu.VMEM((2, tm, w), BF16), pltpu.SemaphoreType.DMA((2,))],
        compiler_params=_params(2),
    )(dmix, wo_full, ga, gb, ya, yb, *deps)


def _causal_conv(xp_ref, w_ref, ntap, r0):
    n = CONV_CHUNK + CONV_PAD
    win = xp_ref[pl.ds(r0, n), :]
    acc = None
    for k in range(ntap):
        back = ntap - 1 - k
        shifted = pltpu.roll(win, n - (CONV_PAD - back), 0)
        term = w_ref[k:k + 1, :] * shifted[:CONV_CHUNK]
        acc = term if acc is None else acc + term
    return acc


def _anticausal_conv(xp_ref, w_ref, ntap, r0):
    n = CONV_CHUNK + CONV_PAD
    win = xp_ref[pl.ds(pl.multiple_of(CONV_PAD + r0, CONV_PAD), n), :]
    acc = None
    for k in range(ntap):
        ahead = ntap - 1 - k
        shifted = win if ahead == 0 else pltpu.roll(win, n - ahead, 0)
        term = w_ref[k:k + 1, :] * shifted[:CONV_CHUNK]
        acc = term if acc is None else acc + term
    return acc


def _conv_weight_grad(dw_ref, d_chunk, xp_ref, ntap, r0):
    n = CONV_CHUNK + CONV_PAD
    win = xp_ref[pl.ds(r0, n), :]
    for k in range(ntap):
        back = ntap - 1 - k
        shifted = pltpu.roll(win, n - (CONV_PAD - back), 0)
        dw_ref[k * SUB:(k + 1) * SUB, :] += _colsum8(d_chunk * shifted[:CONV_CHUNK])


def _zero_pads(ref, t):
    ref[0:CONV_PAD, :] = jnp.zeros((CONV_PAD, LANE), F32)
    ref[CONV_PAD + t:CONV_PAD + t + CONV_PAD, :] = jnp.zeros((CONV_PAD, LANE), F32)


def _for_chunks(t, fn):
    def step(idx, carry):
        fn(pl.multiple_of(idx * CONV_CHUNK, CONV_CHUNK))
        return carry

    lax.fori_loop(0, t // CONV_CHUNK, step, 0)


def _conv_forward(proj, conf_w, conf_b, short_w, dc, deps=()):
    t = proj.shape[0]
    nc = dc // LANE

    def body(av_ref, ag_ref, bg_ref, cg_ref, v_ref, cw_ref, cb_ref, sw_ref, *rest):
        a1_ref, s_ref, xa, xb = rest[len(deps):]
        _zero_pads(xa, t)
        _zero_pads(xb, t)
        xa[CONV_PAD:CONV_PAD + t, :] = av_ref[...] * _sigmoid(ag_ref[...])
        xb[CONV_PAD:CONV_PAD + t, :] = cg_ref[...] * v_ref[...]

        def chunk(r0):
            rs = pl.ds(r0, CONV_CHUNK)
            a1_ref[rs, :] = _causal_conv(xa, cw_ref, CONF_K, r0) + cb_ref[...]
            s_ref[rs, :] = (bg_ref[rs, :] * _causal_conv(xb, sw_ref, SHORT_K, r0)).astype(BF16)

        _for_chunks(t, chunk)

    col = lambda g: pl.BlockSpec((t, LANE), lambda c, g=g: (0, g * nc + c))
    return pl.pallas_call(
        body, name="conv_forward", grid=(nc,),
        in_specs=[col(0), col(1), col(2), col(3), col(4),
                  pl.BlockSpec((CONF_K, LANE), lambda c: (0, c)),
                  pl.BlockSpec((1, LANE), lambda c: (0, c)),
                  pl.BlockSpec((SHORT_K, LANE), lambda c: (0, c))] + [ANY] * len(deps),
        out_specs=[pl.BlockSpec((t, LANE), lambda c: (0, c)), pl.BlockSpec((t, LANE), lambda c: (0, c))],
        out_shape=[jax.ShapeDtypeStruct((t, dc), F32), jax.ShapeDtypeStruct((t, dc), BF16)],
        scratch_shapes=[pltpu.VMEM((t + 2 * CONV_PAD, LANE), F32), pltpu.VMEM((t + 2 * CONV_PAD, LANE), F32)],
        compiler_params=_params(1),
    )(proj, proj, proj, proj, proj, conf_w, conf_b, short_w, *deps)


def _conv_backward(dproj, proj, da1, ds, conf_w, short_w, dc):
    t = proj.shape[0]
    nc = dc // LANE

    def body(dp_in, av_ref, ag_ref, bg_ref, cg_ref, v_ref, da1_ref, ds_ref, cw_ref, sw_ref,
             dp_ref, dcw_ref, dcb_ref, dsw_ref, xa, xb, da, db, stage, sems):
        del dp_in
        c = pl.program_id(0)
        for ref in (xa, xb, da, db):
            _zero_pads(ref, t)
        xa[CONV_PAD:CONV_PAD + t, :] = av_ref[...] * _sigmoid(ag_ref[...])
        xb[CONV_PAD:CONV_PAD + t, :] = cg_ref[...] * v_ref[...]
        da[CONV_PAD:CONV_PAD + t, :] = da1_ref[...]
        dcw_ref[...] = jnp.zeros(dcw_ref.shape, F32)
        dsw_ref[...] = jnp.zeros(dsw_ref.shape, F32)
        dcb_ref[...] = jnp.zeros(dcb_ref.shape, F32)

        def through_gate(r0):
            rs = pl.ds(r0, CONV_CHUNK)
            ds_c = ds_ref[rs, :]
            stage[2, rs, :] = (ds_c * _causal_conv(xb, sw_ref, SHORT_K, r0)).astype(BF16)
            db[pl.ds(pl.multiple_of(CONV_PAD + r0, CONV_PAD), CONV_CHUNK), :] = ds_c * bg_ref[rs, :]

        _for_chunks(t, through_gate)

        def through_convs(r0):
            rs = pl.ds(r0, CONV_CHUNK)
            da0 = _anticausal_conv(da, cw_ref, CONF_K, r0)
            sg = _sigmoid(ag_ref[rs, :])
            stage[0, rs, :] = (da0 * sg).astype(BF16)
            stage[1, rs, :] = (da0 * av_ref[rs, :] * sg * (1.0 - sg)).astype(BF16)
            dcv = _anticausal_conv(db, sw_ref, SHORT_K, r0)
            stage[3, rs, :] = (dcv * v_ref[rs, :]).astype(BF16)
            stage[4, rs, :] = (dcv * cg_ref[rs, :]).astype(BF16)
            da1_c = da1_ref[rs, :]
            _conv_weight_grad(dcw_ref, da1_c, xa, CONF_K, r0)
            _conv_weight_grad(dsw_ref, ds_ref[rs, :] * bg_ref[rs, :], xb, SHORT_K, r0)
            dcb_ref[...] += _colsum8(da1_c)

        _for_chunks(t, through_convs)
        copies = [pltpu.make_async_copy(
            stage.at[g], dp_ref.at[:, pl.ds(pl.multiple_of((g * nc + c) * LANE, LANE), LANE)], sems.at[g])
            for g in range(5)]
        for cp in copies:
            cp.start()
        for cp in copies:
            cp.wait()

    col = lambda g: pl.BlockSpec((t, LANE), lambda c, g=g: (0, g * nc + c))
    blk = pl.BlockSpec((t, LANE), lambda c: (0, c))
    return pl.pallas_call(
        body, name="conv_backward", grid=(nc,),
        in_specs=[ANY, col(0), col(1), col(2), col(3), col(4), blk, blk,
                  pl.BlockSpec((CONF_K, LANE), lambda c: (0, c)),
                  pl.BlockSpec((SHORT_K, LANE), lambda c: (0, c))],
        out_specs=[ANY,
                   pl.BlockSpec((CONF_K * SUB, LANE), lambda c: (0, c)),
                   pl.BlockSpec((SUB, LANE), lambda c: (0, c)),
                   pl.BlockSpec((SHORT_K * SUB, LANE), lambda c: (0, c))],
        out_shape=[jax.ShapeDtypeStruct(dproj.shape, dproj.dtype),
                   jax.ShapeDtypeStruct((CONF_K * SUB, dc), F32),
                   jax.ShapeDtypeStruct((SUB, dc), F32),
                   jax.ShapeDtypeStruct((SHORT_K * SUB, dc), F32)],
        scratch_shapes=[pltpu.VMEM((t + 2 * CONV_PAD, LANE), F32)] * 4
                       + [pltpu.VMEM((5, t, LANE), BF16), pltpu.SemaphoreType.DMA((5,))],
        input_output_aliases={0: 0},
        compiler_params=_params(1),
    )(dproj, proj, proj, proj, proj, proj, da1, ds, conf_w, short_w)


def _adamw_math(w, g, m, v):
    m = ADAM_B1 * m + (1.0 - ADAM_B1) * g
    v = ADAM_B2 * v + (1.0 - ADAM_B2) * (g * g)
    m_hat = m / (1.0 - ADAM_B1 ** ADAM_STEP)
    v_hat = v / (1.0 - ADAM_B2 ** ADAM_STEP)
    delta = -ADAM_LR * (m_hat / (jnp.sqrt(v_hat) + ADAM_EPS) + ADAM_WD * w)
    return delta, m, v


def _cast_into_slot(name, w, me_arr, deps=()):
    r, c = w.shape
    tr = 256

    def body(me_ref, w_ref, *rest):
        del me_ref
        rest[-1][0] = w_ref[...].astype(BF16)

    return pl.pallas_call(
        body, name=name,
        grid_spec=pltpu.PrefetchScalarGridSpec(
            num_scalar_prefetch=1, grid=(r // tr,),
            in_specs=[pl.BlockSpec((tr, c), lambda i, me: (i, 0))] + [ANY] * len(deps),
            out_specs=pl.BlockSpec((1, tr, c), lambda i, me: (me[0], i, 0))),
        out_shape=jax.ShapeDtypeStruct((N_DEV, r, c), BF16),
        compiler_params=_params(1),
    )(me_arr, w, *deps)


def _chip_sum(name, full, from_sibling, me_arr):
    _, r, c = full.shape
    tr = min(r, 512)

    def body(me_ref, full_ref, sib_ref, sums_ref):
        del me_ref
        sums_ref[0] = (full_ref[0].astype(F32) + sib_ref[0].astype(F32)).astype(BF16)

    other = lambda k, me: (me[0] // 2 + 1 + k) % 4
    return pl.pallas_call(
        body, name=name,
        grid_spec=pltpu.PrefetchScalarGridSpec(
            num_scalar_prefetch=1, grid=(r // tr, 3),
            in_specs=[pl.BlockSpec((1, tr, c), lambda i, k, me: (2 * other(k, me) + me[0] % 2, i, 0)),
                      pl.BlockSpec((1, tr, c), lambda i, k, me: (other(k, me), i, 0))],
            out_specs=pl.BlockSpec((1, tr, c), lambda i, k, me: (other(k, me), i, 0))),
        out_shape=jax.ShapeDtypeStruct((4, r, c), BF16),
        compiler_params=_params(2),
    )(me_arr, full, from_sibling)


def _adamw_shard(name, w, m, v, parts, me_arr, deps=()):
    r, c = w.shape
    tr = min(256, r // len(parts))
    np_ = len(parts)
    per = r // np_ // tr

    def body(me_ref, w_ref, m_ref, v_ref, *rest):
        g_out, d_out, m_out, v_out = rest[5 * np_ + len(deps):]
        g = None
        for p in range(np_):
            gp = rest[5 * p][...]
            for l_ref in rest[5 * p + 1:5 * p + 5]:
                gp = gp + l_ref[0].astype(F32)
            g = gp if g is None else jnp.where(pl.program_id(0) // per == p, gp, g)
        delta, m_new, v_new = _adamw_math(w_ref[...], g, m_ref[...], v_ref[...])
        g_out[...] = g
        d_out[...] = delta
        m_out[...] = m_new
        v_out[...] = v_new

    tile = pl.BlockSpec((tr, c), lambda i, me: (i, 0))
    part_specs, part_args = [], []
    for p, (g_own, from_sibling, landed) in enumerate(parts):
        row = lambda i, p=p: jnp.clip(i - p * per, 0, per - 1)
        part_specs.append(pl.BlockSpec((tr, c), lambda i, me, row=row: (row(i), 0)))
        part_specs += [pl.BlockSpec((1, tr, c), lambda i, me, k=k, row=row: ((me[0] // 2 + k) % 4, row(i), 0))
                       for k in range(4)]
        part_args += [g_own, from_sibling, landed, landed, landed]
    return pl.pallas_call(
        body, name=name,
        grid_spec=pltpu.PrefetchScalarGridSpec(
            num_scalar_prefetch=1, grid=(r // tr,),
            in_specs=[tile] * 3 + part_specs + [ANY] * len(deps), out_specs=[tile] * 4),
        out_shape=[jax.ShapeDtypeStruct((r, c), F32)] * 4,
        compiler_params=_params(1),
    )(me_arr, w, m, v, *part_args, *deps)


SMALL_W = 1024
VEC_ROWS = 16
LOSS_ROW = 15
META_ROW0 = 16
CONF_ROW0 = 64
SHORT_ROW0 = 96
SMALL_ROWS = 104


def _pack_small(vec_parts, dmeta, dcw, dsw, loss_blk, me_arr):
    widths = [p.shape[1] for p in vec_parts]
    nv = len(vec_parts)

    def body(me_ref, *refs):
        del me_ref
        parts, (dmeta_ref, dcw_ref, dsw_ref, loss_ref, out_ref) = refs[:nv], refs[nv:]
        out_ref[0] = jnp.zeros((SMALL_ROWS, SMALL_W), F32)
        out_ref[0, LOSS_ROW:LOSS_ROW + 1, 0:LANE] = loss_ref[0:1, :]
        row = 0
        for p_ref, wd in zip(parts, widths):
            s = jnp.sum(p_ref[...], axis=0, keepdims=True)
            for h in range(wd // SMALL_W):
                out_ref[0, row:row + 1, :] = s[:, h * SMALL_W:(h + 1) * SMALL_W]
                row += 1
        for h in range(dmeta_ref.shape[1] // SMALL_W):
            out_ref[0, META_ROW0 + h * N_META:META_ROW0 + (h + 1) * N_META, :] = dmeta_ref[:, h * SMALL_W:(h + 1) * SMALL_W]
        for k in range(CONF_K):
            out_ref[0, CONF_ROW0 + k:CONF_ROW0 + k + 1, :] = jnp.sum(dcw_ref[k * SUB:(k + 1) * SUB, :], axis=0, keepdims=True)
        for k in range(SHORT_K):
            out_ref[0, SHORT_ROW0 + k:SHORT_ROW0 + k + 1, :] = jnp.sum(dsw_ref[k * SUB:(k + 1) * SUB, :], axis=0, keepdims=True)

    ins = [*vec_parts, dmeta, dcw, dsw, loss_blk]
    return pl.pallas_call(
        body, name="pack_small",
        grid_spec=pltpu.PrefetchScalarGridSpec(
            num_scalar_prefetch=1, grid=(1,),
            in_specs=[pl.BlockSpec(a.shape, lambda i, me: (0, 0)) for a in ins],
            out_specs=pl.BlockSpec((1, SMALL_ROWS, SMALL_W), lambda i, me: (me[0], 0, 0))),
        out_shape=jax.ShapeDtypeStruct((N_DEV, SMALL_ROWS, SMALL_W), F32),
        compiler_params=_params(1),
    )(me_arr, *ins)


def _small_update(gathered, me_arr, vec_params, meta_p, conf_p, short_p):
    widths = [p[0].shape[1] for p in vec_params]
    nv = len(vec_params)
    mcols = meta_p[0].shape[1]
    per_row = SMALL_W // mcols

    def body(me_ref, gv_ref, gm_ref, gc_ref, gs_ref, *rest):
        del me_ref
        ins, outs = rest[:3 * (nv + 3)], rest[3 * (nv + 3):]

        def total(ref, r0, rows):
            s = ref[0, r0:r0 + rows, :]
            for dev in range(1, N_DEV):
                s = s + ref[dev, r0:r0 + rows, :]
            return s

        grads = []
        row = 0
        for wd in widths:
            pieces = [total(gv_ref, row + h, 1) for h in range(wd // SMALL_W)]
            grads.append(pieces[0] if len(pieces) == 1 else jnp.concatenate(pieces, axis=1))
            row += len(pieces)
        grads.append(total(gm_ref, 0, N_META))
        grads.append(total(gc_ref, 0, CONF_K))
        grads.append(total(gs_ref, 0, SHORT_K))
        loss = gv_ref[0, LOSS_ROW:LOSS_ROW + 1, 0:LANE]
        for dev in range(1, N_DEV):
            loss = loss + gv_ref[dev, LOSS_ROW:LOSS_ROW + 1, 0:LANE]
        outs[-1][...] = loss
        for idx, g in enumerate(grads):
            w_ref, m_ref, v_ref = ins[3 * idx:3 * idx + 3]
            delta, m_new, v_new = _adamw_math(w_ref[...], g, m_ref[...], v_ref[...])
            g_out, d_out, m_out, v_out = outs[4 * idx:4 * idx + 4]
            g_out[...] = g
            d_out[...] = delta
            m_out[...] = m_new
            v_out[...] = v_new

    params = list(vec_params) + [meta_p, conf_p, short_p]
    flat = [a for p in params for a in p]
    whole = lambda a: pl.BlockSpec(a.shape, lambda i, me: (0,) * a.ndim)
    outs = pl.pallas_call(
        body, name="small_update",
        grid_spec=pltpu.PrefetchScalarGridSpec(
            num_scalar_prefetch=1, grid=(1,),
            in_specs=[pl.BlockSpec((N_DEV, VEC_ROWS, SMALL_W), lambda i, me: (0, 0, 0)),
                      pl.BlockSpec((N_DEV, N_META, mcols),
                                   lambda i, me: (0, META_ROW0 // N_META + me[0] // per_row, me[0] % per_row)),
                      pl.BlockSpec((N_DEV, 32, LANE), lambda i, me: (0, CONF_ROW0 // 32, me[0])),
                      pl.BlockSpec((N_DEV, SUB, LANE), lambda i, me: (0, SHORT_ROW0 // SUB, me[0]))]
                     + [whole(a) for a in flat],
            out_specs=[whole(p[0]) for p in params for _ in range(4)]
                      + [pl.BlockSpec((1, LANE), lambda i, me: (0, 0))]),
        out_shape=[jax.ShapeDtypeStruct(p[0].shape, F32) for p in params for _ in range(4)]
                  + [jax.ShapeDtypeStruct((1, LANE), F32)],
        compiler_params=_params(1),
    )(me_arr, gathered, gathered, gathered, gathered, *flat)
    return [tuple(outs[4 * i:4 * i + 4]) for i in range(len(params))], outs[-1][0, 0]


def kernel(x, meta, g_pre_mix, w_in, b_gates, conf_dw_w, conf_dw_b, conf_ln_g, conf_ln_b, conf_w_pw, short_dw_w, short_w_out, w_o, g_post_mix, g_pre_mlp, w_up, w_down, g_post_mlp, loss_target, m_meta, m_g_pre_mix, m_w_in, m_b_gates, m_conf_dw_w, m_conf_dw_b, m_conf_ln_g, m_conf_ln_b, m_conf_w_pw, m_short_dw_w, m_short_w_out, m_w_o, m_g_post_mix, m_g_pre_mlp, m_w_up, m_w_down, m_g_post_mlp, v_meta, v_g_pre_mix, v_w_in, v_b_gates, v_conf_dw_w, v_conf_dw_b, v_conf_ln_g, v_conf_ln_b, v_conf_w_pw, v_short_dw_w, v_short_w_out, v_w_o, v_g_post_mix, v_g_pre_mlp, v_w_up, v_w_down, v_g_post_mlp):
    seq, d = x.shape[1], x.shape[2]
    dc = conf_w_pw.shape[1]
    t_real = N_META + seq
    t = -(-t_real // ROW_TILE) * ROW_TILE
    tm = t // 2
    assert tm % 16 == 0 and d % 1024 == 0 and dc % 1024 == 0
    x_idx, y_idx, c_idx = _position()
    me_arr = jnp.reshape(4 * x_idx + 2 * y_idx + c_idx, (1,)).astype(jnp.int32)

    big = [w_in[0], conf_w_pw[0], short_w_out[0], w_o[0], w_up[0], w_down[0]]
    big_names = ["w_in", "conf_w_pw", "short_w_out", "w_o", "w_up", "w_down"]
    groups = [[0], [1, 2, 3], [4], [5]]
    meta_g, cw_g, sw_g = _all_gather("gather_small_params", [meta, conf_dw_w[0], short_dw_w[0]])
    slots, tok = [], meta_g
    for g, idxs in enumerate(groups):
        slots.append([_cast_into_slot("cast_" + big_names[i], big[i], me_arr, deps=[tok]) for i in idxs])
        if g == 0:
            direct0 = _remote_start("gather0_direct_start", "gather_direct", slots[0], deps=[tok])
            tok = direct0[3]
    started = tok[0, 0] * 0.0

    def start_direct(g, deps):
        send, recv, bufs, tok = _remote_start("gather%d_direct_start" % g, "gather_direct", slots[g], deps=deps)
        return (send, recv, bufs), tok

    def relay(g, state, after):
        send, recv, bufs = state
        bufs = _remote_wait("gather%d_direct_wait" % g, "gather_direct", send, recv, bufs, len(bufs), after)
        send, recv, bufs, tok = _remote_start("gather%d_relay_start" % g, "gather_relay", bufs)
        return (send, recv, bufs), tok

    def gathered(g, state, after):
        send, recv, bufs = state
        bufs = _remote_wait("gather%d_relay_wait" % g, "gather_relay", send, recv, bufs, len(bufs), after)
        send, recv, bufs, tok = _remote_start("gather%d_diag_start" % g, "gather_diag", bufs)
        return _remote_wait("gather%d_diag_wait" % g, "gather_diag", send, recv, bufs, len(bufs), [tok])

    unshard =lambda g: jnp.transpose(g, (1, 0, 2)).reshape(g.shape[1], -1)
    meta_full, cw_full, sw_full = unshard(meta_g), unshard(cw_g), unshard(sw_g)

    zrows = jnp.zeros((t - t_real, d), F32) + started
    h0 = jnp.concatenate([meta_full, x[0], zrows], axis=0)
    tgt = jnp.concatenate([jnp.zeros((N_META, d), F32), loss_target[0], zrows], axis=0)
    n = _pre_norm(h0, g_pre_mix)
    relay0, tok = relay(0, direct0[:3], [n])
    direct1, tok = start_direct(1, [tok])
    direct2, tok = start_direct(2, [tok])
    win_g, = gathered(0, relay0, [tok])
    proj = _mm_cols("proj", n, win_g, tm=tm)[0]
    relay1, tok = relay(1, direct1, [proj])
    direct3, tok = start_direct(3, [tok])
    a1, s = _conv_forward(proj, cw_full, conf_dw_b, sw_full, dc, deps=[tok])
    a3 = _layer_norm_silu(a1, conf_ln_g, conf_ln_b)
    wpw_g, wso_g, wo_g = gathered(1, relay1, [a3])
    wo_full = wo_g.reshape(d, d)
    ya, yb, gate_a, gate_b, m_mix = _branch_merge(a3, s, wpw_g, wso_g, proj, b_gates, d)
    mix = _mm_rows("mix", m_mix, wo_full, tm=tm // 2, tn=d)
    relay2, tok = relay(2, direct2, [mix])
    relay3, tok = relay(3, direct3, [tok])
    h1, n2 = _post_mix(mix, h0, g_post_mix, g_pre_mlp, deps=[tok])
    wup_g, = gathered(2, relay2, [n2])

    def up_epilogue(acc):
        r = jnp.maximum(acc, 0.0)
        return r * r, r

    f, relu_up = _mm_cols("mlp_up", n2, wup_g, tm=tm, epilogue=up_epilogue, out_dtypes=(BF16, BF16))
    wdn_g, = gathered(3, relay3, [f])
    wdn_full = wdn_g.reshape(-1, d)
    fo = _mm_rows("mlp_down", f, wdn_full, tm=tm // 2, tn=512)
    dfo, dh2, dg_post_mlp, loss_blk = _loss_head(fo, h1, tgt, g_post_mlp, t_real)

    def reduce_start(tag, fulls, deps):
        lands = [lax.empty((4,) + g.shape[1:], BF16) for g in fulls]
        send, recv, bufs, tok = _remote_start("reduce_%s_d2d_start" % tag, "reduce_d2d", fulls, lands, deps=deps)
        return (send, recv, bufs), tok

    def reduce_middle(tag, state, owns, after):
        send, recv, bufs = state
        k = len(owns)
        bufs = _remote_wait("reduce_%s_d2d_wait" % tag, "reduce_d2d", send, recv, bufs, k, after)
        from_sibling = bufs[k:]
        sums = [_chip_sum("chip_sum_%s%d" % (tag, i), bufs[i], from_sibling[i], me_arr) for i in range(k)]
        lands = [lax.empty(sm.shape, BF16) for sm in sums]
        send, recv, bufs, tok = _remote_start("reduce_%s_ici_start" % tag, "reduce_ici", sums, lands)
        return (send, recv, bufs, list(zip(owns, from_sibling))), tok

    def reduce_finish(tag, state, after):
        send, recv, bufs, local = state
        k = len(local)
        bufs = _remote_wait("reduce_%s_ici_wait" % tag, "reduce_ici", send, recv, bufs, k, after)
        return [(own, sib, landed) for (own, sib), landed in zip(local, bufs[k:])]

    dup = _mm_nt_blocks("d_up", dfo, wdn_full, tm=tm, tkb=1024, extra=(relu_up,),
                        epilogue=lambda acc, r: (acc * (2.0 * r.astype(F32)),), out_dtypes=(BF16,))[0]
    gw_down, gw_down_own = _mm_tn("dw_down", f, dfo, me_arr, m=f.shape[1], n=d, tma=512, tn=d, sharded="rows")
    red_down, tok = reduce_start("down", [gw_down], ())
    dn2 = _mm_nt_acc("d_n2", dup, wup_g, tm=tm // 2, tn=512, deps=[tok])
    gw_up, gw_up_own = _mm_tn("dw_up", n2, dup, me_arr, m=d, n=dup.shape[1], tma=512, tn=1024, sharded="cols")
    red_down, tok = reduce_middle("down", red_down, [gw_down_own], [dn2])
    red_up, tok = reduce_start("up", [gw_up], [tok])
    dh1, dmix, dg_pre_mlp, dg_post_mix = _mid_norm_bwd(dn2, h1, dh2, mix, g_pre_mlp, g_post_mix, deps=[tok])
    dya, dyb, dproj, db_a, db_b = _gate_backward(dmix, wo_full, gate_a, gate_b, ya, yb, proj.shape[1], tm // 2)
    db_gates = jnp.concatenate([db_a, db_b], axis=1)
    red_up, tok = reduce_middle("up", red_up, [gw_up_own], [dya])
    gw_o, gw_o_own = _mm_tn("dw_o", m_mix, dmix, me_arr, m=d, n=d, tma=d // N_DEV, tn=d, sharded="rows", deps=[tok])
    da3 = _mm_nt_acc("d_a3", dya, wpw_g, tm=tm, tn=512)
    gw_pw, gw_pw_own = _mm_tn("dw_pw", a3, dya, me_arr, m=dc, n=d, tma=512, tn=d, sharded="cols")
    dsb = _mm_nt_acc("d_s", dyb, wso_g, tm=tm, tn=512)
    gw_so, gw_so_own = _mm_tn("dw_so", s, dyb, me_arr, m=dc, n=d, tma=512, tn=d, sharded="cols")
    red_mix, tok = reduce_start("mix", [gw_pw, gw_so, gw_o], ())
    da1, dln_g, dln_b = _layer_norm_silu_bwd(da3, a1, conf_ln_g, conf_ln_b, deps=[tok])
    dproj, dcw, dcb, dsw = _conv_backward(dproj, proj, da1, dsb, cw_full, sw_full, dc)
    red_mix, tok = reduce_middle("mix", red_mix, [gw_pw_own, gw_so_own, gw_o_own], [dcb])
    in_cb = w_in.shape[2]
    half = d // 2
    red_in = []
    for part in range(2):
        gw, own = _mm_tn("dw_in%d" % part, n, dproj, me_arr, m=half, n=proj.shape[1], tma=512, tn=2 * in_cb,
                         sharded="cols", a_off=part * (half // 512), deps=[tok])
        state, tok = reduce_start("in%d" % part, [gw], ())
        red_in.append((state, own))
    for part in range(2):
        state, own = red_in[part]
        red_in[part], tok = reduce_middle("in%d" % part, state, [own], [tok])
    dn = _mm_nt_acc("d_n", dproj, win_g, tm=tm // 2, tn=512, deps=[tok])
    dh0, dg_pre_mix = _pre_norm_bwd(dn, h0, dh1, g_pre_mix)
    grad_x = dh0[N_META:t_real][None]

    vec_parts = [dg_pre_mix, db_gates, dcb, dln_g, dln_b, dg_post_mix, dg_pre_mlp, dg_post_mlp]
    packed = _pack_small(vec_parts, dh0[:N_META], dcw, dsw, loss_blk, me_arr)
    send, recv, bufs, tok = _remote_start("small_grads_ici_start", "gather_ici", [packed])
    vec_names = ["g_pre_mix", "b_gates", "conf_dw_b", "conf_ln_g", "conf_ln_b", "g_post_mix", "g_pre_mlp", "g_post_mlp"]
    env = locals()
    results = {}

    def update(nm, parts, deps=()):
        res = _adamw_shard("adamw_" + nm, env[nm][0], env["m_" + nm][0], env["v_" + nm][0], parts, me_arr, deps=deps)
        results[nm] = tuple(r[None] for r in res)
        return res[0]

    done = [update("w_down", reduce_finish("down", red_down, [tok]), deps=[tok])]
    done.append(update("w_up", reduce_finish("up", red_up, done)))
    bufs = _remote_wait("small_grads_ici_wait", "gather_ici", send, recv, bufs, 1, done)
    send, recv, bufs, tok = _remote_start("small_grads_d2d_start", "gather_d2d", bufs)
    for nm, pair in zip(["conf_w_pw", "short_w_out", "w_o"], reduce_finish("mix", red_mix, [tok])):
        done.append(update(nm, [pair], deps=[tok]))
    small_g, = _remote_wait("small_grads_d2d_wait", "gather_d2d", send, recv, bufs, 1, done)
    triple = lambda nm, sq: tuple(env[p + nm][0] if sq else env[p + nm] for p in ("", "m_", "v_"))
    small, loss = _small_update(small_g, me_arr, [triple(nm, False) for nm in vec_names],
                                triple("meta", False), triple("conf_dw_w", True), triple("short_dw_w", True))
    for nm, res in zip(vec_names + ["meta"], small[:len(vec_names) + 1]):
        results[nm] = res
    results["conf_dw_w"] = tuple(r[None] for r in small[-2])
    results["short_dw_w"] = tuple(r[None] for r in small[-1])
    update("w_in", [reduce_finish("in%d" % part, red_in[part], [small[0][0]])[0] for part in range(2)])

    order = ["meta", "g_pre_mix", "w_in", "b_gates", "conf_dw_w", "conf_dw_b", "conf_ln_g", "conf_ln_b", "conf_w_pw",
             "short_dw_w", "short_w_out", "w_o", "g_post_mix", "g_pre_mlp", "w_up", "w_down", "g_post_mlp"]
    return (loss, grad_x, *[results[nm][0] for nm in order], *[results[nm][1] for nm in order],
            *[results[nm][2] for nm in order], *[results[nm][3] for nm in order])
```

```python
import jax
import jax.numpy as jnp
from jax import lax
from jax.experimental import pallas as pl
from jax.experimental.pallas import tpu as pltpu

N_DEV = 8
N_META = 16
CONF_K = 31
SHORT_K = 3
RMS_EPS = 1e-6
LN_EPS = 1e-5
ADAM_LR = 0.001
ADAM_B1 = 0.9
ADAM_B2 = 0.999
ADAM_EPS = 1e-08
ADAM_WD = 0.01
ADAM_STEP = 10

LANE = 128
SUB = 8
ROW_TILE = 128
CONV_PAD = 32
CONV_CHUNK = 128
VMEM_LIMIT = 56 * 1024 * 1024

F32 = jnp.float32
BF16 = jnp.bfloat16
MESH = pl.DeviceIdType.MESH
ANY = pl.BlockSpec(memory_space=pl.ANY)
HBM_SPEC = pl.BlockSpec(memory_space=pltpu.HBM)
SEM_SPEC = pl.BlockSpec(memory_space=pltpu.SEMAPHORE)
EFFECT = pltpu.SideEffectType.DATAFLOW_SIDE_EFFECTING


def _params(n_axes):
    return pltpu.CompilerParams(dimension_semantics=("arbitrary",) * n_axes, vmem_limit_bytes=VMEM_LIMIT)


def _sigmoid(z):
    return 0.5 * jnp.tanh(0.5 * z) + 0.5


def _colsum8(v):
    r, c = v.shape
    return jnp.sum(v.reshape(r // SUB, SUB, c), axis=0)


def _position():
    x, y, c = lax.axis_index("x"), lax.axis_index("y"), lax.axis_index("c")
    return x, y, c


def _flat(p):
    return 4 * p[0] + 2 * p[1] + p[2]


def _all_gather(name, shards, deps=()):
    n, nd = len(shards), len(deps)

    def body(*refs):
        ins, outs = refs[:n], refs[n + nd:2 * n + nd]
        send_sems, recv_sems, local_sems = refs[2 * n + nd:]
        x, y, c = _position()
        me, sibling = (x, y, c), (x, y, 1 - c)
        chips = [(1 - x, y), (x, 1 - y), (1 - x, 1 - y)]

        def copy(q, k, block, to, src=None):
            dst = outs[q].at[_flat(block)]
            return pltpu.make_async_remote_copy(
                src_ref=dst if src is None else src, dst_ref=dst,
                send_sem=send_sems.at[q, k], recv_sem=recv_sems.at[q, k],
                device_id=to, device_id_type=MESH)

        mine = [pltpu.make_async_copy(ins[q], outs[q].at[_flat(me)], local_sems.at[q]) for q in range(n)]
        for cp in mine:
            cp.start()
        first = []
        for q in range(n):
            first.append(copy(q, 0, me, sibling, src=ins[q]))
            for j, chip in enumerate(chips):
                first.append(copy(q, 1 + j, me, (*chip, c), src=ins[q]))
        for cp in first:
            cp.start()
        passed = []
        for q in range(n):
            for j, chip in enumerate(chips):
                copy(q, 1 + j, (*chip, c), me).wait_recv()
                fwd = copy(q, 4 + j, (*chip, c), sibling)
                fwd.start()
                passed.append(fwd)
        for q in range(n):
            copy(q, 0, sibling, me).wait_recv()
            for j, chip in enumerate(chips):
                copy(q, 4 + j, (*chip, 1 - c), me).wait_recv()
        for cp in first + passed:
            cp.wait_send()
        for cp in mine:
            cp.wait()

    return pl.pallas_call(
        body, name=name,
        in_specs=[ANY] * (n + nd), out_specs=[ANY] * n,
        out_shape=[jax.ShapeDtypeStruct((N_DEV,) + s.shape, s.dtype) for s in shards],
        scratch_shapes=[pltpu.SemaphoreType.DMA((n, 7)), pltpu.SemaphoreType.DMA((n, 7)),
                        pltpu.SemaphoreType.DMA((n,))],
    )(*shards, *deps)


N_COPIES = {"gather_ici": 4, "gather_d2d": 3, "gather_direct": 3, "gather_relay": 3, "gather_diag": 1,
            "reduce_d2d": 4, "reduce_ici": 3}


def _copy_plan(kind):
    x, y, c = _position()
    me, sibling = (x, y, c), (x, y, 1 - c)
    chips = [(1 - x, y), (x, 1 - y), (1 - x, 1 - y)]
    if kind == "gather_ici":
        return [(_flat(me), _flat(me), sibling)] + [(_flat(me), _flat(me), (*ch, c)) for ch in chips]
    if kind == "gather_d2d":
        return [(_flat((*ch, c)), _flat((*ch, c)), sibling) for ch in chips]
    if kind == "gather_direct":
        return [(_flat(me), _flat(me), sibling)] + [(_flat(me), _flat(me), (*ch, c)) for ch in chips[:2]]
    if kind == "gather_relay":
        held, to = (x ^ (1 - c), y ^ c, c), (x ^ c, y ^ (1 - c), c)
        return [(_flat(held), _flat(held), to)] + [(_flat((*ch, c)), _flat((*ch, c)), sibling) for ch in chips[:2]]
    if kind == "gather_diag":
        return [(_flat((*chips[2], c)), _flat((*chips[2], c)), sibling)]
    if kind == "reduce_d2d":
        return [(2 * chip + (1 - c), chip, sibling) for chip in range(4)]
    return [(2 * ch[0] + ch[1], 2 * x + y, (*ch, c)) for ch in chips]


def _planned_copies(kind, srcs, dsts, send_sems, recv_sems):
    plan = _copy_plan(kind)
    return [pltpu.make_async_remote_copy(
        src_ref=src.at[s_slot], dst_ref=dst.at[d_slot],
        send_sem=send_sems.at[q * len(plan) + k], recv_sem=recv_sems.at[q * len(plan) + k],
        device_id=to, device_id_type=MESH)
        for q, (src, dst) in enumerate(zip(srcs, dsts)) for k, (s_slot, d_slot, to) in enumerate(plan)]


def _remote_start(name, kind, srcs, lands=None, deps=()):
    n = len(srcs)
    bufs = list(srcs) + ([] if lands is None else list(lands))
    nb, nd = len(bufs), len(deps)
    nsem = n * N_COPIES[kind]

    def body(*refs):
        ins = refs[:nb]
        send_sems, recv_sems = refs[nb + nd], refs[nb + nd + 1]
        token = refs[-1]
        for cp in _planned_copies(kind, ins[:n], ins[:n] if lands is None else ins[n:], send_sems, recv_sems):
            cp.start()
        token[...] = jnp.zeros_like(token)

    outs = pl.pallas_call(
        body, name=name,
        out_shape=(pltpu.SemaphoreType.DMA((nsem,)), pltpu.SemaphoreType.DMA((nsem,)),
                   *[pltpu.HBM(b.shape, b.dtype) for b in bufs], jax.ShapeDtypeStruct((SUB, LANE), F32)),
        in_specs=[HBM_SPEC] * nb + [ANY] * nd,
        out_specs=(SEM_SPEC, SEM_SPEC, *[HBM_SPEC] * nb, pl.BlockSpec(memory_space=pltpu.VMEM)),
        input_output_aliases={i: 2 + i for i in range(nb)},
        compiler_params=pltpu.CompilerParams(has_side_effects=EFFECT),
    )(*[pltpu.with_memory_space_constraint(b, pltpu.HBM) for b in bufs], *deps)
    return outs[0], outs[1], list(outs[2:2 + nb]), outs[-1]


def _remote_wait(name, kind, send_sems, recv_sems, bufs, n, after):
    nb, na = len(bufs), len(after)
    same = nb == n

    def body(*refs):
        ins = refs[:nb]
        sends, recvs = refs[nb], refs[nb + 1]
        for cp in _planned_copies(kind, ins[:n], ins[:n] if same else ins[n:], sends, recvs):
            cp.wait_send()
            cp.wait_recv()

    outs = pl.pallas_call(
        body, name=name,
        out_shape=[pltpu.HBM(b.shape, b.dtype) for b in bufs],
        in_specs=[HBM_SPEC] * nb + [SEM_SPEC, SEM_SPEC] + [ANY] * na,
        out_specs=[HBM_SPEC] * nb,
        input_output_aliases={i: i for i in range(nb)},
        compiler_params=pltpu.CompilerParams(has_side_effects=EFFECT),
    )(*bufs, send_sems, recv_sems, *after)
    return list(outs)


def _mm_cols(name, a, w, *, tm, nb=1, epilogue=None, out_dtypes=(F32,)):
    t, k = a.shape
    nblk, _, cb = w.shape

    def body(a_ref, w_ref, *o_refs):
        av = a_ref[...]
        for b in range(nb):
            acc = jnp.dot(av, w_ref[b], preferred_element_type=F32)
            outs = (acc,) if epilogue is None else epilogue(acc)
            for o_ref, o in zip(o_refs, outs):
                o_ref[:, b * cb:(b + 1) * cb] = o.astype(o_ref.dtype)

    return pl.pallas_call(
        body, name=name, grid=(nblk // nb, t // tm),
        in_specs=[pl.BlockSpec((tm, k), lambda j, i: (i, 0)),
                  pl.BlockSpec((nb, k, cb), lambda j, i: (j, 0, 0))],
        out_specs=[pl.BlockSpec((tm, nb * cb), lambda j, i: (i, j)) for _ in out_dtypes],
        out_shape=[jax.ShapeDtypeStruct((t, nblk * cb), dt) for dt in out_dtypes],
        compiler_params=_params(2),
    )(a, w)


def _mm_rows(name, a, w2d, *, tm, tn):
    t, kf = a.shape
    n = w2d.shape[1]

    def body(a_ref, w_ref, o_ref):
        o_ref[...] = jnp.dot(a_ref[...], w_ref[...], preferred_element_type=F32)

    return pl.pallas_call(
        body, name=name, grid=(t // tm, n // tn),
        in_specs=[pl.BlockSpec((tm, kf), lambda i, j: (i, 0)),
                  pl.BlockSpec((kf, tn), lambda i, j: (0, j))],
        out_specs=pl.BlockSpec((tm, tn), lambda i, j: (i, j)),
        out_shape=jax.ShapeDtypeStruct((t, n), F32),
        compiler_params=_params(2),
    )(a, w2d)


def _mm_nt_acc(name, dy, w, *, tm, tn, col_off=0, deps=()):
    t = dy.shape[0]
    nblk, k, cb = w.shape

    def body(dy_ref, w_ref, *rest):
        acc = None
        for b in range(nblk):
            d = lax.dot_general(dy_ref[:, b * cb:(b + 1) * cb], w_ref[b], (((1,), (1,)), ((), ())),
                                preferred_element_type=F32)
            acc = d if acc is None else acc + d
        rest[-1][...] = acc

    return pl.pallas_call(
        body, name=name, grid=(t // tm, k // tn),
        in_specs=[pl.BlockSpec((tm, nblk * cb), lambda i, j: (i, col_off)),
                  pl.BlockSpec((nblk, tn, cb), lambda i, j: (0, j, 0))] + [ANY] * len(deps),
        out_specs=pl.BlockSpec((tm, tn), lambda i, j: (i, j)),
        out_shape=jax.ShapeDtypeStruct((t, k), F32),
        compiler_params=_params(2),
    )(dy, w, *deps)


def _mm_nt_blocks(name, dy, w2d, *, tm, tkb, extra=(), epilogue=None, out_dtypes=(F32,)):
    t, n = dy.shape
    kf = w2d.shape[0]
    ne = len(extra)

    def body(dy_ref, w_ref, *rest):
        acc = lax.dot_general(dy_ref[...], w_ref[...], (((1,), (1,)), ((), ())), preferred_element_type=F32)
        outs = (acc,) if epilogue is None else epilogue(acc, *[e[...] for e in rest[:ne]])
        for o_ref, o in zip(rest[ne:], outs):
            o_ref[...] = o.astype(o_ref.dtype)

    return pl.pallas_call(
        body, name=name, grid=(kf // tkb, t // tm),
        in_specs=[pl.BlockSpec((tm, n), lambda kb, i: (i, 0)),
                  pl.BlockSpec((tkb, n), lambda kb, i: (kb, 0))]
                 + [pl.BlockSpec((tm, tkb), lambda kb, i: (i, kb)) for _ in extra],
        out_specs=[pl.BlockSpec((tm, tkb), lambda kb, i: (i, kb)) for _ in out_dtypes],
        out_shape=[jax.ShapeDtypeStruct((t, kf), dt) for dt in out_dtypes],
        compiler_params=_params(2),
    )(dy, w2d, *extra)


def _mm_tn(name, a, b, me_arr, *, m, n, tma, tn, sharded, a_off=0, b_off=0, deps=()):
    t = a.shape[0]
    if sharded == "cols":
        cb = n // N_DEV
        nb, q = max(tn // cb, 1), max(cb // tn, 1)
        tw = tn // nb
        full_shape, own_shape = (N_DEV, m, cb), (m, cb)
        full_spec = pl.BlockSpec((nb, tma, tw), lambda i, j, me: (j // q, i, j % q))
    else:
        kb = m // N_DEV
        p = kb // tma
        nb, tw = 1, tn
        full_shape, own_shape = (m, n), (kb, n)
        full_spec = pl.BlockSpec((tma, tn), lambda i, j, me: (i, j))

    def body(me_ref, a_ref, b_ref, *rest):
        full_ref, own_ref, stage, sem = rest[len(deps):]
        i, j = pl.program_id(0), pl.program_id(1)
        acc = lax.dot_general(a_ref[...], b_ref[...], (((0,), (0,)), ((), ())), preferred_element_type=F32)
        for blk in range(nb):
            part = acc[:, blk * tw:(blk + 1) * tw]
            if sharded == "cols":
                full_ref[blk] = part.astype(BF16)
                owner, r0, c0 = (j // q) * nb + blk, i * tma, (j % q) * tw
            else:
                full_ref[...] = part.astype(BF16)
                owner, r0, c0 = i // p, (i % p) * tma, j * tn

            @pl.when(owner == me_ref[0])
            def _():
                stage[...] = part
                cp = pltpu.make_async_copy(
                    stage, own_ref.at[pl.ds(pl.multiple_of(r0, tma), tma), pl.ds(pl.multiple_of(c0, tw), tw)], sem)
                cp.start()
                cp.wait()

    full, own = pl.pallas_call(
        body, name=name,
        grid_spec=pltpu.PrefetchScalarGridSpec(
            num_scalar_prefetch=1, grid=(m // tma, n // tn),
            in_specs=[pl.BlockSpec((t, tma), lambda i, j, me: (0, a_off + i)),
                      pl.BlockSpec((t, tn), lambda i, j, me: (0, b_off + j))] + [ANY] * len(deps),
            out_specs=[full_spec, ANY],
            scratch_shapes=[pltpu.VMEM((tma, tw), F32), pltpu.SemaphoreType.DMA(())]),
        out_shape=[jax.ShapeDtypeStruct(full_shape, BF16), jax.ShapeDtypeStruct(own_shape, F32)],
        compiler_params=_params(2),
    )(me_arr, a, b, *deps)
    if sharded == "rows":
        full = full.reshape(N_DEV, m // N_DEV, n)
    return full, own


def _row_tile(t):
    return t // 8 if (t // 8) % 16 == 0 else ROW_TILE


def _row_call(name, body, t, row_ins, full_ins, row_outs, acc_outs, scratch=(), deps=()):
    tm = _row_tile(t)
    nin = len(row_ins) + len(full_ins)

    def without_deps(*refs):
        body(*refs[:nin], *refs[nin + len(deps):])

    return pl.pallas_call(
        without_deps, name=name, grid=(t // tm,),
        in_specs=[pl.BlockSpec((tm, a.shape[1]), lambda i: (i, 0)) for a in row_ins]
                 + [pl.BlockSpec(a.shape, lambda i: (0, 0)) for a in full_ins] + [ANY] * len(deps),
        out_specs=[pl.BlockSpec((tm, c), lambda i: (i, 0)) for c, _ in row_outs]
                  + [pl.BlockSpec((r, c), lambda i: (0, 0)) for r, c in acc_outs],
        out_shape=[jax.ShapeDtypeStruct((t, c), dt) for c, dt in row_outs]
                  + [jax.ShapeDtypeStruct((r, c), F32) for r, c in acc_outs],
        scratch_shapes=list(scratch),
        compiler_params=_params(1),
    )(*row_ins, *full_ins, *deps)


ROW_CHUNK = 16


def _by_chunks(t, fn):
    def step(idx, carry):
        fn(pl.ds(pl.multiple_of(idx * ROW_CHUNK, ROW_CHUNK), ROW_CHUNK))
        return carry

    lax.fori_loop(0, _row_tile(t) // ROW_CHUNK, step, 0)


def _zero_at_start(*refs):
    @pl.when(pl.program_id(0) == 0)
    def _():
        for ref in refs:
            ref[...] = jnp.zeros(ref.shape, F32)


def _rms(v):
    return lax.rsqrt(jnp.mean(v * v, axis=-1, keepdims=True) + RMS_EPS)


def _rms_bwd(dout, u, r, g):
    du = dout * g
    dx = r * (du - u * jnp.mean(du * u, axis=-1, keepdims=True))
    return dx, _colsum8(dout * u)


def _pre_norm(h0, g):
    t, d = h0.shape

    def body(h_ref, g_ref, n_ref):
        def chunk(rows):
            h = h_ref[rows, :]
            n_ref[rows, :] = (h * _rms(h) * g_ref[...]).astype(BF16)

        _by_chunks(t, chunk)

    return _row_call("pre_norm", body, t, [h0], [g], [(d, BF16)], [])[0]


def _post_mix(mix, h0, g_post, g_pre, deps=()):
    t, d = h0.shape

    def body(mix_ref, h0_ref, gp_ref, gq_ref, h1_ref, n2_ref):
        def chunk(rows):
            mix_v = mix_ref[rows, :]
            h1 = h0_ref[rows, :] + mix_v * _rms(mix_v) * gp_ref[...]
            h1_ref[rows, :] = h1
            n2_ref[rows, :] = (h1 * _rms(h1) * gq_ref[...]).astype(BF16)

        _by_chunks(t, chunk)

    return _row_call("post_mix", body, t, [mix, h0], [g_post, g_pre], [(d, F32), (d, BF16)], [], deps=deps)


def _loss_head(fo, h1, tgt, g_post_mlp, t_real):
    t, d = h1.shape
    tile = _row_tile(t)

    def body(fo_ref, h1_ref, tgt_ref, g_ref, dfo_ref, dh2_ref, dg_ref, loss_ref, lacc):
        i = pl.program_id(0)
        _zero_at_start(dg_ref, lacc)

        def chunk(rows):
            fo_v = fo_ref[rows, :]
            g = g_ref[...]
            r = _rms(fo_v)
            u = fo_v * r
            h2 = h1_ref[rows, :] + u * g
            row = i * tile + rows.start + lax.broadcasted_iota(jnp.int32, (ROW_CHUNK, 1), 0)
            valid = jnp.logical_and(row >= N_META, row < t_real)
            diff = jnp.where(valid, h2 - tgt_ref[rows, :], 0.0)
            dh2 = diff * (1.0 / d)
            dh2_ref[rows, :] = dh2
            dfo, dg = _rms_bwd(dh2, u, r, g)
            dfo_ref[rows, :] = dfo.astype(BF16)
            dg_ref[...] += dg
            lacc[...] += _colsum8(diff * diff)

        _by_chunks(t, chunk)

        @pl.when(i == pl.num_programs(0) - 1)
        def _():
            loss_ref[...] = jnp.full((SUB, LANE), (0.5 / d) * jnp.sum(lacc[...]), F32)

    return _row_call("loss_head", body, t, [fo, h1, tgt], [g_post_mlp],
                     [(d, BF16), (d, F32)], [(SUB, d), (SUB, LANE)], scratch=[pltpu.VMEM((SUB, d), F32)])


def _mid_norm_bwd(dn2, h1, dh2, mix, g_pre_mlp, g_post_mix, deps=()):
    t, d = h1.shape

    def body(dn2_ref, h1_ref, dh2_ref, mix_ref, gq_ref, gp_ref, dh1_ref, dmix_ref, dgq_ref, dgp_ref):
        _zero_at_start(dgq_ref, dgp_ref)

        def chunk(rows):
            h1 = h1_ref[rows, :]
            r3 = _rms(h1)
            dx, dgq = _rms_bwd(dn2_ref[rows, :], h1 * r3, r3, gq_ref[...])
            dh1 = dh2_ref[rows, :] + dx
            dh1_ref[rows, :] = dh1
            mix_v = mix_ref[rows, :]
            r2 = _rms(mix_v)
            dmix, dgp = _rms_bwd(dh1, mix_v * r2, r2, gp_ref[...])
            dmix_ref[rows, :] = dmix.astype(BF16)
            dgq_ref[...] += dgq
            dgp_ref[...] += dgp

        _by_chunks(t, chunk)

    return _row_call("mid_norm_bwd", body, t, [dn2, h1, dh2, mix], [g_pre_mlp, g_post_mix],
                     [(d, F32), (d, BF16)], [(SUB, d), (SUB, d)], deps=deps)


def _pre_norm_bwd(dn, h0, dh1, g_pre_mix, deps=()):
    t, d = h0.shape

    def body(dn_ref, h0_ref, dh1_ref, g_ref, dh0_ref, dg_ref):
        _zero_at_start(dg_ref)

        def chunk(rows):
            h0 = h0_ref[rows, :]
            r = _rms(h0)
            dx, dg = _rms_bwd(dn_ref[rows, :], h0 * r, r, g_ref[...])
            dh0_ref[rows, :] = dh1_ref[rows, :] + dx
            dg_ref[...] += dg

        _by_chunks(t, chunk)

    return _row_call("pre_norm_bwd", body, t, [dn, h0, dh1], [g_pre_mix], [(d, F32)], [(SUB, d)], deps=deps)


def _layer_norm_silu(a1, ln_g, ln_b):
    t, c = a1.shape

    def body(a1_ref, g_ref, b_ref, a3_ref):
        a = a1_ref[...]
        mu = jnp.mean(a, axis=-1, keepdims=True)
        xc = a - mu
        rstd = lax.rsqrt(jnp.mean(xc * xc, axis=-1, keepdims=True) + LN_EPS)
        z = xc * rstd * g_ref[...] + b_ref[...]
        a3_ref[...] = (z * _sigmoid(z)).astype(BF16)

    return _row_call("layer_norm_silu", body, t, [a1], [ln_g, ln_b], [(c, BF16)], [])[0]


def _layer_norm_silu_bwd(da3, a1, ln_g, ln_b, deps=()):
    t, c = a1.shape

    def body(da3_ref, a1_ref, g_ref, b_ref, da1_ref, dg_ref, db_ref):
        a = a1_ref[...]
        g = g_ref[...]
        mu = jnp.mean(a, axis=-1, keepdims=True)
        xc = a - mu
        rstd = lax.rsqrt(jnp.mean(xc * xc, axis=-1, keepdims=True) + LN_EPS)
        xhat = xc * rstd
        z = xhat * g + b_ref[...]
        sg = _sigmoid(z)
        dz = da3_ref[...] * (sg * (1.0 + z * (1.0 - sg)))
        dxhat = dz * g
        da1_ref[...] = rstd * (dxhat - jnp.mean(dxhat, axis=-1, keepdims=True)
                               - xhat * jnp.mean(dxhat * xhat, axis=-1, keepdims=True))
        _zero_at_start(dg_ref, db_ref)
        dg_ref[...] += _colsum8(dz * xhat)
        db_ref[...] += _colsum8(dz)

    return _row_call("layer_norm_silu_bwd", body, t, [da3, a1], [ln_g, ln_b], [(c, F32)], [(SUB, c), (SUB, c)], deps=deps)


def _branch_merge(a3, s, wpw, wso, proj, b_gates, d, deps=()):
    t, cols = proj.shape
    nblk, k, cb = wpw.shape
    w = 1024
    nh = d // w
    per = w // cb
    ga0 = (cols - 2 * d) // w
    tm = _row_tile(t)

    def body(a3_ref, s_ref, wpw_ref, wso_ref, *rest):
        pa_refs, pb_refs, bg_ref = rest[:nh], rest[nh:2 * nh], rest[2 * nh]
        ya_ref, yb_ref, ga_ref, gb_ref, m_ref = rest[2 * nh + 1 + len(deps):]
        a3v, sv = a3_ref[...], s_ref[...]
        for b in range(nblk):
            here = slice(b * cb, (b + 1) * cb)
            local = slice((b % per) * cb, (b % per + 1) * cb)
            ya = jnp.dot(a3v, wpw_ref[b], preferred_element_type=F32)
            yb = jnp.dot(sv, wso_ref[b], preferred_element_type=F32)
            ga = _sigmoid(pa_refs[b // per][:, local] + bg_ref[:, here])
            gb = _sigmoid(pb_refs[b // per][:, local] + bg_ref[:, d + b * cb:d + (b + 1) * cb])
            ya_ref[:, here] = ya.astype(BF16)
            yb_ref[:, here] = yb.astype(BF16)
            ga_ref[:, here] = ga.astype(BF16)
            gb_ref[:, here] = gb.astype(BF16)
            m_ref[:, here] = (ga * ya + gb * yb).astype(BF16)

    tile = pl.BlockSpec((tm, d), lambda i: (i, 0))
    return pl.pallas_call(
        body, name="branch_merge", grid=(t // tm,),
        in_specs=[pl.BlockSpec((tm, k), lambda i: (i, 0)), pl.BlockSpec((tm, k), lambda i: (i, 0)),
                  pl.BlockSpec((nblk, k, cb), lambda i: (0, 0, 0)), pl.BlockSpec((nblk, k, cb), lambda i: (0, 0, 0))]
                 + [pl.BlockSpec((tm, w), lambda i, h=h: (i, ga0 + h)) for h in range(2 * nh)]
                 + [pl.BlockSpec((1, 2 * d), lambda i: (0, 0))] + [ANY] * len(deps),
        out_specs=[tile] * 5,
        out_shape=[jax.ShapeDtypeStruct((t, d), BF16)] * 5,
        compiler_params=_params(1),
    )(a3, s, wpw, wso, *([proj] * (2 * nh)), b_gates, *deps)


def _gate_backward(dmix, wo_full, ga, gb, ya, yb, cols, tm, deps=()):
    t, d = ya.shape
    w = 1024
    nh = d // w
    ga0 = (cols - 2 * d) // w

    def body(dmix_ref, wo_ref, ga_ref, gb_ref, ya_ref, yb_ref, *rest):
        dya_ref, dyb_ref, dp_ref, dba_ref, dbb_ref, stage, sems = rest[len(deps):]
        h, i = pl.program_id(0), pl.program_id(1)
        dm = lax.dot_general(dmix_ref[...], wo_ref[...], (((1,), (1,)), ((), ())), preferred_element_type=F32)
        ga = ga_ref[...].astype(F32)
        gb = gb_ref[...].astype(F32)
        dya_ref[...] = (dm * ga).astype(BF16)
        dyb_ref[...] = (dm * gb).astype(BF16)
        dpa = dm * ya_ref[...].astype(F32) * ga * (1.0 - ga)
        dpb = dm * yb_ref[...].astype(F32) * gb * (1.0 - gb)
        stage[0] = dpa.astype(BF16)
        stage[1] = dpb.astype(BF16)
        rows = pl.ds(pl.multiple_of(i * tm, tm), tm)
        copies = [pltpu.make_async_copy(
            stage.at[g], dp_ref.at[rows, pl.ds(pl.multiple_of((ga0 + g * nh + h) * w, w), w)], sems.at[g])
            for g in range(2)]
        for cp in copies:
            cp.start()

        @pl.when(i == 0)
        def _():
            dba_ref[...] = _colsum8(dpa)
            dbb_ref[...] = _colsum8(dpb)

        @pl.when(i > 0)
        def _():
            dba_ref[...] += _colsum8(dpa)
            dbb_ref[...] += _colsum8(dpb)

        for cp in copies:
            cp.wait()

    tile = pl.BlockSpec((tm, w), lambda h, i: (i, h))
    return pl.pallas_call(
        body, name="gate_backward", grid=(nh, t // tm),
        in_specs=[pl.BlockSpec((tm, d), lambda h, i: (i, 0)),
                  pl.BlockSpec((w, d), lambda h, i: (h, 0)),
                  tile, tile, tile, tile] + [ANY] * len(deps),
        out_specs=[tile, tile, ANY,
                   pl.BlockSpec((SUB, w), lambda h, i: (0, h)),
                   pl.BlockSpec((SUB, w), lambda h, i: (0, h))],
        out_shape=[jax.ShapeDtypeStruct((t, d), BF16), jax.ShapeDtypeStruct((t, d), BF16),
                   jax.ShapeDtypeStruct((t, cols), BF16),
                   jax.ShapeDtypeStruct((SUB, d), F32), jax.ShapeDtypeStruct((SUB, d), F32)],
        scratch_shapes=[pltpu.VMEM((2, tm, w), BF16), pltpu.SemaphoreType.DMA((2,))],
        compiler_params=_params(2),
    )(dmix, wo_full, ga, gb, ya, yb, *deps)


def _causal_conv(xp_ref, w_ref, ntap, r0):
    n = CONV_CHUNK + CONV_PAD
    win = xp_ref[pl.ds(r0, n), :]
    acc = None
    for k in range(ntap):
        back = ntap - 1 - k
        shifted = pltpu.roll(win, n - (CONV_PAD - back), 0)
        term = w_ref[k:k + 1, :] * shifted[:CONV_CHUNK]
        acc = term if acc is None else acc + term
    return acc


def _anticausal_conv(xp_ref, w_ref, ntap, r0):
    n = CONV_CHUNK + CONV_PAD
    win = xp_ref[pl.ds(pl.multiple_of(CONV_PAD + r0, CONV_PAD), n), :]
    acc = None
    for k in range(ntap):
        ahead = ntap - 1 - k
        shifted = win if ahead == 0 else pltpu.roll(win, n - ahead, 0)
        term = w_ref[k:k + 1, :] * shifted[:CONV_CHUNK]
        acc = term if acc is None else acc + term
    return acc


def _conv_weight_grad(dw_ref, d_chunk, xp_ref, ntap, r0):
    n = CONV_CHUNK + CONV_PAD
    win = xp_ref[pl.ds(r0, n), :]
    for k in range(ntap):
        back = ntap - 1 - k
        shifted = pltpu.roll(win, n - (CONV_PAD - back), 0)
        dw_ref[k * SUB:(k + 1) * SUB, :] += _colsum8(d_chunk * shifted[:CONV_CHUNK])


def _zero_pads(ref, t):
    ref[0:CONV_PAD, :] = jnp.zeros((CONV_PAD, LANE), F32)
    ref[CONV_PAD + t:CONV_PAD + t + CONV_PAD, :] = jnp.zeros((CONV_PAD, LANE), F32)


def _for_chunks(t, fn):
    def step(idx, carry):
        fn(pl.multiple_of(idx * CONV_CHUNK, CONV_CHUNK))
        return carry

    lax.fori_loop(0, t // CONV_CHUNK, step, 0)


def _conv_forward(proj, conf_w, conf_b, short_w, dc, deps=()):
    t = proj.shape[0]
    nc = dc // LANE

    def body(av_ref, ag_ref, bg_ref, cg_ref, v_ref, cw_ref, cb_ref, sw_ref, *rest):
        a1_ref, s_ref, xa, xb = rest[len(deps):]
        _zero_pads(xa, t)
        _zero_pads(xb, t)
        xa[CONV_PAD:CONV_PAD + t, :] = av_ref[...] * _sigmoid(ag_ref[...])
        xb[CONV_PAD:CONV_PAD + t, :] = cg_ref[...] * v_ref[...]

        def chunk(r0):
            rs = pl.ds(r0, CONV_CHUNK)
            a1_ref[rs, :] = _causal_conv(xa, cw_ref, CONF_K, r0) + cb_ref[...]
            s_ref[rs, :] = (bg_ref[rs, :] * _causal_conv(xb, sw_ref, SHORT_K, r0)).astype(BF16)

        _for_chunks(t, chunk)

    col = lambda g: pl.BlockSpec((t, LANE), lambda c, g=g: (0, g * nc + c))
    return pl.pallas_call(
        body, name="conv_forward", grid=(nc,),
        in_specs=[col(0), col(1), col(2), col(3), col(4),
                  pl.BlockSpec((CONF_K, LANE), lambda c: (0, c)),
                  pl.BlockSpec((1, LANE), lambda c: (0, c)),
                  pl.BlockSpec((SHORT_K, LANE), lambda c: (0, c))] + [ANY] * len(deps),
        out_specs=[pl.BlockSpec((t, LANE), lambda c: (0, c)), pl.BlockSpec((t, LANE), lambda c: (0, c))],
        out_shape=[jax.ShapeDtypeStruct((t, dc), F32), jax.ShapeDtypeStruct((t, dc), BF16)],
        scratch_shapes=[pltpu.VMEM((t + 2 * CONV_PAD, LANE), F32), pltpu.VMEM((t + 2 * CONV_PAD, LANE), F32)],
        compiler_params=_params(1),
    )(proj, proj, proj, proj, proj, conf_w, conf_b, short_w, *deps)


def _conv_backward(dproj, proj, da1, ds, conf_w, short_w, dc):
    t = proj.shape[0]
    nc = dc // LANE

    def body(dp_in, av_ref, ag_ref, bg_ref, cg_ref, v_ref, da1_ref, ds_ref, cw_ref, sw_ref,
             dp_ref, dcw_ref, dcb_ref, dsw_ref, xa, xb, da, db, stage, sems):
        del dp_in
        c = pl.program_id(0)
        for ref in (xa, xb, da, db):
            _zero_pads(ref, t)
        xa[CONV_PAD:CONV_PAD + t, :] = av_ref[...] * _sigmoid(ag_ref[...])
        xb[CONV_PAD:CONV_PAD + t, :] = cg_ref[...] * v_ref[...]
        da[CONV_PAD:CONV_PAD + t, :] = da1_ref[...]
        dcw_ref[...] = jnp.zeros(dcw_ref.shape, F32)
        dsw_ref[...] = jnp.zeros(dsw_ref.shape, F32)
        dcb_ref[...] = jnp.zeros(dcb_ref.shape, F32)

        def through_gate(r0):
            rs = pl.ds(r0, CONV_CHUNK)
            ds_c = ds_ref[rs, :]
            stage[2, rs, :] = (ds_c * _causal_conv(xb, sw_ref, SHORT_K, r0)).astype(BF16)
            db[pl.ds(pl.multiple_of(CONV_PAD + r0, CONV_PAD), CONV_CHUNK), :] = ds_c * bg_ref[rs, :]

        _for_chunks(t, through_gate)

        def through_convs(r0):
            rs = pl.ds(r0, CONV_CHUNK)
            da0 = _anticausal_conv(da, cw_ref, CONF_K, r0)
            sg = _sigmoid(ag_ref[rs, :])
            stage[0, rs, :] = (da0 * sg).astype(BF16)
            stage[1, rs, :] = (da0 * av_ref[rs, :] * sg * (1.0 - sg)).astype(BF16)
            dcv = _anticausal_conv(db, sw_ref, SHORT_K, r0)
            stage[3, rs, :] = (dcv * v_ref[rs, :]).astype(BF16)
            stage[4, rs, :] = (dcv * cg_ref[rs, :]).astype(BF16)
            da1_c = da1_ref[rs, :]
            _conv_weight_grad(dcw_ref, da1_c, xa, CONF_K, r0)
            _conv_weight_grad(dsw_ref, ds_ref[rs, :] * bg_ref[rs, :], xb, SHORT_K, r0)
            dcb_ref[...] += _colsum8(da1_c)

        _for_chunks(t, through_convs)
        copies = [pltpu.make_async_copy(
            stage.at[g], dp_ref.at[:, pl.ds(pl.multiple_of((g * nc + c) * LANE, LANE), LANE)], sems.at[g])
            for g in range(5)]
        for cp in copies:
            cp.start()
        for cp in copies:
            cp.wait()

    col = lambda g: pl.BlockSpec((t, LANE), lambda c, g=g: (0, g * nc + c))
    blk = pl.BlockSpec((t, LANE), lambda c: (0, c))
    return pl.pallas_call(
        body, name="conv_backward", grid=(nc,),
        in_specs=[ANY, col(0), col(1), col(2), col(3), col(4), blk, blk,
                  pl.BlockSpec((CONF_K, LANE), lambda c: (0, c)),
                  pl.BlockSpec((SHORT_K, LANE), lambda c: (0, c))],
        out_specs=[ANY,
                   pl.BlockSpec((CONF_K * SUB, LANE), lambda c: (0, c)),
                   pl.BlockSpec((SUB, LANE), lambda c: (0, c)),
                   pl.BlockSpec((SHORT_K * SUB, LANE), lambda c: (0, c))],
        out_shape=[jax.ShapeDtypeStruct(dproj.shape, dproj.dtype),
                   jax.ShapeDtypeStruct((CONF_K * SUB, dc), F32),
                   jax.ShapeDtypeStruct((SUB, dc), F32),
                   jax.ShapeDtypeStruct((SHORT_K * SUB, dc), F32)],
        scratch_shapes=[pltpu.VMEM((t + 2 * CONV_PAD, LANE), F32)] * 4
                       + [pltpu.VMEM((5, t, LANE), BF16), pltpu.SemaphoreType.DMA((5,))],
        input_output_aliases={0: 0},
        compiler_params=_params(1),
    )(dproj, proj, proj, proj, proj, proj, da1, ds, conf_w, short_w)


def _adamw_math(w, g, m, v):
    m = ADAM_B1 * m + (1.0 - ADAM_B1) * g
    v = ADAM_B2 * v + (1.0 - ADAM_B2) * (g * g)
    m_hat = m / (1.0 - ADAM_B1 ** ADAM_STEP)
    v_hat = v / (1.0 - ADAM_B2 ** ADAM_STEP)
    delta = -ADAM_LR * (m_hat / (jnp.sqrt(v_hat) + ADAM_EPS) + ADAM_WD * w)
    return delta, m, v


def _cast_into_slot(name, w, me_arr, deps=()):
    r, c = w.shape
    tr = 256

    def body(me_ref, w_ref, *rest):
        del me_ref
        rest[-1][0] = w_ref[...].astype(BF16)

    return pl.pallas_call(
        body, name=name,
        grid_spec=pltpu.PrefetchScalarGridSpec(
            num_scalar_prefetch=1, grid=(r // tr,),
            in_specs=[pl.BlockSpec((tr, c), lambda i, me: (i, 0))] + [ANY] * len(deps),
            out_specs=pl.BlockSpec((1, tr, c), lambda i, me: (me[0], i, 0))),
        out_shape=jax.ShapeDtypeStruct((N_DEV, r, c), BF16),
        compiler_params=_params(1),
    )(me_arr, w, *deps)


def _chip_sum(name, full, from_sibling, me_arr):
    _, r, c = full.shape
    tr = min(r, 512)

    def body(me_ref, full_ref, sib_ref, sums_ref):
        del me_ref
        sums_ref[0] = (full_ref[0].astype(F32) + sib_ref[0].astype(F32)).astype(BF16)

    other = lambda k, me: (me[0] // 2 + 1 + k) % 4
    return pl.pallas_call(
        body, name=name,
        grid_spec=pltpu.PrefetchScalarGridSpec(
            num_scalar_prefetch=1, grid=(r // tr, 3),
            in_specs=[pl.BlockSpec((1, tr, c), lambda i, k, me: (2 * other(k, me) + me[0] % 2, i, 0)),
                      pl.BlockSpec((1, tr, c), lambda i, k, me: (other(k, me), i, 0))],
            out_specs=pl.BlockSpec((1, tr, c), lambda i, k, me: (other(k, me), i, 0))),
        out_shape=jax.ShapeDtypeStruct((4, r, c), BF16),
        compiler_params=_params(2),
    )(me_arr, full, from_sibling)


def _adamw_shard(name, w, m, v, parts, me_arr, deps=()):
    r, c = w.shape
    tr = min(256, r // len(parts))
    np_ = len(parts)
    per = r // np_ // tr

    def body(me_ref, w_ref, m_ref, v_ref, *rest):
        g_out, d_out, m_out, v_out = rest[5 * np_ + len(deps):]
        g = None
        for p in range(np_):
            gp = rest[5 * p][...]
            for l_ref in rest[5 * p + 1:5 * p + 5]:
                gp = gp + l_ref[0].astype(F32)
            g = gp if g is None else jnp.where(pl.program_id(0) // per == p, gp, g)
        delta, m_new, v_new = _adamw_math(w_ref[...], g, m_ref[...], v_ref[...])
        g_out[...] = g
        d_out[...] = delta
        m_out[...] = m_new
        v_out[...] = v_new

    tile = pl.BlockSpec((tr, c), lambda i, me: (i, 0))
    part_specs, part_args = [], []
    for p, (g_own, from_sibling, landed) in enumerate(parts):
        row = lambda i, p=p: jnp.clip(i - p * per, 0, per - 1)
        part_specs.append(pl.BlockSpec((tr, c), lambda i, me, row=row: (row(i), 0)))
        part_specs += [pl.BlockSpec((1, tr, c), lambda i, me, k=k, row=row: ((me[0] // 2 + k) % 4, row(i), 0))
                       for k in range(4)]
        part_args += [g_own, from_sibling, landed, landed, landed]
    return pl.pallas_call(
        body, name=name,
        grid_spec=pltpu.PrefetchScalarGridSpec(
            num_scalar_prefetch=1, grid=(r // tr,),
            in_specs=[tile] * 3 + part_specs + [ANY] * len(deps), out_specs=[tile] * 4),
        out_shape=[jax.ShapeDtypeStruct((r, c), F32)] * 4,
        compiler_params=_params(1),
    )(me_arr, w, m, v, *part_args, *deps)


SMALL_W = 1024
VEC_ROWS = 16
LOSS_ROW = 15
META_ROW0 = 16
CONF_ROW0 = 64
SHORT_ROW0 = 96
SMALL_ROWS = 104


def _pack_small(vec_parts, dmeta, dcw, dsw, loss_blk, me_arr):
    widths = [p.shape[1] for p in vec_parts]
    nv = len(vec_parts)

    def body(me_ref, *refs):
        del me_ref
        parts, (dmeta_ref, dcw_ref, dsw_ref, loss_ref, out_ref) = refs[:nv], refs[nv:]
        out_ref[0] = jnp.zeros((SMALL_ROWS, SMALL_W), F32)
        out_ref[0, LOSS_ROW:LOSS_ROW + 1, 0:LANE] = loss_ref[0:1, :]
        row = 0
        for p_ref, wd in zip(parts, widths):
            s = jnp.sum(p_ref[...], axis=0, keepdims=True)
            for h in range(wd // SMALL_W):
                out_ref[0, row:row + 1, :] = s[:, h * SMALL_W:(h + 1) * SMALL_W]
                row += 1
        for h in range(dmeta_ref.shape[1] // SMALL_W):
            out_ref[0, META_ROW0 + h * N_META:META_ROW0 + (h + 1) * N_META, :] = dmeta_ref[:, h * SMALL_W:(h + 1) * SMALL_W]
        for k in range(CONF_K):
            out_ref[0, CONF_ROW0 + k:CONF_ROW0 + k + 1, :] = jnp.sum(dcw_ref[k * SUB:(k + 1) * SUB, :], axis=0, keepdims=True)
        for k in range(SHORT_K):
            out_ref[0, SHORT_ROW0 + k:SHORT_ROW0 + k + 1, :] = jnp.sum(dsw_ref[k * SUB:(k + 1) * SUB, :], axis=0, keepdims=True)

    ins = [*vec_parts, dmeta, dcw, dsw, loss_blk]
    return pl.pallas_call(
        body, name="pack_small",
        grid_spec=pltpu.PrefetchScalarGridSpec(
            num_scalar_prefetch=1, grid=(1,),
            in_specs=[pl.BlockSpec(a.shape, lambda i, me: (0, 0)) for a in ins],
            out_specs=pl.BlockSpec((1, SMALL_ROWS, SMALL_W), lambda i, me: (me[0], 0, 0))),
        out_shape=jax.ShapeDtypeStruct((N_DEV, SMALL_ROWS, SMALL_W), F32),
        compiler_params=_params(1),
    )(me_arr, *ins)


def _small_update(gathered, me_arr, vec_params, meta_p, conf_p, short_p):
    widths = [p[0].shape[1] for p in vec_params]
    nv = len(vec_params)
    mcols = meta_p[0].shape[1]
    per_row = SMALL_W // mcols

    def body(me_ref, gv_ref, gm_ref, gc_ref, gs_ref, *rest):
        del me_ref
        ins, outs = rest[:3 * (nv + 3)], rest[3 * (nv + 3):]

        def total(ref, r0, rows):
            s = ref[0, r0:r0 + rows, :]
            for dev in range(1, N_DEV):
                s = s + ref[dev, r0:r0 + rows, :]
            return s

        grads = []
        row = 0
        for wd in widths:
            pieces = [total(gv_ref, row + h, 1) for h in range(wd // SMALL_W)]
            grads.append(pieces[0] if len(pieces) == 1 else jnp.concatenate(pieces, axis=1))
            row += len(pieces)
        grads.append(total(gm_ref, 0, N_META))
        grads.append(total(gc_ref, 0, CONF_K))
        grads.append(total(gs_ref, 0, SHORT_K))
        loss = gv_ref[0, LOSS_ROW:LOSS_ROW + 1, 0:LANE]
        for dev in range(1, N_DEV):
            loss = loss + gv_ref[dev, LOSS_ROW:LOSS_ROW + 1, 0:LANE]
        outs[-1][...] = loss
        for idx, g in enumerate(grads):
            w_ref, m_ref, v_ref = ins[3 * idx:3 * idx + 3]
            delta, m_new, v_new = _adamw_math(w_ref[...], g, m_ref[...], v_ref[...])
            g_out, d_out, m_out, v_out = outs[4 * idx:4 * idx + 4]
            g_out[...] = g
            d_out[...] = delta
            m_out[...] = m_new
            v_out[...] = v_new

    params = list(vec_params) + [meta_p, conf_p, short_p]
    flat = [a for p in params for a in p]
    whole = lambda a: pl.BlockSpec(a.shape, lambda i, me: (0,) * a.ndim)
    outs = pl.pallas_call(
        body, name="small_update",
        grid_spec=pltpu.PrefetchScalarGridSpec(
            num_scalar_prefetch=1, grid=(1,),
            in_specs=[pl.BlockSpec((N_DEV, VEC_ROWS, SMALL_W), lambda i, me: (0, 0, 0)),
                      pl.BlockSpec((N_DEV, N_META, mcols),
                                   lambda i, me: (0, META_ROW0 // N_META + me[0] // per_row, me[0] % per_row)),
                      pl.BlockSpec((N_DEV, 32, LANE), lambda i, me: (0, CONF_ROW0 // 32, me[0])),
                      pl.BlockSpec((N_DEV, SUB, LANE), lambda i, me: (0, SHORT_ROW0 // SUB, me[0]))]
                     + [whole(a) for a in flat],
            out_specs=[whole(p[0]) for p in params for _ in range(4)]
                      + [pl.BlockSpec((1, LANE), lambda i, me: (0, 0))]),
        out_shape=[jax.ShapeDtypeStruct(p[0].shape, F32) for p in params for _ in range(4)]
                  + [jax.ShapeDtypeStruct((1, LANE), F32)],
        compiler_params=_params(1),
    )(me_arr, gathered, gathered, gathered, gathered, *flat)
    return [tuple(outs[4 * i:4 * i + 4]) for i in range(len(params))], outs[-1][0, 0]


def kernel(x, meta, g_pre_mix, w_in, b_gates, conf_dw_w, conf_dw_b, conf_ln_g, conf_ln_b, conf_w_pw, short_dw_w, short_w_out, w_o, g_post_mix, g_pre_mlp, w_up, w_down, g_post_mlp, loss_target, m_meta, m_g_pre_mix, m_w_in, m_b_gates, m_conf_dw_w, m_conf_dw_b, m_conf_ln_g, m_conf_ln_b, m_conf_w_pw, m_short_dw_w, m_short_w_out, m_w_o, m_g_post_mix, m_g_pre_mlp, m_w_up, m_w_down, m_g_post_mlp, v_meta, v_g_pre_mix, v_w_in, v_b_gates, v_conf_dw_w, v_conf_dw_b, v_conf_ln_g, v_conf_ln_b, v_conf_w_pw, v_short_dw_w, v_short_w_out, v_w_o, v_g_post_mix, v_g_pre_mlp, v_w_up, v_w_down, v_g_post_mlp):
    seq, d = x.shape[1], x.shape[2]
    dc = conf_w_pw.shape[1]
    t_real = N_META + seq
    t = -(-t_real // ROW_TILE) * ROW_TILE
    tm = t // 2
    assert tm % 16 == 0 and d % 1024 == 0 and dc % 1024 == 0
    x_idx, y_idx, c_idx = _position()
    me_arr = jnp.reshape(4 * x_idx + 2 * y_idx + c_idx, (1,)).astype(jnp.int32)

    big = [w_in[0], conf_w_pw[0], short_w_out[0], w_o[0], w_up[0], w_down[0]]
    big_names = ["w_in", "conf_w_pw", "short_w_out", "w_o", "w_up", "w_down"]
    groups = [[0], [1, 2, 3], [4], [5]]
    slots, deps = [], []
    for g, idxs in enumerate(groups):
        slots.append([_cast_into_slot("cast_" + big_names[i], big[i], me_arr, deps=deps) for i in idxs])
        if g == 0:
            direct0 = _remote_start("gather0_direct_start", "gather_direct", slots[0])
            deps = [direct0[3]]
    meta_g, cw_g, sw_g = _all_gather("gather_small_params", [meta, conf_dw_w[0], short_dw_w[0]], deps=deps)

    def start_direct(g, deps):
        send, recv, bufs, tok = _remote_start("gather%d_direct_start" % g, "gather_direct", slots[g], deps=deps)
        return (send, recv, bufs), tok

    def relay(g, state, after):
        send, recv, bufs = state
        bufs = _remote_wait("gather%d_direct_wait" % g, "gather_direct", send, recv, bufs, len(bufs), after)
        send, recv, bufs, tok = _remote_start("gather%d_relay_start" % g, "gather_relay", bufs)
        return (send, recv, bufs), tok

    def gathered(g, state, after):
        send, recv, bufs = state
        bufs = _remote_wait("gather%d_relay_wait" % g, "gather_relay", send, recv, bufs, len(bufs), after)
        send, recv, bufs, tok = _remote_start("gather%d_diag_start" % g, "gather_diag", bufs)
        return _remote_wait("gather%d_diag_wait" % g, "gather_diag", send, recv, bufs, len(bufs), [tok])

    unshard =lambda g: jnp.transpose(g, (1, 0, 2)).reshape(g.shape[1], -1)
    meta_full, cw_full, sw_full = unshard(meta_g), unshard(cw_g), unshard(sw_g)

    relay0, tok = relay(0, direct0[:3], [meta_g])
    zrows = jnp.zeros((t - t_real, d), F32) + tok[0, 0] * 0.0
    h0 = jnp.concatenate([meta_full, x[0], zrows], axis=0)
    tgt = jnp.concatenate([jnp.zeros((N_META, d), F32), loss_target[0], zrows], axis=0)
    n = _pre_norm(h0, g_pre_mix)
    direct1, tok = start_direct(1, [tok])
    direct2, tok = start_direct(2, [tok])
    win_g, = gathered(0, relay0, [tok, n])
    proj = _mm_cols("proj", n, win_g, tm=tm)[0]
    relay1, tok = relay(1, direct1, [proj])
    direct3, tok = start_direct(3, [tok])
    a1, s = _conv_forward(proj, cw_full, conf_dw_b, sw_full, dc, deps=[tok])
    a3 = _layer_norm_silu(a1, conf_ln_g, conf_ln_b)
    wpw_g, wso_g, wo_g = gathered(1, relay1, [a3])
    wo_full = wo_g.reshape(d, d)
    ya, yb, gate_a, gate_b, m_mix = _branch_merge(a3, s, wpw_g, wso_g, proj, b_gates, d)
    mix = _mm_rows("mix", m_mix, wo_full, tm=tm // 2, tn=d)
    relay2, tok = relay(2, direct2, [mix])
    relay3, tok = relay(3, direct3, [tok])
    h1, n2 = _post_mix(mix, h0, g_post_mix, g_pre_mlp, deps=[tok])
    wup_g, = gathered(2, relay2, [n2])

    def up_epilogue(acc):
        r = jnp.maximum(acc, 0.0)
        return r * r, r

    f, relu_up = _mm_cols("mlp_up", n2, wup_g, tm=tm, epilogue=up_epilogue, out_dtypes=(BF16, BF16))
    wdn_g, = gathered(3, relay3, [f])
    wdn_full = wdn_g.reshape(-1, d)
    fo = _mm_rows("mlp_down", f, wdn_full, tm=tm // 2, tn=512)
    dfo, dh2, dg_post_mlp, loss_blk = _loss_head(fo, h1, tgt, g_post_mlp, t_real)

    def reduce_start(tag, fulls, deps):
        lands = [lax.empty((4,) + g.shape[1:], BF16) for g in fulls]
        send, recv, bufs, tok = _remote_start("reduce_%s_d2d_start" % tag, "reduce_d2d", fulls, lands, deps=deps)
        return (send, recv, bufs), tok

    def reduce_middle(tag, state, owns, after):
        send, recv, bufs = state
        k = len(owns)
        bufs = _remote_wait("reduce_%s_d2d_wait" % tag, "reduce_d2d", send, recv, bufs, k, after)
        from_sibling = bufs[k:]
        sums = [_chip_sum("chip_sum_%s%d" % (tag, i), bufs[i], from_sibling[i], me_arr) for i in range(k)]
        lands = [lax.empty(sm.shape, BF16) for sm in sums]
        send, recv, bufs, tok = _remote_start("reduce_%s_ici_start" % tag, "reduce_ici", sums, lands)
        return (send, recv, bufs, list(zip(owns, from_sibling))), tok

    def reduce_finish(tag, state, after):
        send, recv, bufs, local = state
        k = len(local)
        bufs = _remote_wait("reduce_%s_ici_wait" % tag, "reduce_ici", send, recv, bufs, k, after)
        return [(own, sib, landed) for (own, sib), landed in zip(local, bufs[k:])]

    dup = _mm_nt_blocks("d_up", dfo, wdn_full, tm=tm, tkb=1024, extra=(relu_up,),
                        epilogue=lambda acc, r: (acc * (2.0 * r.astype(F32)),), out_dtypes=(BF16,))[0]
    gw_down, gw_down_own = _mm_tn("dw_down", f, dfo, me_arr, m=f.shape[1], n=d, tma=512, tn=d, sharded="rows")
    red_down, tok = reduce_start("down", [gw_down], ())
    dn2 = _mm_nt_acc("d_n2", dup, wup_g, tm=tm // 2, tn=512, deps=[tok])
    gw_up, gw_up_own = _mm_tn("dw_up", n2, dup, me_arr, m=d, n=dup.shape[1], tma=512, tn=1024, sharded="cols")
    red_down, tok = reduce_middle("down", red_down, [gw_down_own], [dn2])
    red_up, tok = reduce_start("up", [gw_up], [tok])
    dh1, dmix, dg_pre_mlp, dg_post_mix = _mid_norm_bwd(dn2, h1, dh2, mix, g_pre_mlp, g_post_mix, deps=[tok])
    dya, dyb, dproj, db_a, db_b = _gate_backward(dmix, wo_full, gate_a, gate_b, ya, yb, proj.shape[1], tm // 2)
    db_gates = jnp.concatenate([db_a, db_b], axis=1)
    red_up, tok = reduce_middle("up", red_up, [gw_up_own], [dya])
    gw_o, gw_o_own = _mm_tn("dw_o", m_mix, dmix, me_arr, m=d, n=d, tma=d // N_DEV, tn=d, sharded="rows", deps=[tok])
    da3 = _mm_nt_acc("d_a3", dya, wpw_g, tm=tm, tn=512)
    gw_pw, gw_pw_own = _mm_tn("dw_pw", a3, dya, me_arr, m=dc, n=d, tma=512, tn=d, sharded="cols")
    dsb = _mm_nt_acc("d_s", dyb, wso_g, tm=tm, tn=512)
    gw_so, gw_so_own = _mm_tn("dw_so", s, dyb, me_arr, m=dc, n=d, tma=512, tn=d, sharded="cols")
    red_mix, tok = reduce_start("mix", [gw_pw, gw_so, gw_o], ())
    da1, dln_g, dln_b = _layer_norm_silu_bwd(da3, a1, conf_ln_g, conf_ln_b, deps=[tok])
    dproj, dcw, dcb, dsw = _conv_backward(dproj, proj, da1, dsb, cw_full, sw_full, dc)
    red_mix, tok = reduce_middle("mix", red_mix, [gw_pw_own, gw_so_own, gw_o_own], [dcb])
    in_cb = w_in.shape[2]
    half = d // 2
    red_in = []
    for part in range(2):
        gw, own = _mm_tn("dw_in%d" % part, n, dproj, me_arr, m=half, n=proj.shape[1], tma=512, tn=2 * in_cb,
                         sharded="cols", a_off=part * (half // 512), deps=[tok])
        state, tok = reduce_start("in%d" % part, [gw], ())
        red_in.append((state, own))
    for part in range(2):
        state, own = red_in[part]
        red_in[part], tok = reduce_middle("in%d" % part, state, [own], [tok])
    dn = _mm_nt_acc("d_n", dproj, win_g, tm=tm // 2, tn=512, deps=[tok])
    dh0, dg_pre_mix = _pre_norm_bwd(dn, h0, dh1, g_pre_mix)
    grad_x = dh0[N_META:t_real][None]

    vec_parts = [dg_pre_mix, db_gates, dcb, dln_g, dln_b, dg_post_mix, dg_pre_mlp, dg_post_mlp]
    packed = _pack_small(vec_parts, dh0[:N_META], dcw, dsw, loss_blk, me_arr)
    send, recv, bufs, tok = _remote_start("small_grads_ici_start", "gather_ici", [packed])
    vec_names = ["g_pre_mix", "b_gates", "conf_dw_b", "conf_ln_g", "conf_ln_b", "g_post_mix", "g_pre_mlp", "g_post_mlp"]
    env = locals()
    results = {}

    def update(nm, parts, deps=()):
        res = _adamw_shard("adamw_" + nm, env[nm][0], env["m_" + nm][0], env["v_" + nm][0], parts, me_arr, deps=deps)
        results[nm] = tuple(r[None] for r in res)
        return res[0]

    done = [update("w_down", reduce_finish("down", red_down, [tok]), deps=[tok])]
    done.append(update("w_up", reduce_finish("up", red_up, done)))
    bufs = _remote_wait("small_grads_ici_wait", "gather_ici", send, recv, bufs, 1, done)
    send, recv, bufs, tok = _remote_start("small_grads_d2d_start", "gather_d2d", bufs)
    for nm, pair in zip(["conf_w_pw", "short_w_out", "w_o"], reduce_finish("mix", red_mix, [tok])):
        done.append(update(nm, [pair], deps=[tok]))
    small_g, = _remote_wait("small_grads_d2d_wait", "gather_d2d", send, recv, bufs, 1, done)
    triple = lambda nm, sq: tuple(env[p + nm][0] if sq else env[p + nm] for p in ("", "m_", "v_"))
    small, loss = _small_update(small_g, me_arr, [triple(nm, False) for nm in vec_names],
                                triple("meta", False), triple("conf_dw_w", True), triple("short_dw_w", True))
    for nm, res in zip(vec_names + ["meta"], small[:len(vec_names) + 1]):
        results[nm] = res
    results["conf_dw_w"] = tuple(r[None] for r in small[-2])
    results["short_dw_w"] = tuple(r[None] for r in small[-1])
    update("w_in", [reduce_finish("in%d" % part, red_in[part], [small[0][0]])[0] for part in range(2)])

    order = ["meta", "g_pre_mix", "w_in", "b_gates", "conf_dw_w", "conf_dw_b", "conf_ln_g", "conf_ln_b", "conf_w_pw",
             "short_dw_w", "short_w_out", "w_o", "g_post_mix", "g_pre_mlp", "w_up", "w_down", "g_post_mlp"]
    return (loss, grad_x, *[results[nm][0] for nm in order], *[results[nm][1] for nm in order],
            *[results[nm][2] for nm in order], *[results[nm][3] for nm in order])
```

```python
import jax
import jax.numpy as jnp
from jax import lax
from jax.experimental import pallas as pl
from jax.experimental.pallas import tpu as pltpu

N_DEV = 8
N_META = 16
CONF_K = 31
SHORT_K = 3
RMS_EPS = 1e-6
LN_EPS = 1e-5
ADAM_LR = 0.001
ADAM_B1 = 0.9
ADAM_B2 = 0.999
ADAM_EPS = 1e-08
ADAM_WD = 0.01
ADAM_STEP = 10

LANE = 128
SUB = 8
ROW_TILE = 128
CONV_PAD = 32
CONV_CHUNK = 128
VMEM_LIMIT = 56 * 1024 * 1024

F32 = jnp.float32
BF16 = jnp.bfloat16
MESH = pl.DeviceIdType.MESH
ANY = pl.BlockSpec(memory_space=pl.ANY)
HBM_SPEC = pl.BlockSpec(memory_space=pltpu.HBM)
SEM_SPEC = pl.BlockSpec(memory_space=pltpu.SEMAPHORE)
EFFECT = pltpu.SideEffectType.DATAFLOW_SIDE_EFFECTING


def _params(n_axes):
    return pltpu.CompilerParams(dimension_semantics=("arbitrary",) * n_axes, vmem_limit_bytes=VMEM_LIMIT)


def _sigmoid(z):
    return 1.0 / (1.0 + jnp.exp(-z))


def _colsum8(v):
    r, c = v.shape
    return jnp.sum(v.reshape(r // SUB, SUB, c), axis=0)


def _position():
    x, y, c = lax.axis_index("x"), lax.axis_index("y"), lax.axis_index("c")
    return x, y, c


def _flat(p):
    return 4 * p[0] + 2 * p[1] + p[2]


def _all_gather(name, shards, deps=()):
    n, nd = len(shards), len(deps)

    def body(*refs):
        ins, outs = refs[:n], refs[n + nd:2 * n + nd]
        send_sems, recv_sems, local_sems = refs[2 * n + nd:]
        x, y, c = _position()
        me, sibling = (x, y, c), (x, y, 1 - c)
        chips = [(1 - x, y), (x, 1 - y), (1 - x, 1 - y)]

        def copy(q, k, block, to, src=None):
            dst = outs[q].at[_flat(block)]
            return pltpu.make_async_remote_copy(
                src_ref=dst if src is None else src, dst_ref=dst,
                send_sem=send_sems.at[q, k], recv_sem=recv_sems.at[q, k],
                device_id=to, device_id_type=MESH)

        mine = [pltpu.make_async_copy(ins[q], outs[q].at[_flat(me)], local_sems.at[q]) for q in range(n)]
        for cp in mine:
            cp.start()
        first = []
        for q in range(n):
            first.append(copy(q, 0, me, sibling, src=ins[q]))
            for j, chip in enumerate(chips):
                first.append(copy(q, 1 + j, me, (*chip, c), src=ins[q]))
        for cp in first:
            cp.start()
        passed = []
        for q in range(n):
            for j, chip in enumerate(chips):
                copy(q, 1 + j, (*chip, c), me).wait_recv()
                fwd = copy(q, 4 + j, (*chip, c), sibling)
                fwd.start()
                passed.append(fwd)
        for q in range(n):
            copy(q, 0, sibling, me).wait_recv()
            for j, chip in enumerate(chips):
                copy(q, 4 + j, (*chip, 1 - c), me).wait_recv()
        for cp in first + passed:
            cp.wait_send()
        for cp in mine:
            cp.wait()

    return pl.pallas_call(
        body, name=name,
        in_specs=[ANY] * (n + nd), out_specs=[ANY] * n,
        out_shape=[jax.ShapeDtypeStruct((N_DEV,) + s.shape, s.dtype) for s in shards],
        scratch_shapes=[pltpu.SemaphoreType.DMA((n, 7)), pltpu.SemaphoreType.DMA((n, 7)),
                        pltpu.SemaphoreType.DMA((n,))],
    )(*shards, *deps)


N_COPIES = {"gather_ici": 4, "gather_d2d": 3, "gather_direct": 3, "gather_relay": 3, "gather_diag": 1,
            "reduce_d2d": 4, "reduce_ici": 3}


def _copy_plan(kind):
    x, y, c = _position()
    me, sibling = (x, y, c), (x, y, 1 - c)
    chips = [(1 - x, y), (x, 1 - y), (1 - x, 1 - y)]
    if kind == "gather_ici":
        return [(_flat(me), _flat(me), sibling)] + [(_flat(me), _flat(me), (*ch, c)) for ch in chips]
    if kind == "gather_d2d":
        return [(_flat((*ch, c)), _flat((*ch, c)), sibling) for ch in chips]
    if kind == "gather_direct":
        return [(_flat(me), _flat(me), sibling)] + [(_flat(me), _flat(me), (*ch, c)) for ch in chips[:2]]
    if kind == "gather_relay":
        held, to = (x ^ (1 - c), y ^ c, c), (x ^ c, y ^ (1 - c), c)
        return [(_flat(held), _flat(held), to)] + [(_flat((*ch, c)), _flat((*ch, c)), sibling) for ch in chips[:2]]
    if kind == "gather_diag":
        return [(_flat((*chips[2], c)), _flat((*chips[2], c)), sibling)]
    if kind == "reduce_d2d":
        return [(2 * chip + (1 - c), chip, sibling) for chip in range(4)]
    return [(2 * ch[0] + ch[1], 2 * x + y, (*ch, c)) for ch in chips]


def _planned_copies(kind, srcs, dsts, send_sems, recv_sems):
    plan = _copy_plan(kind)
    return [pltpu.make_async_remote_copy(
        src_ref=src.at[s_slot], dst_ref=dst.at[d_slot],
        send_sem=send_sems.at[q * len(plan) + k], recv_sem=recv_sems.at[q * len(plan) + k],
        device_id=to, device_id_type=MESH)
        for q, (src, dst) in enumerate(zip(srcs, dsts)) for k, (s_slot, d_slot, to) in enumerate(plan)]


def _remote_start(name, kind, srcs, lands=None, deps=()):
    n = len(srcs)
    bufs = list(srcs) + ([] if lands is None else list(lands))
    nb, nd = len(bufs), len(deps)
    nsem = n * N_COPIES[kind]

    def body(*refs):
        ins = refs[:nb]
        send_sems, recv_sems = refs[nb + nd], refs[nb + nd + 1]
        token = refs[-1]
        for cp in _planned_copies(kind, ins[:n], ins[:n] if lands is None else ins[n:], send_sems, recv_sems):
            cp.start()
        token[...] = jnp.zeros_like(token)

    outs = pl.pallas_call(
        body, name=name,
        out_shape=(pltpu.SemaphoreType.DMA((nsem,)), pltpu.SemaphoreType.DMA((nsem,)),
                   *[pltpu.HBM(b.shape, b.dtype) for b in bufs], jax.ShapeDtypeStruct((SUB, LANE), F32)),
        in_specs=[HBM_SPEC] * nb + [ANY] * nd,
        out_specs=(SEM_SPEC, SEM_SPEC, *[HBM_SPEC] * nb, pl.BlockSpec(memory_space=pltpu.VMEM)),
        input_output_aliases={i: 2 + i for i in range(nb)},
        compiler_params=pltpu.CompilerParams(has_side_effects=EFFECT),
    )(*[pltpu.with_memory_space_constraint(b, pltpu.HBM) for b in bufs], *deps)
    return outs[0], outs[1], list(outs[2:2 + nb]), outs[-1]


def _remote_wait(name, kind, send_sems, recv_sems, bufs, n, after):
    nb, na = len(bufs), len(after)
    same = nb == n

    def body(*refs):
        ins = refs[:nb]
        sends, recvs = refs[nb], refs[nb + 1]
        for cp in _planned_copies(kind, ins[:n], ins[:n] if same else ins[n:], sends, recvs):
            cp.wait_send()
            cp.wait_recv()

    outs = pl.pallas_call(
        body, name=name,
        out_shape=[pltpu.HBM(b.shape, b.dtype) for b in bufs],
        in_specs=[HBM_SPEC] * nb + [SEM_SPEC, SEM_SPEC] + [ANY] * na,
        out_specs=[HBM_SPEC] * nb,
        input_output_aliases={i: i for i in range(nb)},
        compiler_params=pltpu.CompilerParams(has_side_effects=EFFECT),
    )(*bufs, send_sems, recv_sems, *after)
    return list(outs)


def _mm_cols(name, a, w, *, tm, nb=1, epilogue=None, out_dtypes=(F32,)):
    t, k = a.shape
    nblk, _, cb = w.shape

    def body(a_ref, w_ref, *o_refs):
        av = a_ref[...]
        for b in range(nb):
            acc = jnp.dot(av, w_ref[b], preferred_element_type=F32)
            outs = (acc,) if epilogue is None else epilogue(acc)
            for o_ref, o in zip(o_refs, outs):
                o_ref[:, b * cb:(b + 1) * cb] = o.astype(o_ref.dtype)

    return pl.pallas_call(
        body, name=name, grid=(nblk // nb, t // tm),
        in_specs=[pl.BlockSpec((tm, k), lambda j, i: (i, 0)),
                  pl.BlockSpec((nb, k, cb), lambda j, i: (j, 0, 0))],
        out_specs=[pl.BlockSpec((tm, nb * cb), lambda j, i: (i, j)) for _ in out_dtypes],
        out_shape=[jax.ShapeDtypeStruct((t, nblk * cb), dt) for dt in out_dtypes],
        compiler_params=_params(2),
    )(a, w)


def _mm_rows(name, a, w2d, *, tm, tn):
    t, kf = a.shape
    n = w2d.shape[1]

    def body(a_ref, w_ref, o_ref):
        o_ref[...] = jnp.dot(a_ref[...], w_ref[...], preferred_element_type=F32)

    return pl.pallas_call(
        body, name=name, grid=(t // tm, n // tn),
        in_specs=[pl.BlockSpec((tm, kf), lambda i, j: (i, 0)),
                  pl.BlockSpec((kf, tn), lambda i, j: (0, j))],
        out_specs=pl.BlockSpec((tm, tn), lambda i, j: (i, j)),
        out_shape=jax.ShapeDtypeStruct((t, n), F32),
        compiler_params=_params(2),
    )(a, w2d)


def _mm_nt_acc(name, dy, w, *, tm, tn, col_off=0, deps=()):
    t = dy.shape[0]
    nblk, k, cb = w.shape

    def body(dy_ref, w_ref, *rest):
        acc = None
        for b in range(nblk):
            d = lax.dot_general(dy_ref[:, b * cb:(b + 1) * cb], w_ref[b], (((1,), (1,)), ((), ())),
                                preferred_element_type=F32)
            acc = d if acc is None else acc + d
        rest[-1][...] = acc

    return pl.pallas_call(
        body, name=name, grid=(t // tm, k // tn),
        in_specs=[pl.BlockSpec((tm, nblk * cb), lambda i, j: (i, col_off)),
                  pl.BlockSpec((nblk, tn, cb), lambda i, j: (0, j, 0))] + [ANY] * len(deps),
        out_specs=pl.BlockSpec((tm, tn), lambda i, j: (i, j)),
        out_shape=jax.ShapeDtypeStruct((t, k), F32),
        compiler_params=_params(2),
    )(dy, w, *deps)


def _mm_nt_blocks(name, dy, w2d, *, tm, tkb, extra=(), epilogue=None, out_dtypes=(F32,)):
    t, n = dy.shape
    kf = w2d.shape[0]
    ne = len(extra)

    def body(dy_ref, w_ref, *rest):
        acc = lax.dot_general(dy_ref[...], w_ref[...], (((1,), (1,)), ((), ())), preferred_element_type=F32)
        outs = (acc,) if epilogue is None else epilogue(acc, *[e[...] for e in rest[:ne]])
        for o_ref, o in zip(rest[ne:], outs):
            o_ref[...] = o.astype(o_ref.dtype)

    return pl.pallas_call(
        body, name=name, grid=(kf // tkb, t // tm),
        in_specs=[pl.BlockSpec((tm, n), lambda kb, i: (i, 0)),
                  pl.BlockSpec((tkb, n), lambda kb, i: (kb, 0))]
                 + [pl.BlockSpec((tm, tkb), lambda kb, i: (i, kb)) for _ in extra],
        out_specs=[pl.BlockSpec((tm, tkb), lambda kb, i: (i, kb)) for _ in out_dtypes],
        out_shape=[jax.ShapeDtypeStruct((t, kf), dt) for dt in out_dtypes],
        compiler_params=_params(2),
    )(dy, w2d, *extra)


def _mm_tn(name, a, b, me_arr, *, m, n, tma, tn, sharded, a_off=0, b_off=0, deps=()):
    t = a.shape[0]
    if sharded == "cols":
        cb = n // N_DEV
        nb, q = max(tn // cb, 1), max(cb // tn, 1)
        tw = tn // nb
        full_shape, own_shape = (N_DEV, m, cb), (m, cb)
        full_spec = pl.BlockSpec((nb, tma, tw), lambda i, j, me: (j // q, i, j % q))
    else:
        kb = m // N_DEV
        p = kb // tma
        nb, tw = 1, tn
        full_shape, own_shape = (m, n), (kb, n)
        full_spec = pl.BlockSpec((tma, tn), lambda i, j, me: (i, j))

    def body(me_ref, a_ref, b_ref, *rest):
        full_ref, own_ref, stage, sem = rest[len(deps):]
        i, j = pl.program_id(0), pl.program_id(1)
        acc = lax.dot_general(a_ref[...], b_ref[...], (((0,), (0,)), ((), ())), preferred_element_type=F32)
        for blk in range(nb):
            part = acc[:, blk * tw:(blk + 1) * tw]
            if sharded == "cols":
                full_ref[blk] = part.astype(BF16)
                owner, r0, c0 = (j // q) * nb + blk, i * tma, (j % q) * tw
            else:
                full_ref[...] = part.astype(BF16)
                owner, r0, c0 = i // p, (i % p) * tma, j * tn

            @pl.when(owner == me_ref[0])
            def _():
                stage[...] = part
                cp = pltpu.make_async_copy(
                    stage, own_ref.at[pl.ds(pl.multiple_of(r0, tma), tma), pl.ds(pl.multiple_of(c0, tw), tw)], sem)
                cp.start()
                cp.wait()

    full, own = pl.pallas_call(
        body, name=name,
        grid_spec=pltpu.PrefetchScalarGridSpec(
            num_scalar_prefetch=1, grid=(m // tma, n // tn),
            in_specs=[pl.BlockSpec((t, tma), lambda i, j, me: (0, a_off + i)),
                      pl.BlockSpec((t, tn), lambda i, j, me: (0, b_off + j))] + [ANY] * len(deps),
            out_specs=[full_spec, ANY],
            scratch_shapes=[pltpu.VMEM((tma, tw), F32), pltpu.SemaphoreType.DMA(())]),
        out_shape=[jax.ShapeDtypeStruct(full_shape, BF16), jax.ShapeDtypeStruct(own_shape, F32)],
        compiler_params=_params(2),
    )(me_arr, a, b, *deps)
    if sharded == "rows":
        full = full.reshape(N_DEV, m // N_DEV, n)
    return full, own


def _row_tile(t):
    return t // 8 if (t // 8) % 16 == 0 else ROW_TILE


def _row_call(name, body, t, row_ins, full_ins, row_outs, acc_outs, scratch=(), deps=()):
    tm = _row_tile(t)
    nin = len(row_ins) + len(full_ins)

    def without_deps(*refs):
        body(*refs[:nin], *refs[nin + len(deps):])

    return pl.pallas_call(
        without_deps, name=name, grid=(t // tm,),
        in_specs=[pl.BlockSpec((tm, a.shape[1]), lambda i: (i, 0)) for a in row_ins]
                 + [pl.BlockSpec(a.shape, lambda i: (0, 0)) for a in full_ins] + [ANY] * len(deps),
        out_specs=[pl.BlockSpec((tm, c), lambda i: (i, 0)) for c, _ in row_outs]
                  + [pl.BlockSpec((r, c), lambda i: (0, 0)) for r, c in acc_outs],
        out_shape=[jax.ShapeDtypeStruct((t, c), dt) for c, dt in row_outs]
                  + [jax.ShapeDtypeStruct((r, c), F32) for r, c in acc_outs],
        scratch_shapes=list(scratch),
        compiler_params=_params(1),
    )(*row_ins, *full_ins, *deps)


def _accumulate(ref, v):
    @pl.when(pl.program_id(0) == 0)
    def _():
        ref[...] = v

    @pl.when(pl.program_id(0) > 0)
    def _():
        ref[...] += v


def _rms(v):
    return lax.rsqrt(jnp.mean(v * v, axis=-1, keepdims=True) + RMS_EPS)


def _rms_bwd(dout, u, r, g):
    du = dout * g
    dx = r * (du - u * jnp.mean(du * u, axis=-1, keepdims=True))
    return dx, _colsum8(dout * u)


def _pre_norm(h0, g):
    t, d = h0.shape

    def body(h_ref, g_ref, n_ref):
        h = h_ref[...]
        n_ref[...] = (h * _rms(h) * g_ref[...]).astype(BF16)

    return _row_call("pre_norm", body, t, [h0], [g], [(d, BF16)], [])[0]


def _post_mix(mix, h0, g_post, g_pre, deps=()):
    t, d = h0.shape

    def body(mix_ref, h0_ref, gp_ref, gq_ref, h1_ref, n2_ref):
        mix_v = mix_ref[...]
        h1 = h0_ref[...] + mix_v * _rms(mix_v) * gp_ref[...]
        h1_ref[...] = h1
        n2_ref[...] = (h1 * _rms(h1) * gq_ref[...]).astype(BF16)

    return _row_call("post_mix", body, t, [mix, h0], [g_post, g_pre], [(d, F32), (d, BF16)], [], deps=deps)


def _loss_head(fo, h1, tgt, g_post_mlp, t_real):
    t, d = h1.shape
    tile = _row_tile(t)

    def body(fo_ref, h1_ref, tgt_ref, g_ref, dfo_ref, dh2_ref, dg_ref, loss_ref, lacc):
        i = pl.program_id(0)
        fo_v = fo_ref[...]
        g = g_ref[...]
        r = _rms(fo_v)
        u = fo_v * r
        h2 = h1_ref[...] + u * g
        row = i * tile + lax.broadcasted_iota(jnp.int32, (tile, 1), 0)
        valid = jnp.logical_and(row >= N_META, row < t_real)
        diff = jnp.where(valid, h2 - tgt_ref[...], 0.0)
        dh2 = diff * (1.0 / d)
        dh2_ref[...] = dh2
        dfo, dg = _rms_bwd(dh2, u, r, g)
        dfo_ref[...] = dfo.astype(BF16)
        _accumulate(dg_ref, dg)
        _accumulate(lacc, _colsum8(diff * diff))

        @pl.when(i == pl.num_programs(0) - 1)
        def _():
            loss_ref[...] = jnp.full((SUB, LANE), (0.5 / d) * jnp.sum(lacc[...]), F32)

    return _row_call("loss_head", body, t, [fo, h1, tgt], [g_post_mlp],
                     [(d, BF16), (d, F32)], [(SUB, d), (SUB, LANE)], scratch=[pltpu.VMEM((SUB, d), F32)])


def _mid_norm_bwd(dn2, h1, dh2, mix, g_pre_mlp, g_post_mix, deps=()):
    t, d = h1.shape

    def body(dn2_ref, h1_ref, dh2_ref, mix_ref, gq_ref, gp_ref, dh1_ref, dmix_ref, dgq_ref, dgp_ref):
        h1 = h1_ref[...]
        r3 = _rms(h1)
        dx, dgq = _rms_bwd(dn2_ref[...], h1 * r3, r3, gq_ref[...])
        dh1 = dh2_ref[...] + dx
        dh1_ref[...] = dh1
        mix_v = mix_ref[...]
        r2 = _rms(mix_v)
        dmix, dgp = _rms_bwd(dh1, mix_v * r2, r2, gp_ref[...])
        dmix_ref[...] = dmix.astype(BF16)
        _accumulate(dgq_ref, dgq)
        _accumulate(dgp_ref, dgp)

    return _row_call("mid_norm_bwd", body, t, [dn2, h1, dh2, mix], [g_pre_mlp, g_post_mix],
                     [(d, F32), (d, BF16)], [(SUB, d), (SUB, d)], deps=deps)


def _pre_norm_bwd(dn, h0, dh1, g_pre_mix, deps=()):
    t, d = h0.shape

    def body(dn_ref, h0_ref, dh1_ref, g_ref, dh0_ref, dg_ref):
        h0 = h0_ref[...]
        r = _rms(h0)
        dx, dg = _rms_bwd(dn_ref[...], h0 * r, r, g_ref[...])
        dh0_ref[...] = dh1_ref[...] + dx
        _accumulate(dg_ref, dg)

    return _row_call("pre_norm_bwd", body, t, [dn, h0, dh1], [g_pre_mix], [(d, F32)], [(SUB, d)], deps=deps)


def _layer_norm_silu(a1, ln_g, ln_b):
    t, c = a1.shape

    def body(a1_ref, g_ref, b_ref, a3_ref):
        a = a1_ref[...]
        mu = jnp.mean(a, axis=-1, keepdims=True)
        xc = a - mu
        rstd = lax.rsqrt(jnp.mean(xc * xc, axis=-1, keepdims=True) + LN_EPS)
        z = xc * rstd * g_ref[...] + b_ref[...]
        a3_ref[...] = (z * _sigmoid(z)).astype(BF16)

    return _row_call("layer_norm_silu", body, t, [a1], [ln_g, ln_b], [(c, BF16)], [])[0]


def _layer_norm_silu_bwd(da3, a1, ln_g, ln_b, deps=()):
    t, c = a1.shape

    def body(da3_ref, a1_ref, g_ref, b_ref, da1_ref, dg_ref, db_ref):
        a = a1_ref[...]
        g = g_ref[...]
        mu = jnp.mean(a, axis=-1, keepdims=True)
        xc = a - mu
        rstd = lax.rsqrt(jnp.mean(xc * xc, axis=-1, keepdims=True) + LN_EPS)
        xhat = xc * rstd
        z = xhat * g + b_ref[...]
        sg = _sigmoid(z)
        dz = da3_ref[...] * (sg * (1.0 + z * (1.0 - sg)))
        dxhat = dz * g
        da1_ref[...] = rstd * (dxhat - jnp.mean(dxhat, axis=-1, keepdims=True)
                               - xhat * jnp.mean(dxhat * xhat, axis=-1, keepdims=True))
        _accumulate(dg_ref, _colsum8(dz * xhat))
        _accumulate(db_ref, _colsum8(dz))

    return _row_call("layer_norm_silu_bwd", body, t, [da3, a1], [ln_g, ln_b], [(c, F32)], [(SUB, c), (SUB, c)], deps=deps)


def _branch_merge(a3, s, wpw, wso, proj, b_gates, d, deps=()):
    t, cols = proj.shape
    nblk, k, cb = wpw.shape
    w = 1024
    nh = d // w
    per = w // cb
    ga0 = (cols - 2 * d) // w
    tm = _row_tile(t)

    def body(a3_ref, s_ref, wpw_ref, wso_ref, *rest):
        pa_refs, pb_refs, bg_ref = rest[:nh], rest[nh:2 * nh], rest[2 * nh]
        ya_ref, yb_ref, ga_ref, gb_ref, m_ref = rest[2 * nh + 1 + len(deps):]
        a3v, sv = a3_ref[...], s_ref[...]
        for b in range(nblk):
            here = slice(b * cb, (b + 1) * cb)
            local = slice((b % per) * cb, (b % per + 1) * cb)
            ya = jnp.dot(a3v, wpw_ref[b], preferred_element_type=F32)
            yb = jnp.dot(sv, wso_ref[b], preferred_element_type=F32)
            ga = _sigmoid(pa_refs[b // per][:, local] + bg_ref[:, here])
            gb = _sigmoid(pb_refs[b // per][:, local] + bg_ref[:, d + b * cb:d + (b + 1) * cb])
            ya_ref[:, here] = ya.astype(BF16)
            yb_ref[:, here] = yb.astype(BF16)
            ga_ref[:, here] = ga.astype(BF16)
            gb_ref[:, here] = gb.astype(BF16)
            m_ref[:, here] = (ga * ya + gb * yb).astype(BF16)

    tile = pl.BlockSpec((tm, d), lambda i: (i, 0))
    return pl.pallas_call(
        body, name="branch_merge", grid=(t // tm,),
        in_specs=[pl.BlockSpec((tm, k), lambda i: (i, 0)), pl.BlockSpec((tm, k), lambda i: (i, 0)),
                  pl.BlockSpec((nblk, k, cb), lambda i: (0, 0, 0)), pl.BlockSpec((nblk, k, cb), lambda i: (0, 0, 0))]
                 + [pl.BlockSpec((tm, w), lambda i, h=h: (i, ga0 + h)) for h in range(2 * nh)]
                 + [pl.BlockSpec((1, 2 * d), lambda i: (0, 0))] + [ANY] * len(deps),
        out_specs=[tile] * 5,
        out_shape=[jax.ShapeDtypeStruct((t, d), BF16)] * 5,
        compiler_params=_params(1),
    )(a3, s, wpw, wso, *([proj] * (2 * nh)), b_gates, *deps)


def _gate_backward(dmix, wo_full, ga, gb, ya, yb, cols, tm, deps=()):
    t, d = ya.shape
    w = 1024
    nh = d // w
    ga0 = (cols - 2 * d) // w

    def body(dmix_ref, wo_ref, ga_ref, gb_ref, ya_ref, yb_ref, *rest):
        dya_ref, dyb_ref, dp_ref, dba_ref, dbb_ref, stage, sems = rest[len(deps):]
        h, i = pl.program_id(0), pl.program_id(1)
        dm = lax.dot_general(dmix_ref[...], wo_ref[...], (((1,), (1,)), ((), ())), preferred_element_type=F32)
        ga = ga_ref[...].astype(F32)
        gb = gb_ref[...].astype(F32)
        dya_ref[...] = (dm * ga).astype(BF16)
        dyb_ref[...] = (dm * gb).astype(BF16)
        dpa = dm * ya_ref[...].astype(F32) * ga * (1.0 - ga)
        dpb = dm * yb_ref[...].astype(F32) * gb * (1.0 - gb)
        stage[0] = dpa.astype(BF16)
        stage[1] = dpb.astype(BF16)
        rows = pl.ds(pl.multiple_of(i * tm, tm), tm)
        copies = [pltpu.make_async_copy(
            stage.at[g], dp_ref.at[rows, pl.ds(pl.multiple_of((ga0 + g * nh + h) * w, w), w)], sems.at[g])
            for g in range(2)]
        for cp in copies:
            cp.start()

        @pl.when(i == 0)
        def _():
            dba_ref[...] = _colsum8(dpa)
            dbb_ref[...] = _colsum8(dpb)

        @pl.when(i > 0)
        def _():
            dba_ref[...] += _colsum8(dpa)
            dbb_ref[...] += _colsum8(dpb)

        for cp in copies:
            cp.wait()

    tile = pl.BlockSpec((tm, w), lambda h, i: (i, h))
    return pl.pallas_call(
        body, name="gate_backward", grid=(nh, t // tm),
        in_specs=[pl.BlockSpec((tm, d), lambda h, i: (i, 0)),
                  pl.BlockSpec((w, d), lambda h, i: (h, 0)),
                  tile, tile, tile, tile] + [ANY] * len(deps),
        out_specs=[tile, tile, ANY,
                   pl.BlockSpec((SUB, w), lambda h, i: (0, h)),
                   pl.BlockSpec((SUB, w), lambda h, i: (0, h))],
        out_shape=[jax.ShapeDtypeStruct((t, d), BF16), jax.ShapeDtypeStruct((t, d), BF16),
                   jax.ShapeDtypeStruct((t, cols), BF16),
                   jax.ShapeDtypeStruct((SUB, d), F32), jax.ShapeDtypeStruct((SUB, d), F32)],
        scratch_shapes=[pltpu.VMEM((2, tm, w), BF16), pltpu.SemaphoreType.DMA((2,))],
        compiler_params=_params(2),
    )(dmix, wo_full, ga, gb, ya, yb, *deps)


def _causal_conv(xp_ref, w_ref, ntap, r0):
    n = CONV_CHUNK + CONV_PAD
    win = xp_ref[pl.ds(r0, n), :]
    acc = None
    for k in range(ntap):
        back = ntap - 1 - k
        shifted = pltpu.roll(win, n - (CONV_PAD - back), 0)
        term = w_ref[k:k + 1, :] * shifted[:CONV_CHUNK]
        acc = term if acc is None else acc + term
    return acc


def _anticausal_conv(xp_ref, w_ref, ntap, r0):
    n = CONV_CHUNK + CONV_PAD
    win = xp_ref[pl.ds(pl.multiple_of(CONV_PAD + r0, CONV_PAD), n), :]
    acc = None
    for k in range(ntap):
        ahead = ntap - 1 - k
        shifted = win if ahead == 0 else pltpu.roll(win, n - ahead, 0)
        term = w_ref[k:k + 1, :] * shifted[:CONV_CHUNK]
        acc = term if acc is None else acc + term
    return acc


def _conv_weight_grad(dw_ref, d_chunk, xp_ref, ntap, r0):
    n = CONV_CHUNK + CONV_PAD
    win = xp_ref[pl.ds(r0, n), :]
    for k in range(ntap):
        back = ntap - 1 - k
        shifted = pltpu.roll(win, n - (CONV_PAD - back), 0)
        dw_ref[k * SUB:(k + 1) * SUB, :] += _colsum8(d_chunk * shifted[:CONV_CHUNK])


def _zero_pads(ref, t):
    ref[0:CONV_PAD, :] = jnp.zeros((CONV_PAD, LANE), F32)
    ref[CONV_PAD + t:CONV_PAD + t + CONV_PAD, :] = jnp.zeros((CONV_PAD, LANE), F32)


def _for_chunks(t, fn):
    def step(idx, carry):
        fn(pl.multiple_of(idx * CONV_CHUNK, CONV_CHUNK))
        return carry

    lax.fori_loop(0, t // CONV_CHUNK, step, 0)


def _conv_forward(proj, conf_w, conf_b, short_w, dc, deps=()):
    t = proj.shape[0]
    nc = dc // LANE

    def body(av_ref, ag_ref, bg_ref, cg_ref, v_ref, cw_ref, cb_ref, sw_ref, *rest):
        a1_ref, s_ref, xa, xb = rest[len(deps):]
        _zero_pads(xa, t)
        _zero_pads(xb, t)
        xa[CONV_PAD:CONV_PAD + t, :] = av_ref[...] * _sigmoid(ag_ref[...])
        xb[CONV_PAD:CONV_PAD + t, :] = cg_ref[...] * v_ref[...]

        def chunk(r0):
            rs = pl.ds(r0, CONV_CHUNK)
            a1_ref[rs, :] = _causal_conv(xa, cw_ref, CONF_K, r0) + cb_ref[...]
            s_ref[rs, :] = (bg_ref[rs, :] * _causal_conv(xb, sw_ref, SHORT_K, r0)).astype(BF16)

        _for_chunks(t, chunk)

    col = lambda g: pl.BlockSpec((t, LANE), lambda c, g=g: (0, g * nc + c))
    return pl.pallas_call(
        body, name="conv_forward", grid=(nc,),
        in_specs=[col(0), col(1), col(2), col(3), col(4),
                  pl.BlockSpec((CONF_K, LANE), lambda c: (0, c)),
                  pl.BlockSpec((1, LANE), lambda c: (0, c)),
                  pl.BlockSpec((SHORT_K, LANE), lambda c: (0, c))] + [ANY] * len(deps),
        out_specs=[pl.BlockSpec((t, LANE), lambda c: (0, c)), pl.BlockSpec((t, LANE), lambda c: (0, c))],
        out_shape=[jax.ShapeDtypeStruct((t, dc), F32), jax.ShapeDtypeStruct((t, dc), BF16)],
        scratch_shapes=[pltpu.VMEM((t + 2 * CONV_PAD, LANE), F32), pltpu.VMEM((t + 2 * CONV_PAD, LANE), F32)],
        compiler_params=_params(1),
    )(proj, proj, proj, proj, proj, conf_w, conf_b, short_w, *deps)


def _conv_backward(dproj, proj, da1, ds, conf_w, short_w, dc):
    t = proj.shape[0]
    nc = dc // LANE

    def body(dp_in, av_ref, ag_ref, bg_ref, cg_ref, v_ref, da1_ref, ds_ref, cw_ref, sw_ref,
             dp_ref, dcw_ref, dcb_ref, dsw_ref, xa, xb, da, db, stage, sems):
        del dp_in
        c = pl.program_id(0)
        for ref in (xa, xb, da, db):
            _zero_pads(ref, t)
        xa[CONV_PAD:CONV_PAD + t, :] = av_ref[...] * _sigmoid(ag_ref[...])
        xb[CONV_PAD:CONV_PAD + t, :] = cg_ref[...] * v_ref[...]
        da[CONV_PAD:CONV_PAD + t, :] = da1_ref[...]
        dcw_ref[...] = jnp.zeros(dcw_ref.shape, F32)
        dsw_ref[...] = jnp.zeros(dsw_ref.shape, F32)
        dcb_ref[...] = jnp.zeros(dcb_ref.shape, F32)

        def through_gate(r0):
            rs = pl.ds(r0, CONV_CHUNK)
            ds_c = ds_ref[rs, :]
            stage[2, rs, :] = (ds_c * _causal_conv(xb, sw_ref, SHORT_K, r0)).astype(BF16)
            db[pl.ds(pl.multiple_of(CONV_PAD + r0, CONV_PAD), CONV_CHUNK), :] = ds_c * bg_ref[rs, :]

        _for_chunks(t, through_gate)

        def through_convs(r0):
            rs = pl.ds(r0, CONV_CHUNK)
            da0 = _anticausal_conv(da, cw_ref, CONF_K, r0)
            sg = _sigmoid(ag_ref[rs, :])
            stage[0, rs, :] = (da0 * sg).astype(BF16)
            stage[1, rs, :] = (da0 * av_ref[rs, :] * sg * (1.0 - sg)).astype(BF16)
            dcv = _anticausal_conv(db, sw_ref, SHORT_K, r0)
            stage[3, rs, :] = (dcv * v_ref[rs, :]).astype(BF16)
            stage[4, rs, :] = (dcv * cg_ref[rs, :]).astype(BF16)
            da1_c = da1_ref[rs, :]
            _conv_weight_grad(dcw_ref, da1_c, xa, CONF_K, r0)
            _conv_weight_grad(dsw_ref, ds_ref[rs, :] * bg_ref[rs, :], xb, SHORT_K, r0)
            dcb_ref[...] += _colsum8(da1_c)

        _for_chunks(t, through_convs)
        copies = [pltpu.make_async_copy(
            stage.at[g], dp_ref.at[:, pl.ds(pl.multiple_of((g * nc + c) * LANE, LANE), LANE)], sems.at[g])
            for g in range(5)]
        for cp in copies:
            cp.start()
        for cp in copies:
            cp.wait()

    col = lambda g: pl.BlockSpec((t, LANE), lambda c, g=g: (0, g * nc + c))
    blk = pl.BlockSpec((t, LANE), lambda c: (0, c))
    return pl.pallas_call(
        body, name="conv_backward", grid=(nc,),
        in_specs=[ANY, col(0), col(1), col(2), col(3), col(4), blk, blk,
                  pl.BlockSpec((CONF_K, LANE), lambda c: (0, c)),
                  pl.BlockSpec((SHORT_K, LANE), lambda c: (0, c))],
        out_specs=[ANY,
                   pl.BlockSpec((CONF_K * SUB, LANE), lambda c: (0, c)),
                   pl.BlockSpec((SUB, LANE), lambda c: (0, c)),
                   pl.BlockSpec((SHORT_K * SUB, LANE), lambda c: (0, c))],
        out_shape=[jax.ShapeDtypeStruct(dproj.shape, dproj.dtype),
                   jax.ShapeDtypeStruct((CONF_K * SUB, dc), F32),
                   jax.ShapeDtypeStruct((SUB, dc), F32),
                   jax.ShapeDtypeStruct((SHORT_K * SUB, dc), F32)],
        scratch_shapes=[pltpu.VMEM((t + 2 * CONV_PAD, LANE), F32)] * 4
                       + [pltpu.VMEM((5, t, LANE), BF16), pltpu.SemaphoreType.DMA((5,))],
        input_output_aliases={0: 0},
        compiler_params=_params(1),
    )(dproj, proj, proj, proj, proj, proj, da1, ds, conf_w, short_w)


def _adamw_math(w, g, m, v):
    m = ADAM_B1 * m + (1.0 - ADAM_B1) * g
    v = ADAM_B2 * v + (1.0 - ADAM_B2) * (g * g)
    m_hat = m / (1.0 - ADAM_B1 ** ADAM_STEP)
    v_hat = v / (1.0 - ADAM_B2 ** ADAM_STEP)
    delta = -ADAM_LR * (m_hat / (jnp.sqrt(v_hat) + ADAM_EPS) + ADAM_WD * w)
    return delta, m, v


def _cast_into_slot(name, w, me_arr, deps=()):
    r, c = w.shape
    tr = 256

    def body(me_ref, w_ref, *rest):
        del me_ref
        rest[-1][0] = w_ref[...].astype(BF16)

    return pl.pallas_call(
        body, name=name,
        grid_spec=pltpu.PrefetchScalarGridSpec(
            num_scalar_prefetch=1, grid=(r // tr,),
            in_specs=[pl.BlockSpec((tr, c), lambda i, me: (i, 0))] + [ANY] * len(deps),
            out_specs=pl.BlockSpec((1, tr, c), lambda i, me: (me[0], i, 0))),
        out_shape=jax.ShapeDtypeStruct((N_DEV, r, c), BF16),
        compiler_params=_params(1),
    )(me_arr, w, *deps)


def _chip_sum(name, full, from_sibling, me_arr):
    _, r, c = full.shape
    tr = min(r, 512)

    def body(me_ref, full_ref, sib_ref, sums_ref):
        del me_ref
        sums_ref[0] = (full_ref[0].astype(F32) + sib_ref[0].astype(F32)).astype(BF16)

    other = lambda k, me: (me[0] // 2 + 1 + k) % 4
    return pl.pallas_call(
        body, name=name,
        grid_spec=pltpu.PrefetchScalarGridSpec(
            num_scalar_prefetch=1, grid=(r // tr, 3),
            in_specs=[pl.BlockSpec((1, tr, c), lambda i, k, me: (2 * other(k, me) + me[0] % 2, i, 0)),
                      pl.BlockSpec((1, tr, c), lambda i, k, me: (other(k, me), i, 0))],
            out_specs=pl.BlockSpec((1, tr, c), lambda i, k, me: (other(k, me), i, 0))),
        out_shape=jax.ShapeDtypeStruct((4, r, c), BF16),
        compiler_params=_params(2),
    )(me_arr, full, from_sibling)


def _adamw_shard(name, w, m, v, parts, me_arr, deps=()):
    r, c = w.shape
    tr = min(256, r // len(parts))
    np_ = len(parts)
    per = r // np_ // tr

    def body(me_ref, w_ref, m_ref, v_ref, *rest):
        g_out, d_out, m_out, v_out = rest[5 * np_ + len(deps):]
        g = None
        for p in range(np_):
            gp = rest[5 * p][...]
            for l_ref in rest[5 * p + 1:5 * p + 5]:
                gp = gp + l_ref[0].astype(F32)
            g = gp if g is None else jnp.where(pl.program_id(0) // per == p, gp, g)
        delta, m_new, v_new = _adamw_math(w_ref[...], g, m_ref[...], v_ref[...])
        g_out[...] = g
        d_out[...] = delta
        m_out[...] = m_new
        v_out[...] = v_new

    tile = pl.BlockSpec((tr, c), lambda i, me: (i, 0))
    part_specs, part_args = [], []
    for p, (g_own, from_sibling, landed) in enumerate(parts):
        row = lambda i, p=p: jnp.clip(i - p * per, 0, per - 1)
        part_specs.append(pl.BlockSpec((tr, c), lambda i, me, row=row: (row(i), 0)))
        part_specs += [pl.BlockSpec((1, tr, c), lambda i, me, k=k, row=row: ((me[0] // 2 + k) % 4, row(i), 0))
                       for k in range(4)]
        part_args += [g_own, from_sibling, landed, landed, landed]
    return pl.pallas_call(
        body, name=name,
        grid_spec=pltpu.PrefetchScalarGridSpec(
            num_scalar_prefetch=1, grid=(r // tr,),
            in_specs=[tile] * 3 + part_specs + [ANY] * len(deps), out_specs=[tile] * 4),
        out_shape=[jax.ShapeDtypeStruct((r, c), F32)] * 4,
        compiler_params=_params(1),
    )(me_arr, w, m, v, *part_args, *deps)


SMALL_W = 1024
VEC_ROWS = 16
LOSS_ROW = 15
META_ROW0 = 16
CONF_ROW0 = 64
SHORT_ROW0 = 96
SMALL_ROWS = 104


def _pack_small(vec_parts, dmeta, dcw, dsw, loss_blk, me_arr):
    widths = [p.shape[1] for p in vec_parts]
    nv = len(vec_parts)

    def body(me_ref, *refs):
        del me_ref
        parts, (dmeta_ref, dcw_ref, dsw_ref, loss_ref, out_ref) = refs[:nv], refs[nv:]
        out_ref[0] = jnp.zeros((SMALL_ROWS, SMALL_W), F32)
        out_ref[0, LOSS_ROW:LOSS_ROW + 1, 0:LANE] = loss_ref[0:1, :]
        row = 0
        for p_ref, wd in zip(parts, widths):
            s = jnp.sum(p_ref[...], axis=0, keepdims=True)
            for h in range(wd // SMALL_W):
                out_ref[0, row:row + 1, :] = s[:, h * SMALL_W:(h + 1) * SMALL_W]
                row += 1
        for h in range(dmeta_ref.shape[1] // SMALL_W):
            out_ref[0, META_ROW0 + h * N_META:META_ROW0 + (h + 1) * N_META, :] = dmeta_ref[:, h * SMALL_W:(h + 1) * SMALL_W]
        for k in range(CONF_K):
            out_ref[0, CONF_ROW0 + k:CONF_ROW0 + k + 1, :] = jnp.sum(dcw_ref[k * SUB:(k + 1) * SUB, :], axis=0, keepdims=True)
        for k in range(SHORT_K):
            out_ref[0, SHORT_ROW0 + k:SHORT_ROW0 + k + 1, :] = jnp.sum(dsw_ref[k * SUB:(k + 1) * SUB, :], axis=0, keepdims=True)

    ins = [*vec_parts, dmeta, dcw, dsw, loss_blk]
    return pl.pallas_call(
        body, name="pack_small",
        grid_spec=pltpu.PrefetchScalarGridSpec(
            num_scalar_prefetch=1, grid=(1,),
            in_specs=[pl.BlockSpec(a.shape, lambda i, me: (0, 0)) for a in ins],
            out_specs=pl.BlockSpec((1, SMALL_ROWS, SMALL_W), lambda i, me: (me[0], 0, 0))),
        out_shape=jax.ShapeDtypeStruct((N_DEV, SMALL_ROWS, SMALL_W), F32),
        compiler_params=_params(1),
    )(me_arr, *ins)


def _small_update(gathered, me_arr, vec_params, meta_p, conf_p, short_p):
    widths = [p[0].shape[1] for p in vec_params]
    nv = len(vec_params)
    mcols = meta_p[0].shape[1]
    per_row = SMALL_W // mcols

    def body(me_ref, gv_ref, gm_ref, gc_ref, gs_ref, *rest):
        del me_ref
        ins, outs = rest[:3 * (nv + 3)], rest[3 * (nv + 3):]

        def total(ref, r0, rows):
            s = ref[0, r0:r0 + rows, :]
            for dev in range(1, N_DEV):
                s = s + ref[dev, r0:r0 + rows, :]
            return s

        grads = []
        row = 0
        for wd in widths:
            pieces = [total(gv_ref, row + h, 1) for h in range(wd // SMALL_W)]
            grads.append(pieces[0] if len(pieces) == 1 else jnp.concatenate(pieces, axis=1))
            row += len(pieces)
        grads.append(total(gm_ref, 0, N_META))
        grads.append(total(gc_ref, 0, CONF_K))
        grads.append(total(gs_ref, 0, SHORT_K))
        loss = gv_ref[0, LOSS_ROW:LOSS_ROW + 1, 0:LANE]
        for dev in range(1, N_DEV):
            loss = loss + gv_ref[dev, LOSS_ROW:LOSS_ROW + 1, 0:LANE]
        outs[-1][...] = loss
        for idx, g in enumerate(grads):
            w_ref, m_ref, v_ref = ins[3 * idx:3 * idx + 3]
            delta, m_new, v_new = _adamw_math(w_ref[...], g, m_ref[...], v_ref[...])
            g_out, d_out, m_out, v_out = outs[4 * idx:4 * idx + 4]
            g_out[...] = g
            d_out[...] = delta
            m_out[...] = m_new
            v_out[...] = v_new

    params = list(vec_params) + [meta_p, conf_p, short_p]
    flat = [a for p in params for a in p]
    whole = lambda a: pl.BlockSpec(a.shape, lambda i, me: (0,) * a.ndim)
    outs = pl.pallas_call(
        body, name="small_update",
        grid_spec=pltpu.PrefetchScalarGridSpec(
            num_scalar_prefetch=1, grid=(1,),
            in_specs=[pl.BlockSpec((N_DEV, VEC_ROWS, SMALL_W), lambda i, me: (0, 0, 0)),
                      pl.BlockSpec((N_DEV, N_META, mcols),
                                   lambda i, me: (0, META_ROW0 // N_META + me[0] // per_row, me[0] % per_row)),
                      pl.BlockSpec((N_DEV, 32, LANE), lambda i, me: (0, CONF_ROW0 // 32, me[0])),
                      pl.BlockSpec((N_DEV, SUB, LANE), lambda i, me: (0, SHORT_ROW0 // SUB, me[0]))]
                     + [whole(a) for a in flat],
            out_specs=[whole(p[0]) for p in params for _ in range(4)]
                      + [pl.BlockSpec((1, LANE), lambda i, me: (0, 0))]),
        out_shape=[jax.ShapeDtypeStruct(p[0].shape, F32) for p in params for _ in range(4)]
                  + [jax.ShapeDtypeStruct((1, LANE), F32)],
        compiler_params=_params(1),
    )(me_arr, gathered, gathered, gathered, gathered, *flat)
    return [tuple(outs[4 * i:4 * i + 4]) for i in range(len(params))], outs[-1][0, 0]


def kernel(x, meta, g_pre_mix, w_in, b_gates, conf_dw_w, conf_dw_b, conf_ln_g, conf_ln_b, conf_w_pw, short_dw_w, short_w_out, w_o, g_post_mix, g_pre_mlp, w_up, w_down, g_post_mlp, loss_target, m_meta, m_g_pre_mix, m_w_in, m_b_gates, m_conf_dw_w, m_conf_dw_b, m_conf_ln_g, m_conf_ln_b, m_conf_w_pw, m_short_dw_w, m_short_w_out, m_w_o, m_g_post_mix, m_g_pre_mlp, m_w_up, m_w_down, m_g_post_mlp, v_meta, v_g_pre_mix, v_w_in, v_b_gates, v_conf_dw_w, v_conf_dw_b, v_conf_ln_g, v_conf_ln_b, v_conf_w_pw, v_short_dw_w, v_short_w_out, v_w_o, v_g_post_mix, v_g_pre_mlp, v_w_up, v_w_down, v_g_post_mlp):
    seq, d = x.shape[1], x.shape[2]
    dc = conf_w_pw.shape[1]
    t_real = N_META + seq
    t = -(-t_real // ROW_TILE) * ROW_TILE
    tm = t // 2
    assert tm % 16 == 0 and d % 1024 == 0 and dc % 1024 == 0
    x_idx, y_idx, c_idx = _position()
    me_arr = jnp.reshape(4 * x_idx + 2 * y_idx + c_idx, (1,)).astype(jnp.int32)

    big = [w_in[0], conf_w_pw[0], short_w_out[0], w_o[0], w_up[0], w_down[0]]
    big_names = ["w_in", "conf_w_pw", "short_w_out", "w_o", "w_up", "w_down"]
    groups = [[0], [1, 2, 3], [4], [5]]
    slots, deps = [], []
    for g, idxs in enumerate(groups):
        slots.append([_cast_into_slot("cast_" + big_names[i], big[i], me_arr, deps=deps) for i in idxs])
        if g == 0:
            direct0 = _remote_start("gather0_direct_start", "gather_direct", slots[0])
            deps = [direct0[3]]
    casts = [sl for group in slots[1:] for sl in group]
    meta_g, cw_g, sw_g = _all_gather("gather_small_params", [meta, conf_dw_w[0], short_dw_w[0]], deps=casts)

    def start_direct(g, deps):
        send, recv, bufs, tok = _remote_start("gather%d_direct_start" % g, "gather_direct", slots[g], deps=deps)
        return (send, recv, bufs), tok

    def relay(g, state, after):
        send, recv, bufs = state
        bufs = _remote_wait("gather%d_direct_wait" % g, "gather_direct", send, recv, bufs, len(bufs), after)
        send, recv, bufs, tok = _remote_start("gather%d_relay_start" % g, "gather_relay", bufs)
        return (send, recv, bufs), tok

    def gathered(g, state, after):
        send, recv, bufs = state
        bufs = _remote_wait("gather%d_relay_wait" % g, "gather_relay", send, recv, bufs, len(bufs), after)
        send, recv, bufs, tok = _remote_start("gather%d_diag_start" % g, "gather_diag", bufs)
        return _remote_wait("gather%d_diag_wait" % g, "gather_diag", send, recv, bufs, len(bufs), [tok])

    unshard =lambda g: jnp.transpose(g, (1, 0, 2)).reshape(g.shape[1], -1)
    meta_full, cw_full, sw_full = unshard(meta_g), unshard(cw_g), unshard(sw_g)

    relay0, tok = relay(0, direct0[:3], [meta_g])
    zrows = jnp.zeros((t - t_real, d), F32) + tok[0, 0] * 0.0
    h0 = jnp.concatenate([meta_full, x[0], zrows], axis=0)
    tgt = jnp.concatenate([jnp.zeros((N_META, d), F32), loss_target[0], zrows], axis=0)
    n = _pre_norm(h0, g_pre_mix)
    direct1, tok = start_direct(1, [tok])
    direct2, tok = start_direct(2, [tok])
    win_g, = gathered(0, relay0, [tok, n])
    proj = _mm_cols("proj", n, win_g, tm=tm)[0]
    relay1, tok = relay(1, direct1, [proj])
    direct3, tok = start_direct(3, [tok])
    a1, s = _conv_forward(proj, cw_full, conf_dw_b, sw_full, dc, deps=[tok])
    a3 = _layer_norm_silu(a1, conf_ln_g, conf_ln_b)
    wpw_g, wso_g, wo_g = gathered(1, relay1, [a3])
    wo_full = wo_g.reshape(d, d)
    ya, yb, gate_a, gate_b, m_mix = _branch_merge(a3, s, wpw_g, wso_g, proj, b_gates, d)
    mix = _mm_rows("mix", m_mix, wo_full, tm=tm // 2, tn=d)
    relay2, tok = relay(2, direct2, [mix])
    relay3, tok = relay(3, direct3, [tok])
    h1, n2 = _post_mix(mix, h0, g_post_mix, g_pre_mlp, deps=[tok])
    wup_g, = gathered(2, relay2, [n2])

    def up_epilogue(acc):
        r = jnp.maximum(acc, 0.0)
        return r * r, r

    f, relu_up = _mm_cols("mlp_up", n2, wup_g, tm=tm, epilogue=up_epilogue, out_dtypes=(BF16, BF16))
    wdn_g, = gathered(3, relay3, [f])
    wdn_full = wdn_g.reshape(-1, d)
    fo = _mm_rows("mlp_down", f, wdn_full, tm=tm // 2, tn=512)
    dfo, dh2, dg_post_mlp, loss_blk = _loss_head(fo, h1, tgt, g_post_mlp, t_real)

    def reduce_start(tag, fulls, deps):
        lands = [lax.empty((4,) + g.shape[1:], BF16) for g in fulls]
        send, recv, bufs, tok = _remote_start("reduce_%s_d2d_start" % tag, "reduce_d2d", fulls, lands, deps=deps)
        return (send, recv, bufs), tok

    def reduce_middle(tag, state, owns, after):
        send, recv, bufs = state
        k = len(owns)
        bufs = _remote_wait("reduce_%s_d2d_wait" % tag, "reduce_d2d", send, recv, bufs, k, after)
        from_sibling = bufs[k:]
        sums = [_chip_sum("chip_sum_%s%d" % (tag, i), bufs[i], from_sibling[i], me_arr) for i in range(k)]
        lands = [lax.empty(sm.shape, BF16) for sm in sums]
        send, recv, bufs, tok = _remote_start("reduce_%s_ici_start" % tag, "reduce_ici", sums, lands)
        return (send, recv, bufs, list(zip(owns, from_sibling))), tok

    def reduce_finish(tag, state, after):
        send, recv, bufs, local = state
        k = len(local)
        bufs = _remote_wait("reduce_%s_ici_wait" % tag, "reduce_ici", send, recv, bufs, k, after)
        return [(own, sib, landed) for (own, sib), landed in zip(local, bufs[k:])]

    dup = _mm_nt_blocks("d_up", dfo, wdn_full, tm=tm, tkb=1024, extra=(relu_up,),
                        epilogue=lambda acc, r: (acc * (2.0 * r.astype(F32)),), out_dtypes=(BF16,))[0]
    gw_down, gw_down_own = _mm_tn("dw_down", f, dfo, me_arr, m=f.shape[1], n=d, tma=512, tn=d, sharded="rows")
    red_down, tok = reduce_start("down", [gw_down], ())
    dn2 = _mm_nt_acc("d_n2", dup, wup_g, tm=tm // 2, tn=512, deps=[tok])
    gw_up, gw_up_own = _mm_tn("dw_up", n2, dup, me_arr, m=d, n=dup.shape[1], tma=512, tn=1024, sharded="cols")
    red_down, tok = reduce_middle("down", red_down, [gw_down_own], [dn2])
    red_up, tok = reduce_start("up", [gw_up], [tok])
    dh1, dmix, dg_pre_mlp, dg_post_mix = _mid_norm_bwd(dn2, h1, dh2, mix, g_pre_mlp, g_post_mix, deps=[tok])
    dya, dyb, dproj, db_a, db_b = _gate_backward(dmix, wo_full, gate_a, gate_b, ya, yb, proj.shape[1], tm // 2)
    db_gates = jnp.concatenate([db_a, db_b], axis=1)
    red_up, tok = reduce_middle("up", red_up, [gw_up_own], [dya])
    gw_o, gw_o_own = _mm_tn("dw_o", m_mix, dmix, me_arr, m=d, n=d, tma=d // N_DEV, tn=d, sharded="rows", deps=[tok])
    da3 = _mm_nt_acc("d_a3", dya, wpw_g, tm=tm, tn=512)
    gw_pw, gw_pw_own = _mm_tn("dw_pw", a3, dya, me_arr, m=dc, n=d, tma=512, tn=d, sharded="cols")
    dsb = _mm_nt_acc("d_s", dyb, wso_g, tm=tm, tn=512)
    gw_so, gw_so_own = _mm_tn("dw_so", s, dyb, me_arr, m=dc, n=d, tma=512, tn=d, sharded="cols")
    red_mix, tok = reduce_start("mix", [gw_pw, gw_so, gw_o], ())
    da1, dln_g, dln_b = _layer_norm_silu_bwd(da3, a1, conf_ln_g, conf_ln_b, deps=[tok])
    dproj, dcw, dcb, dsw = _conv_backward(dproj, proj, da1, dsb, cw_full, sw_full, dc)
    red_mix, tok = reduce_middle("mix", red_mix, [gw_pw_own, gw_so_own, gw_o_own], [dcb])
    in_cb = w_in.shape[2]
    half = d // 2
    red_in = []
    for part in range(2):
        gw, own = _mm_tn("dw_in%d" % part, n, dproj, me_arr, m=half, n=proj.shape[1], tma=512, tn=2 * in_cb,
                         sharded="cols", a_off=part * (half // 512), deps=[tok])
        state, tok = reduce_start("in%d" % part, [gw], ())
        red_in.append((state, own))
    for part in range(2):
        state, own = red_in[part]
        red_in[part], tok = reduce_middle("in%d" % part, state, [own], [tok])
    dn = _mm_nt_acc("d_n", dproj, win_g, tm=tm // 2, tn=512, deps=[tok])
    dh0, dg_pre_mix = _pre_norm_bwd(dn, h0, dh1, g_pre_mix)
    grad_x = dh0[N_META:t_real][None]

    vec_parts = [dg_pre_mix, db_gates, dcb, dln_g, dln_b, dg_post_mix, dg_pre_mlp, dg_post_mlp]
    packed = _pack_small(vec_parts, dh0[:N_META], dcw, dsw, loss_blk, me_arr)
    send, recv, bufs, tok = _remote_start("small_grads_ici_start", "gather_ici", [packed])
    vec_names = ["g_pre_mix", "b_gates", "conf_dw_b", "conf_ln_g", "conf_ln_b", "g_post_mix", "g_pre_mlp", "g_post_mlp"]
    env = locals()
    results = {}

    def update(nm, parts, deps=()):
        res = _adamw_shard("adamw_" + nm, env[nm][0], env["m_" + nm][0], env["v_" + nm][0], parts, me_arr, deps=deps)
        results[nm] = tuple(r[None] for r in res)
        return res[0]

    done = [update("w_down", reduce_finish("down", red_down, [tok]), deps=[tok])]
    done.append(update("w_up", reduce_finish("up", red_up, done)))
    bufs = _remote_wait("small_grads_ici_wait", "gather_ici", send, recv, bufs, 1, done)
    send, recv, bufs, tok = _remote_start("small_grads_d2d_start", "gather_d2d", bufs)
    for nm, pair in zip(["conf_w_pw", "short_w_out", "w_o"], reduce_finish("mix", red_mix, [tok])):
        done.append(update(nm, [pair], deps=[tok]))
    small_g, = _remote_wait("small_grads_d2d_wait", "gather_d2d", send, recv, bufs, 1, done)
    triple = lambda nm, sq: tuple(env[p + nm][0] if sq else env[p + nm] for p in ("", "m_", "v_"))
    small, loss = _small_update(small_g, me_arr, [triple(nm, False) for nm in vec_names],
                                triple("meta", False), triple("conf_dw_w", True), triple("short_dw_w", True))
    for nm, res in zip(vec_names + ["meta"], small[:len(vec_names) + 1]):
        results[nm] = res
    results["conf_dw_w"] = tuple(r[None] for r in small[-2])
    results["short_dw_w"] = tuple(r[None] for r in small[-1])
    update("w_in", [reduce_finish("in%d" % part, red_in[part], [small[0][0]])[0] for part in range(2)])

    order = ["meta", "g_pre_mix", "w_in", "b_gates", "conf_dw_w", "conf_dw_b", "conf_ln_g", "conf_ln_b", "conf_w_pw",
             "short_dw_w", "short_w_out", "w_o", "g_post_mix", "g_pre_mlp", "w_up", "w_down", "g_post_mlp"]
    return (loss, grad_x, *[results[nm][0] for nm in order], *[results[nm][1] for nm in order],
            *[results[nm][2] for nm in order], *[results[nm][3] for nm in order])
```

```python
import jax
import jax.numpy as jnp
from jax import lax
from jax.experimental import pallas as pl
from jax.experimental.pallas import tpu as pltpu

N_DEV = 8
N_META = 16
CONF_K = 31
SHORT_K = 3
RMS_EPS = 1e-6
LN_EPS = 1e-5
ADAM_LR = 0.001
ADAM_B1 = 0.9
ADAM_B2 = 0.999
ADAM_EPS = 1e-08
ADAM_WD = 0.01
ADAM_STEP = 10

LANE = 128
SUB = 8
ROW_TILE = 128
CONV_PAD = 32
CONV_CHUNK = 128
VMEM_LIMIT = 56 * 1024 * 1024
MID_VMEM_LIMIT = 60 * 1024 * 1024

F32 = jnp.float32
BF16 = jnp.bfloat16
MESH = pl.DeviceIdType.MESH
ANY = pl.BlockSpec(memory_space=pl.ANY)
HBM_SPEC = pl.BlockSpec(memory_space=pltpu.HBM)
SEM_SPEC = pl.BlockSpec(memory_space=pltpu.SEMAPHORE)
EFFECT = pltpu.SideEffectType.DATAFLOW_SIDE_EFFECTING


def _params(n_axes):
    return pltpu.CompilerParams(dimension_semantics=("arbitrary",) * n_axes, vmem_limit_bytes=VMEM_LIMIT)


def _sigmoid(z):
    return 1.0 / (1.0 + jnp.exp(-z))


def _colsum8(v):
    r, c = v.shape
    return jnp.sum(v.reshape(r // SUB, SUB, c), axis=0)


def _position():
    x, y, c = lax.axis_index("x"), lax.axis_index("y"), lax.axis_index("c")
    return x, y, c


def _flat(p):
    return 4 * p[0] + 2 * p[1] + p[2]


def _all_gather(name, shards, deps=()):
    n, nd = len(shards), len(deps)

    def body(*refs):
        ins, outs = refs[:n], refs[n + nd:2 * n + nd]
        send_sems, recv_sems, local_sems = refs[2 * n + nd:]
        x, y, c = _position()
        me, sibling = (x, y, c), (x, y, 1 - c)
        chips = [(1 - x, y), (x, 1 - y), (1 - x, 1 - y)]

        def copy(q, k, block, to, src=None):
            dst = outs[q].at[_flat(block)]
            return pltpu.make_async_remote_copy(
                src_ref=dst if src is None else src, dst_ref=dst,
                send_sem=send_sems.at[q, k], recv_sem=recv_sems.at[q, k],
                device_id=to, device_id_type=MESH)

        mine = [pltpu.make_async_copy(ins[q], outs[q].at[_flat(me)], local_sems.at[q]) for q in range(n)]
        for cp in mine:
            cp.start()
        first = []
        for q in range(n):
            first.append(copy(q, 0, me, sibling, src=ins[q]))
            for j, chip in enumerate(chips):
                first.append(copy(q, 1 + j, me, (*chip, c), src=ins[q]))
        for cp in first:
            cp.start()
        passed = []
        for q in range(n):
            for j, chip in enumerate(chips):
                copy(q, 1 + j, (*chip, c), me).wait_recv()
                fwd = copy(q, 4 + j, (*chip, c), sibling)
                fwd.start()
                passed.append(fwd)
        for q in range(n):
            copy(q, 0, sibling, me).wait_recv()
            for j, chip in enumerate(chips):
                copy(q, 4 + j, (*chip, 1 - c), me).wait_recv()
        for cp in first + passed:
            cp.wait_send()
        for cp in mine:
            cp.wait()

    return pl.pallas_call(
        body, name=name,
        in_specs=[ANY] * (n + nd), out_specs=[ANY] * n,
        out_shape=[jax.ShapeDtypeStruct((N_DEV,) + s.shape, s.dtype) for s in shards],
        scratch_shapes=[pltpu.SemaphoreType.DMA((n, 7)), pltpu.SemaphoreType.DMA((n, 7)),
                        pltpu.SemaphoreType.DMA((n,))],
    )(*shards, *deps)


N_COPIES = {"gather_ici": 4, "gather_d2d": 3, "gather_direct": 3, "gather_relay": 3, "gather_diag": 1,
            "reduce_d2d": 4, "reduce_ici": 3}


def _copy_plan(kind):
    x, y, c = _position()
    me, sibling = (x, y, c), (x, y, 1 - c)
    chips = [(1 - x, y), (x, 1 - y), (1 - x, 1 - y)]
    if kind == "gather_ici":
        return [(_flat(me), _flat(me), sibling)] + [(_flat(me), _flat(me), (*ch, c)) for ch in chips]
    if kind == "gather_d2d":
        return [(_flat((*ch, c)), _flat((*ch, c)), sibling) for ch in chips]
    if kind == "gather_direct":
        return [(_flat(me), _flat(me), sibling)] + [(_flat(me), _flat(me), (*ch, c)) for ch in chips[:2]]
    if kind == "gather_relay":
        held, to = (x ^ (1 - c), y ^ c, c), (x ^ c, y ^ (1 - c), c)
        return [(_flat(held), _flat(held), to)] + [(_flat((*ch, c)), _flat((*ch, c)), sibling) for ch in chips[:2]]
    if kind == "gather_diag":
        return [(_flat((*chips[2], c)), _flat((*chips[2], c)), sibling)]
    if kind == "reduce_d2d":
        return [(2 * chip + (1 - c), chip, sibling) for chip in range(4)]
    return [(2 * ch[0] + ch[1], 2 * x + y, (*ch, c)) for ch in chips]


def _planned_copies(kind, srcs, dsts, send_sems, recv_sems):
    plan = _copy_plan(kind)
    return [pltpu.make_async_remote_copy(
        src_ref=src.at[s_slot], dst_ref=dst.at[d_slot],
        send_sem=send_sems.at[q * len(plan) + k], recv_sem=recv_sems.at[q * len(plan) + k],
        device_id=to, device_id_type=MESH)
        for q, (src, dst) in enumerate(zip(srcs, dsts)) for k, (s_slot, d_slot, to) in enumerate(plan)]


def _remote_start(name, kind, srcs, lands=None, deps=()):
    n = len(srcs)
    bufs = list(srcs) + ([] if lands is None else list(lands))
    nb, nd = len(bufs), len(deps)
    nsem = n * N_COPIES[kind]

    def body(*refs):
        ins = refs[:nb]
        send_sems, recv_sems = refs[nb + nd], refs[nb + nd + 1]
        token = refs[-1]
        for cp in _planned_copies(kind, ins[:n], ins[:n] if lands is None else ins[n:], send_sems, recv_sems):
            cp.start()
        token[...] = jnp.zeros_like(token)

    outs = pl.pallas_call(
        body, name=name,
        out_shape=(pltpu.SemaphoreType.DMA((nsem,)), pltpu.SemaphoreType.DMA((nsem,)),
                   *[pltpu.HBM(b.shape, b.dtype) for b in bufs], jax.ShapeDtypeStruct((SUB, LANE), F32)),
        in_specs=[HBM_SPEC] * nb + [ANY] * nd,
        out_specs=(SEM_SPEC, SEM_SPEC, *[HBM_SPEC] * nb, pl.BlockSpec(memory_space=pltpu.VMEM)),
        input_output_aliases={i: 2 + i for i in range(nb)},
        compiler_params=pltpu.CompilerParams(has_side_effects=EFFECT),
    )(*[pltpu.with_memory_space_constraint(b, pltpu.HBM) for b in bufs], *deps)
    return outs[0], outs[1], list(outs[2:2 + nb]), outs[-1]


def _remote_wait(name, kind, send_sems, recv_sems, bufs, n, after):
    nb, na = len(bufs), len(after)
    same = nb == n

    def body(*refs):
        ins = refs[:nb]
        sends, recvs = refs[nb], refs[nb + 1]
        for cp in _planned_copies(kind, ins[:n], ins[:n] if same else ins[n:], sends, recvs):
            cp.wait_send()
            cp.wait_recv()

    outs = pl.pallas_call(
        body, name=name,
        out_shape=[pltpu.HBM(b.shape, b.dtype) for b in bufs],
        in_specs=[HBM_SPEC] * nb + [SEM_SPEC, SEM_SPEC] + [ANY] * na,
        out_specs=[HBM_SPEC] * nb,
        input_output_aliases={i: i for i in range(nb)},
        compiler_params=pltpu.CompilerParams(has_side_effects=EFFECT),
    )(*bufs, send_sems, recv_sems, *after)
    return list(outs)


def _mm_cols(name, a, w, *, tm, nb=1, epilogue=None, out_dtypes=(F32,)):
    t, k = a.shape
    nblk, _, cb = w.shape

    def body(a_ref, w_ref, *o_refs):
        av = a_ref[...]
        for b in range(nb):
            acc = jnp.dot(av, w_ref[b], preferred_element_type=F32)
            outs = (acc,) if epilogue is None else epilogue(acc)
            for o_ref, o in zip(o_refs, outs):
                o_ref[:, b * cb:(b + 1) * cb] = o.astype(o_ref.dtype)

    return pl.pallas_call(
        body, name=name, grid=(nblk // nb, t // tm),
        in_specs=[pl.BlockSpec((tm, k), lambda j, i: (i, 0)),
                  pl.BlockSpec((nb, k, cb), lambda j, i: (j, 0, 0))],
        out_specs=[pl.BlockSpec((tm, nb * cb), lambda j, i: (i, j)) for _ in out_dtypes],
        out_shape=[jax.ShapeDtypeStruct((t, nblk * cb), dt) for dt in out_dtypes],
        compiler_params=_params(2),
    )(a, w)


def _mm_rows(name, a, w2d, *, tm, tn):
    t, kf = a.shape
    n = w2d.shape[1]

    def body(a_ref, w_ref, o_ref):
        o_ref[...] = jnp.dot(a_ref[...], w_ref[...], preferred_element_type=F32)

    return pl.pallas_call(
        body, name=name, grid=(t // tm, n // tn),
        in_specs=[pl.BlockSpec((tm, kf), lambda i, j: (i, 0)),
                  pl.BlockSpec((kf, tn), lambda i, j: (0, j))],
        out_specs=pl.BlockSpec((tm, tn), lambda i, j: (i, j)),
        out_shape=jax.ShapeDtypeStruct((t, n), F32),
        compiler_params=_params(2),
    )(a, w2d)


def _mm_nt_acc(name, dy, w, *, tm, tn, col_off=0, deps=()):
    t = dy.shape[0]
    nblk, k, cb = w.shape

    def body(dy_ref, w_ref, *rest):
        acc = None
        for b in range(nblk):
            d = lax.dot_general(dy_ref[:, b * cb:(b + 1) * cb], w_ref[b], (((1,), (1,)), ((), ())),
                                preferred_element_type=F32)
            acc = d if acc is None else acc + d
        rest[-1][...] = acc

    return pl.pallas_call(
        body, name=name, grid=(t // tm, k // tn),
        in_specs=[pl.BlockSpec((tm, nblk * cb), lambda i, j: (i, col_off)),
                  pl.BlockSpec((nblk, tn, cb), lambda i, j: (0, j, 0))] + [ANY] * len(deps),
        out_specs=pl.BlockSpec((tm, tn), lambda i, j: (i, j)),
        out_shape=jax.ShapeDtypeStruct((t, k), F32),
        compiler_params=_params(2),
    )(dy, w, *deps)


def _mm_nt_blocks(name, dy, w2d, *, tm, tkb, extra=(), epilogue=None, out_dtypes=(F32,)):
    t, n = dy.shape
    kf = w2d.shape[0]
    ne = len(extra)

    def body(dy_ref, w_ref, *rest):
        acc = lax.dot_general(dy_ref[...], w_ref[...], (((1,), (1,)), ((), ())), preferred_element_type=F32)
        outs = (acc,) if epilogue is None else epilogue(acc, *[e[...] for e in rest[:ne]])
        for o_ref, o in zip(rest[ne:], outs):
            o_ref[...] = o.astype(o_ref.dtype)

    return pl.pallas_call(
        body, name=name, grid=(kf // tkb, t // tm),
        in_specs=[pl.BlockSpec((tm, n), lambda kb, i: (i, 0)),
                  pl.BlockSpec((tkb, n), lambda kb, i: (kb, 0))]
                 + [pl.BlockSpec((tm, tkb), lambda kb, i: (i, kb)) for _ in extra],
        out_specs=[pl.BlockSpec((tm, tkb), lambda kb, i: (i, kb)) for _ in out_dtypes],
        out_shape=[jax.ShapeDtypeStruct((t, kf), dt) for dt in out_dtypes],
        compiler_params=_params(2),
    )(dy, w2d, *extra)


def _mm_tn(name, a, b, me_arr, *, m, n, tma, tn, sharded, a_off=0, b_off=0, deps=()):
    t = a.shape[0]
    if sharded == "cols":
        cb = n // N_DEV
        nb, q = max(tn // cb, 1), max(cb // tn, 1)
        tw = tn // nb
        full_shape, own_shape = (N_DEV, m, cb), (m, cb)
        full_spec = pl.BlockSpec((nb, tma, tw), lambda i, j, me: (j // q, i, j % q))
    else:
        kb = m // N_DEV
        p = kb // tma
        nb, tw = 1, tn
        full_shape, own_shape = (m, n), (kb, n)
        full_spec = pl.BlockSpec((tma, tn), lambda i, j, me: (i, j))

    def body(me_ref, a_ref, b_ref, *rest):
        full_ref, own_ref, stage, sem = rest[len(deps):]
        i, j = pl.program_id(0), pl.program_id(1)
        acc = lax.dot_general(a_ref[...], b_ref[...], (((0,), (0,)), ((), ())), preferred_element_type=F32)
        for blk in range(nb):
            part = acc[:, blk * tw:(blk + 1) * tw]
            if sharded == "cols":
                full_ref[blk] = part.astype(BF16)
                owner, r0, c0 = (j // q) * nb + blk, i * tma, (j % q) * tw
            else:
                full_ref[...] = part.astype(BF16)
                owner, r0, c0 = i // p, (i % p) * tma, j * tn

            @pl.when(owner == me_ref[0])
            def _():
                stage[...] = part
                cp = pltpu.make_async_copy(
                    stage, own_ref.at[pl.ds(pl.multiple_of(r0, tma), tma), pl.ds(pl.multiple_of(c0, tw), tw)], sem)
                cp.start()
                cp.wait()

    full, own = pl.pallas_call(
        body, name=name,
        grid_spec=pltpu.PrefetchScalarGridSpec(
            num_scalar_prefetch=1, grid=(m // tma, n // tn),
            in_specs=[pl.BlockSpec((t, tma), lambda i, j, me: (0, a_off + i)),
                      pl.BlockSpec((t, tn), lambda i, j, me: (0, b_off + j))] + [ANY] * len(deps),
            out_specs=[full_spec, ANY],
            scratch_shapes=[pltpu.VMEM((tma, tw), F32), pltpu.SemaphoreType.DMA(())]),
        out_shape=[jax.ShapeDtypeStruct(full_shape, BF16), jax.ShapeDtypeStruct(own_shape, F32)],
        compiler_params=_params(2),
    )(me_arr, a, b, *deps)
    if sharded == "rows":
        full = full.reshape(N_DEV, m // N_DEV, n)
    return full, own


def _row_tile(t):
    return t // 8 if (t // 8) % 16 == 0 else ROW_TILE


def _row_call(name, body, t, row_ins, full_ins, row_outs, acc_outs, scratch=(), deps=()):
    tm = _row_tile(t)
    nin = len(row_ins) + len(full_ins)

    def without_deps(*refs):
        body(*refs[:nin], *refs[nin + len(deps):])

    return pl.pallas_call(
        without_deps, name=name, grid=(t // tm,),
        in_specs=[pl.BlockSpec((tm, a.shape[1]), lambda i: (i, 0)) for a in row_ins]
                 + [pl.BlockSpec(a.shape, lambda i: (0, 0)) for a in full_ins] + [ANY] * len(deps),
        out_specs=[pl.BlockSpec((tm, c), lambda i: (i, 0)) for c, _ in row_outs]
                  + [pl.BlockSpec((r, c), lambda i: (0, 0)) for r, c in acc_outs],
        out_shape=[jax.ShapeDtypeStruct((t, c), dt) for c, dt in row_outs]
                  + [jax.ShapeDtypeStruct((r, c), F32) for r, c in acc_outs],
        scratch_shapes=list(scratch),
        compiler_params=_params(1),
    )(*row_ins, *full_ins, *deps)


def _accumulate(ref, v):
    @pl.when(pl.program_id(0) == 0)
    def _():
        ref[...] = v

    @pl.when(pl.program_id(0) > 0)
    def _():
        ref[...] += v


def _rms(v):
    return lax.rsqrt(jnp.mean(v * v, axis=-1, keepdims=True) + RMS_EPS)


def _rms_bwd(dout, u, r, g):
    du = dout * g
    dx = r * (du - u * jnp.mean(du * u, axis=-1, keepdims=True))
    return dx, _colsum8(dout * u)


def _pre_norm(h0, g):
    t, d = h0.shape

    def body(h_ref, g_ref, n_ref):
        h = h_ref[...]
        n_ref[...] = (h * _rms(h) * g_ref[...]).astype(BF16)

    return _row_call("pre_norm", body, t, [h0], [g], [(d, BF16)], [])[0]


def _mix_post(m_mix, wo_full, h0, g_post, g_pre, deps=()):
    t, d = h0.shape
    tm = _row_tile(t)

    def body(m_ref, wo_ref, h0_ref, gp_ref, gq_ref, *rest):
        mix_ref, h1_ref, n2_ref = rest[len(deps):]
        mix_v = jnp.dot(m_ref[...], wo_ref[...], preferred_element_type=F32)
        mix_ref[...] = mix_v
        h1 = h0_ref[...] + mix_v * _rms(mix_v) * gp_ref[...]
        h1_ref[...] = h1
        n2_ref[...] = (h1 * _rms(h1) * gq_ref[...]).astype(BF16)

    tile = pl.BlockSpec((tm, d), lambda i: (i, 0))
    gain = pl.BlockSpec((1, d), lambda i: (0, 0))
    return pl.pallas_call(
        body, name="mix_post", grid=(t // tm,),
        in_specs=[tile, pl.BlockSpec((d, d), lambda i: (0, 0)), tile, gain, gain] + [ANY] * len(deps),
        out_specs=[tile, tile, tile],
        out_shape=[jax.ShapeDtypeStruct((t, d), F32), jax.ShapeDtypeStruct((t, d), F32),
                   jax.ShapeDtypeStruct((t, d), BF16)],
        compiler_params=_params(1),
    )(m_mix, wo_full, h0, g_post, g_pre, *deps)


def _loss_head(fo, h1, tgt, g_post_mlp, t_real):
    t, d = h1.shape
    tile = _row_tile(t)

    def body(fo_ref, h1_ref, tgt_ref, g_ref, dfo_ref, dh2_ref, dg_ref, loss_ref, lacc):
        i = pl.program_id(0)
        fo_v = fo_ref[...]
        g = g_ref[...]
        r = _rms(fo_v)
        u = fo_v * r
        h2 = h1_ref[...] + u * g
        row = i * tile + lax.broadcasted_iota(jnp.int32, (tile, 1), 0)
        valid = jnp.logical_and(row >= N_META, row < t_real)
        diff = jnp.where(valid, h2 - tgt_ref[...], 0.0)
        dh2 = diff * (1.0 / d)
        dh2_ref[...] = dh2
        dfo, dg = _rms_bwd(dh2, u, r, g)
        dfo_ref[...] = dfo.astype(BF16)
        _accumulate(dg_ref, dg)
        _accumulate(lacc, _colsum8(diff * diff))

        @pl.when(i == pl.num_programs(0) - 1)
        def _():
            loss_ref[...] = jnp.full((SUB, LANE), (0.5 / d) * jnp.sum(lacc[...]), F32)

    return _row_call("loss_head", body, t, [fo, h1, tgt], [g_post_mlp],
                     [(d, BF16), (d, F32)], [(SUB, d), (SUB, LANE)], scratch=[pltpu.VMEM((SUB, d), F32)])


def _pre_norm_bwd(dn, h0, dh1, g_pre_mix, deps=()):
    t, d = h0.shape

    def body(dn_ref, h0_ref, dh1_ref, g_ref, dh0_ref, dg_ref):
        h0 = h0_ref[...]
        r = _rms(h0)
        dx, dg = _rms_bwd(dn_ref[...], h0 * r, r, g_ref[...])
        dh0_ref[...] = dh1_ref[...] + dx
        _accumulate(dg_ref, dg)

    return _row_call("pre_norm_bwd", body, t, [dn, h0, dh1], [g_pre_mix], [(d, F32)], [(SUB, d)], deps=deps)


def _layer_norm_silu(a1, ln_g, ln_b):
    t, c = a1.shape

    def body(a1_ref, g_ref, b_ref, a3_ref):
        a = a1_ref[...]
        mu = jnp.mean(a, axis=-1, keepdims=True)
        xc = a - mu
        rstd = lax.rsqrt(jnp.mean(xc * xc, axis=-1, keepdims=True) + LN_EPS)
        z = xc * rstd * g_ref[...] + b_ref[...]
        a3_ref[...] = (z * _sigmoid(z)).astype(BF16)

    return _row_call("layer_norm_silu", body, t, [a1], [ln_g, ln_b], [(c, BF16)], [])[0]


def _layer_norm_silu_bwd(da3, a1, ln_g, ln_b, deps=()):
    t, c = a1.shape

    def body(da3_ref, a1_ref, g_ref, b_ref, da1_ref, dg_ref, db_ref):
        a = a1_ref[...]
        g = g_ref[...]
        mu = jnp.mean(a, axis=-1, keepdims=True)
        xc = a - mu
        rstd = lax.rsqrt(jnp.mean(xc * xc, axis=-1, keepdims=True) + LN_EPS)
        xhat = xc * rstd
        z = xhat * g + b_ref[...]
        sg = _sigmoid(z)
        dz = da3_ref[...] * (sg * (1.0 + z * (1.0 - sg)))
        dxhat = dz * g
        da1_ref[...] = rstd * (dxhat - jnp.mean(dxhat, axis=-1, keepdims=True)
                               - xhat * jnp.mean(dxhat * xhat, axis=-1, keepdims=True))
        _accumulate(dg_ref, _colsum8(dz * xhat))
        _accumulate(db_ref, _colsum8(dz))

    return _row_call("layer_norm_silu_bwd", body, t, [da3, a1], [ln_g, ln_b], [(c, F32)], [(SUB, c), (SUB, c)], deps=deps)


def _branch_merge(a3, s, wpw, wso, proj, b_gates, d, deps=()):
    t, cols = proj.shape
    nblk, k, cb = wpw.shape
    w = 1024
    nh = d // w
    per = w // cb
    ga0 = (cols - 2 * d) // w
    tm = _row_tile(t)

    def body(a3_ref, s_ref, wpw_ref, wso_ref, *rest):
        pa_refs, pb_refs, bg_ref = rest[:nh], rest[nh:2 * nh], rest[2 * nh]
        ya_ref, yb_ref, ga_ref, gb_ref, m_ref = rest[2 * nh + 1 + len(deps):]
        a3v, sv = a3_ref[...], s_ref[...]
        for b in range(nblk):
            here = slice(b * cb, (b + 1) * cb)
            local = slice((b % per) * cb, (b % per + 1) * cb)
            ya = jnp.dot(a3v, wpw_ref[b], preferred_element_type=F32)
            yb = jnp.dot(sv, wso_ref[b], preferred_element_type=F32)
            ga = _sigmoid(pa_refs[b // per][:, local] + bg_ref[:, here])
            gb = _sigmoid(pb_refs[b // per][:, local] + bg_ref[:, d + b * cb:d + (b + 1) * cb])
            ya_ref[:, here] = ya.astype(BF16)
            yb_ref[:, here] = yb.astype(BF16)
            ga_ref[:, here] = ga.astype(BF16)
            gb_ref[:, here] = gb.astype(BF16)
            m_ref[:, here] = (ga * ya + gb * yb).astype(BF16)

    tile = pl.BlockSpec((tm, d), lambda i: (i, 0))
    return pl.pallas_call(
        body, name="branch_merge", grid=(t // tm,),
        in_specs=[pl.BlockSpec((tm, k), lambda i: (i, 0)), pl.BlockSpec((tm, k), lambda i: (i, 0)),
                  pl.BlockSpec((nblk, k, cb), lambda i: (0, 0, 0)), pl.BlockSpec((nblk, k, cb), lambda i: (0, 0, 0))]
                 + [pl.BlockSpec((tm, w), lambda i, h=h: (i, ga0 + h)) for h in range(2 * nh)]
                 + [pl.BlockSpec((1, 2 * d), lambda i: (0, 0))] + [ANY] * len(deps),
        out_specs=[tile] * 5,
        out_shape=[jax.ShapeDtypeStruct((t, d), BF16)] * 5,
        compiler_params=_params(1),
    )(a3, s, wpw, wso, *([proj] * (2 * nh)), b_gates, *deps)


def _mid_backward(dn2, h1, dh2, mix, g_pre_mlp, g_post_mix, wo_full, ga, gb, ya, yb, cols, deps=()):
    t, d = h1.shape
    w = 1024
    nh = d // w
    ga0 = (cols - 2 * d) // w
    tm = _row_tile(t)

    def body(dn2_ref, h1_ref, dh2_ref, mix_ref, gq_ref, gp_ref, wo_ref, ga_ref, gb_ref, ya_ref, yb_ref, *rest):
        (dh1_ref, dmix_ref, dya_ref, dyb_ref, dp_ref, dgq_ref, dgp_ref, dba_ref, dbb_ref,
         dmix_s, stage, sems) = rest[len(deps):]
        i, h = pl.program_id(0), pl.program_id(1)

        def add_to(ref, cols_, v, first):
            @pl.when(first)
            def _():
                ref[:, cols_] = v

            @pl.when(jnp.logical_not(first))
            def _():
                ref[:, cols_] += v

        @pl.when(h == 0)
        def _():
            h1 = h1_ref[...]
            r3 = _rms(h1)
            dx, dgq = _rms_bwd(dn2_ref[...], h1 * r3, r3, gq_ref[...])
            dh1 = dh2_ref[...] + dx
            dh1_ref[...] = dh1
            mix_v = mix_ref[...]
            r2 = _rms(mix_v)
            dmix, dgp = _rms_bwd(dh1, mix_v * r2, r2, gp_ref[...])
            dmix_s[...] = dmix.astype(BF16)
            dmix_ref[...] = dmix_s[...]
            add_to(dgq_ref, slice(None), dgq, i == 0)
            add_to(dgp_ref, slice(None), dgp, i == 0)

        dm = lax.dot_general(dmix_s[...], wo_ref[...], (((1,), (1,)), ((), ())), preferred_element_type=F32)
        ga = ga_ref[...].astype(F32)
        gb = gb_ref[...].astype(F32)
        dya_ref[...] = (dm * ga).astype(BF16)
        dyb_ref[...] = (dm * gb).astype(BF16)
        dpa = dm * ya_ref[...].astype(F32) * ga * (1.0 - ga)
        dpb = dm * yb_ref[...].astype(F32) * gb * (1.0 - gb)
        stage[0] = dpa.astype(BF16)
        stage[1] = dpb.astype(BF16)
        rows = pl.ds(pl.multiple_of(i * tm, tm), tm)
        copies = [pltpu.make_async_copy(
            stage.at[g], dp_ref.at[rows, pl.ds(pl.multiple_of((ga0 + g * nh + h) * w, w), w)], sems.at[g])
            for g in range(2)]
        for cp in copies:
            cp.start()
        for hh in range(nh):
            @pl.when(h == hh)
            def _(hh=hh):
                add_to(dba_ref, slice(hh * w, (hh + 1) * w), _colsum8(dpa), i == 0)
                add_to(dbb_ref, slice(hh * w, (hh + 1) * w), _colsum8(dpb), i == 0)
        for cp in copies:
            cp.wait()

    rows_f = pl.BlockSpec((tm, d), lambda i, h: (i, 0))
    half = pl.BlockSpec((tm, w), lambda i, h: (i, h))
    gain = pl.BlockSpec((1, d), lambda i, h: (0, 0))
    acc = pl.BlockSpec((SUB, d), lambda i, h: (0, 0))
    return pl.pallas_call(
        body, name="mid_backward", grid=(t // tm, nh),
        in_specs=[rows_f, rows_f, rows_f, rows_f, gain, gain, pl.BlockSpec((w, d), lambda i, h: (h, 0)),
                  half, half, half, half] + [ANY] * len(deps),
        out_specs=[rows_f, rows_f, half, half, ANY, acc, acc, acc, acc],
        out_shape=[jax.ShapeDtypeStruct((t, d), F32), jax.ShapeDtypeStruct((t, d), BF16),
                   jax.ShapeDtypeStruct((t, d), BF16), jax.ShapeDtypeStruct((t, d), BF16),
                   jax.ShapeDtypeStruct((t, cols), BF16)] + [jax.ShapeDtypeStruct((SUB, d), F32)] * 4,
        scratch_shapes=[pltpu.VMEM((tm, d), BF16), pltpu.VMEM((2, tm, w), BF16), pltpu.SemaphoreType.DMA((2,))],
        compiler_params=pltpu.CompilerParams(dimension_semantics=("arbitrary", "arbitrary"),
                                             vmem_limit_bytes=MID_VMEM_LIMIT),
    )(dn2, h1, dh2, mix, g_pre_mlp, g_post_mix, wo_full, ga, gb, ya, yb, *deps)


def _causal_conv(xp_ref, w_ref, ntap, r0):
    n = CONV_CHUNK + CONV_PAD
    win = xp_ref[pl.ds(r0, n), :]
    acc = None
    for k in range(ntap):
        back = ntap - 1 - k
        shifted = pltpu.roll(win, n - (CONV_PAD - back), 0)
        term = w_ref[k:k + 1, :] * shifted[:CONV_CHUNK]
        acc = term if acc is None else acc + term
    return acc


def _anticausal_conv(xp_ref, w_ref, ntap, r0):
    n = CONV_CHUNK + CONV_PAD
    win = xp_ref[pl.ds(pl.multiple_of(CONV_PAD + r0, CONV_PAD), n), :]
    acc = None
    for k in range(ntap):
        ahead = ntap - 1 - k
        shifted = win if ahead == 0 else pltpu.roll(win, n - ahead, 0)
        term = w_ref[k:k + 1, :] * shifted[:CONV_CHUNK]
        acc = term if acc is None else acc + term
    return acc


def _conv_weight_grad(dw_ref, d_chunk, xp_ref, ntap, r0):
    n = CONV_CHUNK + CONV_PAD
    win = xp_ref[pl.ds(r0, n), :]
    for k in range(ntap):
        back = ntap - 1 - k
        shifted = pltpu.roll(win, n - (CONV_PAD - back), 0)
        dw_ref[k * SUB:(k + 1) * SUB, :] += _colsum8(d_chunk * shifted[:CONV_CHUNK])


def _zero_pads(ref, t):
    ref[0:CONV_PAD, :] = jnp.zeros((CONV_PAD, LANE), F32)
    ref[CONV_PAD + t:CONV_PAD + t + CONV_PAD, :] = jnp.zeros((CONV_PAD, LANE), F32)


def _for_chunks(t, fn):
    def step(idx, carry):
        fn(pl.multiple_of(idx * CONV_CHUNK, CONV_CHUNK))
        return carry

    lax.fori_loop(0, t // CONV_CHUNK, step, 0)


def _conv_forward(proj, conf_w, conf_b, short_w, dc, deps=()):
    t = proj.shape[0]
    nc = dc // LANE

    def body(av_ref, ag_ref, bg_ref, cg_ref, v_ref, cw_ref, cb_ref, sw_ref, *rest):
        a1_ref, s_ref, xa, xb = rest[len(deps):]
        _zero_pads(xa, t)
        _zero_pads(xb, t)
        xa[CONV_PAD:CONV_PAD + t, :] = av_ref[...] * _sigmoid(ag_ref[...])
        xb[CONV_PAD:CONV_PAD + t, :] = cg_ref[...] * v_ref[...]

        def chunk(r0):
            rs = pl.ds(r0, CONV_CHUNK)
            a1_ref[rs, :] = _causal_conv(xa, cw_ref, CONF_K, r0) + cb_ref[...]
            s_ref[rs, :] = (bg_ref[rs, :] * _causal_conv(xb, sw_ref, SHORT_K, r0)).astype(BF16)

        _for_chunks(t, chunk)

    col = lambda g: pl.BlockSpec((t, LANE), lambda c, g=g: (0, g * nc + c))
    return pl.pallas_call(
        body, name="conv_forward", grid=(nc,),
        in_specs=[col(0), col(1), col(2), col(3), col(4),
                  pl.BlockSpec((CONF_K, LANE), lambda c: (0, c)),
                  pl.BlockSpec((1, LANE), lambda c: (0, c)),
                  pl.BlockSpec((SHORT_K, LANE), lambda c: (0, c))] + [ANY] * len(deps),
        out_specs=[pl.BlockSpec((t, LANE), lambda c: (0, c)), pl.BlockSpec((t, LANE), lambda c: (0, c))],
        out_shape=[jax.ShapeDtypeStruct((t, dc), F32), jax.ShapeDtypeStruct((t, dc), BF16)],
        scratch_shapes=[pltpu.VMEM((t + 2 * CONV_PAD, LANE), F32), pltpu.VMEM((t + 2 * CONV_PAD, LANE), F32)],
        compiler_params=_params(1),
    )(proj, proj, proj, proj, proj, conf_w, conf_b, short_w, *deps)


def _conv_backward(dproj, proj, da1, ds, conf_w, short_w, dc):
    t = proj.shape[0]
    nc = dc // LANE

    def body(dp_in, av_ref, ag_ref, bg_ref, cg_ref, v_ref, da1_ref, ds_ref, cw_ref, sw_ref,
             dp_ref, dcw_ref, dcb_ref, dsw_ref, xa, xb, da, db, stage, sems):
        del dp_in
        c = pl.program_id(0)
        for ref in (xa, xb, da, db):
            _zero_pads(ref, t)
        xa[CONV_PAD:CONV_PAD + t, :] = av_ref[...] * _sigmoid(ag_ref[...])
        xb[CONV_PAD:CONV_PAD + t, :] = cg_ref[...] * v_ref[...]
        da[CONV_PAD:CONV_PAD + t, :] = da1_ref[...]
        dcw_ref[...] = jnp.zeros(dcw_ref.shape, F32)
        dsw_ref[...] = jnp.zeros(dsw_ref.shape, F32)
        dcb_ref[...] = jnp.zeros(dcb_ref.shape, F32)

        def through_gate(r0):
            rs = pl.ds(r0, CONV_CHUNK)
            ds_c = ds_ref[rs, :]
            stage[2, rs, :] = (ds_c * _causal_conv(xb, sw_ref, SHORT_K, r0)).astype(BF16)
            db[pl.ds(pl.multiple_of(CONV_PAD + r0, CONV_PAD), CONV_CHUNK), :] = ds_c * bg_ref[rs, :]

        _for_chunks(t, through_gate)

        def through_convs(r0):
            rs = pl.ds(r0, CONV_CHUNK)
            da0 = _anticausal_conv(da, cw_ref, CONF_K, r0)
            sg = _sigmoid(ag_ref[rs, :])
            stage[0, rs, :] = (da0 * sg).astype(BF16)
            stage[1, rs, :] = (da0 * av_ref[rs, :] * sg * (1.0 - sg)).astype(BF16)
            dcv = _anticausal_conv(db, sw_ref, SHORT_K, r0)
            stage[3, rs, :] = (dcv * v_ref[rs, :]).astype(BF16)
            stage[4, rs, :] = (dcv * cg_ref[rs, :]).astype(BF16)
            da1_c = da1_ref[rs, :]
            _conv_weight_grad(dcw_ref, da1_c, xa, CONF_K, r0)
            _conv_weight_grad(dsw_ref, ds_ref[rs, :] * bg_ref[rs, :], xb, SHORT_K, r0)
            dcb_ref[...] += _colsum8(da1_c)

        _for_chunks(t, through_convs)
        copies = [pltpu.make_async_copy(
            stage.at[g], dp_ref.at[:, pl.ds(pl.multiple_of((g * nc + c) * LANE, LANE), LANE)], sems.at[g])
            for g in range(5)]
        for cp in copies:
            cp.start()
        for cp in copies:
            cp.wait()

    col = lambda g: pl.BlockSpec((t, LANE), lambda c, g=g: (0, g * nc + c))
    blk = pl.BlockSpec((t, LANE), lambda c: (0, c))
    return pl.pallas_call(
        body, name="conv_backward", grid=(nc,),
        in_specs=[ANY, col(0), col(1), col(2), col(3), col(4), blk, blk,
                  pl.BlockSpec((CONF_K, LANE), lambda c: (0, c)),
                  pl.BlockSpec((SHORT_K, LANE), lambda c: (0, c))],
        out_specs=[ANY,
                   pl.BlockSpec((CONF_K * SUB, LANE), lambda c: (0, c)),
                   pl.BlockSpec((SUB, LANE), lambda c: (0, c)),
                   pl.BlockSpec((SHORT_K * SUB, LANE), lambda c: (0, c))],
        out_shape=[jax.ShapeDtypeStruct(dproj.shape, dproj.dtype),
                   jax.ShapeDtypeStruct((CONF_K * SUB, dc), F32),
                   jax.ShapeDtypeStruct((SUB, dc), F32),
                   jax.ShapeDtypeStruct((SHORT_K * SUB, dc), F32)],
        scratch_shapes=[pltpu.VMEM((t + 2 * CONV_PAD, LANE), F32)] * 4
                       + [pltpu.VMEM((5, t, LANE), BF16), pltpu.SemaphoreType.DMA((5,))],
        input_output_aliases={0: 0},
        compiler_params=_params(1),
    )(dproj, proj, proj, proj, proj, proj, da1, ds, conf_w, short_w)


def _adamw_math(w, g, m, v):
    m = ADAM_B1 * m + (1.0 - ADAM_B1) * g
    v = ADAM_B2 * v + (1.0 - ADAM_B2) * (g * g)
    m_hat = m / (1.0 - ADAM_B1 ** ADAM_STEP)
    v_hat = v / (1.0 - ADAM_B2 ** ADAM_STEP)
    delta = -ADAM_LR * (m_hat / (jnp.sqrt(v_hat) + ADAM_EPS) + ADAM_WD * w)
    return delta, m, v


def _cast_into_slot(name, w, me_arr, deps=()):
    r, c = w.shape
    tr = 256

    def body(me_ref, w_ref, *rest):
        del me_ref
        rest[-1][0] = w_ref[...].astype(BF16)

    return pl.pallas_call(
        body, name=name,
        grid_spec=pltpu.PrefetchScalarGridSpec(
            num_scalar_prefetch=1, grid=(r // tr,),
            in_specs=[pl.BlockSpec((tr, c), lambda i, me: (i, 0))] + [ANY] * len(deps),
            out_specs=pl.BlockSpec((1, tr, c), lambda i, me: (me[0], i, 0))),
        out_shape=jax.ShapeDtypeStruct((N_DEV, r, c), BF16),
        compiler_params=_params(1),
    )(me_arr, w, *deps)


def _chip_sum(name, full, from_sibling, me_arr):
    _, r, c = full.shape
    tr = min(r, 512)

    def body(me_ref, full_ref, sib_ref, sums_ref):
        del me_ref
        sums_ref[0] = (full_ref[0].astype(F32) + sib_ref[0].astype(F32)).astype(BF16)

    other = lambda k, me: (me[0] // 2 + 1 + k) % 4
    return pl.pallas_call(
        body, name=name,
        grid_spec=pltpu.PrefetchScalarGridSpec(
            num_scalar_prefetch=1, grid=(r // tr, 3),
            in_specs=[pl.BlockSpec((1, tr, c), lambda i, k, me: (2 * other(k, me) + me[0] % 2, i, 0)),
                      pl.BlockSpec((1, tr, c), lambda i, k, me: (other(k, me), i, 0))],
            out_specs=pl.BlockSpec((1, tr, c), lambda i, k, me: (other(k, me), i, 0))),
        out_shape=jax.ShapeDtypeStruct((4, r, c), BF16),
        compiler_params=_params(2),
    )(me_arr, full, from_sibling)


def _adamw_shard(name, w, m, v, parts, me_arr, deps=()):
    r, c = w.shape
    tr = min(256, r // len(parts))
    np_ = len(parts)
    per = r // np_ // tr

    def body(me_ref, w_ref, m_ref, v_ref, *rest):
        g_out, d_out, m_out, v_out = rest[5 * np_ + len(deps):]
        g = None
        for p in range(np_):
            gp = rest[5 * p][...]
            for l_ref in rest[5 * p + 1:5 * p + 5]:
                gp = gp + l_ref[0].astype(F32)
            g = gp if g is None else jnp.where(pl.program_id(0) // per == p, gp, g)
        delta, m_new, v_new = _adamw_math(w_ref[...], g, m_ref[...], v_ref[...])
        g_out[...] = g
        d_out[...] = delta
        m_out[...] = m_new
        v_out[...] = v_new

    tile = pl.BlockSpec((tr, c), lambda i, me: (i, 0))
    part_specs, part_args = [], []
    for p, (g_own, from_sibling, landed) in enumerate(parts):
        row = lambda i, p=p: jnp.clip(i - p * per, 0, per - 1)
        part_specs.append(pl.BlockSpec((tr, c), lambda i, me, row=row: (row(i), 0)))
        part_specs += [pl.BlockSpec((1, tr, c), lambda i, me, k=k, row=row: ((me[0] // 2 + k) % 4, row(i), 0))
                       for k in range(4)]
        part_args += [g_own, from_sibling, landed, landed, landed]
    return pl.pallas_call(
        body, name=name,
        grid_spec=pltpu.PrefetchScalarGridSpec(
            num_scalar_prefetch=1, grid=(r // tr,),
            in_specs=[tile] * 3 + part_specs + [ANY] * len(deps), out_specs=[tile] * 4),
        out_shape=[jax.ShapeDtypeStruct((r, c), F32)] * 4,
        compiler_params=_params(1),
    )(me_arr, w, m, v, *part_args, *deps)


SMALL_W = 1024
VEC_ROWS = 16
LOSS_ROW = 15
META_ROW0 = 16
CONF_ROW0 = 64
SHORT_ROW0 = 96
SMALL_ROWS = 104


def _pack_small(vec_parts, dmeta, dcw, dsw, loss_blk, me_arr):
    widths = [p.shape[1] for p in vec_parts]
    nv = len(vec_parts)

    def body(me_ref, *refs):
        del me_ref
        parts, (dmeta_ref, dcw_ref, dsw_ref, loss_ref, out_ref) = refs[:nv], refs[nv:]
        out_ref[0] = jnp.zeros((SMALL_ROWS, SMALL_W), F32)
        out_ref[0, LOSS_ROW:LOSS_ROW + 1, 0:LANE] = loss_ref[0:1, :]
        row = 0
        for p_ref, wd in zip(parts, widths):
            s = jnp.sum(p_ref[...], axis=0, keepdims=True)
            for h in range(wd // SMALL_W):
                out_ref[0, row:row + 1, :] = s[:, h * SMALL_W:(h + 1) * SMALL_W]
                row += 1
        for h in range(dmeta_ref.shape[1] // SMALL_W):
            out_ref[0, META_ROW0 + h * N_META:META_ROW0 + (h + 1) * N_META, :] = dmeta_ref[:, h * SMALL_W:(h + 1) * SMALL_W]
        for k in range(CONF_K):
            out_ref[0, CONF_ROW0 + k:CONF_ROW0 + k + 1, :] = jnp.sum(dcw_ref[k * SUB:(k + 1) * SUB, :], axis=0, keepdims=True)
        for k in range(SHORT_K):
            out_ref[0, SHORT_ROW0 + k:SHORT_ROW0 + k + 1, :] = jnp.sum(dsw_ref[k * SUB:(k + 1) * SUB, :], axis=0, keepdims=True)

    ins = [*vec_parts, dmeta, dcw, dsw, loss_blk]
    return pl.pallas_call(
        body, name="pack_small",
        grid_spec=pltpu.PrefetchScalarGridSpec(
            num_scalar_prefetch=1, grid=(1,),
            in_specs=[pl.BlockSpec(a.shape, lambda i, me: (0, 0)) for a in ins],
            out_specs=pl.BlockSpec((1, SMALL_ROWS, SMALL_W), lambda i, me: (me[0], 0, 0))),
        out_shape=jax.ShapeDtypeStruct((N_DEV, SMALL_ROWS, SMALL_W), F32),
        compiler_params=_params(1),
    )(me_arr, *ins)


def _small_update(gathered, me_arr, vec_params, meta_p, conf_p, short_p):
    widths = [p[0].shape[1] for p in vec_params]
    nv = len(vec_params)
    mcols = meta_p[0].shape[1]
    per_row = SMALL_W // mcols

    def body(me_ref, gv_ref, gm_ref, gc_ref, gs_ref, *rest):
        del me_ref
        ins, outs = rest[:3 * (nv + 3)], rest[3 * (nv + 3):]

        def total(ref, r0, rows):
            s = ref[0, r0:r0 + rows, :]
            for dev in range(1, N_DEV):
                s = s + ref[dev, r0:r0 + rows, :]
            return s

        grads = []
        row = 0
        for wd in widths:
            pieces = [total(gv_ref, row + h, 1) for h in range(wd // SMALL_W)]
            grads.append(pieces[0] if len(pieces) == 1 else jnp.concatenate(pieces, axis=1))
            row += len(pieces)
        grads.append(total(gm_ref, 0, N_META))
        grads.append(total(gc_ref, 0, CONF_K))
        grads.append(total(gs_ref, 0, SHORT_K))
        loss = gv_ref[0, LOSS_ROW:LOSS_ROW + 1, 0:LANE]
        for dev in range(1, N_DEV):
            loss = loss + gv_ref[dev, LOSS_ROW:LOSS_ROW + 1, 0:LANE]
        outs[-1][...] = loss
        for idx, g in enumerate(grads):
            w_ref, m_ref, v_ref = ins[3 * idx:3 * idx + 3]
            delta, m_new, v_new = _adamw_math(w_ref[...], g, m_ref[...], v_ref[...])
            g_out, d_out, m_out, v_out = outs[4 * idx:4 * idx + 4]
            g_out[...] = g
            d_out[...] = delta
            m_out[...] = m_new
            v_out[...] = v_new

    params = list(vec_params) + [meta_p, conf_p, short_p]
    flat = [a for p in params for a in p]
    whole = lambda a: pl.BlockSpec(a.shape, lambda i, me: (0,) * a.ndim)
    outs = pl.pallas_call(
        body, name="small_update",
        grid_spec=pltpu.PrefetchScalarGridSpec(
            num_scalar_prefetch=1, grid=(1,),
            in_specs=[pl.BlockSpec((N_DEV, VEC_ROWS, SMALL_W), lambda i, me: (0, 0, 0)),
                      pl.BlockSpec((N_DEV, N_META, mcols),
                                   lambda i, me: (0, META_ROW0 // N_META + me[0] // per_row, me[0] % per_row)),
                      pl.BlockSpec((N_DEV, 32, LANE), lambda i, me: (0, CONF_ROW0 // 32, me[0])),
                      pl.BlockSpec((N_DEV, SUB, LANE), lambda i, me: (0, SHORT_ROW0 // SUB, me[0]))]
                     + [whole(a) for a in flat],
            out_specs=[whole(p[0]) for p in params for _ in range(4)]
                      + [pl.BlockSpec((1, LANE), lambda i, me: (0, 0))]),
        out_shape=[jax.ShapeDtypeStruct(p[0].shape, F32) for p in params for _ in range(4)]
                  + [jax.ShapeDtypeStruct((1, LANE), F32)],
        compiler_params=_params(1),
    )(me_arr, gathered, gathered, gathered, gathered, *flat)
    return [tuple(outs[4 * i:4 * i + 4]) for i in range(len(params))], outs[-1][0, 0]


def kernel(x, meta, g_pre_mix, w_in, b_gates, conf_dw_w, conf_dw_b, conf_ln_g, conf_ln_b, conf_w_pw, short_dw_w, short_w_out, w_o, g_post_mix, g_pre_mlp, w_up, w_down, g_post_mlp, loss_target, m_meta, m_g_pre_mix, m_w_in, m_b_gates, m_conf_dw_w, m_conf_dw_b, m_conf_ln_g, m_conf_ln_b, m_conf_w_pw, m_short_dw_w, m_short_w_out, m_w_o, m_g_post_mix, m_g_pre_mlp, m_w_up, m_w_down, m_g_post_mlp, v_meta, v_g_pre_mix, v_w_in, v_b_gates, v_conf_dw_w, v_conf_dw_b, v_conf_ln_g, v_conf_ln_b, v_conf_w_pw, v_short_dw_w, v_short_w_out, v_w_o, v_g_post_mix, v_g_pre_mlp, v_w_up, v_w_down, v_g_post_mlp):
    seq, d = x.shape[1], x.shape[2]
    dc = conf_w_pw.shape[1]
    t_real = N_META + seq
    t = -(-t_real // ROW_TILE) * ROW_TILE
    tm = t // 2
    assert tm % 16 == 0 and d % 1024 == 0 and dc % 1024 == 0
    x_idx, y_idx, c_idx = _position()
    me_arr = jnp.reshape(4 * x_idx + 2 * y_idx + c_idx, (1,)).astype(jnp.int32)

    big = [w_in[0], conf_w_pw[0], short_w_out[0], w_o[0], w_up[0], w_down[0]]
    big_names = ["w_in", "conf_w_pw", "short_w_out", "w_o", "w_up", "w_down"]
    groups = [[0], [1, 2, 3], [4], [5]]
    slots, deps = [], []
    for g, idxs in enumerate(groups):
        slots.append([_cast_into_slot("cast_" + big_names[i], big[i], me_arr, deps=deps) for i in idxs])
        if g == 0:
            direct0 = _remote_start("gather0_direct_start", "gather_direct", slots[0])
            deps = [direct0[3]]
    casts = [sl for group in slots[1:] for sl in group]
    meta_g, cw_g, sw_g = _all_gather("gather_small_params", [meta, conf_dw_w[0], short_dw_w[0]], deps=casts)

    def start_direct(g, deps):
        send, recv, bufs, tok = _remote_start("gather%d_direct_start" % g, "gather_direct", slots[g], deps=deps)
        return (send, recv, bufs), tok

    def relay(g, state, after):
        send, recv, bufs = state
        bufs = _remote_wait("gather%d_direct_wait" % g, "gather_direct", send, recv, bufs, len(bufs), after)
        send, recv, bufs, tok = _remote_start("gather%d_relay_start" % g, "gather_relay", bufs)
        return (send, recv, bufs), tok

    def gathered(g, state, after):
        send, recv, bufs = state
        bufs = _remote_wait("gather%d_relay_wait" % g, "gather_relay", send, recv, bufs, len(bufs), after)
        send, recv, bufs, tok = _remote_start("gather%d_diag_start" % g, "gather_diag", bufs)
        return _remote_wait("gather%d_diag_wait" % g, "gather_diag", send, recv, bufs, len(bufs), [tok])

    unshard =lambda g: jnp.transpose(g, (1, 0, 2)).reshape(g.shape[1], -1)
    meta_full, cw_full, sw_full = unshard(meta_g), unshard(cw_g), unshard(sw_g)

    relay0, tok = relay(0, direct0[:3], [meta_g])
    zrows = jnp.zeros((t - t_real, d), F32) + tok[0, 0] * 0.0
    h0 = jnp.concatenate([meta_full, x[0], zrows], axis=0)
    tgt = jnp.concatenate([jnp.zeros((N_META, d), F32), loss_target[0], zrows], axis=0)
    n = _pre_norm(h0, g_pre_mix)
    direct1, tok = start_direct(1, [tok])
    direct2, tok = start_direct(2, [tok])
    win_g, = gathered(0, relay0, [tok, n])
    proj = _mm_cols("proj", n, win_g, tm=tm)[0]
    relay1, tok = relay(1, direct1, [proj])
    direct3, tok = start_direct(3, [tok])
    a1, s = _conv_forward(proj, cw_full, conf_dw_b, sw_full, dc, deps=[tok])
    a3 = _layer_norm_silu(a1, conf_ln_g, conf_ln_b)
    wpw_g, wso_g, wo_g = gathered(1, relay1, [a3])
    wo_full = wo_g.reshape(d, d)
    ya, yb, gate_a, gate_b, m_mix = _branch_merge(a3, s, wpw_g, wso_g, proj, b_gates, d)
    relay2, tok = relay(2, direct2, [m_mix])
    relay3, tok = relay(3, direct3, [tok])
    mix, h1, n2 = _mix_post(m_mix, wo_full, h0, g_post_mix, g_pre_mlp, deps=[tok])
    wup_g, = gathered(2, relay2, [n2])

    def up_epilogue(acc):
        r = jnp.maximum(acc, 0.0)
        return r * r, r

    f, relu_up = _mm_cols("mlp_up", n2, wup_g, tm=tm, epilogue=up_epilogue, out_dtypes=(BF16, BF16))
    wdn_g, = gathered(3, relay3, [f])
    wdn_full = wdn_g.reshape(-1, d)
    fo = _mm_rows("mlp_down", f, wdn_full, tm=tm // 2, tn=512)
    dfo, dh2, dg_post_mlp, loss_blk = _loss_head(fo, h1, tgt, g_post_mlp, t_real)

    def reduce_start(tag, fulls, deps):
        lands = [lax.empty((4,) + g.shape[1:], BF16) for g in fulls]
        send, recv, bufs, tok = _remote_start("reduce_%s_d2d_start" % tag, "reduce_d2d", fulls, lands, deps=deps)
        return (send, recv, bufs), tok

    def reduce_middle(tag, state, owns, after):
        send, recv, bufs = state
        k = len(owns)
        bufs = _remote_wait("reduce_%s_d2d_wait" % tag, "reduce_d2d", send, recv, bufs, k, after)
        from_sibling = bufs[k:]
        sums = [_chip_sum("chip_sum_%s%d" % (tag, i), bufs[i], from_sibling[i], me_arr) for i in range(k)]
        lands = [lax.empty(sm.shape, BF16) for sm in sums]
        send, recv, bufs, tok = _remote_start("reduce_%s_ici_start" % tag, "reduce_ici", sums, lands)
        return (send, recv, bufs, list(zip(owns, from_sibling))), tok

    def reduce_finish(tag, state, after):
        send, recv, bufs, local = state
        k = len(local)
        bufs = _remote_wait("reduce_%s_ici_wait" % tag, "reduce_ici", send, recv, bufs, k, after)
        return [(own, sib, landed) for (own, sib), landed in zip(local, bufs[k:])]

    dup = _mm_nt_blocks("d_up", dfo, wdn_full, tm=tm, tkb=1024, extra=(relu_up,),
                        epilogue=lambda acc, r: (acc * (2.0 * r.astype(F32)),), out_dtypes=(BF16,))[0]
    gw_down, gw_down_own = _mm_tn("dw_down", f, dfo, me_arr, m=f.shape[1], n=d, tma=512, tn=d, sharded="rows")
    red_down, tok = reduce_start("down", [gw_down], ())
    dn2 = _mm_nt_acc("d_n2", dup, wup_g, tm=tm // 2, tn=512, deps=[tok])
    gw_up, gw_up_own = _mm_tn("dw_up", n2, dup, me_arr, m=d, n=dup.shape[1], tma=512, tn=1024, sharded="cols")
    red_down, tok = reduce_middle("down", red_down, [gw_down_own], [dn2])
    red_up, tok = reduce_start("up", [gw_up], [tok])
    dh1, dmix, dya, dyb, dproj, dg_pre_mlp, dg_post_mix, db_a, db_b = _mid_backward(
        dn2, h1, dh2, mix, g_pre_mlp, g_post_mix, wo_full, gate_a, gate_b, ya, yb, proj.shape[1], deps=[tok])
    db_gates = jnp.concatenate([db_a, db_b], axis=1)
    red_up, tok = reduce_middle("up", red_up, [gw_up_own], [dya])
    gw_o, gw_o_own = _mm_tn("dw_o", m_mix, dmix, me_arr, m=d, n=d, tma=d // N_DEV, tn=d, sharded="rows", deps=[tok])
    da3 = _mm_nt_acc("d_a3", dya, wpw_g, tm=tm, tn=512)
    gw_pw, gw_pw_own = _mm_tn("dw_pw", a3, dya, me_arr, m=dc, n=d, tma=512, tn=d, sharded="cols")
    dsb = _mm_nt_acc("d_s", dyb, wso_g, tm=tm, tn=512)
    gw_so, gw_so_own = _mm_tn("dw_so", s, dyb, me_arr, m=dc, n=d, tma=512, tn=d, sharded="cols")
    red_mix, tok = reduce_start("mix", [gw_pw, gw_so, gw_o], ())
    da1, dln_g, dln_b = _layer_norm_silu_bwd(da3, a1, conf_ln_g, conf_ln_b, deps=[tok])
    dproj, dcw, dcb, dsw = _conv_backward(dproj, proj, da1, dsb, cw_full, sw_full, dc)
    red_mix, tok = reduce_middle("mix", red_mix, [gw_pw_own, gw_so_own, gw_o_own], [dcb])
    in_cb = w_in.shape[2]
    half = d // 2
    red_in = []
    for part in range(2):
        gw, own = _mm_tn("dw_in%d" % part, n, dproj, me_arr, m=half, n=proj.shape[1], tma=512, tn=2 * in_cb,
                         sharded="cols", a_off=part * (half // 512), deps=[tok])
        state, tok = reduce_start("in%d" % part, [gw], ())
        red_in.append((state, own))
    for part in range(2):
        state, own = red_in[part]
        red_in[part], tok = reduce_middle("in%d" % part, state, [own], [tok])
    dn = _mm_nt_acc("d_n", dproj, win_g, tm=tm // 2, tn=512, deps=[tok])
    dh0, dg_pre_mix = _pre_norm_bwd(dn, h0, dh1, g_pre_mix)
    grad_x = dh0[N_META:t_real][None]

    vec_parts = [dg_pre_mix, db_gates, dcb, dln_g, dln_b, dg_post_mix, dg_pre_mlp, dg_post_mlp]
    packed = _pack_small(vec_parts, dh0[:N_META], dcw, dsw, loss_blk, me_arr)
    send, recv, bufs, tok = _remote_start("small_grads_ici_start", "gather_ici", [packed])
    vec_names = ["g_pre_mix", "b_gates", "conf_dw_b", "conf_ln_g", "conf_ln_b", "g_post_mix", "g_pre_mlp", "g_post_mlp"]
    env = locals()
    results = {}

    def update(nm, parts, deps=()):
        res = _adamw_shard("adamw_" + nm, env[nm][0], env["m_" + nm][0], env["v_" + nm][0], parts, me_arr, deps=deps)
        results[nm] = tuple(r[None] for r in res)
        return res[0]

    done = [update("w_down", reduce_finish("down", red_down, [tok]), deps=[tok])]
    done.append(update("w_up", reduce_finish("up", red_up, done)))
    bufs = _remote_wait("small_grads_ici_wait", "gather_ici", send, recv, bufs, 1, done)
    send, recv, bufs, tok = _remote_start("small_grads_d2d_start", "gather_d2d", bufs)
    for nm, pair in zip(["conf_w_pw", "short_w_out", "w_o"], reduce_finish("mix", red_mix, [tok])):
        done.append(update(nm, [pair], deps=[tok]))
    small_g, = _remote_wait("small_grads_d2d_wait", "gather_d2d", send, recv, bufs, 1, done)
    triple = lambda nm, sq: tuple(env[p + nm][0] if sq else env[p + nm] for p in ("", "m_", "v_"))
    small, loss = _small_update(small_g, me_arr, [triple(nm, False) for nm in vec_names],
                                triple("meta", False), triple("conf_dw_w", True), triple("short_dw_w", True))
    for nm, res in zip(vec_names + ["meta"], small[:len(vec_names) + 1]):
        results[nm] = res
    results["conf_dw_w"] = tuple(r[None] for r in small[-2])
    results["short_dw_w"] = tuple(r[None] for r in small[-1])
    update("w_in", [reduce_finish("in%d" % part, red_in[part], [small[0][0]])[0] for part in range(2)])

    order = ["meta", "g_pre_mix", "w_in", "b_gates", "conf_dw_w", "conf_dw_b", "conf_ln_g", "conf_ln_b", "conf_w_pw",
             "short_dw_w", "short_w_out", "w_o", "g_post_mix", "g_pre_mlp", "w_up", "w_down", "g_post_mlp"]
    return (loss, grad_x, *[results[nm][0] for nm in order], *[results[nm][1] for nm in order],
            *[results[nm][2] for nm in order], *[results[nm][3] for nm in order])
```

```python
import jax
import jax.numpy as jnp
from jax import lax
from jax.experimental import pallas as pl
from jax.experimental.pallas import tpu as pltpu

N_DEV = 8
N_META = 16
CONF_K = 31
SHORT_K = 3
RMS_EPS = 1e-6
LN_EPS = 1e-5
ADAM_LR = 0.001
ADAM_B1 = 0.9
ADAM_B2 = 0.999
ADAM_EPS = 1e-08
ADAM_WD = 0.01
ADAM_STEP = 10

LANE = 128
SUB = 8
ROW_TILE = 128
CONV_PAD = 32
CONV_CHUNK = 128
VMEM_LIMIT = 56 * 1024 * 1024

F32 = jnp.float32
BF16 = jnp.bfloat16
MESH = pl.DeviceIdType.MESH
ANY = pl.BlockSpec(memory_space=pl.ANY)
HBM_SPEC = pl.BlockSpec(memory_space=pltpu.HBM)
SEM_SPEC = pl.BlockSpec(memory_space=pltpu.SEMAPHORE)
EFFECT = pltpu.SideEffectType.DATAFLOW_SIDE_EFFECTING


def _params(n_axes):
    return pltpu.CompilerParams(dimension_semantics=("arbitrary",) * n_axes, vmem_limit_bytes=VMEM_LIMIT)


def _sigmoid(z):
    return 1.0 / (1.0 + jnp.exp(-z))


def _colsum8(v):
    r, c = v.shape
    return jnp.sum(v.reshape(r // SUB, SUB, c), axis=0)


def _position():
    x, y, c = lax.axis_index("x"), lax.axis_index("y"), lax.axis_index("c")
    return x, y, c


def _flat(p):
    return 4 * p[0] + 2 * p[1] + p[2]


def _all_gather(name, shards, deps=()):
    n, nd = len(shards), len(deps)

    def body(*refs):
        ins, outs = refs[:n], refs[n + nd:2 * n + nd]
        send_sems, recv_sems, local_sems = refs[2 * n + nd:]
        x, y, c = _position()
        me, sibling = (x, y, c), (x, y, 1 - c)
        chips = [(1 - x, y), (x, 1 - y), (1 - x, 1 - y)]

        def copy(q, k, block, to, src=None):
            dst = outs[q].at[_flat(block)]
            return pltpu.make_async_remote_copy(
                src_ref=dst if src is None else src, dst_ref=dst,
                send_sem=send_sems.at[q, k], recv_sem=recv_sems.at[q, k],
                device_id=to, device_id_type=MESH)

        mine = [pltpu.make_async_copy(ins[q], outs[q].at[_flat(me)], local_sems.at[q]) for q in range(n)]
        for cp in mine:
            cp.start()
        first = []
        for q in range(n):
            first.append(copy(q, 0, me, sibling, src=ins[q]))
            for j, chip in enumerate(chips):
                first.append(copy(q, 1 + j, me, (*chip, c), src=ins[q]))
        for cp in first:
            cp.start()
        passed = []
        for q in range(n):
            for j, chip in enumerate(chips):
                copy(q, 1 + j, (*chip, c), me).wait_recv()
                fwd = copy(q, 4 + j, (*chip, c), sibling)
                fwd.start()
                passed.append(fwd)
        for q in range(n):
            copy(q, 0, sibling, me).wait_recv()
            for j, chip in enumerate(chips):
                copy(q, 4 + j, (*chip, 1 - c), me).wait_recv()
        for cp in first + passed:
            cp.wait_send()
        for cp in mine:
            cp.wait()

    return pl.pallas_call(
        body, name=name,
        in_specs=[ANY] * (n + nd), out_specs=[ANY] * n,
        out_shape=[jax.ShapeDtypeStruct((N_DEV,) + s.shape, s.dtype) for s in shards],
        scratch_shapes=[pltpu.SemaphoreType.DMA((n, 7)), pltpu.SemaphoreType.DMA((n, 7)),
                        pltpu.SemaphoreType.DMA((n,))],
    )(*shards, *deps)


N_COPIES = {"gather_ici": 4, "gather_d2d": 3, "gather_direct": 3, "gather_relay": 3, "gather_diag": 1,
            "reduce_d2d": 4, "reduce_ici": 3}


def _copy_plan(kind):
    x, y, c = _position()
    me, sibling = (x, y, c), (x, y, 1 - c)
    chips = [(1 - x, y), (x, 1 - y), (1 - x, 1 - y)]
    if kind == "gather_ici":
        return [(_flat(me), _flat(me), sibling)] + [(_flat(me), _flat(me), (*ch, c)) for ch in chips]
    if kind == "gather_d2d":
        return [(_flat((*ch, c)), _flat((*ch, c)), sibling) for ch in chips]
    if kind == "gather_direct":
        return [(_flat(me), _flat(me), sibling)] + [(_flat(me), _flat(me), (*ch, c)) for ch in chips[:2]]
    if kind == "gather_relay":
        held, to = (x ^ (1 - c), y ^ c, c), (x ^ c, y ^ (1 - c), c)
        return [(_flat(held), _flat(held), to)] + [(_flat((*ch, c)), _flat((*ch, c)), sibling) for ch in chips[:2]]
    if kind == "gather_diag":
        return [(_flat((*chips[2], c)), _flat((*chips[2], c)), sibling)]
    if kind == "reduce_d2d":
        return [(2 * chip + (1 - c), chip, sibling) for chip in range(4)]
    return [(2 * ch[0] + ch[1], 2 * x + y, (*ch, c)) for ch in chips]


def _planned_copies(kind, srcs, dsts, send_sems, recv_sems):
    plan = _copy_plan(kind)
    return [pltpu.make_async_remote_copy(
        src_ref=src.at[s_slot], dst_ref=dst.at[d_slot],
        send_sem=send_sems.at[q * len(plan) + k], recv_sem=recv_sems.at[q * len(plan) + k],
        device_id=to, device_id_type=MESH)
        for q, (src, dst) in enumerate(zip(srcs, dsts)) for k, (s_slot, d_slot, to) in enumerate(plan)]


def _remote_start(name, kind, srcs, lands=None, deps=()):
    n = len(srcs)
    bufs = list(srcs) + ([] if lands is None else list(lands))
    nb, nd = len(bufs), len(deps)
    nsem = n * N_COPIES[kind]

    def body(*refs):
        ins = refs[:nb]
        send_sems, recv_sems = refs[nb + nd], refs[nb + nd + 1]
        token = refs[-1]
        for cp in _planned_copies(kind, ins[:n], ins[:n] if lands is None else ins[n:], send_sems, recv_sems):
            cp.start()
        token[...] = jnp.zeros_like(token)

    outs = pl.pallas_call(
        body, name=name,
        out_shape=(pltpu.SemaphoreType.DMA((nsem,)), pltpu.SemaphoreType.DMA((nsem,)),
                   *[pltpu.HBM(b.shape, b.dtype) for b in bufs], jax.ShapeDtypeStruct((SUB, LANE), F32)),
        in_specs=[HBM_SPEC] * nb + [ANY] * nd,
        out_specs=(SEM_SPEC, SEM_SPEC, *[HBM_SPEC] * nb, pl.BlockSpec(memory_space=pltpu.VMEM)),
        input_output_aliases={i: 2 + i for i in range(nb)},
        compiler_params=pltpu.CompilerParams(has_side_effects=EFFECT),
    )(*[pltpu.with_memory_space_constraint(b, pltpu.HBM) for b in bufs], *deps)
    return outs[0], outs[1], list(outs[2:2 + nb]), outs[-1]


def _remote_wait(name, kind, send_sems, recv_sems, bufs, n, after):
    nb, na = len(bufs), len(after)
    same = nb == n

    def body(*refs):
        ins = refs[:nb]
        sends, recvs = refs[nb], refs[nb + 1]
        for cp in _planned_copies(kind, ins[:n], ins[:n] if same else ins[n:], sends, recvs):
            cp.wait_send()
            cp.wait_recv()

    outs = pl.pallas_call(
        body, name=name,
        out_shape=[pltpu.HBM(b.shape, b.dtype) for b in bufs],
        in_specs=[HBM_SPEC] * nb + [SEM_SPEC, SEM_SPEC] + [ANY] * na,
        out_specs=[HBM_SPEC] * nb,
        input_output_aliases={i: i for i in range(nb)},
        compiler_params=pltpu.CompilerParams(has_side_effects=EFFECT),
    )(*bufs, send_sems, recv_sems, *after)
    return list(outs)


def _mm_cols(name, a, w, *, tm, nb=1, epilogue=None, out_dtypes=(F32,)):
    t, k = a.shape
    nblk, _, cb = w.shape

    def body(a_ref, w_ref, *o_refs):
        av = a_ref[...]
        for b in range(nb):
            acc = jnp.dot(av, w_ref[b], preferred_element_type=F32)
            outs = (acc,) if epilogue is None else epilogue(acc)
            for o_ref, o in zip(o_refs, outs):
                o_ref[:, b * cb:(b + 1) * cb] = o.astype(o_ref.dtype)

    return pl.pallas_call(
        body, name=name, grid=(nblk // nb, t // tm),
        in_specs=[pl.BlockSpec((tm, k), lambda j, i: (i, 0)),
                  pl.BlockSpec((nb, k, cb), lambda j, i: (j, 0, 0))],
        out_specs=[pl.BlockSpec((tm, nb * cb), lambda j, i: (i, j)) for _ in out_dtypes],
        out_shape=[jax.ShapeDtypeStruct((t, nblk * cb), dt) for dt in out_dtypes],
        compiler_params=_params(2),
    )(a, w)


MXU_WIDTH = 256


def _mm_cols_pairs(name, a, w, *, tm):
    t, k = a.shape
    nblk, _, cb = w.shape
    main = cb // MXU_WIDTH * MXU_WIDTH
    tail = cb - main
    assert 2 * tail == MXU_WIDTH and nblk % 2 == 0

    def body(a_ref, w_ref, o_ref):
        av = a_ref[...]
        for b in range(2):
            o_ref[:, b * cb:b * cb + main] = jnp.dot(av, w_ref[b, :, 0:main], preferred_element_type=F32)
        tails = jnp.dot(av, jnp.concatenate([w_ref[0, :, main:cb], w_ref[1, :, main:cb]], axis=1),
                        preferred_element_type=F32)
        for b in range(2):
            o_ref[:, b * cb + main:(b + 1) * cb] = tails[:, b * tail:(b + 1) * tail]

    return pl.pallas_call(
        body, name=name, grid=(nblk // 2, t // tm),
        in_specs=[pl.BlockSpec((tm, k), lambda j, i: (i, 0)),
                  pl.BlockSpec((2, k, cb), lambda j, i: (j, 0, 0))],
        out_specs=pl.BlockSpec((tm, 2 * cb), lambda j, i: (i, j)),
        out_shape=jax.ShapeDtypeStruct((t, nblk * cb), F32),
        compiler_params=_params(2),
    )(a, w)


def _mm_rows(name, a, w2d, *, tm, tn):
    t, kf = a.shape
    n = w2d.shape[1]

    def body(a_ref, w_ref, o_ref):
        o_ref[...] = jnp.dot(a_ref[...], w_ref[...], preferred_element_type=F32)

    return pl.pallas_call(
        body, name=name, grid=(t // tm, n // tn),
        in_specs=[pl.BlockSpec((tm, kf), lambda i, j: (i, 0)),
                  pl.BlockSpec((kf, tn), lambda i, j: (0, j))],
        out_specs=pl.BlockSpec((tm, tn), lambda i, j: (i, j)),
        out_shape=jax.ShapeDtypeStruct((t, n), F32),
        compiler_params=_params(2),
    )(a, w2d)


def _mm_nt_acc(name, dy, w, *, tm, tn, col_off=0, deps=()):
    t = dy.shape[0]
    nblk, k, cb = w.shape

    main = cb // MXU_WIDTH * MXU_WIDTH

    def body(dy_ref, w_ref, *rest):
        nt = (((1,), (1,)), ((), ()))
        acc = None
        for b in range(nblk):
            d = lax.dot_general(dy_ref[:, b * cb:b * cb + main], w_ref[b, :, 0:main], nt, preferred_element_type=F32)
            acc = d if acc is None else acc + d
        if main < cb:
            dy_tails = jnp.concatenate([dy_ref[:, b * cb + main:(b + 1) * cb] for b in range(nblk)], axis=1)
            w_tails = jnp.concatenate([w_ref[b, :, main:cb] for b in range(nblk)], axis=1)
            acc = acc + lax.dot_general(dy_tails, w_tails, nt, preferred_element_type=F32)
        rest[-1][...] = acc

    return pl.pallas_call(
        body, name=name, grid=(t // tm, k // tn),
        in_specs=[pl.BlockSpec((tm, nblk * cb), lambda i, j: (i, col_off)),
                  pl.BlockSpec((nblk, tn, cb), lambda i, j: (0, j, 0))] + [ANY] * len(deps),
        out_specs=pl.BlockSpec((tm, tn), lambda i, j: (i, j)),
        out_shape=jax.ShapeDtypeStruct((t, k), F32),
        compiler_params=_params(2),
    )(dy, w, *deps)


def _mm_nt_blocks(name, dy, w2d, *, tm, tkb, extra=(), epilogue=None, out_dtypes=(F32,)):
    t, n = dy.shape
    kf = w2d.shape[0]
    ne = len(extra)

    def body(dy_ref, w_ref, *rest):
        acc = lax.dot_general(dy_ref[...], w_ref[...], (((1,), (1,)), ((), ())), preferred_element_type=F32)
        outs = (acc,) if epilogue is None else epilogue(acc, *[e[...] for e in rest[:ne]])
        for o_ref, o in zip(rest[ne:], outs):
            o_ref[...] = o.astype(o_ref.dtype)

    return pl.pallas_call(
        body, name=name, grid=(kf // tkb, t // tm),
        in_specs=[pl.BlockSpec((tm, n), lambda kb, i: (i, 0)),
                  pl.BlockSpec((tkb, n), lambda kb, i: (kb, 0))]
                 + [pl.BlockSpec((tm, tkb), lambda kb, i: (i, kb)) for _ in extra],
        out_specs=[pl.BlockSpec((tm, tkb), lambda kb, i: (i, kb)) for _ in out_dtypes],
        out_shape=[jax.ShapeDtypeStruct((t, kf), dt) for dt in out_dtypes],
        compiler_params=_params(2),
    )(dy, w2d, *extra)


def _mm_tn(name, a, b, me_arr, *, m, n, tma, tn, sharded, a_off=0, b_off=0, deps=()):
    t = a.shape[0]
    if sharded == "cols":
        cb = n // N_DEV
        nb, q = max(tn // cb, 1), max(cb // tn, 1)
        tw = tn // nb
        full_shape, own_shape = (N_DEV, m, cb), (m, cb)
        full_spec = pl.BlockSpec((nb, tma, tw), lambda i, j, me: (j // q, i, j % q))
    else:
        kb = m // N_DEV
        p = kb // tma
        nb, tw = 1, tn
        full_shape, own_shape = (m, n), (kb, n)
        full_spec = pl.BlockSpec((tma, tn), lambda i, j, me: (i, j))

    def body(me_ref, a_ref, b_ref, *rest):
        full_ref, own_ref, stage, sem = rest[len(deps):]
        i, j = pl.program_id(0), pl.program_id(1)
        acc = lax.dot_general(a_ref[...], b_ref[...], (((0,), (0,)), ((), ())), preferred_element_type=F32)
        for blk in range(nb):
            part = acc[:, blk * tw:(blk + 1) * tw]
            if sharded == "cols":
                full_ref[blk] = part.astype(BF16)
                owner, r0, c0 = (j // q) * nb + blk, i * tma, (j % q) * tw
            else:
                full_ref[...] = part.astype(BF16)
                owner, r0, c0 = i // p, (i % p) * tma, j * tn

            @pl.when(owner == me_ref[0])
            def _():
                stage[...] = part
                cp = pltpu.make_async_copy(
                    stage, own_ref.at[pl.ds(pl.multiple_of(r0, tma), tma), pl.ds(pl.multiple_of(c0, tw), tw)], sem)
                cp.start()
                cp.wait()

    full, own = pl.pallas_call(
        body, name=name,
        grid_spec=pltpu.PrefetchScalarGridSpec(
            num_scalar_prefetch=1, grid=(m // tma, n // tn),
            in_specs=[pl.BlockSpec((t, tma), lambda i, j, me: (0, a_off + i)),
                      pl.BlockSpec((t, tn), lambda i, j, me: (0, b_off + j))] + [ANY] * len(deps),
            out_specs=[full_spec, ANY],
            scratch_shapes=[pltpu.VMEM((tma, tw), F32), pltpu.SemaphoreType.DMA(())]),
        out_shape=[jax.ShapeDtypeStruct(full_shape, BF16), jax.ShapeDtypeStruct(own_shape, F32)],
        compiler_params=_params(2),
    )(me_arr, a, b, *deps)
    if sharded == "rows":
        full = full.reshape(N_DEV, m // N_DEV, n)
    return full, own


def _row_tile(t):
    return t // 8 if (t // 8) % 16 == 0 else ROW_TILE


def _row_call(name, body, t, row_ins, full_ins, row_outs, acc_outs, scratch=(), deps=()):
    tm = _row_tile(t)
    nin = len(row_ins) + len(full_ins)

    def without_deps(*refs):
        body(*refs[:nin], *refs[nin + len(deps):])

    return pl.pallas_call(
        without_deps, name=name, grid=(t // tm,),
        in_specs=[pl.BlockSpec((tm, a.shape[1]), lambda i: (i, 0)) for a in row_ins]
                 + [pl.BlockSpec(a.shape, lambda i: (0, 0)) for a in full_ins] + [ANY] * len(deps),
        out_specs=[pl.BlockSpec((tm, c), lambda i: (i, 0)) for c, _ in row_outs]
                  + [pl.BlockSpec((r, c), lambda i: (0, 0)) for r, c in acc_outs],
        out_shape=[jax.ShapeDtypeStruct((t, c), dt) for c, dt in row_outs]
                  + [jax.ShapeDtypeStruct((r, c), F32) for r, c in acc_outs],
        scratch_shapes=list(scratch),
        compiler_params=_params(1),
    )(*row_ins, *full_ins, *deps)


def _accumulate(ref, v):
    @pl.when(pl.program_id(0) == 0)
    def _():
        ref[...] = v

    @pl.when(pl.program_id(0) > 0)
    def _():
        ref[...] += v


def _rms(v):
    return lax.rsqrt(jnp.mean(v * v, axis=-1, keepdims=True) + RMS_EPS)


def _rms_bwd(dout, u, r, g):
    du = dout * g
    dx = r * (du - u * jnp.mean(du * u, axis=-1, keepdims=True))
    return dx, _colsum8(dout * u)


def _pre_norm(h0, g):
    t, d = h0.shape

    def body(h_ref, g_ref, n_ref):
        h = h_ref[...]
        n_ref[...] = (h * _rms(h) * g_ref[...]).astype(BF16)

    return _row_call("pre_norm", body, t, [h0], [g], [(d, BF16)], [])[0]


def _mix_post(m_mix, wo_full, h0, g_post, g_pre, deps=()):
    t, d = h0.shape
    tm = _row_tile(t)

    def body(m_ref, wo_ref, h0_ref, gp_ref, gq_ref, *rest):
        mix_ref, h1_ref, n2_ref = rest[len(deps):]
        mix_v = jnp.dot(m_ref[...], wo_ref[...], preferred_element_type=F32)
        mix_ref[...] = mix_v
        h1 = h0_ref[...] + mix_v * _rms(mix_v) * gp_ref[...]
        h1_ref[...] = h1
        n2_ref[...] = (h1 * _rms(h1) * gq_ref[...]).astype(BF16)

    tile = pl.BlockSpec((tm, d), lambda i: (i, 0))
    gain = pl.BlockSpec((1, d), lambda i: (0, 0))
    return pl.pallas_call(
        body, name="mix_post", grid=(t // tm,),
        in_specs=[tile, pl.BlockSpec((d, d), lambda i: (0, 0)), tile, gain, gain] + [ANY] * len(deps),
        out_specs=[tile, tile, tile],
        out_shape=[jax.ShapeDtypeStruct((t, d), F32), jax.ShapeDtypeStruct((t, d), F32),
                   jax.ShapeDtypeStruct((t, d), BF16)],
        compiler_params=_params(1),
    )(m_mix, wo_full, h0, g_post, g_pre, *deps)


def _loss_head(fo, h1, tgt, g_post_mlp, t_real):
    t, d = h1.shape
    tile = _row_tile(t)

    def body(fo_ref, h1_ref, tgt_ref, g_ref, dfo_ref, dh2_ref, dg_ref, loss_ref, lacc):
        i = pl.program_id(0)
        fo_v = fo_ref[...]
        g = g_ref[...]
        r = _rms(fo_v)
        u = fo_v * r
        h2 = h1_ref[...] + u * g
        row = i * tile + lax.broadcasted_iota(jnp.int32, (tile, 1), 0)
        valid = jnp.logical_and(row >= N_META, row < t_real)
        diff = jnp.where(valid, h2 - tgt_ref[...], 0.0)
        dh2 = diff * (1.0 / d)
        dh2_ref[...] = dh2
        dfo, dg = _rms_bwd(dh2, u, r, g)
        dfo_ref[...] = dfo.astype(BF16)
        _accumulate(dg_ref, dg)
        _accumulate(lacc, _colsum8(diff * diff))

        @pl.when(i == pl.num_programs(0) - 1)
        def _():
            loss_ref[...] = jnp.full((SUB, LANE), (0.5 / d) * jnp.sum(lacc[...]), F32)

    return _row_call("loss_head", body, t, [fo, h1, tgt], [g_post_mlp],
                     [(d, BF16), (d, F32)], [(SUB, d), (SUB, LANE)], scratch=[pltpu.VMEM((SUB, d), F32)])


def _mid_norm_bwd(dn2, h1, dh2, mix, g_pre_mlp, g_post_mix, deps=()):
    t, d = h1.shape

    def body(dn2_ref, h1_ref, dh2_ref, mix_ref, gq_ref, gp_ref, dh1_ref, dmix_ref, dgq_ref, dgp_ref):
        h1 = h1_ref[...]
        r3 = _rms(h1)
        dx, dgq = _rms_bwd(dn2_ref[...], h1 * r3, r3, gq_ref[...])
        dh1 = dh2_ref[...] + dx
        dh1_ref[...] = dh1
        mix_v = mix_ref[...]
        r2 = _rms(mix_v)
        dmix, dgp = _rms_bwd(dh1, mix_v * r2, r2, gp_ref[...])
        dmix_ref[...] = dmix.astype(BF16)
        _accumulate(dgq_ref, dgq)
        _accumulate(dgp_ref, dgp)

    return _row_call("mid_norm_bwd", body, t, [dn2, h1, dh2, mix], [g_pre_mlp, g_post_mix],
                     [(d, F32), (d, BF16)], [(SUB, d), (SUB, d)], deps=deps)


def _pre_norm_bwd(dn, h0, dh1, g_pre_mix, deps=()):
    t, d = h0.shape

    def body(dn_ref, h0_ref, dh1_ref, g_ref, dh0_ref, dg_ref):
        h0 = h0_ref[...]
        r = _rms(h0)
        dx, dg = _rms_bwd(dn_ref[...], h0 * r, r, g_ref[...])
        dh0_ref[...] = dh1_ref[...] + dx
        _accumulate(dg_ref, dg)

    return _row_call("pre_norm_bwd", body, t, [dn, h0, dh1], [g_pre_mix], [(d, F32)], [(SUB, d)], deps=deps)


def _layer_norm_silu(a1, ln_g, ln_b):
    t, c = a1.shape

    def body(a1_ref, g_ref, b_ref, a3_ref):
        a = a1_ref[...]
        mu = jnp.mean(a, axis=-1, keepdims=True)
        xc = a - mu
        rstd = lax.rsqrt(jnp.mean(xc * xc, axis=-1, keepdims=True) + LN_EPS)
        z = xc * rstd * g_ref[...] + b_ref[...]
        a3_ref[...] = (z * _sigmoid(z)).astype(BF16)

    return _row_call("layer_norm_silu", body, t, [a1], [ln_g, ln_b], [(c, BF16)], [])[0]


def _layer_norm_silu_bwd(da3, a1, ln_g, ln_b, deps=()):
    t, c = a1.shape

    def body(da3_ref, a1_ref, g_ref, b_ref, da1_ref, dg_ref, db_ref):
        a = a1_ref[...]
        g = g_ref[...]
        mu = jnp.mean(a, axis=-1, keepdims=True)
        xc = a - mu
        rstd = lax.rsqrt(jnp.mean(xc * xc, axis=-1, keepdims=True) + LN_EPS)
        xhat = xc * rstd
        z = xhat * g + b_ref[...]
        sg = _sigmoid(z)
        dz = da3_ref[...] * (sg * (1.0 + z * (1.0 - sg)))
        dxhat = dz * g
        da1_ref[...] = rstd * (dxhat - jnp.mean(dxhat, axis=-1, keepdims=True)
                               - xhat * jnp.mean(dxhat * xhat, axis=-1, keepdims=True))
        _accumulate(dg_ref, _colsum8(dz * xhat))
        _accumulate(db_ref, _colsum8(dz))

    return _row_call("layer_norm_silu_bwd", body, t, [da3, a1], [ln_g, ln_b], [(c, F32)], [(SUB, c), (SUB, c)], deps=deps)


def _branch_merge(a3, s, wpw, wso, proj, b_gates, d, deps=()):
    t, cols = proj.shape
    nblk, k, cb = wpw.shape
    w = 1024
    nh = d // w
    per = w // cb
    ga0 = (cols - 2 * d) // w
    tm = _row_tile(t)

    def body(a3_ref, s_ref, wpw_ref, wso_ref, *rest):
        pa_refs, pb_refs, bg_ref = rest[:nh], rest[nh:2 * nh], rest[2 * nh]
        ya_ref, yb_ref, ga_ref, gb_ref, m_ref = rest[2 * nh + 1 + len(deps):]
        a3v, sv = a3_ref[...], s_ref[...]
        for b in range(nblk):
            here = slice(b * cb, (b + 1) * cb)
            local = slice((b % per) * cb, (b % per + 1) * cb)
            ya = jnp.dot(a3v, wpw_ref[b], preferred_element_type=F32)
            yb = jnp.dot(sv, wso_ref[b], preferred_element_type=F32)
            ga = _sigmoid(pa_refs[b // per][:, local] + bg_ref[:, here])
            gb = _sigmoid(pb_refs[b // per][:, local] + bg_ref[:, d + b * cb:d + (b + 1) * cb])
            ya_ref[:, here] = ya.astype(BF16)
            yb_ref[:, here] = yb.astype(BF16)
            ga_ref[:, here] = ga.astype(BF16)
            gb_ref[:, here] = gb.astype(BF16)
            m_ref[:, here] = (ga * ya + gb * yb).astype(BF16)

    tile = pl.BlockSpec((tm, d), lambda i: (i, 0))
    return pl.pallas_call(
        body, name="branch_merge", grid=(t // tm,),
        in_specs=[pl.BlockSpec((tm, k), lambda i: (i, 0)), pl.BlockSpec((tm, k), lambda i: (i, 0)),
                  pl.BlockSpec((nblk, k, cb), lambda i: (0, 0, 0)), pl.BlockSpec((nblk, k, cb), lambda i: (0, 0, 0))]
                 + [pl.BlockSpec((tm, w), lambda i, h=h: (i, ga0 + h)) for h in range(2 * nh)]
                 + [pl.BlockSpec((1, 2 * d), lambda i: (0, 0))] + [ANY] * len(deps),
        out_specs=[tile] * 5,
        out_shape=[jax.ShapeDtypeStruct((t, d), BF16)] * 5,
        compiler_params=_params(1),
    )(a3, s, wpw, wso, *([proj] * (2 * nh)), b_gates, *deps)


def _gate_backward(dmix, wo_full, ga, gb, ya, yb, cols, tm, deps=()):
    t, d = ya.shape
    w = 1024
    nh = d // w
    ga0 = (cols - 2 * d) // w

    def body(dmix_ref, wo_ref, ga_ref, gb_ref, ya_ref, yb_ref, *rest):
        dya_ref, dyb_ref, dp_ref, dba_ref, dbb_ref, stage, sems = rest[len(deps):]
        h, i = pl.program_id(0), pl.program_id(1)
        dm = lax.dot_general(dmix_ref[...], wo_ref[...], (((1,), (1,)), ((), ())), preferred_element_type=F32)
        ga = ga_ref[...].astype(F32)
        gb = gb_ref[...].astype(F32)
        dya_ref[...] = (dm * ga).astype(BF16)
        dyb_ref[...] = (dm * gb).astype(BF16)
        dpa = dm * ya_ref[...].astype(F32) * ga * (1.0 - ga)
        dpb = dm * yb_ref[...].astype(F32) * gb * (1.0 - gb)
        stage[0] = dpa.astype(BF16)
        stage[1] = dpb.astype(BF16)
        rows = pl.ds(pl.multiple_of(i * tm, tm), tm)
        copies = [pltpu.make_async_copy(
            stage.at[g], dp_ref.at[rows, pl.ds(pl.multiple_of((ga0 + g * nh + h) * w, w), w)], sems.at[g])
            for g in range(2)]
        for cp in copies:
            cp.start()

        @pl.when(i == 0)
        def _():
            dba_ref[...] = _colsum8(dpa)
            dbb_ref[...] = _colsum8(dpb)

        @pl.when(i > 0)
        def _():
            dba_ref[...] += _colsum8(dpa)
            dbb_ref[...] += _colsum8(dpb)

        for cp in copies:
            cp.wait()

    tile = pl.BlockSpec((tm, w), lambda h, i: (i, h))
    return pl.pallas_call(
        body, name="gate_backward", grid=(nh, t // tm),
        in_specs=[pl.BlockSpec((tm, d), lambda h, i: (i, 0)),
                  pl.BlockSpec((w, d), lambda h, i: (h, 0)),
                  tile, tile, tile, tile] + [ANY] * len(deps),
        out_specs=[tile, tile, ANY,
                   pl.BlockSpec((SUB, w), lambda h, i: (0, h)),
                   pl.BlockSpec((SUB, w), lambda h, i: (0, h))],
        out_shape=[jax.ShapeDtypeStruct((t, d), BF16), jax.ShapeDtypeStruct((t, d), BF16),
                   jax.ShapeDtypeStruct((t, cols), BF16),
                   jax.ShapeDtypeStruct((SUB, d), F32), jax.ShapeDtypeStruct((SUB, d), F32)],
        scratch_shapes=[pltpu.VMEM((2, tm, w), BF16), pltpu.SemaphoreType.DMA((2,))],
        compiler_params=_params(2),
    )(dmix, wo_full, ga, gb, ya, yb, *deps)


def _causal_conv(xp_ref, w_ref, ntap, r0):
    n = CONV_CHUNK + CONV_PAD
    win = xp_ref[pl.ds(r0, n), :]
    acc = None
    for k in range(ntap):
        back = ntap - 1 - k
        shifted = pltpu.roll(win, n - (CONV_PAD - back), 0)
        term = w_ref[k:k + 1, :] * shifted[:CONV_CHUNK]
        acc = term if acc is None else acc + term
    return acc


def _anticausal_conv(xp_ref, w_ref, ntap, r0):
    n = CONV_CHUNK + CONV_PAD
    win = xp_ref[pl.ds(pl.multiple_of(CONV_PAD + r0, CONV_PAD), n), :]
    acc = None
    for k in range(ntap):
        ahead = ntap - 1 - k
        shifted = win if ahead == 0 else pltpu.roll(win, n - ahead, 0)
        term = w_ref[k:k + 1, :] * shifted[:CONV_CHUNK]
        acc = term if acc is None else acc + term
    return acc


def _conv_weight_grad(dw_ref, d_chunk, xp_ref, ntap, r0):
    n = CONV_CHUNK + CONV_PAD
    win = xp_ref[pl.ds(r0, n), :]
    for k in range(ntap):
        back = ntap - 1 - k
        shifted = pltpu.roll(win, n - (CONV_PAD - back), 0)
        dw_ref[k * SUB:(k + 1) * SUB, :] += _colsum8(d_chunk * shifted[:CONV_CHUNK])


def _zero_pads(ref, t):
    ref[0:CONV_PAD, :] = jnp.zeros((CONV_PAD, LANE), F32)
    ref[CONV_PAD + t:CONV_PAD + t + CONV_PAD, :] = jnp.zeros((CONV_PAD, LANE), F32)


def _for_chunks(t, fn):
    def step(idx, carry):
        fn(pl.multiple_of(idx * CONV_CHUNK, CONV_CHUNK))
        return carry

    lax.fori_loop(0, t // CONV_CHUNK, step, 0)


def _conv_forward(proj, conf_w, conf_b, short_w, dc, deps=()):
    t = proj.shape[0]
    nc = dc // LANE

    def body(av_ref, ag_ref, bg_ref, cg_ref, v_ref, cw_ref, cb_ref, sw_ref, *rest):
        a1_ref, s_ref, xa, xb = rest[len(deps):]
        _zero_pads(xa, t)
        _zero_pads(xb, t)
        xa[CONV_PAD:CONV_PAD + t, :] = av_ref[...] * _sigmoid(ag_ref[...])
        xb[CONV_PAD:CONV_PAD + t, :] = cg_ref[...] * v_ref[...]

        def chunk(r0):
            rs = pl.ds(r0, CONV_CHUNK)
            a1_ref[rs, :] = _causal_conv(xa, cw_ref, CONF_K, r0) + cb_ref[...]
            s_ref[rs, :] = (bg_ref[rs, :] * _causal_conv(xb, sw_ref, SHORT_K, r0)).astype(BF16)

        _for_chunks(t, chunk)

    col = lambda g: pl.BlockSpec((t, LANE), lambda c, g=g: (0, g * nc + c))
    return pl.pallas_call(
        body, name="conv_forward", grid=(nc,),
        in_specs=[col(0), col(1), col(2), col(3), col(4),
                  pl.BlockSpec((CONF_K, LANE), lambda c: (0, c)),
                  pl.BlockSpec((1, LANE), lambda c: (0, c)),
                  pl.BlockSpec((SHORT_K, LANE), lambda c: (0, c))] + [ANY] * len(deps),
        out_specs=[pl.BlockSpec((t, LANE), lambda c: (0, c)), pl.BlockSpec((t, LANE), lambda c: (0, c))],
        out_shape=[jax.ShapeDtypeStruct((t, dc), F32), jax.ShapeDtypeStruct((t, dc), BF16)],
        scratch_shapes=[pltpu.VMEM((t + 2 * CONV_PAD, LANE), F32), pltpu.VMEM((t + 2 * CONV_PAD, LANE), F32)],
        compiler_params=_params(1),
    )(proj, proj, proj, proj, proj, conf_w, conf_b, short_w, *deps)


def _conv_backward(dproj, proj, da1, ds, conf_w, short_w, dc):
    t = proj.shape[0]
    nc = dc // LANE

    def body(dp_in, av_ref, ag_ref, bg_ref, cg_ref, v_ref, da1_ref, ds_ref, cw_ref, sw_ref,
             dp_ref, dcw_ref, dcb_ref, dsw_ref, xa, xb, da, db, stage, sems):
        del dp_in
        c = pl.program_id(0)
        for ref in (xa, xb, da, db):
            _zero_pads(ref, t)
        xa[CONV_PAD:CONV_PAD + t, :] = av_ref[...] * _sigmoid(ag_ref[...])
        xb[CONV_PAD:CONV_PAD + t, :] = cg_ref[...] * v_ref[...]
        da[CONV_PAD:CONV_PAD + t, :] = da1_ref[...]
        dcw_ref[...] = jnp.zeros(dcw_ref.shape, F32)
        dsw_ref[...] = jnp.zeros(dsw_ref.shape, F32)
        dcb_ref[...] = jnp.zeros(dcb_ref.shape, F32)

        def through_gate(r0):
            rs = pl.ds(r0, CONV_CHUNK)
            ds_c = ds_ref[rs, :]
            stage[2, rs, :] = (ds_c * _causal_conv(xb, sw_ref, SHORT_K, r0)).astype(BF16)
            db[pl.ds(pl.multiple_of(CONV_PAD + r0, CONV_PAD), CONV_CHUNK), :] = ds_c * bg_ref[rs, :]

        _for_chunks(t, through_gate)

        def through_convs(r0):
            rs = pl.ds(r0, CONV_CHUNK)
            da0 = _anticausal_conv(da, cw_ref, CONF_K, r0)
            sg = _sigmoid(ag_ref[rs, :])
            stage[0, rs, :] = (da0 * sg).astype(BF16)
            stage[1, rs, :] = (da0 * av_ref[rs, :] * sg * (1.0 - sg)).astype(BF16)
            dcv = _anticausal_conv(db, sw_ref, SHORT_K, r0)
            stage[3, rs, :] = (dcv * v_ref[rs, :]).astype(BF16)
            stage[4, rs, :] = (dcv * cg_ref[rs, :]).astype(BF16)
            da1_c = da1_ref[rs, :]
            _conv_weight_grad(dcw_ref, da1_c, xa, CONF_K, r0)
            _conv_weight_grad(dsw_ref, ds_ref[rs, :] * bg_ref[rs, :], xb, SHORT_K, r0)
            dcb_ref[...] += _colsum8(da1_c)

        _for_chunks(t, through_convs)
        copies = [pltpu.make_async_copy(
            stage.at[g], dp_ref.at[:, pl.ds(pl.multiple_of((g * nc + c) * LANE, LANE), LANE)], sems.at[g])
            for g in range(5)]
        for cp in copies:
            cp.start()
        for cp in copies:
            cp.wait()

    col = lambda g: pl.BlockSpec((t, LANE), lambda c, g=g: (0, g * nc + c))
    blk = pl.BlockSpec((t, LANE), lambda c: (0, c))
    return pl.pallas_call(
        body, name="conv_backward", grid=(nc,),
        in_specs=[ANY, col(0), col(1), col(2), col(3), col(4), blk, blk,
                  pl.BlockSpec((CONF_K, LANE), lambda c: (0, c)),
                  pl.BlockSpec((SHORT_K, LANE), lambda c: (0, c))],
        out_specs=[ANY,
                   pl.BlockSpec((CONF_K * SUB, LANE), lambda c: (0, c)),
                   pl.BlockSpec((SUB, LANE), lambda c: (0, c)),
                   pl.BlockSpec((SHORT_K * SUB, LANE), lambda c: (0, c))],
        out_shape=[jax.ShapeDtypeStruct(dproj.shape, dproj.dtype),
                   jax.ShapeDtypeStruct((CONF_K * SUB, dc), F32),
                   jax.ShapeDtypeStruct((SUB, dc), F32),
                   jax.ShapeDtypeStruct((SHORT_K * SUB, dc), F32)],
        scratch_shapes=[pltpu.VMEM((t + 2 * CONV_PAD, LANE), F32)] * 4
                       + [pltpu.VMEM((5, t, LANE), BF16), pltpu.SemaphoreType.DMA((5,))],
        input_output_aliases={0: 0},
        compiler_params=_params(1),
    )(dproj, proj, proj, proj, proj, proj, da1, ds, conf_w, short_w)


def _adamw_math(w, g, m, v):
    m = ADAM_B1 * m + (1.0 - ADAM_B1) * g
    v = ADAM_B2 * v + (1.0 - ADAM_B2) * (g * g)
    m_hat = m / (1.0 - ADAM_B1 ** ADAM_STEP)
    v_hat = v / (1.0 - ADAM_B2 ** ADAM_STEP)
    delta = -ADAM_LR * (m_hat / (jnp.sqrt(v_hat) + ADAM_EPS) + ADAM_WD * w)
    return delta, m, v


def _cast_into_slot(name, w, me_arr, deps=()):
    r, c = w.shape
    tr = 256

    def body(me_ref, w_ref, *rest):
        del me_ref
        rest[-1][0] = w_ref[...].astype(BF16)

    return pl.pallas_call(
        body, name=name,
        grid_spec=pltpu.PrefetchScalarGridSpec(
            num_scalar_prefetch=1, grid=(r // tr,),
            in_specs=[pl.BlockSpec((tr, c), lambda i, me: (i, 0))] + [ANY] * len(deps),
            out_specs=pl.BlockSpec((1, tr, c), lambda i, me: (me[0], i, 0))),
        out_shape=jax.ShapeDtypeStruct((N_DEV, r, c), BF16),
        compiler_params=_params(1),
    )(me_arr, w, *deps)


def _chip_sum(name, full, from_sibling, me_arr):
    _, r, c = full.shape
    tr = min(r, 512)

    def body(me_ref, full_ref, sib_ref, sums_ref):
        del me_ref
        sums_ref[0] = (full_ref[0].astype(F32) + sib_ref[0].astype(F32)).astype(BF16)

    other = lambda k, me: (me[0] // 2 + 1 + k) % 4
    return pl.pallas_call(
        body, name=name,
        grid_spec=pltpu.PrefetchScalarGridSpec(
            num_scalar_prefetch=1, grid=(r // tr, 3),
            in_specs=[pl.BlockSpec((1, tr, c), lambda i, k, me: (2 * other(k, me) + me[0] % 2, i, 0)),
                      pl.BlockSpec((1, tr, c), lambda i, k, me: (other(k, me), i, 0))],
            out_specs=pl.BlockSpec((1, tr, c), lambda i, k, me: (other(k, me), i, 0))),
        out_shape=jax.ShapeDtypeStruct((4, r, c), BF16),
        compiler_params=_params(2),
    )(me_arr, full, from_sibling)


def _adamw_shard(name, w, m, v, parts, me_arr, deps=()):
    r, c = w.shape
    tr = min(256, r // len(parts))
    np_ = len(parts)
    per = r // np_ // tr

    def body(me_ref, w_ref, m_ref, v_ref, *rest):
        g_out, d_out, m_out, v_out = rest[5 * np_ + len(deps):]
        g = None
        for p in range(np_):
            gp = rest[5 * p][...]
            for l_ref in rest[5 * p + 1:5 * p + 5]:
                gp = gp + l_ref[0].astype(F32)
            g = gp if g is None else jnp.where(pl.program_id(0) // per == p, gp, g)
        delta, m_new, v_new = _adamw_math(w_ref[...], g, m_ref[...], v_ref[...])
        g_out[...] = g
        d_out[...] = delta
        m_out[...] = m_new
        v_out[...] = v_new

    tile = pl.BlockSpec((tr, c), lambda i, me: (i, 0))
    part_specs, part_args = [], []
    for p, (g_own, from_sibling, landed) in enumerate(parts):
        row = lambda i, p=p: jnp.clip(i - p * per, 0, per - 1)
        part_specs.append(pl.BlockSpec((tr, c), lambda i, me, row=row: (row(i), 0)))
        part_specs += [pl.BlockSpec((1, tr, c), lambda i, me, k=k, row=row: ((me[0] // 2 + k) % 4, row(i), 0))
                       for k in range(4)]
        part_args += [g_own, from_sibling, landed, landed, landed]
    return pl.pallas_call(
        body, name=name,
        grid_spec=pltpu.PrefetchScalarGridSpec(
            num_scalar_prefetch=1, grid=(r // tr,),
            in_specs=[tile] * 3 + part_specs + [ANY] * len(deps), out_specs=[tile] * 4),
        out_shape=[jax.ShapeDtypeStruct((r, c), F32)] * 4,
        compiler_params=_params(1),
    )(me_arr, w, m, v, *part_args, *deps)


SMALL_W = 1024
VEC_ROWS = 16
LOSS_ROW = 15
META_ROW0 = 16
CONF_ROW0 = 64
SHORT_ROW0 = 96
SMALL_ROWS = 104


def _pack_small(vec_parts, dmeta, dcw, dsw, loss_blk, me_arr):
    widths = [p.shape[1] for p in vec_parts]
    nv = len(vec_parts)

    def body(me_ref, *refs):
        del me_ref
        parts, (dmeta_ref, dcw_ref, dsw_ref, loss_ref, out_ref) = refs[:nv], refs[nv:]
        out_ref[0] = jnp.zeros((SMALL_ROWS, SMALL_W), F32)
        out_ref[0, LOSS_ROW:LOSS_ROW + 1, 0:LANE] = loss_ref[0:1, :]
        row = 0
        for p_ref, wd in zip(parts, widths):
            s = jnp.sum(p_ref[...], axis=0, keepdims=True)
            for h in range(wd // SMALL_W):
                out_ref[0, row:row + 1, :] = s[:, h * SMALL_W:(h + 1) * SMALL_W]
                row += 1
        for h in range(dmeta_ref.shape[1] // SMALL_W):
            out_ref[0, META_ROW0 + h * N_META:META_ROW0 + (h + 1) * N_META, :] = dmeta_ref[:, h * SMALL_W:(h + 1) * SMALL_W]
        for k in range(CONF_K):
            out_ref[0, CONF_ROW0 + k:CONF_ROW0 + k + 1, :] = jnp.sum(dcw_ref[k * SUB:(k + 1) * SUB, :], axis=0, keepdims=True)
        for k in range(SHORT_K):
            out_ref[0, SHORT_ROW0 + k:SHORT_ROW0 + k + 1, :] = jnp.sum(dsw_ref[k * SUB:(k + 1) * SUB, :], axis=0, keepdims=True)

    ins = [*vec_parts, dmeta, dcw, dsw, loss_blk]
    return pl.pallas_call(
        body, name="pack_small",
        grid_spec=pltpu.PrefetchScalarGridSpec(
            num_scalar_prefetch=1, grid=(1,),
            in_specs=[pl.BlockSpec(a.shape, lambda i, me: (0, 0)) for a in ins],
            out_specs=pl.BlockSpec((1, SMALL_ROWS, SMALL_W), lambda i, me: (me[0], 0, 0))),
        out_shape=jax.ShapeDtypeStruct((N_DEV, SMALL_ROWS, SMALL_W), F32),
        compiler_params=_params(1),
    )(me_arr, *ins)


def _small_update(gathered, me_arr, vec_params, meta_p, conf_p, short_p):
    widths = [p[0].shape[1] for p in vec_params]
    nv = len(vec_params)
    mcols = meta_p[0].shape[1]
    per_row = SMALL_W // mcols

    def body(me_ref, gv_ref, gm_ref, gc_ref, gs_ref, *rest):
        del me_ref
        ins, outs = rest[:3 * (nv + 3)], rest[3 * (nv + 3):]

        def total(ref, r0, rows):
            s = ref[0, r0:r0 + rows, :]
            for dev in range(1, N_DEV):
                s = s + ref[dev, r0:r0 + rows, :]
            return s

        grads = []
        row = 0
        for wd in widths:
            pieces = [total(gv_ref, row + h, 1) for h in range(wd // SMALL_W)]
            grads.append(pieces[0] if len(pieces) == 1 else jnp.concatenate(pieces, axis=1))
            row += len(pieces)
        grads.append(total(gm_ref, 0, N_META))
        grads.append(total(gc_ref, 0, CONF_K))
        grads.append(total(gs_ref, 0, SHORT_K))
        loss = gv_ref[0, LOSS_ROW:LOSS_ROW + 1, 0:LANE]
        for dev in range(1, N_DEV):
            loss = loss + gv_ref[dev, LOSS_ROW:LOSS_ROW + 1, 0:LANE]
        outs[-1][...] = loss
        for idx, g in enumerate(grads):
            w_ref, m_ref, v_ref = ins[3 * idx:3 * idx + 3]
            delta, m_new, v_new = _adamw_math(w_ref[...], g, m_ref[...], v_ref[...])
            g_out, d_out, m_out, v_out = outs[4 * idx:4 * idx + 4]
            g_out[...] = g
            d_out[...] = delta
            m_out[...] = m_new
            v_out[...] = v_new

    params = list(vec_params) + [meta_p, conf_p, short_p]
    flat = [a for p in params for a in p]
    whole = lambda a: pl.BlockSpec(a.shape, lambda i, me: (0,) * a.ndim)
    outs = pl.pallas_call(
        body, name="small_update",
        grid_spec=pltpu.PrefetchScalarGridSpec(
            num_scalar_prefetch=1, grid=(1,),
            in_specs=[pl.BlockSpec((N_DEV, VEC_ROWS, SMALL_W), lambda i, me: (0, 0, 0)),
                      pl.BlockSpec((N_DEV, N_META, mcols),
                                   lambda i, me: (0, META_ROW0 // N_META + me[0] // per_row, me[0] % per_row)),
                      pl.BlockSpec((N_DEV, 32, LANE), lambda i, me: (0, CONF_ROW0 // 32, me[0])),
                      pl.BlockSpec((N_DEV, SUB, LANE), lambda i, me: (0, SHORT_ROW0 // SUB, me[0]))]
                     + [whole(a) for a in flat],
            out_specs=[whole(p[0]) for p in params for _ in range(4)]
                      + [pl.BlockSpec((1, LANE), lambda i, me: (0, 0))]),
        out_shape=[jax.ShapeDtypeStruct(p[0].shape, F32) for p in params for _ in range(4)]
                  + [jax.ShapeDtypeStruct((1, LANE), F32)],
        compiler_params=_params(1),
    )(me_arr, gathered, gathered, gathered, gathered, *flat)
    return [tuple(outs[4 * i:4 * i + 4]) for i in range(len(params))], outs[-1][0, 0]


def kernel(x, meta, g_pre_mix, w_in, b_gates, conf_dw_w, conf_dw_b, conf_ln_g, conf_ln_b, conf_w_pw, short_dw_w, short_w_out, w_o, g_post_mix, g_pre_mlp, w_up, w_down, g_post_mlp, loss_target, m_meta, m_g_pre_mix, m_w_in, m_b_gates, m_conf_dw_w, m_conf_dw_b, m_conf_ln_g, m_conf_ln_b, m_conf_w_pw, m_short_dw_w, m_short_w_out, m_w_o, m_g_post_mix, m_g_pre_mlp, m_w_up, m_w_down, m_g_post_mlp, v_meta, v_g_pre_mix, v_w_in, v_b_gates, v_conf_dw_w, v_conf_dw_b, v_conf_ln_g, v_conf_ln_b, v_conf_w_pw, v_short_dw_w, v_short_w_out, v_w_o, v_g_post_mix, v_g_pre_mlp, v_w_up, v_w_down, v_g_post_mlp):
    seq, d = x.shape[1], x.shape[2]
    dc = conf_w_pw.shape[1]
    t_real = N_META + seq
    t = -(-t_real // ROW_TILE) * ROW_TILE
    tm = t // 2
    assert tm % 16 == 0 and d % 1024 == 0 and dc % 1024 == 0
    x_idx, y_idx, c_idx = _position()
    me_arr = jnp.reshape(4 * x_idx + 2 * y_idx + c_idx, (1,)).astype(jnp.int32)

    big = [w_in[0], conf_w_pw[0], short_w_out[0], w_o[0], w_up[0], w_down[0]]
    big_names = ["w_in", "conf_w_pw", "short_w_out", "w_o", "w_up", "w_down"]
    groups = [[0], [1, 2, 3], [4], [5]]
    slots, deps = [], []
    for g, idxs in enumerate(groups):
        slots.append([_cast_into_slot("cast_" + big_names[i], big[i], me_arr, deps=deps) for i in idxs])
        if g == 0:
            direct0 = _remote_start("gather0_direct_start", "gather_direct", slots[0])
            deps = [direct0[3]]
    casts = [sl for group in slots[1:] for sl in group]
    meta_g, cw_g, sw_g = _all_gather("gather_small_params", [meta, conf_dw_w[0], short_dw_w[0]], deps=casts)

    def start_direct(g, deps):
        send, recv, bufs, tok = _remote_start("gather%d_direct_start" % g, "gather_direct", slots[g], deps=deps)
        return (send, recv, bufs), tok

    def relay(g, state, after):
        send, recv, bufs = state
        bufs = _remote_wait("gather%d_direct_wait" % g, "gather_direct", send, recv, bufs, len(bufs), after)
        send, recv, bufs, tok = _remote_start("gather%d_relay_start" % g, "gather_relay", bufs)
        return (send, recv, bufs), tok

    def gathered(g, state, after):
        send, recv, bufs = state
        bufs = _remote_wait("gather%d_relay_wait" % g, "gather_relay", send, recv, bufs, len(bufs), after)
        send, recv, bufs, tok = _remote_start("gather%d_diag_start" % g, "gather_diag", bufs)
        return _remote_wait("gather%d_diag_wait" % g, "gather_diag", send, recv, bufs, len(bufs), [tok])

    unshard =lambda g: jnp.transpose(g, (1, 0, 2)).reshape(g.shape[1], -1)
    meta_full, cw_full, sw_full = unshard(meta_g), unshard(cw_g), unshard(sw_g)

    relay0, tok = relay(0, direct0[:3], [meta_g])
    zrows = jnp.zeros((t - t_real, d), F32) + tok[0, 0] * 0.0
    h0 = jnp.concatenate([meta_full, x[0], zrows], axis=0)
    tgt = jnp.concatenate([jnp.zeros((N_META, d), F32), loss_target[0], zrows], axis=0)
    n = _pre_norm(h0, g_pre_mix)
    direct1, tok = start_direct(1, [tok])
    direct2, tok = start_direct(2, [tok])
    win_g, = gathered(0, relay0, [tok, n])
    proj = _mm_cols_pairs("proj", n, win_g, tm=tm // 2)
    relay1, tok = relay(1, direct1, [proj])
    direct3, tok = start_direct(3, [tok])
    a1, s = _conv_forward(proj, cw_full, conf_dw_b, sw_full, dc, deps=[tok])
    a3 = _layer_norm_silu(a1, conf_ln_g, conf_ln_b)
    wpw_g, wso_g, wo_g = gathered(1, relay1, [a3])
    wo_full = wo_g.reshape(d, d)
    ya, yb, gate_a, gate_b, m_mix = _branch_merge(a3, s, wpw_g, wso_g, proj, b_gates, d)
    relay2, tok = relay(2, direct2, [m_mix])
    mix, h1, n2 = _mix_post(m_mix, wo_full, h0, g_post_mix, g_pre_mlp, deps=[tok])
    relay3, tok = relay(3, direct3, [n2])
    wup_g, = gathered(2, relay2, [tok])

    def up_epilogue(acc):
        r = jnp.maximum(acc, 0.0)
        return r * r, r

    f, relu_up = _mm_cols("mlp_up", n2, wup_g, tm=tm, epilogue=up_epilogue, out_dtypes=(BF16, BF16))
    wdn_g, = gathered(3, relay3, [f])
    wdn_full = wdn_g.reshape(-1, d)
    fo = _mm_rows("mlp_down", f, wdn_full, tm=tm // 2, tn=512)
    dfo, dh2, dg_post_mlp, loss_blk = _loss_head(fo, h1, tgt, g_post_mlp, t_real)

    def reduce_start(tag, fulls, deps):
        lands = [lax.empty((4,) + g.shape[1:], BF16) for g in fulls]
        send, recv, bufs, tok = _remote_start("reduce_%s_d2d_start" % tag, "reduce_d2d", fulls, lands, deps=deps)
        return (send, recv, bufs), tok

    def reduce_middle(tag, state, owns, after):
        send, recv, bufs = state
        k = len(owns)
        bufs = _remote_wait("reduce_%s_d2d_wait" % tag, "reduce_d2d", send, recv, bufs, k, after)
        from_sibling = bufs[k:]
        sums = [_chip_sum("chip_sum_%s%d" % (tag, i), bufs[i], from_sibling[i], me_arr) for i in range(k)]
        lands = [lax.empty(sm.shape, BF16) for sm in sums]
        send, recv, bufs, tok = _remote_start("reduce_%s_ici_start" % tag, "reduce_ici", sums, lands)
        return (send, recv, bufs, list(zip(owns, from_sibling))), tok

    def reduce_finish(tag, state, after):
        send, recv, bufs, local = state
        k = len(local)
        bufs = _remote_wait("reduce_%s_ici_wait" % tag, "reduce_ici", send, recv, bufs, k, after)
        return [(own, sib, landed) for (own, sib), landed in zip(local, bufs[k:])]

    dup = _mm_nt_blocks("d_up", dfo, wdn_full, tm=tm, tkb=1024, extra=(relu_up,),
                        epilogue=lambda acc, r: (acc * (2.0 * r.astype(F32)),), out_dtypes=(BF16,))[0]
    gw_down, gw_down_own = _mm_tn("dw_down", f, dfo, me_arr, m=f.shape[1], n=d, tma=512, tn=d, sharded="rows")
    red_down, tok = reduce_start("down", [gw_down], ())
    dn2 = _mm_nt_acc("d_n2", dup, wup_g, tm=tm // 2, tn=512, deps=[tok])
    gw_up, gw_up_own = _mm_tn("dw_up", n2, dup, me_arr, m=d, n=dup.shape[1], tma=512, tn=1024, sharded="cols")
    red_down, tok = reduce_middle("down", red_down, [gw_down_own], [dn2])
    red_up, tok = reduce_start("up", [gw_up], [tok])
    dh1, dmix, dg_pre_mlp, dg_post_mix = _mid_norm_bwd(dn2, h1, dh2, mix, g_pre_mlp, g_post_mix, deps=[tok])
    dya, dyb, dproj, db_a, db_b = _gate_backward(dmix, wo_full, gate_a, gate_b, ya, yb, proj.shape[1], tm // 2)
    db_gates = jnp.concatenate([db_a, db_b], axis=1)
    red_up, tok = reduce_middle("up", red_up, [gw_up_own], [dya])
    gw_o, gw_o_own = _mm_tn("dw_o", m_mix, dmix, me_arr, m=d, n=d, tma=d // N_DEV, tn=d, sharded="rows", deps=[tok])
    da3 = _mm_nt_acc("d_a3", dya, wpw_g, tm=tm, tn=512)
    gw_pw, gw_pw_own = _mm_tn("dw_pw", a3, dya, me_arr, m=dc, n=d, tma=512, tn=d, sharded="cols")
    dsb = _mm_nt_acc("d_s", dyb, wso_g, tm=tm, tn=512)
    gw_so, gw_so_own = _mm_tn("dw_so", s, dyb, me_arr, m=dc, n=d, tma=512, tn=d, sharded="cols")
    red_mix, tok = reduce_start("mix", [gw_pw, gw_so, gw_o], ())
    da1, dln_g, dln_b = _layer_norm_silu_bwd(da3, a1, conf_ln_g, conf_ln_b, deps=[tok])
    dproj, dcw, dcb, dsw = _conv_backward(dproj, proj, da1, dsb, cw_full, sw_full, dc)
    red_mix, tok = reduce_middle("mix", red_mix, [gw_pw_own, gw_so_own, gw_o_own], [dcb])
    in_cb = w_in.shape[2]
    half = d // 2
    red_in = []
    for part in range(2):
        gw, own = _mm_tn("dw_in%d" % part, n, dproj, me_arr, m=half, n=proj.shape[1], tma=512, tn=2 * in_cb,
                         sharded="cols", a_off=part * (half // 512), deps=[tok])
        state, tok = reduce_start("in%d" % part, [gw], ())
        red_in.append((state, own))
    for part in range(2):
        state, own = red_in[part]
        red_in[part], tok = reduce_middle("in%d" % part, state, [own], [tok])
    dn = _mm_nt_acc("d_n", dproj, win_g, tm=tm // 2, tn=512, deps=[tok])
    dh0, dg_pre_mix = _pre_norm_bwd(dn, h0, dh1, g_pre_mix)
    grad_x = dh0[N_META:t_real][None]

    vec_parts = [dg_pre_mix, db_gates, dcb, dln_g, dln_b, dg_post_mix, dg_pre_mlp, dg_post_mlp]
    packed = _pack_small(vec_parts, dh0[:N_META], dcw, dsw, loss_blk, me_arr)
    send, recv, bufs, tok = _remote_start("small_grads_ici_start", "gather_ici", [packed])
    vec_names = ["g_pre_mix", "b_gates", "conf_dw_b", "conf_ln_g", "conf_ln_b", "g_post_mix", "g_pre_mlp", "g_post_mlp"]
    env = locals()
    results = {}

    def update(nm, parts, deps=()):
        res = _adamw_shard("adamw_" + nm, env[nm][0], env["m_" + nm][0], env["v_" + nm][0], parts, me_arr, deps=deps)
        results[nm] = tuple(r[None] for r in res)
        return res[0]

    done = [update("w_down", reduce_finish("down", red_down, [tok]), deps=[tok])]
    done.append(update("w_up", reduce_finish("up", red_up, done)))
    bufs = _remote_wait("small_grads_ici_wait", "gather_ici", send, recv, bufs, 1, done)
    send, recv, bufs, tok = _remote_start("small_grads_d2d_start", "gather_d2d", bufs)
    for nm, pair in zip(["conf_w_pw", "short_w_out", "w_o"], reduce_finish("mix", red_mix, [tok])):
        done.append(update(nm, [pair], deps=[tok]))
    small_g, = _remote_wait("small_grads_d2d_wait", "gather_d2d", send, recv, bufs, 1, done)
    triple = lambda nm, sq: tuple(env[p + nm][0] if sq else env[p + nm] for p in ("", "m_", "v_"))
    small, loss = _small_update(small_g, me_arr, [triple(nm, False) for nm in vec_names],
                                triple("meta", False), triple("conf_dw_w", True), triple("short_dw_w", True))
    for nm, res in zip(vec_names + ["meta"], small[:len(vec_names) + 1]):
        results[nm] = res
    results["conf_dw_w"] = tuple(r[None] for r in small[-2])
    results["short_dw_w"] = tuple(r[None] for r in small[-1])
    update("w_in", [reduce_finish("in%d" % part, red_in[part], [small[0][0]])[0] for part in range(2)])

    order = ["meta", "g_pre_mix", "w_in", "b_gates", "conf_dw_w", "conf_dw_b", "conf_ln_g", "conf_ln_b", "conf_w_pw",
             "short_dw_w", "short_w_out", "w_o", "g_post_mix", "g_pre_mlp", "w_up", "w_down", "g_post_mlp"]
    return (loss, grad_x, *[results[nm][0] for nm in order], *[results[nm][1] for nm in order],
            *[results[nm][2] for nm in order], *[results[nm][3] for nm in order])
```

```python
import jax
import jax.numpy as jnp
from jax import lax
from jax.experimental import pallas as pl
from jax.experimental.pallas import tpu as pltpu

N_DEV = 8
N_META = 16
CONF_K = 31
SHORT_K = 3
RMS_EPS = 1e-6
LN_EPS = 1e-5
ADAM_LR = 0.001
ADAM_B1 = 0.9
ADAM_B2 = 0.999
ADAM_EPS = 1e-08
ADAM_WD = 0.01
ADAM_STEP = 10

LANE = 128
SUB = 8
ROW_TILE = 128
CONV_PAD = 32
CONV_CHUNK = 128
VMEM_LIMIT = 56 * 1024 * 1024

F32 = jnp.float32
BF16 = jnp.bfloat16
MESH = pl.DeviceIdType.MESH
ANY = pl.BlockSpec(memory_space=pl.ANY)
HBM_SPEC = pl.BlockSpec(memory_space=pltpu.HBM)
SEM_SPEC = pl.BlockSpec(memory_space=pltpu.SEMAPHORE)
EFFECT = pltpu.SideEffectType.DATAFLOW_SIDE_EFFECTING


def _params(n_axes):
    return pltpu.CompilerParams(dimension_semantics=("arbitrary",) * n_axes, vmem_limit_bytes=VMEM_LIMIT)


def _sigmoid(z):
    return 1.0 / (1.0 + jnp.exp(-z))


def _colsum8(v):
    r, c = v.shape
    return jnp.sum(v.reshape(r // SUB, SUB, c), axis=0)


def _position():
    x, y, c = lax.axis_index("x"), lax.axis_index("y"), lax.axis_index("c")
    return x, y, c


def _flat(p):
    return 4 * p[0] + 2 * p[1] + p[2]


def _all_gather(name, shards, deps=()):
    n, nd = len(shards), len(deps)

    def body(*refs):
        ins, outs = refs[:n], refs[n + nd:2 * n + nd]
        send_sems, recv_sems, local_sems = refs[2 * n + nd:]
        x, y, c = _position()
        me, sibling = (x, y, c), (x, y, 1 - c)
        chips = [(1 - x, y), (x, 1 - y), (1 - x, 1 - y)]

        def copy(q, k, block, to, src=None):
            dst = outs[q].at[_flat(block)]
            return pltpu.make_async_remote_copy(
                src_ref=dst if src is None else src, dst_ref=dst,
                send_sem=send_sems.at[q, k], recv_sem=recv_sems.at[q, k],
                device_id=to, device_id_type=MESH)

        mine = [pltpu.make_async_copy(ins[q], outs[q].at[_flat(me)], local_sems.at[q]) for q in range(n)]
        for cp in mine:
            cp.start()
        first = []
        for q in range(n):
            first.append(copy(q, 0, me, sibling, src=ins[q]))
            for j, chip in enumerate(chips):
                first.append(copy(q, 1 + j, me, (*chip, c), src=ins[q]))
        for cp in first:
            cp.start()
        passed = []
        for q in range(n):
            for j, chip in enumerate(chips):
                copy(q, 1 + j, (*chip, c), me).wait_recv()
                fwd = copy(q, 4 + j, (*chip, c), sibling)
                fwd.start()
                passed.append(fwd)
        for q in range(n):
            copy(q, 0, sibling, me).wait_recv()
            for j, chip in enumerate(chips):
                copy(q, 4 + j, (*chip, 1 - c), me).wait_recv()
        for cp in first + passed:
            cp.wait_send()
        for cp in mine:
            cp.wait()

    return pl.pallas_call(
        body, name=name,
        in_specs=[ANY] * (n + nd), out_specs=[ANY] * n,
        out_shape=[jax.ShapeDtypeStruct((N_DEV,) + s.shape, s.dtype) for s in shards],
        scratch_shapes=[pltpu.SemaphoreType.DMA((n, 7)), pltpu.SemaphoreType.DMA((n, 7)),
                        pltpu.SemaphoreType.DMA((n,))],
    )(*shards, *deps)


N_COPIES = {"gather_ici": 4, "gather_d2d": 3, "gather_direct": 3, "gather_relay": 3, "gather_diag": 1,
            "reduce_d2d": 4, "reduce_ici": 3}


def _copy_plan(kind):
    x, y, c = _position()
    me, sibling = (x, y, c), (x, y, 1 - c)
    chips = [(1 - x, y), (x, 1 - y), (1 - x, 1 - y)]
    if kind == "gather_ici":
        return [(_flat(me), _flat(me), sibling)] + [(_flat(me), _flat(me), (*ch, c)) for ch in chips]
    if kind == "gather_d2d":
        return [(_flat((*ch, c)), _flat((*ch, c)), sibling) for ch in chips]
    if kind == "gather_direct":
        return [(_flat(me), _flat(me), sibling)] + [(_flat(me), _flat(me), (*ch, c)) for ch in chips[:2]]
    if kind == "gather_relay":
        held, to = (x ^ (1 - c), y ^ c, c), (x ^ c, y ^ (1 - c), c)
        return [(_flat(held), _flat(held), to)] + [(_flat((*ch, c)), _flat((*ch, c)), sibling) for ch in chips[:2]]
    if kind == "gather_diag":
        return [(_flat((*chips[2], c)), _flat((*chips[2], c)), sibling)]
    if kind == "reduce_d2d":
        return [(2 * chip + (1 - c), chip, sibling) for chip in range(4)]
    return [(2 * ch[0] + ch[1], 2 * x + y, (*ch, c)) for ch in chips]


def _planned_copies(kind, srcs, dsts, send_sems, recv_sems):
    plan = _copy_plan(kind)
    return [pltpu.make_async_remote_copy(
        src_ref=src.at[s_slot], dst_ref=dst.at[d_slot],
        send_sem=send_sems.at[q * len(plan) + k], recv_sem=recv_sems.at[q * len(plan) + k],
        device_id=to, device_id_type=MESH)
        for q, (src, dst) in enumerate(zip(srcs, dsts)) for k, (s_slot, d_slot, to) in enumerate(plan)]


def _remote_start(name, kind, srcs, lands=None, deps=()):
    n = len(srcs)
    bufs = list(srcs) + ([] if lands is None else list(lands))
    nb, nd = len(bufs), len(deps)
    nsem = n * N_COPIES[kind]

    def body(*refs):
        ins = refs[:nb]
        send_sems, recv_sems = refs[nb + nd], refs[nb + nd + 1]
        token = refs[-1]
        for cp in _planned_copies(kind, ins[:n], ins[:n] if lands is None else ins[n:], send_sems, recv_sems):
            cp.start()
        token[...] = jnp.zeros_like(token)

    outs = pl.pallas_call(
        body, name=name,
        out_shape=(pltpu.SemaphoreType.DMA((nsem,)), pltpu.SemaphoreType.DMA((nsem,)),
                   *[pltpu.HBM(b.shape, b.dtype) for b in bufs], jax.ShapeDtypeStruct((SUB, LANE), F32)),
        in_specs=[HBM_SPEC] * nb + [ANY] * nd,
        out_specs=(SEM_SPEC, SEM_SPEC, *[HBM_SPEC] * nb, pl.BlockSpec(memory_space=pltpu.VMEM)),
        input_output_aliases={i: 2 + i for i in range(nb)},
        compiler_params=pltpu.CompilerParams(has_side_effects=EFFECT),
    )(*[pltpu.with_memory_space_constraint(b, pltpu.HBM) for b in bufs], *deps)
    return outs[0], outs[1], list(outs[2:2 + nb]), outs[-1]


def _remote_wait(name, kind, send_sems, recv_sems, bufs, n, after):
    nb, na = len(bufs), len(after)
    same = nb == n

    def body(*refs):
        ins = refs[:nb]
        sends, recvs = refs[nb], refs[nb + 1]
        for cp in _planned_copies(kind, ins[:n], ins[:n] if same else ins[n:], sends, recvs):
            cp.wait_send()
            cp.wait_recv()

    outs = pl.pallas_call(
        body, name=name,
        out_shape=[pltpu.HBM(b.shape, b.dtype) for b in bufs],
        in_specs=[HBM_SPEC] * nb + [SEM_SPEC, SEM_SPEC] + [ANY] * na,
        out_specs=[HBM_SPEC] * nb,
        input_output_aliases={i: i for i in range(nb)},
        compiler_params=pltpu.CompilerParams(has_side_effects=EFFECT),
    )(*bufs, send_sems, recv_sems, *after)
    return list(outs)


def _remote_pass_on(name, done, send_sems, recv_sems, bufs, after, nxt):
    nb, na = len(bufs), len(after)
    nsem = nb * N_COPIES[nxt]

    def body(*refs):
        ins = refs[:nb]
        new_sends, new_recvs = refs[nb + 2 + na], refs[nb + 3 + na]
        token = refs[-1]
        for cp in _planned_copies(done, ins, ins, refs[nb], refs[nb + 1]):
            cp.wait_send()
            cp.wait_recv()
        for cp in _planned_copies(nxt, ins, ins, new_sends, new_recvs):
            cp.start()
        token[...] = jnp.zeros_like(token)

    outs = pl.pallas_call(
        body, name=name,
        out_shape=(pltpu.SemaphoreType.DMA((nsem,)), pltpu.SemaphoreType.DMA((nsem,)),
                   *[pltpu.HBM(b.shape, b.dtype) for b in bufs], jax.ShapeDtypeStruct((SUB, LANE), F32)),
        in_specs=[HBM_SPEC] * nb + [SEM_SPEC, SEM_SPEC] + [ANY] * na,
        out_specs=(SEM_SPEC, SEM_SPEC, *[HBM_SPEC] * nb, pl.BlockSpec(memory_space=pltpu.VMEM)),
        input_output_aliases={i: 2 + i for i in range(nb)},
        compiler_params=pltpu.CompilerParams(has_side_effects=EFFECT),
    )(*bufs, send_sems, recv_sems, *after)
    return outs[0], outs[1], list(outs[2:2 + nb]), outs[-1]


def _mm_cols(name, a, w, *, tm, nb=1, epilogue=None, out_dtypes=(F32,)):
    t, k = a.shape
    nblk, _, cb = w.shape

    def body(a_ref, w_ref, *o_refs):
        av = a_ref[...]
        for b in range(nb):
            acc = jnp.dot(av, w_ref[b], preferred_element_type=F32)
            outs = (acc,) if epilogue is None else epilogue(acc)
            for o_ref, o in zip(o_refs, outs):
                o_ref[:, b * cb:(b + 1) * cb] = o.astype(o_ref.dtype)

    return pl.pallas_call(
        body, name=name, grid=(nblk // nb, t // tm),
        in_specs=[pl.BlockSpec((tm, k), lambda j, i: (i, 0)),
                  pl.BlockSpec((nb, k, cb), lambda j, i: (j, 0, 0))],
        out_specs=[pl.BlockSpec((tm, nb * cb), lambda j, i: (i, j)) for _ in out_dtypes],
        out_shape=[jax.ShapeDtypeStruct((t, nblk * cb), dt) for dt in out_dtypes],
        compiler_params=_params(2),
    )(a, w)


MXU_WIDTH = 256


def _mm_cols_pairs(name, a, w, *, tm):
    t, k = a.shape
    nblk, _, cb = w.shape
    main = cb // MXU_WIDTH * MXU_WIDTH
    tail = cb - main
    assert 2 * tail == MXU_WIDTH and nblk % 2 == 0

    def body(a_ref, w_ref, o_ref):
        av = a_ref[...]
        for b in range(2):
            o_ref[:, b * cb:b * cb + main] = jnp.dot(av, w_ref[b, :, 0:main], preferred_element_type=F32)
        tails = jnp.dot(av, jnp.concatenate([w_ref[0, :, main:cb], w_ref[1, :, main:cb]], axis=1),
                        preferred_element_type=F32)
        for b in range(2):
            o_ref[:, b * cb + main:(b + 1) * cb] = tails[:, b * tail:(b + 1) * tail]

    return pl.pallas_call(
        body, name=name, grid=(nblk // 2, t // tm),
        in_specs=[pl.BlockSpec((tm, k), lambda j, i: (i, 0)),
                  pl.BlockSpec((2, k, cb), lambda j, i: (j, 0, 0))],
        out_specs=pl.BlockSpec((tm, 2 * cb), lambda j, i: (i, j)),
        out_shape=jax.ShapeDtypeStruct((t, nblk * cb), F32),
        compiler_params=_params(2),
    )(a, w)


def _mm_rows(name, a, w2d, *, tm, tn):
    t, kf = a.shape
    n = w2d.shape[1]

    def body(a_ref, w_ref, o_ref):
        o_ref[...] = jnp.dot(a_ref[...], w_ref[...], preferred_element_type=F32)

    return pl.pallas_call(
        body, name=name, grid=(t // tm, n // tn),
        in_specs=[pl.BlockSpec((tm, kf), lambda i, j: (i, 0)),
                  pl.BlockSpec((kf, tn), lambda i, j: (0, j))],
        out_specs=pl.BlockSpec((tm, tn), lambda i, j: (i, j)),
        out_shape=jax.ShapeDtypeStruct((t, n), F32),
        compiler_params=_params(2),
    )(a, w2d)


def _mm_nt_acc(name, dy, w, *, tm, tn, col_off=0, deps=()):
    t = dy.shape[0]
    nblk, k, cb = w.shape

    main = cb // MXU_WIDTH * MXU_WIDTH

    def body(dy_ref, w_ref, *rest):
        nt = (((1,), (1,)), ((), ()))
        acc = None
        for b in range(nblk):
            d = lax.dot_general(dy_ref[:, b * cb:b * cb + main], w_ref[b, :, 0:main], nt, preferred_element_type=F32)
            acc = d if acc is None else acc + d
        if main < cb:
            dy_tails = jnp.concatenate([dy_ref[:, b * cb + main:(b + 1) * cb] for b in range(nblk)], axis=1)
            w_tails = jnp.concatenate([w_ref[b, :, main:cb] for b in range(nblk)], axis=1)
            acc = acc + lax.dot_general(dy_tails, w_tails, nt, preferred_element_type=F32)
        rest[-1][...] = acc

    return pl.pallas_call(
        body, name=name, grid=(t // tm, k // tn),
        in_specs=[pl.BlockSpec((tm, nblk * cb), lambda i, j: (i, col_off)),
                  pl.BlockSpec((nblk, tn, cb), lambda i, j: (0, j, 0))] + [ANY] * len(deps),
        out_specs=pl.BlockSpec((tm, tn), lambda i, j: (i, j)),
        out_shape=jax.ShapeDtypeStruct((t, k), F32),
        compiler_params=_params(2),
    )(dy, w, *deps)


def _mm_nt_blocks(name, dy, w2d, *, tm, tkb, extra=(), epilogue=None, out_dtypes=(F32,)):
    t, n = dy.shape
    kf = w2d.shape[0]
    ne = len(extra)

    def body(dy_ref, w_ref, *rest):
        acc = lax.dot_general(dy_ref[...], w_ref[...], (((1,), (1,)), ((), ())), preferred_element_type=F32)
        outs = (acc,) if epilogue is None else epilogue(acc, *[e[...] for e in rest[:ne]])
        for o_ref, o in zip(rest[ne:], outs):
            o_ref[...] = o.astype(o_ref.dtype)

    return pl.pallas_call(
        body, name=name, grid=(kf // tkb, t // tm),
        in_specs=[pl.BlockSpec((tm, n), lambda kb, i: (i, 0)),
                  pl.BlockSpec((tkb, n), lambda kb, i: (kb, 0))]
                 + [pl.BlockSpec((tm, tkb), lambda kb, i: (i, kb)) for _ in extra],
        out_specs=[pl.BlockSpec((tm, tkb), lambda kb, i: (i, kb)) for _ in out_dtypes],
        out_shape=[jax.ShapeDtypeStruct((t, kf), dt) for dt in out_dtypes],
        compiler_params=_params(2),
    )(dy, w2d, *extra)


def _mm_tn(name, a, b, me_arr, *, m, n, tma, tn, sharded, a_off=0, b_off=0, deps=()):
    t = a.shape[0]
    if sharded == "cols":
        cb = n // N_DEV
        nb, q = max(tn // cb, 1), max(cb // tn, 1)
        tw = tn // nb
        full_shape, own_shape = (N_DEV, m, cb), (m, cb)
        full_spec = pl.BlockSpec((nb, tma, tw), lambda i, j, me: (j // q, i, j % q))
    else:
        kb = m // N_DEV
        p = kb // tma
        nb, tw = 1, tn
        full_shape, own_shape = (m, n), (kb, n)
        full_spec = pl.BlockSpec((tma, tn), lambda i, j, me: (i, j))

    def body(me_ref, a_ref, b_ref, *rest):
        full_ref, own_ref, stage, sem = rest[len(deps):]
        i, j = pl.program_id(0), pl.program_id(1)
        acc = lax.dot_general(a_ref[...], b_ref[...], (((0,), (0,)), ((), ())), preferred_element_type=F32)
        for blk in range(nb):
            part = acc[:, blk * tw:(blk + 1) * tw]
            if sharded == "cols":
                full_ref[blk] = part.astype(BF16)
                owner, r0, c0 = (j // q) * nb + blk, i * tma, (j % q) * tw
            else:
                full_ref[...] = part.astype(BF16)
                owner, r0, c0 = i // p, (i % p) * tma, j * tn

            @pl.when(owner == me_ref[0])
            def _():
                stage[...] = part
                cp = pltpu.make_async_copy(
                    stage, own_ref.at[pl.ds(pl.multiple_of(r0, tma), tma), pl.ds(pl.multiple_of(c0, tw), tw)], sem)
                cp.start()
                cp.wait()

    full, own = pl.pallas_call(
        body, name=name,
        grid_spec=pltpu.PrefetchScalarGridSpec(
            num_scalar_prefetch=1, grid=(m // tma, n // tn),
            in_specs=[pl.BlockSpec((t, tma), lambda i, j, me: (0, a_off + i)),
                      pl.BlockSpec((t, tn), lambda i, j, me: (0, b_off + j))] + [ANY] * len(deps),
            out_specs=[full_spec, ANY],
            scratch_shapes=[pltpu.VMEM((tma, tw), F32), pltpu.SemaphoreType.DMA(())]),
        out_shape=[jax.ShapeDtypeStruct(full_shape, BF16), jax.ShapeDtypeStruct(own_shape, F32)],
        compiler_params=_params(2),
    )(me_arr, a, b, *deps)
    if sharded == "rows":
        full = full.reshape(N_DEV, m // N_DEV, n)
    return full, own


def _row_tile(t):
    return t // 8 if (t // 8) % 16 == 0 else ROW_TILE


def _row_call(name, body, t, row_ins, full_ins, row_outs, acc_outs, scratch=(), deps=()):
    tm = _row_tile(t)
    nin = len(row_ins) + len(full_ins)

    def without_deps(*refs):
        body(*refs[:nin], *refs[nin + len(deps):])

    return pl.pallas_call(
        without_deps, name=name, grid=(t // tm,),
        in_specs=[pl.BlockSpec((tm, a.shape[1]), lambda i: (i, 0)) for a in row_ins]
                 + [pl.BlockSpec(a.shape, lambda i: (0, 0)) for a in full_ins] + [ANY] * len(deps),
        out_specs=[pl.BlockSpec((tm, c), lambda i: (i, 0)) for c, _ in row_outs]
                  + [pl.BlockSpec((r, c), lambda i: (0, 0)) for r, c in acc_outs],
        out_shape=[jax.ShapeDtypeStruct((t, c), dt) for c, dt in row_outs]
                  + [jax.ShapeDtypeStruct((r, c), F32) for r, c in acc_outs],
        scratch_shapes=list(scratch),
        compiler_params=_params(1),
    )(*row_ins, *full_ins, *deps)


def _accumulate(ref, v):
    @pl.when(pl.program_id(0) == 0)
    def _():
        ref[...] = v

    @pl.when(pl.program_id(0) > 0)
    def _():
        ref[...] += v


def _rms(v):
    return lax.rsqrt(jnp.mean(v * v, axis=-1, keepdims=True) + RMS_EPS)


def _rms_bwd(dout, u, r, g):
    du = dout * g
    dx = r * (du - u * jnp.mean(du * u, axis=-1, keepdims=True))
    return dx, _colsum8(dout * u)


def _pre_norm(h0, g):
    t, d = h0.shape

    def body(h_ref, g_ref, n_ref):
        h = h_ref[...]
        n_ref[...] = (h * _rms(h) * g_ref[...]).astype(BF16)

    return _row_call("pre_norm", body, t, [h0], [g], [(d, BF16)], [])[0]


def _mix_post(m_mix, wo_full, h0, g_post, g_pre, deps=()):
    t, d = h0.shape
    tm = _row_tile(t)

    def body(m_ref, wo_ref, h0_ref, gp_ref, gq_ref, *rest):
        mix_ref, h1_ref, n2_ref = rest[len(deps):]
        mix_v = jnp.dot(m_ref[...], wo_ref[...], preferred_element_type=F32)
        mix_ref[...] = mix_v
        h1 = h0_ref[...] + mix_v * _rms(mix_v) * gp_ref[...]
        h1_ref[...] = h1
        n2_ref[...] = (h1 * _rms(h1) * gq_ref[...]).astype(BF16)

    tile = pl.BlockSpec((tm, d), lambda i: (i, 0))
    gain = pl.BlockSpec((1, d), lambda i: (0, 0))
    return pl.pallas_call(
        body, name="mix_post", grid=(t // tm,),
        in_specs=[tile, pl.BlockSpec((d, d), lambda i: (0, 0)), tile, gain, gain] + [ANY] * len(deps),
        out_specs=[tile, tile, tile],
        out_shape=[jax.ShapeDtypeStruct((t, d), F32), jax.ShapeDtypeStruct((t, d), F32),
                   jax.ShapeDtypeStruct((t, d), BF16)],
        compiler_params=_params(1),
    )(m_mix, wo_full, h0, g_post, g_pre, *deps)


def _loss_head(fo, h1, tgt, g_post_mlp, t_real):
    t, d = h1.shape
    tile = _row_tile(t)

    def body(fo_ref, h1_ref, tgt_ref, g_ref, dfo_ref, dh2_ref, dg_ref, loss_ref, lacc):
        i = pl.program_id(0)
        fo_v = fo_ref[...]
        g = g_ref[...]
        r = _rms(fo_v)
        u = fo_v * r
        h2 = h1_ref[...] + u * g
        row = i * tile + lax.broadcasted_iota(jnp.int32, (tile, 1), 0)
        valid = jnp.logical_and(row >= N_META, row < t_real)
        diff = jnp.where(valid, h2 - tgt_ref[...], 0.0)
        dh2 = diff * (1.0 / d)
        dh2_ref[...] = dh2
        dfo, dg = _rms_bwd(dh2, u, r, g)
        dfo_ref[...] = dfo.astype(BF16)
        _accumulate(dg_ref, dg)
        _accumulate(lacc, _colsum8(diff * diff))

        @pl.when(i == pl.num_programs(0) - 1)
        def _():
            loss_ref[...] = jnp.full((SUB, LANE), (0.5 / d) * jnp.sum(lacc[...]), F32)

    return _row_call("loss_head", body, t, [fo, h1, tgt], [g_post_mlp],
                     [(d, BF16), (d, F32)], [(SUB, d), (SUB, LANE)], scratch=[pltpu.VMEM((SUB, d), F32)])


def _mid_norm_bwd(dn2, h1, dh2, mix, g_pre_mlp, g_post_mix, deps=()):
    t, d = h1.shape

    def body(dn2_ref, h1_ref, dh2_ref, mix_ref, gq_ref, gp_ref, dh1_ref, dmix_ref, dgq_ref, dgp_ref):
        h1 = h1_ref[...]
        r3 = _rms(h1)
        dx, dgq = _rms_bwd(dn2_ref[...], h1 * r3, r3, gq_ref[...])
        dh1 = dh2_ref[...] + dx
        dh1_ref[...] = dh1
        mix_v = mix_ref[...]
        r2 = _rms(mix_v)
        dmix, dgp = _rms_bwd(dh1, mix_v * r2, r2, gp_ref[...])
        dmix_ref[...] = dmix.astype(BF16)
        _accumulate(dgq_ref, dgq)
        _accumulate(dgp_ref, dgp)

    return _row_call("mid_norm_bwd", body, t, [dn2, h1, dh2, mix], [g_pre_mlp, g_post_mix],
                     [(d, F32), (d, BF16)], [(SUB, d), (SUB, d)], deps=deps)


def _pre_norm_bwd(dn, h0, dh1, g_pre_mix, deps=()):
    t, d = h0.shape

    def body(dn_ref, h0_ref, dh1_ref, g_ref, dh0_ref, dg_ref):
        h0 = h0_ref[...]
        r = _rms(h0)
        dx, dg = _rms_bwd(dn_ref[...], h0 * r, r, g_ref[...])
        dh0_ref[...] = dh1_ref[...] + dx
        _accumulate(dg_ref, dg)

    return _row_call("pre_norm_bwd", body, t, [dn, h0, dh1], [g_pre_mix], [(d, F32)], [(SUB, d)], deps=deps)


def _layer_norm_silu(a1, ln_g, ln_b):
    t, c = a1.shape

    def body(a1_ref, g_ref, b_ref, a3_ref):
        a = a1_ref[...]
        mu = jnp.mean(a, axis=-1, keepdims=True)
        xc = a - mu
        rstd = lax.rsqrt(jnp.mean(xc * xc, axis=-1, keepdims=True) + LN_EPS)
        z = xc * rstd * g_ref[...] + b_ref[...]
        a3_ref[...] = (z * _sigmoid(z)).astype(BF16)

    return _row_call("layer_norm_silu", body, t, [a1], [ln_g, ln_b], [(c, BF16)], [])[0]


def _layer_norm_silu_bwd(da3, a1, ln_g, ln_b, deps=()):
    t, c = a1.shape

    def body(da3_ref, a1_ref, g_ref, b_ref, da1_ref, dg_ref, db_ref):
        a = a1_ref[...]
        g = g_ref[...]
        mu = jnp.mean(a, axis=-1, keepdims=True)
        xc = a - mu
        rstd = lax.rsqrt(jnp.mean(xc * xc, axis=-1, keepdims=True) + LN_EPS)
        xhat = xc * rstd
        z = xhat * g + b_ref[...]
        sg = _sigmoid(z)
        dz = da3_ref[...] * (sg * (1.0 + z * (1.0 - sg)))
        dxhat = dz * g
        da1_ref[...] = rstd * (dxhat - jnp.mean(dxhat, axis=-1, keepdims=True)
                               - xhat * jnp.mean(dxhat * xhat, axis=-1, keepdims=True))
        _accumulate(dg_ref, _colsum8(dz * xhat))
        _accumulate(db_ref, _colsum8(dz))

    return _row_call("layer_norm_silu_bwd", body, t, [da3, a1], [ln_g, ln_b], [(c, F32)], [(SUB, c), (SUB, c)], deps=deps)


def _branch_merge(a3, s, wpw, wso, proj, b_gates, d, deps=()):
    t, cols = proj.shape
    nblk, k, cb = wpw.shape
    w = 1024
    nh = d // w
    per = w // cb
    ga0 = (cols - 2 * d) // w
    tm = _row_tile(t)

    def body(a3_ref, s_ref, wpw_ref, wso_ref, *rest):
        pa_refs, pb_refs, bg_ref = rest[:nh], rest[nh:2 * nh], rest[2 * nh]
        ya_ref, yb_ref, ga_ref, gb_ref, m_ref = rest[2 * nh + 1 + len(deps):]
        a3v, sv = a3_ref[...], s_ref[...]
        for b in range(nblk):
            here = slice(b * cb, (b + 1) * cb)
            local = slice((b % per) * cb, (b % per + 1) * cb)
            ya = jnp.dot(a3v, wpw_ref[b], preferred_element_type=F32)
            yb = jnp.dot(sv, wso_ref[b], preferred_element_type=F32)
            ga = _sigmoid(pa_refs[b // per][:, local] + bg_ref[:, here])
            gb = _sigmoid(pb_refs[b // per][:, local] + bg_ref[:, d + b * cb:d + (b + 1) * cb])
            ya_ref[:, here] = ya.astype(BF16)
            yb_ref[:, here] = yb.astype(BF16)
            ga_ref[:, here] = ga.astype(BF16)
            gb_ref[:, here] = gb.astype(BF16)
            m_ref[:, here] = (ga * ya + gb * yb).astype(BF16)

    tile = pl.BlockSpec((tm, d), lambda i: (i, 0))
    return pl.pallas_call(
        body, name="branch_merge", grid=(t // tm,),
        in_specs=[pl.BlockSpec((tm, k), lambda i: (i, 0)), pl.BlockSpec((tm, k), lambda i: (i, 0)),
                  pl.BlockSpec((nblk, k, cb), lambda i: (0, 0, 0)), pl.BlockSpec((nblk, k, cb), lambda i: (0, 0, 0))]
                 + [pl.BlockSpec((tm, w), lambda i, h=h: (i, ga0 + h)) for h in range(2 * nh)]
                 + [pl.BlockSpec((1, 2 * d), lambda i: (0, 0))] + [ANY] * len(deps),
        out_specs=[tile] * 5,
        out_shape=[jax.ShapeDtypeStruct((t, d), BF16)] * 5,
        compiler_params=_params(1),
    )(a3, s, wpw, wso, *([proj] * (2 * nh)), b_gates, *deps)


def _gate_backward(dmix, wo_full, ga, gb, ya, yb, cols, tm, deps=()):
    t, d = ya.shape
    w = 1024
    nh = d // w
    ga0 = (cols - 2 * d) // w

    def body(dmix_ref, wo_ref, ga_ref, gb_ref, ya_ref, yb_ref, *rest):
        dya_ref, dyb_ref, dp_ref, dba_ref, dbb_ref, stage, sems = rest[len(deps):]
        h, i = pl.program_id(0), pl.program_id(1)
        dm = lax.dot_general(dmix_ref[...], wo_ref[...], (((1,), (1,)), ((), ())), preferred_element_type=F32)
        ga = ga_ref[...].astype(F32)
        gb = gb_ref[...].astype(F32)
        dya_ref[...] = (dm * ga).astype(BF16)
        dyb_ref[...] = (dm * gb).astype(BF16)
        dpa = dm * ya_ref[...].astype(F32) * ga * (1.0 - ga)
        dpb = dm * yb_ref[...].astype(F32) * gb * (1.0 - gb)
        stage[0] = dpa.astype(BF16)
        stage[1] = dpb.astype(BF16)
        rows = pl.ds(pl.multiple_of(i * tm, tm), tm)
        copies = [pltpu.make_async_copy(
            stage.at[g], dp_ref.at[rows, pl.ds(pl.multiple_of((ga0 + g * nh + h) * w, w), w)], sems.at[g])
            for g in range(2)]
        for cp in copies:
            cp.start()

        @pl.when(i == 0)
        def _():
            dba_ref[...] = _colsum8(dpa)
            dbb_ref[...] = _colsum8(dpb)

        @pl.when(i > 0)
        def _():
            dba_ref[...] += _colsum8(dpa)
            dbb_ref[...] += _colsum8(dpb)

        for cp in copies:
            cp.wait()

    tile = pl.BlockSpec((tm, w), lambda h, i: (i, h))
    return pl.pallas_call(
        body, name="gate_backward", grid=(nh, t // tm),
        in_specs=[pl.BlockSpec((tm, d), lambda h, i: (i, 0)),
                  pl.BlockSpec((w, d), lambda h, i: (h, 0)),
                  tile, tile, tile, tile] + [ANY] * len(deps),
        out_specs=[tile, tile, ANY,
                   pl.BlockSpec((SUB, w), lambda h, i: (0, h)),
                   pl.BlockSpec((SUB, w), lambda h, i: (0, h))],
        out_shape=[jax.ShapeDtypeStruct((t, d), BF16), jax.ShapeDtypeStruct((t, d), BF16),
                   jax.ShapeDtypeStruct((t, cols), BF16),
                   jax.ShapeDtypeStruct((SUB, d), F32), jax.ShapeDtypeStruct((SUB, d), F32)],
        scratch_shapes=[pltpu.VMEM((2, tm, w), BF16), pltpu.SemaphoreType.DMA((2,))],
        compiler_params=_params(2),
    )(dmix, wo_full, ga, gb, ya, yb, *deps)


def _shifted_views(win, offsets):
    n = win.shape[0]
    rotated = {}
    views = {}
    for o in offsets:
        q, r = divmod(o, SUB)
        if r not in rotated:
            rotated[r] = win if r == 0 else pltpu.roll(win, n - r, 0)
        views[o] = rotated[r][q * SUB:q * SUB + CONV_CHUNK]
    return views


def _causal_views(xp_ref, ntap, r0):
    win = xp_ref[pl.ds(r0, CONV_CHUNK + CONV_PAD), :]
    views = _shifted_views(win, [CONV_PAD - (ntap - 1 - k) for k in range(ntap)])
    return [views[CONV_PAD - (ntap - 1 - k)] for k in range(ntap)]


def _causal_conv(xp_ref, w_ref, ntap, r0):
    acc = None
    for k, shifted in enumerate(_causal_views(xp_ref, ntap, r0)):
        term = w_ref[k:k + 1, :] * shifted
        acc = term if acc is None else acc + term
    return acc


def _anticausal_conv(xp_ref, w_ref, ntap, r0):
    win = xp_ref[pl.ds(pl.multiple_of(CONV_PAD + r0, CONV_PAD), CONV_CHUNK + CONV_PAD), :]
    views = _shifted_views(win, [ntap - 1 - k for k in range(ntap)])
    acc = None
    for k in range(ntap):
        term = w_ref[k:k + 1, :] * views[ntap - 1 - k]
        acc = term if acc is None else acc + term
    return acc


def _conv_weight_grad(dw_ref, d_chunk, xp_ref, ntap, r0):
    for k, shifted in enumerate(_causal_views(xp_ref, ntap, r0)):
        dw_ref[k * SUB:(k + 1) * SUB, :] += _colsum8(d_chunk * shifted)


def _zero_pads(ref, t):
    ref[0:CONV_PAD, :] = jnp.zeros((CONV_PAD, LANE), F32)
    ref[CONV_PAD + t:CONV_PAD + t + CONV_PAD, :] = jnp.zeros((CONV_PAD, LANE), F32)


def _for_chunks(t, fn):
    def step(idx, carry):
        fn(pl.multiple_of(idx * CONV_CHUNK, CONV_CHUNK))
        return carry

    lax.fori_loop(0, t // CONV_CHUNK, step, 0)


def _conv_forward(proj, conf_w, conf_b, short_w, dc, deps=()):
    t = proj.shape[0]
    nc = dc // LANE

    def body(av_ref, ag_ref, bg_ref, cg_ref, v_ref, cw_ref, cb_ref, sw_ref, *rest):
        a1_ref, s_ref, xa, xb = rest[len(deps):]
        _zero_pads(xa, t)
        _zero_pads(xb, t)
        xa[CONV_PAD:CONV_PAD + t, :] = av_ref[...] * _sigmoid(ag_ref[...])
        xb[CONV_PAD:CONV_PAD + t, :] = cg_ref[...] * v_ref[...]

        def chunk(r0):
            rs = pl.ds(r0, CONV_CHUNK)
            a1_ref[rs, :] = _causal_conv(xa, cw_ref, CONF_K, r0) + cb_ref[...]
            s_ref[rs, :] = (bg_ref[rs, :] * _causal_conv(xb, sw_ref, SHORT_K, r0)).astype(BF16)

        _for_chunks(t, chunk)

    col = lambda g: pl.BlockSpec((t, LANE), lambda c, g=g: (0, g * nc + c))
    return pl.pallas_call(
        body, name="conv_forward", grid=(nc,),
        in_specs=[col(0), col(1), col(2), col(3), col(4),
                  pl.BlockSpec((CONF_K, LANE), lambda c: (0, c)),
                  pl.BlockSpec((1, LANE), lambda c: (0, c)),
                  pl.BlockSpec((SHORT_K, LANE), lambda c: (0, c))] + [ANY] * len(deps),
        out_specs=[pl.BlockSpec((t, LANE), lambda c: (0, c)), pl.BlockSpec((t, LANE), lambda c: (0, c))],
        out_shape=[jax.ShapeDtypeStruct((t, dc), F32), jax.ShapeDtypeStruct((t, dc), BF16)],
        scratch_shapes=[pltpu.VMEM((t + 2 * CONV_PAD, LANE), F32), pltpu.VMEM((t + 2 * CONV_PAD, LANE), F32)],
        compiler_params=_params(1),
    )(proj, proj, proj, proj, proj, conf_w, conf_b, short_w, *deps)


def _conv_backward(dproj, proj, da1, ds, conf_w, short_w, dc):
    t = proj.shape[0]
    nc = dc // LANE

    def body(dp_in, av_ref, ag_ref, bg_ref, cg_ref, v_ref, da1_ref, ds_ref, cw_ref, sw_ref,
             dp_ref, dcw_ref, dcb_ref, dsw_ref, xa, xb, da, db, stage, sems):
        del dp_in
        c = pl.program_id(0)
        for ref in (xa, xb, da, db):
            _zero_pads(ref, t)
        xa[CONV_PAD:CONV_PAD + t, :] = av_ref[...] * _sigmoid(ag_ref[...])
        xb[CONV_PAD:CONV_PAD + t, :] = cg_ref[...] * v_ref[...]
        da[CONV_PAD:CONV_PAD + t, :] = da1_ref[...]
        dcw_ref[...] = jnp.zeros(dcw_ref.shape, F32)
        dsw_ref[...] = jnp.zeros(dsw_ref.shape, F32)
        dcb_ref[...] = jnp.zeros(dcb_ref.shape, F32)

        def through_gate(r0):
            rs = pl.ds(r0, CONV_CHUNK)
            ds_c = ds_ref[rs, :]
            stage[2, rs, :] = (ds_c * _causal_conv(xb, sw_ref, SHORT_K, r0)).astype(BF16)
            db[pl.ds(pl.multiple_of(CONV_PAD + r0, CONV_PAD), CONV_CHUNK), :] = ds_c * bg_ref[rs, :]

        _for_chunks(t, through_gate)

        def through_convs(r0):
            rs = pl.ds(r0, CONV_CHUNK)
            da0 = _anticausal_conv(da, cw_ref, CONF_K, r0)
            sg = _sigmoid(ag_ref[rs, :])
            stage[0, rs, :] = (da0 * sg).astype(BF16)
            stage[1, rs, :] = (da0 * av_ref[rs, :] * sg * (1.0 - sg)).astype(BF16)
            dcv = _anticausal_conv(db, sw_ref, SHORT_K, r0)
            stage[3, rs, :] = (dcv * v_ref[rs, :]).astype(BF16)
            stage[4, rs, :] = (dcv * cg_ref[rs, :]).astype(BF16)
            da1_c = da1_ref[rs, :]
            _conv_weight_grad(dcw_ref, da1_c, xa, CONF_K, r0)
            _conv_weight_grad(dsw_ref, ds_ref[rs, :] * bg_ref[rs, :], xb, SHORT_K, r0)
            dcb_ref[...] += _colsum8(da1_c)

        _for_chunks(t, through_convs)
        copies = [pltpu.make_async_copy(
            stage.at[g], dp_ref.at[:, pl.ds(pl.multiple_of((g * nc + c) * LANE, LANE), LANE)], sems.at[g])
            for g in range(5)]
        for cp in copies:
            cp.start()
        for cp in copies:
            cp.wait()

    col = lambda g: pl.BlockSpec((t, LANE), lambda c, g=g: (0, g * nc + c))
    blk = pl.BlockSpec((t, LANE), lambda c: (0, c))
    return pl.pallas_call(
        body, name="conv_backward", grid=(nc,),
        in_specs=[ANY, col(0), col(1), col(2), col(3), col(4), blk, blk,
                  pl.BlockSpec((CONF_K, LANE), lambda c: (0, c)),
                  pl.BlockSpec((SHORT_K, LANE), lambda c: (0, c))],
        out_specs=[ANY,
                   pl.BlockSpec((CONF_K * SUB, LANE), lambda c: (0, c)),
                   pl.BlockSpec((SUB, LANE), lambda c: (0, c)),
                   pl.BlockSpec((SHORT_K * SUB, LANE), lambda c: (0, c))],
        out_shape=[jax.ShapeDtypeStruct(dproj.shape, dproj.dtype),
                   jax.ShapeDtypeStruct((CONF_K * SUB, dc), F32),
                   jax.ShapeDtypeStruct((SUB, dc), F32),
                   jax.ShapeDtypeStruct((SHORT_K * SUB, dc), F32)],
        scratch_shapes=[pltpu.VMEM((t + 2 * CONV_PAD, LANE), F32)] * 4
                       + [pltpu.VMEM((5, t, LANE), BF16), pltpu.SemaphoreType.DMA((5,))],
        input_output_aliases={0: 0},
        compiler_params=_params(1),
    )(dproj, proj, proj, proj, proj, proj, da1, ds, conf_w, short_w)


def _adamw_math(w, g, m, v):
    m = ADAM_B1 * m + (1.0 - ADAM_B1) * g
    v = ADAM_B2 * v + (1.0 - ADAM_B2) * (g * g)
    m_hat = m / (1.0 - ADAM_B1 ** ADAM_STEP)
    v_hat = v / (1.0 - ADAM_B2 ** ADAM_STEP)
    delta = -ADAM_LR * (m_hat / (jnp.sqrt(v_hat) + ADAM_EPS) + ADAM_WD * w)
    return delta, m, v


def _cast_into_slot(name, w, me_arr, deps=()):
    r, c = w.shape
    tr = 256

    def body(me_ref, w_ref, *rest):
        del me_ref
        rest[-1][0] = w_ref[...].astype(BF16)

    return pl.pallas_call(
        body, name=name,
        grid_spec=pltpu.PrefetchScalarGridSpec(
            num_scalar_prefetch=1, grid=(r // tr,),
            in_specs=[pl.BlockSpec((tr, c), lambda i, me: (i, 0))] + [ANY] * len(deps),
            out_specs=pl.BlockSpec((1, tr, c), lambda i, me: (me[0], i, 0))),
        out_shape=jax.ShapeDtypeStruct((N_DEV, r, c), BF16),
        compiler_params=_params(1),
    )(me_arr, w, *deps)


def _chip_sum(name, full, from_sibling, me_arr):
    _, r, c = full.shape
    tr = min(r, 512)

    def body(me_ref, full_ref, sib_ref, sums_ref):
        del me_ref
        sums_ref[0] = (full_ref[0].astype(F32) + sib_ref[0].astype(F32)).astype(BF16)

    other = lambda k, me: (me[0] // 2 + 1 + k) % 4
    return pl.pallas_call(
        body, name=name,
        grid_spec=pltpu.PrefetchScalarGridSpec(
            num_scalar_prefetch=1, grid=(r // tr, 3),
            in_specs=[pl.BlockSpec((1, tr, c), lambda i, k, me: (2 * other(k, me) + me[0] % 2, i, 0)),
                      pl.BlockSpec((1, tr, c), lambda i, k, me: (other(k, me), i, 0))],
            out_specs=pl.BlockSpec((1, tr, c), lambda i, k, me: (other(k, me), i, 0))),
        out_shape=jax.ShapeDtypeStruct((4, r, c), BF16),
        compiler_params=_params(2),
    )(me_arr, full, from_sibling)


def _adamw_shard(name, w, m, v, parts, me_arr, deps=()):
    r, c = w.shape
    tr = min(256, r // len(parts))
    np_ = len(parts)
    per = r // np_ // tr

    def body(me_ref, w_ref, m_ref, v_ref, *rest):
        g_out, d_out, m_out, v_out = rest[5 * np_ + len(deps):]
        g = None
        for p in range(np_):
            gp = rest[5 * p][...]
            for l_ref in rest[5 * p + 1:5 * p + 5]:
                gp = gp + l_ref[0].astype(F32)
            g = gp if g is None else jnp.where(pl.program_id(0) // per == p, gp, g)
        delta, m_new, v_new = _adamw_math(w_ref[...], g, m_ref[...], v_ref[...])
        g_out[...] = g
        d_out[...] = delta
        m_out[...] = m_new
        v_out[...] = v_new

    tile = pl.BlockSpec((tr, c), lambda i, me: (i, 0))
    part_specs, part_args = [], []
    for p, (g_own, from_sibling, landed) in enumerate(parts):
        row = lambda i, p=p: jnp.clip(i - p * per, 0, per - 1)
        part_specs.append(pl.BlockSpec((tr, c), lambda i, me, row=row: (row(i), 0)))
        part_specs += [pl.BlockSpec((1, tr, c), lambda i, me, k=k, row=row: ((me[0] // 2 + k) % 4, row(i), 0))
                       for k in range(4)]
        part_args += [g_own, from_sibling, landed, landed, landed]
    return pl.pallas_call(
        body, name=name,
        grid_spec=pltpu.PrefetchScalarGridSpec(
            num_scalar_prefetch=1, grid=(r // tr,),
            in_specs=[tile] * 3 + part_specs + [ANY] * len(deps), out_specs=[tile] * 4),
        out_shape=[jax.ShapeDtypeStruct((r, c), F32)] * 4,
        compiler_params=_params(1),
    )(me_arr, w, m, v, *part_args, *deps)


SMALL_W = 1024
VEC_ROWS = 16
LOSS_ROW = 15
META_ROW0 = 16
CONF_ROW0 = 64
SHORT_ROW0 = 96
SMALL_ROWS = 104


def _pack_small(vec_parts, dmeta, dcw, dsw, loss_blk, me_arr):
    widths = [p.shape[1] for p in vec_parts]
    nv = len(vec_parts)

    def body(me_ref, *refs):
        del me_ref
        parts, (dmeta_ref, dcw_ref, dsw_ref, loss_ref, out_ref) = refs[:nv], refs[nv:]
        out_ref[0] = jnp.zeros((SMALL_ROWS, SMALL_W), F32)
        out_ref[0, LOSS_ROW:LOSS_ROW + 1, 0:LANE] = loss_ref[0:1, :]
        row = 0
        for p_ref, wd in zip(parts, widths):
            s = jnp.sum(p_ref[...], axis=0, keepdims=True)
            for h in range(wd // SMALL_W):
                out_ref[0, row:row + 1, :] = s[:, h * SMALL_W:(h + 1) * SMALL_W]
                row += 1
        for h in range(dmeta_ref.shape[1] // SMALL_W):
            out_ref[0, META_ROW0 + h * N_META:META_ROW0 + (h + 1) * N_META, :] = dmeta_ref[:, h * SMALL_W:(h + 1) * SMALL_W]
        for k in range(CONF_K):
            out_ref[0, CONF_ROW0 + k:CONF_ROW0 + k + 1, :] = jnp.sum(dcw_ref[k * SUB:(k + 1) * SUB, :], axis=0, keepdims=True)
        for k in range(SHORT_K):
            out_ref[0, SHORT_ROW0 + k:SHORT_ROW0 + k + 1, :] = jnp.sum(dsw_ref[k * SUB:(k + 1) * SUB, :], axis=0, keepdims=True)

    ins = [*vec_parts, dmeta, dcw, dsw, loss_blk]
    return pl.pallas_call(
        body, name="pack_small",
        grid_spec=pltpu.PrefetchScalarGridSpec(
            num_scalar_prefetch=1, grid=(1,),
            in_specs=[pl.BlockSpec(a.shape, lambda i, me: (0, 0)) for a in ins],
            out_specs=pl.BlockSpec((1, SMALL_ROWS, SMALL_W), lambda i, me: (me[0], 0, 0))),
        out_shape=jax.ShapeDtypeStruct((N_DEV, SMALL_ROWS, SMALL_W), F32),
        compiler_params=_params(1),
    )(me_arr, *ins)


def _small_update(gathered, me_arr, vec_params, meta_p, conf_p, short_p):
    widths = [p[0].shape[1] for p in vec_params]
    nv = len(vec_params)
    mcols = meta_p[0].shape[1]
    per_row = SMALL_W // mcols

    def body(me_ref, gv_ref, gm_ref, gc_ref, gs_ref, *rest):
        del me_ref
        ins, outs = rest[:3 * (nv + 3)], rest[3 * (nv + 3):]

        def total(ref, r0, rows):
            s = ref[0, r0:r0 + rows, :]
            for dev in range(1, N_DEV):
                s = s + ref[dev, r0:r0 + rows, :]
            return s

        grads = []
        row = 0
        for wd in widths:
            pieces = [total(gv_ref, row + h, 1) for h in range(wd // SMALL_W)]
            grads.append(pieces[0] if len(pieces) == 1 else jnp.concatenate(pieces, axis=1))
            row += len(pieces)
        grads.append(total(gm_ref, 0, N_META))
        grads.append(total(gc_ref, 0, CONF_K))
        grads.append(total(gs_ref, 0, SHORT_K))
        loss = gv_ref[0, LOSS_ROW:LOSS_ROW + 1, 0:LANE]
        for dev in range(1, N_DEV):
            loss = loss + gv_ref[dev, LOSS_ROW:LOSS_ROW + 1, 0:LANE]
        outs[-1][...] = loss
        for idx, g in enumerate(grads):
            w_ref, m_ref, v_ref = ins[3 * idx:3 * idx + 3]
            delta, m_new, v_new = _adamw_math(w_ref[...], g, m_ref[...], v_ref[...])
            g_out, d_out, m_out, v_out = outs[4 * idx:4 * idx + 4]
            g_out[...] = g
            d_out[...] = delta
            m_out[...] = m_new
            v_out[...] = v_new

    params = list(vec_params) + [meta_p, conf_p, short_p]
    flat = [a for p in params for a in p]
    whole = lambda a: pl.BlockSpec(a.shape, lambda i, me: (0,) * a.ndim)
    outs = pl.pallas_call(
        body, name="small_update",
        grid_spec=pltpu.PrefetchScalarGridSpec(
            num_scalar_prefetch=1, grid=(1,),
            in_specs=[pl.BlockSpec((N_DEV, VEC_ROWS, SMALL_W), lambda i, me: (0, 0, 0)),
                      pl.BlockSpec((N_DEV, N_META, mcols),
                                   lambda i, me: (0, META_ROW0 // N_META + me[0] // per_row, me[0] % per_row)),
                      pl.BlockSpec((N_DEV, 32, LANE), lambda i, me: (0, CONF_ROW0 // 32, me[0])),
                      pl.BlockSpec((N_DEV, SUB, LANE), lambda i, me: (0, SHORT_ROW0 // SUB, me[0]))]
                     + [whole(a) for a in flat],
            out_specs=[whole(p[0]) for p in params for _ in range(4)]
                      + [pl.BlockSpec((1, LANE), lambda i, me: (0, 0))]),
        out_shape=[jax.ShapeDtypeStruct(p[0].shape, F32) for p in params for _ in range(4)]
                  + [jax.ShapeDtypeStruct((1, LANE), F32)],
        compiler_params=_params(1),
    )(me_arr, gathered, gathered, gathered, gathered, *flat)
    return [tuple(outs[4 * i:4 * i + 4]) for i in range(len(params))], outs[-1][0, 0]


def kernel(x, meta, g_pre_mix, w_in, b_gates, conf_dw_w, conf_dw_b, conf_ln_g, conf_ln_b, conf_w_pw, short_dw_w, short_w_out, w_o, g_post_mix, g_pre_mlp, w_up, w_down, g_post_mlp, loss_target, m_meta, m_g_pre_mix, m_w_in, m_b_gates, m_conf_dw_w, m_conf_dw_b, m_conf_ln_g, m_conf_ln_b, m_conf_w_pw, m_short_dw_w, m_short_w_out, m_w_o, m_g_post_mix, m_g_pre_mlp, m_w_up, m_w_down, m_g_post_mlp, v_meta, v_g_pre_mix, v_w_in, v_b_gates, v_conf_dw_w, v_conf_dw_b, v_conf_ln_g, v_conf_ln_b, v_conf_w_pw, v_short_dw_w, v_short_w_out, v_w_o, v_g_post_mix, v_g_pre_mlp, v_w_up, v_w_down, v_g_post_mlp):
    seq, d = x.shape[1], x.shape[2]
    dc = conf_w_pw.shape[1]
    t_real = N_META + seq
    t = -(-t_real // ROW_TILE) * ROW_TILE
    tm = t // 2
    assert tm % 16 == 0 and d % 1024 == 0 and dc % 1024 == 0
    x_idx, y_idx, c_idx = _position()
    me_arr = jnp.reshape(4 * x_idx + 2 * y_idx + c_idx, (1,)).astype(jnp.int32)

    big = [w_in[0], conf_w_pw[0], short_w_out[0], w_o[0], w_up[0], w_down[0]]
    big_names = ["w_in", "conf_w_pw", "short_w_out", "w_o", "w_up", "w_down"]
    groups = [[0], [1, 2, 3], [4], [5]]
    slots, deps = [], []
    for g, idxs in enumerate(groups):
        slots.append([_cast_into_slot("cast_" + big_names[i], big[i], me_arr, deps=deps) for i in idxs])
        if g == 0:
            direct0 = _remote_start("gather0_direct_start", "gather_direct", slots[0])
            deps = [direct0[3]]
    casts = [sl for group in slots[1:] for sl in group]
    meta_g, cw_g, sw_g = _all_gather("gather_small_params", [meta, conf_dw_w[0], short_dw_w[0]], deps=casts)

    def start_direct(g, deps):
        send, recv, bufs, tok = _remote_start("gather%d_direct_start" % g, "gather_direct", slots[g], deps=deps)
        return (send, recv, bufs), tok

    def relay(g, state, after):
        send, recv, bufs, tok = _remote_pass_on("gather%d_relay" % g, "gather_direct", *state, after, "gather_relay")
        return (send, recv, bufs), tok

    def gathered(g, state, after):
        send, recv, bufs, tok = _remote_pass_on("gather%d_diag" % g, "gather_relay", *state, after, "gather_diag")
        return _remote_wait("gather%d_diag_wait" % g, "gather_diag", send, recv, bufs, len(bufs), [tok])

    unshard =lambda g: jnp.transpose(g, (1, 0, 2)).reshape(g.shape[1], -1)
    meta_full, cw_full, sw_full = unshard(meta_g), unshard(cw_g), unshard(sw_g)

    relay0, tok = relay(0, direct0[:3], [meta_g])
    zrows = jnp.zeros((t - t_real, d), F32) + tok[0, 0] * 0.0
    h0 = jnp.concatenate([meta_full, x[0], zrows], axis=0)
    tgt = jnp.concatenate([jnp.zeros((N_META, d), F32), loss_target[0], zrows], axis=0)
    n = _pre_norm(h0, g_pre_mix)
    direct1, tok = start_direct(1, [tok])
    direct2, tok = start_direct(2, [tok])
    win_g, = gathered(0, relay0, [tok, n])
    proj = _mm_cols_pairs("proj", n, win_g, tm=tm // 2)
    relay1, tok = relay(1, direct1, [proj])
    direct3, tok = start_direct(3, [tok])
    a1, s = _conv_forward(proj, cw_full, conf_dw_b, sw_full, dc, deps=[tok])
    a3 = _layer_norm_silu(a1, conf_ln_g, conf_ln_b)
    wpw_g, wso_g, wo_g = gathered(1, relay1, [a3])
    wo_full = wo_g.reshape(d, d)
    ya, yb, gate_a, gate_b, m_mix = _branch_merge(a3, s, wpw_g, wso_g, proj, b_gates, d)
    relay2, tok = relay(2, direct2, [m_mix])
    mix, h1, n2 = _mix_post(m_mix, wo_full, h0, g_post_mix, g_pre_mlp, deps=[tok])
    relay3, tok = relay(3, direct3, [n2])
    wup_g, = gathered(2, relay2, [tok])

    def up_epilogue(acc):
        r = jnp.maximum(acc, 0.0)
        return r * r, r

    f, relu_up = _mm_cols("mlp_up", n2, wup_g, tm=tm, epilogue=up_epilogue, out_dtypes=(BF16, BF16))
    wdn_g, = gathered(3, relay3, [f])
    wdn_full = wdn_g.reshape(-1, d)
    fo = _mm_rows("mlp_down", f, wdn_full, tm=tm // 2, tn=512)
    dfo, dh2, dg_post_mlp, loss_blk = _loss_head(fo, h1, tgt, g_post_mlp, t_real)

    def reduce_start(tag, fulls, deps):
        lands = [lax.empty((4,) + g.shape[1:], BF16) for g in fulls]
        send, recv, bufs, tok = _remote_start("reduce_%s_d2d_start" % tag, "reduce_d2d", fulls, lands, deps=deps)
        return (send, recv, bufs), tok

    def reduce_middle(tag, state, owns, after):
        send, recv, bufs = state
        k = len(owns)
        bufs = _remote_wait("reduce_%s_d2d_wait" % tag, "reduce_d2d", send, recv, bufs, k, after)
        from_sibling = bufs[k:]
        sums = [_chip_sum("chip_sum_%s%d" % (tag, i), bufs[i], from_sibling[i], me_arr) for i in range(k)]
        lands = [lax.empty(sm.shape, BF16) for sm in sums]
        send, recv, bufs, tok = _remote_start("reduce_%s_ici_start" % tag, "reduce_ici", sums, lands)
        return (send, recv, bufs, list(zip(owns, from_sibling))), tok

    def reduce_finish(tag, state, after):
        send, recv, bufs, local = state
        k = len(local)
        bufs = _remote_wait("reduce_%s_ici_wait" % tag, "reduce_ici", send, recv, bufs, k, after)
        return [(own, sib, landed) for (own, sib), landed in zip(local, bufs[k:])]

    dup = _mm_nt_blocks("d_up", dfo, wdn_full, tm=tm, tkb=1024, extra=(relu_up,),
                        epilogue=lambda acc, r: (acc * (2.0 * r.astype(F32)),), out_dtypes=(BF16,))[0]
    gw_down, gw_down_own = _mm_tn("dw_down", f, dfo, me_arr, m=f.shape[1], n=d, tma=512, tn=d, sharded="rows")
    red_down, tok = reduce_start("down", [gw_down], ())
    dn2 = _mm_nt_acc("d_n2", dup, wup_g, tm=tm // 2, tn=512, deps=[tok])
    gw_up, gw_up_own = _mm_tn("dw_up", n2, dup, me_arr, m=d, n=dup.shape[1], tma=512, tn=1024, sharded="cols")
    red_down, tok = reduce_middle("down", red_down, [gw_down_own], [dn2])
    red_up, tok = reduce_start("up", [gw_up], [tok])
    dh1, dmix, dg_pre_mlp, dg_post_mix = _mid_norm_bwd(dn2, h1, dh2, mix, g_pre_mlp, g_post_mix, deps=[tok])
    dya, dyb, dproj, db_a, db_b = _gate_backward(dmix, wo_full, gate_a, gate_b, ya, yb, proj.shape[1], tm // 2)
    db_gates = jnp.concatenate([db_a, db_b], axis=1)
    red_up, tok = reduce_middle("up", red_up, [gw_up_own], [dya])
    gw_o, gw_o_own = _mm_tn("dw_o", m_mix, dmix, me_arr, m=d, n=d, tma=d // N_DEV, tn=d, sharded="rows", deps=[tok])
    da3 = _mm_nt_acc("d_a3", dya, wpw_g, tm=tm, tn=512)
    gw_pw, gw_pw_own = _mm_tn("dw_pw", a3, dya, me_arr, m=dc, n=d, tma=512, tn=d, sharded="cols")
    dsb = _mm_nt_acc("d_s", dyb, wso_g, tm=tm, tn=512)
    gw_so, gw_so_own = _mm_tn("dw_so", s, dyb, me_arr, m=dc, n=d, tma=512, tn=d, sharded="cols")
    red_mix, tok = reduce_start("mix", [gw_pw, gw_so, gw_o], ())
    da1, dln_g, dln_b = _layer_norm_silu_bwd(da3, a1, conf_ln_g, conf_ln_b, deps=[tok])
    dproj, dcw, dcb, dsw = _conv_backward(dproj, proj, da1, dsb, cw_full, sw_full, dc)
    red_mix, tok = reduce_middle("mix", red_mix, [gw_pw_own, gw_so_own, gw_o_own], [dcb])
    in_cb = w_in.shape[2]
    half = d // 2
    red_in = []
    for part in range(2):
        gw, own = _mm_tn("dw_in%d" % part, n, dproj, me_arr, m=half, n=proj.shape[1], tma=512, tn=2 * in_cb,
                         sharded="cols", a_off=part * (half // 512), deps=[tok])
        state, tok = reduce_start("in%d" % part, [gw], ())
        red_in.append((state, own))
    for part in range(2):
        state, own = red_in[part]
        red_in[part], tok = reduce_middle("in%d" % part, state, [own], [tok])
    dn = _mm_nt_acc("d_n", dproj, win_g, tm=tm // 2, tn=512, deps=[tok])
    dh0, dg_pre_mix = _pre_norm_bwd(dn, h0, dh1, g_pre_mix)
    grad_x = dh0[N_META:t_real][None]

    vec_parts = [dg_pre_mix, db_gates, dcb, dln_g, dln_b, dg_post_mix, dg_pre_mlp, dg_post_mlp]
    packed = _pack_small(vec_parts, dh0[:N_META], dcw, dsw, loss_blk, me_arr)
    send, recv, bufs, tok = _remote_start("small_grads_ici_start", "gather_ici", [packed])
    vec_names = ["g_pre_mix", "b_gates", "conf_dw_b", "conf_ln_g", "conf_ln_b", "g_post_mix", "g_pre_mlp", "g_post_mlp"]
    env = locals()
    results = {}

    def update(nm, parts, deps=()):
        res = _adamw_shard("adamw_" + nm, env[nm][0], env["m_" + nm][0], env["v_" + nm][0], parts, me_arr, deps=deps)
        results[nm] = tuple(r[None] for r in res)
        return res[0]

    done = [update("w_down", reduce_finish("down", red_down, [tok]), deps=[tok])]
    done.append(update("w_up", reduce_finish("up", red_up, done)))
    bufs = _remote_wait("small_grads_ici_wait", "gather_ici", send, recv, bufs, 1, done)
    send, recv, bufs, tok = _remote_start("small_grads_d2d_start", "gather_d2d", bufs)
    for nm, pair in zip(["conf_w_pw", "short_w_out", "w_o"], reduce_finish("mix", red_mix, [tok])):
        done.append(update(nm, [pair], deps=[tok]))
    small_g, = _remote_wait("small_grads_d2d_wait", "gather_d2d", send, recv, bufs, 1, done)
    triple = lambda nm, sq: tuple(env[p + nm][0] if sq else env[p + nm] for p in ("", "m_", "v_"))
    small, loss = _small_update(small_g, me_arr, [triple(nm, False) for nm in vec_names],
                                triple("meta", False), triple("conf_dw_w", True), triple("short_dw_w", True))
    for nm, res in zip(vec_names + ["meta"], small[:len(vec_names) + 1]):
        results[nm] = res
    results["conf_dw_w"] = tuple(r[None] for r in small[-2])
    results["short_dw_w"] = tuple(r[None] for r in small[-1])
    update("w_in", [reduce_finish("in%d" % part, red_in[part], [small[0][0]])[0] for part in range(2)])

    order = ["meta", "g_pre_mix", "w_in", "b_gates", "conf_dw_w", "conf_dw_b", "conf_ln_g", "conf_ln_b", "conf_w_pw",
             "short_dw_w", "short_w_out", "w_o", "g_post_mix", "g_pre_mlp", "w_up", "w_down", "g_post_mlp"]
    return (loss, grad_x, *[results[nm][0] for nm in order], *[results[nm][1] for nm in order],
            *[results[nm][2] for nm in order], *[results[nm][3] for nm in order])
```

```python
import jax
import jax.numpy as jnp
from jax import lax
from jax.experimental import pallas as pl
from jax.experimental.pallas import tpu as pltpu

N_DEV = 8
N_META = 16
CONF_K = 31
SHORT_K = 3
RMS_EPS = 1e-6
LN_EPS = 1e-5
ADAM_LR = 0.001
ADAM_B1 = 0.9
ADAM_B2 = 0.999
ADAM_EPS = 1e-08
ADAM_WD = 0.01
ADAM_STEP = 10

LANE = 128
SUB = 8
ROW_TILE = 128
CONV_PAD = 32
CONV_CHUNK = 128
VMEM_LIMIT = 56 * 1024 * 1024

F32 = jnp.float32
BF16 = jnp.bfloat16
MESH = pl.DeviceIdType.MESH
ANY = pl.BlockSpec(memory_space=pl.ANY)
HBM_SPEC = pl.BlockSpec(memory_space=pltpu.HBM)
SEM_SPEC = pl.BlockSpec(memory_space=pltpu.SEMAPHORE)
EFFECT = pltpu.SideEffectType.DATAFLOW_SIDE_EFFECTING


def _params(n_axes):
    return pltpu.CompilerParams(dimension_semantics=("arbitrary",) * n_axes, vmem_limit_bytes=VMEM_LIMIT)


def _sigmoid(z):
    return 1.0 / (1.0 + jnp.exp(-z))


def _colsum8(v):
    r, c = v.shape
    return jnp.sum(v.reshape(r // SUB, SUB, c), axis=0)


def _position():
    x, y, c = lax.axis_index("x"), lax.axis_index("y"), lax.axis_index("c")
    return x, y, c


def _flat(p):
    return 4 * p[0] + 2 * p[1] + p[2]


def _all_gather(name, shards, deps=()):
    n, nd = len(shards), len(deps)

    def body(*refs):
        ins, outs = refs[:n], refs[n + nd:2 * n + nd]
        send_sems, recv_sems, local_sems = refs[2 * n + nd:]
        x, y, c = _position()
        me, sibling = (x, y, c), (x, y, 1 - c)
        chips = [(1 - x, y), (x, 1 - y), (1 - x, 1 - y)]

        def copy(q, k, block, to, src=None):
            dst = outs[q].at[_flat(block)]
            return pltpu.make_async_remote_copy(
                src_ref=dst if src is None else src, dst_ref=dst,
                send_sem=send_sems.at[q, k], recv_sem=recv_sems.at[q, k],
                device_id=to, device_id_type=MESH)

        mine = [pltpu.make_async_copy(ins[q], outs[q].at[_flat(me)], local_sems.at[q]) for q in range(n)]
        for cp in mine:
            cp.start()
        first = []
        for q in range(n):
            first.append(copy(q, 0, me, sibling, src=ins[q]))
            for j, chip in enumerate(chips):
                first.append(copy(q, 1 + j, me, (*chip, c), src=ins[q]))
        for cp in first:
            cp.start()
        passed = []
        for q in range(n):
            for j, chip in enumerate(chips):
                copy(q, 1 + j, (*chip, c), me).wait_recv()
                fwd = copy(q, 4 + j, (*chip, c), sibling)
                fwd.start()
                passed.append(fwd)
        for q in range(n):
            copy(q, 0, sibling, me).wait_recv()
            for j, chip in enumerate(chips):
                copy(q, 4 + j, (*chip, 1 - c), me).wait_recv()
        for cp in first + passed:
            cp.wait_send()
        for cp in mine:
            cp.wait()

    return pl.pallas_call(
        body, name=name,
        in_specs=[ANY] * (n + nd), out_specs=[ANY] * n,
        out_shape=[jax.ShapeDtypeStruct((N_DEV,) + s.shape, s.dtype) for s in shards],
        scratch_shapes=[pltpu.SemaphoreType.DMA((n, 7)), pltpu.SemaphoreType.DMA((n, 7)),
                        pltpu.SemaphoreType.DMA((n,))],
    )(*shards, *deps)


N_COPIES = {"gather_ici": 4, "gather_d2d": 3, "gather_direct": 3, "gather_relay": 3, "gather_diag": 1,
            "reduce_d2d": 4, "reduce_ici": 3}


def _copy_plan(kind):
    x, y, c = _position()
    me, sibling = (x, y, c), (x, y, 1 - c)
    chips = [(1 - x, y), (x, 1 - y), (1 - x, 1 - y)]
    if kind == "gather_ici":
        return [(_flat(me), _flat(me), sibling)] + [(_flat(me), _flat(me), (*ch, c)) for ch in chips]
    if kind == "gather_d2d":
        return [(_flat((*ch, c)), _flat((*ch, c)), sibling) for ch in chips]
    if kind == "gather_direct":
        return [(_flat(me), _flat(me), sibling)] + [(_flat(me), _flat(me), (*ch, c)) for ch in chips[:2]]
    if kind == "gather_relay":
        held, to = (x ^ (1 - c), y ^ c, c), (x ^ c, y ^ (1 - c), c)
        return [(_flat(held), _flat(held), to)] + [(_flat((*ch, c)), _flat((*ch, c)), sibling) for ch in chips[:2]]
    if kind == "gather_diag":
        return [(_flat((*chips[2], c)), _flat((*chips[2], c)), sibling)]
    if kind == "reduce_d2d":
        return [(2 * chip + (1 - c), chip, sibling) for chip in range(4)]
    return [(2 * ch[0] + ch[1], 2 * x + y, (*ch, c)) for ch in chips]


def _planned_copies(kind, srcs, dsts, send_sems, recv_sems):
    plan = _copy_plan(kind)
    return [pltpu.make_async_remote_copy(
        src_ref=src.at[s_slot], dst_ref=dst.at[d_slot],
        send_sem=send_sems.at[q * len(plan) + k], recv_sem=recv_sems.at[q * len(plan) + k],
        device_id=to, device_id_type=MESH)
        for q, (src, dst) in enumerate(zip(srcs, dsts)) for k, (s_slot, d_slot, to) in enumerate(plan)]


def _remote_start(name, kind, srcs, lands=None, deps=()):
    n = len(srcs)
    bufs = list(srcs) + ([] if lands is None else list(lands))
    nb, nd = len(bufs), len(deps)
    nsem = n * N_COPIES[kind]

    def body(*refs):
        ins = refs[:nb]
        send_sems, recv_sems = refs[nb + nd], refs[nb + nd + 1]
        token = refs[-1]
        for cp in _planned_copies(kind, ins[:n], ins[:n] if lands is None else ins[n:], send_sems, recv_sems):
            cp.start()
        token[...] = jnp.zeros_like(token)

    outs = pl.pallas_call(
        body, name=name,
        out_shape=(pltpu.SemaphoreType.DMA((nsem,)), pltpu.SemaphoreType.DMA((nsem,)),
                   *[pltpu.HBM(b.shape, b.dtype) for b in bufs], jax.ShapeDtypeStruct((SUB, LANE), F32)),
        in_specs=[HBM_SPEC] * nb + [ANY] * nd,
        out_specs=(SEM_SPEC, SEM_SPEC, *[HBM_SPEC] * nb, pl.BlockSpec(memory_space=pltpu.VMEM)),
        input_output_aliases={i: 2 + i for i in range(nb)},
        compiler_params=pltpu.CompilerParams(has_side_effects=EFFECT),
    )(*[pltpu.with_memory_space_constraint(b, pltpu.HBM) for b in bufs], *deps)
    return outs[0], outs[1], list(outs[2:2 + nb]), outs[-1]


def _remote_wait(name, kind, send_sems, recv_sems, bufs, n, after):
    nb, na = len(bufs), len(after)
    same = nb == n

    def body(*refs):
        ins = refs[:nb]
        sends, recvs = refs[nb], refs[nb + 1]
        for cp in _planned_copies(kind, ins[:n], ins[:n] if same else ins[n:], sends, recvs):
            cp.wait_send()
            cp.wait_recv()

    outs = pl.pallas_call(
        body, name=name,
        out_shape=[pltpu.HBM(b.shape, b.dtype) for b in bufs],
        in_specs=[HBM_SPEC] * nb + [SEM_SPEC, SEM_SPEC] + [ANY] * na,
        out_specs=[HBM_SPEC] * nb,
        input_output_aliases={i: i for i in range(nb)},
        compiler_params=pltpu.CompilerParams(has_side_effects=EFFECT),
    )(*bufs, send_sems, recv_sems, *after)
    return list(outs)


def _remote_pass_on(name, done, send_sems, recv_sems, bufs, after, nxt):
    nb, na = len(bufs), len(after)
    nsem = nb * N_COPIES[nxt]

    def body(*refs):
        ins = refs[:nb]
        new_sends, new_recvs = refs[nb + 2 + na], refs[nb + 3 + na]
        token = refs[-1]
        for cp in _planned_copies(done, ins, ins, refs[nb], refs[nb + 1]):
            cp.wait_send()
            cp.wait_recv()
        for cp in _planned_copies(nxt, ins, ins, new_sends, new_recvs):
            cp.start()
        token[...] = jnp.zeros_like(token)

    outs = pl.pallas_call(
        body, name=name,
        out_shape=(pltpu.SemaphoreType.DMA((nsem,)), pltpu.SemaphoreType.DMA((nsem,)),
                   *[pltpu.HBM(b.shape, b.dtype) for b in bufs], jax.ShapeDtypeStruct((SUB, LANE), F32)),
        in_specs=[HBM_SPEC] * nb + [SEM_SPEC, SEM_SPEC] + [ANY] * na,
        out_specs=(SEM_SPEC, SEM_SPEC, *[HBM_SPEC] * nb, pl.BlockSpec(memory_space=pltpu.VMEM)),
        input_output_aliases={i: 2 + i for i in range(nb)},
        compiler_params=pltpu.CompilerParams(has_side_effects=EFFECT),
    )(*bufs, send_sems, recv_sems, *after)
    return outs[0], outs[1], list(outs[2:2 + nb]), outs[-1]


def _mm_cols(name, a, w, *, tm, nb=1, epilogue=None, out_dtypes=(F32,)):
    t, k = a.shape
    nblk, _, cb = w.shape

    def body(a_ref, w_ref, *o_refs):
        av = a_ref[...]
        for b in range(nb):
            acc = jnp.dot(av, w_ref[b], preferred_element_type=F32)
            outs = (acc,) if epilogue is None else epilogue(acc)
            for o_ref, o in zip(o_refs, outs):
                o_ref[:, b * cb:(b + 1) * cb] = o.astype(o_ref.dtype)

    return pl.pallas_call(
        body, name=name, grid=(nblk // nb, t // tm),
        in_specs=[pl.BlockSpec((tm, k), lambda j, i: (i, 0)),
                  pl.BlockSpec((nb, k, cb), lambda j, i: (j, 0, 0))],
        out_specs=[pl.BlockSpec((tm, nb * cb), lambda j, i: (i, j)) for _ in out_dtypes],
        out_shape=[jax.ShapeDtypeStruct((t, nblk * cb), dt) for dt in out_dtypes],
        compiler_params=_params(2),
    )(a, w)


MXU_WIDTH = 256


def _mm_cols_pairs(name, a, w, *, tm):
    t, k = a.shape
    nblk, _, cb = w.shape
    main = cb // MXU_WIDTH * MXU_WIDTH
    tail = cb - main
    assert 2 * tail == MXU_WIDTH and nblk % 2 == 0

    def body(a_ref, w_ref, o_ref):
        av = a_ref[...]
        for b in range(2):
            o_ref[:, b * cb:b * cb + main] = jnp.dot(av, w_ref[b, :, 0:main], preferred_element_type=F32)
        tails = jnp.dot(av, jnp.concatenate([w_ref[0, :, main:cb], w_ref[1, :, main:cb]], axis=1),
                        preferred_element_type=F32)
        for b in range(2):
            o_ref[:, b * cb + main:(b + 1) * cb] = tails[:, b * tail:(b + 1) * tail]

    return pl.pallas_call(
        body, name=name, grid=(nblk // 2, t // tm),
        in_specs=[pl.BlockSpec((tm, k), lambda j, i: (i, 0)),
                  pl.BlockSpec((2, k, cb), lambda j, i: (j, 0, 0))],
        out_specs=pl.BlockSpec((tm, 2 * cb), lambda j, i: (i, j)),
        out_shape=jax.ShapeDtypeStruct((t, nblk * cb), F32),
        compiler_params=_params(2),
    )(a, w)


def _mm_rows(name, a, w2d, *, tm, tn):
    t, kf = a.shape
    n = w2d.shape[1]

    def body(a_ref, w_ref, o_ref):
        o_ref[...] = jnp.dot(a_ref[...], w_ref[...], preferred_element_type=F32)

    return pl.pallas_call(
        body, name=name, grid=(t // tm, n // tn),
        in_specs=[pl.BlockSpec((tm, kf), lambda i, j: (i, 0)),
                  pl.BlockSpec((kf, tn), lambda i, j: (0, j))],
        out_specs=pl.BlockSpec((tm, tn), lambda i, j: (i, j)),
        out_shape=jax.ShapeDtypeStruct((t, n), F32),
        compiler_params=_params(2),
    )(a, w2d)


def _mm_nt_acc(name, dy, w, *, tm, tn, col_off=0, deps=()):
    t = dy.shape[0]
    nblk, k, cb = w.shape

    main = cb // MXU_WIDTH * MXU_WIDTH

    def body(dy_ref, w_ref, *rest):
        nt = (((1,), (1,)), ((), ()))
        acc = None
        for b in range(nblk):
            d = lax.dot_general(dy_ref[:, b * cb:b * cb + main], w_ref[b, :, 0:main], nt, preferred_element_type=F32)
            acc = d if acc is None else acc + d
        if main < cb:
            dy_tails = jnp.concatenate([dy_ref[:, b * cb + main:(b + 1) * cb] for b in range(nblk)], axis=1)
            w_tails = jnp.concatenate([w_ref[b, :, main:cb] for b in range(nblk)], axis=1)
            acc = acc + lax.dot_general(dy_tails, w_tails, nt, preferred_element_type=F32)
        rest[-1][...] = acc

    return pl.pallas_call(
        body, name=name, grid=(t // tm, k // tn),
        in_specs=[pl.BlockSpec((tm, nblk * cb), lambda i, j: (i, col_off)),
                  pl.BlockSpec((nblk, tn, cb), lambda i, j: (0, j, 0))] + [ANY] * len(deps),
        out_specs=pl.BlockSpec((tm, tn), lambda i, j: (i, j)),
        out_shape=jax.ShapeDtypeStruct((t, k), F32),
        compiler_params=_params(2),
    )(dy, w, *deps)


def _mm_nt_blocks(name, dy, w2d, *, tm, tkb, extra=(), epilogue=None, out_dtypes=(F32,)):
    t, n = dy.shape
    kf = w2d.shape[0]
    ne = len(extra)

    def body(dy_ref, w_ref, *rest):
        acc = lax.dot_general(dy_ref[...], w_ref[...], (((1,), (1,)), ((), ())), preferred_element_type=F32)
        outs = (acc,) if epilogue is None else epilogue(acc, *[e[...] for e in rest[:ne]])
        for o_ref, o in zip(rest[ne:], outs):
            o_ref[...] = o.astype(o_ref.dtype)

    return pl.pallas_call(
        body, name=name, grid=(kf // tkb, t // tm),
        in_specs=[pl.BlockSpec((tm, n), lambda kb, i: (i, 0)),
                  pl.BlockSpec((tkb, n), lambda kb, i: (kb, 0))]
                 + [pl.BlockSpec((tm, tkb), lambda kb, i: (i, kb)) for _ in extra],
        out_specs=[pl.BlockSpec((tm, tkb), lambda kb, i: (i, kb)) for _ in out_dtypes],
        out_shape=[jax.ShapeDtypeStruct((t, kf), dt) for dt in out_dtypes],
        compiler_params=_params(2),
    )(dy, w2d, *extra)


def _mm_tn(name, a, b, me_arr, *, m, n, tma, tn, sharded, a_off=0, b_off=0, deps=()):
    t = a.shape[0]
    if sharded == "cols":
        cb = n // N_DEV
        nb, q = max(tn // cb, 1), max(cb // tn, 1)
        tw = tn // nb
        full_shape, own_shape = (N_DEV, m, cb), (m, cb)
        full_spec = pl.BlockSpec((nb, tma, tw), lambda i, j, me: (j // q, i, j % q))
    else:
        kb = m // N_DEV
        p = kb // tma
        nb, tw = 1, tn
        full_shape, own_shape = (m, n), (kb, n)
        full_spec = pl.BlockSpec((tma, tn), lambda i, j, me: (i, j))

    def body(me_ref, a_ref, b_ref, *rest):
        full_ref, own_ref, stage, sem = rest[len(deps):]
        i, j = pl.program_id(0), pl.program_id(1)
        acc = lax.dot_general(a_ref[...], b_ref[...], (((0,), (0,)), ((), ())), preferred_element_type=F32)
        for blk in range(nb):
            part = acc[:, blk * tw:(blk + 1) * tw]
            if sharded == "cols":
                full_ref[blk] = part.astype(BF16)
                owner, r0, c0 = (j // q) * nb + blk, i * tma, (j % q) * tw
            else:
                full_ref[...] = part.astype(BF16)
                owner, r0, c0 = i // p, (i % p) * tma, j * tn

            @pl.when(owner == me_ref[0])
            def _():
                stage[...] = part
                cp = pltpu.make_async_copy(
                    stage, own_ref.at[pl.ds(pl.multiple_of(r0, tma), tma), pl.ds(pl.multiple_of(c0, tw), tw)], sem)
                cp.start()
                cp.wait()

    full, own = pl.pallas_call(
        body, name=name,
        grid_spec=pltpu.PrefetchScalarGridSpec(
            num_scalar_prefetch=1, grid=(m // tma, n // tn),
            in_specs=[pl.BlockSpec((t, tma), lambda i, j, me: (0, a_off + i)),
                      pl.BlockSpec((t, tn), lambda i, j, me: (0, b_off + j))] + [ANY] * len(deps),
            out_specs=[full_spec, ANY],
            scratch_shapes=[pltpu.VMEM((tma, tw), F32), pltpu.SemaphoreType.DMA(())]),
        out_shape=[jax.ShapeDtypeStruct(full_shape, BF16), jax.ShapeDtypeStruct(own_shape, F32)],
        compiler_params=_params(2),
    )(me_arr, a, b, *deps)
    if sharded == "rows":
        full = full.reshape(N_DEV, m // N_DEV, n)
    return full, own


def _row_tile(t):
    return t // 8 if (t // 8) % 16 == 0 else ROW_TILE


def _row_call(name, body, t, row_ins, full_ins, row_outs, acc_outs, scratch=(), deps=()):
    tm = _row_tile(t)
    nin = len(row_ins) + len(full_ins)

    def without_deps(*refs):
        body(*refs[:nin], *refs[nin + len(deps):])

    return pl.pallas_call(
        without_deps, name=name, grid=(t // tm,),
        in_specs=[pl.BlockSpec((tm, a.shape[1]), lambda i: (i, 0)) for a in row_ins]
                 + [pl.BlockSpec(a.shape, lambda i: (0, 0)) for a in full_ins] + [ANY] * len(deps),
        out_specs=[pl.BlockSpec((tm, c), lambda i: (i, 0)) for c, _ in row_outs]
                  + [pl.BlockSpec((r, c), lambda i: (0, 0)) for r, c in acc_outs],
        out_shape=[jax.ShapeDtypeStruct((t, c), dt) for c, dt in row_outs]
                  + [jax.ShapeDtypeStruct((r, c), F32) for r, c in acc_outs],
        scratch_shapes=list(scratch),
        compiler_params=_params(1),
    )(*row_ins, *full_ins, *deps)


def _accumulate(ref, v):
    @pl.when(pl.program_id(0) == 0)
    def _():
        ref[...] = v

    @pl.when(pl.program_id(0) > 0)
    def _():
        ref[...] += v


def _rms(v):
    return lax.rsqrt(jnp.mean(v * v, axis=-1, keepdims=True) + RMS_EPS)


def _rms_bwd(dout, u, r, g):
    du = dout * g
    dx = r * (du - u * jnp.mean(du * u, axis=-1, keepdims=True))
    return dx, _colsum8(dout * u)


def _pre_norm(h0, g):
    t, d = h0.shape

    def body(h_ref, g_ref, n_ref):
        h = h_ref[...]
        n_ref[...] = (h * _rms(h) * g_ref[...]).astype(BF16)

    return _row_call("pre_norm", body, t, [h0], [g], [(d, BF16)], [])[0]


def _mix_post(m_mix, wo_full, h0, g_post, g_pre, deps=()):
    t, d = h0.shape
    tm = _row_tile(t)

    def body(m_ref, wo_ref, h0_ref, gp_ref, gq_ref, *rest):
        mix_ref, h1_ref, n2_ref = rest[len(deps):]
        mix_v = jnp.dot(m_ref[...], wo_ref[...], preferred_element_type=F32)
        mix_ref[...] = mix_v
        h1 = h0_ref[...] + mix_v * _rms(mix_v) * gp_ref[...]
        h1_ref[...] = h1
        n2_ref[...] = (h1 * _rms(h1) * gq_ref[...]).astype(BF16)

    tile = pl.BlockSpec((tm, d), lambda i: (i, 0))
    gain = pl.BlockSpec((1, d), lambda i: (0, 0))
    return pl.pallas_call(
        body, name="mix_post", grid=(t // tm,),
        in_specs=[tile, pl.BlockSpec((d, d), lambda i: (0, 0)), tile, gain, gain] + [ANY] * len(deps),
        out_specs=[tile, tile, tile],
        out_shape=[jax.ShapeDtypeStruct((t, d), F32), jax.ShapeDtypeStruct((t, d), F32),
                   jax.ShapeDtypeStruct((t, d), BF16)],
        compiler_params=_params(1),
    )(m_mix, wo_full, h0, g_post, g_pre, *deps)


def _loss_head(fo, h1, tgt, g_post_mlp, t_real):
    t, d = h1.shape
    tile = _row_tile(t)

    def body(fo_ref, h1_ref, tgt_ref, g_ref, dfo_ref, dh2_ref, dg_ref, loss_ref, lacc):
        i = pl.program_id(0)
        fo_v = fo_ref[...]
        g = g_ref[...]
        r = _rms(fo_v)
        u = fo_v * r
        h2 = h1_ref[...] + u * g
        row = i * tile + lax.broadcasted_iota(jnp.int32, (tile, 1), 0)
        valid = jnp.logical_and(row >= N_META, row < t_real)
        diff = jnp.where(valid, h2 - tgt_ref[...], 0.0)
        dh2 = diff * (1.0 / d)
        dh2_ref[...] = dh2
        dfo, dg = _rms_bwd(dh2, u, r, g)
        dfo_ref[...] = dfo.astype(BF16)
        _accumulate(dg_ref, dg)
        _accumulate(lacc, _colsum8(diff * diff))

        @pl.when(i == pl.num_programs(0) - 1)
        def _():
            loss_ref[...] = jnp.full((SUB, LANE), (0.5 / d) * jnp.sum(lacc[...]), F32)

    return _row_call("loss_head", body, t, [fo, h1, tgt], [g_post_mlp],
                     [(d, BF16), (d, F32)], [(SUB, d), (SUB, LANE)], scratch=[pltpu.VMEM((SUB, d), F32)])


def _mid_norm_bwd(dn2, h1, dh2, mix, g_pre_mlp, g_post_mix, deps=()):
    t, d = h1.shape

    def body(dn2_ref, h1_ref, dh2_ref, mix_ref, gq_ref, gp_ref, dh1_ref, dmix_ref, dgq_ref, dgp_ref):
        h1 = h1_ref[...]
        r3 = _rms(h1)
        dx, dgq = _rms_bwd(dn2_ref[...], h1 * r3, r3, gq_ref[...])
        dh1 = dh2_ref[...] + dx
        dh1_ref[...] = dh1
        mix_v = mix_ref[...]
        r2 = _rms(mix_v)
        dmix, dgp = _rms_bwd(dh1, mix_v * r2, r2, gp_ref[...])
        dmix_ref[...] = dmix.astype(BF16)
        _accumulate(dgq_ref, dgq)
        _accumulate(dgp_ref, dgp)

    return _row_call("mid_norm_bwd", body, t, [dn2, h1, dh2, mix], [g_pre_mlp, g_post_mix],
                     [(d, F32), (d, BF16)], [(SUB, d), (SUB, d)], deps=deps)


def _pre_norm_bwd(dn, h0, dh1, g_pre_mix, deps=()):
    t, d = h0.shape

    def body(dn_ref, h0_ref, dh1_ref, g_ref, dh0_ref, dg_ref):
        h0 = h0_ref[...]
        r = _rms(h0)
        dx, dg = _rms_bwd(dn_ref[...], h0 * r, r, g_ref[...])
        dh0_ref[...] = dh1_ref[...] + dx
        _accumulate(dg_ref, dg)

    return _row_call("pre_norm_bwd", body, t, [dn, h0, dh1], [g_pre_mix], [(d, F32)], [(SUB, d)], deps=deps)


def _layer_norm_silu(a1, ln_g, ln_b, deps=()):
    t, c = a1.shape

    def body(a1_ref, g_ref, b_ref, a3_ref):
        a = a1_ref[...]
        mu = jnp.mean(a, axis=-1, keepdims=True)
        xc = a - mu
        rstd = lax.rsqrt(jnp.mean(xc * xc, axis=-1, keepdims=True) + LN_EPS)
        z = xc * rstd * g_ref[...] + b_ref[...]
        a3_ref[...] = (z * _sigmoid(z)).astype(BF16)

    return _row_call("layer_norm_silu", body, t, [a1], [ln_g, ln_b], [(c, BF16)], [], deps=deps)[0]


def _layer_norm_silu_bwd(da3, a1, ln_g, ln_b, deps=()):
    t, c = a1.shape

    def body(da3_ref, a1_ref, g_ref, b_ref, da1_ref, dg_ref, db_ref):
        a = a1_ref[...]
        g = g_ref[...]
        mu = jnp.mean(a, axis=-1, keepdims=True)
        xc = a - mu
        rstd = lax.rsqrt(jnp.mean(xc * xc, axis=-1, keepdims=True) + LN_EPS)
        xhat = xc * rstd
        z = xhat * g + b_ref[...]
        sg = _sigmoid(z)
        dz = da3_ref[...] * (sg * (1.0 + z * (1.0 - sg)))
        dxhat = dz * g
        da1_ref[...] = rstd * (dxhat - jnp.mean(dxhat, axis=-1, keepdims=True)
                               - xhat * jnp.mean(dxhat * xhat, axis=-1, keepdims=True))
        _accumulate(dg_ref, _colsum8(dz * xhat))
        _accumulate(db_ref, _colsum8(dz))

    return _row_call("layer_norm_silu_bwd", body, t, [da3, a1], [ln_g, ln_b], [(c, F32)], [(SUB, c), (SUB, c)], deps=deps)


def _branch_merge(a3, s, wpw, wso, proj, b_gates, d, deps=()):
    t, cols = proj.shape
    nblk, k, cb = wpw.shape
    w = 1024
    nh = d // w
    per = w // cb
    ga0 = (cols - 2 * d) // w
    tm = _row_tile(t)

    def body(a3_ref, s_ref, wpw_ref, wso_ref, *rest):
        pa_refs, pb_refs, bg_ref = rest[:nh], rest[nh:2 * nh], rest[2 * nh]
        ya_ref, yb_ref, ga_ref, gb_ref, m_ref = rest[2 * nh + 1 + len(deps):]
        a3v, sv = a3_ref[...], s_ref[...]
        for b in range(nblk):
            here = slice(b * cb, (b + 1) * cb)
            local = slice((b % per) * cb, (b % per + 1) * cb)
            ya = jnp.dot(a3v, wpw_ref[b], preferred_element_type=F32)
            yb = jnp.dot(sv, wso_ref[b], preferred_element_type=F32)
            ga = _sigmoid(pa_refs[b // per][:, local] + bg_ref[:, here])
            gb = _sigmoid(pb_refs[b // per][:, local] + bg_ref[:, d + b * cb:d + (b + 1) * cb])
            ya_ref[:, here] = ya.astype(BF16)
            yb_ref[:, here] = yb.astype(BF16)
            ga_ref[:, here] = ga.astype(BF16)
            gb_ref[:, here] = gb.astype(BF16)
            m_ref[:, here] = (ga * ya + gb * yb).astype(BF16)

    tile = pl.BlockSpec((tm, d), lambda i: (i, 0))
    return pl.pallas_call(
        body, name="branch_merge", grid=(t // tm,),
        in_specs=[pl.BlockSpec((tm, k), lambda i: (i, 0)), pl.BlockSpec((tm, k), lambda i: (i, 0)),
                  pl.BlockSpec((nblk, k, cb), lambda i: (0, 0, 0)), pl.BlockSpec((nblk, k, cb), lambda i: (0, 0, 0))]
                 + [pl.BlockSpec((tm, w), lambda i, h=h: (i, ga0 + h)) for h in range(2 * nh)]
                 + [pl.BlockSpec((1, 2 * d), lambda i: (0, 0))] + [ANY] * len(deps),
        out_specs=[tile] * 5,
        out_shape=[jax.ShapeDtypeStruct((t, d), BF16)] * 5,
        compiler_params=_params(1),
    )(a3, s, wpw, wso, *([proj] * (2 * nh)), b_gates, *deps)


def _gate_backward(dmix, wo_full, ga, gb, ya, yb, cols, tm, deps=()):
    t, d = ya.shape
    w = 1024
    nh = d // w
    ga0 = (cols - 2 * d) // w

    def body(dmix_ref, wo_ref, ga_ref, gb_ref, ya_ref, yb_ref, *rest):
        dya_ref, dyb_ref, dp_ref, dba_ref, dbb_ref, stage, sems = rest[len(deps):]
        h, i = pl.program_id(0), pl.program_id(1)
        dm = lax.dot_general(dmix_ref[...], wo_ref[...], (((1,), (1,)), ((), ())), preferred_element_type=F32)
        ga = ga_ref[...].astype(F32)
        gb = gb_ref[...].astype(F32)
        dya_ref[...] = (dm * ga).astype(BF16)
        dyb_ref[...] = (dm * gb).astype(BF16)
        dpa = dm * ya_ref[...].astype(F32) * ga * (1.0 - ga)
        dpb = dm * yb_ref[...].astype(F32) * gb * (1.0 - gb)
        stage[0] = dpa.astype(BF16)
        stage[1] = dpb.astype(BF16)
        rows = pl.ds(pl.multiple_of(i * tm, tm), tm)
        copies = [pltpu.make_async_copy(
            stage.at[g], dp_ref.at[rows, pl.ds(pl.multiple_of((ga0 + g * nh + h) * w, w), w)], sems.at[g])
            for g in range(2)]
        for cp in copies:
            cp.start()

        @pl.when(i == 0)
        def _():
            dba_ref[...] = _colsum8(dpa)
            dbb_ref[...] = _colsum8(dpb)

        @pl.when(i > 0)
        def _():
            dba_ref[...] += _colsum8(dpa)
            dbb_ref[...] += _colsum8(dpb)

        for cp in copies:
            cp.wait()

    tile = pl.BlockSpec((tm, w), lambda h, i: (i, h))
    return pl.pallas_call(
        body, name="gate_backward", grid=(nh, t // tm),
        in_specs=[pl.BlockSpec((tm, d), lambda h, i: (i, 0)),
                  pl.BlockSpec((w, d), lambda h, i: (h, 0)),
                  tile, tile, tile, tile] + [ANY] * len(deps),
        out_specs=[tile, tile, ANY,
                   pl.BlockSpec((SUB, w), lambda h, i: (0, h)),
                   pl.BlockSpec((SUB, w), lambda h, i: (0, h))],
        out_shape=[jax.ShapeDtypeStruct((t, d), BF16), jax.ShapeDtypeStruct((t, d), BF16),
                   jax.ShapeDtypeStruct((t, cols), BF16),
                   jax.ShapeDtypeStruct((SUB, d), F32), jax.ShapeDtypeStruct((SUB, d), F32)],
        scratch_shapes=[pltpu.VMEM((2, tm, w), BF16), pltpu.SemaphoreType.DMA((2,))],
        compiler_params=_params(2),
    )(dmix, wo_full, ga, gb, ya, yb, *deps)


def _shifted_views(win, offsets):
    n = win.shape[0]
    rotated = {}
    views = {}
    for o in offsets:
        q, r = divmod(o, SUB)
        if r not in rotated:
            rotated[r] = win if r == 0 else pltpu.roll(win, n - r, 0)
        views[o] = rotated[r][q * SUB:q * SUB + CONV_CHUNK]
    return views


def _causal_views(xp_ref, ntap, r0):
    win = xp_ref[pl.ds(r0, CONV_CHUNK + CONV_PAD), :]
    views = _shifted_views(win, [CONV_PAD - (ntap - 1 - k) for k in range(ntap)])
    return [views[CONV_PAD - (ntap - 1 - k)] for k in range(ntap)]


def _causal_conv(xp_ref, w_ref, ntap, r0):
    acc = None
    for k, shifted in enumerate(_causal_views(xp_ref, ntap, r0)):
        term = w_ref[k:k + 1, :] * shifted
        acc = term if acc is None else acc + term
    return acc


def _anticausal_conv(xp_ref, w_ref, ntap, r0):
    win = xp_ref[pl.ds(pl.multiple_of(CONV_PAD + r0, CONV_PAD), CONV_CHUNK + CONV_PAD), :]
    views = _shifted_views(win, [ntap - 1 - k for k in range(ntap)])
    acc = None
    for k in range(ntap):
        term = w_ref[k:k + 1, :] * views[ntap - 1 - k]
        acc = term if acc is None else acc + term
    return acc


def _conv_weight_grad(dw_ref, d_chunk, xp_ref, ntap, r0):
    for k, shifted in enumerate(_causal_views(xp_ref, ntap, r0)):
        dw_ref[k * SUB:(k + 1) * SUB, :] += _colsum8(d_chunk * shifted)


def _zero_pads(ref, t):
    ref[0:CONV_PAD, :] = jnp.zeros((CONV_PAD, LANE), F32)
    ref[CONV_PAD + t:CONV_PAD + t + CONV_PAD, :] = jnp.zeros((CONV_PAD, LANE), F32)


def _for_chunks(t, fn):
    def step(idx, carry):
        fn(pl.multiple_of(idx * CONV_CHUNK, CONV_CHUNK))
        return carry

    lax.fori_loop(0, t // CONV_CHUNK, step, 0)


def _conv_forward(proj, conf_w, conf_b, short_w, dc, deps=()):
    t = proj.shape[0]
    nc = dc // LANE

    def body(av_ref, ag_ref, bg_ref, cg_ref, v_ref, cw_ref, cb_ref, sw_ref, *rest):
        a1_ref, s_ref, xa, xb = rest[len(deps):]
        _zero_pads(xa, t)
        _zero_pads(xb, t)
        xa[CONV_PAD:CONV_PAD + t, :] = av_ref[...] * _sigmoid(ag_ref[...])
        xb[CONV_PAD:CONV_PAD + t, :] = cg_ref[...] * v_ref[...]

        def chunk(r0):
            rs = pl.ds(r0, CONV_CHUNK)
            a1_ref[rs, :] = _causal_conv(xa, cw_ref, CONF_K, r0) + cb_ref[...]
            s_ref[rs, :] = (bg_ref[rs, :] * _causal_conv(xb, sw_ref, SHORT_K, r0)).astype(BF16)

        _for_chunks(t, chunk)

    col = lambda g: pl.BlockSpec((t, LANE), lambda c, g=g: (0, g * nc + c))
    return pl.pallas_call(
        body, name="conv_forward", grid=(nc,),
        in_specs=[col(0), col(1), col(2), col(3), col(4),
                  pl.BlockSpec((CONF_K, LANE), lambda c: (0, c)),
                  pl.BlockSpec((1, LANE), lambda c: (0, c)),
                  pl.BlockSpec((SHORT_K, LANE), lambda c: (0, c))] + [ANY] * len(deps),
        out_specs=[pl.BlockSpec((t, LANE), lambda c: (0, c)), pl.BlockSpec((t, LANE), lambda c: (0, c))],
        out_shape=[jax.ShapeDtypeStruct((t, dc), F32), jax.ShapeDtypeStruct((t, dc), BF16)],
        scratch_shapes=[pltpu.VMEM((t + 2 * CONV_PAD, LANE), F32), pltpu.VMEM((t + 2 * CONV_PAD, LANE), F32)],
        compiler_params=_params(1),
    )(proj, proj, proj, proj, proj, conf_w, conf_b, short_w, *deps)


def _conv_backward(dproj, proj, da1, ds, conf_w, short_w, dc):
    t = proj.shape[0]
    nc = dc // LANE

    def body(dp_in, av_ref, ag_ref, bg_ref, cg_ref, v_ref, da1_ref, ds_ref, cw_ref, sw_ref,
             dp_ref, dcw_ref, dcb_ref, dsw_ref, xa, xb, da, db, stage, sems):
        del dp_in
        c = pl.program_id(0)
        for ref in (xa, xb, da, db):
            _zero_pads(ref, t)
        xa[CONV_PAD:CONV_PAD + t, :] = av_ref[...] * _sigmoid(ag_ref[...])
        xb[CONV_PAD:CONV_PAD + t, :] = cg_ref[...] * v_ref[...]
        da[CONV_PAD:CONV_PAD + t, :] = da1_ref[...]
        dcw_ref[...] = jnp.zeros(dcw_ref.shape, F32)
        dsw_ref[...] = jnp.zeros(dsw_ref.shape, F32)
        dcb_ref[...] = jnp.zeros(dcb_ref.shape, F32)

        def through_gate(r0):
            rs = pl.ds(r0, CONV_CHUNK)
            ds_c = ds_ref[rs, :]
            stage[2, rs, :] = (ds_c * _causal_conv(xb, sw_ref, SHORT_K, r0)).astype(BF16)
            db[pl.ds(pl.multiple_of(CONV_PAD + r0, CONV_PAD), CONV_CHUNK), :] = ds_c * bg_ref[rs, :]

        _for_chunks(t, through_gate)

        def through_convs(r0):
            rs = pl.ds(r0, CONV_CHUNK)
            da0 = _anticausal_conv(da, cw_ref, CONF_K, r0)
            sg = _sigmoid(ag_ref[rs, :])
            stage[0, rs, :] = (da0 * sg).astype(BF16)
            stage[1, rs, :] = (da0 * av_ref[rs, :] * sg * (1.0 - sg)).astype(BF16)
            dcv = _anticausal_conv(db, sw_ref, SHORT_K, r0)
            stage[3, rs, :] = (dcv * v_ref[rs, :]).astype(BF16)
            stage[4, rs, :] = (dcv * cg_ref[rs, :]).astype(BF16)
            da1_c = da1_ref[rs, :]
            _conv_weight_grad(dcw_ref, da1_c, xa, CONF_K, r0)
            _conv_weight_grad(dsw_ref, ds_ref[rs, :] * bg_ref[rs, :], xb, SHORT_K, r0)
            dcb_ref[...] += _colsum8(da1_c)

        _for_chunks(t, through_convs)
        copies = [pltpu.make_async_copy(
            stage.at[g], dp_ref.at[:, pl.ds(pl.multiple_of((g * nc + c) * LANE, LANE), LANE)], sems.at[g])
            for g in range(5)]
        for cp in copies:
            cp.start()
        for cp in copies:
            cp.wait()

    col = lambda g: pl.BlockSpec((t, LANE), lambda c, g=g: (0, g * nc + c))
    blk = pl.BlockSpec((t, LANE), lambda c: (0, c))
    return pl.pallas_call(
        body, name="conv_backward", grid=(nc,),
        in_specs=[ANY, col(0), col(1), col(2), col(3), col(4), blk, blk,
                  pl.BlockSpec((CONF_K, LANE), lambda c: (0, c)),
                  pl.BlockSpec((SHORT_K, LANE), lambda c: (0, c))],
        out_specs=[ANY,
                   pl.BlockSpec((CONF_K * SUB, LANE), lambda c: (0, c)),
                   pl.BlockSpec((SUB, LANE), lambda c: (0, c)),
                   pl.BlockSpec((SHORT_K * SUB, LANE), lambda c: (0, c))],
        out_shape=[jax.ShapeDtypeStruct(dproj.shape, dproj.dtype),
                   jax.ShapeDtypeStruct((CONF_K * SUB, dc), F32),
                   jax.ShapeDtypeStruct((SUB, dc), F32),
                   jax.ShapeDtypeStruct((SHORT_K * SUB, dc), F32)],
        scratch_shapes=[pltpu.VMEM((t + 2 * CONV_PAD, LANE), F32)] * 4
                       + [pltpu.VMEM((5, t, LANE), BF16), pltpu.SemaphoreType.DMA((5,))],
        input_output_aliases={0: 0},
        compiler_params=_params(1),
    )(dproj, proj, proj, proj, proj, proj, da1, ds, conf_w, short_w)


def _adamw_math(w, g, m, v):
    m = ADAM_B1 * m + (1.0 - ADAM_B1) * g
    v = ADAM_B2 * v + (1.0 - ADAM_B2) * (g * g)
    m_hat = m / (1.0 - ADAM_B1 ** ADAM_STEP)
    v_hat = v / (1.0 - ADAM_B2 ** ADAM_STEP)
    delta = -ADAM_LR * (m_hat / (jnp.sqrt(v_hat) + ADAM_EPS) + ADAM_WD * w)
    return delta, m, v


def _cast_into_slot(name, w, me_arr, deps=()):
    r, c = w.shape
    tr = 256

    def body(me_ref, w_ref, *rest):
        del me_ref
        rest[-1][0] = w_ref[...].astype(BF16)

    return pl.pallas_call(
        body, name=name,
        grid_spec=pltpu.PrefetchScalarGridSpec(
            num_scalar_prefetch=1, grid=(r // tr,),
            in_specs=[pl.BlockSpec((tr, c), lambda i, me: (i, 0))] + [ANY] * len(deps),
            out_specs=pl.BlockSpec((1, tr, c), lambda i, me: (me[0], i, 0))),
        out_shape=jax.ShapeDtypeStruct((N_DEV, r, c), BF16),
        compiler_params=_params(1),
    )(me_arr, w, *deps)


def _chip_sum(name, full, from_sibling, me_arr):
    _, r, c = full.shape
    tr = min(r, 512)

    def body(me_ref, full_ref, sib_ref, sums_ref):
        del me_ref
        sums_ref[0] = (full_ref[0].astype(F32) + sib_ref[0].astype(F32)).astype(BF16)

    other = lambda k, me: (me[0] // 2 + 1 + k) % 4
    return pl.pallas_call(
        body, name=name,
        grid_spec=pltpu.PrefetchScalarGridSpec(
            num_scalar_prefetch=1, grid=(r // tr, 3),
            in_specs=[pl.BlockSpec((1, tr, c), lambda i, k, me: (2 * other(k, me) + me[0] % 2, i, 0)),
                      pl.BlockSpec((1, tr, c), lambda i, k, me: (other(k, me), i, 0))],
            out_specs=pl.BlockSpec((1, tr, c), lambda i, k, me: (other(k, me), i, 0))),
        out_shape=jax.ShapeDtypeStruct((4, r, c), BF16),
        compiler_params=_params(2),
    )(me_arr, full, from_sibling)


def _adamw_shard(name, w, m, v, parts, me_arr, deps=()):
    r, c = w.shape
    tr = min(256, r // len(parts))
    np_ = len(parts)
    per = r // np_ // tr

    def body(me_ref, w_ref, m_ref, v_ref, *rest):
        g_out, d_out, m_out, v_out = rest[5 * np_ + len(deps):]
        g = None
        for p in range(np_):
            gp = rest[5 * p][...]
            for l_ref in rest[5 * p + 1:5 * p + 5]:
                gp = gp + l_ref[0].astype(F32)
            g = gp if g is None else jnp.where(pl.program_id(0) // per == p, gp, g)
        delta, m_new, v_new = _adamw_math(w_ref[...], g, m_ref[...], v_ref[...])
        g_out[...] = g
        d_out[...] = delta
        m_out[...] = m_new
        v_out[...] = v_new

    tile = pl.BlockSpec((tr, c), lambda i, me: (i, 0))
    part_specs, part_args = [], []
    for p, (g_own, from_sibling, landed) in enumerate(parts):
        row = lambda i, p=p: jnp.clip(i - p * per, 0, per - 1)
        part_specs.append(pl.BlockSpec((tr, c), lambda i, me, row=row: (row(i), 0)))
        part_specs += [pl.BlockSpec((1, tr, c), lambda i, me, k=k, row=row: ((me[0] // 2 + k) % 4, row(i), 0))
                       for k in range(4)]
        part_args += [g_own, from_sibling, landed, landed, landed]
    return pl.pallas_call(
        body, name=name,
        grid_spec=pltpu.PrefetchScalarGridSpec(
            num_scalar_prefetch=1, grid=(r // tr,),
            in_specs=[tile] * 3 + part_specs + [ANY] * len(deps), out_specs=[tile] * 4),
        out_shape=[jax.ShapeDtypeStruct((r, c), F32)] * 4,
        compiler_params=_params(1),
    )(me_arr, w, m, v, *part_args, *deps)


SMALL_W = 1024
VEC_ROWS = 16
LOSS_ROW = 15
META_ROW0 = 16
CONF_ROW0 = 64
SHORT_ROW0 = 96
SMALL_ROWS = 104


def _pack_small(vec_parts, dmeta, dcw, dsw, loss_blk, me_arr):
    widths = [p.shape[1] for p in vec_parts]
    nv = len(vec_parts)

    def body(me_ref, *refs):
        del me_ref
        parts, (dmeta_ref, dcw_ref, dsw_ref, loss_ref, out_ref) = refs[:nv], refs[nv:]
        out_ref[0] = jnp.zeros((SMALL_ROWS, SMALL_W), F32)
        out_ref[0, LOSS_ROW:LOSS_ROW + 1, 0:LANE] = loss_ref[0:1, :]
        row = 0
        for p_ref, wd in zip(parts, widths):
            s = jnp.sum(p_ref[...], axis=0, keepdims=True)
            for h in range(wd // SMALL_W):
                out_ref[0, row:row + 1, :] = s[:, h * SMALL_W:(h + 1) * SMALL_W]
                row += 1
        for h in range(dmeta_ref.shape[1] // SMALL_W):
            out_ref[0, META_ROW0 + h * N_META:META_ROW0 + (h + 1) * N_META, :] = dmeta_ref[:, h * SMALL_W:(h + 1) * SMALL_W]
        for k in range(CONF_K):
            out_ref[0, CONF_ROW0 + k:CONF_ROW0 + k + 1, :] = jnp.sum(dcw_ref[k * SUB:(k + 1) * SUB, :], axis=0, keepdims=True)
        for k in range(SHORT_K):
            out_ref[0, SHORT_ROW0 + k:SHORT_ROW0 + k + 1, :] = jnp.sum(dsw_ref[k * SUB:(k + 1) * SUB, :], axis=0, keepdims=True)

    ins = [*vec_parts, dmeta, dcw, dsw, loss_blk]
    return pl.pallas_call(
        body, name="pack_small",
        grid_spec=pltpu.PrefetchScalarGridSpec(
            num_scalar_prefetch=1, grid=(1,),
            in_specs=[pl.BlockSpec(a.shape, lambda i, me: (0, 0)) for a in ins],
            out_specs=pl.BlockSpec((1, SMALL_ROWS, SMALL_W), lambda i, me: (me[0], 0, 0))),
        out_shape=jax.ShapeDtypeStruct((N_DEV, SMALL_ROWS, SMALL_W), F32),
        compiler_params=_params(1),
    )(me_arr, *ins)


def _small_update(gathered, me_arr, vec_params, meta_p, conf_p, short_p):
    widths = [p[0].shape[1] for p in vec_params]
    nv = len(vec_params)
    mcols = meta_p[0].shape[1]
    per_row = SMALL_W // mcols

    def body(me_ref, gv_ref, gm_ref, gc_ref, gs_ref, *rest):
        del me_ref
        ins, outs = rest[:3 * (nv + 3)], rest[3 * (nv + 3):]

        def total(ref, r0, rows):
            s = ref[0, r0:r0 + rows, :]
            for dev in range(1, N_DEV):
                s = s + ref[dev, r0:r0 + rows, :]
            return s

        grads = []
        row = 0
        for wd in widths:
            pieces = [total(gv_ref, row + h, 1) for h in range(wd // SMALL_W)]
            grads.append(pieces[0] if len(pieces) == 1 else jnp.concatenate(pieces, axis=1))
            row += len(pieces)
        grads.append(total(gm_ref, 0, N_META))
        grads.append(total(gc_ref, 0, CONF_K))
        grads.append(total(gs_ref, 0, SHORT_K))
        loss = gv_ref[0, LOSS_ROW:LOSS_ROW + 1, 0:LANE]
        for dev in range(1, N_DEV):
            loss = loss + gv_ref[dev, LOSS_ROW:LOSS_ROW + 1, 0:LANE]
        outs[-1][...] = loss
        for idx, g in enumerate(grads):
            w_ref, m_ref, v_ref = ins[3 * idx:3 * idx + 3]
            delta, m_new, v_new = _adamw_math(w_ref[...], g, m_ref[...], v_ref[...])
            g_out, d_out, m_out, v_out = outs[4 * idx:4 * idx + 4]
            g_out[...] = g
            d_out[...] = delta
            m_out[...] = m_new
            v_out[...] = v_new

    params = list(vec_params) + [meta_p, conf_p, short_p]
    flat = [a for p in params for a in p]
    whole = lambda a: pl.BlockSpec(a.shape, lambda i, me: (0,) * a.ndim)
    outs = pl.pallas_call(
        body, name="small_update",
        grid_spec=pltpu.PrefetchScalarGridSpec(
            num_scalar_prefetch=1, grid=(1,),
            in_specs=[pl.BlockSpec((N_DEV, VEC_ROWS, SMALL_W), lambda i, me: (0, 0, 0)),
                      pl.BlockSpec((N_DEV, N_META, mcols),
                                   lambda i, me: (0, META_ROW0 // N_META + me[0] // per_row, me[0] % per_row)),
                      pl.BlockSpec((N_DEV, 32, LANE), lambda i, me: (0, CONF_ROW0 // 32, me[0])),
                      pl.BlockSpec((N_DEV, SUB, LANE), lambda i, me: (0, SHORT_ROW0 // SUB, me[0]))]
                     + [whole(a) for a in flat],
            out_specs=[whole(p[0]) for p in params for _ in range(4)]
                      + [pl.BlockSpec((1, LANE), lambda i, me: (0, 0))]),
        out_shape=[jax.ShapeDtypeStruct(p[0].shape, F32) for p in params for _ in range(4)]
                  + [jax.ShapeDtypeStruct((1, LANE), F32)],
        compiler_params=_params(1),
    )(me_arr, gathered, gathered, gathered, gathered, *flat)
    return [tuple(outs[4 * i:4 * i + 4]) for i in range(len(params))], outs[-1][0, 0]


def kernel(x, meta, g_pre_mix, w_in, b_gates, conf_dw_w, conf_dw_b, conf_ln_g, conf_ln_b, conf_w_pw, short_dw_w, short_w_out, w_o, g_post_mix, g_pre_mlp, w_up, w_down, g_post_mlp, loss_target, m_meta, m_g_pre_mix, m_w_in, m_b_gates, m_conf_dw_w, m_conf_dw_b, m_conf_ln_g, m_conf_ln_b, m_conf_w_pw, m_short_dw_w, m_short_w_out, m_w_o, m_g_post_mix, m_g_pre_mlp, m_w_up, m_w_down, m_g_post_mlp, v_meta, v_g_pre_mix, v_w_in, v_b_gates, v_conf_dw_w, v_conf_dw_b, v_conf_ln_g, v_conf_ln_b, v_conf_w_pw, v_short_dw_w, v_short_w_out, v_w_o, v_g_post_mix, v_g_pre_mlp, v_w_up, v_w_down, v_g_post_mlp):
    seq, d = x.shape[1], x.shape[2]
    dc = conf_w_pw.shape[1]
    t_real = N_META + seq
    t = -(-t_real // ROW_TILE) * ROW_TILE
    tm = t // 2
    assert tm % 16 == 0 and d % 1024 == 0 and dc % 1024 == 0
    x_idx, y_idx, c_idx = _position()
    me_arr = jnp.reshape(4 * x_idx + 2 * y_idx + c_idx, (1,)).astype(jnp.int32)

    big = [w_in[0], conf_w_pw[0], short_w_out[0], w_o[0], w_up[0], w_down[0]]
    big_names = ["w_in", "conf_w_pw", "short_w_out", "w_o", "w_up", "w_down"]
    groups = [[0], [1, 2, 3], [4], [5]]
    slots, deps = [], []
    for g, idxs in enumerate(groups):
        slots.append([_cast_into_slot("cast_" + big_names[i], big[i], me_arr, deps=deps) for i in idxs])
        if g == 0:
            direct0 = _remote_start("gather0_direct_start", "gather_direct", slots[0])
            deps = [direct0[3]]
    casts = [sl for group in slots[1:] for sl in group]
    meta_g, cw_g, sw_g = _all_gather("gather_small_params", [meta, conf_dw_w[0], short_dw_w[0]], deps=casts)

    def start_direct(g, deps):
        send, recv, bufs, tok = _remote_start("gather%d_direct_start" % g, "gather_direct", slots[g], deps=deps)
        return (send, recv, bufs), tok

    def relay(g, state, after):
        send, recv, bufs, tok = _remote_pass_on("gather%d_relay" % g, "gather_direct", *state, after, "gather_relay")
        return (send, recv, bufs), tok

    def pass_diag(g, state, after):
        send, recv, bufs, tok = _remote_pass_on("gather%d_diag" % g, "gather_relay", *state, after, "gather_diag")
        return (send, recv, bufs), tok

    def gathered(g, state, after):
        send, recv, bufs = state
        return _remote_wait("gather%d_diag_wait" % g, "gather_diag", send, recv, bufs, len(bufs), after)

    unshard =lambda g: jnp.transpose(g, (1, 0, 2)).reshape(g.shape[1], -1)
    meta_full, cw_full, sw_full = unshard(meta_g), unshard(cw_g), unshard(sw_g)

    relay0, tok = relay(0, direct0[:3], [meta_g])
    zrows = jnp.zeros((t - t_real, d), F32) + tok[0, 0] * 0.0
    h0 = jnp.concatenate([meta_full, x[0], zrows], axis=0)
    tgt = jnp.concatenate([jnp.zeros((N_META, d), F32), loss_target[0], zrows], axis=0)
    n = _pre_norm(h0, g_pre_mix)
    direct1, tok = start_direct(1, [tok])
    direct2, tok = start_direct(2, [tok])
    diag0, tok = pass_diag(0, relay0, [tok, n])
    win_g, = gathered(0, diag0, [tok])
    proj = _mm_cols_pairs("proj", n, win_g, tm=tm // 2)
    relay1, tok = relay(1, direct1, [proj])
    direct3, tok = start_direct(3, [tok])
    a1, s = _conv_forward(proj, cw_full, conf_dw_b, sw_full, dc, deps=[tok])
    relay2, tok = relay(2, direct2, [a1])
    diag1, tok = pass_diag(1, relay1, [tok])
    a3 = _layer_norm_silu(a1, conf_ln_g, conf_ln_b, deps=[tok])
    wpw_g, wso_g, wo_g = gathered(1, diag1, [a3])
    wo_full = wo_g.reshape(d, d)
    ya, yb, gate_a, gate_b, m_mix = _branch_merge(a3, s, wpw_g, wso_g, proj, b_gates, d)
    diag2, tok = pass_diag(2, relay2, [m_mix])
    mix, h1, n2 = _mix_post(m_mix, wo_full, h0, g_post_mix, g_pre_mlp, deps=[tok])
    relay3, tok = relay(3, direct3, [n2])
    wup_g, = gathered(2, diag2, [tok])

    def up_epilogue(acc):
        r = jnp.maximum(acc, 0.0)
        return r * r, r

    f, relu_up = _mm_cols("mlp_up", n2, wup_g, tm=tm, epilogue=up_epilogue, out_dtypes=(BF16, BF16))
    diag3, tok = pass_diag(3, relay3, [f])
    wdn_g, = gathered(3, diag3, [tok])
    wdn_full = wdn_g.reshape(-1, d)
    fo = _mm_rows("mlp_down", f, wdn_full, tm=tm // 2, tn=512)
    dfo, dh2, dg_post_mlp, loss_blk = _loss_head(fo, h1, tgt, g_post_mlp, t_real)

    def reduce_start(tag, fulls, deps):
        lands = [lax.empty((4,) + g.shape[1:], BF16) for g in fulls]
        send, recv, bufs, tok = _remote_start("reduce_%s_d2d_start" % tag, "reduce_d2d", fulls, lands, deps=deps)
        return (send, recv, bufs), tok

    def reduce_middle(tag, state, owns, after):
        send, recv, bufs = state
        k = len(owns)
        bufs = _remote_wait("reduce_%s_d2d_wait" % tag, "reduce_d2d", send, recv, bufs, k, after)
        from_sibling = bufs[k:]
        sums = [_chip_sum("chip_sum_%s%d" % (tag, i), bufs[i], from_sibling[i], me_arr) for i in range(k)]
        lands = [lax.empty(sm.shape, BF16) for sm in sums]
        send, recv, bufs, tok = _remote_start("reduce_%s_ici_start" % tag, "reduce_ici", sums, lands)
        return (send, recv, bufs, list(zip(owns, from_sibling))), tok

    def reduce_finish(tag, state, after):
        send, recv, bufs, local = state
        k = len(local)
        bufs = _remote_wait("reduce_%s_ici_wait" % tag, "reduce_ici", send, recv, bufs, k, after)
        return [(own, sib, landed) for (own, sib), landed in zip(local, bufs[k:])]

    dup = _mm_nt_blocks("d_up", dfo, wdn_full, tm=tm, tkb=1024, extra=(relu_up,),
                        epilogue=lambda acc, r: (acc * (2.0 * r.astype(F32)),), out_dtypes=(BF16,))[0]
    gw_down, gw_down_own = _mm_tn("dw_down", f, dfo, me_arr, m=f.shape[1], n=d, tma=512, tn=d, sharded="rows")
    red_down, tok = reduce_start("down", [gw_down], ())
    dn2 = _mm_nt_acc("d_n2", dup, wup_g, tm=tm // 2, tn=512, deps=[tok])
    gw_up, gw_up_own = _mm_tn("dw_up", n2, dup, me_arr, m=d, n=dup.shape[1], tma=512, tn=2048, sharded="cols")
    red_down, tok = reduce_middle("down", red_down, [gw_down_own], [dn2])
    red_up, tok = reduce_start("up", [gw_up], [tok])
    dh1, dmix, dg_pre_mlp, dg_post_mix = _mid_norm_bwd(dn2, h1, dh2, mix, g_pre_mlp, g_post_mix, deps=[tok])
    dya, dyb, dproj, db_a, db_b = _gate_backward(dmix, wo_full, gate_a, gate_b, ya, yb, proj.shape[1], tm // 2)
    db_gates = jnp.concatenate([db_a, db_b], axis=1)
    red_up, tok = reduce_middle("up", red_up, [gw_up_own], [dya])
    gw_o, gw_o_own = _mm_tn("dw_o", m_mix, dmix, me_arr, m=d, n=d, tma=d // N_DEV, tn=d, sharded="rows", deps=[tok])
    da3 = _mm_nt_acc("d_a3", dya, wpw_g, tm=tm, tn=512)
    gw_pw, gw_pw_own = _mm_tn("dw_pw", a3, dya, me_arr, m=dc, n=d, tma=512, tn=d, sharded="cols")
    dsb = _mm_nt_acc("d_s", dyb, wso_g, tm=tm, tn=512)
    gw_so, gw_so_own = _mm_tn("dw_so", s, dyb, me_arr, m=dc, n=d, tma=512, tn=d, sharded="cols")
    red_mix, tok = reduce_start("mix", [gw_pw, gw_so, gw_o], ())
    da1, dln_g, dln_b = _layer_norm_silu_bwd(da3, a1, conf_ln_g, conf_ln_b, deps=[tok])
    dproj, dcw, dcb, dsw = _conv_backward(dproj, proj, da1, dsb, cw_full, sw_full, dc)
    red_mix, tok = reduce_middle("mix", red_mix, [gw_pw_own, gw_so_own, gw_o_own], [dcb])
    in_cb = w_in.shape[2]
    half = d // 2
    red_in = []
    for part in range(2):
        gw, own = _mm_tn("dw_in%d" % part, n, dproj, me_arr, m=half, n=proj.shape[1], tma=512, tn=2 * in_cb,
                         sharded="cols", a_off=part * (half // 512), deps=[tok])
        state, tok = reduce_start("in%d" % part, [gw], ())
        red_in.append((state, own))
    for part in range(2):
        state, own = red_in[part]
        red_in[part], tok = reduce_middle("in%d" % part, state, [own], [tok])
    dn = _mm_nt_acc("d_n", dproj, win_g, tm=tm // 2, tn=512, deps=[tok])
    dh0, dg_pre_mix = _pre_norm_bwd(dn, h0, dh1, g_pre_mix)
    grad_x = dh0[N_META:t_real][None]

    vec_parts = [dg_pre_mix, db_gates, dcb, dln_g, dln_b, dg_post_mix, dg_pre_mlp, dg_post_mlp]
    packed = _pack_small(vec_parts, dh0[:N_META], dcw, dsw, loss_blk, me_arr)
    send, recv, bufs, tok = _remote_start("small_grads_ici_start", "gather_ici", [packed])
    vec_names = ["g_pre_mix", "b_gates", "conf_dw_b", "conf_ln_g", "conf_ln_b", "g_post_mix", "g_pre_mlp", "g_post_mlp"]
    env = locals()
    results = {}

    def update(nm, parts, deps=()):
        res = _adamw_shard("adamw_" + nm, env[nm][0], env["m_" + nm][0], env["v_" + nm][0], parts, me_arr, deps=deps)
        results[nm] = tuple(r[None] for r in res)
        return res[0]

    done = [update("w_down", reduce_finish("down", red_down, [tok]), deps=[tok])]
    done.append(update("w_up", reduce_finish("up", red_up, done)))
    bufs = _remote_wait("small_grads_ici_wait", "gather_ici", send, recv, bufs, 1, done)
    send, recv, bufs, tok = _remote_start("small_grads_d2d_start", "gather_d2d", bufs)
    for nm, pair in zip(["conf_w_pw", "short_w_out", "w_o"], reduce_finish("mix", red_mix, [tok])):
        done.append(update(nm, [pair], deps=[tok]))
    small_g, = _remote_wait("small_grads_d2d_wait", "gather_d2d", send, recv, bufs, 1, done)
    triple = lambda nm, sq: tuple(env[p + nm][0] if sq else env[p + nm] for p in ("", "m_", "v_"))
    small, loss = _small_update(small_g, me_arr, [triple(nm, False) for nm in vec_names],
                                triple("meta", False), triple("conf_dw_w", True), triple("short_dw_w", True))
    for nm, res in zip(vec_names + ["meta"], small[:len(vec_names) + 1]):
        results[nm] = res
    results["conf_dw_w"] = tuple(r[None] for r in small[-2])
    results["short_dw_w"] = tuple(r[None] for r in small[-1])
    update("w_in", [reduce_finish("in%d" % part, red_in[part], [small[0][0]])[0] for part in range(2)])

    order = ["meta", "g_pre_mix", "w_in", "b_gates", "conf_dw_w", "conf_dw_b", "conf_ln_g", "conf_ln_b", "conf_w_pw",
             "short_dw_w", "short_w_out", "w_o", "g_post_mix", "g_pre_mlp", "w_up", "w_down", "g_post_mlp"]
    return (loss, grad_x, *[results[nm][0] for nm in order], *[results[nm][1] for nm in order],
            *[results[nm][2] for nm in order], *[results[nm][3] for nm in order])
```

```python
import jax
import jax.numpy as jnp
from jax import lax
from jax.experimental import pallas as pl
from jax.experimental.pallas import tpu as pltpu

N_DEV = 8
N_META = 16
CONF_K = 31
SHORT_K = 3
RMS_EPS = 1e-6
LN_EPS = 1e-5
ADAM_LR = 0.001
ADAM_B1 = 0.9
ADAM_B2 = 0.999
ADAM_EPS = 1e-08
ADAM_WD = 0.01
ADAM_STEP = 10

LANE = 128
SUB = 8
ROW_TILE = 128
CONV_PAD = 32
CONV_CHUNK = 128
VMEM_LIMIT = 56 * 1024 * 1024

F32 = jnp.float32
BF16 = jnp.bfloat16
MESH = pl.DeviceIdType.MESH
ANY = pl.BlockSpec(memory_space=pl.ANY)
HBM_SPEC = pl.BlockSpec(memory_space=pltpu.HBM)
SEM_SPEC = pl.BlockSpec(memory_space=pltpu.SEMAPHORE)
EFFECT = pltpu.SideEffectType.DATAFLOW_SIDE_EFFECTING


def _params(n_axes):
    return pltpu.CompilerParams(dimension_semantics=("arbitrary",) * n_axes, vmem_limit_bytes=VMEM_LIMIT)


def _sigmoid(z):
    return 1.0 / (1.0 + jnp.exp(-z))


def _colsum8(v):
    r, c = v.shape
    return jnp.sum(v.reshape(r // SUB, SUB, c), axis=0)


def _position():
    x, y, c = lax.axis_index("x"), lax.axis_index("y"), lax.axis_index("c")
    return x, y, c


def _flat(p):
    return 4 * p[0] + 2 * p[1] + p[2]


def _all_gather(name, shards, deps=()):
    n, nd = len(shards), len(deps)

    def body(*refs):
        ins, outs = refs[:n], refs[n + nd:2 * n + nd]
        send_sems, recv_sems, local_sems = refs[2 * n + nd:]
        x, y, c = _position()
        me, sibling = (x, y, c), (x, y, 1 - c)
        chips = [(1 - x, y), (x, 1 - y), (1 - x, 1 - y)]

        def copy(q, k, block, to, src=None):
            dst = outs[q].at[_flat(block)]
            return pltpu.make_async_remote_copy(
                src_ref=dst if src is None else src, dst_ref=dst,
                send_sem=send_sems.at[q, k], recv_sem=recv_sems.at[q, k],
                device_id=to, device_id_type=MESH)

        mine = [pltpu.make_async_copy(ins[q], outs[q].at[_flat(me)], local_sems.at[q]) for q in range(n)]
        for cp in mine:
            cp.start()
        first = []
        for q in range(n):
            first.append(copy(q, 0, me, sibling, src=ins[q]))
            for j, chip in enumerate(chips):
                first.append(copy(q, 1 + j, me, (*chip, c), src=ins[q]))
        for cp in first:
            cp.start()
        passed = []
        for q in range(n):
            for j, chip in enumerate(chips):
                copy(q, 1 + j, (*chip, c), me).wait_recv()
                fwd = copy(q, 4 + j, (*chip, c), sibling)
                fwd.start()
                passed.append(fwd)
        for q in range(n):
            copy(q, 0, sibling, me).wait_recv()
            for j, chip in enumerate(chips):
                copy(q, 4 + j, (*chip, 1 - c), me).wait_recv()
        for cp in first + passed:
            cp.wait_send()
        for cp in mine:
            cp.wait()

    return pl.pallas_call(
        body, name=name,
        in_specs=[ANY] * (n + nd), out_specs=[ANY] * n,
        out_shape=[jax.ShapeDtypeStruct((N_DEV,) + s.shape, s.dtype) for s in shards],
        scratch_shapes=[pltpu.SemaphoreType.DMA((n, 7)), pltpu.SemaphoreType.DMA((n, 7)),
                        pltpu.SemaphoreType.DMA((n,))],
    )(*shards, *deps)


N_COPIES = {"gather_ici": 4, "gather_d2d": 3, "gather_direct": 3, "gather_relay": 3, "gather_diag": 1,
            "reduce_d2d": 4, "reduce_ici": 3}


def _copy_plan(kind):
    x, y, c = _position()
    me, sibling = (x, y, c), (x, y, 1 - c)
    chips = [(1 - x, y), (x, 1 - y), (1 - x, 1 - y)]
    if kind == "gather_ici":
        return [(_flat(me), _flat(me), sibling)] + [(_flat(me), _flat(me), (*ch, c)) for ch in chips]
    if kind == "gather_d2d":
        return [(_flat((*ch, c)), _flat((*ch, c)), sibling) for ch in chips]
    if kind == "gather_direct":
        return [(_flat(me), _flat(me), sibling)] + [(_flat(me), _flat(me), (*ch, c)) for ch in chips[:2]]
    if kind == "gather_relay":
        held, to = (x ^ (1 - c), y ^ c, c), (x ^ c, y ^ (1 - c), c)
        return [(_flat(held), _flat(held), to)] + [(_flat((*ch, c)), _flat((*ch, c)), sibling) for ch in chips[:2]]
    if kind == "gather_diag":
        return [(_flat((*chips[2], c)), _flat((*chips[2], c)), sibling)]
    if kind == "reduce_d2d":
        return [(2 * chip + (1 - c), chip, sibling) for chip in range(4)]
    return [(2 * ch[0] + ch[1], 2 * x + y, (*ch, c)) for ch in chips]


def _planned_copies(kind, srcs, dsts, send_sems, recv_sems):
    plan = _copy_plan(kind)
    return [pltpu.make_async_remote_copy(
        src_ref=src.at[s_slot], dst_ref=dst.at[d_slot],
        send_sem=send_sems.at[q * len(plan) + k], recv_sem=recv_sems.at[q * len(plan) + k],
        device_id=to, device_id_type=MESH)
        for q, (src, dst) in enumerate(zip(srcs, dsts)) for k, (s_slot, d_slot, to) in enumerate(plan)]


def _remote_start(name, kind, srcs, lands=None, deps=()):
    n = len(srcs)
    bufs = list(srcs) + ([] if lands is None else list(lands))
    nb, nd = len(bufs), len(deps)
    nsem = n * N_COPIES[kind]

    def body(*refs):
        ins = refs[:nb]
        send_sems, recv_sems = refs[nb + nd], refs[nb + nd + 1]
        token = refs[-1]
        for cp in _planned_copies(kind, ins[:n], ins[:n] if lands is None else ins[n:], send_sems, recv_sems):
            cp.start()
        token[...] = jnp.zeros_like(token)

    outs = pl.pallas_call(
        body, name=name,
        out_shape=(pltpu.SemaphoreType.DMA((nsem,)), pltpu.SemaphoreType.DMA((nsem,)),
                   *[pltpu.HBM(b.shape, b.dtype) for b in bufs], jax.ShapeDtypeStruct((SUB, LANE), F32)),
        in_specs=[HBM_SPEC] * nb + [ANY] * nd,
        out_specs=(SEM_SPEC, SEM_SPEC, *[HBM_SPEC] * nb, pl.BlockSpec(memory_space=pltpu.VMEM)),
        input_output_aliases={i: 2 + i for i in range(nb)},
        compiler_params=pltpu.CompilerParams(has_side_effects=EFFECT),
    )(*[pltpu.with_memory_space_constraint(b, pltpu.HBM) for b in bufs], *deps)
    return outs[0], outs[1], list(outs[2:2 + nb]), outs[-1]


def _remote_wait(name, kind, send_sems, recv_sems, bufs, n, after):
    nb, na = len(bufs), len(after)
    same = nb == n

    def body(*refs):
        ins = refs[:nb]
        sends, recvs = refs[nb], refs[nb + 1]
        for cp in _planned_copies(kind, ins[:n], ins[:n] if same else ins[n:], sends, recvs):
            cp.wait_send()
            cp.wait_recv()

    outs = pl.pallas_call(
        body, name=name,
        out_shape=[pltpu.HBM(b.shape, b.dtype) for b in bufs],
        in_specs=[HBM_SPEC] * nb + [SEM_SPEC, SEM_SPEC] + [ANY] * na,
        out_specs=[HBM_SPEC] * nb,
        input_output_aliases={i: i for i in range(nb)},
        compiler_params=pltpu.CompilerParams(has_side_effects=EFFECT),
    )(*bufs, send_sems, recv_sems, *after)
    return list(outs)


def _remote_pass_on(name, done, send_sems, recv_sems, bufs, after, nxt):
    nb, na = len(bufs), len(after)
    nsem = nb * N_COPIES[nxt]

    def body(*refs):
        ins = refs[:nb]
        new_sends, new_recvs = refs[nb + 2 + na], refs[nb + 3 + na]
        token = refs[-1]
        for cp in _planned_copies(done, ins, ins, refs[nb], refs[nb + 1]):
            cp.wait_send()
            cp.wait_recv()
        for cp in _planned_copies(nxt, ins, ins, new_sends, new_recvs):
            cp.start()
        token[...] = jnp.zeros_like(token)

    outs = pl.pallas_call(
        body, name=name,
        out_shape=(pltpu.SemaphoreType.DMA((nsem,)), pltpu.SemaphoreType.DMA((nsem,)),
                   *[pltpu.HBM(b.shape, b.dtype) for b in bufs], jax.ShapeDtypeStruct((SUB, LANE), F32)),
        in_specs=[HBM_SPEC] * nb + [SEM_SPEC, SEM_SPEC] + [ANY] * na,
        out_specs=(SEM_SPEC, SEM_SPEC, *[HBM_SPEC] * nb, pl.BlockSpec(memory_space=pltpu.VMEM)),
        input_output_aliases={i: 2 + i for i in range(nb)},
        compiler_params=pltpu.CompilerParams(has_side_effects=EFFECT),
    )(*bufs, send_sems, recv_sems, *after)
    return outs[0], outs[1], list(outs[2:2 + nb]), outs[-1]


def _mm_cols(name, a, w, *, tm, blocks, epilogue, out_dtypes, into=(), deps=()):
    t, k = a.shape
    nblk, _, cb = w.shape
    j0, j1 = blocks
    no = len(out_dtypes)

    def body(a_ref, w_ref, *rest):
        acc = jnp.dot(a_ref[...], w_ref[0], preferred_element_type=F32)
        for o_ref, o in zip(rest[len(into) + len(deps):], epilogue(acc)):
            o_ref[...] = o.astype(o_ref.dtype)

    return pl.pallas_call(
        body, name=name, grid=(j1 - j0, t // tm),
        in_specs=[pl.BlockSpec((tm, k), lambda j, i: (i, 0)),
                  pl.BlockSpec((1, k, cb), lambda j, i: (j0 + j, 0, 0))] + [ANY] * (len(into) + len(deps)),
        out_specs=[pl.BlockSpec((tm, cb), lambda j, i: (i, j0 + j)) for _ in range(no)],
        out_shape=[jax.ShapeDtypeStruct((t, nblk * cb), dt) for dt in out_dtypes],
        input_output_aliases={2 + idx: idx for idx in range(len(into))},
        compiler_params=_params(2),
    )(a, w, *into, *deps)


MXU_WIDTH = 256


def _mm_cols_pairs(name, a, w, *, tm):
    t, k = a.shape
    nblk, _, cb = w.shape
    main = cb // MXU_WIDTH * MXU_WIDTH
    tail = cb - main
    assert 2 * tail == MXU_WIDTH and nblk % 2 == 0

    def body(a_ref, w_ref, o_ref):
        av = a_ref[...]
        for b in range(2):
            o_ref[:, b * cb:b * cb + main] = jnp.dot(av, w_ref[b, :, 0:main], preferred_element_type=F32)
        tails = jnp.dot(av, jnp.concatenate([w_ref[0, :, main:cb], w_ref[1, :, main:cb]], axis=1),
                        preferred_element_type=F32)
        for b in range(2):
            o_ref[:, b * cb + main:(b + 1) * cb] = tails[:, b * tail:(b + 1) * tail]

    return pl.pallas_call(
        body, name=name, grid=(nblk // 2, t // tm),
        in_specs=[pl.BlockSpec((tm, k), lambda j, i: (i, 0)),
                  pl.BlockSpec((2, k, cb), lambda j, i: (j, 0, 0))],
        out_specs=pl.BlockSpec((tm, 2 * cb), lambda j, i: (i, j)),
        out_shape=jax.ShapeDtypeStruct((t, nblk * cb), F32),
        compiler_params=_params(2),
    )(a, w)


def _mm_rows(name, a, w2d, *, tm, tn):
    t, kf = a.shape
    n = w2d.shape[1]

    def body(a_ref, w_ref, o_ref):
        o_ref[...] = jnp.dot(a_ref[...], w_ref[...], preferred_element_type=F32)

    return pl.pallas_call(
        body, name=name, grid=(t // tm, n // tn),
        in_specs=[pl.BlockSpec((tm, kf), lambda i, j: (i, 0)),
                  pl.BlockSpec((kf, tn), lambda i, j: (0, j))],
        out_specs=pl.BlockSpec((tm, tn), lambda i, j: (i, j)),
        out_shape=jax.ShapeDtypeStruct((t, n), F32),
        compiler_params=_params(2),
    )(a, w2d)


def _mm_nt_acc(name, dy, w, *, tm, tn, col_off=0, deps=()):
    t = dy.shape[0]
    nblk, k, cb = w.shape

    main = cb // MXU_WIDTH * MXU_WIDTH

    def body(dy_ref, w_ref, *rest):
        nt = (((1,), (1,)), ((), ()))
        acc = None
        for b in range(nblk):
            d = lax.dot_general(dy_ref[:, b * cb:b * cb + main], w_ref[b, :, 0:main], nt, preferred_element_type=F32)
            acc = d if acc is None else acc + d
        if main < cb:
            dy_tails = jnp.concatenate([dy_ref[:, b * cb + main:(b + 1) * cb] for b in range(nblk)], axis=1)
            w_tails = jnp.concatenate([w_ref[b, :, main:cb] for b in range(nblk)], axis=1)
            acc = acc + lax.dot_general(dy_tails, w_tails, nt, preferred_element_type=F32)
        rest[-1][...] = acc

    return pl.pallas_call(
        body, name=name, grid=(t // tm, k // tn),
        in_specs=[pl.BlockSpec((tm, nblk * cb), lambda i, j: (i, col_off)),
                  pl.BlockSpec((nblk, tn, cb), lambda i, j: (0, j, 0))] + [ANY] * len(deps),
        out_specs=pl.BlockSpec((tm, tn), lambda i, j: (i, j)),
        out_shape=jax.ShapeDtypeStruct((t, k), F32),
        compiler_params=_params(2),
    )(dy, w, *deps)


def _mm_nt_blocks(name, dy, w2d, *, tm, tkb, extra=(), epilogue=None, out_dtypes=(F32,)):
    t, n = dy.shape
    kf = w2d.shape[0]
    ne = len(extra)

    def body(dy_ref, w_ref, *rest):
        acc = lax.dot_general(dy_ref[...], w_ref[...], (((1,), (1,)), ((), ())), preferred_element_type=F32)
        outs = (acc,) if epilogue is None else epilogue(acc, *[e[...] for e in rest[:ne]])
        for o_ref, o in zip(rest[ne:], outs):
            o_ref[...] = o.astype(o_ref.dtype)

    return pl.pallas_call(
        body, name=name, grid=(kf // tkb, t // tm),
        in_specs=[pl.BlockSpec((tm, n), lambda kb, i: (i, 0)),
                  pl.BlockSpec((tkb, n), lambda kb, i: (kb, 0))]
                 + [pl.BlockSpec((tm, tkb), lambda kb, i: (i, kb)) for _ in extra],
        out_specs=[pl.BlockSpec((tm, tkb), lambda kb, i: (i, kb)) for _ in out_dtypes],
        out_shape=[jax.ShapeDtypeStruct((t, kf), dt) for dt in out_dtypes],
        compiler_params=_params(2),
    )(dy, w2d, *extra)


def _mm_tn(name, a, b, me_arr, *, m, n, tma, tn, sharded, a_off=0, b_off=0, deps=()):
    t = a.shape[0]
    if sharded == "cols":
        cb = n // N_DEV
        nb, q = max(tn // cb, 1), max(cb // tn, 1)
        tw = tn // nb
        full_shape, own_shape = (N_DEV, m, cb), (m, cb)
        full_spec = pl.BlockSpec((nb, tma, tw), lambda i, j, me: (j // q, i, j % q))
    else:
        kb = m // N_DEV
        p = kb // tma
        nb, tw = 1, tn
        full_shape, own_shape = (m, n), (kb, n)
        full_spec = pl.BlockSpec((tma, tn), lambda i, j, me: (i, j))

    def body(me_ref, a_ref, b_ref, *rest):
        full_ref, own_ref, stage, sem = rest[len(deps):]
        i, j = pl.program_id(0), pl.program_id(1)
        acc = lax.dot_general(a_ref[...], b_ref[...], (((0,), (0,)), ((), ())), preferred_element_type=F32)
        for blk in range(nb):
            part = acc[:, blk * tw:(blk + 1) * tw]
            if sharded == "cols":
                full_ref[blk] = part.astype(BF16)
                owner, r0, c0 = (j // q) * nb + blk, i * tma, (j % q) * tw
            else:
                full_ref[...] = part.astype(BF16)
                owner, r0, c0 = i // p, (i % p) * tma, j * tn

            @pl.when(owner == me_ref[0])
            def _():
                stage[...] = part
                cp = pltpu.make_async_copy(
                    stage, own_ref.at[pl.ds(pl.multiple_of(r0, tma), tma), pl.ds(pl.multiple_of(c0, tw), tw)], sem)
                cp.start()
                cp.wait()

    full, own = pl.pallas_call(
        body, name=name,
        grid_spec=pltpu.PrefetchScalarGridSpec(
            num_scalar_prefetch=1, grid=(m // tma, n // tn),
            in_specs=[pl.BlockSpec((t, tma), lambda i, j, me: (0, a_off + i)),
                      pl.BlockSpec((t, tn), lambda i, j, me: (0, b_off + j))] + [ANY] * len(deps),
            out_specs=[full_spec, ANY],
            scratch_shapes=[pltpu.VMEM((tma, tw), F32), pltpu.SemaphoreType.DMA(())]),
        out_shape=[jax.ShapeDtypeStruct(full_shape, BF16), jax.ShapeDtypeStruct(own_shape, F32)],
        compiler_params=_params(2),
    )(me_arr, a, b, *deps)
    if sharded == "rows":
        full = full.reshape(N_DEV, m // N_DEV, n)
    return full, own


def _row_tile(t):
    return t // 8 if (t // 8) % 16 == 0 else ROW_TILE


def _row_call(name, body, t, row_ins, full_ins, row_outs, acc_outs, scratch=(), deps=()):
    tm = _row_tile(t)
    nin = len(row_ins) + len(full_ins)

    def without_deps(*refs):
        body(*refs[:nin], *refs[nin + len(deps):])

    return pl.pallas_call(
        without_deps, name=name, grid=(t // tm,),
        in_specs=[pl.BlockSpec((tm, a.shape[1]), lambda i: (i, 0)) for a in row_ins]
                 + [pl.BlockSpec(a.shape, lambda i: (0, 0)) for a in full_ins] + [ANY] * len(deps),
        out_specs=[pl.BlockSpec((tm, c), lambda i: (i, 0)) for c, _ in row_outs]
                  + [pl.BlockSpec((r, c), lambda i: (0, 0)) for r, c in acc_outs],
        out_shape=[jax.ShapeDtypeStruct((t, c), dt) for c, dt in row_outs]
                  + [jax.ShapeDtypeStruct((r, c), F32) for r, c in acc_outs],
        scratch_shapes=list(scratch),
        compiler_params=_params(1),
    )(*row_ins, *full_ins, *deps)


def _accumulate(ref, v):
    @pl.when(pl.program_id(0) == 0)
    def _():
        ref[...] = v

    @pl.when(pl.program_id(0) > 0)
    def _():
        ref[...] += v


def _rms(v):
    return lax.rsqrt(jnp.mean(v * v, axis=-1, keepdims=True) + RMS_EPS)


def _rms_bwd(dout, u, r, g):
    du = dout * g
    dx = r * (du - u * jnp.mean(du * u, axis=-1, keepdims=True))
    return dx, _colsum8(dout * u)


def _pre_norm(h0, g):
    t, d = h0.shape

    def body(h_ref, g_ref, n_ref):
        h = h_ref[...]
        n_ref[...] = (h * _rms(h) * g_ref[...]).astype(BF16)

    return _row_call("pre_norm", body, t, [h0], [g], [(d, BF16)], [])[0]


def _mix_post(m_mix, wo_full, h0, g_post, g_pre, deps=()):
    t, d = h0.shape
    tm = _row_tile(t)

    def body(m_ref, wo_ref, h0_ref, gp_ref, gq_ref, *rest):
        mix_ref, h1_ref, n2_ref = rest[len(deps):]
        mix_v = jnp.dot(m_ref[...], wo_ref[...], preferred_element_type=F32)
        mix_ref[...] = mix_v
        h1 = h0_ref[...] + mix_v * _rms(mix_v) * gp_ref[...]
        h1_ref[...] = h1
        n2_ref[...] = (h1 * _rms(h1) * gq_ref[...]).astype(BF16)

    tile = pl.BlockSpec((tm, d), lambda i: (i, 0))
    gain = pl.BlockSpec((1, d), lambda i: (0, 0))
    return pl.pallas_call(
        body, name="mix_post", grid=(t // tm,),
        in_specs=[tile, pl.BlockSpec((d, d), lambda i: (0, 0)), tile, gain, gain] + [ANY] * len(deps),
        out_specs=[tile, tile, tile],
        out_shape=[jax.ShapeDtypeStruct((t, d), F32), jax.ShapeDtypeStruct((t, d), F32),
                   jax.ShapeDtypeStruct((t, d), BF16)],
        compiler_params=_params(1),
    )(m_mix, wo_full, h0, g_post, g_pre, *deps)


def _loss_head(fo, h1, tgt, g_post_mlp, t_real):
    t, d = h1.shape
    tile = _row_tile(t)

    def body(fo_ref, h1_ref, tgt_ref, g_ref, dfo_ref, dh2_ref, dg_ref, loss_ref, lacc):
        i = pl.program_id(0)
        fo_v = fo_ref[...]
        g = g_ref[...]
        r = _rms(fo_v)
        u = fo_v * r
        h2 = h1_ref[...] + u * g
        row = i * tile + lax.broadcasted_iota(jnp.int32, (tile, 1), 0)
        valid = jnp.logical_and(row >= N_META, row < t_real)
        diff = jnp.where(valid, h2 - tgt_ref[...], 0.0)
        dh2 = diff * (1.0 / d)
        dh2_ref[...] = dh2
        dfo, dg = _rms_bwd(dh2, u, r, g)
        dfo_ref[...] = dfo.astype(BF16)
        _accumulate(dg_ref, dg)
        _accumulate(lacc, _colsum8(diff * diff))

        @pl.when(i == pl.num_programs(0) - 1)
        def _():
            loss_ref[...] = jnp.full((SUB, LANE), (0.5 / d) * jnp.sum(lacc[...]), F32)

    return _row_call("loss_head", body, t, [fo, h1, tgt], [g_post_mlp],
                     [(d, BF16), (d, F32)], [(SUB, d), (SUB, LANE)], scratch=[pltpu.VMEM((SUB, d), F32)])


def _mid_norm_bwd(dn2, h1, dh2, mix, g_pre_mlp, g_post_mix, deps=()):
    t, d = h1.shape

    def body(dn2_ref, h1_ref, dh2_ref, mix_ref, gq_ref, gp_ref, dh1_ref, dmix_ref, dgq_ref, dgp_ref):
        h1 = h1_ref[...]
        r3 = _rms(h1)
        dx, dgq = _rms_bwd(dn2_ref[...], h1 * r3, r3, gq_ref[...])
        dh1 = dh2_ref[...] + dx
        dh1_ref[...] = dh1
        mix_v = mix_ref[...]
        r2 = _rms(mix_v)
        dmix, dgp = _rms_bwd(dh1, mix_v * r2, r2, gp_ref[...])
        dmix_ref[...] = dmix.astype(BF16)
        _accumulate(dgq_ref, dgq)
        _accumulate(dgp_ref, dgp)

    return _row_call("mid_norm_bwd", body, t, [dn2, h1, dh2, mix], [g_pre_mlp, g_post_mix],
                     [(d, F32), (d, BF16)], [(SUB, d), (SUB, d)], deps=deps)


def _pre_norm_bwd(dn, h0, dh1, g_pre_mix, deps=()):
    t, d = h0.shape

    def body(dn_ref, h0_ref, dh1_ref, g_ref, dh0_ref, dg_ref):
        h0 = h0_ref[...]
        r = _rms(h0)
        dx, dg = _rms_bwd(dn_ref[...], h0 * r, r, g_ref[...])
        dh0_ref[...] = dh1_ref[...] + dx
        _accumulate(dg_ref, dg)

    return _row_call("pre_norm_bwd", body, t, [dn, h0, dh1], [g_pre_mix], [(d, F32)], [(SUB, d)], deps=deps)


def _layer_norm_silu(a1, ln_g, ln_b, deps=()):
    t, c = a1.shape

    def body(a1_ref, g_ref, b_ref, a3_ref):
        a = a1_ref[...]
        mu = jnp.mean(a, axis=-1, keepdims=True)
        xc = a - mu
        rstd = lax.rsqrt(jnp.mean(xc * xc, axis=-1, keepdims=True) + LN_EPS)
        z = xc * rstd * g_ref[...] + b_ref[...]
        a3_ref[...] = (z * _sigmoid(z)).astype(BF16)

    return _row_call("layer_norm_silu", body, t, [a1], [ln_g, ln_b], [(c, BF16)], [], deps=deps)[0]


def _layer_norm_silu_bwd(da3, a1, ln_g, ln_b, deps=()):
    t, c = a1.shape

    def body(da3_ref, a1_ref, g_ref, b_ref, da1_ref, dg_ref, db_ref):
        a = a1_ref[...]
        g = g_ref[...]
        mu = jnp.mean(a, axis=-1, keepdims=True)
        xc = a - mu
        rstd = lax.rsqrt(jnp.mean(xc * xc, axis=-1, keepdims=True) + LN_EPS)
        xhat = xc * rstd
        z = xhat * g + b_ref[...]
        sg = _sigmoid(z)
        dz = da3_ref[...] * (sg * (1.0 + z * (1.0 - sg)))
        dxhat = dz * g
        da1_ref[...] = rstd * (dxhat - jnp.mean(dxhat, axis=-1, keepdims=True)
                               - xhat * jnp.mean(dxhat * xhat, axis=-1, keepdims=True))
        _accumulate(dg_ref, _colsum8(dz * xhat))
        _accumulate(db_ref, _colsum8(dz))

    return _row_call("layer_norm_silu_bwd", body, t, [da3, a1], [ln_g, ln_b], [(c, F32)], [(SUB, c), (SUB, c)], deps=deps)


def _branch_merge(a3, s, wpw, wso, proj, b_gates, d, deps=()):
    t, cols = proj.shape
    nblk, k, cb = wpw.shape
    w = 1024
    nh = d // w
    per = w // cb
    ga0 = (cols - 2 * d) // w
    tm = _row_tile(t)

    def body(a3_ref, s_ref, wpw_ref, wso_ref, *rest):
        pa_refs, pb_refs, bg_ref = rest[:nh], rest[nh:2 * nh], rest[2 * nh]
        ya_ref, yb_ref, ga_ref, gb_ref, m_ref = rest[2 * nh + 1 + len(deps):]
        a3v, sv = a3_ref[...], s_ref[...]
        for b in range(nblk):
            here = slice(b * cb, (b + 1) * cb)
            local = slice((b % per) * cb, (b % per + 1) * cb)
            ya = jnp.dot(a3v, wpw_ref[b], preferred_element_type=F32)
            yb = jnp.dot(sv, wso_ref[b], preferred_element_type=F32)
            ga = _sigmoid(pa_refs[b // per][:, local] + bg_ref[:, here])
            gb = _sigmoid(pb_refs[b // per][:, local] + bg_ref[:, d + b * cb:d + (b + 1) * cb])
            ya_ref[:, here] = ya.astype(BF16)
            yb_ref[:, here] = yb.astype(BF16)
            ga_ref[:, here] = ga.astype(BF16)
            gb_ref[:, here] = gb.astype(BF16)
            m_ref[:, here] = (ga * ya + gb * yb).astype(BF16)

    tile = pl.BlockSpec((tm, d), lambda i: (i, 0))
    return pl.pallas_call(
        body, name="branch_merge", grid=(t // tm,),
        in_specs=[pl.BlockSpec((tm, k), lambda i: (i, 0)), pl.BlockSpec((tm, k), lambda i: (i, 0)),
                  pl.BlockSpec((nblk, k, cb), lambda i: (0, 0, 0)), pl.BlockSpec((nblk, k, cb), lambda i: (0, 0, 0))]
                 + [pl.BlockSpec((tm, w), lambda i, h=h: (i, ga0 + h)) for h in range(2 * nh)]
                 + [pl.BlockSpec((1, 2 * d), lambda i: (0, 0))] + [ANY] * len(deps),
        out_specs=[tile] * 5,
        out_shape=[jax.ShapeDtypeStruct((t, d), BF16)] * 5,
        compiler_params=_params(1),
    )(a3, s, wpw, wso, *([proj] * (2 * nh)), b_gates, *deps)


def _gate_backward(dmix, wo_full, ga, gb, ya, yb, cols, tm, deps=()):
    t, d = ya.shape
    w = 1024
    nh = d // w
    ga0 = (cols - 2 * d) // w

    def body(dmix_ref, wo_ref, ga_ref, gb_ref, ya_ref, yb_ref, *rest):
        dya_ref, dyb_ref, dp_ref, dba_ref, dbb_ref, stage, sems = rest[len(deps):]
        h, i = pl.program_id(0), pl.program_id(1)
        dm = lax.dot_general(dmix_ref[...], wo_ref[...], (((1,), (1,)), ((), ())), preferred_element_type=F32)
        ga = ga_ref[...].astype(F32)
        gb = gb_ref[...].astype(F32)
        dya_ref[...] = (dm * ga).astype(BF16)
        dyb_ref[...] = (dm * gb).astype(BF16)
        dpa = dm * ya_ref[...].astype(F32) * ga * (1.0 - ga)
        dpb = dm * yb_ref[...].astype(F32) * gb * (1.0 - gb)
        stage[0] = dpa.astype(BF16)
        stage[1] = dpb.astype(BF16)
        rows = pl.ds(pl.multiple_of(i * tm, tm), tm)
        copies = [pltpu.make_async_copy(
            stage.at[g], dp_ref.at[rows, pl.ds(pl.multiple_of((ga0 + g * nh + h) * w, w), w)], sems.at[g])
            for g in range(2)]
        for cp in copies:
            cp.start()

        @pl.when(i == 0)
        def _():
            dba_ref[...] = _colsum8(dpa)
            dbb_ref[...] = _colsum8(dpb)

        @pl.when(i > 0)
        def _():
            dba_ref[...] += _colsum8(dpa)
            dbb_ref[...] += _colsum8(dpb)

        for cp in copies:
            cp.wait()

    tile = pl.BlockSpec((tm, w), lambda h, i: (i, h))
    return pl.pallas_call(
        body, name="gate_backward", grid=(nh, t // tm),
        in_specs=[pl.BlockSpec((tm, d), lambda h, i: (i, 0)),
                  pl.BlockSpec((w, d), lambda h, i: (h, 0)),
                  tile, tile, tile, tile] + [ANY] * len(deps),
        out_specs=[tile, tile, ANY,
                   pl.BlockSpec((SUB, w), lambda h, i: (0, h)),
                   pl.BlockSpec((SUB, w), lambda h, i: (0, h))],
        out_shape=[jax.ShapeDtypeStruct((t, d), BF16), jax.ShapeDtypeStruct((t, d), BF16),
                   jax.ShapeDtypeStruct((t, cols), BF16),
                   jax.ShapeDtypeStruct((SUB, d), F32), jax.ShapeDtypeStruct((SUB, d), F32)],
        scratch_shapes=[pltpu.VMEM((2, tm, w), BF16), pltpu.SemaphoreType.DMA((2,))],
        compiler_params=_params(2),
    )(dmix, wo_full, ga, gb, ya, yb, *deps)


def _shifted_views(win, offsets):
    n = win.shape[0]
    rotated = {}
    views = {}
    for o in offsets:
        q, r = divmod(o, SUB)
        if r not in rotated:
            rotated[r] = win if r == 0 else pltpu.roll(win, n - r, 0)
        views[o] = rotated[r][q * SUB:q * SUB + CONV_CHUNK]
    return views


def _causal_views(xp_ref, ntap, r0):
    win = xp_ref[pl.ds(r0, CONV_CHUNK + CONV_PAD), :]
    views = _shifted_views(win, [CONV_PAD - (ntap - 1 - k) for k in range(ntap)])
    return [views[CONV_PAD - (ntap - 1 - k)] for k in range(ntap)]


def _causal_conv(xp_ref, w_ref, ntap, r0):
    acc = None
    for k, shifted in enumerate(_causal_views(xp_ref, ntap, r0)):
        term = w_ref[k:k + 1, :] * shifted
        acc = term if acc is None else acc + term
    return acc


def _anticausal_conv(xp_ref, w_ref, ntap, r0):
    win = xp_ref[pl.ds(pl.multiple_of(CONV_PAD + r0, CONV_PAD), CONV_CHUNK + CONV_PAD), :]
    views = _shifted_views(win, [ntap - 1 - k for k in range(ntap)])
    acc = None
    for k in range(ntap):
        term = w_ref[k:k + 1, :] * views[ntap - 1 - k]
        acc = term if acc is None else acc + term
    return acc


def _conv_weight_grad(dw_ref, d_chunk, xp_ref, ntap, r0):
    for k, shifted in enumerate(_causal_views(xp_ref, ntap, r0)):
        dw_ref[k * SUB:(k + 1) * SUB, :] += _colsum8(d_chunk * shifted)


def _zero_pads(ref, t):
    ref[0:CONV_PAD, :] = jnp.zeros((CONV_PAD, LANE), F32)
    ref[CONV_PAD + t:CONV_PAD + t + CONV_PAD, :] = jnp.zeros((CONV_PAD, LANE), F32)


def _for_chunks(t, fn):
    def step(idx, carry):
        fn(pl.multiple_of(idx * CONV_CHUNK, CONV_CHUNK))
        return carry

    lax.fori_loop(0, t // CONV_CHUNK, step, 0)


def _conv_forward(proj, conf_w, conf_b, short_w, dc, deps=()):
    t = proj.shape[0]
    nc = dc // LANE

    def body(av_ref, ag_ref, bg_ref, cg_ref, v_ref, cw_ref, cb_ref, sw_ref, *rest):
        a1_ref, s_ref, xa, xb = rest[len(deps):]
        _zero_pads(xa, t)
        _zero_pads(xb, t)
        xa[CONV_PAD:CONV_PAD + t, :] = av_ref[...] * _sigmoid(ag_ref[...])
        xb[CONV_PAD:CONV_PAD + t, :] = cg_ref[...] * v_ref[...]

        def chunk(r0):
            rs = pl.ds(r0, CONV_CHUNK)
            a1_ref[rs, :] = _causal_conv(xa, cw_ref, CONF_K, r0) + cb_ref[...]
            s_ref[rs, :] = (bg_ref[rs, :] * _causal_conv(xb, sw_ref, SHORT_K, r0)).astype(BF16)

        _for_chunks(t, chunk)

    col = lambda g: pl.BlockSpec((t, LANE), lambda c, g=g: (0, g * nc + c))
    return pl.pallas_call(
        body, name="conv_forward", grid=(nc,),
        in_specs=[col(0), col(1), col(2), col(3), col(4),
                  pl.BlockSpec((CONF_K, LANE), lambda c: (0, c)),
                  pl.BlockSpec((1, LANE), lambda c: (0, c)),
                  pl.BlockSpec((SHORT_K, LANE), lambda c: (0, c))] + [ANY] * len(deps),
        out_specs=[pl.BlockSpec((t, LANE), lambda c: (0, c)), pl.BlockSpec((t, LANE), lambda c: (0, c))],
        out_shape=[jax.ShapeDtypeStruct((t, dc), F32), jax.ShapeDtypeStruct((t, dc), BF16)],
        scratch_shapes=[pltpu.VMEM((t + 2 * CONV_PAD, LANE), F32), pltpu.VMEM((t + 2 * CONV_PAD, LANE), F32)],
        compiler_params=_params(1),
    )(proj, proj, proj, proj, proj, conf_w, conf_b, short_w, *deps)


def _conv_backward(dproj, proj, da1, ds, conf_w, short_w, dc):
    t = proj.shape[0]
    nc = dc // LANE

    def body(dp_in, av_ref, ag_ref, bg_ref, cg_ref, v_ref, da1_ref, ds_ref, cw_ref, sw_ref,
             dp_ref, dcw_ref, dcb_ref, dsw_ref, xa, xb, da, db, stage, sems):
        del dp_in
        c = pl.program_id(0)
        for ref in (xa, xb, da, db):
            _zero_pads(ref, t)
        xa[CONV_PAD:CONV_PAD + t, :] = av_ref[...] * _sigmoid(ag_ref[...])
        xb[CONV_PAD:CONV_PAD + t, :] = cg_ref[...] * v_ref[...]
        da[CONV_PAD:CONV_PAD + t, :] = da1_ref[...]
        dcw_ref[...] = jnp.zeros(dcw_ref.shape, F32)
        dsw_ref[...] = jnp.zeros(dsw_ref.shape, F32)
        dcb_ref[...] = jnp.zeros(dcb_ref.shape, F32)

        def through_gate(r0):
            rs = pl.ds(r0, CONV_CHUNK)
            ds_c = ds_ref[rs, :]
            stage[2, rs, :] = (ds_c * _causal_conv(xb, sw_ref, SHORT_K, r0)).astype(BF16)
            db[pl.ds(pl.multiple_of(CONV_PAD + r0, CONV_PAD), CONV_CHUNK), :] = ds_c * bg_ref[rs, :]

        _for_chunks(t, through_gate)

        def through_convs(r0):
            rs = pl.ds(r0, CONV_CHUNK)
            da0 = _anticausal_conv(da, cw_ref, CONF_K, r0)
            sg = _sigmoid(ag_ref[rs, :])
            stage[0, rs, :] = (da0 * sg).astype(BF16)
            stage[1, rs, :] = (da0 * av_ref[rs, :] * sg * (1.0 - sg)).astype(BF16)
            dcv = _anticausal_conv(db, sw_ref, SHORT_K, r0)
            stage[3, rs, :] = (dcv * v_ref[rs, :]).astype(BF16)
            stage[4, rs, :] = (dcv * cg_ref[rs, :]).astype(BF16)
            da1_c = da1_ref[rs, :]
            _conv_weight_grad(dcw_ref, da1_c, xa, CONF_K, r0)
            _conv_weight_grad(dsw_ref, ds_ref[rs, :] * bg_ref[rs, :], xb, SHORT_K, r0)
            dcb_ref[...] += _colsum8(da1_c)

        _for_chunks(t, through_convs)
        copies = [pltpu.make_async_copy(
            stage.at[g], dp_ref.at[:, pl.ds(pl.multiple_of((g * nc + c) * LANE, LANE), LANE)], sems.at[g])
            for g in range(5)]
        for cp in copies:
            cp.start()
        for cp in copies:
            cp.wait()

    col = lambda g: pl.BlockSpec((t, LANE), lambda c, g=g: (0, g * nc + c))
    blk = pl.BlockSpec((t, LANE), lambda c: (0, c))
    return pl.pallas_call(
        body, name="conv_backward", grid=(nc,),
        in_specs=[ANY, col(0), col(1), col(2), col(3), col(4), blk, blk,
                  pl.BlockSpec((CONF_K, LANE), lambda c: (0, c)),
                  pl.BlockSpec((SHORT_K, LANE), lambda c: (0, c))],
        out_specs=[ANY,
                   pl.BlockSpec((CONF_K * SUB, LANE), lambda c: (0, c)),
                   pl.BlockSpec((SUB, LANE), lambda c: (0, c)),
                   pl.BlockSpec((SHORT_K * SUB, LANE), lambda c: (0, c))],
        out_shape=[jax.ShapeDtypeStruct(dproj.shape, dproj.dtype),
                   jax.ShapeDtypeStruct((CONF_K * SUB, dc), F32),
                   jax.ShapeDtypeStruct((SUB, dc), F32),
                   jax.ShapeDtypeStruct((SHORT_K * SUB, dc), F32)],
        scratch_shapes=[pltpu.VMEM((t + 2 * CONV_PAD, LANE), F32)] * 4
                       + [pltpu.VMEM((5, t, LANE), BF16), pltpu.SemaphoreType.DMA((5,))],
        input_output_aliases={0: 0},
        compiler_params=_params(1),
    )(dproj, proj, proj, proj, proj, proj, da1, ds, conf_w, short_w)


def _adamw_math(w, g, m, v):
    m = ADAM_B1 * m + (1.0 - ADAM_B1) * g
    v = ADAM_B2 * v + (1.0 - ADAM_B2) * (g * g)
    m_hat = m / (1.0 - ADAM_B1 ** ADAM_STEP)
    v_hat = v / (1.0 - ADAM_B2 ** ADAM_STEP)
    delta = -ADAM_LR * (m_hat / (jnp.sqrt(v_hat) + ADAM_EPS) + ADAM_WD * w)
    return delta, m, v


def _cast_into_slot(name, w, me_arr, deps=()):
    r, c = w.shape
    tr = 256

    def body(me_ref, w_ref, *rest):
        del me_ref
        rest[-1][0] = w_ref[...].astype(BF16)

    return pl.pallas_call(
        body, name=name,
        grid_spec=pltpu.PrefetchScalarGridSpec(
            num_scalar_prefetch=1, grid=(r // tr,),
            in_specs=[pl.BlockSpec((tr, c), lambda i, me: (i, 0))] + [ANY] * len(deps),
            out_specs=pl.BlockSpec((1, tr, c), lambda i, me: (me[0], i, 0))),
        out_shape=jax.ShapeDtypeStruct((N_DEV, r, c), BF16),
        compiler_params=_params(1),
    )(me_arr, w, *deps)


def _chip_sum(name, full, from_sibling, me_arr):
    _, r, c = full.shape
    tr = min(r, 512)

    def body(me_ref, full_ref, sib_ref, sums_ref):
        del me_ref
        sums_ref[0] = (full_ref[0].astype(F32) + sib_ref[0].astype(F32)).astype(BF16)

    other = lambda k, me: (me[0] // 2 + 1 + k) % 4
    return pl.pallas_call(
        body, name=name,
        grid_spec=pltpu.PrefetchScalarGridSpec(
            num_scalar_prefetch=1, grid=(r // tr, 3),
            in_specs=[pl.BlockSpec((1, tr, c), lambda i, k, me: (2 * other(k, me) + me[0] % 2, i, 0)),
                      pl.BlockSpec((1, tr, c), lambda i, k, me: (other(k, me), i, 0))],
            out_specs=pl.BlockSpec((1, tr, c), lambda i, k, me: (other(k, me), i, 0))),
        out_shape=jax.ShapeDtypeStruct((4, r, c), BF16),
        compiler_params=_params(2),
    )(me_arr, full, from_sibling)


def _adamw_shard(name, w, m, v, parts, me_arr, deps=()):
    r, c = w.shape
    tr = min(256, r // len(parts))
    np_ = len(parts)
    per = r // np_ // tr

    def body(me_ref, w_ref, m_ref, v_ref, *rest):
        g_out, d_out, m_out, v_out = rest[5 * np_ + len(deps):]
        g = None
        for p in range(np_):
            gp = rest[5 * p][...]
            for l_ref in rest[5 * p + 1:5 * p + 5]:
                gp = gp + l_ref[0].astype(F32)
            g = gp if g is None else jnp.where(pl.program_id(0) // per == p, gp, g)
        delta, m_new, v_new = _adamw_math(w_ref[...], g, m_ref[...], v_ref[...])
        g_out[...] = g
        d_out[...] = delta
        m_out[...] = m_new
        v_out[...] = v_new

    tile = pl.BlockSpec((tr, c), lambda i, me: (i, 0))
    part_specs, part_args = [], []
    for p, (g_own, from_sibling, landed) in enumerate(parts):
        row = lambda i, p=p: jnp.clip(i - p * per, 0, per - 1)
        part_specs.append(pl.BlockSpec((tr, c), lambda i, me, row=row: (row(i), 0)))
        part_specs += [pl.BlockSpec((1, tr, c), lambda i, me, k=k, row=row: ((me[0] // 2 + k) % 4, row(i), 0))
                       for k in range(4)]
        part_args += [g_own, from_sibling, landed, landed, landed]
    return pl.pallas_call(
        body, name=name,
        grid_spec=pltpu.PrefetchScalarGridSpec(
            num_scalar_prefetch=1, grid=(r // tr,),
            in_specs=[tile] * 3 + part_specs + [ANY] * len(deps), out_specs=[tile] * 4),
        out_shape=[jax.ShapeDtypeStruct((r, c), F32)] * 4,
        compiler_params=_params(1),
    )(me_arr, w, m, v, *part_args, *deps)


SMALL_W = 1024
VEC_ROWS = 16
LOSS_ROW = 15
META_ROW0 = 16
CONF_ROW0 = 64
SHORT_ROW0 = 96
SMALL_ROWS = 104


def _pack_small(vec_parts, dmeta, dcw, dsw, loss_blk, me_arr):
    widths = [p.shape[1] for p in vec_parts]
    nv = len(vec_parts)

    def body(me_ref, *refs):
        del me_ref
        parts, (dmeta_ref, dcw_ref, dsw_ref, loss_ref, out_ref) = refs[:nv], refs[nv:]
        out_ref[0] = jnp.zeros((SMALL_ROWS, SMALL_W), F32)
        out_ref[0, LOSS_ROW:LOSS_ROW + 1, 0:LANE] = loss_ref[0:1, :]
        row = 0
        for p_ref, wd in zip(parts, widths):
            s = jnp.sum(p_ref[...], axis=0, keepdims=True)
            for h in range(wd // SMALL_W):
                out_ref[0, row:row + 1, :] = s[:, h * SMALL_W:(h + 1) * SMALL_W]
                row += 1
        for h in range(dmeta_ref.shape[1] // SMALL_W):
            out_ref[0, META_ROW0 + h * N_META:META_ROW0 + (h + 1) * N_META, :] = dmeta_ref[:, h * SMALL_W:(h + 1) * SMALL_W]
        for k in range(CONF_K):
            out_ref[0, CONF_ROW0 + k:CONF_ROW0 + k + 1, :] = jnp.sum(dcw_ref[k * SUB:(k + 1) * SUB, :], axis=0, keepdims=True)
        for k in range(SHORT_K):
            out_ref[0, SHORT_ROW0 + k:SHORT_ROW0 + k + 1, :] = jnp.sum(dsw_ref[k * SUB:(k + 1) * SUB, :], axis=0, keepdims=True)

    ins = [*vec_parts, dmeta, dcw, dsw, loss_blk]
    return pl.pallas_call(
        body, name="pack_small",
        grid_spec=pltpu.PrefetchScalarGridSpec(
            num_scalar_prefetch=1, grid=(1,),
            in_specs=[pl.BlockSpec(a.shape, lambda i, me: (0, 0)) for a in ins],
            out_specs=pl.BlockSpec((1, SMALL_ROWS, SMALL_W), lambda i, me: (me[0], 0, 0))),
        out_shape=jax.ShapeDtypeStruct((N_DEV, SMALL_ROWS, SMALL_W), F32),
        compiler_params=_params(1),
    )(me_arr, *ins)


def _small_update(gathered, me_arr, vec_params, meta_p, conf_p, short_p):
    widths = [p[0].shape[1] for p in vec_params]
    nv = len(vec_params)
    mcols = meta_p[0].shape[1]
    per_row = SMALL_W // mcols

    def body(me_ref, gv_ref, gm_ref, gc_ref, gs_ref, *rest):
        del me_ref
        ins, outs = rest[:3 * (nv + 3)], rest[3 * (nv + 3):]

        def total(ref, r0, rows):
            s = ref[0, r0:r0 + rows, :]
            for dev in range(1, N_DEV):
                s = s + ref[dev, r0:r0 + rows, :]
            return s

        grads = []
        row = 0
        for wd in widths:
            pieces = [total(gv_ref, row + h, 1) for h in range(wd // SMALL_W)]
            grads.append(pieces[0] if len(pieces) == 1 else jnp.concatenate(pieces, axis=1))
            row += len(pieces)
        grads.append(total(gm_ref, 0, N_META))
        grads.append(total(gc_ref, 0, CONF_K))
        grads.append(total(gs_ref, 0, SHORT_K))
        loss = gv_ref[0, LOSS_ROW:LOSS_ROW + 1, 0:LANE]
        for dev in range(1, N_DEV):
            loss = loss + gv_ref[dev, LOSS_ROW:LOSS_ROW + 1, 0:LANE]
        outs[-1][...] = loss
        for idx, g in enumerate(grads):
            w_ref, m_ref, v_ref = ins[3 * idx:3 * idx + 3]
            delta, m_new, v_new = _adamw_math(w_ref[...], g, m_ref[...], v_ref[...])
            g_out, d_out, m_out, v_out = outs[4 * idx:4 * idx + 4]
            g_out[...] = g
            d_out[...] = delta
            m_out[...] = m_new
            v_out[...] = v_new

    params = list(vec_params) + [meta_p, conf_p, short_p]
    flat = [a for p in params for a in p]
    whole = lambda a: pl.BlockSpec(a.shape, lambda i, me: (0,) * a.ndim)
    outs = pl.pallas_call(
        body, name="small_update",
        grid_spec=pltpu.PrefetchScalarGridSpec(
            num_scalar_prefetch=1, grid=(1,),
            in_specs=[pl.BlockSpec((N_DEV, VEC_ROWS, SMALL_W), lambda i, me: (0, 0, 0)),
                      pl.BlockSpec((N_DEV, N_META, mcols),
                                   lambda i, me: (0, META_ROW0 // N_META + me[0] // per_row, me[0] % per_row)),
                      pl.BlockSpec((N_DEV, 32, LANE), lambda i, me: (0, CONF_ROW0 // 32, me[0])),
                      pl.BlockSpec((N_DEV, SUB, LANE), lambda i, me: (0, SHORT_ROW0 // SUB, me[0]))]
                     + [whole(a) for a in flat],
            out_specs=[whole(p[0]) for p in params for _ in range(4)]
                      + [pl.BlockSpec((1, LANE), lambda i, me: (0, 0))]),
        out_shape=[jax.ShapeDtypeStruct(p[0].shape, F32) for p in params for _ in range(4)]
                  + [jax.ShapeDtypeStruct((1, LANE), F32)],
        compiler_params=_params(1),
    )(me_arr, gathered, gathered, gathered, gathered, *flat)
    return [tuple(outs[4 * i:4 * i + 4]) for i in range(len(params))], outs[-1][0, 0]


def kernel(x, meta, g_pre_mix, w_in, b_gates, conf_dw_w, conf_dw_b, conf_ln_g, conf_ln_b, conf_w_pw, short_dw_w, short_w_out, w_o, g_post_mix, g_pre_mlp, w_up, w_down, g_post_mlp, loss_target, m_meta, m_g_pre_mix, m_w_in, m_b_gates, m_conf_dw_w, m_conf_dw_b, m_conf_ln_g, m_conf_ln_b, m_conf_w_pw, m_short_dw_w, m_short_w_out, m_w_o, m_g_post_mix, m_g_pre_mlp, m_w_up, m_w_down, m_g_post_mlp, v_meta, v_g_pre_mix, v_w_in, v_b_gates, v_conf_dw_w, v_conf_dw_b, v_conf_ln_g, v_conf_ln_b, v_conf_w_pw, v_short_dw_w, v_short_w_out, v_w_o, v_g_post_mix, v_g_pre_mlp, v_w_up, v_w_down, v_g_post_mlp):
    seq, d = x.shape[1], x.shape[2]
    dc = conf_w_pw.shape[1]
    t_real = N_META + seq
    t = -(-t_real // ROW_TILE) * ROW_TILE
    tm = t // 2
    assert tm % 16 == 0 and d % 1024 == 0 and dc % 1024 == 0
    x_idx, y_idx, c_idx = _position()
    me_arr = jnp.reshape(4 * x_idx + 2 * y_idx + c_idx, (1,)).astype(jnp.int32)

    big = [w_in[0], conf_w_pw[0], short_w_out[0], w_o[0], w_up[0], w_down[0]]
    big_names = ["w_in", "conf_w_pw", "short_w_out", "w_o", "w_up", "w_down"]
    groups = [[0], [1, 2, 3], [4], [5]]
    slots, deps = [], []
    for g, idxs in enumerate(groups):
        slots.append([_cast_into_slot("cast_" + big_names[i], big[i], me_arr, deps=deps) for i in idxs])
        if g == 0:
            direct0 = _remote_start("gather0_direct_start", "gather_direct", slots[0])
            deps = [direct0[3]]
    casts = [sl for group in slots[1:] for sl in group]
    meta_g, cw_g, sw_g = _all_gather("gather_small_params", [meta, conf_dw_w[0], short_dw_w[0]], deps=casts)

    def start_direct(g, deps):
        send, recv, bufs, tok = _remote_start("gather%d_direct_start" % g, "gather_direct", slots[g], deps=deps)
        return (send, recv, bufs), tok

    def relay(g, state, after):
        send, recv, bufs, tok = _remote_pass_on("gather%d_relay" % g, "gather_direct", *state, after, "gather_relay")
        return (send, recv, bufs), tok

    def gathered(g, state, after):
        send, recv, bufs, tok = _remote_pass_on("gather%d_diag" % g, "gather_relay", *state, after, "gather_diag")
        return _remote_wait("gather%d_diag_wait" % g, "gather_diag", send, recv, bufs, len(bufs), [tok])

    unshard =lambda g: jnp.transpose(g, (1, 0, 2)).reshape(g.shape[1], -1)
    meta_full, cw_full, sw_full = unshard(meta_g), unshard(cw_g), unshard(sw_g)

    relay0, tok = relay(0, direct0[:3], [meta_g])
    zrows = jnp.zeros((t - t_real, d), F32) + tok[0, 0] * 0.0
    h0 = jnp.concatenate([meta_full, x[0], zrows], axis=0)
    tgt = jnp.concatenate([jnp.zeros((N_META, d), F32), loss_target[0], zrows], axis=0)
    n = _pre_norm(h0, g_pre_mix)
    direct1, tok = start_direct(1, [tok])
    direct2, tok = start_direct(2, [tok])
    win_g, = gathered(0, relay0, [tok, n])
    proj = _mm_cols_pairs("proj", n, win_g, tm=tm // 2)
    relay1, tok = relay(1, direct1, [proj])
    a1, s = _conv_forward(proj, cw_full, conf_dw_b, sw_full, dc, deps=[tok])
    relay2, tok = relay(2, direct2, [a1])
    direct3, tok = start_direct(3, [tok])
    a3 = _layer_norm_silu(a1, conf_ln_g, conf_ln_b, deps=[tok])
    wpw_g, wso_g, wo_g = gathered(1, relay1, [a3])
    wo_full = wo_g.reshape(d, d)
    ya, yb, gate_a, gate_b, m_mix = _branch_merge(a3, s, wpw_g, wso_g, proj, b_gates, d)
    mix, h1, n2 = _mix_post(m_mix, wo_full, h0, g_post_mix, g_pre_mlp)
    wup_g, = gathered(2, relay2, [n2])

    def up_epilogue(acc):
        r = jnp.maximum(acc, 0.0)
        return r * r, r

    half_up = dict(tm=tm, epilogue=up_epilogue, out_dtypes=(BF16, BF16))
    f, relu_up = _mm_cols("mlp_up0", n2, wup_g, blocks=(0, N_DEV // 2), **half_up)
    relay3, tok = relay(3, direct3, [f])
    f, relu_up = _mm_cols("mlp_up1", n2, wup_g, blocks=(N_DEV // 2, N_DEV), into=(f, relu_up), deps=[tok], **half_up)
    wdn_g, = gathered(3, relay3, [f])
    wdn_full = wdn_g.reshape(-1, d)
    fo = _mm_rows("mlp_down", f, wdn_full, tm=tm // 2, tn=512)
    dfo, dh2, dg_post_mlp, loss_blk = _loss_head(fo, h1, tgt, g_post_mlp, t_real)

    def reduce_start(tag, fulls, deps):
        lands = [lax.empty((4,) + g.shape[1:], BF16) for g in fulls]
        send, recv, bufs, tok = _remote_start("reduce_%s_d2d_start" % tag, "reduce_d2d", fulls, lands, deps=deps)
        return (send, recv, bufs), tok

    def reduce_middle(tag, state, owns, after):
        send, recv, bufs = state
        k = len(owns)
        bufs = _remote_wait("reduce_%s_d2d_wait" % tag, "reduce_d2d", send, recv, bufs, k, after)
        from_sibling = bufs[k:]
        sums = [_chip_sum("chip_sum_%s%d" % (tag, i), bufs[i], from_sibling[i], me_arr) for i in range(k)]
        lands = [lax.empty(sm.shape, BF16) for sm in sums]
        send, recv, bufs, tok = _remote_start("reduce_%s_ici_start" % tag, "reduce_ici", sums, lands)
        return (send, recv, bufs, list(zip(owns, from_sibling))), tok

    def reduce_finish(tag, state, after):
        send, recv, bufs, local = state
        k = len(local)
        bufs = _remote_wait("reduce_%s_ici_wait" % tag, "reduce_ici", send, recv, bufs, k, after)
        return [(own, sib, landed) for (own, sib), landed in zip(local, bufs[k:])]

    dup = _mm_nt_blocks("d_up", dfo, wdn_full, tm=tm, tkb=1024, extra=(relu_up,),
                        epilogue=lambda acc, r: (acc * (2.0 * r.astype(F32)),), out_dtypes=(BF16,))[0]
    gw_down, gw_down_own = _mm_tn("dw_down", f, dfo, me_arr, m=f.shape[1], n=d, tma=512, tn=d, sharded="rows")
    red_down, tok = reduce_start("down", [gw_down], ())
    dn2 = _mm_nt_acc("d_n2", dup, wup_g, tm=tm // 2, tn=512, deps=[tok])
    gw_up, gw_up_own = _mm_tn("dw_up", n2, dup, me_arr, m=d, n=dup.shape[1], tma=512, tn=2048, sharded="cols")
    red_down, tok = reduce_middle("down", red_down, [gw_down_own], [dn2])
    red_up, tok = reduce_start("up", [gw_up], [tok])
    dh1, dmix, dg_pre_mlp, dg_post_mix = _mid_norm_bwd(dn2, h1, dh2, mix, g_pre_mlp, g_post_mix, deps=[tok])
    dya, dyb, dproj, db_a, db_b = _gate_backward(dmix, wo_full, gate_a, gate_b, ya, yb, proj.shape[1], tm // 2)
    db_gates = jnp.concatenate([db_a, db_b], axis=1)
    red_up, tok = reduce_middle("up", red_up, [gw_up_own], [dya])
    gw_o, gw_o_own = _mm_tn("dw_o", m_mix, dmix, me_arr, m=d, n=d, tma=d // N_DEV, tn=d, sharded="rows", deps=[tok])
    da3 = _mm_nt_acc("d_a3", dya, wpw_g, tm=tm, tn=512)
    gw_pw, gw_pw_own = _mm_tn("dw_pw", a3, dya, me_arr, m=dc, n=d, tma=512, tn=d, sharded="cols")
    dsb = _mm_nt_acc("d_s", dyb, wso_g, tm=tm, tn=512)
    gw_so, gw_so_own = _mm_tn("dw_so", s, dyb, me_arr, m=dc, n=d, tma=512, tn=d, sharded="cols")
    red_mix, tok = reduce_start("mix", [gw_pw, gw_so, gw_o], ())
    da1, dln_g, dln_b = _layer_norm_silu_bwd(da3, a1, conf_ln_g, conf_ln_b, deps=[tok])
    dproj, dcw, dcb, dsw = _conv_backward(dproj, proj, da1, dsb, cw_full, sw_full, dc)
    red_mix, tok = reduce_middle("mix", red_mix, [gw_pw_own, gw_so_own, gw_o_own], [dcb])
    in_cb = w_in.shape[2]
    half = d // 2
    red_in = []
    for part in range(2):
        gw, own = _mm_tn("dw_in%d" % part, n, dproj, me_arr, m=half, n=proj.shape[1], tma=512, tn=2 * in_cb,
                         sharded="cols", a_off=part * (half // 512), deps=[tok])
        state, tok = reduce_start("in%d" % part, [gw], ())
        red_in.append((state, own))
    for part in range(2):
        state, own = red_in[part]
        red_in[part], tok = reduce_middle("in%d" % part, state, [own], [tok])
    dn = _mm_nt_acc("d_n", dproj, win_g, tm=tm // 2, tn=512, deps=[tok])
    dh0, dg_pre_mix = _pre_norm_bwd(dn, h0, dh1, g_pre_mix)
    grad_x = dh0[N_META:t_real][None]

    vec_parts = [dg_pre_mix, db_gates, dcb, dln_g, dln_b, dg_post_mix, dg_pre_mlp, dg_post_mlp]
    packed = _pack_small(vec_parts, dh0[:N_META], dcw, dsw, loss_blk, me_arr)
    send, recv, bufs, tok = _remote_start("small_grads_ici_start", "gather_ici", [packed])
    vec_names = ["g_pre_mix", "b_gates", "conf_dw_b", "conf_ln_g", "conf_ln_b", "g_post_mix", "g_pre_mlp", "g_post_mlp"]
    env = locals()
    results = {}

    def update(nm, parts, deps=()):
        res = _adamw_shard("adamw_" + nm, env[nm][0], env["m_" + nm][0], env["v_" + nm][0], parts, me_arr, deps=deps)
        results[nm] = tuple(r[None] for r in res)
        return res[0]

    done = [update("w_down", reduce_finish("down", red_down, [tok]), deps=[tok])]
    done.append(update("w_up", reduce_finish("up", red_up, done)))
    bufs = _remote_wait("small_grads_ici_wait", "gather_ici", send, recv, bufs, 1, done)
    send, recv, bufs, tok = _remote_start("small_grads_d2d_start", "gather_d2d", bufs)
    for nm, pair in zip(["conf_w_pw", "short_w_out", "w_o"], reduce_finish("mix", red_mix, [tok])):
        done.append(update(nm, [pair], deps=[tok]))
    small_g, = _remote_wait("small_grads_d2d_wait", "gather_d2d", send, recv, bufs, 1, done)
    triple = lambda nm, sq: tuple(env[p + nm][0] if sq else env[p + nm] for p in ("", "m_", "v_"))
    small, loss = _small_update(small_g, me_arr, [triple(nm, False) for nm in vec_names],
                                triple("meta", False), triple("conf_dw_w", True), triple("short_dw_w", True))
    for nm, res in zip(vec_names + ["meta"], small[:len(vec_names) + 1]):
        results[nm] = res
    results["conf_dw_w"] = tuple(r[None] for r in small[-2])
    results["short_dw_w"] = tuple(r[None] for r in small[-1])
    update("w_in", [reduce_finish("in%d" % part, red_in[part], [small[0][0]])[0] for part in range(2)])

    order = ["meta", "g_pre_mix", "w_in", "b_gates", "conf_dw_w", "conf_dw_b", "conf_ln_g", "conf_ln_b", "conf_w_pw",
             "short_dw_w", "short_w_out", "w_o", "g_post_mix", "g_pre_mlp", "w_up", "w_down", "g_post_mlp"]
    return (loss, grad_x, *[results[nm][0] for nm in order], *[results[nm][1] for nm in order],
            *[results[nm][2] for nm in order], *[results[nm][3] for nm in order])
```

```python
import jax
import jax.numpy as jnp
from jax import lax
from jax.experimental import pallas as pl
from jax.experimental.pallas import tpu as pltpu

N_DEV = 8
N_META = 16
CONF_K = 31
SHORT_K = 3
RMS_EPS = 1e-6
LN_EPS = 1e-5
ADAM_LR = 0.001
ADAM_B1 = 0.9
ADAM_B2 = 0.999
ADAM_EPS = 1e-08
ADAM_WD = 0.01
ADAM_STEP = 10

LANE = 128
SUB = 8
ROW_TILE = 128
CONV_PAD = 32
CONV_CHUNK = 128
VMEM_LIMIT = 56 * 1024 * 1024

F32 = jnp.float32
BF16 = jnp.bfloat16
MESH = pl.DeviceIdType.MESH
ANY = pl.BlockSpec(memory_space=pl.ANY)
HBM_SPEC = pl.BlockSpec(memory_space=pltpu.HBM)
SEM_SPEC = pl.BlockSpec(memory_space=pltpu.SEMAPHORE)
EFFECT = pltpu.SideEffectType.DATAFLOW_SIDE_EFFECTING


def _params(n_axes):
    return pltpu.CompilerParams(dimension_semantics=("arbitrary",) * n_axes, vmem_limit_bytes=VMEM_LIMIT)


def _sigmoid(z):
    return 1.0 / (1.0 + jnp.exp(-z))


def _colsum8(v):
    r, c = v.shape
    return jnp.sum(v.reshape(r // SUB, SUB, c), axis=0)


def _position():
    x, y, c = lax.axis_index("x"), lax.axis_index("y"), lax.axis_index("c")
    return x, y, c


def _flat(p):
    return 4 * p[0] + 2 * p[1] + p[2]


def _all_gather(name, shards, deps=()):
    n, nd = len(shards), len(deps)

    def body(*refs):
        ins, outs = refs[:n], refs[n + nd:2 * n + nd]
        send_sems, recv_sems, local_sems = refs[2 * n + nd:]
        x, y, c = _position()
        me, sibling = (x, y, c), (x, y, 1 - c)
        chips = [(1 - x, y), (x, 1 - y), (1 - x, 1 - y)]

        def copy(q, k, block, to, src=None):
            dst = outs[q].at[_flat(block)]
            return pltpu.make_async_remote_copy(
                src_ref=dst if src is None else src, dst_ref=dst,
                send_sem=send_sems.at[q, k], recv_sem=recv_sems.at[q, k],
                device_id=to, device_id_type=MESH)

        mine = [pltpu.make_async_copy(ins[q], outs[q].at[_flat(me)], local_sems.at[q]) for q in range(n)]
        for cp in mine:
            cp.start()
        first = []
        for q in range(n):
            first.append(copy(q, 0, me, sibling, src=ins[q]))
            for j, chip in enumerate(chips):
                first.append(copy(q, 1 + j, me, (*chip, c), src=ins[q]))
        for cp in first:
            cp.start()
        passed = []
        for q in range(n):
            for j, chip in enumerate(chips):
                copy(q, 1 + j, (*chip, c), me).wait_recv()
                fwd = copy(q, 4 + j, (*chip, c), sibling)
                fwd.start()
                passed.append(fwd)
        for q in range(n):
            copy(q, 0, sibling, me).wait_recv()
            for j, chip in enumerate(chips):
                copy(q, 4 + j, (*chip, 1 - c), me).wait_recv()
        for cp in first + passed:
            cp.wait_send()
        for cp in mine:
            cp.wait()

    return pl.pallas_call(
        body, name=name,
        in_specs=[ANY] * (n + nd), out_specs=[ANY] * n,
        out_shape=[jax.ShapeDtypeStruct((N_DEV,) + s.shape, s.dtype) for s in shards],
        scratch_shapes=[pltpu.SemaphoreType.DMA((n, 7)), pltpu.SemaphoreType.DMA((n, 7)),
                        pltpu.SemaphoreType.DMA((n,))],
    )(*shards, *deps)


N_COPIES = {"gather_ici": 4, "gather_d2d": 3, "gather_direct": 3, "gather_relay": 3, "gather_diag": 1,
            "reduce_d2d": 4, "reduce_ici": 3}


def _copy_plan(kind):
    x, y, c = _position()
    me, sibling = (x, y, c), (x, y, 1 - c)
    chips = [(1 - x, y), (x, 1 - y), (1 - x, 1 - y)]
    if kind == "gather_ici":
        return [(_flat(me), _flat(me), sibling)] + [(_flat(me), _flat(me), (*ch, c)) for ch in chips]
    if kind == "gather_d2d":
        return [(_flat((*ch, c)), _flat((*ch, c)), sibling) for ch in chips]
    if kind == "gather_direct":
        return [(_flat(me), _flat(me), sibling)] + [(_flat(me), _flat(me), (*ch, c)) for ch in chips[:2]]
    if kind == "gather_relay":
        held, to = (x ^ (1 - c), y ^ c, c), (x ^ c, y ^ (1 - c), c)
        return [(_flat(held), _flat(held), to)] + [(_flat((*ch, c)), _flat((*ch, c)), sibling) for ch in chips[:2]]
    if kind == "gather_diag":
        return [(_flat((*chips[2], c)), _flat((*chips[2], c)), sibling)]
    if kind == "reduce_d2d":
        return [(2 * chip + (1 - c), chip, sibling) for chip in range(4)]
    return [(2 * ch[0] + ch[1], 2 * x + y, (*ch, c)) for ch in chips]


def _planned_copies(kind, srcs, dsts, send_sems, recv_sems):
    plan = _copy_plan(kind)
    return [pltpu.make_async_remote_copy(
        src_ref=src.at[s_slot], dst_ref=dst.at[d_slot],
        send_sem=send_sems.at[q * len(plan) + k], recv_sem=recv_sems.at[q * len(plan) + k],
        device_id=to, device_id_type=MESH)
        for q, (src, dst) in enumerate(zip(srcs, dsts)) for k, (s_slot, d_slot, to) in enumerate(plan)]


def _remote_start(name, kind, srcs, lands=None, deps=()):
    n = len(srcs)
    bufs = list(srcs) + ([] if lands is None else list(lands))
    nb, nd = len(bufs), len(deps)
    nsem = n * N_COPIES[kind]

    def body(*refs):
        ins = refs[:nb]
        send_sems, recv_sems = refs[nb + nd], refs[nb + nd + 1]
        token = refs[-1]
        for cp in _planned_copies(kind, ins[:n], ins[:n] if lands is None else ins[n:], send_sems, recv_sems):
            cp.start()
        token[...] = jnp.zeros_like(token)

    outs = pl.pallas_call(
        body, name=name,
        out_shape=(pltpu.SemaphoreType.DMA((nsem,)), pltpu.SemaphoreType.DMA((nsem,)),
                   *[pltpu.HBM(b.shape, b.dtype) for b in bufs], jax.ShapeDtypeStruct((SUB, LANE), F32)),
        in_specs=[HBM_SPEC] * nb + [ANY] * nd,
        out_specs=(SEM_SPEC, SEM_SPEC, *[HBM_SPEC] * nb, pl.BlockSpec(memory_space=pltpu.VMEM)),
        input_output_aliases={i: 2 + i for i in range(nb)},
        compiler_params=pltpu.CompilerParams(has_side_effects=EFFECT),
    )(*[pltpu.with_memory_space_constraint(b, pltpu.HBM) for b in bufs], *deps)
    return outs[0], outs[1], list(outs[2:2 + nb]), outs[-1]


def _remote_wait(name, kind, send_sems, recv_sems, bufs, n, after):
    nb, na = len(bufs), len(after)
    same = nb == n

    def body(*refs):
        ins = refs[:nb]
        sends, recvs = refs[nb], refs[nb + 1]
        for cp in _planned_copies(kind, ins[:n], ins[:n] if same else ins[n:], sends, recvs):
            cp.wait_send()
            cp.wait_recv()

    outs = pl.pallas_call(
        body, name=name,
        out_shape=[pltpu.HBM(b.shape, b.dtype) for b in bufs],
        in_specs=[HBM_SPEC] * nb + [SEM_SPEC, SEM_SPEC] + [ANY] * na,
        out_specs=[HBM_SPEC] * nb,
        input_output_aliases={i: i for i in range(nb)},
        compiler_params=pltpu.CompilerParams(has_side_effects=EFFECT),
    )(*bufs, send_sems, recv_sems, *after)
    return list(outs)


def _remote_pass_on(name, done, send_sems, recv_sems, bufs, after, nxt):
    nb, na = len(bufs), len(after)
    nsem = nb * N_COPIES[nxt]

    def body(*refs):
        ins = refs[:nb]
        new_sends, new_recvs = refs[nb + 2 + na], refs[nb + 3 + na]
        token = refs[-1]
        for cp in _planned_copies(done, ins, ins, refs[nb], refs[nb + 1]):
            cp.wait_send()
            cp.wait_recv()
        for cp in _planned_copies(nxt, ins, ins, new_sends, new_recvs):
            cp.start()
        token[...] = jnp.zeros_like(token)

    outs = pl.pallas_call(
        body, name=name,
        out_shape=(pltpu.SemaphoreType.DMA((nsem,)), pltpu.SemaphoreType.DMA((nsem,)),
                   *[pltpu.HBM(b.shape, b.dtype) for b in bufs], jax.ShapeDtypeStruct((SUB, LANE), F32)),
        in_specs=[HBM_SPEC] * nb + [SEM_SPEC, SEM_SPEC] + [ANY] * na,
        out_specs=(SEM_SPEC, SEM_SPEC, *[HBM_SPEC] * nb, pl.BlockSpec(memory_space=pltpu.VMEM)),
        input_output_aliases={i: 2 + i for i in range(nb)},
        compiler_params=pltpu.CompilerParams(has_side_effects=EFFECT),
    )(*bufs, send_sems, recv_sems, *after)
    return outs[0], outs[1], list(outs[2:2 + nb]), outs[-1]


def _mm_cols(name, a, w, *, tm, blocks, epilogue, out_dtypes, into=(), deps=()):
    t, k = a.shape
    nblk, _, cb = w.shape
    j0, j1 = blocks
    no = len(out_dtypes)

    def body(a_ref, w_ref, *rest):
        acc = jnp.dot(a_ref[...], w_ref[0], preferred_element_type=F32)
        for o_ref, o in zip(rest[len(into) + len(deps):], epilogue(acc)):
            o_ref[...] = o.astype(o_ref.dtype)

    return pl.pallas_call(
        body, name=name, grid=(j1 - j0, t // tm),
        in_specs=[pl.BlockSpec((tm, k), lambda j, i: (i, 0)),
                  pl.BlockSpec((1, k, cb), lambda j, i: (j0 + j, 0, 0))] + [ANY] * (len(into) + len(deps)),
        out_specs=[pl.BlockSpec((tm, cb), lambda j, i: (i, j0 + j)) for _ in range(no)],
        out_shape=[jax.ShapeDtypeStruct((t, nblk * cb), dt) for dt in out_dtypes],
        input_output_aliases={2 + idx: idx for idx in range(len(into))},
        compiler_params=_params(2),
    )(a, w, *into, *deps)


MXU_WIDTH = 256


def _mm_cols_pairs(name, a, w, *, tm):
    t, k = a.shape
    nblk, _, cb = w.shape
    main = cb // MXU_WIDTH * MXU_WIDTH
    tail = cb - main
    assert 2 * tail == MXU_WIDTH and nblk % 2 == 0

    def body(a_ref, w_ref, o_ref):
        av = a_ref[...]
        for b in range(2):
            o_ref[:, b * cb:b * cb + main] = jnp.dot(av, w_ref[b, :, 0:main], preferred_element_type=F32)
        tails = jnp.dot(av, jnp.concatenate([w_ref[0, :, main:cb], w_ref[1, :, main:cb]], axis=1),
                        preferred_element_type=F32)
        for b in range(2):
            o_ref[:, b * cb + main:(b + 1) * cb] = tails[:, b * tail:(b + 1) * tail]

    return pl.pallas_call(
        body, name=name, grid=(nblk // 2, t // tm),
        in_specs=[pl.BlockSpec((tm, k), lambda j, i: (i, 0)),
                  pl.BlockSpec((2, k, cb), lambda j, i: (j, 0, 0))],
        out_specs=pl.BlockSpec((tm, 2 * cb), lambda j, i: (i, j)),
        out_shape=jax.ShapeDtypeStruct((t, nblk * cb), F32),
        compiler_params=_params(2),
    )(a, w)


def _mm_rows(name, a, w2d, *, tm, tn):
    t, kf = a.shape
    n = w2d.shape[1]

    def body(a_ref, w_ref, o_ref):
        o_ref[...] = jnp.dot(a_ref[...], w_ref[...], preferred_element_type=F32)

    return pl.pallas_call(
        body, name=name, grid=(t // tm, n // tn),
        in_specs=[pl.BlockSpec((tm, kf), lambda i, j: (i, 0)),
                  pl.BlockSpec((kf, tn), lambda i, j: (0, j))],
        out_specs=pl.BlockSpec((tm, tn), lambda i, j: (i, j)),
        out_shape=jax.ShapeDtypeStruct((t, n), F32),
        compiler_params=_params(2),
    )(a, w2d)


def _mm_nt_acc(name, dy, w, *, tm, tn, col_off=0, deps=()):
    t = dy.shape[0]
    nblk, k, cb = w.shape

    main = cb // MXU_WIDTH * MXU_WIDTH

    def body(dy_ref, w_ref, *rest):
        nt = (((1,), (1,)), ((), ()))
        acc = None
        for b in range(nblk):
            d = lax.dot_general(dy_ref[:, b * cb:b * cb + main], w_ref[b, :, 0:main], nt, preferred_element_type=F32)
            acc = d if acc is None else acc + d
        if main < cb:
            dy_tails = jnp.concatenate([dy_ref[:, b * cb + main:(b + 1) * cb] for b in range(nblk)], axis=1)
            w_tails = jnp.concatenate([w_ref[b, :, main:cb] for b in range(nblk)], axis=1)
            acc = acc + lax.dot_general(dy_tails, w_tails, nt, preferred_element_type=F32)
        rest[-1][...] = acc

    return pl.pallas_call(
        body, name=name, grid=(t // tm, k // tn),
        in_specs=[pl.BlockSpec((tm, nblk * cb), lambda i, j: (i, col_off)),
                  pl.BlockSpec((nblk, tn, cb), lambda i, j: (0, j, 0))] + [ANY] * len(deps),
        out_specs=pl.BlockSpec((tm, tn), lambda i, j: (i, j)),
        out_shape=jax.ShapeDtypeStruct((t, k), F32),
        compiler_params=_params(2),
    )(dy, w, *deps)


def _mm_nt_blocks(name, dy, w2d, *, tm, tkb, extra=(), epilogue=None, out_dtypes=(F32,)):
    t, n = dy.shape
    kf = w2d.shape[0]
    ne = len(extra)

    def body(dy_ref, w_ref, *rest):
        acc = lax.dot_general(dy_ref[...], w_ref[...], (((1,), (1,)), ((), ())), preferred_element_type=F32)
        outs = (acc,) if epilogue is None else epilogue(acc, *[e[...] for e in rest[:ne]])
        for o_ref, o in zip(rest[ne:], outs):
            o_ref[...] = o.astype(o_ref.dtype)

    return pl.pallas_call(
        body, name=name, grid=(kf // tkb, t // tm),
        in_specs=[pl.BlockSpec((tm, n), lambda kb, i: (i, 0)),
                  pl.BlockSpec((tkb, n), lambda kb, i: (kb, 0))]
                 + [pl.BlockSpec((tm, tkb), lambda kb, i: (i, kb)) for _ in extra],
        out_specs=[pl.BlockSpec((tm, tkb), lambda kb, i: (i, kb)) for _ in out_dtypes],
        out_shape=[jax.ShapeDtypeStruct((t, kf), dt) for dt in out_dtypes],
        compiler_params=_params(2),
    )(dy, w2d, *extra)


def _mm_tn(name, a, b, me_arr, *, m, n, tma, tn, sharded, a_off=0, b_off=0, deps=()):
    t = a.shape[0]
    if sharded == "cols":
        cb = n // N_DEV
        nb, q = max(tn // cb, 1), max(cb // tn, 1)
        tw = tn // nb
        full_shape, own_shape = (N_DEV, m, cb), (m, cb)
        full_spec = pl.BlockSpec((nb, tma, tw), lambda i, j, me: (j // q, i, j % q))
    else:
        kb = m // N_DEV
        p = kb // tma
        nb, tw = 1, tn
        full_shape, own_shape = (m, n), (kb, n)
        full_spec = pl.BlockSpec((tma, tn), lambda i, j, me: (i, j))

    def body(me_ref, a_ref, b_ref, *rest):
        full_ref, own_ref, stage, sem = rest[len(deps):]
        i, j = pl.program_id(0), pl.program_id(1)
        acc = lax.dot_general(a_ref[...], b_ref[...], (((0,), (0,)), ((), ())), preferred_element_type=F32)
        for blk in range(nb):
            part = acc[:, blk * tw:(blk + 1) * tw]
            if sharded == "cols":
                full_ref[blk] = part.astype(BF16)
                owner, r0, c0 = (j // q) * nb + blk, i * tma, (j % q) * tw
            else:
                full_ref[...] = part.astype(BF16)
                owner, r0, c0 = i // p, (i % p) * tma, j * tn

            @pl.when(owner == me_ref[0])
            def _():
                stage[...] = part
                cp = pltpu.make_async_copy(
                    stage, own_ref.at[pl.ds(pl.multiple_of(r0, tma), tma), pl.ds(pl.multiple_of(c0, tw), tw)], sem)
                cp.start()
                cp.wait()

    full, own = pl.pallas_call(
        body, name=name,
        grid_spec=pltpu.PrefetchScalarGridSpec(
            num_scalar_prefetch=1, grid=(m // tma, n // tn),
            in_specs=[pl.BlockSpec((t, tma), lambda i, j, me: (0, a_off + i)),
                      pl.BlockSpec((t, tn), lambda i, j, me: (0, b_off + j))] + [ANY] * len(deps),
            out_specs=[full_spec, ANY],
            scratch_shapes=[pltpu.VMEM((tma, tw), F32), pltpu.SemaphoreType.DMA(())]),
        out_shape=[jax.ShapeDtypeStruct(full_shape, BF16), jax.ShapeDtypeStruct(own_shape, F32)],
        compiler_params=_params(2),
    )(me_arr, a, b, *deps)
    if sharded == "rows":
        full = full.reshape(N_DEV, m // N_DEV, n)
    return full, own


def _row_tile(t):
    return t // 8 if (t // 8) % 16 == 0 else ROW_TILE


def _row_call(name, body, t, row_ins, full_ins, row_outs, acc_outs, scratch=(), deps=()):
    tm = _row_tile(t)
    nin = len(row_ins) + len(full_ins)

    def without_deps(*refs):
        body(*refs[:nin], *refs[nin + len(deps):])

    return pl.pallas_call(
        without_deps, name=name, grid=(t // tm,),
        in_specs=[pl.BlockSpec((tm, a.shape[1]), lambda i: (i, 0)) for a in row_ins]
                 + [pl.BlockSpec(a.shape, lambda i: (0, 0)) for a in full_ins] + [ANY] * len(deps),
        out_specs=[pl.BlockSpec((tm, c), lambda i: (i, 0)) for c, _ in row_outs]
                  + [pl.BlockSpec((r, c), lambda i: (0, 0)) for r, c in acc_outs],
        out_shape=[jax.ShapeDtypeStruct((t, c), dt) for c, dt in row_outs]
                  + [jax.ShapeDtypeStruct((r, c), F32) for r, c in acc_outs],
        scratch_shapes=list(scratch),
        compiler_params=_params(1),
    )(*row_ins, *full_ins, *deps)


def _accumulate(ref, v):
    @pl.when(pl.program_id(0) == 0)
    def _():
        ref[...] = v

    @pl.when(pl.program_id(0) > 0)
    def _():
        ref[...] += v


def _rms(v):
    return lax.rsqrt(jnp.mean(v * v, axis=-1, keepdims=True) + RMS_EPS)


def _rms_bwd(dout, u, r, g):
    du = dout * g
    dx = r * (du - u * jnp.mean(du * u, axis=-1, keepdims=True))
    return dx, _colsum8(dout * u)


def _pre_norm(h0, g):
    t, d = h0.shape

    def body(h_ref, g_ref, n_ref):
        h = h_ref[...]
        n_ref[...] = (h * _rms(h) * g_ref[...]).astype(BF16)

    return _row_call("pre_norm", body, t, [h0], [g], [(d, BF16)], [])[0]


def _mix_post(m_mix, wo_full, h0, g_post, g_pre, deps=()):
    t, d = h0.shape
    tm = _row_tile(t)

    def body(m_ref, wo_ref, h0_ref, gp_ref, gq_ref, *rest):
        mix_ref, h1_ref, n2_ref = rest[len(deps):]
        mix_v = jnp.dot(m_ref[...], wo_ref[...], preferred_element_type=F32)
        mix_ref[...] = mix_v
        h1 = h0_ref[...] + mix_v * _rms(mix_v) * gp_ref[...]
        h1_ref[...] = h1
        n2_ref[...] = (h1 * _rms(h1) * gq_ref[...]).astype(BF16)

    tile = pl.BlockSpec((tm, d), lambda i: (i, 0))
    gain = pl.BlockSpec((1, d), lambda i: (0, 0))
    return pl.pallas_call(
        body, name="mix_post", grid=(t // tm,),
        in_specs=[tile, pl.BlockSpec((d, d), lambda i: (0, 0)), tile, gain, gain] + [ANY] * len(deps),
        out_specs=[tile, tile, tile],
        out_shape=[jax.ShapeDtypeStruct((t, d), F32), jax.ShapeDtypeStruct((t, d), F32),
                   jax.ShapeDtypeStruct((t, d), BF16)],
        compiler_params=_params(1),
    )(m_mix, wo_full, h0, g_post, g_pre, *deps)


def _loss_head(fo, h1, tgt, g_post_mlp, t_real):
    t, d = h1.shape
    tile = _row_tile(t)

    def body(fo_ref, h1_ref, tgt_ref, g_ref, dfo_ref, dh2_ref, dg_ref, loss_ref, lacc):
        i = pl.program_id(0)
        fo_v = fo_ref[...]
        g = g_ref[...]
        r = _rms(fo_v)
        u = fo_v * r
        h2 = h1_ref[...] + u * g
        row = i * tile + lax.broadcasted_iota(jnp.int32, (tile, 1), 0)
        valid = jnp.logical_and(row >= N_META, row < t_real)
        diff = jnp.where(valid, h2 - tgt_ref[...], 0.0)
        dh2 = diff * (1.0 / d)
        dh2_ref[...] = dh2
        dfo, dg = _rms_bwd(dh2, u, r, g)
        dfo_ref[...] = dfo.astype(BF16)
        _accumulate(dg_ref, dg)
        _accumulate(lacc, _colsum8(diff * diff))

        @pl.when(i == pl.num_programs(0) - 1)
        def _():
            loss_ref[...] = jnp.full((SUB, LANE), (0.5 / d) * jnp.sum(lacc[...]), F32)

    return _row_call("loss_head", body, t, [fo, h1, tgt], [g_post_mlp],
                     [(d, BF16), (d, F32)], [(SUB, d), (SUB, LANE)], scratch=[pltpu.VMEM((SUB, d), F32)])


def _mid_norm_bwd(dn2, h1, dh2, mix, g_pre_mlp, g_post_mix, deps=()):
    t, d = h1.shape

    def body(dn2_ref, h1_ref, dh2_ref, mix_ref, gq_ref, gp_ref, dh1_ref, dmix_ref, dgq_ref, dgp_ref):
        h1 = h1_ref[...]
        r3 = _rms(h1)
        dx, dgq = _rms_bwd(dn2_ref[...], h1 * r3, r3, gq_ref[...])
        dh1 = dh2_ref[...] + dx
        dh1_ref[...] = dh1
        mix_v = mix_ref[...]
        r2 = _rms(mix_v)
        dmix, dgp = _rms_bwd(dh1, mix_v * r2, r2, gp_ref[...])
        dmix_ref[...] = dmix.astype(BF16)
        _accumulate(dgq_ref, dgq)
        _accumulate(dgp_ref, dgp)

    return _row_call("mid_norm_bwd", body, t, [dn2, h1, dh2, mix], [g_pre_mlp, g_post_mix],
                     [(d, F32), (d, BF16)], [(SUB, d), (SUB, d)], deps=deps)


def _pre_norm_bwd(dn, h0, dh1, g_pre_mix, deps=()):
    t, d = h0.shape

    def body(dn_ref, h0_ref, dh1_ref, g_ref, dh0_ref, dg_ref):
        h0 = h0_ref[...]
        r = _rms(h0)
        dx, dg = _rms_bwd(dn_ref[...], h0 * r, r, g_ref[...])
        dh0_ref[...] = dh1_ref[...] + dx
        _accumulate(dg_ref, dg)

    return _row_call("pre_norm_bwd", body, t, [dn, h0, dh1], [g_pre_mix], [(d, F32)], [(SUB, d)], deps=deps)


def _layer_norm_silu(a1, ln_g, ln_b, deps=()):
    t, c = a1.shape

    def body(a1_ref, g_ref, b_ref, a3_ref):
        a = a1_ref[...]
        mu = jnp.mean(a, axis=-1, keepdims=True)
        xc = a - mu
        rstd = lax.rsqrt(jnp.mean(xc * xc, axis=-1, keepdims=True) + LN_EPS)
        z = xc * rstd * g_ref[...] + b_ref[...]
        a3_ref[...] = (z * _sigmoid(z)).astype(BF16)

    return _row_call("layer_norm_silu", body, t, [a1], [ln_g, ln_b], [(c, BF16)], [], deps=deps)[0]


def _layer_norm_silu_bwd(da3, a1, ln_g, ln_b, deps=()):
    t, c = a1.shape

    def body(da3_ref, a1_ref, g_ref, b_ref, da1_ref, dg_ref, db_ref):
        a = a1_ref[...]
        g = g_ref[...]
        mu = jnp.mean(a, axis=-1, keepdims=True)
        xc = a - mu
        rstd = lax.rsqrt(jnp.mean(xc * xc, axis=-1, keepdims=True) + LN_EPS)
        xhat = xc * rstd
        z = xhat * g + b_ref[...]
        sg = _sigmoid(z)
        dz = da3_ref[...] * (sg * (1.0 + z * (1.0 - sg)))
        dxhat = dz * g
        da1_ref[...] = rstd * (dxhat - jnp.mean(dxhat, axis=-1, keepdims=True)
                               - xhat * jnp.mean(dxhat * xhat, axis=-1, keepdims=True))
        _accumulate(dg_ref, _colsum8(dz * xhat))
        _accumulate(db_ref, _colsum8(dz))

    return _row_call("layer_norm_silu_bwd", body, t, [da3, a1], [ln_g, ln_b], [(c, F32)], [(SUB, c), (SUB, c)], deps=deps)


def _branch_merge(a3, s, wpw, wso, proj, b_gates, d, deps=()):
    t, cols = proj.shape
    nblk, k, cb = wpw.shape
    w = 1024
    nh = d // w
    per = w // cb
    ga0 = (cols - 2 * d) // w
    tm = _row_tile(t)

    def body(a3_ref, s_ref, wpw_ref, wso_ref, *rest):
        pa_refs, pb_refs, bg_ref = rest[:nh], rest[nh:2 * nh], rest[2 * nh]
        ya_ref, yb_ref, ga_ref, gb_ref, m_ref = rest[2 * nh + 1 + len(deps):]
        a3v, sv = a3_ref[...], s_ref[...]
        for b in range(nblk):
            here = slice(b * cb, (b + 1) * cb)
            local = slice((b % per) * cb, (b % per + 1) * cb)
            ya = jnp.dot(a3v, wpw_ref[b], preferred_element_type=F32)
            yb = jnp.dot(sv, wso_ref[b], preferred_element_type=F32)
            ga = _sigmoid(pa_refs[b // per][:, local] + bg_ref[:, here])
            gb = _sigmoid(pb_refs[b // per][:, local] + bg_ref[:, d + b * cb:d + (b + 1) * cb])
            ya_ref[:, here] = ya.astype(BF16)
            yb_ref[:, here] = yb.astype(BF16)
            ga_ref[:, here] = ga.astype(BF16)
            gb_ref[:, here] = gb.astype(BF16)
            m_ref[:, here] = (ga * ya + gb * yb).astype(BF16)

    tile = pl.BlockSpec((tm, d), lambda i: (i, 0))
    return pl.pallas_call(
        body, name="branch_merge", grid=(t // tm,),
        in_specs=[pl.BlockSpec((tm, k), lambda i: (i, 0)), pl.BlockSpec((tm, k), lambda i: (i, 0)),
                  pl.BlockSpec((nblk, k, cb), lambda i: (0, 0, 0)), pl.BlockSpec((nblk, k, cb), lambda i: (0, 0, 0))]
                 + [pl.BlockSpec((tm, w), lambda i, h=h: (i, ga0 + h)) for h in range(2 * nh)]
                 + [pl.BlockSpec((1, 2 * d), lambda i: (0, 0))] + [ANY] * len(deps),
        out_specs=[tile] * 5,
        out_shape=[jax.ShapeDtypeStruct((t, d), BF16)] * 5,
        compiler_params=_params(1),
    )(a3, s, wpw, wso, *([proj] * (2 * nh)), b_gates, *deps)


def _gate_backward(dmix, wo_full, ga, gb, ya, yb, cols, tm, deps=()):
    t, d = ya.shape
    w = 1024
    nh = d // w
    ga0 = (cols - 2 * d) // w

    def body(dmix_ref, wo_ref, ga_ref, gb_ref, ya_ref, yb_ref, *rest):
        dya_ref, dyb_ref, dp_ref, dba_ref, dbb_ref, stage, sems = rest[len(deps):]
        h, i = pl.program_id(0), pl.program_id(1)
        dm = lax.dot_general(dmix_ref[...], wo_ref[...], (((1,), (1,)), ((), ())), preferred_element_type=F32)
        ga = ga_ref[...].astype(F32)
        gb = gb_ref[...].astype(F32)
        dya_ref[...] = (dm * ga).astype(BF16)
        dyb_ref[...] = (dm * gb).astype(BF16)
        dpa = dm * ya_ref[...].astype(F32) * ga * (1.0 - ga)
        dpb = dm * yb_ref[...].astype(F32) * gb * (1.0 - gb)
        stage[0] = dpa.astype(BF16)
        stage[1] = dpb.astype(BF16)
        rows = pl.ds(pl.multiple_of(i * tm, tm), tm)
        copies = [pltpu.make_async_copy(
            stage.at[g], dp_ref.at[rows, pl.ds(pl.multiple_of((ga0 + g * nh + h) * w, w), w)], sems.at[g])
            for g in range(2)]
        for cp in copies:
            cp.start()

        @pl.when(i == 0)
        def _():
            dba_ref[...] = _colsum8(dpa)
            dbb_ref[...] = _colsum8(dpb)

        @pl.when(i > 0)
        def _():
            dba_ref[...] += _colsum8(dpa)
            dbb_ref[...] += _colsum8(dpb)

        for cp in copies:
            cp.wait()

    tile = pl.BlockSpec((tm, w), lambda h, i: (i, h))
    return pl.pallas_call(
        body, name="gate_backward", grid=(nh, t // tm),
        in_specs=[pl.BlockSpec((tm, d), lambda h, i: (i, 0)),
                  pl.BlockSpec((w, d), lambda h, i: (h, 0)),
                  tile, tile, tile, tile] + [ANY] * len(deps),
        out_specs=[tile, tile, ANY,
                   pl.BlockSpec((SUB, w), lambda h, i: (0, h)),
                   pl.BlockSpec((SUB, w), lambda h, i: (0, h))],
        out_shape=[jax.ShapeDtypeStruct((t, d), BF16), jax.ShapeDtypeStruct((t, d), BF16),
                   jax.ShapeDtypeStruct((t, cols), BF16),
                   jax.ShapeDtypeStruct((SUB, d), F32), jax.ShapeDtypeStruct((SUB, d), F32)],
        scratch_shapes=[pltpu.VMEM((2, tm, w), BF16), pltpu.SemaphoreType.DMA((2,))],
        compiler_params=_params(2),
    )(dmix, wo_full, ga, gb, ya, yb, *deps)


def _shifted_views(win, offsets):
    n = win.shape[0]
    rotated = {}
    views = {}
    for o in offsets:
        q, r = divmod(o, SUB)
        if r not in rotated:
            rotated[r] = win if r == 0 else pltpu.roll(win, n - r, 0)
        views[o] = rotated[r][q * SUB:q * SUB + CONV_CHUNK]
    return views


def _causal_views(xp_ref, ntap, r0):
    win = xp_ref[pl.ds(r0, CONV_CHUNK + CONV_PAD), :]
    views = _shifted_views(win, [CONV_PAD - (ntap - 1 - k) for k in range(ntap)])
    return [views[CONV_PAD - (ntap - 1 - k)] for k in range(ntap)]


def _causal_conv(xp_ref, w_ref, ntap, r0):
    acc = None
    for k, shifted in enumerate(_causal_views(xp_ref, ntap, r0)):
        term = w_ref[k:k + 1, :] * shifted
        acc = term if acc is None else acc + term
    return acc


def _anticausal_conv(xp_ref, w_ref, ntap, r0):
    win = xp_ref[pl.ds(pl.multiple_of(CONV_PAD + r0, CONV_PAD), CONV_CHUNK + CONV_PAD), :]
    views = _shifted_views(win, [ntap - 1 - k for k in range(ntap)])
    acc = None
    for k in range(ntap):
        term = w_ref[k:k + 1, :] * views[ntap - 1 - k]
        acc = term if acc is None else acc + term
    return acc


def _conv_weight_grad(dw_ref, d_chunk, xp_ref, ntap, r0):
    for k, shifted in enumerate(_causal_views(xp_ref, ntap, r0)):
        dw_ref[k * SUB:(k + 1) * SUB, :] += _colsum8(d_chunk * shifted)


def _zero_pads(ref, t):
    ref[0:CONV_PAD, :] = jnp.zeros((CONV_PAD, LANE), F32)
    ref[CONV_PAD + t:CONV_PAD + t + CONV_PAD, :] = jnp.zeros((CONV_PAD, LANE), F32)


def _for_chunks(t, fn):
    def step(idx, carry):
        fn(pl.multiple_of(idx * CONV_CHUNK, CONV_CHUNK))
        return carry

    lax.fori_loop(0, t // CONV_CHUNK, step, 0)


def _conv_forward(proj, conf_w, conf_b, short_w, dc, deps=()):
    t = proj.shape[0]
    nc = dc // LANE

    def body(av_ref, ag_ref, bg_ref, cg_ref, v_ref, cw_ref, cb_ref, sw_ref, *rest):
        a1_ref, s_ref, xa, xb = rest[len(deps):]
        _zero_pads(xa, t)
        _zero_pads(xb, t)
        xa[CONV_PAD:CONV_PAD + t, :] = av_ref[...] * _sigmoid(ag_ref[...])
        xb[CONV_PAD:CONV_PAD + t, :] = cg_ref[...] * v_ref[...]

        def chunk(r0):
            rs = pl.ds(r0, CONV_CHUNK)
            a1_ref[rs, :] = _causal_conv(xa, cw_ref, CONF_K, r0) + cb_ref[...]
            s_ref[rs, :] = (bg_ref[rs, :] * _causal_conv(xb, sw_ref, SHORT_K, r0)).astype(BF16)

        _for_chunks(t, chunk)

    col = lambda g: pl.BlockSpec((t, LANE), lambda c, g=g: (0, g * nc + c))
    return pl.pallas_call(
        body, name="conv_forward", grid=(nc,),
        in_specs=[col(0), col(1), col(2), col(3), col(4),
                  pl.BlockSpec((CONF_K, LANE), lambda c: (0, c)),
                  pl.BlockSpec((1, LANE), lambda c: (0, c)),
                  pl.BlockSpec((SHORT_K, LANE), lambda c: (0, c))] + [ANY] * len(deps),
        out_specs=[pl.BlockSpec((t, LANE), lambda c: (0, c)), pl.BlockSpec((t, LANE), lambda c: (0, c))],
        out_shape=[jax.ShapeDtypeStruct((t, dc), F32), jax.ShapeDtypeStruct((t, dc), BF16)],
        scratch_shapes=[pltpu.VMEM((t + 2 * CONV_PAD, LANE), F32), pltpu.VMEM((t + 2 * CONV_PAD, LANE), F32)],
        compiler_params=_params(1),
    )(proj, proj, proj, proj, proj, conf_w, conf_b, short_w, *deps)


def _conv_backward(dproj, proj, da1, ds, conf_w, short_w, dc):
    t = proj.shape[0]
    nc = dc // LANE

    def body(dp_in, av_ref, ag_ref, bg_ref, cg_ref, v_ref, da1_ref, ds_ref, cw_ref, sw_ref,
             dp_ref, dcw_ref, dcb_ref, dsw_ref, xa, xb, da, db, stage, sems):
        del dp_in
        c = pl.program_id(0)
        for ref in (xa, xb, da, db):
            _zero_pads(ref, t)
        xa[CONV_PAD:CONV_PAD + t, :] = av_ref[...] * _sigmoid(ag_ref[...])
        xb[CONV_PAD:CONV_PAD + t, :] = cg_ref[...] * v_ref[...]
        da[CONV_PAD:CONV_PAD + t, :] = da1_ref[...]
        dcw_ref[...] = jnp.zeros(dcw_ref.shape, F32)
        dsw_ref[...] = jnp.zeros(dsw_ref.shape, F32)
        dcb_ref[...] = jnp.zeros(dcb_ref.shape, F32)

        def through_gate(r0):
            rs = pl.ds(r0, CONV_CHUNK)
            ds_c = ds_ref[rs, :]
            stage[2, rs, :] = (ds_c * _causal_conv(xb, sw_ref, SHORT_K, r0)).astype(BF16)
            db[pl.ds(pl.multiple_of(CONV_PAD + r0, CONV_PAD), CONV_CHUNK), :] = ds_c * bg_ref[rs, :]

        _for_chunks(t, through_gate)

        def through_convs(r0):
            rs = pl.ds(r0, CONV_CHUNK)
            da0 = _anticausal_conv(da, cw_ref, CONF_K, r0)
            sg = _sigmoid(ag_ref[rs, :])
            stage[0, rs, :] = (da0 * sg).astype(BF16)
            stage[1, rs, :] = (da0 * av_ref[rs, :] * sg * (1.0 - sg)).astype(BF16)
            dcv = _anticausal_conv(db, sw_ref, SHORT_K, r0)
            stage[3, rs, :] = (dcv * v_ref[rs, :]).astype(BF16)
            stage[4, rs, :] = (dcv * cg_ref[rs, :]).astype(BF16)
            da1_c = da1_ref[rs, :]
            _conv_weight_grad(dcw_ref, da1_c, xa, CONF_K, r0)
            _conv_weight_grad(dsw_ref, ds_ref[rs, :] * bg_ref[rs, :], xb, SHORT_K, r0)
            dcb_ref[...] += _colsum8(da1_c)

        _for_chunks(t, through_convs)
        copies = [pltpu.make_async_copy(
            stage.at[g], dp_ref.at[:, pl.ds(pl.multiple_of((g * nc + c) * LANE, LANE), LANE)], sems.at[g])
            for g in range(5)]
        for cp in copies:
            cp.start()
        for cp in copies:
            cp.wait()

    col = lambda g: pl.BlockSpec((t, LANE), lambda c, g=g: (0, g * nc + c))
    blk = pl.BlockSpec((t, LANE), lambda c: (0, c))
    return pl.pallas_call(
        body, name="conv_backward", grid=(nc,),
        in_specs=[ANY, col(0), col(1), col(2), col(3), col(4), blk, blk,
                  pl.BlockSpec((CONF_K, LANE), lambda c: (0, c)),
                  pl.BlockSpec((SHORT_K, LANE), lambda c: (0, c))],
        out_specs=[ANY,
                   pl.BlockSpec((CONF_K * SUB, LANE), lambda c: (0, c)),
                   pl.BlockSpec((SUB, LANE), lambda c: (0, c)),
                   pl.BlockSpec((SHORT_K * SUB, LANE), lambda c: (0, c))],
        out_shape=[jax.ShapeDtypeStruct(dproj.shape, dproj.dtype),
                   jax.ShapeDtypeStruct((CONF_K * SUB, dc), F32),
                   jax.ShapeDtypeStruct((SUB, dc), F32),
                   jax.ShapeDtypeStruct((SHORT_K * SUB, dc), F32)],
        scratch_shapes=[pltpu.VMEM((t + 2 * CONV_PAD, LANE), F32)] * 4
                       + [pltpu.VMEM((5, t, LANE), BF16), pltpu.SemaphoreType.DMA((5,))],
        input_output_aliases={0: 0},
        compiler_params=_params(1),
    )(dproj, proj, proj, proj, proj, proj, da1, ds, conf_w, short_w)


def _adamw_math(w, g, m, v):
    m = ADAM_B1 * m + (1.0 - ADAM_B1) * g
    v = ADAM_B2 * v + (1.0 - ADAM_B2) * (g * g)
    m_hat = m / (1.0 - ADAM_B1 ** ADAM_STEP)
    v_hat = v / (1.0 - ADAM_B2 ** ADAM_STEP)
    delta = -ADAM_LR * (m_hat / (jnp.sqrt(v_hat) + ADAM_EPS) + ADAM_WD * w)
    return delta, m, v


def _cast_into_slot(name, w, me_arr, deps=()):
    r, c = w.shape
    tr = 256

    def body(me_ref, w_ref, *rest):
        del me_ref
        rest[-1][0] = w_ref[...].astype(BF16)

    return pl.pallas_call(
        body, name=name,
        grid_spec=pltpu.PrefetchScalarGridSpec(
            num_scalar_prefetch=1, grid=(r // tr,),
            in_specs=[pl.BlockSpec((tr, c), lambda i, me: (i, 0))] + [ANY] * len(deps),
            out_specs=pl.BlockSpec((1, tr, c), lambda i, me: (me[0], i, 0))),
        out_shape=jax.ShapeDtypeStruct((N_DEV, r, c), BF16),
        compiler_params=_params(1),
    )(me_arr, w, *deps)


def _chip_sum(name, full, from_sibling, me_arr):
    _, r, c = full.shape
    tr = min(r, 1024)

    def body(me_ref, full_ref, sib_ref, sums_ref):
        del me_ref
        sums_ref[0] = (full_ref[0].astype(F32) + sib_ref[0].astype(F32)).astype(BF16)

    other = lambda k, me: (me[0] // 2 + 1 + k) % 4
    return pl.pallas_call(
        body, name=name,
        grid_spec=pltpu.PrefetchScalarGridSpec(
            num_scalar_prefetch=1, grid=(r // tr, 3),
            in_specs=[pl.BlockSpec((1, tr, c), lambda i, k, me: (2 * other(k, me) + me[0] % 2, i, 0)),
                      pl.BlockSpec((1, tr, c), lambda i, k, me: (other(k, me), i, 0))],
            out_specs=pl.BlockSpec((1, tr, c), lambda i, k, me: (other(k, me), i, 0))),
        out_shape=jax.ShapeDtypeStruct((4, r, c), BF16),
        compiler_params=_params(2),
    )(me_arr, full, from_sibling)


def _adamw_shard(name, w, m, v, parts, me_arr, deps=()):
    r, c = w.shape
    tr = min(256, r // len(parts))
    np_ = len(parts)
    per = r // np_ // tr

    def body(me_ref, w_ref, m_ref, v_ref, *rest):
        g_out, d_out, m_out, v_out = rest[5 * np_ + len(deps):]
        g = None
        for p in range(np_):
            gp = rest[5 * p][...]
            for l_ref in rest[5 * p + 1:5 * p + 5]:
                gp = gp + l_ref[0].astype(F32)
            g = gp if g is None else jnp.where(pl.program_id(0) // per == p, gp, g)
        delta, m_new, v_new = _adamw_math(w_ref[...], g, m_ref[...], v_ref[...])
        g_out[...] = g
        d_out[...] = delta
        m_out[...] = m_new
        v_out[...] = v_new

    tile = pl.BlockSpec((tr, c), lambda i, me: (i, 0))
    part_specs, part_args = [], []
    for p, (g_own, from_sibling, landed) in enumerate(parts):
        row = lambda i, p=p: jnp.clip(i - p * per, 0, per - 1)
        part_specs.append(pl.BlockSpec((tr, c), lambda i, me, row=row: (row(i), 0)))
        part_specs += [pl.BlockSpec((1, tr, c), lambda i, me, k=k, row=row: ((me[0] // 2 + k) % 4, row(i), 0))
                       for k in range(4)]
        part_args += [g_own, from_sibling, landed, landed, landed]
    return pl.pallas_call(
        body, name=name,
        grid_spec=pltpu.PrefetchScalarGridSpec(
            num_scalar_prefetch=1, grid=(r // tr,),
            in_specs=[tile] * 3 + part_specs + [ANY] * len(deps), out_specs=[tile] * 4),
        out_shape=[jax.ShapeDtypeStruct((r, c), F32)] * 4,
        compiler_params=_params(1),
    )(me_arr, w, m, v, *part_args, *deps)


SMALL_W = 1024
VEC_ROWS = 16
LOSS_ROW = 15
META_ROW0 = 16
CONF_ROW0 = 64
SHORT_ROW0 = 96
SMALL_ROWS = 104


def _pack_small(vec_parts, dmeta, dcw, dsw, loss_blk, me_arr):
    widths = [p.shape[1] for p in vec_parts]
    nv = len(vec_parts)

    def body(me_ref, *refs):
        del me_ref
        parts, (dmeta_ref, dcw_ref, dsw_ref, loss_ref, out_ref) = refs[:nv], refs[nv:]
        out_ref[0] = jnp.zeros((SMALL_ROWS, SMALL_W), F32)
        out_ref[0, LOSS_ROW:LOSS_ROW + 1, 0:LANE] = loss_ref[0:1, :]
        row = 0
        for p_ref, wd in zip(parts, widths):
            s = jnp.sum(p_ref[...], axis=0, keepdims=True)
            for h in range(wd // SMALL_W):
                out_ref[0, row:row + 1, :] = s[:, h * SMALL_W:(h + 1) * SMALL_W]
                row += 1
        for h in range(dmeta_ref.shape[1] // SMALL_W):
            out_ref[0, META_ROW0 + h * N_META:META_ROW0 + (h + 1) * N_META, :] = dmeta_ref[:, h * SMALL_W:(h + 1) * SMALL_W]
        for k in range(CONF_K):
            out_ref[0, CONF_ROW0 + k:CONF_ROW0 + k + 1, :] = jnp.sum(dcw_ref[k * SUB:(k + 1) * SUB, :], axis=0, keepdims=True)
        for k in range(SHORT_K):
            out_ref[0, SHORT_ROW0 + k:SHORT_ROW0 + k + 1, :] = jnp.sum(dsw_ref[k * SUB:(k + 1) * SUB, :], axis=0, keepdims=True)

    ins = [*vec_parts, dmeta, dcw, dsw, loss_blk]
    return pl.pallas_call(
        body, name="pack_small",
        grid_spec=pltpu.PrefetchScalarGridSpec(
            num_scalar_prefetch=1, grid=(1,),
            in_specs=[pl.BlockSpec(a.shape, lambda i, me: (0, 0)) for a in ins],
            out_specs=pl.BlockSpec((1, SMALL_ROWS, SMALL_W), lambda i, me: (me[0], 0, 0))),
        out_shape=jax.ShapeDtypeStruct((N_DEV, SMALL_ROWS, SMALL_W), F32),
        compiler_params=_params(1),
    )(me_arr, *ins)


def _small_update(gathered, me_arr, vec_params, meta_p, conf_p, short_p):
    widths = [p[0].shape[1] for p in vec_params]
    nv = len(vec_params)
    mcols = meta_p[0].shape[1]
    per_row = SMALL_W // mcols

    def body(me_ref, gv_ref, gm_ref, gc_ref, gs_ref, *rest):
        del me_ref
        ins, outs = rest[:3 * (nv + 3)], rest[3 * (nv + 3):]

        def total(ref, r0, rows):
            s = ref[0, r0:r0 + rows, :]
            for dev in range(1, N_DEV):
                s = s + ref[dev, r0:r0 + rows, :]
            return s

        grads = []
        row = 0
        for wd in widths:
            pieces = [total(gv_ref, row + h, 1) for h in range(wd // SMALL_W)]
            grads.append(pieces[0] if len(pieces) == 1 else jnp.concatenate(pieces, axis=1))
            row += len(pieces)
        grads.append(total(gm_ref, 0, N_META))
        grads.append(total(gc_ref, 0, CONF_K))
        grads.append(total(gs_ref, 0, SHORT_K))
        loss = gv_ref[0, LOSS_ROW:LOSS_ROW + 1, 0:LANE]
        for dev in range(1, N_DEV):
            loss = loss + gv_ref[dev, LOSS_ROW:LOSS_ROW + 1, 0:LANE]
        outs[-1][...] = loss
        for idx, g in enumerate(grads):
            w_ref, m_ref, v_ref = ins[3 * idx:3 * idx + 3]
            delta, m_new, v_new = _adamw_math(w_ref[...], g, m_ref[...], v_ref[...])
            g_out, d_out, m_out, v_out = outs[4 * idx:4 * idx + 4]
            g_out[...] = g
            d_out[...] = delta
            m_out[...] = m_new
            v_out[...] = v_new

    params = list(vec_params) + [meta_p, conf_p, short_p]
    flat = [a for p in params for a in p]
    whole = lambda a: pl.BlockSpec(a.shape, lambda i, me: (0,) * a.ndim)
    outs = pl.pallas_call(
        body, name="small_update",
        grid_spec=pltpu.PrefetchScalarGridSpec(
            num_scalar_prefetch=1, grid=(1,),
            in_specs=[pl.BlockSpec((N_DEV, VEC_ROWS, SMALL_W), lambda i, me: (0, 0, 0)),
                      pl.BlockSpec((N_DEV, N_META, mcols),
                                   lambda i, me: (0, META_ROW0 // N_META + me[0] // per_row, me[0] % per_row)),
                      pl.BlockSpec((N_DEV, 32, LANE), lambda i, me: (0, CONF_ROW0 // 32, me[0])),
                      pl.BlockSpec((N_DEV, SUB, LANE), lambda i, me: (0, SHORT_ROW0 // SUB, me[0]))]
                     + [whole(a) for a in flat],
            out_specs=[whole(p[0]) for p in params for _ in range(4)]
                      + [pl.BlockSpec((1, LANE), lambda i, me: (0, 0))]),
        out_shape=[jax.ShapeDtypeStruct(p[0].shape, F32) for p in params for _ in range(4)]
                  + [jax.ShapeDtypeStruct((1, LANE), F32)],
        compiler_params=_params(1),
    )(me_arr, gathered, gathered, gathered, gathered, *flat)
    return [tuple(outs[4 * i:4 * i + 4]) for i in range(len(params))], outs[-1][0, 0]


def kernel(x, meta, g_pre_mix, w_in, b_gates, conf_dw_w, conf_dw_b, conf_ln_g, conf_ln_b, conf_w_pw, short_dw_w, short_w_out, w_o, g_post_mix, g_pre_mlp, w_up, w_down, g_post_mlp, loss_target, m_meta, m_g_pre_mix, m_w_in, m_b_gates, m_conf_dw_w, m_conf_dw_b, m_conf_ln_g, m_conf_ln_b, m_conf_w_pw, m_short_dw_w, m_short_w_out, m_w_o, m_g_post_mix, m_g_pre_mlp, m_w_up, m_w_down, m_g_post_mlp, v_meta, v_g_pre_mix, v_w_in, v_b_gates, v_conf_dw_w, v_conf_dw_b, v_conf_ln_g, v_conf_ln_b, v_conf_w_pw, v_short_dw_w, v_short_w_out, v_w_o, v_g_post_mix, v_g_pre_mlp, v_w_up, v_w_down, v_g_post_mlp):
    seq, d = x.shape[1], x.shape[2]
    dc = conf_w_pw.shape[1]
    t_real = N_META + seq
    t = -(-t_real // ROW_TILE) * ROW_TILE
    tm = t // 2
    assert tm % 16 == 0 and d % 1024 == 0 and dc % 1024 == 0
    x_idx, y_idx, c_idx = _position()
    me_arr = jnp.reshape(4 * x_idx + 2 * y_idx + c_idx, (1,)).astype(jnp.int32)

    big = [w_in[0], conf_w_pw[0], short_w_out[0], w_o[0], w_up[0], w_down[0]]
    big_names = ["w_in", "conf_w_pw", "short_w_out", "w_o", "w_up", "w_down"]
    groups = [[0], [1, 2, 3], [4], [5]]
    slots, deps = [], []
    for g, idxs in enumerate(groups):
        slots.append([_cast_into_slot("cast_" + big_names[i], big[i], me_arr, deps=deps) for i in idxs])
        if g == 0:
            direct0 = _remote_start("gather0_direct_start", "gather_direct", slots[0])
            deps = [direct0[3]]
    casts = [sl for group in slots[1:] for sl in group]
    meta_g, cw_g, sw_g = _all_gather("gather_small_params", [meta, conf_dw_w[0], short_dw_w[0]], deps=casts)

    def start_direct(g, deps):
        send, recv, bufs, tok = _remote_start("gather%d_direct_start" % g, "gather_direct", slots[g], deps=deps)
        return (send, recv, bufs), tok

    def relay(g, state, after):
        send, recv, bufs, tok = _remote_pass_on("gather%d_relay" % g, "gather_direct", *state, after, "gather_relay")
        return (send, recv, bufs), tok

    def pass_diag(g, state, after):
        send, recv, bufs, tok = _remote_pass_on("gather%d_diag" % g, "gather_relay", *state, after, "gather_diag")
        return (send, recv, bufs), tok

    def gathered(g, state, after):
        send, recv, bufs = state
        return _remote_wait("gather%d_diag_wait" % g, "gather_diag", send, recv, bufs, len(bufs), after)

    unshard =lambda g: jnp.transpose(g, (1, 0, 2)).reshape(g.shape[1], -1)
    meta_full, cw_full, sw_full = unshard(meta_g), unshard(cw_g), unshard(sw_g)

    relay0, tok = relay(0, direct0[:3], [meta_g])
    zrows = jnp.zeros((t - t_real, d), F32) + tok[0, 0] * 0.0
    h0 = jnp.concatenate([meta_full, x[0], zrows], axis=0)
    tgt = jnp.concatenate([jnp.zeros((N_META, d), F32), loss_target[0], zrows], axis=0)
    n = _pre_norm(h0, g_pre_mix)
    direct1, tok = start_direct(1, [tok])
    direct2, tok = start_direct(2, [tok])
    diag0, tok = pass_diag(0, relay0, [tok, n])
    win_g, = gathered(0, diag0, [tok])
    proj = _mm_cols_pairs("proj", n, win_g, tm=tm // 2)
    relay1, tok = relay(1, direct1, [proj])
    a1, s = _conv_forward(proj, cw_full, conf_dw_b, sw_full, dc, deps=[tok])
    relay2, tok = relay(2, direct2, [a1])
    direct3, tok = start_direct(3, [tok])
    diag1, tok = pass_diag(1, relay1, [tok])
    a3 = _layer_norm_silu(a1, conf_ln_g, conf_ln_b, deps=[tok])
    wpw_g, wso_g, wo_g = gathered(1, diag1, [a3])
    wo_full = wo_g.reshape(d, d)
    ya, yb, gate_a, gate_b, m_mix = _branch_merge(a3, s, wpw_g, wso_g, proj, b_gates, d)
    diag2, tok = pass_diag(2, relay2, [m_mix])
    mix, h1, n2 = _mix_post(m_mix, wo_full, h0, g_post_mix, g_pre_mlp, deps=[tok])
    wup_g, = gathered(2, diag2, [n2])

    def up_epilogue(acc):
        r = jnp.maximum(acc, 0.0)
        return r * r, r

    half_up = dict(tm=tm, epilogue=up_epilogue, out_dtypes=(BF16, BF16))
    f, relu_up = _mm_cols("mlp_up0", n2, wup_g, blocks=(0, N_DEV // 2), **half_up)
    relay3, tok = relay(3, direct3, [f])
    f, relu_up = _mm_cols("mlp_up1", n2, wup_g, blocks=(N_DEV // 2, N_DEV), into=(f, relu_up), deps=[tok], **half_up)
    diag3, tok = pass_diag(3, relay3, [f])
    wdn_g, = gathered(3, diag3, [tok])
    wdn_full = wdn_g.reshape(-1, d)
    fo = _mm_rows("mlp_down", f, wdn_full, tm=tm // 2, tn=512)
    dfo, dh2, dg_post_mlp, loss_blk = _loss_head(fo, h1, tgt, g_post_mlp, t_real)

    def reduce_start(tag, fulls, deps):
        lands = [lax.empty((4,) + g.shape[1:], BF16) for g in fulls]
        send, recv, bufs, tok = _remote_start("reduce_%s_d2d_start" % tag, "reduce_d2d", fulls, lands, deps=deps)
        return (send, recv, bufs), tok

    def reduce_middle(tag, state, owns, after):
        send, recv, bufs = state
        k = len(owns)
        bufs = _remote_wait("reduce_%s_d2d_wait" % tag, "reduce_d2d", send, recv, bufs, k, after)
        from_sibling = bufs[k:]
        sums = [_chip_sum("chip_sum_%s%d" % (tag, i), bufs[i], from_sibling[i], me_arr) for i in range(k)]
        lands = [lax.empty(sm.shape, BF16) for sm in sums]
        send, recv, bufs, tok = _remote_start("reduce_%s_ici_start" % tag, "reduce_ici", sums, lands)
        return (send, recv, bufs, list(zip(owns, from_sibling))), tok

    def reduce_finish(tag, state, after):
        send, recv, bufs, local = state
        k = len(local)
        bufs = _remote_wait("reduce_%s_ici_wait" % tag, "reduce_ici", send, recv, bufs, k, after)
        return [(own, sib, landed) for (own, sib), landed in zip(local, bufs[k:])]

    dup = _mm_nt_blocks("d_up", dfo, wdn_full, tm=tm, tkb=1024, extra=(relu_up,),
                        epilogue=lambda acc, r: (acc * (2.0 * r.astype(F32)),), out_dtypes=(BF16,))[0]
    gw_down, gw_down_own = _mm_tn("dw_down", f, dfo, me_arr, m=f.shape[1], n=d, tma=512, tn=d, sharded="rows")
    red_down, tok = reduce_start("down", [gw_down], ())
    dn2 = _mm_nt_acc("d_n2", dup, wup_g, tm=tm // 2, tn=512, deps=[tok])
    gw_up, gw_up_own = _mm_tn("dw_up", n2, dup, me_arr, m=d, n=dup.shape[1], tma=512, tn=2048, sharded="cols")
    red_down, tok = reduce_middle("down", red_down, [gw_down_own], [dn2])
    red_up, tok = reduce_start("up", [gw_up], [tok])
    dh1, dmix, dg_pre_mlp, dg_post_mix = _mid_norm_bwd(dn2, h1, dh2, mix, g_pre_mlp, g_post_mix, deps=[tok])
    dya, dyb, dproj, db_a, db_b = _gate_backward(dmix, wo_full, gate_a, gate_b, ya, yb, proj.shape[1], tm // 2)
    db_gates = jnp.concatenate([db_a, db_b], axis=1)
    red_up, tok = reduce_middle("up", red_up, [gw_up_own], [dya])
    gw_o, gw_o_own = _mm_tn("dw_o", m_mix, dmix, me_arr, m=d, n=d, tma=d // N_DEV, tn=d, sharded="rows", deps=[tok])
    da3 = _mm_nt_acc("d_a3", dya, wpw_g, tm=tm, tn=512)
    gw_pw, gw_pw_own = _mm_tn("dw_pw", a3, dya, me_arr, m=dc, n=d, tma=512, tn=d, sharded="cols")
    dsb = _mm_nt_acc("d_s", dyb, wso_g, tm=tm, tn=512)
    gw_so, gw_so_own = _mm_tn("dw_so", s, dyb, me_arr, m=dc, n=d, tma=512, tn=d, sharded="cols")
    red_mix, tok = reduce_start("mix", [gw_pw, gw_so, gw_o], ())
    da1, dln_g, dln_b = _layer_norm_silu_bwd(da3, a1, conf_ln_g, conf_ln_b, deps=[tok])
    dproj, dcw, dcb, dsw = _conv_backward(dproj, proj, da1, dsb, cw_full, sw_full, dc)
    red_mix, tok = reduce_middle("mix", red_mix, [gw_pw_own, gw_so_own, gw_o_own], [dcb])
    in_cb = w_in.shape[2]
    half = d // 2
    red_in = []
    for part in range(2):
        gw, own = _mm_tn("dw_in%d" % part, n, dproj, me_arr, m=half, n=proj.shape[1], tma=512, tn=2 * in_cb,
                         sharded="cols", a_off=part * (half // 512), deps=[tok])
        state, tok = reduce_start("in%d" % part, [gw], ())
        red_in.append((state, own))
    for part in range(2):
        state, own = red_in[part]
        red_in[part], tok = reduce_middle("in%d" % part, state, [own], [tok])
    dn = _mm_nt_acc("d_n", dproj, win_g, tm=tm // 2, tn=512, deps=[tok])
    dh0, dg_pre_mix = _pre_norm_bwd(dn, h0, dh1, g_pre_mix)
    grad_x = dh0[N_META:t_real][None]

    vec_parts = [dg_pre_mix, db_gates, dcb, dln_g, dln_b, dg_post_mix, dg_pre_mlp, dg_post_mlp]
    packed = _pack_small(vec_parts, dh0[:N_META], dcw, dsw, loss_blk, me_arr)
    send, recv, bufs, tok = _remote_start("small_grads_ici_start", "gather_ici", [packed])
    vec_names = ["g_pre_mix", "b_gates", "conf_dw_b", "conf_ln_g", "conf_ln_b", "g_post_mix", "g_pre_mlp", "g_post_mlp"]
    env = locals()
    results = {}

    def update(nm, parts, deps=()):
        res = _adamw_shard("adamw_" + nm, env[nm][0], env["m_" + nm][0], env["v_" + nm][0], parts, me_arr, deps=deps)
        results[nm] = tuple(r[None] for r in res)
        return res[0]

    done = [update("w_down", reduce_finish("down", red_down, [tok]), deps=[tok])]
    done.append(update("w_up", reduce_finish("up", red_up, done)))
    bufs = _remote_wait("small_grads_ici_wait", "gather_ici", send, recv, bufs, 1, done)
    send, recv, bufs, tok = _remote_start("small_grads_d2d_start", "gather_d2d", bufs)
    for nm, pair in zip(["conf_w_pw", "short_w_out", "w_o"], reduce_finish("mix", red_mix, [tok])):
        done.append(update(nm, [pair], deps=[tok]))
    small_g, = _remote_wait("small_grads_d2d_wait", "gather_d2d", send, recv, bufs, 1, done)
    triple = lambda nm, sq: tuple(env[p + nm][0] if sq else env[p + nm] for p in ("", "m_", "v_"))
    small, loss = _small_update(small_g, me_arr, [triple(nm, False) for nm in vec_names],
                                triple("meta", False), triple("conf_dw_w", True), triple("short_dw_w", True))
    for nm, res in zip(vec_names + ["meta"], small[:len(vec_names) + 1]):
        results[nm] = res
    results["conf_dw_w"] = tuple(r[None] for r in small[-2])
    results["short_dw_w"] = tuple(r[None] for r in small[-1])
    update("w_in", [reduce_finish("in%d" % part, red_in[part], [small[0][0]])[0] for part in range(2)])

    order = ["meta", "g_pre_mix", "w_in", "b_gates", "conf_dw_w", "conf_dw_b", "conf_ln_g", "conf_ln_b", "conf_w_pw",
             "short_dw_w", "short_w_out", "w_o", "g_post_mix", "g_pre_mlp", "w_up", "w_down", "g_post_mlp"]
    return (loss, grad_x, *[results[nm][0] for nm in order], *[results[nm][1] for nm in order],
            *[results[nm][2] for nm in order], *[results[nm][3] for nm in order])
```

```python
import jax
import jax.numpy as jnp
from jax import lax
from jax.experimental import pallas as pl
from jax.experimental.pallas import tpu as pltpu
from jax.experimental.pallas import tpu_sc as plsc

N_DEV = 8
N_META = 16
CONF_K = 31
SHORT_K = 3
RMS_EPS = 1e-6
LN_EPS = 1e-5
ADAM_LR = 0.001
ADAM_B1 = 0.9
ADAM_B2 = 0.999
ADAM_EPS = 1e-08
ADAM_WD = 0.01
ADAM_STEP = 10

LANE = 128
SUB = 8
ROW_TILE = 128
CONV_PAD = 32
CONV_CHUNK = 128
VMEM_LIMIT = 56 * 1024 * 1024

F32 = jnp.float32
BF16 = jnp.bfloat16
MESH = pl.DeviceIdType.MESH
ANY = pl.BlockSpec(memory_space=pl.ANY)
HBM_SPEC = pl.BlockSpec(memory_space=pltpu.HBM)
SEM_SPEC = pl.BlockSpec(memory_space=pltpu.SEMAPHORE)
EFFECT = pltpu.SideEffectType.DATAFLOW_SIDE_EFFECTING


def _params(n_axes):
    return pltpu.CompilerParams(dimension_semantics=("arbitrary",) * n_axes, vmem_limit_bytes=VMEM_LIMIT)


def _sigmoid(z):
    return 1.0 / (1.0 + jnp.exp(-z))


def _colsum8(v):
    r, c = v.shape
    return jnp.sum(v.reshape(r // SUB, SUB, c), axis=0)


def _position():
    x, y, c = lax.axis_index("x"), lax.axis_index("y"), lax.axis_index("c")
    return x, y, c


def _flat(p):
    return 4 * p[0] + 2 * p[1] + p[2]


def _all_gather(name, shards, deps=()):
    n, nd = len(shards), len(deps)

    def body(*refs):
        ins, outs = refs[:n], refs[n + nd:2 * n + nd]
        send_sems, recv_sems, local_sems = refs[2 * n + nd:]
        x, y, c = _position()
        me, sibling = (x, y, c), (x, y, 1 - c)
        chips = [(1 - x, y), (x, 1 - y), (1 - x, 1 - y)]

        def copy(q, k, block, to, src=None):
            dst = outs[q].at[_flat(block)]
            return pltpu.make_async_remote_copy(
                src_ref=dst if src is None else src, dst_ref=dst,
                send_sem=send_sems.at[q, k], recv_sem=recv_sems.at[q, k],
                device_id=to, device_id_type=MESH)

        mine = [pltpu.make_async_copy(ins[q], outs[q].at[_flat(me)], local_sems.at[q]) for q in range(n)]
        for cp in mine:
            cp.start()
        first = []
        for q in range(n):
            first.append(copy(q, 0, me, sibling, src=ins[q]))
            for j, chip in enumerate(chips):
                first.append(copy(q, 1 + j, me, (*chip, c), src=ins[q]))
        for cp in first:
            cp.start()
        passed = []
        for q in range(n):
            for j, chip in enumerate(chips):
                copy(q, 1 + j, (*chip, c), me).wait_recv()
                fwd = copy(q, 4 + j, (*chip, c), sibling)
                fwd.start()
                passed.append(fwd)
        for q in range(n):
            copy(q, 0, sibling, me).wait_recv()
            for j, chip in enumerate(chips):
                copy(q, 4 + j, (*chip, 1 - c), me).wait_recv()
        for cp in first + passed:
            cp.wait_send()
        for cp in mine:
            cp.wait()

    return pl.pallas_call(
        body, name=name,
        in_specs=[ANY] * (n + nd), out_specs=[ANY] * n,
        out_shape=[jax.ShapeDtypeStruct((N_DEV,) + s.shape, s.dtype) for s in shards],
        scratch_shapes=[pltpu.SemaphoreType.DMA((n, 7)), pltpu.SemaphoreType.DMA((n, 7)),
                        pltpu.SemaphoreType.DMA((n,))],
    )(*shards, *deps)


N_COPIES = {"gather_ici": 4, "gather_d2d": 3, "gather_direct": 3, "gather_relay": 3, "gather_diag": 1,
            "reduce_d2d": 4, "reduce_ici": 3}


def _copy_plan(kind):
    x, y, c = _position()
    me, sibling = (x, y, c), (x, y, 1 - c)
    chips = [(1 - x, y), (x, 1 - y), (1 - x, 1 - y)]
    if kind == "gather_ici":
        return [(_flat(me), _flat(me), sibling)] + [(_flat(me), _flat(me), (*ch, c)) for ch in chips]
    if kind == "gather_d2d":
        return [(_flat((*ch, c)), _flat((*ch, c)), sibling) for ch in chips]
    if kind == "gather_direct":
        return [(_flat(me), _flat(me), sibling)] + [(_flat(me), _flat(me), (*ch, c)) for ch in chips[:2]]
    if kind == "gather_relay":
        held, to = (x ^ (1 - c), y ^ c, c), (x ^ c, y ^ (1 - c), c)
        return [(_flat(held), _flat(held), to)] + [(_flat((*ch, c)), _flat((*ch, c)), sibling) for ch in chips[:2]]
    if kind == "gather_diag":
        return [(_flat((*chips[2], c)), _flat((*chips[2], c)), sibling)]
    if kind == "reduce_d2d":
        return [(2 * chip + (1 - c), chip, sibling) for chip in range(4)]
    return [(2 * ch[0] + ch[1], 2 * x + y, (*ch, c)) for ch in chips]


def _planned_copies(kind, srcs, dsts, send_sems, recv_sems):
    plan = _copy_plan(kind)
    return [pltpu.make_async_remote_copy(
        src_ref=src.at[s_slot], dst_ref=dst.at[d_slot],
        send_sem=send_sems.at[q * len(plan) + k], recv_sem=recv_sems.at[q * len(plan) + k],
        device_id=to, device_id_type=MESH)
        for q, (src, dst) in enumerate(zip(srcs, dsts)) for k, (s_slot, d_slot, to) in enumerate(plan)]


def _remote_start(name, kind, srcs, lands=None, deps=()):
    n = len(srcs)
    bufs = list(srcs) + ([] if lands is None else list(lands))
    nb, nd = len(bufs), len(deps)
    nsem = n * N_COPIES[kind]

    def body(*refs):
        ins = refs[:nb]
        send_sems, recv_sems = refs[nb + nd], refs[nb + nd + 1]
        token = refs[-1]
        for cp in _planned_copies(kind, ins[:n], ins[:n] if lands is None else ins[n:], send_sems, recv_sems):
            cp.start()
        token[...] = jnp.zeros_like(token)

    outs = pl.pallas_call(
        body, name=name,
        out_shape=(pltpu.SemaphoreType.DMA((nsem,)), pltpu.SemaphoreType.DMA((nsem,)),
                   *[pltpu.HBM(b.shape, b.dtype) for b in bufs], jax.ShapeDtypeStruct((SUB, LANE), F32)),
        in_specs=[HBM_SPEC] * nb + [ANY] * nd,
        out_specs=(SEM_SPEC, SEM_SPEC, *[HBM_SPEC] * nb, pl.BlockSpec(memory_space=pltpu.VMEM)),
        input_output_aliases={i: 2 + i for i in range(nb)},
        compiler_params=pltpu.CompilerParams(has_side_effects=EFFECT),
    )(*[pltpu.with_memory_space_constraint(b, pltpu.HBM) for b in bufs], *deps)
    return outs[0], outs[1], list(outs[2:2 + nb]), outs[-1]


def _remote_wait(name, kind, send_sems, recv_sems, bufs, n, after):
    nb, na = len(bufs), len(after)
    same = nb == n

    def body(*refs):
        ins = refs[:nb]
        sends, recvs = refs[nb], refs[nb + 1]
        for cp in _planned_copies(kind, ins[:n], ins[:n] if same else ins[n:], sends, recvs):
            cp.wait_send()
            cp.wait_recv()

    outs = pl.pallas_call(
        body, name=name,
        out_shape=[pltpu.HBM(b.shape, b.dtype) for b in bufs],
        in_specs=[HBM_SPEC] * nb + [SEM_SPEC, SEM_SPEC] + [ANY] * na,
        out_specs=[HBM_SPEC] * nb,
        input_output_aliases={i: i for i in range(nb)},
        compiler_params=pltpu.CompilerParams(has_side_effects=EFFECT),
    )(*bufs, send_sems, recv_sems, *after)
    return list(outs)


def _remote_pass_on(name, done, send_sems, recv_sems, bufs, after, nxt):
    nb, na = len(bufs), len(after)
    nsem = nb * N_COPIES[nxt]

    def body(*refs):
        ins = refs[:nb]
        new_sends, new_recvs = refs[nb + 2 + na], refs[nb + 3 + na]
        token = refs[-1]
        for cp in _planned_copies(done, ins, ins, refs[nb], refs[nb + 1]):
            cp.wait_send()
            cp.wait_recv()
        for cp in _planned_copies(nxt, ins, ins, new_sends, new_recvs):
            cp.start()
        token[...] = jnp.zeros_like(token)

    outs = pl.pallas_call(
        body, name=name,
        out_shape=(pltpu.SemaphoreType.DMA((nsem,)), pltpu.SemaphoreType.DMA((nsem,)),
                   *[pltpu.HBM(b.shape, b.dtype) for b in bufs], jax.ShapeDtypeStruct((SUB, LANE), F32)),
        in_specs=[HBM_SPEC] * nb + [SEM_SPEC, SEM_SPEC] + [ANY] * na,
        out_specs=(SEM_SPEC, SEM_SPEC, *[HBM_SPEC] * nb, pl.BlockSpec(memory_space=pltpu.VMEM)),
        input_output_aliases={i: 2 + i for i in range(nb)},
        compiler_params=pltpu.CompilerParams(has_side_effects=EFFECT),
    )(*bufs, send_sems, recv_sems, *after)
    return outs[0], outs[1], list(outs[2:2 + nb]), outs[-1]


def _mm_cols(name, a, w, *, tm, blocks, epilogue, out_dtypes, into=(), deps=()):
    t, k = a.shape
    nblk, _, cb = w.shape
    j0, j1 = blocks
    no = len(out_dtypes)

    def body(a_ref, w_ref, *rest):
        acc = jnp.dot(a_ref[...], w_ref[0], preferred_element_type=F32)
        for o_ref, o in zip(rest[len(into) + len(deps):], epilogue(acc)):
            o_ref[...] = o.astype(o_ref.dtype)

    return pl.pallas_call(
        body, name=name, grid=(j1 - j0, t // tm),
        in_specs=[pl.BlockSpec((tm, k), lambda j, i: (i, 0)),
                  pl.BlockSpec((1, k, cb), lambda j, i: (j0 + j, 0, 0))] + [ANY] * (len(into) + len(deps)),
        out_specs=[pl.BlockSpec((tm, cb), lambda j, i: (i, j0 + j)) for _ in range(no)],
        out_shape=[jax.ShapeDtypeStruct((t, nblk * cb), dt) for dt in out_dtypes],
        input_output_aliases={2 + idx: idx for idx in range(len(into))},
        compiler_params=_params(2),
    )(a, w, *into, *deps)


MXU_WIDTH = 256


def _mm_cols_pairs(name, a, w, *, tm):
    t, k = a.shape
    nblk, _, cb = w.shape
    main = cb // MXU_WIDTH * MXU_WIDTH
    tail = cb - main
    assert 2 * tail == MXU_WIDTH and nblk % 2 == 0

    def body(a_ref, w_ref, o_ref):
        av = a_ref[...]
        for b in range(2):
            o_ref[:, b * cb:b * cb + main] = jnp.dot(av, w_ref[b, :, 0:main], preferred_element_type=F32)
        tails = jnp.dot(av, jnp.concatenate([w_ref[0, :, main:cb], w_ref[1, :, main:cb]], axis=1),
                        preferred_element_type=F32)
        for b in range(2):
            o_ref[:, b * cb + main:(b + 1) * cb] = tails[:, b * tail:(b + 1) * tail]

    return pl.pallas_call(
        body, name=name, grid=(nblk // 2, t // tm),
        in_specs=[pl.BlockSpec((tm, k), lambda j, i: (i, 0)),
                  pl.BlockSpec((2, k, cb), lambda j, i: (j, 0, 0))],
        out_specs=pl.BlockSpec((tm, 2 * cb), lambda j, i: (i, j)),
        out_shape=jax.ShapeDtypeStruct((t, nblk * cb), F32),
        compiler_params=_params(2),
    )(a, w)


def _mm_rows(name, a, w2d, *, tm, tn):
    t, kf = a.shape
    n = w2d.shape[1]

    def body(a_ref, w_ref, o_ref):
        o_ref[...] = jnp.dot(a_ref[...], w_ref[...], preferred_element_type=F32)

    return pl.pallas_call(
        body, name=name, grid=(t // tm, n // tn),
        in_specs=[pl.BlockSpec((tm, kf), lambda i, j: (i, 0)),
                  pl.BlockSpec((kf, tn), lambda i, j: (0, j))],
        out_specs=pl.BlockSpec((tm, tn), lambda i, j: (i, j)),
        out_shape=jax.ShapeDtypeStruct((t, n), F32),
        compiler_params=_params(2),
    )(a, w2d)


def _mm_nt_acc(name, dy, w, *, tm, tn, col_off=0, deps=()):
    t = dy.shape[0]
    nblk, k, cb = w.shape

    main = cb // MXU_WIDTH * MXU_WIDTH

    def body(dy_ref, w_ref, *rest):
        nt = (((1,), (1,)), ((), ()))
        acc = None
        for b in range(nblk):
            d = lax.dot_general(dy_ref[:, b * cb:b * cb + main], w_ref[b, :, 0:main], nt, preferred_element_type=F32)
            acc = d if acc is None else acc + d
        if main < cb:
            dy_tails = jnp.concatenate([dy_ref[:, b * cb + main:(b + 1) * cb] for b in range(nblk)], axis=1)
            w_tails = jnp.concatenate([w_ref[b, :, main:cb] for b in range(nblk)], axis=1)
            acc = acc + lax.dot_general(dy_tails, w_tails, nt, preferred_element_type=F32)
        rest[-1][...] = acc

    return pl.pallas_call(
        body, name=name, grid=(t // tm, k // tn),
        in_specs=[pl.BlockSpec((tm, nblk * cb), lambda i, j: (i, col_off)),
                  pl.BlockSpec((nblk, tn, cb), lambda i, j: (0, j, 0))] + [ANY] * len(deps),
        out_specs=pl.BlockSpec((tm, tn), lambda i, j: (i, j)),
        out_shape=jax.ShapeDtypeStruct((t, k), F32),
        compiler_params=_params(2),
    )(dy, w, *deps)


def _mm_nt_blocks(name, dy, w2d, *, tm, tkb, extra=(), epilogue=None, out_dtypes=(F32,)):
    t, n = dy.shape
    kf = w2d.shape[0]
    ne = len(extra)

    def body(dy_ref, w_ref, *rest):
        acc = lax.dot_general(dy_ref[...], w_ref[...], (((1,), (1,)), ((), ())), preferred_element_type=F32)
        outs = (acc,) if epilogue is None else epilogue(acc, *[e[...] for e in rest[:ne]])
        for o_ref, o in zip(rest[ne:], outs):
            o_ref[...] = o.astype(o_ref.dtype)

    return pl.pallas_call(
        body, name=name, grid=(kf // tkb, t // tm),
        in_specs=[pl.BlockSpec((tm, n), lambda kb, i: (i, 0)),
                  pl.BlockSpec((tkb, n), lambda kb, i: (kb, 0))]
                 + [pl.BlockSpec((tm, tkb), lambda kb, i: (i, kb)) for _ in extra],
        out_specs=[pl.BlockSpec((tm, tkb), lambda kb, i: (i, kb)) for _ in out_dtypes],
        out_shape=[jax.ShapeDtypeStruct((t, kf), dt) for dt in out_dtypes],
        compiler_params=_params(2),
    )(dy, w2d, *extra)


def _mm_tn(name, a, b, me_arr, *, m, n, tma, tn, sharded, a_off=0, b_off=0, deps=()):
    t = a.shape[0]
    if sharded == "cols":
        cb = n // N_DEV
        nb, q = max(tn // cb, 1), max(cb // tn, 1)
        tw = tn // nb
        full_shape, own_shape = (N_DEV, m, cb), (m, cb)
        full_spec = pl.BlockSpec((nb, tma, tw), lambda i, j, me: (j // q, i, j % q))
    else:
        kb = m // N_DEV
        p = kb // tma
        nb, tw = 1, tn
        full_shape, own_shape = (m, n), (kb, n)
        full_spec = pl.BlockSpec((tma, tn), lambda i, j, me: (i, j))

    def body(me_ref, a_ref, b_ref, *rest):
        full_ref, own_ref, stage, sem = rest[len(deps):]
        i, j = pl.program_id(0), pl.program_id(1)
        acc = lax.dot_general(a_ref[...], b_ref[...], (((0,), (0,)), ((), ())), preferred_element_type=F32)
        for blk in range(nb):
            part = acc[:, blk * tw:(blk + 1) * tw]
            if sharded == "cols":
                full_ref[blk] = part.astype(BF16)
                owner, r0, c0 = (j // q) * nb + blk, i * tma, (j % q) * tw
            else:
                full_ref[...] = part.astype(BF16)
                owner, r0, c0 = i // p, (i % p) * tma, j * tn

            @pl.when(owner == me_ref[0])
            def _():
                stage[...] = part
                cp = pltpu.make_async_copy(
                    stage, own_ref.at[pl.ds(pl.multiple_of(r0, tma), tma), pl.ds(pl.multiple_of(c0, tw), tw)], sem)
                cp.start()
                cp.wait()

    full, own = pl.pallas_call(
        body, name=name,
        grid_spec=pltpu.PrefetchScalarGridSpec(
            num_scalar_prefetch=1, grid=(m // tma, n // tn),
            in_specs=[pl.BlockSpec((t, tma), lambda i, j, me: (0, a_off + i)),
                      pl.BlockSpec((t, tn), lambda i, j, me: (0, b_off + j))] + [ANY] * len(deps),
            out_specs=[full_spec, ANY],
            scratch_shapes=[pltpu.VMEM((tma, tw), F32), pltpu.SemaphoreType.DMA(())]),
        out_shape=[jax.ShapeDtypeStruct(full_shape, BF16), jax.ShapeDtypeStruct(own_shape, F32)],
        compiler_params=_params(2),
    )(me_arr, a, b, *deps)
    if sharded == "rows":
        full = full.reshape(N_DEV, m // N_DEV, n)
    return full, own


def _row_tile(t):
    return t // 8 if (t // 8) % 16 == 0 else ROW_TILE


def _row_call(name, body, t, row_ins, full_ins, row_outs, acc_outs, scratch=(), deps=()):
    tm = _row_tile(t)
    nin = len(row_ins) + len(full_ins)

    def without_deps(*refs):
        body(*refs[:nin], *refs[nin + len(deps):])

    return pl.pallas_call(
        without_deps, name=name, grid=(t // tm,),
        in_specs=[pl.BlockSpec((tm, a.shape[1]), lambda i: (i, 0)) for a in row_ins]
                 + [pl.BlockSpec(a.shape, lambda i: (0, 0)) for a in full_ins] + [ANY] * len(deps),
        out_specs=[pl.BlockSpec((tm, c), lambda i: (i, 0)) for c, _ in row_outs]
                  + [pl.BlockSpec((r, c), lambda i: (0, 0)) for r, c in acc_outs],
        out_shape=[jax.ShapeDtypeStruct((t, c), dt) for c, dt in row_outs]
                  + [jax.ShapeDtypeStruct((r, c), F32) for r, c in acc_outs],
        scratch_shapes=list(scratch),
        compiler_params=_params(1),
    )(*row_ins, *full_ins, *deps)


def _accumulate(ref, v):
    @pl.when(pl.program_id(0) == 0)
    def _():
        ref[...] = v

    @pl.when(pl.program_id(0) > 0)
    def _():
        ref[...] += v


def _rms(v):
    return lax.rsqrt(jnp.mean(v * v, axis=-1, keepdims=True) + RMS_EPS)


def _rms_bwd(dout, u, r, g):
    du = dout * g
    dx = r * (du - u * jnp.mean(du * u, axis=-1, keepdims=True))
    return dx, _colsum8(dout * u)


def _pre_norm(h0, g):
    t, d = h0.shape

    def body(h_ref, g_ref, n_ref):
        h = h_ref[...]
        n_ref[...] = (h * _rms(h) * g_ref[...]).astype(BF16)

    return _row_call("pre_norm", body, t, [h0], [g], [(d, BF16)], [])[0]


def _mix_post(m_mix, wo_full, h0, g_post, g_pre, deps=()):
    t, d = h0.shape
    tm = _row_tile(t)

    def body(m_ref, wo_ref, h0_ref, gp_ref, gq_ref, *rest):
        mix_ref, h1_ref, n2_ref = rest[len(deps):]
        mix_v = jnp.dot(m_ref[...], wo_ref[...], preferred_element_type=F32)
        mix_ref[...] = mix_v
        h1 = h0_ref[...] + mix_v * _rms(mix_v) * gp_ref[...]
        h1_ref[...] = h1
        n2_ref[...] = (h1 * _rms(h1) * gq_ref[...]).astype(BF16)

    tile = pl.BlockSpec((tm, d), lambda i: (i, 0))
    gain = pl.BlockSpec((1, d), lambda i: (0, 0))
    return pl.pallas_call(
        body, name="mix_post", grid=(t // tm,),
        in_specs=[tile, pl.BlockSpec((d, d), lambda i: (0, 0)), tile, gain, gain] + [ANY] * len(deps),
        out_specs=[tile, tile, tile],
        out_shape=[jax.ShapeDtypeStruct((t, d), F32), jax.ShapeDtypeStruct((t, d), F32),
                   jax.ShapeDtypeStruct((t, d), BF16)],
        compiler_params=_params(1),
    )(m_mix, wo_full, h0, g_post, g_pre, *deps)


def _loss_head(fo, h1, tgt, g_post_mlp, t_real):
    t, d = h1.shape
    tile = _row_tile(t)

    def body(fo_ref, h1_ref, tgt_ref, g_ref, dfo_ref, dh2_ref, dg_ref, loss_ref, lacc):
        i = pl.program_id(0)
        fo_v = fo_ref[...]
        g = g_ref[...]
        r = _rms(fo_v)
        u = fo_v * r
        h2 = h1_ref[...] + u * g
        row = i * tile + lax.broadcasted_iota(jnp.int32, (tile, 1), 0)
        valid = jnp.logical_and(row >= N_META, row < t_real)
        diff = jnp.where(valid, h2 - tgt_ref[...], 0.0)
        dh2 = diff * (1.0 / d)
        dh2_ref[...] = dh2
        dfo, dg = _rms_bwd(dh2, u, r, g)
        dfo_ref[...] = dfo.astype(BF16)
        _accumulate(dg_ref, dg)
        _accumulate(lacc, _colsum8(diff * diff))

        @pl.when(i == pl.num_programs(0) - 1)
        def _():
            loss_ref[...] = jnp.full((SUB, LANE), (0.5 / d) * jnp.sum(lacc[...]), F32)

    return _row_call("loss_head", body, t, [fo, h1, tgt], [g_post_mlp],
                     [(d, BF16), (d, F32)], [(SUB, d), (SUB, LANE)], scratch=[pltpu.VMEM((SUB, d), F32)])


def _mid_norm_bwd(dn2, h1, dh2, mix, g_pre_mlp, g_post_mix, deps=()):
    t, d = h1.shape

    def body(dn2_ref, h1_ref, dh2_ref, mix_ref, gq_ref, gp_ref, dh1_ref, dmix_ref, dgq_ref, dgp_ref):
        h1 = h1_ref[...]
        r3 = _rms(h1)
        dx, dgq = _rms_bwd(dn2_ref[...], h1 * r3, r3, gq_ref[...])
        dh1 = dh2_ref[...] + dx
        dh1_ref[...] = dh1
        mix_v = mix_ref[...]
        r2 = _rms(mix_v)
        dmix, dgp = _rms_bwd(dh1, mix_v * r2, r2, gp_ref[...])
        dmix_ref[...] = dmix.astype(BF16)
        _accumulate(dgq_ref, dgq)
        _accumulate(dgp_ref, dgp)

    return _row_call("mid_norm_bwd", body, t, [dn2, h1, dh2, mix], [g_pre_mlp, g_post_mix],
                     [(d, F32), (d, BF16)], [(SUB, d), (SUB, d)], deps=deps)


def _pre_norm_bwd(dn, h0, dh1, g_pre_mix, deps=()):
    t, d = h0.shape

    def body(dn_ref, h0_ref, dh1_ref, g_ref, dh0_ref, dg_ref):
        h0 = h0_ref[...]
        r = _rms(h0)
        dx, dg = _rms_bwd(dn_ref[...], h0 * r, r, g_ref[...])
        dh0_ref[...] = dh1_ref[...] + dx
        _accumulate(dg_ref, dg)

    return _row_call("pre_norm_bwd", body, t, [dn, h0, dh1], [g_pre_mix], [(d, F32)], [(SUB, d)], deps=deps)


def _layer_norm_silu(a1, ln_g, ln_b, deps=()):
    t, c = a1.shape

    def body(a1_ref, g_ref, b_ref, a3_ref):
        a = a1_ref[...]
        mu = jnp.mean(a, axis=-1, keepdims=True)
        xc = a - mu
        rstd = lax.rsqrt(jnp.mean(xc * xc, axis=-1, keepdims=True) + LN_EPS)
        z = xc * rstd * g_ref[...] + b_ref[...]
        a3_ref[...] = (z * _sigmoid(z)).astype(BF16)

    return _row_call("layer_norm_silu", body, t, [a1], [ln_g, ln_b], [(c, BF16)], [], deps=deps)[0]


def _layer_norm_silu_bwd(da3, a1, ln_g, ln_b, deps=()):
    t, c = a1.shape

    def body(da3_ref, a1_ref, g_ref, b_ref, da1_ref, dg_ref, db_ref):
        a = a1_ref[...]
        g = g_ref[...]
        mu = jnp.mean(a, axis=-1, keepdims=True)
        xc = a - mu
        rstd = lax.rsqrt(jnp.mean(xc * xc, axis=-1, keepdims=True) + LN_EPS)
        xhat = xc * rstd
        z = xhat * g + b_ref[...]
        sg = _sigmoid(z)
        dz = da3_ref[...] * (sg * (1.0 + z * (1.0 - sg)))
        dxhat = dz * g
        da1_ref[...] = rstd * (dxhat - jnp.mean(dxhat, axis=-1, keepdims=True)
                               - xhat * jnp.mean(dxhat * xhat, axis=-1, keepdims=True))
        _accumulate(dg_ref, _colsum8(dz * xhat))
        _accumulate(db_ref, _colsum8(dz))

    return _row_call("layer_norm_silu_bwd", body, t, [da3, a1], [ln_g, ln_b], [(c, F32)], [(SUB, c), (SUB, c)], deps=deps)


def _branch_merge(a3, s, wpw, wso, proj, b_gates, d, deps=()):
    t, cols = proj.shape
    nblk, k, cb = wpw.shape
    w = 1024
    nh = d // w
    per = w // cb
    ga0 = (cols - 2 * d) // w
    tm = _row_tile(t)

    def body(a3_ref, s_ref, wpw_ref, wso_ref, *rest):
        pa_refs, pb_refs, bg_ref = rest[:nh], rest[nh:2 * nh], rest[2 * nh]
        ya_ref, yb_ref, ga_ref, gb_ref, m_ref = rest[2 * nh + 1 + len(deps):]
        a3v, sv = a3_ref[...], s_ref[...]
        for b in range(nblk):
            here = slice(b * cb, (b + 1) * cb)
            local = slice((b % per) * cb, (b % per + 1) * cb)
            ya = jnp.dot(a3v, wpw_ref[b], preferred_element_type=F32)
            yb = jnp.dot(sv, wso_ref[b], preferred_element_type=F32)
            ga = _sigmoid(pa_refs[b // per][:, local] + bg_ref[:, here])
            gb = _sigmoid(pb_refs[b // per][:, local] + bg_ref[:, d + b * cb:d + (b + 1) * cb])
            ya_ref[:, here] = ya.astype(BF16)
            yb_ref[:, here] = yb.astype(BF16)
            ga_ref[:, here] = ga.astype(BF16)
            gb_ref[:, here] = gb.astype(BF16)
            m_ref[:, here] = (ga * ya + gb * yb).astype(BF16)

    tile = pl.BlockSpec((tm, d), lambda i: (i, 0))
    return pl.pallas_call(
        body, name="branch_merge", grid=(t // tm,),
        in_specs=[pl.BlockSpec((tm, k), lambda i: (i, 0)), pl.BlockSpec((tm, k), lambda i: (i, 0)),
                  pl.BlockSpec((nblk, k, cb), lambda i: (0, 0, 0)), pl.BlockSpec((nblk, k, cb), lambda i: (0, 0, 0))]
                 + [pl.BlockSpec((tm, w), lambda i, h=h: (i, ga0 + h)) for h in range(2 * nh)]
                 + [pl.BlockSpec((1, 2 * d), lambda i: (0, 0))] + [ANY] * len(deps),
        out_specs=[tile] * 5,
        out_shape=[jax.ShapeDtypeStruct((t, d), BF16)] * 5,
        compiler_params=_params(1),
    )(a3, s, wpw, wso, *([proj] * (2 * nh)), b_gates, *deps)


def _gate_backward(dmix, wo_full, ga, gb, ya, yb, cols, tm, deps=()):
    t, d = ya.shape
    w = 1024
    nh = d // w
    ga0 = (cols - 2 * d) // w

    def body(dmix_ref, wo_ref, ga_ref, gb_ref, ya_ref, yb_ref, *rest):
        dya_ref, dyb_ref, dp_ref, dba_ref, dbb_ref, stage, sems = rest[len(deps):]
        h, i = pl.program_id(0), pl.program_id(1)
        dm = lax.dot_general(dmix_ref[...], wo_ref[...], (((1,), (1,)), ((), ())), preferred_element_type=F32)
        ga = ga_ref[...].astype(F32)
        gb = gb_ref[...].astype(F32)
        dya_ref[...] = (dm * ga).astype(BF16)
        dyb_ref[...] = (dm * gb).astype(BF16)
        dpa = dm * ya_ref[...].astype(F32) * ga * (1.0 - ga)
        dpb = dm * yb_ref[...].astype(F32) * gb * (1.0 - gb)
        stage[0] = dpa.astype(BF16)
        stage[1] = dpb.astype(BF16)
        rows = pl.ds(pl.multiple_of(i * tm, tm), tm)
        copies = [pltpu.make_async_copy(
            stage.at[g], dp_ref.at[rows, pl.ds(pl.multiple_of((ga0 + g * nh + h) * w, w), w)], sems.at[g])
            for g in range(2)]
        for cp in copies:
            cp.start()

        @pl.when(i == 0)
        def _():
            dba_ref[...] = _colsum8(dpa)
            dbb_ref[...] = _colsum8(dpb)

        @pl.when(i > 0)
        def _():
            dba_ref[...] += _colsum8(dpa)
            dbb_ref[...] += _colsum8(dpb)

        for cp in copies:
            cp.wait()

    tile = pl.BlockSpec((tm, w), lambda h, i: (i, h))
    return pl.pallas_call(
        body, name="gate_backward", grid=(nh, t // tm),
        in_specs=[pl.BlockSpec((tm, d), lambda h, i: (i, 0)),
                  pl.BlockSpec((w, d), lambda h, i: (h, 0)),
                  tile, tile, tile, tile] + [ANY] * len(deps),
        out_specs=[tile, tile, ANY,
                   pl.BlockSpec((SUB, w), lambda h, i: (0, h)),
                   pl.BlockSpec((SUB, w), lambda h, i: (0, h))],
        out_shape=[jax.ShapeDtypeStruct((t, d), BF16), jax.ShapeDtypeStruct((t, d), BF16),
                   jax.ShapeDtypeStruct((t, cols), BF16),
                   jax.ShapeDtypeStruct((SUB, d), F32), jax.ShapeDtypeStruct((SUB, d), F32)],
        scratch_shapes=[pltpu.VMEM((2, tm, w), BF16), pltpu.SemaphoreType.DMA((2,))],
        compiler_params=_params(2),
    )(dmix, wo_full, ga, gb, ya, yb, *deps)


def _shifted_views(win, offsets):
    n = win.shape[0]
    rotated = {}
    views = {}
    for o in offsets:
        q, r = divmod(o, SUB)
        if r not in rotated:
            rotated[r] = win if r == 0 else pltpu.roll(win, n - r, 0)
        views[o] = rotated[r][q * SUB:q * SUB + CONV_CHUNK]
    return views


def _causal_views(xp_ref, ntap, r0):
    win = xp_ref[pl.ds(r0, CONV_CHUNK + CONV_PAD), :]
    views = _shifted_views(win, [CONV_PAD - (ntap - 1 - k) for k in range(ntap)])
    return [views[CONV_PAD - (ntap - 1 - k)] for k in range(ntap)]


def _causal_conv(xp_ref, w_ref, ntap, r0):
    acc = None
    for k, shifted in enumerate(_causal_views(xp_ref, ntap, r0)):
        term = w_ref[k:k + 1, :] * shifted
        acc = term if acc is None else acc + term
    return acc


def _anticausal_conv(xp_ref, w_ref, ntap, r0):
    win = xp_ref[pl.ds(pl.multiple_of(CONV_PAD + r0, CONV_PAD), CONV_CHUNK + CONV_PAD), :]
    views = _shifted_views(win, [ntap - 1 - k for k in range(ntap)])
    acc = None
    for k in range(ntap):
        term = w_ref[k:k + 1, :] * views[ntap - 1 - k]
        acc = term if acc is None else acc + term
    return acc


def _conv_weight_grad(dw_ref, d_chunk, xp_ref, ntap, r0):
    for k, shifted in enumerate(_causal_views(xp_ref, ntap, r0)):
        dw_ref[k * SUB:(k + 1) * SUB, :] += _colsum8(d_chunk * shifted)


def _zero_pads(ref, t):
    ref[0:CONV_PAD, :] = jnp.zeros((CONV_PAD, LANE), F32)
    ref[CONV_PAD + t:CONV_PAD + t + CONV_PAD, :] = jnp.zeros((CONV_PAD, LANE), F32)


def _for_chunks(t, fn):
    def step(idx, carry):
        fn(pl.multiple_of(idx * CONV_CHUNK, CONV_CHUNK))
        return carry

    lax.fori_loop(0, t // CONV_CHUNK, step, 0)


def _conv_forward(proj, conf_w, conf_b, short_w, dc, deps=()):
    t = proj.shape[0]
    nc = dc // LANE

    def body(av_ref, ag_ref, bg_ref, cg_ref, v_ref, cw_ref, cb_ref, sw_ref, *rest):
        a1_ref, s_ref, xa, xb = rest[len(deps):]
        _zero_pads(xa, t)
        _zero_pads(xb, t)
        xa[CONV_PAD:CONV_PAD + t, :] = av_ref[...] * _sigmoid(ag_ref[...])
        xb[CONV_PAD:CONV_PAD + t, :] = cg_ref[...] * v_ref[...]

        def chunk(r0):
            rs = pl.ds(r0, CONV_CHUNK)
            a1_ref[rs, :] = _causal_conv(xa, cw_ref, CONF_K, r0) + cb_ref[...]
            s_ref[rs, :] = (bg_ref[rs, :] * _causal_conv(xb, sw_ref, SHORT_K, r0)).astype(BF16)

        _for_chunks(t, chunk)

    col = lambda g: pl.BlockSpec((t, LANE), lambda c, g=g: (0, g * nc + c))
    return pl.pallas_call(
        body, name="conv_forward", grid=(nc,),
        in_specs=[col(0), col(1), col(2), col(3), col(4),
                  pl.BlockSpec((CONF_K, LANE), lambda c: (0, c)),
                  pl.BlockSpec((1, LANE), lambda c: (0, c)),
                  pl.BlockSpec((SHORT_K, LANE), lambda c: (0, c))] + [ANY] * len(deps),
        out_specs=[pl.BlockSpec((t, LANE), lambda c: (0, c)), pl.BlockSpec((t, LANE), lambda c: (0, c))],
        out_shape=[jax.ShapeDtypeStruct((t, dc), F32), jax.ShapeDtypeStruct((t, dc), BF16)],
        scratch_shapes=[pltpu.VMEM((t + 2 * CONV_PAD, LANE), F32), pltpu.VMEM((t + 2 * CONV_PAD, LANE), F32)],
        compiler_params=_params(1),
    )(proj, proj, proj, proj, proj, conf_w, conf_b, short_w, *deps)


def _conv_backward(dproj, proj, da1, ds, conf_w, short_w, dc):
    t = proj.shape[0]
    nc = dc // LANE

    def body(dp_in, av_ref, ag_ref, bg_ref, cg_ref, v_ref, da1_ref, ds_ref, cw_ref, sw_ref,
             dp_ref, dcw_ref, dcb_ref, dsw_ref, xa, xb, da, db, stage, sems):
        del dp_in
        c = pl.program_id(0)
        for ref in (xa, xb, da, db):
            _zero_pads(ref, t)
        xa[CONV_PAD:CONV_PAD + t, :] = av_ref[...] * _sigmoid(ag_ref[...])
        xb[CONV_PAD:CONV_PAD + t, :] = cg_ref[...] * v_ref[...]
        da[CONV_PAD:CONV_PAD + t, :] = da1_ref[...]
        dcw_ref[...] = jnp.zeros(dcw_ref.shape, F32)
        dsw_ref[...] = jnp.zeros(dsw_ref.shape, F32)
        dcb_ref[...] = jnp.zeros(dcb_ref.shape, F32)

        def through_gate(r0):
            rs = pl.ds(r0, CONV_CHUNK)
            ds_c = ds_ref[rs, :]
            stage[2, rs, :] = (ds_c * _causal_conv(xb, sw_ref, SHORT_K, r0)).astype(BF16)
            db[pl.ds(pl.multiple_of(CONV_PAD + r0, CONV_PAD), CONV_CHUNK), :] = ds_c * bg_ref[rs, :]

        _for_chunks(t, through_gate)

        def through_convs(r0):
            rs = pl.ds(r0, CONV_CHUNK)
            da0 = _anticausal_conv(da, cw_ref, CONF_K, r0)
            sg = _sigmoid(ag_ref[rs, :])
            stage[0, rs, :] = (da0 * sg).astype(BF16)
            stage[1, rs, :] = (da0 * av_ref[rs, :] * sg * (1.0 - sg)).astype(BF16)
            dcv = _anticausal_conv(db, sw_ref, SHORT_K, r0)
            stage[3, rs, :] = (dcv * v_ref[rs, :]).astype(BF16)
            stage[4, rs, :] = (dcv * cg_ref[rs, :]).astype(BF16)
            da1_c = da1_ref[rs, :]
            _conv_weight_grad(dcw_ref, da1_c, xa, CONF_K, r0)
            _conv_weight_grad(dsw_ref, ds_ref[rs, :] * bg_ref[rs, :], xb, SHORT_K, r0)
            dcb_ref[...] += _colsum8(da1_c)

        _for_chunks(t, through_convs)
        copies = [pltpu.make_async_copy(
            stage.at[g], dp_ref.at[:, pl.ds(pl.multiple_of((g * nc + c) * LANE, LANE), LANE)], sems.at[g])
            for g in range(5)]
        for cp in copies:
            cp.start()
        for cp in copies:
            cp.wait()

    col = lambda g: pl.BlockSpec((t, LANE), lambda c, g=g: (0, g * nc + c))
    blk = pl.BlockSpec((t, LANE), lambda c: (0, c))
    return pl.pallas_call(
        body, name="conv_backward", grid=(nc,),
        in_specs=[ANY, col(0), col(1), col(2), col(3), col(4), blk, blk,
                  pl.BlockSpec((CONF_K, LANE), lambda c: (0, c)),
                  pl.BlockSpec((SHORT_K, LANE), lambda c: (0, c))],
        out_specs=[ANY,
                   pl.BlockSpec((CONF_K * SUB, LANE), lambda c: (0, c)),
                   pl.BlockSpec((SUB, LANE), lambda c: (0, c)),
                   pl.BlockSpec((SHORT_K * SUB, LANE), lambda c: (0, c))],
        out_shape=[jax.ShapeDtypeStruct(dproj.shape, dproj.dtype),
                   jax.ShapeDtypeStruct((CONF_K * SUB, dc), F32),
                   jax.ShapeDtypeStruct((SUB, dc), F32),
                   jax.ShapeDtypeStruct((SHORT_K * SUB, dc), F32)],
        scratch_shapes=[pltpu.VMEM((t + 2 * CONV_PAD, LANE), F32)] * 4
                       + [pltpu.VMEM((5, t, LANE), BF16), pltpu.SemaphoreType.DMA((5,))],
        input_output_aliases={0: 0},
        compiler_params=_params(1),
    )(dproj, proj, proj, proj, proj, proj, da1, ds, conf_w, short_w)


def _adamw_math(w, g, m, v):
    m = ADAM_B1 * m + (1.0 - ADAM_B1) * g
    v = ADAM_B2 * v + (1.0 - ADAM_B2) * (g * g)
    m_hat = m / (1.0 - ADAM_B1 ** ADAM_STEP)
    v_hat = v / (1.0 - ADAM_B2 ** ADAM_STEP)
    delta = -ADAM_LR * (m_hat / (jnp.sqrt(v_hat) + ADAM_EPS) + ADAM_WD * w)
    return delta, m, v


def _cast_into_slot(name, w, me_arr, deps=()):
    r, c = w.shape
    tr = 256

    def body(me_ref, w_ref, *rest):
        del me_ref
        rest[-1][0] = w_ref[...].astype(BF16)

    return pl.pallas_call(
        body, name=name,
        grid_spec=pltpu.PrefetchScalarGridSpec(
            num_scalar_prefetch=1, grid=(r // tr,),
            in_specs=[pl.BlockSpec((tr, c), lambda i, me: (i, 0))] + [ANY] * len(deps),
            out_specs=pl.BlockSpec((1, tr, c), lambda i, me: (me[0], i, 0))),
        out_shape=jax.ShapeDtypeStruct((N_DEV, r, c), BF16),
        compiler_params=_params(1),
    )(me_arr, w, *deps)


def _chip_sum(name, full, from_sibling, me_arr):
    _, r, c = full.shape
    tr = min(r, 1024)

    def body(me_ref, full_ref, sib_ref, sums_ref):
        del me_ref
        sums_ref[0] = (full_ref[0].astype(F32) + sib_ref[0].astype(F32)).astype(BF16)

    other = lambda k, me: (me[0] // 2 + 1 + k) % 4
    return pl.pallas_call(
        body, name=name,
        grid_spec=pltpu.PrefetchScalarGridSpec(
            num_scalar_prefetch=1, grid=(r // tr, 3),
            in_specs=[pl.BlockSpec((1, tr, c), lambda i, k, me: (2 * other(k, me) + me[0] % 2, i, 0)),
                      pl.BlockSpec((1, tr, c), lambda i, k, me: (other(k, me), i, 0))],
            out_specs=pl.BlockSpec((1, tr, c), lambda i, k, me: (other(k, me), i, 0))),
        out_shape=jax.ShapeDtypeStruct((4, r, c), BF16),
        compiler_params=_params(2),
    )(me_arr, full, from_sibling)


SC_TILES = 32


def _sc_adamw(name, w, g, m, v):
    r, c = w.shape
    rows_per_tile = r // SC_TILES

    def body(w_hbm, g_hbm, m_hbm, v_hbm, d_hbm, mo_hbm, vo_hbm, wb, gb, mb, vb):
        tile = lax.axis_index("sc_tile") * 2 + lax.axis_index("sc_core")

        @pl.loop(0, rows_per_tile, step=SUB)
        def _(r0):
            rows = pl.ds(tile * rows_per_tile + r0, SUB)
            pltpu.sync_copy(w_hbm.at[rows], wb)
            pltpu.sync_copy(g_hbm.at[rows], gb)
            pltpu.sync_copy(m_hbm.at[rows], mb)
            pltpu.sync_copy(v_hbm.at[rows], vb)

            @pl.loop(0, SUB)
            def _(rr):
                @pl.loop(0, c, step=16)
                def _(i):
                    lanes = pl.ds(i, 16)
                    delta, m_new, v_new = _adamw_math(wb[rr, lanes], gb[rr, lanes], mb[rr, lanes], vb[rr, lanes])
                    wb[rr, lanes] = delta
                    mb[rr, lanes] = m_new
                    vb[rr, lanes] = v_new

            pltpu.sync_copy(wb, d_hbm.at[rows])
            pltpu.sync_copy(mb, mo_hbm.at[rows])
            pltpu.sync_copy(vb, vo_hbm.at[rows])

    return pl.kernel(
        body, name=name,
        out_type=[jax.ShapeDtypeStruct((r, c), F32)] * 3,
        mesh=plsc.VectorSubcoreMesh(core_axis_name="sc_core", subcore_axis_name="sc_tile"),
        scratch_types=[pltpu.VMEM((SUB, c), F32)] * 4,
    )(w, g, m, v)


def _grad_total(name, g_own, from_sibling, landed, me_arr, deps=()):
    r, c = g_own.shape
    tr = min(256, r)

    def body(me_ref, g_ref, *rest):
        g = g_ref[...]
        for l_ref in rest[:4]:
            g = g + l_ref[0].astype(F32)
        rest[-1][...] = g

    tile = pl.BlockSpec((tr, c), lambda i, me: (i, 0))
    return pl.pallas_call(
        body, name=name,
        grid_spec=pltpu.PrefetchScalarGridSpec(
            num_scalar_prefetch=1, grid=(r // tr,),
            in_specs=[tile] + [pl.BlockSpec((1, tr, c), lambda i, me, k=k: ((me[0] // 2 + k) % 4, i, 0))
                               for k in range(4)] + [ANY] * len(deps),
            out_specs=tile),
        out_shape=jax.ShapeDtypeStruct((r, c), F32),
        compiler_params=_params(1),
    )(me_arr, g_own, from_sibling, landed, landed, landed, *deps)


def _adamw_shard(name, w, m, v, parts, me_arr, deps=()):
    r, c = w.shape
    tr = min(256, r // len(parts))
    np_ = len(parts)
    per = r // np_ // tr

    def body(me_ref, w_ref, m_ref, v_ref, *rest):
        g_out, d_out, m_out, v_out = rest[5 * np_ + len(deps):]
        g = None
        for p in range(np_):
            gp = rest[5 * p][...]
            for l_ref in rest[5 * p + 1:5 * p + 5]:
                gp = gp + l_ref[0].astype(F32)
            g = gp if g is None else jnp.where(pl.program_id(0) // per == p, gp, g)
        delta, m_new, v_new = _adamw_math(w_ref[...], g, m_ref[...], v_ref[...])
        g_out[...] = g
        d_out[...] = delta
        m_out[...] = m_new
        v_out[...] = v_new

    tile = pl.BlockSpec((tr, c), lambda i, me: (i, 0))
    part_specs, part_args = [], []
    for p, (g_own, from_sibling, landed) in enumerate(parts):
        row = lambda i, p=p: jnp.clip(i - p * per, 0, per - 1)
        part_specs.append(pl.BlockSpec((tr, c), lambda i, me, row=row: (row(i), 0)))
        part_specs += [pl.BlockSpec((1, tr, c), lambda i, me, k=k, row=row: ((me[0] // 2 + k) % 4, row(i), 0))
                       for k in range(4)]
        part_args += [g_own, from_sibling, landed, landed, landed]
    return pl.pallas_call(
        body, name=name,
        grid_spec=pltpu.PrefetchScalarGridSpec(
            num_scalar_prefetch=1, grid=(r // tr,),
            in_specs=[tile] * 3 + part_specs + [ANY] * len(deps), out_specs=[tile] * 4),
        out_shape=[jax.ShapeDtypeStruct((r, c), F32)] * 4,
        compiler_params=_params(1),
    )(me_arr, w, m, v, *part_args, *deps)


SMALL_W = 1024
VEC_ROWS = 16
LOSS_ROW = 15
META_ROW0 = 16
CONF_ROW0 = 64
SHORT_ROW0 = 96
SMALL_ROWS = 104


def _pack_small(vec_parts, dmeta, dcw, dsw, loss_blk, me_arr):
    widths = [p.shape[1] for p in vec_parts]
    nv = len(vec_parts)

    def body(me_ref, *refs):
        del me_ref
        parts, (dmeta_ref, dcw_ref, dsw_ref, loss_ref, out_ref) = refs[:nv], refs[nv:]
        out_ref[0] = jnp.zeros((SMALL_ROWS, SMALL_W), F32)
        out_ref[0, LOSS_ROW:LOSS_ROW + 1, 0:LANE] = loss_ref[0:1, :]
        row = 0
        for p_ref, wd in zip(parts, widths):
            s = jnp.sum(p_ref[...], axis=0, keepdims=True)
            for h in range(wd // SMALL_W):
                out_ref[0, row:row + 1, :] = s[:, h * SMALL_W:(h + 1) * SMALL_W]
                row += 1
        for h in range(dmeta_ref.shape[1] // SMALL_W):
            out_ref[0, META_ROW0 + h * N_META:META_ROW0 + (h + 1) * N_META, :] = dmeta_ref[:, h * SMALL_W:(h + 1) * SMALL_W]
        for k in range(CONF_K):
            out_ref[0, CONF_ROW0 + k:CONF_ROW0 + k + 1, :] = jnp.sum(dcw_ref[k * SUB:(k + 1) * SUB, :], axis=0, keepdims=True)
        for k in range(SHORT_K):
            out_ref[0, SHORT_ROW0 + k:SHORT_ROW0 + k + 1, :] = jnp.sum(dsw_ref[k * SUB:(k + 1) * SUB, :], axis=0, keepdims=True)

    ins = [*vec_parts, dmeta, dcw, dsw, loss_blk]
    return pl.pallas_call(
        body, name="pack_small",
        grid_spec=pltpu.PrefetchScalarGridSpec(
            num_scalar_prefetch=1, grid=(1,),
            in_specs=[pl.BlockSpec(a.shape, lambda i, me: (0, 0)) for a in ins],
            out_specs=pl.BlockSpec((1, SMALL_ROWS, SMALL_W), lambda i, me: (me[0], 0, 0))),
        out_shape=jax.ShapeDtypeStruct((N_DEV, SMALL_ROWS, SMALL_W), F32),
        compiler_params=_params(1),
    )(me_arr, *ins)


def _small_update(gathered, me_arr, vec_params, meta_p, conf_p, short_p):
    widths = [p[0].shape[1] for p in vec_params]
    nv = len(vec_params)
    mcols = meta_p[0].shape[1]
    per_row = SMALL_W // mcols

    def body(me_ref, gv_ref, gm_ref, gc_ref, gs_ref, *rest):
        del me_ref
        ins, outs = rest[:3 * (nv + 3)], rest[3 * (nv + 3):]

        def total(ref, r0, rows):
            s = ref[0, r0:r0 + rows, :]
            for dev in range(1, N_DEV):
                s = s + ref[dev, r0:r0 + rows, :]
            return s

        grads = []
        row = 0
        for wd in widths:
            pieces = [total(gv_ref, row + h, 1) for h in range(wd // SMALL_W)]
            grads.append(pieces[0] if len(pieces) == 1 else jnp.concatenate(pieces, axis=1))
            row += len(pieces)
        grads.append(total(gm_ref, 0, N_META))
        grads.append(total(gc_ref, 0, CONF_K))
        grads.append(total(gs_ref, 0, SHORT_K))
        loss = gv_ref[0, LOSS_ROW:LOSS_ROW + 1, 0:LANE]
        for dev in range(1, N_DEV):
            loss = loss + gv_ref[dev, LOSS_ROW:LOSS_ROW + 1, 0:LANE]
        outs[-1][...] = loss
        for idx, g in enumerate(grads):
            w_ref, m_ref, v_ref = ins[3 * idx:3 * idx + 3]
            delta, m_new, v_new = _adamw_math(w_ref[...], g, m_ref[...], v_ref[...])
            g_out, d_out, m_out, v_out = outs[4 * idx:4 * idx + 4]
            g_out[...] = g
            d_out[...] = delta
            m_out[...] = m_new
            v_out[...] = v_new

    params = list(vec_params) + [meta_p, conf_p, short_p]
    flat = [a for p in params for a in p]
    whole = lambda a: pl.BlockSpec(a.shape, lambda i, me: (0,) * a.ndim)
    outs = pl.pallas_call(
        body, name="small_update",
        grid_spec=pltpu.PrefetchScalarGridSpec(
            num_scalar_prefetch=1, grid=(1,),
            in_specs=[pl.BlockSpec((N_DEV, VEC_ROWS, SMALL_W), lambda i, me: (0, 0, 0)),
                      pl.BlockSpec((N_DEV, N_META, mcols),
                                   lambda i, me: (0, META_ROW0 // N_META + me[0] // per_row, me[0] % per_row)),
                      pl.BlockSpec((N_DEV, 32, LANE), lambda i, me: (0, CONF_ROW0 // 32, me[0])),
                      pl.BlockSpec((N_DEV, SUB, LANE), lambda i, me: (0, SHORT_ROW0 // SUB, me[0]))]
                     + [whole(a) for a in flat],
            out_specs=[whole(p[0]) for p in params for _ in range(4)]
                      + [pl.BlockSpec((1, LANE), lambda i, me: (0, 0))]),
        out_shape=[jax.ShapeDtypeStruct(p[0].shape, F32) for p in params for _ in range(4)]
                  + [jax.ShapeDtypeStruct((1, LANE), F32)],
        compiler_params=_params(1),
    )(me_arr, gathered, gathered, gathered, gathered, *flat)
    return [tuple(outs[4 * i:4 * i + 4]) for i in range(len(params))], outs[-1][0, 0]


def kernel(x, meta, g_pre_mix, w_in, b_gates, conf_dw_w, conf_dw_b, conf_ln_g, conf_ln_b, conf_w_pw, short_dw_w, short_w_out, w_o, g_post_mix, g_pre_mlp, w_up, w_down, g_post_mlp, loss_target, m_meta, m_g_pre_mix, m_w_in, m_b_gates, m_conf_dw_w, m_conf_dw_b, m_conf_ln_g, m_conf_ln_b, m_conf_w_pw, m_short_dw_w, m_short_w_out, m_w_o, m_g_post_mix, m_g_pre_mlp, m_w_up, m_w_down, m_g_post_mlp, v_meta, v_g_pre_mix, v_w_in, v_b_gates, v_conf_dw_w, v_conf_dw_b, v_conf_ln_g, v_conf_ln_b, v_conf_w_pw, v_short_dw_w, v_short_w_out, v_w_o, v_g_post_mix, v_g_pre_mlp, v_w_up, v_w_down, v_g_post_mlp):
    seq, d = x.shape[1], x.shape[2]
    dc = conf_w_pw.shape[1]
    t_real = N_META + seq
    t = -(-t_real // ROW_TILE) * ROW_TILE
    tm = t // 2
    assert tm % 16 == 0 and d % 1024 == 0 and dc % 1024 == 0
    x_idx, y_idx, c_idx = _position()
    me_arr = jnp.reshape(4 * x_idx + 2 * y_idx + c_idx, (1,)).astype(jnp.int32)

    big = [w_in[0], conf_w_pw[0], short_w_out[0], w_o[0], w_up[0], w_down[0]]
    big_names = ["w_in", "conf_w_pw", "short_w_out", "w_o", "w_up", "w_down"]
    groups = [[0], [1, 2, 3], [4], [5]]
    slots, deps = [], []
    for g, idxs in enumerate(groups):
        slots.append([_cast_into_slot("cast_" + big_names[i], big[i], me_arr, deps=deps) for i in idxs])
        if g == 0:
            direct0 = _remote_start("gather0_direct_start", "gather_direct", slots[0])
            deps = [direct0[3]]
    casts = [sl for group in slots[1:] for sl in group]
    meta_g, cw_g, sw_g = _all_gather("gather_small_params", [meta, conf_dw_w[0], short_dw_w[0]], deps=casts)

    def start_direct(g, deps):
        send, recv, bufs, tok = _remote_start("gather%d_direct_start" % g, "gather_direct", slots[g], deps=deps)
        return (send, recv, bufs), tok

    def relay(g, state, after):
        send, recv, bufs, tok = _remote_pass_on("gather%d_relay" % g, "gather_direct", *state, after, "gather_relay")
        return (send, recv, bufs), tok

    def gathered(g, state, after):
        send, recv, bufs, tok = _remote_pass_on("gather%d_diag" % g, "gather_relay", *state, after, "gather_diag")
        return _remote_wait("gather%d_diag_wait" % g, "gather_diag", send, recv, bufs, len(bufs), [tok])

    unshard =lambda g: jnp.transpose(g, (1, 0, 2)).reshape(g.shape[1], -1)
    meta_full, cw_full, sw_full = unshard(meta_g), unshard(cw_g), unshard(sw_g)

    relay0, tok = relay(0, direct0[:3], [meta_g])
    zrows = jnp.zeros((t - t_real, d), F32) + tok[0, 0] * 0.0
    h0 = jnp.concatenate([meta_full, x[0], zrows], axis=0)
    tgt = jnp.concatenate([jnp.zeros((N_META, d), F32), loss_target[0], zrows], axis=0)
    n = _pre_norm(h0, g_pre_mix)
    direct1, tok = start_direct(1, [tok])
    direct2, tok = start_direct(2, [tok])
    win_g, = gathered(0, relay0, [tok, n])
    proj = _mm_cols_pairs("proj", n, win_g, tm=tm // 2)
    relay1, tok = relay(1, direct1, [proj])
    a1, s = _conv_forward(proj, cw_full, conf_dw_b, sw_full, dc, deps=[tok])
    relay2, tok = relay(2, direct2, [a1])
    direct3, tok = start_direct(3, [tok])
    a3 = _layer_norm_silu(a1, conf_ln_g, conf_ln_b, deps=[tok])
    wpw_g, wso_g, wo_g = gathered(1, relay1, [a3])
    wo_full = wo_g.reshape(d, d)
    ya, yb, gate_a, gate_b, m_mix = _branch_merge(a3, s, wpw_g, wso_g, proj, b_gates, d)
    mix, h1, n2 = _mix_post(m_mix, wo_full, h0, g_post_mix, g_pre_mlp)
    wup_g, = gathered(2, relay2, [n2])

    def up_epilogue(acc):
        r = jnp.maximum(acc, 0.0)
        return r * r, r

    half_up = dict(tm=tm, epilogue=up_epilogue, out_dtypes=(BF16, BF16))
    f, relu_up = _mm_cols("mlp_up0", n2, wup_g, blocks=(0, N_DEV // 2), **half_up)
    relay3, tok = relay(3, direct3, [f])
    f, relu_up = _mm_cols("mlp_up1", n2, wup_g, blocks=(N_DEV // 2, N_DEV), into=(f, relu_up), deps=[tok], **half_up)
    wdn_g, = gathered(3, relay3, [f])
    wdn_full = wdn_g.reshape(-1, d)
    fo = _mm_rows("mlp_down", f, wdn_full, tm=tm // 2, tn=512)
    dfo, dh2, dg_post_mlp, loss_blk = _loss_head(fo, h1, tgt, g_post_mlp, t_real)

    def reduce_start(tag, fulls, deps):
        lands = [lax.empty((4,) + g.shape[1:], BF16) for g in fulls]
        send, recv, bufs, tok = _remote_start("reduce_%s_d2d_start" % tag, "reduce_d2d", fulls, lands, deps=deps)
        return (send, recv, bufs), tok

    def reduce_middle(tag, state, owns, after):
        send, recv, bufs = state
        k = len(owns)
        bufs = _remote_wait("reduce_%s_d2d_wait" % tag, "reduce_d2d", send, recv, bufs, k, after)
        from_sibling = bufs[k:]
        sums = [_chip_sum("chip_sum_%s%d" % (tag, i), bufs[i], from_sibling[i], me_arr) for i in range(k)]
        lands = [lax.empty(sm.shape, BF16) for sm in sums]
        send, recv, bufs, tok = _remote_start("reduce_%s_ici_start" % tag, "reduce_ici", sums, lands)
        return (send, recv, bufs, list(zip(owns, from_sibling))), tok

    def reduce_finish(tag, state, after):
        send, recv, bufs, local = state
        k = len(local)
        bufs = _remote_wait("reduce_%s_ici_wait" % tag, "reduce_ici", send, recv, bufs, k, after)
        return [(own, sib, landed) for (own, sib), landed in zip(local, bufs[k:])]

    dup = _mm_nt_blocks("d_up", dfo, wdn_full, tm=tm, tkb=1024, extra=(relu_up,),
                        epilogue=lambda acc, r: (acc * (2.0 * r.astype(F32)),), out_dtypes=(BF16,))[0]
    gw_down, gw_down_own = _mm_tn("dw_down", f, dfo, me_arr, m=f.shape[1], n=d, tma=512, tn=d, sharded="rows")
    red_down, tok = reduce_start("down", [gw_down], ())
    dn2 = _mm_nt_acc("d_n2", dup, wup_g, tm=tm // 2, tn=512, deps=[tok])
    gw_up, gw_up_own = _mm_tn("dw_up", n2, dup, me_arr, m=d, n=dup.shape[1], tma=512, tn=2048, sharded="cols")
    red_down, tok = reduce_middle("down", red_down, [gw_down_own], [dn2])
    red_up, tok = reduce_start("up", [gw_up], [tok])
    dh1, dmix, dg_pre_mlp, dg_post_mix = _mid_norm_bwd(dn2, h1, dh2, mix, g_pre_mlp, g_post_mix, deps=[tok])
    dya, dyb, dproj, db_a, db_b = _gate_backward(dmix, wo_full, gate_a, gate_b, ya, yb, proj.shape[1], tm // 2)
    db_gates = jnp.concatenate([db_a, db_b], axis=1)
    red_up, tok = reduce_middle("up", red_up, [gw_up_own], [dya])
    gw_o, gw_o_own = _mm_tn("dw_o", m_mix, dmix, me_arr, m=d, n=d, tma=d // N_DEV, tn=d, sharded="rows", deps=[tok])
    da3 = _mm_nt_acc("d_a3", dya, wpw_g, tm=tm, tn=512)
    gw_pw, gw_pw_own = _mm_tn("dw_pw", a3, dya, me_arr, m=dc, n=d, tma=512, tn=d, sharded="cols")
    dsb = _mm_nt_acc("d_s", dyb, wso_g, tm=tm, tn=512)
    gw_so, gw_so_own = _mm_tn("dw_so", s, dyb, me_arr, m=dc, n=d, tma=512, tn=d, sharded="cols")
    (own_dn, sib_dn, landed_dn), = reduce_finish("down", red_down, [gw_so])
    g_down = _grad_total("grad_w_down", own_dn, sib_dn, landed_dn, me_arr)
    sc_down = _sc_adamw("sc_adamw_w_down", w_down[0], g_down, m_w_down[0], v_w_down[0])
    red_mix, tok = reduce_start("mix", [gw_pw, gw_so, gw_o], [g_down])
    da1, dln_g, dln_b = _layer_norm_silu_bwd(da3, a1, conf_ln_g, conf_ln_b, deps=[tok])
    dproj, dcw, dcb, dsw = _conv_backward(dproj, proj, da1, dsb, cw_full, sw_full, dc)
    red_mix, tok = reduce_middle("mix", red_mix, [gw_pw_own, gw_so_own, gw_o_own], [dcb])
    in_cb = w_in.shape[2]
    half = d // 2
    red_in = []
    for part in range(2):
        gw, own = _mm_tn("dw_in%d" % part, n, dproj, me_arr, m=half, n=proj.shape[1], tma=512, tn=2 * in_cb,
                         sharded="cols", a_off=part * (half // 512), deps=[tok])
        state, tok = reduce_start("in%d" % part, [gw], ())
        red_in.append((state, own))
    for part in range(2):
        state, own = red_in[part]
        red_in[part], tok = reduce_middle("in%d" % part, state, [own], [tok])
    dn = _mm_nt_acc("d_n", dproj, win_g, tm=tm // 2, tn=512, deps=[tok])
    dh0, dg_pre_mix = _pre_norm_bwd(dn, h0, dh1, g_pre_mix)
    grad_x = dh0[N_META:t_real][None]

    vec_parts = [dg_pre_mix, db_gates, dcb, dln_g, dln_b, dg_post_mix, dg_pre_mlp, dg_post_mlp]
    packed = _pack_small(vec_parts, dh0[:N_META], dcw, dsw, loss_blk, me_arr)
    send, recv, bufs, tok = _remote_start("small_grads_ici_start", "gather_ici", [packed])
    vec_names = ["g_pre_mix", "b_gates", "conf_dw_b", "conf_ln_g", "conf_ln_b", "g_post_mix", "g_pre_mlp", "g_post_mlp"]
    env = locals()
    results = {}

    def update(nm, parts, deps=()):
        res = _adamw_shard("adamw_" + nm, env[nm][0], env["m_" + nm][0], env["v_" + nm][0], parts, me_arr, deps=deps)
        results[nm] = tuple(r[None] for r in res)
        return res[0]

    results["w_down"] = tuple(r[None] for r in (g_down, *sc_down))
    done = [update("w_up", reduce_finish("up", red_up, [tok]), deps=[tok])]
    bufs = _remote_wait("small_grads_ici_wait", "gather_ici", send, recv, bufs, 1, done)
    send, recv, bufs, tok = _remote_start("small_grads_d2d_start", "gather_d2d", bufs)
    for nm, pair in zip(["conf_w_pw", "short_w_out", "w_o"], reduce_finish("mix", red_mix, [tok])):
        done.append(update(nm, [pair], deps=[tok]))
    small_g, = _remote_wait("small_grads_d2d_wait", "gather_d2d", send, recv, bufs, 1, done)
    triple = lambda nm, sq: tuple(env[p + nm][0] if sq else env[p + nm] for p in ("", "m_", "v_"))
    small, loss = _small_update(small_g, me_arr, [triple(nm, False) for nm in vec_names],
                                triple("meta", False), triple("conf_dw_w", True), triple("short_dw_w", True))
    for nm, res in zip(vec_names + ["meta"], small[:len(vec_names) + 1]):
        results[nm] = res
    results["conf_dw_w"] = tuple(r[None] for r in small[-2])
    results["short_dw_w"] = tuple(r[None] for r in small[-1])
    update("w_in", [reduce_finish("in%d" % part, red_in[part], [small[0][0]])[0] for part in range(2)])

    order = ["meta", "g_pre_mix", "w_in", "b_gates", "conf_dw_w", "conf_dw_b", "conf_ln_g", "conf_ln_b", "conf_w_pw",
             "short_dw_w", "short_w_out", "w_o", "g_post_mix", "g_pre_mlp", "w_up", "w_down", "g_post_mlp"]
    return (loss, grad_x, *[results[nm][0] for nm in order], *[results[nm][1] for nm in order],
            *[results[nm][2] for nm in order], *[results[nm][3] for nm in order])
```

```python
import jax
import jax.numpy as jnp
from jax import lax
from jax.experimental import pallas as pl
from jax.experimental.pallas import tpu as pltpu

N_DEV = 8
N_META = 16
CONF_K = 31
SHORT_K = 3
RMS_EPS = 1e-6
LN_EPS = 1e-5
ADAM_LR = 0.001
ADAM_B1 = 0.9
ADAM_B2 = 0.999
ADAM_EPS = 1e-08
ADAM_WD = 0.01
ADAM_STEP = 10

LANE = 128
SUB = 8
ROW_TILE = 128
CONV_PAD = 32
CONV_CHUNK = 128
VMEM_LIMIT = 56 * 1024 * 1024

F32 = jnp.float32
BF16 = jnp.bfloat16
MESH = pl.DeviceIdType.MESH
ANY = pl.BlockSpec(memory_space=pl.ANY)
HBM_SPEC = pl.BlockSpec(memory_space=pltpu.HBM)
SEM_SPEC = pl.BlockSpec(memory_space=pltpu.SEMAPHORE)
EFFECT = pltpu.SideEffectType.DATAFLOW_SIDE_EFFECTING


def _params(n_axes):
    return pltpu.CompilerParams(dimension_semantics=("arbitrary",) * n_axes, vmem_limit_bytes=VMEM_LIMIT)


def _sigmoid(z):
    return 1.0 / (1.0 + jnp.exp(-z))


def _colsum8(v):
    r, c = v.shape
    return jnp.sum(v.reshape(r // SUB, SUB, c), axis=0)


def _position():
    x, y, c = lax.axis_index("x"), lax.axis_index("y"), lax.axis_index("c")
    return x, y, c


def _flat(p):
    return 4 * p[0] + 2 * p[1] + p[2]


def _all_gather(name, shards, deps=()):
    n, nd = len(shards), len(deps)

    def body(*refs):
        ins, outs = refs[:n], refs[n + nd:2 * n + nd]
        send_sems, recv_sems, local_sems = refs[2 * n + nd:]
        x, y, c = _position()
        me, sibling = (x, y, c), (x, y, 1 - c)
        chips = [(1 - x, y), (x, 1 - y), (1 - x, 1 - y)]

        def copy(q, k, block, to, src=None):
            dst = outs[q].at[_flat(block)]
            return pltpu.make_async_remote_copy(
                src_ref=dst if src is None else src, dst_ref=dst,
                send_sem=send_sems.at[q, k], recv_sem=recv_sems.at[q, k],
                device_id=to, device_id_type=MESH)

        mine = [pltpu.make_async_copy(ins[q], outs[q].at[_flat(me)], local_sems.at[q]) for q in range(n)]
        for cp in mine:
            cp.start()
        first = []
        for q in range(n):
            first.append(copy(q, 0, me, sibling, src=ins[q]))
            for j, chip in enumerate(chips):
                first.append(copy(q, 1 + j, me, (*chip, c), src=ins[q]))
        for cp in first:
            cp.start()
        passed = []
        for q in range(n):
            for j, chip in enumerate(chips):
                copy(q, 1 + j, (*chip, c), me).wait_recv()
                fwd = copy(q, 4 + j, (*chip, c), sibling)
                fwd.start()
                passed.append(fwd)
        for q in range(n):
            copy(q, 0, sibling, me).wait_recv()
            for j, chip in enumerate(chips):
                copy(q, 4 + j, (*chip, 1 - c), me).wait_recv()
        for cp in first + passed:
            cp.wait_send()
        for cp in mine:
            cp.wait()

    return pl.pallas_call(
        body, name=name,
        in_specs=[ANY] * (n + nd), out_specs=[ANY] * n,
        out_shape=[jax.ShapeDtypeStruct((N_DEV,) + s.shape, s.dtype) for s in shards],
        scratch_shapes=[pltpu.SemaphoreType.DMA((n, 7)), pltpu.SemaphoreType.DMA((n, 7)),
                        pltpu.SemaphoreType.DMA((n,))],
    )(*shards, *deps)


N_COPIES = {"gather_ici": 4, "gather_d2d": 3, "gather_direct": 3, "gather_relay": 3, "gather_diag": 1,
            "reduce_d2d": 4, "reduce_ici": 3}


def _copy_plan(kind):
    x, y, c = _position()
    me, sibling = (x, y, c), (x, y, 1 - c)
    chips = [(1 - x, y), (x, 1 - y), (1 - x, 1 - y)]
    if kind == "gather_ici":
        return [(_flat(me), _flat(me), sibling)] + [(_flat(me), _flat(me), (*ch, c)) for ch in chips]
    if kind == "gather_d2d":
        return [(_flat((*ch, c)), _flat((*ch, c)), sibling) for ch in chips]
    if kind == "gather_direct":
        return [(_flat(me), _flat(me), sibling)] + [(_flat(me), _flat(me), (*ch, c)) for ch in chips[:2]]
    if kind == "gather_relay":
        held, to = (x ^ (1 - c), y ^ c, c), (x ^ c, y ^ (1 - c), c)
        return [(_flat(held), _flat(held), to)] + [(_flat((*ch, c)), _flat((*ch, c)), sibling) for ch in chips[:2]]
    if kind == "gather_diag":
        return [(_flat((*chips[2], c)), _flat((*chips[2], c)), sibling)]
    if kind == "reduce_d2d":
        return [(2 * chip + (1 - c), chip, sibling) for chip in range(4)]
    return [(2 * ch[0] + ch[1], 2 * x + y, (*ch, c)) for ch in chips]


def _planned_copies(kind, srcs, dsts, send_sems, recv_sems):
    plan = _copy_plan(kind)
    return [pltpu.make_async_remote_copy(
        src_ref=src.at[s_slot], dst_ref=dst.at[d_slot],
        send_sem=send_sems.at[q * len(plan) + k], recv_sem=recv_sems.at[q * len(plan) + k],
        device_id=to, device_id_type=MESH)
        for q, (src, dst) in enumerate(zip(srcs, dsts)) for k, (s_slot, d_slot, to) in enumerate(plan)]


def _remote_start(name, kind, srcs, lands=None, deps=()):
    n = len(srcs)
    bufs = list(srcs) + ([] if lands is None else list(lands))
    nb, nd = len(bufs), len(deps)
    nsem = n * N_COPIES[kind]

    def body(*refs):
        ins = refs[:nb]
        send_sems, recv_sems = refs[nb + nd], refs[nb + nd + 1]
        token = refs[-1]
        for cp in _planned_copies(kind, ins[:n], ins[:n] if lands is None else ins[n:], send_sems, recv_sems):
            cp.start()
        token[...] = jnp.zeros_like(token)

    outs = pl.pallas_call(
        body, name=name,
        out_shape=(pltpu.SemaphoreType.DMA((nsem,)), pltpu.SemaphoreType.DMA((nsem,)),
                   *[pltpu.HBM(b.shape, b.dtype) for b in bufs], jax.ShapeDtypeStruct((SUB, LANE), F32)),
        in_specs=[HBM_SPEC] * nb + [ANY] * nd,
        out_specs=(SEM_SPEC, SEM_SPEC, *[HBM_SPEC] * nb, pl.BlockSpec(memory_space=pltpu.VMEM)),
        input_output_aliases={i: 2 + i for i in range(nb)},
        compiler_params=pltpu.CompilerParams(has_side_effects=EFFECT),
    )(*[pltpu.with_memory_space_constraint(b, pltpu.HBM) for b in bufs], *deps)
    return outs[0], outs[1], list(outs[2:2 + nb]), outs[-1]


def _remote_wait(name, kind, send_sems, recv_sems, bufs, n, after):
    nb, na = len(bufs), len(after)
    same = nb == n

    def body(*refs):
        ins = refs[:nb]
        sends, recvs = refs[nb], refs[nb + 1]
        for cp in _planned_copies(kind, ins[:n], ins[:n] if same else ins[n:], sends, recvs):
            cp.wait_send()
            cp.wait_recv()

    outs = pl.pallas_call(
        body, name=name,
        out_shape=[pltpu.HBM(b.shape, b.dtype) for b in bufs],
        in_specs=[HBM_SPEC] * nb + [SEM_SPEC, SEM_SPEC] + [ANY] * na,
        out_specs=[HBM_SPEC] * nb,
        input_output_aliases={i: i for i in range(nb)},
        compiler_params=pltpu.CompilerParams(has_side_effects=EFFECT),
    )(*bufs, send_sems, recv_sems, *after)
    return list(outs)


def _remote_pass_on(name, done, send_sems, recv_sems, bufs, after, nxt):
    nb, na = len(bufs), len(after)
    nsem = nb * N_COPIES[nxt]

    def body(*refs):
        ins = refs[:nb]
        new_sends, new_recvs = refs[nb + 2 + na], refs[nb + 3 + na]
        token = refs[-1]
        for cp in _planned_copies(done, ins, ins, refs[nb], refs[nb + 1]):
            cp.wait_send()
            cp.wait_recv()
        for cp in _planned_copies(nxt, ins, ins, new_sends, new_recvs):
            cp.start()
        token[...] = jnp.zeros_like(token)

    outs = pl.pallas_call(
        body, name=name,
        out_shape=(pltpu.SemaphoreType.DMA((nsem,)), pltpu.SemaphoreType.DMA((nsem,)),
                   *[pltpu.HBM(b.shape, b.dtype) for b in bufs], jax.ShapeDtypeStruct((SUB, LANE), F32)),
        in_specs=[HBM_SPEC] * nb + [SEM_SPEC, SEM_SPEC] + [ANY] * na,
        out_specs=(SEM_SPEC, SEM_SPEC, *[HBM_SPEC] * nb, pl.BlockSpec(memory_space=pltpu.VMEM)),
        input_output_aliases={i: 2 + i for i in range(nb)},
        compiler_params=pltpu.CompilerParams(has_side_effects=EFFECT),
    )(*bufs, send_sems, recv_sems, *after)
    return outs[0], outs[1], list(outs[2:2 + nb]), outs[-1]


def _mm_cols(name, a, w, *, tm, blocks, epilogue, out_dtypes, into=(), deps=()):
    t, k = a.shape
    nblk, _, cb = w.shape
    j0, j1 = blocks
    no = len(out_dtypes)

    def body(a_ref, w_ref, *rest):
        acc = jnp.dot(a_ref[...], w_ref[0], preferred_element_type=F32)
        for o_ref, o in zip(rest[len(into) + len(deps):], epilogue(acc)):
            o_ref[...] = o.astype(o_ref.dtype)

    return pl.pallas_call(
        body, name=name, grid=(j1 - j0, t // tm),
        in_specs=[pl.BlockSpec((tm, k), lambda j, i: (i, 0)),
                  pl.BlockSpec((1, k, cb), lambda j, i: (j0 + j, 0, 0))] + [ANY] * (len(into) + len(deps)),
        out_specs=[pl.BlockSpec((tm, cb), lambda j, i: (i, j0 + j)) for _ in range(no)],
        out_shape=[jax.ShapeDtypeStruct((t, nblk * cb), dt) for dt in out_dtypes],
        input_output_aliases={2 + idx: idx for idx in range(len(into))},
        compiler_params=_params(2),
    )(a, w, *into, *deps)


MXU_WIDTH = 256


def _mm_cols_pairs(name, a, w, *, tm):
    t, k = a.shape
    nblk, _, cb = w.shape
    main = cb // MXU_WIDTH * MXU_WIDTH
    tail = cb - main
    assert 2 * tail == MXU_WIDTH and nblk % 2 == 0

    def body(a_ref, w_ref, o_ref):
        av = a_ref[...]
        for b in range(2):
            o_ref[:, b * cb:b * cb + main] = jnp.dot(av, w_ref[b, :, 0:main], preferred_element_type=F32)
        tails = jnp.dot(av, jnp.concatenate([w_ref[0, :, main:cb], w_ref[1, :, main:cb]], axis=1),
                        preferred_element_type=F32)
        for b in range(2):
            o_ref[:, b * cb + main:(b + 1) * cb] = tails[:, b * tail:(b + 1) * tail]

    return pl.pallas_call(
        body, name=name, grid=(nblk // 2, t // tm),
        in_specs=[pl.BlockSpec((tm, k), lambda j, i: (i, 0)),
                  pl.BlockSpec((2, k, cb), lambda j, i: (j, 0, 0))],
        out_specs=pl.BlockSpec((tm, 2 * cb), lambda j, i: (i, j)),
        out_shape=jax.ShapeDtypeStruct((t, nblk * cb), F32),
        compiler_params=_params(2),
    )(a, w)


def _mm_rows(name, a, w2d, *, tm, tn):
    t, kf = a.shape
    n = w2d.shape[1]

    def body(a_ref, w_ref, o_ref):
        o_ref[...] = jnp.dot(a_ref[...], w_ref[...], preferred_element_type=F32)

    return pl.pallas_call(
        body, name=name, grid=(t // tm, n // tn),
        in_specs=[pl.BlockSpec((tm, kf), lambda i, j: (i, 0)),
                  pl.BlockSpec((kf, tn), lambda i, j: (0, j))],
        out_specs=pl.BlockSpec((tm, tn), lambda i, j: (i, j)),
        out_shape=jax.ShapeDtypeStruct((t, n), F32),
        compiler_params=_params(2),
    )(a, w2d)


def _mm_nt_acc(name, dy, w, *, tm, tn, col_off=0, deps=()):
    t = dy.shape[0]
    nblk, k, cb = w.shape

    main = cb // MXU_WIDTH * MXU_WIDTH

    def body(dy_ref, w_ref, *rest):
        nt = (((1,), (1,)), ((), ()))
        acc = None
        for b in range(nblk):
            d = lax.dot_general(dy_ref[:, b * cb:b * cb + main], w_ref[b, :, 0:main], nt, preferred_element_type=F32)
            acc = d if acc is None else acc + d
        if main < cb:
            dy_tails = jnp.concatenate([dy_ref[:, b * cb + main:(b + 1) * cb] for b in range(nblk)], axis=1)
            w_tails = jnp.concatenate([w_ref[b, :, main:cb] for b in range(nblk)], axis=1)
            acc = acc + lax.dot_general(dy_tails, w_tails, nt, preferred_element_type=F32)
        rest[-1][...] = acc

    return pl.pallas_call(
        body, name=name, grid=(t // tm, k // tn),
        in_specs=[pl.BlockSpec((tm, nblk * cb), lambda i, j: (i, col_off)),
                  pl.BlockSpec((nblk, tn, cb), lambda i, j: (0, j, 0))] + [ANY] * len(deps),
        out_specs=pl.BlockSpec((tm, tn), lambda i, j: (i, j)),
        out_shape=jax.ShapeDtypeStruct((t, k), F32),
        compiler_params=_params(2),
    )(dy, w, *deps)


def _mm_nt_blocks(name, dy, w2d, *, tm, tkb, extra=(), epilogue=None, out_dtypes=(F32,)):
    t, n = dy.shape
    kf = w2d.shape[0]
    ne = len(extra)

    def body(dy_ref, w_ref, *rest):
        acc = lax.dot_general(dy_ref[...], w_ref[...], (((1,), (1,)), ((), ())), preferred_element_type=F32)
        outs = (acc,) if epilogue is None else epilogue(acc, *[e[...] for e in rest[:ne]])
        for o_ref, o in zip(rest[ne:], outs):
            o_ref[...] = o.astype(o_ref.dtype)

    return pl.pallas_call(
        body, name=name, grid=(kf // tkb, t // tm),
        in_specs=[pl.BlockSpec((tm, n), lambda kb, i: (i, 0)),
                  pl.BlockSpec((tkb, n), lambda kb, i: (kb, 0))]
                 + [pl.BlockSpec((tm, tkb), lambda kb, i: (i, kb)) for _ in extra],
        out_specs=[pl.BlockSpec((tm, tkb), lambda kb, i: (i, kb)) for _ in out_dtypes],
        out_shape=[jax.ShapeDtypeStruct((t, kf), dt) for dt in out_dtypes],
        compiler_params=_params(2),
    )(dy, w2d, *extra)


def _mm_tn(name, a, b, me_arr, *, m, n, tma, tn, sharded, a_off=0, b_off=0, deps=()):
    t = a.shape[0]
    if sharded == "cols":
        cb = n // N_DEV
        nb, q = max(tn // cb, 1), max(cb // tn, 1)
        tw = tn // nb
        full_shape, own_shape = (N_DEV, m, cb), (m, cb)
        full_spec = pl.BlockSpec((nb, tma, tw), lambda i, j, me: (j // q, i, j % q))
    else:
        kb = m // N_DEV
        p = kb // tma
        nb, tw = 1, tn
        full_shape, own_shape = (m, n), (kb, n)
        full_spec = pl.BlockSpec((tma, tn), lambda i, j, me: (i, j))

    def body(me_ref, a_ref, b_ref, *rest):
        full_ref, own_ref, stage, sem = rest[len(deps):]
        i, j = pl.program_id(0), pl.program_id(1)
        acc = lax.dot_general(a_ref[...], b_ref[...], (((0,), (0,)), ((), ())), preferred_element_type=F32)
        for blk in range(nb):
            part = acc[:, blk * tw:(blk + 1) * tw]
            if sharded == "cols":
                full_ref[blk] = part.astype(BF16)
                owner, r0, c0 = (j // q) * nb + blk, i * tma, (j % q) * tw
            else:
                full_ref[...] = part.astype(BF16)
                owner, r0, c0 = i // p, (i % p) * tma, j * tn

            @pl.when(owner == me_ref[0])
            def _():
                stage[...] = part
                cp = pltpu.make_async_copy(
                    stage, own_ref.at[pl.ds(pl.multiple_of(r0, tma), tma), pl.ds(pl.multiple_of(c0, tw), tw)], sem)
                cp.start()
                cp.wait()

    full, own = pl.pallas_call(
        body, name=name,
        grid_spec=pltpu.PrefetchScalarGridSpec(
            num_scalar_prefetch=1, grid=(m // tma, n // tn),
            in_specs=[pl.BlockSpec((t, tma), lambda i, j, me: (0, a_off + i)),
                      pl.BlockSpec((t, tn), lambda i, j, me: (0, b_off + j))] + [ANY] * len(deps),
            out_specs=[full_spec, ANY],
            scratch_shapes=[pltpu.VMEM((tma, tw), F32), pltpu.SemaphoreType.DMA(())]),
        out_shape=[jax.ShapeDtypeStruct(full_shape, BF16), jax.ShapeDtypeStruct(own_shape, F32)],
        compiler_params=_params(2),
    )(me_arr, a, b, *deps)
    if sharded == "rows":
        full = full.reshape(N_DEV, m // N_DEV, n)
    return full, own


def _row_tile(t):
    return t // 8 if (t // 8) % 16 == 0 else ROW_TILE


def _row_call(name, body, t, row_ins, full_ins, row_outs, acc_outs, scratch=(), deps=()):
    tm = _row_tile(t)
    nin = len(row_ins) + len(full_ins)

    def without_deps(*refs):
        body(*refs[:nin], *refs[nin + len(deps):])

    return pl.pallas_call(
        without_deps, name=name, grid=(t // tm,),
        in_specs=[pl.BlockSpec((tm, a.shape[1]), lambda i: (i, 0)) for a in row_ins]
                 + [pl.BlockSpec(a.shape, lambda i: (0, 0)) for a in full_ins] + [ANY] * len(deps),
        out_specs=[pl.BlockSpec((tm, c), lambda i: (i, 0)) for c, _ in row_outs]
                  + [pl.BlockSpec((r, c), lambda i: (0, 0)) for r, c in acc_outs],
        out_shape=[jax.ShapeDtypeStruct((t, c), dt) for c, dt in row_outs]
                  + [jax.ShapeDtypeStruct((r, c), F32) for r, c in acc_outs],
        scratch_shapes=list(scratch),
        compiler_params=_params(1),
    )(*row_ins, *full_ins, *deps)


def _accumulate(ref, v):
    @pl.when(pl.program_id(0) == 0)
    def _():
        ref[...] = v

    @pl.when(pl.program_id(0) > 0)
    def _():
        ref[...] += v


def _rms(v):
    return lax.rsqrt(jnp.mean(v * v, axis=-1, keepdims=True) + RMS_EPS)


def _rms_bwd(dout, u, r, g):
    du = dout * g
    dx = r * (du - u * jnp.mean(du * u, axis=-1, keepdims=True))
    return dx, _colsum8(dout * u)


def _pre_norm(h0, g):
    t, d = h0.shape

    def body(h_ref, g_ref, n_ref):
        h = h_ref[...]
        n_ref[...] = (h * _rms(h) * g_ref[...]).astype(BF16)

    return _row_call("pre_norm", body, t, [h0], [g], [(d, BF16)], [])[0]


def _mix_post(m_mix, wo_full, h0, g_post, g_pre, deps=()):
    t, d = h0.shape
    tm = _row_tile(t)

    def body(m_ref, wo_ref, h0_ref, gp_ref, gq_ref, *rest):
        mix_ref, h1_ref, n2_ref = rest[len(deps):]
        mix_v = jnp.dot(m_ref[...], wo_ref[...], preferred_element_type=F32)
        mix_ref[...] = mix_v
        h1 = h0_ref[...] + mix_v * _rms(mix_v) * gp_ref[...]
        h1_ref[...] = h1
        n2_ref[...] = (h1 * _rms(h1) * gq_ref[...]).astype(BF16)

    tile = pl.BlockSpec((tm, d), lambda i: (i, 0))
    gain = pl.BlockSpec((1, d), lambda i: (0, 0))
    return pl.pallas_call(
        body, name="mix_post", grid=(t // tm,),
        in_specs=[tile, pl.BlockSpec((d, d), lambda i: (0, 0)), tile, gain, gain] + [ANY] * len(deps),
        out_specs=[tile, tile, tile],
        out_shape=[jax.ShapeDtypeStruct((t, d), F32), jax.ShapeDtypeStruct((t, d), F32),
                   jax.ShapeDtypeStruct((t, d), BF16)],
        compiler_params=_params(1),
    )(m_mix, wo_full, h0, g_post, g_pre, *deps)


def _loss_head(fo, h1, tgt, g_post_mlp, t_real):
    t, d = h1.shape
    tile = _row_tile(t)

    def body(fo_ref, h1_ref, tgt_ref, g_ref, dfo_ref, dh2_ref, dg_ref, loss_ref, lacc):
        i = pl.program_id(0)
        fo_v = fo_ref[...]
        g = g_ref[...]
        r = _rms(fo_v)
        u = fo_v * r
        h2 = h1_ref[...] + u * g
        row = i * tile + lax.broadcasted_iota(jnp.int32, (tile, 1), 0)
        valid = jnp.logical_and(row >= N_META, row < t_real)
        diff = jnp.where(valid, h2 - tgt_ref[...], 0.0)
        dh2 = diff * (1.0 / d)
        dh2_ref[...] = dh2
        dfo, dg = _rms_bwd(dh2, u, r, g)
        dfo_ref[...] = dfo.astype(BF16)
        _accumulate(dg_ref, dg)
        _accumulate(lacc, _colsum8(diff * diff))

        @pl.when(i == pl.num_programs(0) - 1)
        def _():
            loss_ref[...] = jnp.full((SUB, LANE), (0.5 / d) * jnp.sum(lacc[...]), F32)

    return _row_call("loss_head", body, t, [fo, h1, tgt], [g_post_mlp],
                     [(d, BF16), (d, F32)], [(SUB, d), (SUB, LANE)], scratch=[pltpu.VMEM((SUB, d), F32)])


def _mid_norm_bwd(dn2, h1, dh2, mix, g_pre_mlp, g_post_mix, deps=()):
    t, d = h1.shape

    def body(dn2_ref, h1_ref, dh2_ref, mix_ref, gq_ref, gp_ref, dh1_ref, dmix_ref, dgq_ref, dgp_ref):
        h1 = h1_ref[...]
        r3 = _rms(h1)
        dx, dgq = _rms_bwd(dn2_ref[...], h1 * r3, r3, gq_ref[...])
        dh1 = dh2_ref[...] + dx
        dh1_ref[...] = dh1
        mix_v = mix_ref[...]
        r2 = _rms(mix_v)
        dmix, dgp = _rms_bwd(dh1, mix_v * r2, r2, gp_ref[...])
        dmix_ref[...] = dmix.astype(BF16)
        _accumulate(dgq_ref, dgq)
        _accumulate(dgp_ref, dgp)

    return _row_call("mid_norm_bwd", body, t, [dn2, h1, dh2, mix], [g_pre_mlp, g_post_mix],
                     [(d, F32), (d, BF16)], [(SUB, d), (SUB, d)], deps=deps)


def _pre_norm_bwd(dn, h0, dh1, g_pre_mix, deps=()):
    t, d = h0.shape

    def body(dn_ref, h0_ref, dh1_ref, g_ref, dh0_ref, dg_ref):
        h0 = h0_ref[...]
        r = _rms(h0)
        dx, dg = _rms_bwd(dn_ref[...], h0 * r, r, g_ref[...])
        dh0_ref[...] = dh1_ref[...] + dx
        _accumulate(dg_ref, dg)

    return _row_call("pre_norm_bwd", body, t, [dn, h0, dh1], [g_pre_mix], [(d, F32)], [(SUB, d)], deps=deps)


def _layer_norm_silu(a1, ln_g, ln_b, deps=()):
    t, c = a1.shape

    def body(a1_ref, g_ref, b_ref, a3_ref):
        a = a1_ref[...]
        mu = jnp.mean(a, axis=-1, keepdims=True)
        xc = a - mu
        rstd = lax.rsqrt(jnp.mean(xc * xc, axis=-1, keepdims=True) + LN_EPS)
        z = xc * rstd * g_ref[...] + b_ref[...]
        a3_ref[...] = (z * _sigmoid(z)).astype(BF16)

    return _row_call("layer_norm_silu", body, t, [a1], [ln_g, ln_b], [(c, BF16)], [], deps=deps)[0]


def _layer_norm_silu_bwd(da3, a1, ln_g, ln_b, deps=()):
    t, c = a1.shape

    def body(da3_ref, a1_ref, g_ref, b_ref, da1_ref, dg_ref, db_ref):
        a = a1_ref[...]
        g = g_ref[...]
        mu = jnp.mean(a, axis=-1, keepdims=True)
        xc = a - mu
        rstd = lax.rsqrt(jnp.mean(xc * xc, axis=-1, keepdims=True) + LN_EPS)
        xhat = xc * rstd
        z = xhat * g + b_ref[...]
        sg = _sigmoid(z)
        dz = da3_ref[...] * (sg * (1.0 + z * (1.0 - sg)))
        dxhat = dz * g
        da1_ref[...] = rstd * (dxhat - jnp.mean(dxhat, axis=-1, keepdims=True)
                               - xhat * jnp.mean(dxhat * xhat, axis=-1, keepdims=True))
        _accumulate(dg_ref, _colsum8(dz * xhat))
        _accumulate(db_ref, _colsum8(dz))

    return _row_call("layer_norm_silu_bwd", body, t, [da3, a1], [ln_g, ln_b], [(c, F32)], [(SUB, c), (SUB, c)], deps=deps)


def _branch_merge(a3, s, wpw, wso, proj, b_gates, d, deps=()):
    t, cols = proj.shape
    nblk, k, cb = wpw.shape
    w = 1024
    nh = d // w
    per = w // cb
    ga0 = (cols - 2 * d) // w
    tm = _row_tile(t)

    def body(a3_ref, s_ref, wpw_ref, wso_ref, *rest):
        pa_refs, pb_refs, bg_ref = rest[:nh], rest[nh:2 * nh], rest[2 * nh]
        ya_ref, yb_ref, ga_ref, gb_ref, m_ref = rest[2 * nh + 1 + len(deps):]
        a3v, sv = a3_ref[...], s_ref[...]
        for b in range(nblk):
            here = slice(b * cb, (b + 1) * cb)
            local = slice((b % per) * cb, (b % per + 1) * cb)
            ya = jnp.dot(a3v, wpw_ref[b], preferred_element_type=F32)
            yb = jnp.dot(sv, wso_ref[b], preferred_element_type=F32)
            ga = _sigmoid(pa_refs[b // per][:, local] + bg_ref[:, here])
            gb = _sigmoid(pb_refs[b // per][:, local] + bg_ref[:, d + b * cb:d + (b + 1) * cb])
            ya_ref[:, here] = ya.astype(BF16)
            yb_ref[:, here] = yb.astype(BF16)
            ga_ref[:, here] = ga.astype(BF16)
            gb_ref[:, here] = gb.astype(BF16)
            m_ref[:, here] = (ga * ya + gb * yb).astype(BF16)

    tile = pl.BlockSpec((tm, d), lambda i: (i, 0))
    return pl.pallas_call(
        body, name="branch_merge", grid=(t // tm,),
        in_specs=[pl.BlockSpec((tm, k), lambda i: (i, 0)), pl.BlockSpec((tm, k), lambda i: (i, 0)),
                  pl.BlockSpec((nblk, k, cb), lambda i: (0, 0, 0)), pl.BlockSpec((nblk, k, cb), lambda i: (0, 0, 0))]
                 + [pl.BlockSpec((tm, w), lambda i, h=h: (i, ga0 + h)) for h in range(2 * nh)]
                 + [pl.BlockSpec((1, 2 * d), lambda i: (0, 0))] + [ANY] * len(deps),
        out_specs=[tile] * 5,
        out_shape=[jax.ShapeDtypeStruct((t, d), BF16)] * 5,
        compiler_params=_params(1),
    )(a3, s, wpw, wso, *([proj] * (2 * nh)), b_gates, *deps)


def _gate_backward(dmix, wo_full, ga, gb, ya, yb, cols, tm, deps=()):
    t, d = ya.shape
    w = 1024
    nh = d // w
    ga0 = (cols - 2 * d) // w

    def body(dmix_ref, wo_ref, ga_ref, gb_ref, ya_ref, yb_ref, *rest):
        dya_ref, dyb_ref, dp_ref, dba_ref, dbb_ref, stage, sems = rest[len(deps):]
        h, i = pl.program_id(0), pl.program_id(1)
        dm = lax.dot_general(dmix_ref[...], wo_ref[...], (((1,), (1,)), ((), ())), preferred_element_type=F32)
        ga = ga_ref[...].astype(F32)
        gb = gb_ref[...].astype(F32)
        dya_ref[...] = (dm * ga).astype(BF16)
        dyb_ref[...] = (dm * gb).astype(BF16)
        dpa = dm * ya_ref[...].astype(F32) * ga * (1.0 - ga)
        dpb = dm * yb_ref[...].astype(F32) * gb * (1.0 - gb)
        stage[0] = dpa.astype(BF16)
        stage[1] = dpb.astype(BF16)
        rows = pl.ds(pl.multiple_of(i * tm, tm), tm)
        copies = [pltpu.make_async_copy(
            stage.at[g], dp_ref.at[rows, pl.ds(pl.multiple_of((ga0 + g * nh + h) * w, w), w)], sems.at[g])
            for g in range(2)]
        for cp in copies:
            cp.start()

        @pl.when(i == 0)
        def _():
            dba_ref[...] = _colsum8(dpa)
            dbb_ref[...] = _colsum8(dpb)

        @pl.when(i > 0)
        def _():
            dba_ref[...] += _colsum8(dpa)
            dbb_ref[...] += _colsum8(dpb)

        for cp in copies:
            cp.wait()

    tile = pl.BlockSpec((tm, w), lambda h, i: (i, h))
    return pl.pallas_call(
        body, name="gate_backward", grid=(nh, t // tm),
        in_specs=[pl.BlockSpec((tm, d), lambda h, i: (i, 0)),
                  pl.BlockSpec((w, d), lambda h, i: (h, 0)),
                  tile, tile, tile, tile] + [ANY] * len(deps),
        out_specs=[tile, tile, ANY,
                   pl.BlockSpec((SUB, w), lambda h, i: (0, h)),
                   pl.BlockSpec((SUB, w), lambda h, i: (0, h))],
        out_shape=[jax.ShapeDtypeStruct((t, d), BF16), jax.ShapeDtypeStruct((t, d), BF16),
                   jax.ShapeDtypeStruct((t, cols), BF16),
                   jax.ShapeDtypeStruct((SUB, d), F32), jax.ShapeDtypeStruct((SUB, d), F32)],
        scratch_shapes=[pltpu.VMEM((2, tm, w), BF16), pltpu.SemaphoreType.DMA((2,))],
        compiler_params=_params(2),
    )(dmix, wo_full, ga, gb, ya, yb, *deps)


def _shifted_views(win, offsets):
    n = win.shape[0]
    rotated = {}
    views = {}
    for o in offsets:
        q, r = divmod(o, SUB)
        if r not in rotated:
            rotated[r] = win if r == 0 else pltpu.roll(win, n - r, 0)
        views[o] = rotated[r][q * SUB:q * SUB + CONV_CHUNK]
    return views


def _causal_views(xp_ref, ntap, r0):
    win = xp_ref[pl.ds(r0, CONV_CHUNK + CONV_PAD), :]
    views = _shifted_views(win, [CONV_PAD - (ntap - 1 - k) for k in range(ntap)])
    return [views[CONV_PAD - (ntap - 1 - k)] for k in range(ntap)]


def _causal_conv(xp_ref, w_ref, ntap, r0):
    acc = None
    for k, shifted in enumerate(_causal_views(xp_ref, ntap, r0)):
        term = w_ref[k:k + 1, :] * shifted
        acc = term if acc is None else acc + term
    return acc


def _anticausal_conv(xp_ref, w_ref, ntap, r0):
    win = xp_ref[pl.ds(pl.multiple_of(CONV_PAD + r0, CONV_PAD), CONV_CHUNK + CONV_PAD), :]
    views = _shifted_views(win, [ntap - 1 - k for k in range(ntap)])
    acc = None
    for k in range(ntap):
        term = w_ref[k:k + 1, :] * views[ntap - 1 - k]
        acc = term if acc is None else acc + term
    return acc


def _conv_weight_grad(dw_ref, d_chunk, xp_ref, ntap, r0):
    for k, shifted in enumerate(_causal_views(xp_ref, ntap, r0)):
        dw_ref[k * SUB:(k + 1) * SUB, :] += _colsum8(d_chunk * shifted)


def _zero_pads(ref, t):
    ref[0:CONV_PAD, :] = jnp.zeros((CONV_PAD, LANE), F32)
    ref[CONV_PAD + t:CONV_PAD + t + CONV_PAD, :] = jnp.zeros((CONV_PAD, LANE), F32)


def _for_chunks(t, fn):
    def step(idx, carry):
        fn(pl.multiple_of(idx * CONV_CHUNK, CONV_CHUNK))
        return carry

    lax.fori_loop(0, t // CONV_CHUNK, step, 0)


def _conv_forward(proj, conf_w, conf_b, short_w, dc, deps=()):
    t = proj.shape[0]
    nc = dc // LANE

    def body(av_ref, ag_ref, bg_ref, cg_ref, v_ref, cw_ref, cb_ref, sw_ref, *rest):
        a1_ref, s_ref, xa, xb = rest[len(deps):]
        _zero_pads(xa, t)
        _zero_pads(xb, t)
        xa[CONV_PAD:CONV_PAD + t, :] = av_ref[...] * _sigmoid(ag_ref[...])
        xb[CONV_PAD:CONV_PAD + t, :] = cg_ref[...] * v_ref[...]

        def chunk(r0):
            rs = pl.ds(r0, CONV_CHUNK)
            a1_ref[rs, :] = _causal_conv(xa, cw_ref, CONF_K, r0) + cb_ref[...]
            s_ref[rs, :] = (bg_ref[rs, :] * _causal_conv(xb, sw_ref, SHORT_K, r0)).astype(BF16)

        _for_chunks(t, chunk)

    col = lambda g: pl.BlockSpec((t, LANE), lambda c, g=g: (0, g * nc + c))
    return pl.pallas_call(
        body, name="conv_forward", grid=(nc,),
        in_specs=[col(0), col(1), col(2), col(3), col(4),
                  pl.BlockSpec((CONF_K, LANE), lambda c: (0, c)),
                  pl.BlockSpec((1, LANE), lambda c: (0, c)),
                  pl.BlockSpec((SHORT_K, LANE), lambda c: (0, c))] + [ANY] * len(deps),
        out_specs=[pl.BlockSpec((t, LANE), lambda c: (0, c)), pl.BlockSpec((t, LANE), lambda c: (0, c))],
        out_shape=[jax.ShapeDtypeStruct((t, dc), F32), jax.ShapeDtypeStruct((t, dc), BF16)],
        scratch_shapes=[pltpu.VMEM((t + 2 * CONV_PAD, LANE), F32), pltpu.VMEM((t + 2 * CONV_PAD, LANE), F32)],
        compiler_params=_params(1),
    )(proj, proj, proj, proj, proj, conf_w, conf_b, short_w, *deps)


def _conv_backward(dproj, proj, da1, ds, conf_w, short_w, dc):
    t = proj.shape[0]
    nc = dc // LANE

    def body(dp_in, av_ref, ag_ref, bg_ref, cg_ref, v_ref, da1_ref, ds_ref, cw_ref, sw_ref,
             dp_ref, dcw_ref, dcb_ref, dsw_ref, xa, xb, da, db, stage, sems):
        del dp_in
        c = pl.program_id(0)
        for ref in (xa, xb, da, db):
            _zero_pads(ref, t)
        xa[CONV_PAD:CONV_PAD + t, :] = av_ref[...] * _sigmoid(ag_ref[...])
        xb[CONV_PAD:CONV_PAD + t, :] = cg_ref[...] * v_ref[...]
        da[CONV_PAD:CONV_PAD + t, :] = da1_ref[...]
        dcw_ref[...] = jnp.zeros(dcw_ref.shape, F32)
        dsw_ref[...] = jnp.zeros(dsw_ref.shape, F32)
        dcb_ref[...] = jnp.zeros(dcb_ref.shape, F32)

        def through_gate(r0):
            rs = pl.ds(r0, CONV_CHUNK)
            ds_c = ds_ref[rs, :]
            stage[2, rs, :] = (ds_c * _causal_conv(xb, sw_ref, SHORT_K, r0)).astype(BF16)
            db[pl.ds(pl.multiple_of(CONV_PAD + r0, CONV_PAD), CONV_CHUNK), :] = ds_c * bg_ref[rs, :]

        _for_chunks(t, through_gate)

        def through_convs(r0):
            rs = pl.ds(r0, CONV_CHUNK)
            da0 = _anticausal_conv(da, cw_ref, CONF_K, r0)
            sg = _sigmoid(ag_ref[rs, :])
            stage[0, rs, :] = (da0 * sg).astype(BF16)
            stage[1, rs, :] = (da0 * av_ref[rs, :] * sg * (1.0 - sg)).astype(BF16)
            dcv = _anticausal_conv(db, sw_ref, SHORT_K, r0)
            stage[3, rs, :] = (dcv * v_ref[rs, :]).astype(BF16)
            stage[4, rs, :] = (dcv * cg_ref[rs, :]).astype(BF16)
            da1_c = da1_ref[rs, :]
            _conv_weight_grad(dcw_ref, da1_c, xa, CONF_K, r0)
            _conv_weight_grad(dsw_ref, ds_ref[rs, :] * bg_ref[rs, :], xb, SHORT_K, r0)
            dcb_ref[...] += _colsum8(da1_c)

        _for_chunks(t, through_convs)
        copies = [pltpu.make_async_copy(
            stage.at[g], dp_ref.at[:, pl.ds(pl.multiple_of((g * nc + c) * LANE, LANE), LANE)], sems.at[g])
            for g in range(5)]
        for cp in copies:
            cp.start()
        for cp in copies:
            cp.wait()

    col = lambda g: pl.BlockSpec((t, LANE), lambda c, g=g: (0, g * nc + c))
    blk = pl.BlockSpec((t, LANE), lambda c: (0, c))
    return pl.pallas_call(
        body, name="conv_backward", grid=(nc,),
        in_specs=[ANY, col(0), col(1), col(2), col(3), col(4), blk, blk,
                  pl.BlockSpec((CONF_K, LANE), lambda c: (0, c)),
                  pl.BlockSpec((SHORT_K, LANE), lambda c: (0, c))],
        out_specs=[ANY,
                   pl.BlockSpec((CONF_K * SUB, LANE), lambda c: (0, c)),
                   pl.BlockSpec((SUB, LANE), lambda c: (0, c)),
                   pl.BlockSpec((SHORT_K * SUB, LANE), lambda c: (0, c))],
        out_shape=[jax.ShapeDtypeStruct(dproj.shape, dproj.dtype),
                   jax.ShapeDtypeStruct((CONF_K * SUB, dc), F32),
                   jax.ShapeDtypeStruct((SUB, dc), F32),
                   jax.ShapeDtypeStruct((SHORT_K * SUB, dc), F32)],
        scratch_shapes=[pltpu.VMEM((t + 2 * CONV_PAD, LANE), F32)] * 4
                       + [pltpu.VMEM((5, t, LANE), BF16), pltpu.SemaphoreType.DMA((5,))],
        input_output_aliases={0: 0},
        compiler_params=_params(1),
    )(dproj, proj, proj, proj, proj, proj, da1, ds, conf_w, short_w)


def _adamw_math(w, g, m, v):
    m = ADAM_B1 * m + (1.0 - ADAM_B1) * g
    v = ADAM_B2 * v + (1.0 - ADAM_B2) * (g * g)
    m_hat = m / (1.0 - ADAM_B1 ** ADAM_STEP)
    v_hat = v / (1.0 - ADAM_B2 ** ADAM_STEP)
    delta = -ADAM_LR * (m_hat / (jnp.sqrt(v_hat) + ADAM_EPS) + ADAM_WD * w)
    return delta, m, v


def _cast_into_slot(name, w, me_arr, deps=()):
    r, c = w.shape
    tr = 256

    def body(me_ref, w_ref, *rest):
        del me_ref
        rest[-1][0] = w_ref[...].astype(BF16)

    return pl.pallas_call(
        body, name=name,
        grid_spec=pltpu.PrefetchScalarGridSpec(
            num_scalar_prefetch=1, grid=(r // tr,),
            in_specs=[pl.BlockSpec((tr, c), lambda i, me: (i, 0))] + [ANY] * len(deps),
            out_specs=pl.BlockSpec((1, tr, c), lambda i, me: (me[0], i, 0))),
        out_shape=jax.ShapeDtypeStruct((N_DEV, r, c), BF16),
        compiler_params=_params(1),
    )(me_arr, w, *deps)


def _chip_sum(name, full, from_sibling, me_arr):
    _, r, c = full.shape
    tr = min(r, 1024)

    def body(me_ref, full_ref, sib_ref, sums_ref):
        del me_ref
        sums_ref[0] = (full_ref[0].astype(F32) + sib_ref[0].astype(F32)).astype(BF16)

    other = lambda k, me: (me[0] // 2 + 1 + k) % 4
    return pl.pallas_call(
        body, name=name,
        grid_spec=pltpu.PrefetchScalarGridSpec(
            num_scalar_prefetch=1, grid=(r // tr, 3),
            in_specs=[pl.BlockSpec((1, tr, c), lambda i, k, me: (2 * other(k, me) + me[0] % 2, i, 0)),
                      pl.BlockSpec((1, tr, c), lambda i, k, me: (other(k, me), i, 0))],
            out_specs=pl.BlockSpec((1, tr, c), lambda i, k, me: (other(k, me), i, 0))),
        out_shape=jax.ShapeDtypeStruct((4, r, c), BF16),
        compiler_params=_params(2),
    )(me_arr, full, from_sibling)


def _adamw_shard(name, w, m, v, parts, me_arr, deps=()):
    r, c = w.shape
    tr = min(256, r // len(parts))
    np_ = len(parts)
    per = r // np_ // tr

    def body(me_ref, w_ref, m_ref, v_ref, *rest):
        g_out, d_out, m_out, v_out = rest[5 * np_ + len(deps):]
        g = None
        for p in range(np_):
            gp = rest[5 * p][...]
            for l_ref in rest[5 * p + 1:5 * p + 5]:
                gp = gp + l_ref[0].astype(F32)
            g = gp if g is None else jnp.where(pl.program_id(0) // per == p, gp, g)
        delta, m_new, v_new = _adamw_math(w_ref[...], g, m_ref[...], v_ref[...])
        g_out[...] = g
        d_out[...] = delta
        m_out[...] = m_new
        v_out[...] = v_new

    tile = pl.BlockSpec((tr, c), lambda i, me: (i, 0))
    part_specs, part_args = [], []
    for p, (g_own, from_sibling, landed) in enumerate(parts):
        row = lambda i, p=p: jnp.clip(i - p * per, 0, per - 1)
        part_specs.append(pl.BlockSpec((tr, c), lambda i, me, row=row: (row(i), 0)))
        part_specs += [pl.BlockSpec((1, tr, c), lambda i, me, k=k, row=row: ((me[0] // 2 + k) % 4, row(i), 0))
                       for k in range(4)]
        part_args += [g_own, from_sibling, landed, landed, landed]
    return pl.pallas_call(
        body, name=name,
        grid_spec=pltpu.PrefetchScalarGridSpec(
            num_scalar_prefetch=1, grid=(r // tr,),
            in_specs=[tile] * 3 + part_specs + [ANY] * len(deps), out_specs=[tile] * 4),
        out_shape=[jax.ShapeDtypeStruct((r, c), F32)] * 4,
        compiler_params=_params(1),
    )(me_arr, w, m, v, *part_args, *deps)


SMALL_W = 1024
VEC_ROWS = 16
LOSS_ROW = 15
META_ROW0 = 16
CONF_ROW0 = 64
SHORT_ROW0 = 96
SMALL_ROWS = 104


def _pack_small(vec_parts, dmeta, dcw, dsw, loss_blk, me_arr):
    widths = [p.shape[1] for p in vec_parts]
    nv = len(vec_parts)

    def body(me_ref, *refs):
        del me_ref
        parts, (dmeta_ref, dcw_ref, dsw_ref, loss_ref, out_ref) = refs[:nv], refs[nv:]
        out_ref[0] = jnp.zeros((SMALL_ROWS, SMALL_W), F32)
        out_ref[0, LOSS_ROW:LOSS_ROW + 1, 0:LANE] = loss_ref[0:1, :]
        row = 0
        for p_ref, wd in zip(parts, widths):
            s = jnp.sum(p_ref[...], axis=0, keepdims=True)
            for h in range(wd // SMALL_W):
                out_ref[0, row:row + 1, :] = s[:, h * SMALL_W:(h + 1) * SMALL_W]
                row += 1
        for h in range(dmeta_ref.shape[1] // SMALL_W):
            out_ref[0, META_ROW0 + h * N_META:META_ROW0 + (h + 1) * N_META, :] = dmeta_ref[:, h * SMALL_W:(h + 1) * SMALL_W]
        for k in range(CONF_K):
            out_ref[0, CONF_ROW0 + k:CONF_ROW0 + k + 1, :] = jnp.sum(dcw_ref[k * SUB:(k + 1) * SUB, :], axis=0, keepdims=True)
        for k in range(SHORT_K):
            out_ref[0, SHORT_ROW0 + k:SHORT_ROW0 + k + 1, :] = jnp.sum(dsw_ref[k * SUB:(k + 1) * SUB, :], axis=0, keepdims=True)

    ins = [*vec_parts, dmeta, dcw, dsw, loss_blk]
    return pl.pallas_call(
        body, name="pack_small",
        grid_spec=pltpu.PrefetchScalarGridSpec(
            num_scalar_prefetch=1, grid=(1,),
            in_specs=[pl.BlockSpec(a.shape, lambda i, me: (0, 0)) for a in ins],
            out_specs=pl.BlockSpec((1, SMALL_ROWS, SMALL_W), lambda i, me: (me[0], 0, 0))),
        out_shape=jax.ShapeDtypeStruct((N_DEV, SMALL_ROWS, SMALL_W), F32),
        compiler_params=_params(1),
    )(me_arr, *ins)


def _small_update(gathered, me_arr, vec_params, meta_p, conf_p, short_p):
    widths = [p[0].shape[1] for p in vec_params]
    nv = len(vec_params)
    mcols = meta_p[0].shape[1]
    per_row = SMALL_W // mcols

    def body(me_ref, gv_ref, gm_ref, gc_ref, gs_ref, *rest):
        del me_ref
        ins, outs = rest[:3 * (nv + 3)], rest[3 * (nv + 3):]

        def total(ref, r0, rows):
            s = ref[0, r0:r0 + rows, :]
            for dev in range(1, N_DEV):
                s = s + ref[dev, r0:r0 + rows, :]
            return s

        grads = []
        row = 0
        for wd in widths:
            pieces = [total(gv_ref, row + h, 1) for h in range(wd // SMALL_W)]
            grads.append(pieces[0] if len(pieces) == 1 else jnp.concatenate(pieces, axis=1))
            row += len(pieces)
        grads.append(total(gm_ref, 0, N_META))
        grads.append(total(gc_ref, 0, CONF_K))
        grads.append(total(gs_ref, 0, SHORT_K))
        loss = gv_ref[0, LOSS_ROW:LOSS_ROW + 1, 0:LANE]
        for dev in range(1, N_DEV):
            loss = loss + gv_ref[dev, LOSS_ROW:LOSS_ROW + 1, 0:LANE]
        outs[-1][...] = loss
        for idx, g in enumerate(grads):
            w_ref, m_ref, v_ref = ins[3 * idx:3 * idx + 3]
            delta, m_new, v_new = _adamw_math(w_ref[...], g, m_ref[...], v_ref[...])
            g_out, d_out, m_out, v_out = outs[4 * idx:4 * idx + 4]
            g_out[...] = g
            d_out[...] = delta
            m_out[...] = m_new
            v_out[...] = v_new

    params = list(vec_params) + [meta_p, conf_p, short_p]
    flat = [a for p in params for a in p]
    whole = lambda a: pl.BlockSpec(a.shape, lambda i, me: (0,) * a.ndim)
    outs = pl.pallas_call(
        body, name="small_update",
        grid_spec=pltpu.PrefetchScalarGridSpec(
            num_scalar_prefetch=1, grid=(1,),
            in_specs=[pl.BlockSpec((N_DEV, VEC_ROWS, SMALL_W), lambda i, me: (0, 0, 0)),
                      pl.BlockSpec((N_DEV, N_META, mcols),
                                   lambda i, me: (0, META_ROW0 // N_META + me[0] // per_row, me[0] % per_row)),
                      pl.BlockSpec((N_DEV, 32, LANE), lambda i, me: (0, CONF_ROW0 // 32, me[0])),
                      pl.BlockSpec((N_DEV, SUB, LANE), lambda i, me: (0, SHORT_ROW0 // SUB, me[0]))]
                     + [whole(a) for a in flat],
            out_specs=[whole(p[0]) for p in params for _ in range(4)]
                      + [pl.BlockSpec((1, LANE), lambda i, me: (0, 0))]),
        out_shape=[jax.ShapeDtypeStruct(p[0].shape, F32) for p in params for _ in range(4)]
                  + [jax.ShapeDtypeStruct((1, LANE), F32)],
        compiler_params=_params(1),
    )(me_arr, gathered, gathered, gathered, gathered, *flat)
    return [tuple(outs[4 * i:4 * i + 4]) for i in range(len(params))], outs[-1][0, 0]


def kernel(x, meta, g_pre_mix, w_in, b_gates, conf_dw_w, conf_dw_b, conf_ln_g, conf_ln_b, conf_w_pw, short_dw_w, short_w_out, w_o, g_post_mix, g_pre_mlp, w_up, w_down, g_post_mlp, loss_target, m_meta, m_g_pre_mix, m_w_in, m_b_gates, m_conf_dw_w, m_conf_dw_b, m_conf_ln_g, m_conf_ln_b, m_conf_w_pw, m_short_dw_w, m_short_w_out, m_w_o, m_g_post_mix, m_g_pre_mlp, m_w_up, m_w_down, m_g_post_mlp, v_meta, v_g_pre_mix, v_w_in, v_b_gates, v_conf_dw_w, v_conf_dw_b, v_conf_ln_g, v_conf_ln_b, v_conf_w_pw, v_short_dw_w, v_short_w_out, v_w_o, v_g_post_mix, v_g_pre_mlp, v_w_up, v_w_down, v_g_post_mlp):
    seq, d = x.shape[1], x.shape[2]
    dc = conf_w_pw.shape[1]
    t_real = N_META + seq
    t = -(-t_real // ROW_TILE) * ROW_TILE
    tm = t // 2
    assert tm % 16 == 0 and d % 1024 == 0 and dc % 1024 == 0
    x_idx, y_idx, c_idx = _position()
    me_arr = jnp.reshape(4 * x_idx + 2 * y_idx + c_idx, (1,)).astype(jnp.int32)

    big = [w_in[0], conf_w_pw[0], short_w_out[0], w_o[0], w_up[0], w_down[0]]
    big_names = ["w_in", "conf_w_pw", "short_w_out", "w_o", "w_up", "w_down"]
    groups = [[0], [1, 2, 3], [4], [5]]
    slots, deps = [], []
    for g, idxs in enumerate(groups):
        slots.append([_cast_into_slot("cast_" + big_names[i], big[i], me_arr, deps=deps) for i in idxs])
        if g == 0:
            direct0 = _remote_start("gather0_direct_start", "gather_direct", slots[0])
            deps = [direct0[3]]
    casts = [sl for group in slots[1:] for sl in group]
    meta_g, cw_g, sw_g = _all_gather("gather_small_params", [meta, conf_dw_w[0], short_dw_w[0]], deps=casts)

    def start_direct(g, deps):
        send, recv, bufs, tok = _remote_start("gather%d_direct_start" % g, "gather_direct", slots[g], deps=deps)
        return (send, recv, bufs), tok

    def relay(g, state, after):
        send, recv, bufs, tok = _remote_pass_on("gather%d_relay" % g, "gather_direct", *state, after, "gather_relay")
        return (send, recv, bufs), tok

    def gathered(g, state, after):
        send, recv, bufs, tok = _remote_pass_on("gather%d_diag" % g, "gather_relay", *state, after, "gather_diag")
        return _remote_wait("gather%d_diag_wait" % g, "gather_diag", send, recv, bufs, len(bufs), [tok])

    unshard =lambda g: jnp.transpose(g, (1, 0, 2)).reshape(g.shape[1], -1)
    meta_full, cw_full, sw_full = unshard(meta_g), unshard(cw_g), unshard(sw_g)

    relay0, tok = relay(0, direct0[:3], [meta_g])
    zrows = jnp.zeros((t - t_real, d), F32) + tok[0, 0] * 0.0
    h0 = jnp.concatenate([meta_full, x[0], zrows], axis=0)
    tgt = jnp.concatenate([jnp.zeros((N_META, d), F32), loss_target[0], zrows], axis=0)
    n = _pre_norm(h0, g_pre_mix)
    direct1, tok = start_direct(1, [tok])
    direct2, tok = start_direct(2, [tok])
    win_g, = gathered(0, relay0, [tok, n])
    proj = _mm_cols_pairs("proj", n, win_g, tm=tm // 2)
    relay1, tok = relay(1, direct1, [proj])
    a1, s = _conv_forward(proj, cw_full, conf_dw_b, sw_full, dc, deps=[tok])
    relay2, tok = relay(2, direct2, [a1])
    direct3, tok = start_direct(3, [tok])
    a3 = _layer_norm_silu(a1, conf_ln_g, conf_ln_b, deps=[tok])
    wpw_g, wso_g, wo_g = gathered(1, relay1, [a3])
    wo_full = wo_g.reshape(d, d)
    ya, yb, gate_a, gate_b, m_mix = _branch_merge(a3, s, wpw_g, wso_g, proj, b_gates, d)
    mix, h1, n2 = _mix_post(m_mix, wo_full, h0, g_post_mix, g_pre_mlp)
    wup_g, = gathered(2, relay2, [n2])

    def up_epilogue(acc):
        r = jnp.maximum(acc, 0.0)
        return r * r, r

    half_up = dict(tm=tm, epilogue=up_epilogue, out_dtypes=(BF16, BF16))
    f, relu_up = _mm_cols("mlp_up0", n2, wup_g, blocks=(0, N_DEV // 2), **half_up)
    relay3, tok = relay(3, direct3, [f])
    f, relu_up = _mm_cols("mlp_up1", n2, wup_g, blocks=(N_DEV // 2, N_DEV), into=(f, relu_up), deps=[tok], **half_up)
    wdn_g, = gathered(3, relay3, [f])
    wdn_full = wdn_g.reshape(-1, d)
    fo = _mm_rows("mlp_down", f, wdn_full, tm=tm // 2, tn=512)
    dfo, dh2, dg_post_mlp, loss_blk = _loss_head(fo, h1, tgt, g_post_mlp, t_real)

    def reduce_start(tag, fulls, deps):
        lands = [lax.empty((4,) + g.shape[1:], BF16) for g in fulls]
        send, recv, bufs, tok = _remote_start("reduce_%s_d2d_start" % tag, "reduce_d2d", fulls, lands, deps=deps)
        return (send, recv, bufs), tok

    def reduce_middle(tag, state, owns, after):
        send, recv, bufs = state
        k = len(owns)
        bufs = _remote_wait("reduce_%s_d2d_wait" % tag, "reduce_d2d", send, recv, bufs, k, after)
        from_sibling = bufs[k:]
        sums = [_chip_sum("chip_sum_%s%d" % (tag, i), bufs[i], from_sibling[i], me_arr) for i in range(k)]
        lands = [lax.empty(sm.shape, BF16) for sm in sums]
        send, recv, bufs, tok = _remote_start("reduce_%s_ici_start" % tag, "reduce_ici", sums, lands)
        return (send, recv, bufs, list(zip(owns, from_sibling))), tok

    def reduce_finish(tag, state, after):
        send, recv, bufs, local = state
        k = len(local)
        bufs = _remote_wait("reduce_%s_ici_wait" % tag, "reduce_ici", send, recv, bufs, k, after)
        return [(own, sib, landed) for (own, sib), landed in zip(local, bufs[k:])]

    dup = _mm_nt_blocks("d_up", dfo, wdn_full, tm=tm, tkb=1024, extra=(relu_up,),
                        epilogue=lambda acc, r: (acc * (2.0 * r.astype(F32)),), out_dtypes=(BF16,))[0]
    gw_down, gw_down_own = _mm_tn("dw_down", f, dfo, me_arr, m=f.shape[1], n=d, tma=512, tn=d, sharded="rows")
    red_down, tok = reduce_start("down", [gw_down], ())
    dn2 = _mm_nt_acc("d_n2", dup, wup_g, tm=tm // 2, tn=512, deps=[tok])
    gw_up, gw_up_own = _mm_tn("dw_up", n2, dup, me_arr, m=d, n=dup.shape[1], tma=512, tn=2048, sharded="cols")
    red_down, tok = reduce_middle("down", red_down, [gw_down_own], [dn2])
    red_up, tok = reduce_start("up", [gw_up], [tok])
    dh1, dmix, dg_pre_mlp, dg_post_mix = _mid_norm_bwd(dn2, h1, dh2, mix, g_pre_mlp, g_post_mix, deps=[tok])
    dya, dyb, dproj, db_a, db_b = _gate_backward(dmix, wo_full, gate_a, gate_b, ya, yb, proj.shape[1], tm // 2)
    db_gates = jnp.concatenate([db_a, db_b], axis=1)
    red_up, tok = reduce_middle("up", red_up, [gw_up_own], [dya])
    gw_o, gw_o_own = _mm_tn("dw_o", m_mix, dmix, me_arr, m=d, n=d, tma=d // N_DEV, tn=d, sharded="rows", deps=[tok])
    da3 = _mm_nt_acc("d_a3", dya, wpw_g, tm=tm, tn=512)
    gw_pw, gw_pw_own = _mm_tn("dw_pw", a3, dya, me_arr, m=dc, n=d, tma=512, tn=d, sharded="cols")
    dsb = _mm_nt_acc("d_s", dyb, wso_g, tm=tm, tn=512)
    gw_so, gw_so_own = _mm_tn("dw_so", s, dyb, me_arr, m=dc, n=d, tma=512, tn=d, sharded="cols")
    red_mix, tok = reduce_start("mix", [gw_pw, gw_so, gw_o], ())
    da1, dln_g, dln_b = _layer_norm_silu_bwd(da3, a1, conf_ln_g, conf_ln_b, deps=[tok])
    dproj, dcw, dcb, dsw = _conv_backward(dproj, proj, da1, dsb, cw_full, sw_full, dc)
    red_mix, tok = reduce_middle("mix", red_mix, [gw_pw_own, gw_so_own, gw_o_own], [dcb])
    in_cb = w_in.shape[2]
    half = d // 2
    red_in = []
    for part in range(2):
        gw, own = _mm_tn("dw_in%d" % part, n, dproj, me_arr, m=half, n=proj.shape[1], tma=512, tn=2 * in_cb,
                         sharded="cols", a_off=part * (half // 512), deps=[tok])
        state, tok = reduce_start("in%d" % part, [gw], ())
        red_in.append((state, own))
    for part in range(2):
        state, own = red_in[part]
        red_in[part], tok = reduce_middle("in%d" % part, state, [own], [tok])
    dn = _mm_nt_acc("d_n", dproj, win_g, tm=tm // 2, tn=512, deps=[tok])
    dh0, dg_pre_mix = _pre_norm_bwd(dn, h0, dh1, g_pre_mix)
    grad_x = dh0[N_META:t_real][None]

    vec_parts = [dg_pre_mix, db_gates, dcb, dln_g, dln_b, dg_post_mix, dg_pre_mlp, dg_post_mlp]
    packed = _pack_small(vec_parts, dh0[:N_META], dcw, dsw, loss_blk, me_arr)
    send, recv, bufs, tok = _remote_start("small_grads_ici_start", "gather_ici", [packed])
    vec_names = ["g_pre_mix", "b_gates", "conf_dw_b", "conf_ln_g", "conf_ln_b", "g_post_mix", "g_pre_mlp", "g_post_mlp"]
    env = locals()
    results = {}

    def update(nm, parts, deps=()):
        res = _adamw_shard("adamw_" + nm, env[nm][0], env["m_" + nm][0], env["v_" + nm][0], parts, me_arr, deps=deps)
        results[nm] = tuple(r[None] for r in res)
        return res[0]

    done = [update("w_down", reduce_finish("down", red_down, [tok]), deps=[tok])]
    done.append(update("w_up", reduce_finish("up", red_up, done)))
    bufs = _remote_wait("small_grads_ici_wait", "gather_ici", send, recv, bufs, 1, done)
    send, recv, bufs, tok = _remote_start("small_grads_d2d_start", "gather_d2d", bufs)
    for nm, pair in zip(["conf_w_pw", "short_w_out", "w_o"], reduce_finish("mix", red_mix, [tok])):
        done.append(update(nm, [pair], deps=[tok]))
    small_g, = _remote_wait("small_grads_d2d_wait", "gather_d2d", send, recv, bufs, 1, done)
    triple = lambda nm, sq: tuple(env[p + nm][0] if sq else env[p + nm] for p in ("", "m_", "v_"))
    small, loss = _small_update(small_g, me_arr, [triple(nm, False) for nm in vec_names],
                                triple("meta", False), triple("conf_dw_w", True), triple("short_dw_w", True))
    for nm, res in zip(vec_names + ["meta"], small[:len(vec_names) + 1]):
        results[nm] = res
    results["conf_dw_w"] = tuple(r[None] for r in small[-2])
    results["short_dw_w"] = tuple(r[None] for r in small[-1])
    update("w_in", [reduce_finish("in%d" % part, red_in[part], [small[0][0]])[0] for part in range(2)])

    order = ["meta", "g_pre_mix", "w_in", "b_gates", "conf_dw_w", "conf_dw_b", "conf_ln_g", "conf_ln_b", "conf_w_pw",
             "short_dw_w", "short_w_out", "w_o", "g_post_mix", "g_pre_mlp", "w_up", "w_down", "g_post_mlp"]
    return (loss, grad_x, *[results[nm][0] for nm in order], *[results[nm][1] for nm in order],
            *[results[nm][2] for nm in order], *[results[nm][3] for nm in order])
```

```python
import jax
import jax.numpy as jnp
from jax import lax
from jax.experimental import pallas as pl
from jax.experimental.pallas import tpu as pltpu

N_DEV = 8
N_META = 16
CONF_K = 31
SHORT_K = 3
RMS_EPS = 1e-6
LN_EPS = 1e-5
ADAM_LR = 0.001
ADAM_B1 = 0.9
ADAM_B2 = 0.999
ADAM_EPS = 1e-08
ADAM_WD = 0.01
ADAM_STEP = 10

LANE = 128
SUB = 8
ROW_TILE = 128
CONV_PAD = 32
CONV_CHUNK = 128
VMEM_LIMIT = 56 * 1024 * 1024

F32 = jnp.float32
BF16 = jnp.bfloat16
MESH = pl.DeviceIdType.MESH
ANY = pl.BlockSpec(memory_space=pl.ANY)
HBM_SPEC = pl.BlockSpec(memory_space=pltpu.HBM)
SEM_SPEC = pl.BlockSpec(memory_space=pltpu.SEMAPHORE)
EFFECT = pltpu.SideEffectType.DATAFLOW_SIDE_EFFECTING


def _params(n_axes):
    return pltpu.CompilerParams(dimension_semantics=("arbitrary",) * n_axes, vmem_limit_bytes=VMEM_LIMIT)


def _sigmoid(z):
    return 1.0 / (1.0 + jnp.exp(-z))


def _colsum8(v):
    r, c = v.shape
    return jnp.sum(v.reshape(r // SUB, SUB, c), axis=0)


def _position():
    x, y, c = lax.axis_index("x"), lax.axis_index("y"), lax.axis_index("c")
    return x, y, c


def _flat(p):
    return 4 * p[0] + 2 * p[1] + p[2]


def _all_gather(name, shards, deps=()):
    n, nd = len(shards), len(deps)

    def body(*refs):
        ins, outs = refs[:n], refs[n + nd:2 * n + nd]
        send_sems, recv_sems, local_sems = refs[2 * n + nd:]
        x, y, c = _position()
        me, sibling = (x, y, c), (x, y, 1 - c)
        chips = [(1 - x, y), (x, 1 - y), (1 - x, 1 - y)]

        def copy(q, k, block, to, src=None):
            dst = outs[q].at[_flat(block)]
            return pltpu.make_async_remote_copy(
                src_ref=dst if src is None else src, dst_ref=dst,
                send_sem=send_sems.at[q, k], recv_sem=recv_sems.at[q, k],
                device_id=to, device_id_type=MESH)

        mine = [pltpu.make_async_copy(ins[q], outs[q].at[_flat(me)], local_sems.at[q]) for q in range(n)]
        for cp in mine:
            cp.start()
        first = []
        for q in range(n):
            first.append(copy(q, 0, me, sibling, src=ins[q]))
            for j, chip in enumerate(chips):
                first.append(copy(q, 1 + j, me, (*chip, c), src=ins[q]))
        for cp in first:
            cp.start()
        passed = []
        for q in range(n):
            for j, chip in enumerate(chips):
                copy(q, 1 + j, (*chip, c), me).wait_recv()
                fwd = copy(q, 4 + j, (*chip, c), sibling)
                fwd.start()
                passed.append(fwd)
        for q in range(n):
            copy(q, 0, sibling, me).wait_recv()
            for j, chip in enumerate(chips):
                copy(q, 4 + j, (*chip, 1 - c), me).wait_recv()
        for cp in first + passed:
            cp.wait_send()
        for cp in mine:
            cp.wait()

    return pl.pallas_call(
        body, name=name,
        in_specs=[ANY] * (n + nd), out_specs=[ANY] * n,
        out_shape=[jax.ShapeDtypeStruct((N_DEV,) + s.shape, s.dtype) for s in shards],
        scratch_shapes=[pltpu.SemaphoreType.DMA((n, 7)), pltpu.SemaphoreType.DMA((n, 7)),
                        pltpu.SemaphoreType.DMA((n,))],
    )(*shards, *deps)


N_COPIES = {"gather_ici": 4, "gather_d2d": 3, "gather_direct": 3, "gather_relay": 3, "gather_diag": 1,
            "reduce_d2d": 4, "reduce_ici": 3}


def _copy_plan(kind):
    x, y, c = _position()
    me, sibling = (x, y, c), (x, y, 1 - c)
    chips = [(1 - x, y), (x, 1 - y), (1 - x, 1 - y)]
    if kind == "gather_ici":
        return [(_flat(me), _flat(me), sibling)] + [(_flat(me), _flat(me), (*ch, c)) for ch in chips]
    if kind == "gather_d2d":
        return [(_flat((*ch, c)), _flat((*ch, c)), sibling) for ch in chips]
    if kind == "gather_direct":
        return [(_flat(me), _flat(me), sibling)] + [(_flat(me), _flat(me), (*ch, c)) for ch in chips[:2]]
    if kind == "gather_relay":
        held, to = (x ^ (1 - c), y ^ c, c), (x ^ c, y ^ (1 - c), c)
        return [(_flat(held), _flat(held), to)] + [(_flat((*ch, c)), _flat((*ch, c)), sibling) for ch in chips[:2]]
    if kind == "gather_diag":
        return [(_flat((*chips[2], c)), _flat((*chips[2], c)), sibling)]
    if kind == "reduce_d2d":
        return [(2 * chip + (1 - c), chip, sibling) for chip in range(4)]
    return [(2 * ch[0] + ch[1], 2 * x + y, (*ch, c)) for ch in chips]


def _planned_copies(kind, srcs, dsts, send_sems, recv_sems):
    plan = _copy_plan(kind)
    return [pltpu.make_async_remote_copy(
        src_ref=src.at[s_slot], dst_ref=dst.at[d_slot],
        send_sem=send_sems.at[q * len(plan) + k], recv_sem=recv_sems.at[q * len(plan) + k],
        device_id=to, device_id_type=MESH)
        for q, (src, dst) in enumerate(zip(srcs, dsts)) for k, (s_slot, d_slot, to) in enumerate(plan)]


def _remote_start(name, kind, srcs, lands=None, deps=()):
    n = len(srcs)
    bufs = list(srcs) + ([] if lands is None else list(lands))
    nb, nd = len(bufs), len(deps)
    nsem = n * N_COPIES[kind]

    def body(*refs):
        ins = refs[:nb]
        send_sems, recv_sems = refs[nb + nd], refs[nb + nd + 1]
        token = refs[-1]
        for cp in _planned_copies(kind, ins[:n], ins[:n] if lands is None else ins[n:], send_sems, recv_sems):
            cp.start()
        token[...] = jnp.zeros_like(token)

    outs = pl.pallas_call(
        body, name=name,
        out_shape=(pltpu.SemaphoreType.DMA((nsem,)), pltpu.SemaphoreType.DMA((nsem,)),
                   *[pltpu.HBM(b.shape, b.dtype) for b in bufs], jax.ShapeDtypeStruct((SUB, LANE), F32)),
        in_specs=[HBM_SPEC] * nb + [ANY] * nd,
        out_specs=(SEM_SPEC, SEM_SPEC, *[HBM_SPEC] * nb, pl.BlockSpec(memory_space=pltpu.VMEM)),
        input_output_aliases={i: 2 + i for i in range(nb)},
        compiler_params=pltpu.CompilerParams(has_side_effects=EFFECT),
    )(*[pltpu.with_memory_space_constraint(b, pltpu.HBM) for b in bufs], *deps)
    return outs[0], outs[1], list(outs[2:2 + nb]), outs[-1]


def _remote_wait(name, kind, send_sems, recv_sems, bufs, n, after):
    nb, na = len(bufs), len(after)
    same = nb == n

    def body(*refs):
        ins = refs[:nb]
        sends, recvs = refs[nb], refs[nb + 1]
        for cp in _planned_copies(kind, ins[:n], ins[:n] if same else ins[n:], sends, recvs):
            cp.wait_send()
            cp.wait_recv()

    outs = pl.pallas_call(
        body, name=name,
        out_shape=[pltpu.HBM(b.shape, b.dtype) for b in bufs],
        in_specs=[HBM_SPEC] * nb + [SEM_SPEC, SEM_SPEC] + [ANY] * na,
        out_specs=[HBM_SPEC] * nb,
        input_output_aliases={i: i for i in range(nb)},
        compiler_params=pltpu.CompilerParams(has_side_effects=EFFECT),
    )(*bufs, send_sems, recv_sems, *after)
    return list(outs)


def _remote_pass_on(name, done, send_sems, recv_sems, bufs, after, nxt):
    nb, na = len(bufs), len(after)
    nsem = nb * N_COPIES[nxt]

    def body(*refs):
        ins = refs[:nb]
        new_sends, new_recvs = refs[nb + 2 + na], refs[nb + 3 + na]
        token = refs[-1]
        for cp in _planned_copies(done, ins, ins, refs[nb], refs[nb + 1]):
            cp.wait_send()
            cp.wait_recv()
        for cp in _planned_copies(nxt, ins, ins, new_sends, new_recvs):
            cp.start()
        token[...] = jnp.zeros_like(token)

    outs = pl.pallas_call(
        body, name=name,
        out_shape=(pltpu.SemaphoreType.DMA((nsem,)), pltpu.SemaphoreType.DMA((nsem,)),
                   *[pltpu.HBM(b.shape, b.dtype) for b in bufs], jax.ShapeDtypeStruct((SUB, LANE), F32)),
        in_specs=[HBM_SPEC] * nb + [SEM_SPEC, SEM_SPEC] + [ANY] * na,
        out_specs=(SEM_SPEC, SEM_SPEC, *[HBM_SPEC] * nb, pl.BlockSpec(memory_space=pltpu.VMEM)),
        input_output_aliases={i: 2 + i for i in range(nb)},
        compiler_params=pltpu.CompilerParams(has_side_effects=EFFECT),
    )(*bufs, send_sems, recv_sems, *after)
    return outs[0], outs[1], list(outs[2:2 + nb]), outs[-1]


def _mm_cols(name, a, w, *, tm, blocks, epilogue, out_dtypes, into=(), deps=()):
    t, k = a.shape
    nblk, _, cb = w.shape
    j0, j1 = blocks
    no = len(out_dtypes)

    def body(a_ref, w_ref, *rest):
        acc = jnp.dot(a_ref[...], w_ref[0], preferred_element_type=F32)
        for o_ref, o in zip(rest[len(into) + len(deps):], epilogue(acc)):
            o_ref[...] = o.astype(o_ref.dtype)

    return pl.pallas_call(
        body, name=name, grid=(j1 - j0, t // tm),
        in_specs=[pl.BlockSpec((tm, k), lambda j, i: (i, 0)),
                  pl.BlockSpec((1, k, cb), lambda j, i: (j0 + j, 0, 0))] + [ANY] * (len(into) + len(deps)),
        out_specs=[pl.BlockSpec((tm, cb), lambda j, i: (i, j0 + j)) for _ in range(no)],
        out_shape=[jax.ShapeDtypeStruct((t, nblk * cb), dt) for dt in out_dtypes],
        input_output_aliases={2 + idx: idx for idx in range(len(into))},
        compiler_params=_params(2),
    )(a, w, *into, *deps)


MXU_WIDTH = 256


def _mm_cols_pairs(name, a, w, *, tm):
    t, k = a.shape
    nblk, _, cb = w.shape
    main = cb // MXU_WIDTH * MXU_WIDTH
    tail = cb - main
    assert 2 * tail == MXU_WIDTH and nblk % 2 == 0

    def body(a_ref, w_ref, o_ref):
        av = a_ref[...]
        for b in range(2):
            o_ref[:, b * cb:b * cb + main] = jnp.dot(av, w_ref[b, :, 0:main], preferred_element_type=F32)
        tails = jnp.dot(av, jnp.concatenate([w_ref[0, :, main:cb], w_ref[1, :, main:cb]], axis=1),
                        preferred_element_type=F32)
        for b in range(2):
            o_ref[:, b * cb + main:(b + 1) * cb] = tails[:, b * tail:(b + 1) * tail]

    return pl.pallas_call(
        body, name=name, grid=(nblk // 2, t // tm),
        in_specs=[pl.BlockSpec((tm, k), lambda j, i: (i, 0)),
                  pl.BlockSpec((2, k, cb), lambda j, i: (j, 0, 0))],
        out_specs=pl.BlockSpec((tm, 2 * cb), lambda j, i: (i, j)),
        out_shape=jax.ShapeDtypeStruct((t, nblk * cb), F32),
        compiler_params=_params(2),
    )(a, w)


def _mm_rows(name, a, w2d, *, tm, tn):
    t, kf = a.shape
    n = w2d.shape[1]

    def body(a_ref, w_ref, o_ref):
        o_ref[...] = jnp.dot(a_ref[...], w_ref[...], preferred_element_type=F32)

    return pl.pallas_call(
        body, name=name, grid=(t // tm, n // tn),
        in_specs=[pl.BlockSpec((tm, kf), lambda i, j: (i, 0)),
                  pl.BlockSpec((kf, tn), lambda i, j: (0, j))],
        out_specs=pl.BlockSpec((tm, tn), lambda i, j: (i, j)),
        out_shape=jax.ShapeDtypeStruct((t, n), F32),
        compiler_params=_params(2),
    )(a, w2d)


def _mm_nt_acc(name, dy, w, *, tm, tn, col_off=0, deps=()):
    t = dy.shape[0]
    nblk, k, cb = w.shape

    main = cb // MXU_WIDTH * MXU_WIDTH

    def body(dy_ref, w_ref, *rest):
        nt = (((1,), (1,)), ((), ()))
        acc = None
        for b in range(nblk):
            d = lax.dot_general(dy_ref[:, b * cb:b * cb + main], w_ref[b, :, 0:main], nt, preferred_element_type=F32)
            acc = d if acc is None else acc + d
        if main < cb:
            dy_tails = jnp.concatenate([dy_ref[:, b * cb + main:(b + 1) * cb] for b in range(nblk)], axis=1)
            w_tails = jnp.concatenate([w_ref[b, :, main:cb] for b in range(nblk)], axis=1)
            acc = acc + lax.dot_general(dy_tails, w_tails, nt, preferred_element_type=F32)
        rest[-1][...] = acc

    return pl.pallas_call(
        body, name=name, grid=(t // tm, k // tn),
        in_specs=[pl.BlockSpec((tm, nblk * cb), lambda i, j: (i, col_off)),
                  pl.BlockSpec((nblk, tn, cb), lambda i, j: (0, j, 0))] + [ANY] * len(deps),
        out_specs=pl.BlockSpec((tm, tn), lambda i, j: (i, j)),
        out_shape=jax.ShapeDtypeStruct((t, k), F32),
        compiler_params=_params(2),
    )(dy, w, *deps)


def _mm_nt_blocks(name, dy, w2d, *, tm, tkb, extra=(), epilogue=None, out_dtypes=(F32,)):
    t, n = dy.shape
    kf = w2d.shape[0]
    ne = len(extra)

    def body(dy_ref, w_ref, *rest):
        acc = lax.dot_general(dy_ref[...], w_ref[...], (((1,), (1,)), ((), ())), preferred_element_type=F32)
        outs = (acc,) if epilogue is None else epilogue(acc, *[e[...] for e in rest[:ne]])
        for o_ref, o in zip(rest[ne:], outs):
            o_ref[...] = o.astype(o_ref.dtype)

    return pl.pallas_call(
        body, name=name, grid=(kf // tkb, t // tm),
        in_specs=[pl.BlockSpec((tm, n), lambda kb, i: (i, 0)),
                  pl.BlockSpec((tkb, n), lambda kb, i: (kb, 0))]
                 + [pl.BlockSpec((tm, tkb), lambda kb, i: (i, kb)) for _ in extra],
        out_specs=[pl.BlockSpec((tm, tkb), lambda kb, i: (i, kb)) for _ in out_dtypes],
        out_shape=[jax.ShapeDtypeStruct((t, kf), dt) for dt in out_dtypes],
        compiler_params=_params(2),
    )(dy, w2d, *extra)


def _mm_tn(name, a, b, me_arr, *, m, n, tma, tn, sharded, a_off=0, b_off=0, deps=()):
    t = a.shape[0]
    if sharded == "cols":
        cb = n // N_DEV
        nb, q = max(tn // cb, 1), max(cb // tn, 1)
        tw = tn // nb
        full_shape, own_shape = (N_DEV, m, cb), (m, cb)
        full_spec = pl.BlockSpec((nb, tma, tw), lambda i, j, me: (j // q, i, j % q))
    else:
        kb = m // N_DEV
        p = kb // tma
        nb, tw = 1, tn
        full_shape, own_shape = (m, n), (kb, n)
        full_spec = pl.BlockSpec((tma, tn), lambda i, j, me: (i, j))

    def body(me_ref, a_ref, b_ref, *rest):
        full_ref, own_ref, stage, sem, pending = rest[len(deps):]
        i, j = pl.program_id(0), pl.program_id(1)

        def own_copy(r0, c0):
            return pltpu.make_async_copy(
                stage, own_ref.at[pl.ds(pl.multiple_of(r0, tma), tma), pl.ds(pl.multiple_of(c0, tw), tw)], sem)

        def drain():
            @pl.when(pending[0] == 1)
            def _():
                own_copy(0, 0).wait()
                pending[0] = 0

        @pl.when(jnp.logical_and(i == 0, j == 0))
        def _():
            pending[0] = 0

        acc = lax.dot_general(a_ref[...], b_ref[...], (((0,), (0,)), ((), ())), preferred_element_type=F32)
        for blk in range(nb):
            part = acc[:, blk * tw:(blk + 1) * tw]
            if sharded == "cols":
                full_ref[blk] = part.astype(BF16)
                owner, r0, c0 = (j // q) * nb + blk, i * tma, (j % q) * tw
            else:
                full_ref[...] = part.astype(BF16)
                owner, r0, c0 = i // p, (i % p) * tma, j * tn

            @pl.when(owner == me_ref[0])
            def _():
                drain()
                stage[...] = part
                own_copy(r0, c0).start()
                pending[0] = 1

        @pl.when(jnp.logical_and(i == pl.num_programs(0) - 1, j == pl.num_programs(1) - 1))
        def _():
            drain()

    full, own = pl.pallas_call(
        body, name=name,
        grid_spec=pltpu.PrefetchScalarGridSpec(
            num_scalar_prefetch=1, grid=(m // tma, n // tn),
            in_specs=[pl.BlockSpec((t, tma), lambda i, j, me: (0, a_off + i)),
                      pl.BlockSpec((t, tn), lambda i, j, me: (0, b_off + j))] + [ANY] * len(deps),
            out_specs=[full_spec, ANY],
            scratch_shapes=[pltpu.VMEM((tma, tw), F32), pltpu.SemaphoreType.DMA(()), pltpu.SMEM((1,), jnp.int32)]),
        out_shape=[jax.ShapeDtypeStruct(full_shape, BF16), jax.ShapeDtypeStruct(own_shape, F32)],
        compiler_params=_params(2),
    )(me_arr, a, b, *deps)
    if sharded == "rows":
        full = full.reshape(N_DEV, m // N_DEV, n)
    return full, own


def _row_tile(t):
    return t // 8 if (t // 8) % 16 == 0 else ROW_TILE


def _row_call(name, body, t, row_ins, full_ins, row_outs, acc_outs, scratch=(), deps=()):
    tm = _row_tile(t)
    nin = len(row_ins) + len(full_ins)

    def without_deps(*refs):
        body(*refs[:nin], *refs[nin + len(deps):])

    return pl.pallas_call(
        without_deps, name=name, grid=(t // tm,),
        in_specs=[pl.BlockSpec((tm, a.shape[1]), lambda i: (i, 0)) for a in row_ins]
                 + [pl.BlockSpec(a.shape, lambda i: (0, 0)) for a in full_ins] + [ANY] * len(deps),
        out_specs=[pl.BlockSpec((tm, c), lambda i: (i, 0)) for c, _ in row_outs]
                  + [pl.BlockSpec((r, c), lambda i: (0, 0)) for r, c in acc_outs],
        out_shape=[jax.ShapeDtypeStruct((t, c), dt) for c, dt in row_outs]
                  + [jax.ShapeDtypeStruct((r, c), F32) for r, c in acc_outs],
        scratch_shapes=list(scratch),
        compiler_params=_params(1),
    )(*row_ins, *full_ins, *deps)


def _accumulate(ref, v):
    @pl.when(pl.program_id(0) == 0)
    def _():
        ref[...] = v

    @pl.when(pl.program_id(0) > 0)
    def _():
        ref[...] += v


def _rms(v):
    return lax.rsqrt(jnp.mean(v * v, axis=-1, keepdims=True) + RMS_EPS)


def _rms_bwd(dout, u, r, g):
    du = dout * g
    dx = r * (du - u * jnp.mean(du * u, axis=-1, keepdims=True))
    return dx, _colsum8(dout * u)


def _pre_norm(h0, g):
    t, d = h0.shape

    def body(h_ref, g_ref, n_ref):
        h = h_ref[...]
        n_ref[...] = (h * _rms(h) * g_ref[...]).astype(BF16)

    return _row_call("pre_norm", body, t, [h0], [g], [(d, BF16)], [])[0]


def _mix_post(m_mix, wo_full, h0, g_post, g_pre, deps=()):
    t, d = h0.shape
    tm = _row_tile(t)

    def body(m_ref, wo_ref, h0_ref, gp_ref, gq_ref, *rest):
        mix_ref, h1_ref, n2_ref = rest[len(deps):]
        mix_v = jnp.dot(m_ref[...], wo_ref[...], preferred_element_type=F32)
        mix_ref[...] = mix_v
        h1 = h0_ref[...] + mix_v * _rms(mix_v) * gp_ref[...]
        h1_ref[...] = h1
        n2_ref[...] = (h1 * _rms(h1) * gq_ref[...]).astype(BF16)

    tile = pl.BlockSpec((tm, d), lambda i: (i, 0))
    gain = pl.BlockSpec((1, d), lambda i: (0, 0))
    return pl.pallas_call(
        body, name="mix_post", grid=(t // tm,),
        in_specs=[tile, pl.BlockSpec((d, d), lambda i: (0, 0)), tile, gain, gain] + [ANY] * len(deps),
        out_specs=[tile, tile, tile],
        out_shape=[jax.ShapeDtypeStruct((t, d), F32), jax.ShapeDtypeStruct((t, d), F32),
                   jax.ShapeDtypeStruct((t, d), BF16)],
        compiler_params=_params(1),
    )(m_mix, wo_full, h0, g_post, g_pre, *deps)


def _loss_head(fo, h1, tgt, g_post_mlp, t_real):
    t, d = h1.shape
    tile = _row_tile(t)

    def body(fo_ref, h1_ref, tgt_ref, g_ref, dfo_ref, dh2_ref, dg_ref, loss_ref, lacc):
        i = pl.program_id(0)
        fo_v = fo_ref[...]
        g = g_ref[...]
        r = _rms(fo_v)
        u = fo_v * r
        h2 = h1_ref[...] + u * g
        row = i * tile + lax.broadcasted_iota(jnp.int32, (tile, 1), 0)
        valid = jnp.logical_and(row >= N_META, row < t_real)
        diff = jnp.where(valid, h2 - tgt_ref[...], 0.0)
        dh2 = diff * (1.0 / d)
        dh2_ref[...] = dh2
        dfo, dg = _rms_bwd(dh2, u, r, g)
        dfo_ref[...] = dfo.astype(BF16)
        _accumulate(dg_ref, dg)
        _accumulate(lacc, _colsum8(diff * diff))

        @pl.when(i == pl.num_programs(0) - 1)
        def _():
            loss_ref[...] = jnp.full((SUB, LANE), (0.5 / d) * jnp.sum(lacc[...]), F32)

    return _row_call("loss_head", body, t, [fo, h1, tgt], [g_post_mlp],
                     [(d, BF16), (d, F32)], [(SUB, d), (SUB, LANE)], scratch=[pltpu.VMEM((SUB, d), F32)])


def _mid_norm_bwd(dn2, h1, dh2, mix, g_pre_mlp, g_post_mix, deps=()):
    t, d = h1.shape

    def body(dn2_ref, h1_ref, dh2_ref, mix_ref, gq_ref, gp_ref, dh1_ref, dmix_ref, dgq_ref, dgp_ref):
        h1 = h1_ref[...]
        r3 = _rms(h1)
        dx, dgq = _rms_bwd(dn2_ref[...], h1 * r3, r3, gq_ref[...])
        dh1 = dh2_ref[...] + dx
        dh1_ref[...] = dh1
        mix_v = mix_ref[...]
        r2 = _rms(mix_v)
        dmix, dgp = _rms_bwd(dh1, mix_v * r2, r2, gp_ref[...])
        dmix_ref[...] = dmix.astype(BF16)
        _accumulate(dgq_ref, dgq)
        _accumulate(dgp_ref, dgp)

    return _row_call("mid_norm_bwd", body, t, [dn2, h1, dh2, mix], [g_pre_mlp, g_post_mix],
                     [(d, F32), (d, BF16)], [(SUB, d), (SUB, d)], deps=deps)


def _pre_norm_bwd(dn, h0, dh1, g_pre_mix, deps=()):
    t, d = h0.shape

    def body(dn_ref, h0_ref, dh1_ref, g_ref, dh0_ref, dg_ref):
        h0 = h0_ref[...]
        r = _rms(h0)
        dx, dg = _rms_bwd(dn_ref[...], h0 * r, r, g_ref[...])
        dh0_ref[...] = dh1_ref[...] + dx
        _accumulate(dg_ref, dg)

    return _row_call("pre_norm_bwd", body, t, [dn, h0, dh1], [g_pre_mix], [(d, F32)], [(SUB, d)], deps=deps)


def _layer_norm_silu(a1, ln_g, ln_b, deps=()):
    t, c = a1.shape

    def body(a1_ref, g_ref, b_ref, a3_ref):
        a = a1_ref[...]
        mu = jnp.mean(a, axis=-1, keepdims=True)
        xc = a - mu
        rstd = lax.rsqrt(jnp.mean(xc * xc, axis=-1, keepdims=True) + LN_EPS)
        z = xc * rstd * g_ref[...] + b_ref[...]
        a3_ref[...] = (z * _sigmoid(z)).astype(BF16)

    return _row_call("layer_norm_silu", body, t, [a1], [ln_g, ln_b], [(c, BF16)], [], deps=deps)[0]


def _layer_norm_silu_bwd(da3, a1, ln_g, ln_b, deps=()):
    t, c = a1.shape

    def body(da3_ref, a1_ref, g_ref, b_ref, da1_ref, dg_ref, db_ref):
        a = a1_ref[...]
        g = g_ref[...]
        mu = jnp.mean(a, axis=-1, keepdims=True)
        xc = a - mu
        rstd = lax.rsqrt(jnp.mean(xc * xc, axis=-1, keepdims=True) + LN_EPS)
        xhat = xc * rstd
        z = xhat * g + b_ref[...]
        sg = _sigmoid(z)
        dz = da3_ref[...] * (sg * (1.0 + z * (1.0 - sg)))
        dxhat = dz * g
        da1_ref[...] = rstd * (dxhat - jnp.mean(dxhat, axis=-1, keepdims=True)
                               - xhat * jnp.mean(dxhat * xhat, axis=-1, keepdims=True))
        _accumulate(dg_ref, _colsum8(dz * xhat))
        _accumulate(db_ref, _colsum8(dz))

    return _row_call("layer_norm_silu_bwd", body, t, [da3, a1], [ln_g, ln_b], [(c, F32)], [(SUB, c), (SUB, c)], deps=deps)


def _branch_merge(a3, s, wpw, wso, proj, b_gates, d, deps=()):
    t, cols = proj.shape
    nblk, k, cb = wpw.shape
    w = 1024
    nh = d // w
    per = w // cb
    ga0 = (cols - 2 * d) // w
    tm = _row_tile(t)

    def body(a3_ref, s_ref, wpw_ref, wso_ref, *rest):
        pa_refs, pb_refs, bg_ref = rest[:nh], rest[nh:2 * nh], rest[2 * nh]
        ya_ref, yb_ref, ga_ref, gb_ref, m_ref = rest[2 * nh + 1 + len(deps):]
        a3v, sv = a3_ref[...], s_ref[...]
        for b in range(nblk):
            here = slice(b * cb, (b + 1) * cb)
            local = slice((b % per) * cb, (b % per + 1) * cb)
            ya = jnp.dot(a3v, wpw_ref[b], preferred_element_type=F32)
            yb = jnp.dot(sv, wso_ref[b], preferred_element_type=F32)
            ga = _sigmoid(pa_refs[b // per][:, local] + bg_ref[:, here])
            gb = _sigmoid(pb_refs[b // per][:, local] + bg_ref[:, d + b * cb:d + (b + 1) * cb])
            ya_ref[:, here] = ya.astype(BF16)
            yb_ref[:, here] = yb.astype(BF16)
            ga_ref[:, here] = ga.astype(BF16)
            gb_ref[:, here] = gb.astype(BF16)
            m_ref[:, here] = (ga * ya + gb * yb).astype(BF16)

    tile = pl.BlockSpec((tm, d), lambda i: (i, 0))
    return pl.pallas_call(
        body, name="branch_merge", grid=(t // tm,),
        in_specs=[pl.BlockSpec((tm, k), lambda i: (i, 0)), pl.BlockSpec((tm, k), lambda i: (i, 0)),
                  pl.BlockSpec((nblk, k, cb), lambda i: (0, 0, 0)), pl.BlockSpec((nblk, k, cb), lambda i: (0, 0, 0))]
                 + [pl.BlockSpec((tm, w), lambda i, h=h: (i, ga0 + h)) for h in range(2 * nh)]
                 + [pl.BlockSpec((1, 2 * d), lambda i: (0, 0))] + [ANY] * len(deps),
        out_specs=[tile] * 5,
        out_shape=[jax.ShapeDtypeStruct((t, d), BF16)] * 5,
        compiler_params=_params(1),
    )(a3, s, wpw, wso, *([proj] * (2 * nh)), b_gates, *deps)


def _gate_backward(dmix, wo_full, ga, gb, ya, yb, cols, tm, deps=()):
    t, d = ya.shape
    w = 1024
    nh = d // w
    ga0 = (cols - 2 * d) // w

    def body(dmix_ref, wo_ref, ga_ref, gb_ref, ya_ref, yb_ref, *rest):
        dya_ref, dyb_ref, dp_ref, dba_ref, dbb_ref, stage, sems = rest[len(deps):]
        h, i = pl.program_id(0), pl.program_id(1)
        dm = lax.dot_general(dmix_ref[...], wo_ref[...], (((1,), (1,)), ((), ())), preferred_element_type=F32)
        ga = ga_ref[...].astype(F32)
        gb = gb_ref[...].astype(F32)
        dya_ref[...] = (dm * ga).astype(BF16)
        dyb_ref[...] = (dm * gb).astype(BF16)
        dpa = dm * ya_ref[...].astype(F32) * ga * (1.0 - ga)
        dpb = dm * yb_ref[...].astype(F32) * gb * (1.0 - gb)

        def copies(row0, colblk):
            return [pltpu.make_async_copy(
                stage.at[g], dp_ref.at[pl.ds(pl.multiple_of(row0, tm), tm),
                                       pl.ds(pl.multiple_of((ga0 + g * nh + colblk) * w, w), w)], sems.at[g])
                for g in range(2)]

        @pl.when(jnp.logical_or(h > 0, i > 0))
        def _():
            for cp in copies(0, 0):
                cp.wait()

        stage[0] = dpa.astype(BF16)
        stage[1] = dpb.astype(BF16)
        for cp in copies(i * tm, h):
            cp.start()

        @pl.when(i == 0)
        def _():
            dba_ref[...] = _colsum8(dpa)
            dbb_ref[...] = _colsum8(dpb)

        @pl.when(i > 0)
        def _():
            dba_ref[...] += _colsum8(dpa)
            dbb_ref[...] += _colsum8(dpb)

        @pl.when(jnp.logical_and(h == pl.num_programs(0) - 1, i == pl.num_programs(1) - 1))
        def _():
            for cp in copies(0, 0):
                cp.wait()

    tile = pl.BlockSpec((tm, w), lambda h, i: (i, h))
    return pl.pallas_call(
        body, name="gate_backward", grid=(nh, t // tm),
        in_specs=[pl.BlockSpec((tm, d), lambda h, i: (i, 0)),
                  pl.BlockSpec((w, d), lambda h, i: (h, 0)),
                  tile, tile, tile, tile] + [ANY] * len(deps),
        out_specs=[tile, tile, ANY,
                   pl.BlockSpec((SUB, w), lambda h, i: (0, h)),
                   pl.BlockSpec((SUB, w), lambda h, i: (0, h))],
        out_shape=[jax.ShapeDtypeStruct((t, d), BF16), jax.ShapeDtypeStruct((t, d), BF16),
                   jax.ShapeDtypeStruct((t, cols), BF16),
                   jax.ShapeDtypeStruct((SUB, d), F32), jax.ShapeDtypeStruct((SUB, d), F32)],
        scratch_shapes=[pltpu.VMEM((2, tm, w), BF16), pltpu.SemaphoreType.DMA((2,))],
        compiler_params=_params(2),
    )(dmix, wo_full, ga, gb, ya, yb, *deps)


def _shifted_views(win, offsets):
    n = win.shape[0]
    rotated = {}
    views = {}
    for o in offsets:
        q, r = divmod(o, SUB)
        if r not in rotated:
            rotated[r] = win if r == 0 else pltpu.roll(win, n - r, 0)
        views[o] = rotated[r][q * SUB:q * SUB + CONV_CHUNK]
    return views


def _causal_views(xp_ref, ntap, r0):
    win = xp_ref[pl.ds(r0, CONV_CHUNK + CONV_PAD), :]
    views = _shifted_views(win, [CONV_PAD - (ntap - 1 - k) for k in range(ntap)])
    return [views[CONV_PAD - (ntap - 1 - k)] for k in range(ntap)]


def _causal_conv(xp_ref, w_ref, ntap, r0):
    acc = None
    for k, shifted in enumerate(_causal_views(xp_ref, ntap, r0)):
        term = w_ref[k:k + 1, :] * shifted
        acc = term if acc is None else acc + term
    return acc


def _anticausal_conv(xp_ref, w_ref, ntap, r0):
    win = xp_ref[pl.ds(pl.multiple_of(CONV_PAD + r0, CONV_PAD), CONV_CHUNK + CONV_PAD), :]
    views = _shifted_views(win, [ntap - 1 - k for k in range(ntap)])
    acc = None
    for k in range(ntap):
        term = w_ref[k:k + 1, :] * views[ntap - 1 - k]
        acc = term if acc is None else acc + term
    return acc


def _conv_weight_grad(dw_ref, d_chunk, xp_ref, ntap, r0):
    for k, shifted in enumerate(_causal_views(xp_ref, ntap, r0)):
        dw_ref[k * SUB:(k + 1) * SUB, :] += _colsum8(d_chunk * shifted)


def _zero_pads(ref, t):
    ref[0:CONV_PAD, :] = jnp.zeros((CONV_PAD, LANE), F32)
    ref[CONV_PAD + t:CONV_PAD + t + CONV_PAD, :] = jnp.zeros((CONV_PAD, LANE), F32)


def _for_chunks(t, fn):
    def step(idx, carry):
        fn(pl.multiple_of(idx * CONV_CHUNK, CONV_CHUNK))
        return carry

    lax.fori_loop(0, t // CONV_CHUNK, step, 0)


def _conv_forward(proj, conf_w, conf_b, short_w, dc, deps=()):
    t = proj.shape[0]
    nc = dc // LANE

    def body(av_ref, ag_ref, bg_ref, cg_ref, v_ref, cw_ref, cb_ref, sw_ref, *rest):
        a1_ref, s_ref, xa, xb = rest[len(deps):]
        _zero_pads(xa, t)
        _zero_pads(xb, t)
        xa[CONV_PAD:CONV_PAD + t, :] = av_ref[...] * _sigmoid(ag_ref[...])
        xb[CONV_PAD:CONV_PAD + t, :] = cg_ref[...] * v_ref[...]

        def chunk(r0):
            rs = pl.ds(r0, CONV_CHUNK)
            a1_ref[rs, :] = _causal_conv(xa, cw_ref, CONF_K, r0) + cb_ref[...]
            s_ref[rs, :] = (bg_ref[rs, :] * _causal_conv(xb, sw_ref, SHORT_K, r0)).astype(BF16)

        _for_chunks(t, chunk)

    col = lambda g: pl.BlockSpec((t, LANE), lambda c, g=g: (0, g * nc + c))
    return pl.pallas_call(
        body, name="conv_forward", grid=(nc,),
        in_specs=[col(0), col(1), col(2), col(3), col(4),
                  pl.BlockSpec((CONF_K, LANE), lambda c: (0, c)),
                  pl.BlockSpec((1, LANE), lambda c: (0, c)),
                  pl.BlockSpec((SHORT_K, LANE), lambda c: (0, c))] + [ANY] * len(deps),
        out_specs=[pl.BlockSpec((t, LANE), lambda c: (0, c)), pl.BlockSpec((t, LANE), lambda c: (0, c))],
        out_shape=[jax.ShapeDtypeStruct((t, dc), F32), jax.ShapeDtypeStruct((t, dc), BF16)],
        scratch_shapes=[pltpu.VMEM((t + 2 * CONV_PAD, LANE), F32), pltpu.VMEM((t + 2 * CONV_PAD, LANE), F32)],
        compiler_params=_params(1),
    )(proj, proj, proj, proj, proj, conf_w, conf_b, short_w, *deps)


def _conv_backward(dproj, proj, da1, ds, conf_w, short_w, dc):
    t = proj.shape[0]
    nc = dc // LANE

    def body(dp_in, av_ref, ag_ref, bg_ref, cg_ref, v_ref, da1_ref, ds_ref, cw_ref, sw_ref,
             dp_ref, dcw_ref, dcb_ref, dsw_ref, xa, xb, da, db, stage, sems):
        del dp_in
        c = pl.program_id(0)
        for ref in (xa, xb, da, db):
            _zero_pads(ref, t)
        xa[CONV_PAD:CONV_PAD + t, :] = av_ref[...] * _sigmoid(ag_ref[...])
        xb[CONV_PAD:CONV_PAD + t, :] = cg_ref[...] * v_ref[...]
        da[CONV_PAD:CONV_PAD + t, :] = da1_ref[...]
        dcw_ref[...] = jnp.zeros(dcw_ref.shape, F32)
        dsw_ref[...] = jnp.zeros(dsw_ref.shape, F32)
        dcb_ref[...] = jnp.zeros(dcb_ref.shape, F32)

        def copies(colblk):
            return [pltpu.make_async_copy(
                stage.at[g], dp_ref.at[:, pl.ds(pl.multiple_of((g * nc + colblk) * LANE, LANE), LANE)], sems.at[g])
                for g in range(5)]

        @pl.when(c > 0)
        def _():
            for cp in copies(0):
                cp.wait()

        def through_gate(r0):
            rs = pl.ds(r0, CONV_CHUNK)
            ds_c = ds_ref[rs, :]
            stage[2, rs, :] = (ds_c * _causal_conv(xb, sw_ref, SHORT_K, r0)).astype(BF16)
            db[pl.ds(pl.multiple_of(CONV_PAD + r0, CONV_PAD), CONV_CHUNK), :] = ds_c * bg_ref[rs, :]

        _for_chunks(t, through_gate)

        def through_convs(r0):
            rs = pl.ds(r0, CONV_CHUNK)
            da0 = _anticausal_conv(da, cw_ref, CONF_K, r0)
            sg = _sigmoid(ag_ref[rs, :])
            stage[0, rs, :] = (da0 * sg).astype(BF16)
            stage[1, rs, :] = (da0 * av_ref[rs, :] * sg * (1.0 - sg)).astype(BF16)
            dcv = _anticausal_conv(db, sw_ref, SHORT_K, r0)
            stage[3, rs, :] = (dcv * v_ref[rs, :]).astype(BF16)
            stage[4, rs, :] = (dcv * cg_ref[rs, :]).astype(BF16)
            da1_c = da1_ref[rs, :]
            _conv_weight_grad(dcw_ref, da1_c, xa, CONF_K, r0)
            _conv_weight_grad(dsw_ref, ds_ref[rs, :] * bg_ref[rs, :], xb, SHORT_K, r0)
            dcb_ref[...] += _colsum8(da1_c)

        _for_chunks(t, through_convs)
        for cp in copies(c):
            cp.start()

        @pl.when(c == pl.num_programs(0) - 1)
        def _():
            for cp in copies(0):
                cp.wait()

    col = lambda g: pl.BlockSpec((t, LANE), lambda c, g=g: (0, g * nc + c))
    blk = pl.BlockSpec((t, LANE), lambda c: (0, c))
    return pl.pallas_call(
        body, name="conv_backward", grid=(nc,),
        in_specs=[ANY, col(0), col(1), col(2), col(3), col(4), blk, blk,
                  pl.BlockSpec((CONF_K, LANE), lambda c: (0, c)),
                  pl.BlockSpec((SHORT_K, LANE), lambda c: (0, c))],
        out_specs=[ANY,
                   pl.BlockSpec((CONF_K * SUB, LANE), lambda c: (0, c)),
                   pl.BlockSpec((SUB, LANE), lambda c: (0, c)),
                   pl.BlockSpec((SHORT_K * SUB, LANE), lambda c: (0, c))],
        out_shape=[jax.ShapeDtypeStruct(dproj.shape, dproj.dtype),
                   jax.ShapeDtypeStruct((CONF_K * SUB, dc), F32),
                   jax.ShapeDtypeStruct((SUB, dc), F32),
                   jax.ShapeDtypeStruct((SHORT_K * SUB, dc), F32)],
        scratch_shapes=[pltpu.VMEM((t + 2 * CONV_PAD, LANE), F32)] * 4
                       + [pltpu.VMEM((5, t, LANE), BF16), pltpu.SemaphoreType.DMA((5,))],
        input_output_aliases={0: 0},
        compiler_params=_params(1),
    )(dproj, proj, proj, proj, proj, proj, da1, ds, conf_w, short_w)


def _adamw_math(w, g, m, v):
    m = ADAM_B1 * m + (1.0 - ADAM_B1) * g
    v = ADAM_B2 * v + (1.0 - ADAM_B2) * (g * g)
    m_hat = m / (1.0 - ADAM_B1 ** ADAM_STEP)
    v_hat = v / (1.0 - ADAM_B2 ** ADAM_STEP)
    delta = -ADAM_LR * (m_hat / (jnp.sqrt(v_hat) + ADAM_EPS) + ADAM_WD * w)
    return delta, m, v


def _cast_into_slot(name, w, me_arr, deps=()):
    r, c = w.shape
    tr = 256

    def body(me_ref, w_ref, *rest):
        del me_ref
        rest[-1][0] = w_ref[...].astype(BF16)

    return pl.pallas_call(
        body, name=name,
        grid_spec=pltpu.PrefetchScalarGridSpec(
            num_scalar_prefetch=1, grid=(r // tr,),
            in_specs=[pl.BlockSpec((tr, c), lambda i, me: (i, 0))] + [ANY] * len(deps),
            out_specs=pl.BlockSpec((1, tr, c), lambda i, me: (me[0], i, 0))),
        out_shape=jax.ShapeDtypeStruct((N_DEV, r, c), BF16),
        compiler_params=_params(1),
    )(me_arr, w, *deps)


def _chip_sum(name, full, from_sibling, me_arr):
    _, r, c = full.shape
    tr = min(r, 1024)

    def body(me_ref, full_ref, sib_ref, sums_ref):
        del me_ref
        sums_ref[0] = (full_ref[0].astype(F32) + sib_ref[0].astype(F32)).astype(BF16)

    other = lambda k, me: (me[0] // 2 + 1 + k) % 4
    return pl.pallas_call(
        body, name=name,
        grid_spec=pltpu.PrefetchScalarGridSpec(
            num_scalar_prefetch=1, grid=(r // tr, 3),
            in_specs=[pl.BlockSpec((1, tr, c), lambda i, k, me: (2 * other(k, me) + me[0] % 2, i, 0)),
                      pl.BlockSpec((1, tr, c), lambda i, k, me: (other(k, me), i, 0))],
            out_specs=pl.BlockSpec((1, tr, c), lambda i, k, me: (other(k, me), i, 0))),
        out_shape=jax.ShapeDtypeStruct((4, r, c), BF16),
        compiler_params=_params(2),
    )(me_arr, full, from_sibling)


def _adamw_shard(name, w, m, v, parts, me_arr, deps=()):
    r, c = w.shape
    tr = min(256, r // len(parts))
    np_ = len(parts)
    per = r // np_ // tr

    def body(me_ref, w_ref, m_ref, v_ref, *rest):
        g_out, d_out, m_out, v_out = rest[5 * np_ + len(deps):]
        g = None
        for p in range(np_):
            gp = rest[5 * p][...]
            for l_ref in rest[5 * p + 1:5 * p + 5]:
                gp = gp + l_ref[0].astype(F32)
            g = gp if g is None else jnp.where(pl.program_id(0) // per == p, gp, g)
        delta, m_new, v_new = _adamw_math(w_ref[...], g, m_ref[...], v_ref[...])
        g_out[...] = g
        d_out[...] = delta
        m_out[...] = m_new
        v_out[...] = v_new

    tile = pl.BlockSpec((tr, c), lambda i, me: (i, 0))
    part_specs, part_args = [], []
    for p, (g_own, from_sibling, landed) in enumerate(parts):
        row = lambda i, p=p: jnp.clip(i - p * per, 0, per - 1)
        part_specs.append(pl.BlockSpec((tr, c), lambda i, me, row=row: (row(i), 0)))
        part_specs += [pl.BlockSpec((1, tr, c), lambda i, me, k=k, row=row: ((me[0] // 2 + k) % 4, row(i), 0))
                       for k in range(4)]
        part_args += [g_own, from_sibling, landed, landed, landed]
    return pl.pallas_call(
        body, name=name,
        grid_spec=pltpu.PrefetchScalarGridSpec(
            num_scalar_prefetch=1, grid=(r // tr,),
            in_specs=[tile] * 3 + part_specs + [ANY] * len(deps), out_specs=[tile] * 4),
        out_shape=[jax.ShapeDtypeStruct((r, c), F32)] * 4,
        compiler_params=_params(1),
    )(me_arr, w, m, v, *part_args, *deps)


SMALL_W = 1024
VEC_ROWS = 16
LOSS_ROW = 15
META_ROW0 = 16
CONF_ROW0 = 64
SHORT_ROW0 = 96
SMALL_ROWS = 104


def _pack_small(vec_parts, dmeta, dcw, dsw, loss_blk, me_arr):
    widths = [p.shape[1] for p in vec_parts]
    nv = len(vec_parts)

    def body(me_ref, *refs):
        del me_ref
        parts, (dmeta_ref, dcw_ref, dsw_ref, loss_ref, out_ref) = refs[:nv], refs[nv:]
        out_ref[0] = jnp.zeros((SMALL_ROWS, SMALL_W), F32)
        out_ref[0, LOSS_ROW:LOSS_ROW + 1, 0:LANE] = loss_ref[0:1, :]
        row = 0
        for p_ref, wd in zip(parts, widths):
            s = jnp.sum(p_ref[...], axis=0, keepdims=True)
            for h in range(wd // SMALL_W):
                out_ref[0, row:row + 1, :] = s[:, h * SMALL_W:(h + 1) * SMALL_W]
                row += 1
        for h in range(dmeta_ref.shape[1] // SMALL_W):
            out_ref[0, META_ROW0 + h * N_META:META_ROW0 + (h + 1) * N_META, :] = dmeta_ref[:, h * SMALL_W:(h + 1) * SMALL_W]
        for k in range(CONF_K):
            out_ref[0, CONF_ROW0 + k:CONF_ROW0 + k + 1, :] = jnp.sum(dcw_ref[k * SUB:(k + 1) * SUB, :], axis=0, keepdims=True)
        for k in range(SHORT_K):
            out_ref[0, SHORT_ROW0 + k:SHORT_ROW0 + k + 1, :] = jnp.sum(dsw_ref[k * SUB:(k + 1) * SUB, :], axis=0, keepdims=True)

    ins = [*vec_parts, dmeta, dcw, dsw, loss_blk]
    return pl.pallas_call(
        body, name="pack_small",
        grid_spec=pltpu.PrefetchScalarGridSpec(
            num_scalar_prefetch=1, grid=(1,),
            in_specs=[pl.BlockSpec(a.shape, lambda i, me: (0, 0)) for a in ins],
            out_specs=pl.BlockSpec((1, SMALL_ROWS, SMALL_W), lambda i, me: (me[0], 0, 0))),
        out_shape=jax.ShapeDtypeStruct((N_DEV, SMALL_ROWS, SMALL_W), F32),
        compiler_params=_params(1),
    )(me_arr, *ins)


def _small_update(gathered, me_arr, vec_params, meta_p, conf_p, short_p):
    widths = [p[0].shape[1] for p in vec_params]
    nv = len(vec_params)
    mcols = meta_p[0].shape[1]
    per_row = SMALL_W // mcols

    def body(me_ref, gv_ref, gm_ref, gc_ref, gs_ref, *rest):
        del me_ref
        ins, outs = rest[:3 * (nv + 3)], rest[3 * (nv + 3):]

        def total(ref, r0, rows):
            s = ref[0, r0:r0 + rows, :]
            for dev in range(1, N_DEV):
                s = s + ref[dev, r0:r0 + rows, :]
            return s

        grads = []
        row = 0
        for wd in widths:
            pieces = [total(gv_ref, row + h, 1) for h in range(wd // SMALL_W)]
            grads.append(pieces[0] if len(pieces) == 1 else jnp.concatenate(pieces, axis=1))
            row += len(pieces)
        grads.append(total(gm_ref, 0, N_META))
        grads.append(total(gc_ref, 0, CONF_K))
        grads.append(total(gs_ref, 0, SHORT_K))
        loss = gv_ref[0, LOSS_ROW:LOSS_ROW + 1, 0:LANE]
        for dev in range(1, N_DEV):
            loss = loss + gv_ref[dev, LOSS_ROW:LOSS_ROW + 1, 0:LANE]
        outs[-1][...] = loss
        for idx, g in enumerate(grads):
            w_ref, m_ref, v_ref = ins[3 * idx:3 * idx + 3]
            delta, m_new, v_new = _adamw_math(w_ref[...], g, m_ref[...], v_ref[...])
            g_out, d_out, m_out, v_out = outs[4 * idx:4 * idx + 4]
            g_out[...] = g
            d_out[...] = delta
            m_out[...] = m_new
            v_out[...] = v_new

    params = list(vec_params) + [meta_p, conf_p, short_p]
    flat = [a for p in params for a in p]
    whole = lambda a: pl.BlockSpec(a.shape, lambda i, me: (0,) * a.ndim)
    outs = pl.pallas_call(
        body, name="small_update",
        grid_spec=pltpu.PrefetchScalarGridSpec(
            num_scalar_prefetch=1, grid=(1,),
            in_specs=[pl.BlockSpec((N_DEV, VEC_ROWS, SMALL_W), lambda i, me: (0, 0, 0)),
                      pl.BlockSpec((N_DEV, N_META, mcols),
                                   lambda i, me: (0, META_ROW0 // N_META + me[0] // per_row, me[0] % per_row)),
                      pl.BlockSpec((N_DEV, 32, LANE), lambda i, me: (0, CONF_ROW0 // 32, me[0])),
                      pl.BlockSpec((N_DEV, SUB, LANE), lambda i, me: (0, SHORT_ROW0 // SUB, me[0]))]
                     + [whole(a) for a in flat],
            out_specs=[whole(p[0]) for p in params for _ in range(4)]
                      + [pl.BlockSpec((1, LANE), lambda i, me: (0, 0))]),
        out_shape=[jax.ShapeDtypeStruct(p[0].shape, F32) for p in params for _ in range(4)]
                  + [jax.ShapeDtypeStruct((1, LANE), F32)],
        compiler_params=_params(1),
    )(me_arr, gathered, gathered, gathered, gathered, *flat)
    return [tuple(outs[4 * i:4 * i + 4]) for i in range(len(params))], outs[-1][0, 0]


def kernel(x, meta, g_pre_mix, w_in, b_gates, conf_dw_w, conf_dw_b, conf_ln_g, conf_ln_b, conf_w_pw, short_dw_w, short_w_out, w_o, g_post_mix, g_pre_mlp, w_up, w_down, g_post_mlp, loss_target, m_meta, m_g_pre_mix, m_w_in, m_b_gates, m_conf_dw_w, m_conf_dw_b, m_conf_ln_g, m_conf_ln_b, m_conf_w_pw, m_short_dw_w, m_short_w_out, m_w_o, m_g_post_mix, m_g_pre_mlp, m_w_up, m_w_down, m_g_post_mlp, v_meta, v_g_pre_mix, v_w_in, v_b_gates, v_conf_dw_w, v_conf_dw_b, v_conf_ln_g, v_conf_ln_b, v_conf_w_pw, v_short_dw_w, v_short_w_out, v_w_o, v_g_post_mix, v_g_pre_mlp, v_w_up, v_w_down, v_g_post_mlp):
    seq, d = x.shape[1], x.shape[2]
    dc = conf_w_pw.shape[1]
    t_real = N_META + seq
    t = -(-t_real // ROW_TILE) * ROW_TILE
    tm = t // 2
    assert tm % 16 == 0 and d % 1024 == 0 and dc % 1024 == 0
    x_idx, y_idx, c_idx = _position()
    me_arr = jnp.reshape(4 * x_idx + 2 * y_idx + c_idx, (1,)).astype(jnp.int32)

    big = [w_in[0], conf_w_pw[0], short_w_out[0], w_o[0], w_up[0], w_down[0]]
    big_names = ["w_in", "conf_w_pw", "short_w_out", "w_o", "w_up", "w_down"]
    groups = [[0], [1, 2, 3], [4], [5]]
    slots, deps = [], []
    for g, idxs in enumerate(groups):
        slots.append([_cast_into_slot("cast_" + big_names[i], big[i], me_arr, deps=deps) for i in idxs])
        if g == 0:
            direct0 = _remote_start("gather0_direct_start", "gather_direct", slots[0])
            deps = [direct0[3]]
    casts = [sl for group in slots[1:] for sl in group]
    meta_g, cw_g, sw_g = _all_gather("gather_small_params", [meta, conf_dw_w[0], short_dw_w[0]], deps=casts)

    def start_direct(g, deps):
        send, recv, bufs, tok = _remote_start("gather%d_direct_start" % g, "gather_direct", slots[g], deps=deps)
        return (send, recv, bufs), tok

    def relay(g, state, after):
        send, recv, bufs, tok = _remote_pass_on("gather%d_relay" % g, "gather_direct", *state, after, "gather_relay")
        return (send, recv, bufs), tok

    def gathered(g, state, after):
        send, recv, bufs, tok = _remote_pass_on("gather%d_diag" % g, "gather_relay", *state, after, "gather_diag")
        return _remote_wait("gather%d_diag_wait" % g, "gather_diag", send, recv, bufs, len(bufs), [tok])

    unshard =lambda g: jnp.transpose(g, (1, 0, 2)).reshape(g.shape[1], -1)
    meta_full, cw_full, sw_full = unshard(meta_g), unshard(cw_g), unshard(sw_g)

    relay0, tok = relay(0, direct0[:3], [meta_g])
    zrows = jnp.zeros((t - t_real, d), F32) + tok[0, 0] * 0.0
    h0 = jnp.concatenate([meta_full, x[0], zrows], axis=0)
    tgt = jnp.concatenate([jnp.zeros((N_META, d), F32), loss_target[0], zrows], axis=0)
    n = _pre_norm(h0, g_pre_mix)
    direct1, tok = start_direct(1, [tok])
    direct2, tok = start_direct(2, [tok])
    win_g, = gathered(0, relay0, [tok, n])
    proj = _mm_cols_pairs("proj", n, win_g, tm=tm // 2)
    relay1, tok = relay(1, direct1, [proj])
    a1, s = _conv_forward(proj, cw_full, conf_dw_b, sw_full, dc, deps=[tok])
    relay2, tok = relay(2, direct2, [a1])
    direct3, tok = start_direct(3, [tok])
    a3 = _layer_norm_silu(a1, conf_ln_g, conf_ln_b, deps=[tok])
    wpw_g, wso_g, wo_g = gathered(1, relay1, [a3])
    wo_full = wo_g.reshape(d, d)
    ya, yb, gate_a, gate_b, m_mix = _branch_merge(a3, s, wpw_g, wso_g, proj, b_gates, d)
    mix, h1, n2 = _mix_post(m_mix, wo_full, h0, g_post_mix, g_pre_mlp)
    wup_g, = gathered(2, relay2, [n2])

    def up_epilogue(acc):
        r = jnp.maximum(acc, 0.0)
        return r * r, r

    half_up = dict(tm=tm, epilogue=up_epilogue, out_dtypes=(BF16, BF16))
    f, relu_up = _mm_cols("mlp_up0", n2, wup_g, blocks=(0, N_DEV // 2), **half_up)
    relay3, tok = relay(3, direct3, [f])
    f, relu_up = _mm_cols("mlp_up1", n2, wup_g, blocks=(N_DEV // 2, N_DEV), into=(f, relu_up), deps=[tok], **half_up)
    wdn_g, = gathered(3, relay3, [f])
    wdn_full = wdn_g.reshape(-1, d)
    fo = _mm_rows("mlp_down", f, wdn_full, tm=tm // 2, tn=512)
    dfo, dh2, dg_post_mlp, loss_blk = _loss_head(fo, h1, tgt, g_post_mlp, t_real)

    def reduce_start(tag, fulls, deps):
        lands = [lax.empty((4,) + g.shape[1:], BF16) for g in fulls]
        send, recv, bufs, tok = _remote_start("reduce_%s_d2d_start" % tag, "reduce_d2d", fulls, lands, deps=deps)
        return (send, recv, bufs), tok

    def reduce_middle(tag, state, owns, after):
        send, recv, bufs = state
        k = len(owns)
        bufs = _remote_wait("reduce_%s_d2d_wait" % tag, "reduce_d2d", send, recv, bufs, k, after)
        from_sibling = bufs[k:]
        sums = [_chip_sum("chip_sum_%s%d" % (tag, i), bufs[i], from_sibling[i], me_arr) for i in range(k)]
        lands = [lax.empty(sm.shape, BF16) for sm in sums]
        send, recv, bufs, tok = _remote_start("reduce_%s_ici_start" % tag, "reduce_ici", sums, lands)
        return (send, recv, bufs, list(zip(owns, from_sibling))), tok

    def reduce_finish(tag, state, after):
        send, recv, bufs, local = state
        k = len(local)
        bufs = _remote_wait("reduce_%s_ici_wait" % tag, "reduce_ici", send, recv, bufs, k, after)
        return [(own, sib, landed) for (own, sib), landed in zip(local, bufs[k:])]

    dup = _mm_nt_blocks("d_up", dfo, wdn_full, tm=tm, tkb=1024, extra=(relu_up,),
                        epilogue=lambda acc, r: (acc * (2.0 * r.astype(F32)),), out_dtypes=(BF16,))[0]
    gw_down, gw_down_own = _mm_tn("dw_down", f, dfo, me_arr, m=f.shape[1], n=d, tma=512, tn=d, sharded="rows")
    red_down, tok = reduce_start("down", [gw_down], ())
    dn2 = _mm_nt_acc("d_n2", dup, wup_g, tm=tm // 2, tn=512, deps=[tok])
    gw_up, gw_up_own = _mm_tn("dw_up", n2, dup, me_arr, m=d, n=dup.shape[1], tma=512, tn=2048, sharded="cols")
    red_down, tok = reduce_middle("down", red_down, [gw_down_own], [dn2])
    red_up, tok = reduce_start("up", [gw_up], [tok])
    dh1, dmix, dg_pre_mlp, dg_post_mix = _mid_norm_bwd(dn2, h1, dh2, mix, g_pre_mlp, g_post_mix, deps=[tok])
    dya, dyb, dproj, db_a, db_b = _gate_backward(dmix, wo_full, gate_a, gate_b, ya, yb, proj.shape[1], tm // 2)
    db_gates = jnp.concatenate([db_a, db_b], axis=1)
    red_up, tok = reduce_middle("up", red_up, [gw_up_own], [dya])
    gw_o, gw_o_own = _mm_tn("dw_o", m_mix, dmix, me_arr, m=d, n=d, tma=d // N_DEV, tn=d, sharded="rows", deps=[tok])
    da3 = _mm_nt_acc("d_a3", dya, wpw_g, tm=tm, tn=512)
    gw_pw, gw_pw_own = _mm_tn("dw_pw", a3, dya, me_arr, m=dc, n=d, tma=512, tn=d, sharded="cols")
    dsb = _mm_nt_acc("d_s", dyb, wso_g, tm=tm, tn=512)
    gw_so, gw_so_own = _mm_tn("dw_so", s, dyb, me_arr, m=dc, n=d, tma=512, tn=d, sharded="cols")
    red_mix, tok = reduce_start("mix", [gw_pw, gw_so, gw_o], ())
    da1, dln_g, dln_b = _layer_norm_silu_bwd(da3, a1, conf_ln_g, conf_ln_b, deps=[tok])
    dproj, dcw, dcb, dsw = _conv_backward(dproj, proj, da1, dsb, cw_full, sw_full, dc)
    red_mix, tok = reduce_middle("mix", red_mix, [gw_pw_own, gw_so_own, gw_o_own], [dcb])
    in_cb = w_in.shape[2]
    half = d // 2
    red_in = []
    for part in range(2):
        gw, own = _mm_tn("dw_in%d" % part, n, dproj, me_arr, m=half, n=proj.shape[1], tma=512, tn=2 * in_cb,
                         sharded="cols", a_off=part * (half // 512), deps=[tok])
        state, tok = reduce_start("in%d" % part, [gw], ())
        red_in.append((state, own))
    for part in range(2):
        state, own = red_in[part]
        red_in[part], tok = reduce_middle("in%d" % part, state, [own], [tok])
    dn = _mm_nt_acc("d_n", dproj, win_g, tm=tm // 2, tn=512, deps=[tok])
    dh0, dg_pre_mix = _pre_norm_bwd(dn, h0, dh1, g_pre_mix)
    grad_x = dh0[N_META:t_real][None]

    vec_parts = [dg_pre_mix, db_gates, dcb, dln_g, dln_b, dg_post_mix, dg_pre_mlp, dg_post_mlp]
    packed = _pack_small(vec_parts, dh0[:N_META], dcw, dsw, loss_blk, me_arr)
    send, recv, bufs, tok = _remote_start("small_grads_ici_start", "gather_ici", [packed])
    vec_names = ["g_pre_mix", "b_gates", "conf_dw_b", "conf_ln_g", "conf_ln_b", "g_post_mix", "g_pre_mlp", "g_post_mlp"]
    env = locals()
    results = {}

    def update(nm, parts, deps=()):
        res = _adamw_shard("adamw_" + nm, env[nm][0], env["m_" + nm][0], env["v_" + nm][0], parts, me_arr, deps=deps)
        results[nm] = tuple(r[None] for r in res)
        return res[0]

    done = [update("w_down", reduce_finish("down", red_down, [tok]), deps=[tok])]
    done.append(update("w_up", reduce_finish("up", red_up, done)))
    bufs = _remote_wait("small_grads_ici_wait", "gather_ici", send, recv, bufs, 1, done)
    send, recv, bufs, tok = _remote_start("small_grads_d2d_start", "gather_d2d", bufs)
    for nm, pair in zip(["conf_w_pw", "short_w_out", "w_o"], reduce_finish("mix", red_mix, [tok])):
        done.append(update(nm, [pair], deps=[tok]))
    small_g, = _remote_wait("small_grads_d2d_wait", "gather_d2d", send, recv, bufs, 1, done)
    triple = lambda nm, sq: tuple(env[p + nm][0] if sq else env[p + nm] for p in ("", "m_", "v_"))
    small, loss = _small_update(small_g, me_arr, [triple(nm, False) for nm in vec_names],
                                triple("meta", False), triple("conf_dw_w", True), triple("short_dw_w", True))
    for nm, res in zip(vec_names + ["meta"], small[:len(vec_names) + 1]):
        results[nm] = res
    results["conf_dw_w"] = tuple(r[None] for r in small[-2])
    results["short_dw_w"] = tuple(r[None] for r in small[-1])
    update("w_in", [reduce_finish("in%d" % part, red_in[part], [small[0][0]])[0] for part in range(2)])

    order = ["meta", "g_pre_mix", "w_in", "b_gates", "conf_dw_w", "conf_dw_b", "conf_ln_g", "conf_ln_b", "conf_w_pw",
             "short_dw_w", "short_w_out", "w_o", "g_post_mix", "g_pre_mlp", "w_up", "w_down", "g_post_mlp"]
    return (loss, grad_x, *[results[nm][0] for nm in order], *[results[nm][1] for nm in order],
            *[results[nm][2] for nm in order], *[results[nm][3] for nm in order])
```

```python
import jax
import jax.numpy as jnp
from jax import lax
from jax.experimental import pallas as pl
from jax.experimental.pallas import tpu as pltpu

N_DEV = 8
N_META = 16
CONF_K = 31
SHORT_K = 3
RMS_EPS = 1e-6
LN_EPS = 1e-5
ADAM_LR = 0.001
ADAM_B1 = 0.9
ADAM_B2 = 0.999
ADAM_EPS = 1e-08
ADAM_WD = 0.01
ADAM_STEP = 10

LANE = 128
SUB = 8
ROW_TILE = 128
CONV_PAD = 32
CONV_CHUNK = 128
VMEM_LIMIT = 56 * 1024 * 1024

F32 = jnp.float32
BF16 = jnp.bfloat16
MESH = pl.DeviceIdType.MESH
ANY = pl.BlockSpec(memory_space=pl.ANY)
HBM_SPEC = pl.BlockSpec(memory_space=pltpu.HBM)
SEM_SPEC = pl.BlockSpec(memory_space=pltpu.SEMAPHORE)
EFFECT = pltpu.SideEffectType.DATAFLOW_SIDE_EFFECTING


def _params(n_axes):
    return pltpu.CompilerParams(dimension_semantics=("arbitrary",) * n_axes, vmem_limit_bytes=VMEM_LIMIT)


def _sigmoid(z):
    return 1.0 / (1.0 + jnp.exp(-z))


def _colsum8(v):
    r, c = v.shape
    return jnp.sum(v.reshape(r // SUB, SUB, c), axis=0)


def _position():
    x, y, c = lax.axis_index("x"), lax.axis_index("y"), lax.axis_index("c")
    return x, y, c


def _flat(p):
    return 4 * p[0] + 2 * p[1] + p[2]


def _all_gather(name, shards, deps=()):
    n, nd = len(shards), len(deps)

    def body(*refs):
        ins, outs = refs[:n], refs[n + nd:2 * n + nd]
        send_sems, recv_sems, local_sems = refs[2 * n + nd:]
        x, y, c = _position()
        me, sibling = (x, y, c), (x, y, 1 - c)
        chips = [(1 - x, y), (x, 1 - y), (1 - x, 1 - y)]

        def copy(q, k, block, to, src=None):
            dst = outs[q].at[_flat(block)]
            return pltpu.make_async_remote_copy(
                src_ref=dst if src is None else src, dst_ref=dst,
                send_sem=send_sems.at[q, k], recv_sem=recv_sems.at[q, k],
                device_id=to, device_id_type=MESH)

        mine = [pltpu.make_async_copy(ins[q], outs[q].at[_flat(me)], local_sems.at[q]) for q in range(n)]
        for cp in mine:
            cp.start()
        first = []
        for q in range(n):
            first.append(copy(q, 0, me, sibling, src=ins[q]))
            for j, chip in enumerate(chips):
                first.append(copy(q, 1 + j, me, (*chip, c), src=ins[q]))
        for cp in first:
            cp.start()
        passed = []
        for q in range(n):
            for j, chip in enumerate(chips):
                copy(q, 1 + j, (*chip, c), me).wait_recv()
                fwd = copy(q, 4 + j, (*chip, c), sibling)
                fwd.start()
                passed.append(fwd)
        for q in range(n):
            copy(q, 0, sibling, me).wait_recv()
            for j, chip in enumerate(chips):
                copy(q, 4 + j, (*chip, 1 - c), me).wait_recv()
        for cp in first + passed:
            cp.wait_send()
        for cp in mine:
            cp.wait()

    return pl.pallas_call(
        body, name=name,
        in_specs=[ANY] * (n + nd), out_specs=[ANY] * n,
        out_shape=[jax.ShapeDtypeStruct((N_DEV,) + s.shape, s.dtype) for s in shards],
        scratch_shapes=[pltpu.SemaphoreType.DMA((n, 7)), pltpu.SemaphoreType.DMA((n, 7)),
                        pltpu.SemaphoreType.DMA((n,))],
    )(*shards, *deps)


N_COPIES = {"gather_ici": 4, "gather_d2d": 3, "gather_direct": 3, "gather_relay": 3, "gather_diag": 1,
            "reduce_d2d": 4, "reduce_ici": 3}


def _copy_plan(kind):
    x, y, c = _position()
    me, sibling = (x, y, c), (x, y, 1 - c)
    chips = [(1 - x, y), (x, 1 - y), (1 - x, 1 - y)]
    if kind == "gather_ici":
        return [(_flat(me), _flat(me), sibling)] + [(_flat(me), _flat(me), (*ch, c)) for ch in chips]
    if kind == "gather_d2d":
        return [(_flat((*ch, c)), _flat((*ch, c)), sibling) for ch in chips]
    if kind == "gather_direct":
        return [(_flat(me), _flat(me), sibling)] + [(_flat(me), _flat(me), (*ch, c)) for ch in chips[:2]]
    if kind == "gather_relay":
        held, to = (x ^ (1 - c), y ^ c, c), (x ^ c, y ^ (1 - c), c)
        return [(_flat(held), _flat(held), to)] + [(_flat((*ch, c)), _flat((*ch, c)), sibling) for ch in chips[:2]]
    if kind == "gather_diag":
        return [(_flat((*chips[2], c)), _flat((*chips[2], c)), sibling)]
    if kind == "reduce_d2d":
        return [(2 * chip + (1 - c), chip, sibling) for chip in range(4)]
    return [(2 * ch[0] + ch[1], 2 * x + y, (*ch, c)) for ch in chips]


def _planned_copies(kind, srcs, dsts, send_sems, recv_sems):
    plan = _copy_plan(kind)
    return [pltpu.make_async_remote_copy(
        src_ref=src.at[s_slot], dst_ref=dst.at[d_slot],
        send_sem=send_sems.at[q * len(plan) + k], recv_sem=recv_sems.at[q * len(plan) + k],
        device_id=to, device_id_type=MESH)
        for q, (src, dst) in enumerate(zip(srcs, dsts)) for k, (s_slot, d_slot, to) in enumerate(plan)]


def _remote_start(name, kind, srcs, lands=None, deps=()):
    n = len(srcs)
    bufs = list(srcs) + ([] if lands is None else list(lands))
    nb, nd = len(bufs), len(deps)
    nsem = n * N_COPIES[kind]

    def body(*refs):
        ins = refs[:nb]
        send_sems, recv_sems = refs[nb + nd], refs[nb + nd + 1]
        token = refs[-1]
        for cp in _planned_copies(kind, ins[:n], ins[:n] if lands is None else ins[n:], send_sems, recv_sems):
            cp.start()
        token[...] = jnp.zeros_like(token)

    outs = pl.pallas_call(
        body, name=name,
        out_shape=(pltpu.SemaphoreType.DMA((nsem,)), pltpu.SemaphoreType.DMA((nsem,)),
                   *[pltpu.HBM(b.shape, b.dtype) for b in bufs], jax.ShapeDtypeStruct((SUB, LANE), F32)),
        in_specs=[HBM_SPEC] * nb + [ANY] * nd,
        out_specs=(SEM_SPEC, SEM_SPEC, *[HBM_SPEC] * nb, pl.BlockSpec(memory_space=pltpu.VMEM)),
        input_output_aliases={i: 2 + i for i in range(nb)},
        compiler_params=pltpu.CompilerParams(has_side_effects=EFFECT),
    )(*[pltpu.with_memory_space_constraint(b, pltpu.HBM) for b in bufs], *deps)
    return outs[0], outs[1], list(outs[2:2 + nb]), outs[-1]


def _remote_wait(name, kind, send_sems, recv_sems, bufs, n, after):
    nb, na = len(bufs), len(after)
    same = nb == n

    def body(*refs):
        ins = refs[:nb]
        sends, recvs = refs[nb], refs[nb + 1]
        for cp in _planned_copies(kind, ins[:n], ins[:n] if same else ins[n:], sends, recvs):
            cp.wait_send()
            cp.wait_recv()

    outs = pl.pallas_call(
        body, name=name,
        out_shape=[pltpu.HBM(b.shape, b.dtype) for b in bufs],
        in_specs=[HBM_SPEC] * nb + [SEM_SPEC, SEM_SPEC] + [ANY] * na,
        out_specs=[HBM_SPEC] * nb,
        input_output_aliases={i: i for i in range(nb)},
        compiler_params=pltpu.CompilerParams(has_side_effects=EFFECT),
    )(*bufs, send_sems, recv_sems, *after)
    return list(outs)


def _remote_pass_on(name, done, send_sems, recv_sems, bufs, after, nxt):
    nb, na = len(bufs), len(after)
    nsem = nb * N_COPIES[nxt]

    def body(*refs):
        ins = refs[:nb]
        new_sends, new_recvs = refs[nb + 2 + na], refs[nb + 3 + na]
        token = refs[-1]
        for cp in _planned_copies(done, ins, ins, refs[nb], refs[nb + 1]):
            cp.wait_send()
            cp.wait_recv()
        for cp in _planned_copies(nxt, ins, ins, new_sends, new_recvs):
            cp.start()
        token[...] = jnp.zeros_like(token)

    outs = pl.pallas_call(
        body, name=name,
        out_shape=(pltpu.SemaphoreType.DMA((nsem,)), pltpu.SemaphoreType.DMA((nsem,)),
                   *[pltpu.HBM(b.shape, b.dtype) for b in bufs], jax.ShapeDtypeStruct((SUB, LANE), F32)),
        in_specs=[HBM_SPEC] * nb + [SEM_SPEC, SEM_SPEC] + [ANY] * na,
        out_specs=(SEM_SPEC, SEM_SPEC, *[HBM_SPEC] * nb, pl.BlockSpec(memory_space=pltpu.VMEM)),
        input_output_aliases={i: 2 + i for i in range(nb)},
        compiler_params=pltpu.CompilerParams(has_side_effects=EFFECT),
    )(*bufs, send_sems, recv_sems, *after)
    return outs[0], outs[1], list(outs[2:2 + nb]), outs[-1]


def _mm_cols(name, a, w, *, tm, blocks, epilogue, out_dtypes, into=(), deps=()):
    t, k = a.shape
    nblk, _, cb = w.shape
    j0, j1 = blocks
    no = len(out_dtypes)

    def body(a_ref, w_ref, *rest):
        acc = jnp.dot(a_ref[...], w_ref[0], preferred_element_type=F32)
        for o_ref, o in zip(rest[len(into) + len(deps):], epilogue(acc)):
            o_ref[...] = o.astype(o_ref.dtype)

    return pl.pallas_call(
        body, name=name, grid=(j1 - j0, t // tm),
        in_specs=[pl.BlockSpec((tm, k), lambda j, i: (i, 0)),
                  pl.BlockSpec((1, k, cb), lambda j, i: (j0 + j, 0, 0))] + [ANY] * (len(into) + len(deps)),
        out_specs=[pl.BlockSpec((tm, cb), lambda j, i: (i, j0 + j)) for _ in range(no)],
        out_shape=[jax.ShapeDtypeStruct((t, nblk * cb), dt) for dt in out_dtypes],
        input_output_aliases={2 + idx: idx for idx in range(len(into))},
        compiler_params=_params(2),
    )(a, w, *into, *deps)


MXU_WIDTH = 256


def _mm_cols_pairs(name, a, w, *, tm):
    t, k = a.shape
    nblk, _, cb = w.shape
    main = cb // MXU_WIDTH * MXU_WIDTH
    tail = cb - main
    assert 2 * tail == MXU_WIDTH and nblk % 2 == 0

    def body(a_ref, w_ref, o_ref):
        av = a_ref[...]
        for b in range(2):
            o_ref[:, b * cb:b * cb + main] = jnp.dot(av, w_ref[b, :, 0:main], preferred_element_type=F32)
        tails = jnp.dot(av, jnp.concatenate([w_ref[0, :, main:cb], w_ref[1, :, main:cb]], axis=1),
                        preferred_element_type=F32)
        for b in range(2):
            o_ref[:, b * cb + main:(b + 1) * cb] = tails[:, b * tail:(b + 1) * tail]

    return pl.pallas_call(
        body, name=name, grid=(nblk // 2, t // tm),
        in_specs=[pl.BlockSpec((tm, k), lambda j, i: (i, 0)),
                  pl.BlockSpec((2, k, cb), lambda j, i: (j, 0, 0))],
        out_specs=pl.BlockSpec((tm, 2 * cb), lambda j, i: (i, j)),
        out_shape=jax.ShapeDtypeStruct((t, nblk * cb), F32),
        compiler_params=_params(2),
    )(a, w)


def _add_columns(o_ref, j, tn, acc, first):
    for jj in range(o_ref.shape[1] // tn):
        cols = slice(jj * tn, (jj + 1) * tn)

        @pl.when(jnp.logical_and(j == jj, first))
        def _(cols=cols):
            o_ref[:, cols] = acc

        @pl.when(jnp.logical_and(j == jj, jnp.logical_not(first)))
        def _(cols=cols):
            o_ref[:, cols] += acc


def _mm_rows(name, a, w2d, *, tm, tn, kparts=2):
    t, kf = a.shape
    n = w2d.shape[1]
    kp = kf // kparts

    def body(a_ref, w_ref, o_ref):
        acc = jnp.dot(a_ref[...], w_ref[...], preferred_element_type=F32)
        _add_columns(o_ref, pl.program_id(2), tn, acc, pl.program_id(1) == 0)

    return pl.pallas_call(
        body, name=name, grid=(t // tm, kparts, n // tn),
        in_specs=[pl.BlockSpec((tm, kp), lambda i, kh, j: (i, kh)),
                  pl.BlockSpec((kp, tn), lambda i, kh, j: (kh, j))],
        out_specs=pl.BlockSpec((tm, n), lambda i, kh, j: (i, 0)),
        out_shape=jax.ShapeDtypeStruct((t, n), F32),
        compiler_params=_params(3),
    )(a, w2d)


def _mm_nt_acc_parts(name, dy, w, *, tm, tn, kparts=2, deps=()):
    t = dy.shape[0]
    nblk, k, cb = w.shape
    per = nblk // kparts
    assert cb % MXU_WIDTH == 0

    def body(dy_ref, w_ref, *rest):
        acc = None
        for b in range(per):
            d = lax.dot_general(dy_ref[:, b * cb:(b + 1) * cb], w_ref[b], (((1,), (1,)), ((), ())),
                                preferred_element_type=F32)
            acc = d if acc is None else acc + d
        _add_columns(rest[-1], pl.program_id(2), tn, acc, pl.program_id(1) == 0)

    return pl.pallas_call(
        body, name=name, grid=(t // tm, kparts, k // tn),
        in_specs=[pl.BlockSpec((tm, per * cb), lambda i, kh, j: (i, kh)),
                  pl.BlockSpec((per, tn, cb), lambda i, kh, j: (kh, j, 0))] + [ANY] * len(deps),
        out_specs=pl.BlockSpec((tm, k), lambda i, kh, j: (i, 0)),
        out_shape=jax.ShapeDtypeStruct((t, k), F32),
        compiler_params=_params(3),
    )(dy, w, *deps)


def _mm_nt_acc(name, dy, w, *, tm, tn, col_off=0, deps=()):
    t = dy.shape[0]
    nblk, k, cb = w.shape

    main = cb // MXU_WIDTH * MXU_WIDTH

    def body(dy_ref, w_ref, *rest):
        nt = (((1,), (1,)), ((), ()))
        acc = None
        for b in range(nblk):
            d = lax.dot_general(dy_ref[:, b * cb:b * cb + main], w_ref[b, :, 0:main], nt, preferred_element_type=F32)
            acc = d if acc is None else acc + d
        if main < cb:
            dy_tails = jnp.concatenate([dy_ref[:, b * cb + main:(b + 1) * cb] for b in range(nblk)], axis=1)
            w_tails = jnp.concatenate([w_ref[b, :, main:cb] for b in range(nblk)], axis=1)
            acc = acc + lax.dot_general(dy_tails, w_tails, nt, preferred_element_type=F32)
        rest[-1][...] = acc

    return pl.pallas_call(
        body, name=name, grid=(t // tm, k // tn),
        in_specs=[pl.BlockSpec((tm, nblk * cb), lambda i, j: (i, col_off)),
                  pl.BlockSpec((nblk, tn, cb), lambda i, j: (0, j, 0))] + [ANY] * len(deps),
        out_specs=pl.BlockSpec((tm, tn), lambda i, j: (i, j)),
        out_shape=jax.ShapeDtypeStruct((t, k), F32),
        compiler_params=_params(2),
    )(dy, w, *deps)


def _mm_nt_blocks(name, dy, w2d, *, tm, tkb, extra=(), epilogue=None, out_dtypes=(F32,)):
    t, n = dy.shape
    kf = w2d.shape[0]
    ne = len(extra)

    def body(dy_ref, w_ref, *rest):
        acc = lax.dot_general(dy_ref[...], w_ref[...], (((1,), (1,)), ((), ())), preferred_element_type=F32)
        outs = (acc,) if epilogue is None else epilogue(acc, *[e[...] for e in rest[:ne]])
        for o_ref, o in zip(rest[ne:], outs):
            o_ref[...] = o.astype(o_ref.dtype)

    return pl.pallas_call(
        body, name=name, grid=(kf // tkb, t // tm),
        in_specs=[pl.BlockSpec((tm, n), lambda kb, i: (i, 0)),
                  pl.BlockSpec((tkb, n), lambda kb, i: (kb, 0))]
                 + [pl.BlockSpec((tm, tkb), lambda kb, i: (i, kb)) for _ in extra],
        out_specs=[pl.BlockSpec((tm, tkb), lambda kb, i: (i, kb)) for _ in out_dtypes],
        out_shape=[jax.ShapeDtypeStruct((t, kf), dt) for dt in out_dtypes],
        compiler_params=_params(2),
    )(dy, w2d, *extra)


def _mm_tn(name, a, b, me_arr, *, m, n, tma, tn, sharded, a_off=0, b_off=0, deps=()):
    t = a.shape[0]
    if sharded == "cols":
        cb = n // N_DEV
        nb, q = max(tn // cb, 1), max(cb // tn, 1)
        tw = tn // nb
        full_shape, own_shape = (N_DEV, m, cb), (m, cb)
        full_spec = pl.BlockSpec((nb, tma, tw), lambda i, j, me: (j // q, i, j % q))
    else:
        kb = m // N_DEV
        p = kb // tma
        nb, tw = 1, tn
        full_shape, own_shape = (m, n), (kb, n)
        full_spec = pl.BlockSpec((tma, tn), lambda i, j, me: (i, j))

    def body(me_ref, a_ref, b_ref, *rest):
        full_ref, own_ref, stage, sem, pending = rest[len(deps):]
        i, j = pl.program_id(0), pl.program_id(1)

        def own_copy(r0, c0):
            return pltpu.make_async_copy(
                stage, own_ref.at[pl.ds(pl.multiple_of(r0, tma), tma), pl.ds(pl.multiple_of(c0, tw), tw)], sem)

        def drain():
            @pl.when(pending[0] == 1)
            def _():
                own_copy(0, 0).wait()
                pending[0] = 0

        @pl.when(jnp.logical_and(i == 0, j == 0))
        def _():
            pending[0] = 0

        acc = lax.dot_general(a_ref[...], b_ref[...], (((0,), (0,)), ((), ())), preferred_element_type=F32)
        for blk in range(nb):
            part = acc[:, blk * tw:(blk + 1) * tw]
            if sharded == "cols":
                full_ref[blk] = part.astype(BF16)
                owner, r0, c0 = (j // q) * nb + blk, i * tma, (j % q) * tw
            else:
                full_ref[...] = part.astype(BF16)
                owner, r0, c0 = i // p, (i % p) * tma, j * tn

            @pl.when(owner == me_ref[0])
            def _():
                drain()
                stage[...] = part
                own_copy(r0, c0).start()
                pending[0] = 1

        @pl.when(jnp.logical_and(i == pl.num_programs(0) - 1, j == pl.num_programs(1) - 1))
        def _():
            drain()

    full, own = pl.pallas_call(
        body, name=name,
        grid_spec=pltpu.PrefetchScalarGridSpec(
            num_scalar_prefetch=1, grid=(m // tma, n // tn),
            in_specs=[pl.BlockSpec((t, tma), lambda i, j, me: (0, a_off + i)),
                      pl.BlockSpec((t, tn), lambda i, j, me: (0, b_off + j))] + [ANY] * len(deps),
            out_specs=[full_spec, ANY],
            scratch_shapes=[pltpu.VMEM((tma, tw), F32), pltpu.SemaphoreType.DMA(()), pltpu.SMEM((1,), jnp.int32)]),
        out_shape=[jax.ShapeDtypeStruct(full_shape, BF16), jax.ShapeDtypeStruct(own_shape, F32)],
        compiler_params=_params(2),
    )(me_arr, a, b, *deps)
    if sharded == "rows":
        full = full.reshape(N_DEV, m // N_DEV, n)
    return full, own


def _row_tile(t):
    return t // 8 if (t // 8) % 16 == 0 else ROW_TILE


def _row_call(name, body, t, row_ins, full_ins, row_outs, acc_outs, scratch=(), deps=()):
    tm = _row_tile(t)
    nin = len(row_ins) + len(full_ins)

    def without_deps(*refs):
        body(*refs[:nin], *refs[nin + len(deps):])

    return pl.pallas_call(
        without_deps, name=name, grid=(t // tm,),
        in_specs=[pl.BlockSpec((tm, a.shape[1]), lambda i: (i, 0)) for a in row_ins]
                 + [pl.BlockSpec(a.shape, lambda i: (0, 0)) for a in full_ins] + [ANY] * len(deps),
        out_specs=[pl.BlockSpec((tm, c), lambda i: (i, 0)) for c, _ in row_outs]
                  + [pl.BlockSpec((r, c), lambda i: (0, 0)) for r, c in acc_outs],
        out_shape=[jax.ShapeDtypeStruct((t, c), dt) for c, dt in row_outs]
                  + [jax.ShapeDtypeStruct((r, c), F32) for r, c in acc_outs],
        scratch_shapes=list(scratch),
        compiler_params=_params(1),
    )(*row_ins, *full_ins, *deps)


def _accumulate(ref, v):
    @pl.when(pl.program_id(0) == 0)
    def _():
        ref[...] = v

    @pl.when(pl.program_id(0) > 0)
    def _():
        ref[...] += v


def _rms(v):
    return lax.rsqrt(jnp.mean(v * v, axis=-1, keepdims=True) + RMS_EPS)


def _rms_bwd(dout, u, r, g):
    du = dout * g
    dx = r * (du - u * jnp.mean(du * u, axis=-1, keepdims=True))
    return dx, _colsum8(dout * u)


def _pre_norm(h0, g):
    t, d = h0.shape

    def body(h_ref, g_ref, n_ref):
        h = h_ref[...]
        n_ref[...] = (h * _rms(h) * g_ref[...]).astype(BF16)

    return _row_call("pre_norm", body, t, [h0], [g], [(d, BF16)], [])[0]


def _mix_post(m_mix, wo_full, h0, g_post, g_pre, deps=()):
    t, d = h0.shape
    tm = _row_tile(t)

    def body(m_ref, wo_ref, h0_ref, gp_ref, gq_ref, *rest):
        mix_ref, h1_ref, n2_ref = rest[len(deps):]
        mix_v = jnp.dot(m_ref[...], wo_ref[...], preferred_element_type=F32)
        mix_ref[...] = mix_v
        h1 = h0_ref[...] + mix_v * _rms(mix_v) * gp_ref[...]
        h1_ref[...] = h1
        n2_ref[...] = (h1 * _rms(h1) * gq_ref[...]).astype(BF16)

    tile = pl.BlockSpec((tm, d), lambda i: (i, 0))
    gain = pl.BlockSpec((1, d), lambda i: (0, 0))
    return pl.pallas_call(
        body, name="mix_post", grid=(t // tm,),
        in_specs=[tile, pl.BlockSpec((d, d), lambda i: (0, 0)), tile, gain, gain] + [ANY] * len(deps),
        out_specs=[tile, tile, tile],
        out_shape=[jax.ShapeDtypeStruct((t, d), F32), jax.ShapeDtypeStruct((t, d), F32),
                   jax.ShapeDtypeStruct((t, d), BF16)],
        compiler_params=_params(1),
    )(m_mix, wo_full, h0, g_post, g_pre, *deps)


def _loss_head(fo, h1, tgt, g_post_mlp, t_real):
    t, d = h1.shape
    tile = _row_tile(t)

    def body(fo_ref, h1_ref, tgt_ref, g_ref, dfo_ref, dh2_ref, dg_ref, loss_ref, lacc):
        i = pl.program_id(0)
        fo_v = fo_ref[...]
        g = g_ref[...]
        r = _rms(fo_v)
        u = fo_v * r
        h2 = h1_ref[...] + u * g
        row = i * tile + lax.broadcasted_iota(jnp.int32, (tile, 1), 0)
        valid = jnp.logical_and(row >= N_META, row < t_real)
        diff = jnp.where(valid, h2 - tgt_ref[...], 0.0)
        dh2 = diff * (1.0 / d)
        dh2_ref[...] = dh2
        dfo, dg = _rms_bwd(dh2, u, r, g)
        dfo_ref[...] = dfo.astype(BF16)
        _accumulate(dg_ref, dg)
        _accumulate(lacc, _colsum8(diff * diff))

        @pl.when(i == pl.num_programs(0) - 1)
        def _():
            loss_ref[...] = jnp.full((SUB, LANE), (0.5 / d) * jnp.sum(lacc[...]), F32)

    return _row_call("loss_head", body, t, [fo, h1, tgt], [g_post_mlp],
                     [(d, BF16), (d, F32)], [(SUB, d), (SUB, LANE)], scratch=[pltpu.VMEM((SUB, d), F32)])


def _mid_norm_bwd(dn2, h1, dh2, mix, g_pre_mlp, g_post_mix, deps=()):
    t, d = h1.shape

    def body(dn2_ref, h1_ref, dh2_ref, mix_ref, gq_ref, gp_ref, dh1_ref, dmix_ref, dgq_ref, dgp_ref):
        h1 = h1_ref[...]
        r3 = _rms(h1)
        dx, dgq = _rms_bwd(dn2_ref[...], h1 * r3, r3, gq_ref[...])
        dh1 = dh2_ref[...] + dx
        dh1_ref[...] = dh1
        mix_v = mix_ref[...]
        r2 = _rms(mix_v)
        dmix, dgp = _rms_bwd(dh1, mix_v * r2, r2, gp_ref[...])
        dmix_ref[...] = dmix.astype(BF16)
        _accumulate(dgq_ref, dgq)
        _accumulate(dgp_ref, dgp)

    return _row_call("mid_norm_bwd", body, t, [dn2, h1, dh2, mix], [g_pre_mlp, g_post_mix],
                     [(d, F32), (d, BF16)], [(SUB, d), (SUB, d)], deps=deps)


def _pre_norm_bwd(dn, h0, dh1, g_pre_mix, deps=()):
    t, d = h0.shape

    def body(dn_ref, h0_ref, dh1_ref, g_ref, dh0_ref, dg_ref):
        h0 = h0_ref[...]
        r = _rms(h0)
        dx, dg = _rms_bwd(dn_ref[...], h0 * r, r, g_ref[...])
        dh0_ref[...] = dh1_ref[...] + dx
        _accumulate(dg_ref, dg)

    return _row_call("pre_norm_bwd", body, t, [dn, h0, dh1], [g_pre_mix], [(d, F32)], [(SUB, d)], deps=deps)


def _layer_norm_silu(a1, ln_g, ln_b, deps=()):
    t, c = a1.shape

    def body(a1_ref, g_ref, b_ref, a3_ref):
        a = a1_ref[...]
        mu = jnp.mean(a, axis=-1, keepdims=True)
        xc = a - mu
        rstd = lax.rsqrt(jnp.mean(xc * xc, axis=-1, keepdims=True) + LN_EPS)
        z = xc * rstd * g_ref[...] + b_ref[...]
        a3_ref[...] = (z * _sigmoid(z)).astype(BF16)

    return _row_call("layer_norm_silu", body, t, [a1], [ln_g, ln_b], [(c, BF16)], [], deps=deps)[0]


def _layer_norm_silu_bwd(da3, a1, ln_g, ln_b, deps=()):
    t, c = a1.shape

    def body(da3_ref, a1_ref, g_ref, b_ref, da1_ref, dg_ref, db_ref):
        a = a1_ref[...]
        g = g_ref[...]
        mu = jnp.mean(a, axis=-1, keepdims=True)
        xc = a - mu
        rstd = lax.rsqrt(jnp.mean(xc * xc, axis=-1, keepdims=True) + LN_EPS)
        xhat = xc * rstd
        z = xhat * g + b_ref[...]
        sg = _sigmoid(z)
        dz = da3_ref[...] * (sg * (1.0 + z * (1.0 - sg)))
        dxhat = dz * g
        da1_ref[...] = rstd * (dxhat - jnp.mean(dxhat, axis=-1, keepdims=True)
                               - xhat * jnp.mean(dxhat * xhat, axis=-1, keepdims=True))
        _accumulate(dg_ref, _colsum8(dz * xhat))
        _accumulate(db_ref, _colsum8(dz))

    return _row_call("layer_norm_silu_bwd", body, t, [da3, a1], [ln_g, ln_b], [(c, F32)], [(SUB, c), (SUB, c)], deps=deps)


def _branch_merge(a3, s, wpw, wso, proj, b_gates, d, deps=()):
    t, cols = proj.shape
    nblk, k, cb = wpw.shape
    w = 1024
    nh = d // w
    per = w // cb
    ga0 = (cols - 2 * d) // w
    tm = _row_tile(t)

    def body(a3_ref, s_ref, wpw_ref, wso_ref, *rest):
        pa_refs, pb_refs, bg_ref = rest[:nh], rest[nh:2 * nh], rest[2 * nh]
        ya_ref, yb_ref, ga_ref, gb_ref, m_ref = rest[2 * nh + 1 + len(deps):]
        a3v, sv = a3_ref[...], s_ref[...]
        for b in range(nblk):
            here = slice(b * cb, (b + 1) * cb)
            local = slice((b % per) * cb, (b % per + 1) * cb)
            ya = jnp.dot(a3v, wpw_ref[b], preferred_element_type=F32)
            yb = jnp.dot(sv, wso_ref[b], preferred_element_type=F32)
            ga = _sigmoid(pa_refs[b // per][:, local] + bg_ref[:, here])
            gb = _sigmoid(pb_refs[b // per][:, local] + bg_ref[:, d + b * cb:d + (b + 1) * cb])
            ya_ref[:, here] = ya.astype(BF16)
            yb_ref[:, here] = yb.astype(BF16)
            ga_ref[:, here] = ga.astype(BF16)
            gb_ref[:, here] = gb.astype(BF16)
            m_ref[:, here] = (ga * ya + gb * yb).astype(BF16)

    tile = pl.BlockSpec((tm, d), lambda i: (i, 0))
    return pl.pallas_call(
        body, name="branch_merge", grid=(t // tm,),
        in_specs=[pl.BlockSpec((tm, k), lambda i: (i, 0)), pl.BlockSpec((tm, k), lambda i: (i, 0)),
                  pl.BlockSpec((nblk, k, cb), lambda i: (0, 0, 0)), pl.BlockSpec((nblk, k, cb), lambda i: (0, 0, 0))]
                 + [pl.BlockSpec((tm, w), lambda i, h=h: (i, ga0 + h)) for h in range(2 * nh)]
                 + [pl.BlockSpec((1, 2 * d), lambda i: (0, 0))] + [ANY] * len(deps),
        out_specs=[tile] * 5,
        out_shape=[jax.ShapeDtypeStruct((t, d), BF16)] * 5,
        compiler_params=_params(1),
    )(a3, s, wpw, wso, *([proj] * (2 * nh)), b_gates, *deps)


def _gate_backward(dmix, wo_full, ga, gb, ya, yb, cols, tm, deps=()):
    t, d = ya.shape
    w = 1024
    nh = d // w
    ga0 = (cols - 2 * d) // w

    def body(dmix_ref, wo_ref, ga_ref, gb_ref, ya_ref, yb_ref, *rest):
        dya_ref, dyb_ref, dp_ref, dba_ref, dbb_ref, stage, sems = rest[len(deps):]
        h, i = pl.program_id(0), pl.program_id(1)
        dm = lax.dot_general(dmix_ref[...], wo_ref[...], (((1,), (1,)), ((), ())), preferred_element_type=F32)
        ga = ga_ref[...].astype(F32)
        gb = gb_ref[...].astype(F32)
        dya_ref[...] = (dm * ga).astype(BF16)
        dyb_ref[...] = (dm * gb).astype(BF16)
        dpa = dm * ya_ref[...].astype(F32) * ga * (1.0 - ga)
        dpb = dm * yb_ref[...].astype(F32) * gb * (1.0 - gb)

        def copies(row0, colblk):
            return [pltpu.make_async_copy(
                stage.at[g], dp_ref.at[pl.ds(pl.multiple_of(row0, tm), tm),
                                       pl.ds(pl.multiple_of((ga0 + g * nh + colblk) * w, w), w)], sems.at[g])
                for g in range(2)]

        @pl.when(jnp.logical_or(h > 0, i > 0))
        def _():
            for cp in copies(0, 0):
                cp.wait()

        stage[0] = dpa.astype(BF16)
        stage[1] = dpb.astype(BF16)
        for cp in copies(i * tm, h):
            cp.start()

        @pl.when(i == 0)
        def _():
            dba_ref[...] = _colsum8(dpa)
            dbb_ref[...] = _colsum8(dpb)

        @pl.when(i > 0)
        def _():
            dba_ref[...] += _colsum8(dpa)
            dbb_ref[...] += _colsum8(dpb)

        @pl.when(jnp.logical_and(h == pl.num_programs(0) - 1, i == pl.num_programs(1) - 1))
        def _():
            for cp in copies(0, 0):
                cp.wait()

    tile = pl.BlockSpec((tm, w), lambda h, i: (i, h))
    return pl.pallas_call(
        body, name="gate_backward", grid=(nh, t // tm),
        in_specs=[pl.BlockSpec((tm, d), lambda h, i: (i, 0)),
                  pl.BlockSpec((w, d), lambda h, i: (h, 0)),
                  tile, tile, tile, tile] + [ANY] * len(deps),
        out_specs=[tile, tile, ANY,
                   pl.BlockSpec((SUB, w), lambda h, i: (0, h)),
                   pl.BlockSpec((SUB, w), lambda h, i: (0, h))],
        out_shape=[jax.ShapeDtypeStruct((t, d), BF16), jax.ShapeDtypeStruct((t, d), BF16),
                   jax.ShapeDtypeStruct((t, cols), BF16),
                   jax.ShapeDtypeStruct((SUB, d), F32), jax.ShapeDtypeStruct((SUB, d), F32)],
        scratch_shapes=[pltpu.VMEM((2, tm, w), BF16), pltpu.SemaphoreType.DMA((2,))],
        compiler_params=_params(2),
    )(dmix, wo_full, ga, gb, ya, yb, *deps)


def _shifted_views(win, offsets):
    n = win.shape[0]
    rotated = {}
    views = {}
    for o in offsets:
        q, r = divmod(o, SUB)
        if r not in rotated:
            rotated[r] = win if r == 0 else pltpu.roll(win, n - r, 0)
        views[o] = rotated[r][q * SUB:q * SUB + CONV_CHUNK]
    return views


def _causal_views(xp_ref, ntap, r0):
    win = xp_ref[pl.ds(r0, CONV_CHUNK + CONV_PAD), :]
    views = _shifted_views(win, [CONV_PAD - (ntap - 1 - k) for k in range(ntap)])
    return [views[CONV_PAD - (ntap - 1 - k)] for k in range(ntap)]


def _causal_conv(xp_ref, w_ref, ntap, r0):
    acc = None
    for k, shifted in enumerate(_causal_views(xp_ref, ntap, r0)):
        term = w_ref[k:k + 1, :] * shifted
        acc = term if acc is None else acc + term
    return acc


def _anticausal_conv(xp_ref, w_ref, ntap, r0):
    win = xp_ref[pl.ds(pl.multiple_of(CONV_PAD + r0, CONV_PAD), CONV_CHUNK + CONV_PAD), :]
    views = _shifted_views(win, [ntap - 1 - k for k in range(ntap)])
    acc = None
    for k in range(ntap):
        term = w_ref[k:k + 1, :] * views[ntap - 1 - k]
        acc = term if acc is None else acc + term
    return acc


def _conv_weight_grad(dw_ref, d_chunk, xp_ref, ntap, r0):
    for k, shifted in enumerate(_causal_views(xp_ref, ntap, r0)):
        dw_ref[k * SUB:(k + 1) * SUB, :] += _colsum8(d_chunk * shifted)


def _zero_pads(ref, t):
    ref[0:CONV_PAD, :] = jnp.zeros((CONV_PAD, LANE), F32)
    ref[CONV_PAD + t:CONV_PAD + t + CONV_PAD, :] = jnp.zeros((CONV_PAD, LANE), F32)


def _for_chunks(t, fn):
    def step(idx, carry):
        fn(pl.multiple_of(idx * CONV_CHUNK, CONV_CHUNK))
        return carry

    lax.fori_loop(0, t // CONV_CHUNK, step, 0)


def _conv_forward(proj, conf_w, conf_b, short_w, dc, deps=()):
    t = proj.shape[0]
    nc = dc // LANE

    def body(av_ref, ag_ref, bg_ref, cg_ref, v_ref, cw_ref, cb_ref, sw_ref, *rest):
        a1_ref, s_ref, xa, xb = rest[len(deps):]
        _zero_pads(xa, t)
        _zero_pads(xb, t)
        xa[CONV_PAD:CONV_PAD + t, :] = av_ref[...] * _sigmoid(ag_ref[...])
        xb[CONV_PAD:CONV_PAD + t, :] = cg_ref[...] * v_ref[...]

        def chunk(r0):
            rs = pl.ds(r0, CONV_CHUNK)
            a1_ref[rs, :] = _causal_conv(xa, cw_ref, CONF_K, r0) + cb_ref[...]
            s_ref[rs, :] = (bg_ref[rs, :] * _causal_conv(xb, sw_ref, SHORT_K, r0)).astype(BF16)

        _for_chunks(t, chunk)

    col = lambda g: pl.BlockSpec((t, LANE), lambda c, g=g: (0, g * nc + c))
    return pl.pallas_call(
        body, name="conv_forward", grid=(nc,),
        in_specs=[col(0), col(1), col(2), col(3), col(4),
                  pl.BlockSpec((CONF_K, LANE), lambda c: (0, c)),
                  pl.BlockSpec((1, LANE), lambda c: (0, c)),
                  pl.BlockSpec((SHORT_K, LANE), lambda c: (0, c))] + [ANY] * len(deps),
        out_specs=[pl.BlockSpec((t, LANE), lambda c: (0, c)), pl.BlockSpec((t, LANE), lambda c: (0, c))],
        out_shape=[jax.ShapeDtypeStruct((t, dc), F32), jax.ShapeDtypeStruct((t, dc), BF16)],
        scratch_shapes=[pltpu.VMEM((t + 2 * CONV_PAD, LANE), F32), pltpu.VMEM((t + 2 * CONV_PAD, LANE), F32)],
        compiler_params=_params(1),
    )(proj, proj, proj, proj, proj, conf_w, conf_b, short_w, *deps)


def _conv_backward(dproj, proj, da1, ds, conf_w, short_w, dc):
    t = proj.shape[0]
    nc = dc // LANE

    def body(dp_in, av_ref, ag_ref, bg_ref, cg_ref, v_ref, da1_ref, ds_ref, cw_ref, sw_ref,
             dp_ref, dcw_ref, dcb_ref, dsw_ref, xa, xb, da, db, stage, sems):
        del dp_in
        c = pl.program_id(0)
        for ref in (xa, xb, da, db):
            _zero_pads(ref, t)
        xa[CONV_PAD:CONV_PAD + t, :] = av_ref[...] * _sigmoid(ag_ref[...])
        xb[CONV_PAD:CONV_PAD + t, :] = cg_ref[...] * v_ref[...]
        da[CONV_PAD:CONV_PAD + t, :] = da1_ref[...]
        dcw_ref[...] = jnp.zeros(dcw_ref.shape, F32)
        dsw_ref[...] = jnp.zeros(dsw_ref.shape, F32)
        dcb_ref[...] = jnp.zeros(dcb_ref.shape, F32)

        def copies(colblk):
            return [pltpu.make_async_copy(
                stage.at[g], dp_ref.at[:, pl.ds(pl.multiple_of((g * nc + colblk) * LANE, LANE), LANE)], sems.at[g])
                for g in range(5)]

        @pl.when(c > 0)
        def _():
            for cp in copies(0):
                cp.wait()

        def through_gate(r0):
            rs = pl.ds(r0, CONV_CHUNK)
            ds_c = ds_ref[rs, :]
            stage[2, rs, :] = (ds_c * _causal_conv(xb, sw_ref, SHORT_K, r0)).astype(BF16)
            db[pl.ds(pl.multiple_of(CONV_PAD + r0, CONV_PAD), CONV_CHUNK), :] = ds_c * bg_ref[rs, :]

        _for_chunks(t, through_gate)

        def through_convs(r0):
            rs = pl.ds(r0, CONV_CHUNK)
            da0 = _anticausal_conv(da, cw_ref, CONF_K, r0)
            sg = _sigmoid(ag_ref[rs, :])
            stage[0, rs, :] = (da0 * sg).astype(BF16)
            stage[1, rs, :] = (da0 * av_ref[rs, :] * sg * (1.0 - sg)).astype(BF16)
            dcv = _anticausal_conv(db, sw_ref, SHORT_K, r0)
            stage[3, rs, :] = (dcv * v_ref[rs, :]).astype(BF16)
            stage[4, rs, :] = (dcv * cg_ref[rs, :]).astype(BF16)
            da1_c = da1_ref[rs, :]
            _conv_weight_grad(dcw_ref, da1_c, xa, CONF_K, r0)
            _conv_weight_grad(dsw_ref, ds_ref[rs, :] * bg_ref[rs, :], xb, SHORT_K, r0)
            dcb_ref[...] += _colsum8(da1_c)

        _for_chunks(t, through_convs)
        for cp in copies(c):
            cp.start()

        @pl.when(c == pl.num_programs(0) - 1)
        def _():
            for cp in copies(0):
                cp.wait()

    col = lambda g: pl.BlockSpec((t, LANE), lambda c, g=g: (0, g * nc + c))
    blk = pl.BlockSpec((t, LANE), lambda c: (0, c))
    return pl.pallas_call(
        body, name="conv_backward", grid=(nc,),
        in_specs=[ANY, col(0), col(1), col(2), col(3), col(4), blk, blk,
                  pl.BlockSpec((CONF_K, LANE), lambda c: (0, c)),
                  pl.BlockSpec((SHORT_K, LANE), lambda c: (0, c))],
        out_specs=[ANY,
                   pl.BlockSpec((CONF_K * SUB, LANE), lambda c: (0, c)),
                   pl.BlockSpec((SUB, LANE), lambda c: (0, c)),
                   pl.BlockSpec((SHORT_K * SUB, LANE), lambda c: (0, c))],
        out_shape=[jax.ShapeDtypeStruct(dproj.shape, dproj.dtype),
                   jax.ShapeDtypeStruct((CONF_K * SUB, dc), F32),
                   jax.ShapeDtypeStruct((SUB, dc), F32),
                   jax.ShapeDtypeStruct((SHORT_K * SUB, dc), F32)],
        scratch_shapes=[pltpu.VMEM((t + 2 * CONV_PAD, LANE), F32)] * 4
                       + [pltpu.VMEM((5, t, LANE), BF16), pltpu.SemaphoreType.DMA((5,))],
        input_output_aliases={0: 0},
        compiler_params=_params(1),
    )(dproj, proj, proj, proj, proj, proj, da1, ds, conf_w, short_w)


def _adamw_math(w, g, m, v):
    m = ADAM_B1 * m + (1.0 - ADAM_B1) * g
    v = ADAM_B2 * v + (1.0 - ADAM_B2) * (g * g)
    m_hat = m / (1.0 - ADAM_B1 ** ADAM_STEP)
    v_hat = v / (1.0 - ADAM_B2 ** ADAM_STEP)
    delta = -ADAM_LR * (m_hat / (jnp.sqrt(v_hat) + ADAM_EPS) + ADAM_WD * w)
    return delta, m, v


def _cast_into_slot(name, w, me_arr, deps=()):
    r, c = w.shape
    tr = 256

    def body(me_ref, w_ref, *rest):
        del me_ref
        rest[-1][0] = w_ref[...].astype(BF16)

    return pl.pallas_call(
        body, name=name,
        grid_spec=pltpu.PrefetchScalarGridSpec(
            num_scalar_prefetch=1, grid=(r // tr,),
            in_specs=[pl.BlockSpec((tr, c), lambda i, me: (i, 0))] + [ANY] * len(deps),
            out_specs=pl.BlockSpec((1, tr, c), lambda i, me: (me[0], i, 0))),
        out_shape=jax.ShapeDtypeStruct((N_DEV, r, c), BF16),
        compiler_params=_params(1),
    )(me_arr, w, *deps)


def _chip_sum(name, full, from_sibling, me_arr):
    _, r, c = full.shape
    tr = min(r, 1024)

    def body(me_ref, full_ref, sib_ref, sums_ref):
        del me_ref
        sums_ref[0] = (full_ref[0].astype(F32) + sib_ref[0].astype(F32)).astype(BF16)

    other = lambda k, me: (me[0] // 2 + 1 + k) % 4
    return pl.pallas_call(
        body, name=name,
        grid_spec=pltpu.PrefetchScalarGridSpec(
            num_scalar_prefetch=1, grid=(r // tr, 3),
            in_specs=[pl.BlockSpec((1, tr, c), lambda i, k, me: (2 * other(k, me) + me[0] % 2, i, 0)),
                      pl.BlockSpec((1, tr, c), lambda i, k, me: (other(k, me), i, 0))],
            out_specs=pl.BlockSpec((1, tr, c), lambda i, k, me: (other(k, me), i, 0))),
        out_shape=jax.ShapeDtypeStruct((4, r, c), BF16),
        compiler_params=_params(2),
    )(me_arr, full, from_sibling)


def _adamw_shard(name, w, m, v, parts, me_arr, deps=()):
    r, c = w.shape
    tr = min(256, r // len(parts))
    np_ = len(parts)
    per = r // np_ // tr

    def body(me_ref, w_ref, m_ref, v_ref, *rest):
        g_out, d_out, m_out, v_out = rest[5 * np_ + len(deps):]
        g = None
        for p in range(np_):
            gp = rest[5 * p][...]
            for l_ref in rest[5 * p + 1:5 * p + 5]:
                gp = gp + l_ref[0].astype(F32)
            g = gp if g is None else jnp.where(pl.program_id(0) // per == p, gp, g)
        delta, m_new, v_new = _adamw_math(w_ref[...], g, m_ref[...], v_ref[...])
        g_out[...] = g
        d_out[...] = delta
        m_out[...] = m_new
        v_out[...] = v_new

    tile = pl.BlockSpec((tr, c), lambda i, me: (i, 0))
    part_specs, part_args = [], []
    for p, (g_own, from_sibling, landed) in enumerate(parts):
        row = lambda i, p=p: jnp.clip(i - p * per, 0, per - 1)
        part_specs.append(pl.BlockSpec((tr, c), lambda i, me, row=row: (row(i), 0)))
        part_specs += [pl.BlockSpec((1, tr, c), lambda i, me, k=k, row=row: ((me[0] // 2 + k) % 4, row(i), 0))
                       for k in range(4)]
        part_args += [g_own, from_sibling, landed, landed, landed]
    return pl.pallas_call(
        body, name=name,
        grid_spec=pltpu.PrefetchScalarGridSpec(
            num_scalar_prefetch=1, grid=(r // tr,),
            in_specs=[tile] * 3 + part_specs + [ANY] * len(deps), out_specs=[tile] * 4),
        out_shape=[jax.ShapeDtypeStruct((r, c), F32)] * 4,
        compiler_params=_params(1),
    )(me_arr, w, m, v, *part_args, *deps)


SMALL_W = 1024
VEC_ROWS = 16
LOSS_ROW = 15
META_ROW0 = 16
CONF_ROW0 = 64
SHORT_ROW0 = 96
SMALL_ROWS = 104


def _pack_small(vec_parts, dmeta, dcw, dsw, loss_blk, me_arr):
    widths = [p.shape[1] for p in vec_parts]
    nv = len(vec_parts)

    def body(me_ref, *refs):
        del me_ref
        parts, (dmeta_ref, dcw_ref, dsw_ref, loss_ref, out_ref) = refs[:nv], refs[nv:]
        out_ref[0] = jnp.zeros((SMALL_ROWS, SMALL_W), F32)
        out_ref[0, LOSS_ROW:LOSS_ROW + 1, 0:LANE] = loss_ref[0:1, :]
        row = 0
        for p_ref, wd in zip(parts, widths):
            s = jnp.sum(p_ref[...], axis=0, keepdims=True)
            for h in range(wd // SMALL_W):
                out_ref[0, row:row + 1, :] = s[:, h * SMALL_W:(h + 1) * SMALL_W]
                row += 1
        for h in range(dmeta_ref.shape[1] // SMALL_W):
            out_ref[0, META_ROW0 + h * N_META:META_ROW0 + (h + 1) * N_META, :] = dmeta_ref[:, h * SMALL_W:(h + 1) * SMALL_W]
        for k in range(CONF_K):
            out_ref[0, CONF_ROW0 + k:CONF_ROW0 + k + 1, :] = jnp.sum(dcw_ref[k * SUB:(k + 1) * SUB, :], axis=0, keepdims=True)
        for k in range(SHORT_K):
            out_ref[0, SHORT_ROW0 + k:SHORT_ROW0 + k + 1, :] = jnp.sum(dsw_ref[k * SUB:(k + 1) * SUB, :], axis=0, keepdims=True)

    ins = [*vec_parts, dmeta, dcw, dsw, loss_blk]
    return pl.pallas_call(
        body, name="pack_small",
        grid_spec=pltpu.PrefetchScalarGridSpec(
            num_scalar_prefetch=1, grid=(1,),
            in_specs=[pl.BlockSpec(a.shape, lambda i, me: (0, 0)) for a in ins],
            out_specs=pl.BlockSpec((1, SMALL_ROWS, SMALL_W), lambda i, me: (me[0], 0, 0))),
        out_shape=jax.ShapeDtypeStruct((N_DEV, SMALL_ROWS, SMALL_W), F32),
        compiler_params=_params(1),
    )(me_arr, *ins)


def _small_update(gathered, me_arr, vec_params, meta_p, conf_p, short_p):
    widths = [p[0].shape[1] for p in vec_params]
    nv = len(vec_params)
    mcols = meta_p[0].shape[1]
    per_row = SMALL_W // mcols

    def body(me_ref, gv_ref, gm_ref, gc_ref, gs_ref, *rest):
        del me_ref
        ins, outs = rest[:3 * (nv + 3)], rest[3 * (nv + 3):]

        def total(ref, r0, rows):
            s = ref[0, r0:r0 + rows, :]
            for dev in range(1, N_DEV):
                s = s + ref[dev, r0:r0 + rows, :]
            return s

        grads = []
        row = 0
        for wd in widths:
            pieces = [total(gv_ref, row + h, 1) for h in range(wd // SMALL_W)]
            grads.append(pieces[0] if len(pieces) == 1 else jnp.concatenate(pieces, axis=1))
            row += len(pieces)
        grads.append(total(gm_ref, 0, N_META))
        grads.append(total(gc_ref, 0, CONF_K))
        grads.append(total(gs_ref, 0, SHORT_K))
        loss = gv_ref[0, LOSS_ROW:LOSS_ROW + 1, 0:LANE]
        for dev in range(1, N_DEV):
            loss = loss + gv_ref[dev, LOSS_ROW:LOSS_ROW + 1, 0:LANE]
        outs[-1][...] = loss
        for idx, g in enumerate(grads):
            w_ref, m_ref, v_ref = ins[3 * idx:3 * idx + 3]
            delta, m_new, v_new = _adamw_math(w_ref[...], g, m_ref[...], v_ref[...])
            g_out, d_out, m_out, v_out = outs[4 * idx:4 * idx + 4]
            g_out[...] = g
            d_out[...] = delta
            m_out[...] = m_new
            v_out[...] = v_new

    params = list(vec_params) + [meta_p, conf_p, short_p]
    flat = [a for p in params for a in p]
    whole = lambda a: pl.BlockSpec(a.shape, lambda i, me: (0,) * a.ndim)
    outs = pl.pallas_call(
        body, name="small_update",
        grid_spec=pltpu.PrefetchScalarGridSpec(
            num_scalar_prefetch=1, grid=(1,),
            in_specs=[pl.BlockSpec((N_DEV, VEC_ROWS, SMALL_W), lambda i, me: (0, 0, 0)),
                      pl.BlockSpec((N_DEV, N_META, mcols),
                                   lambda i, me: (0, META_ROW0 // N_META + me[0] // per_row, me[0] % per_row)),
                      pl.BlockSpec((N_DEV, 32, LANE), lambda i, me: (0, CONF_ROW0 // 32, me[0])),
                      pl.BlockSpec((N_DEV, SUB, LANE), lambda i, me: (0, SHORT_ROW0 // SUB, me[0]))]
                     + [whole(a) for a in flat],
            out_specs=[whole(p[0]) for p in params for _ in range(4)]
                      + [pl.BlockSpec((1, LANE), lambda i, me: (0, 0))]),
        out_shape=[jax.ShapeDtypeStruct(p[0].shape, F32) for p in params for _ in range(4)]
                  + [jax.ShapeDtypeStruct((1, LANE), F32)],
        compiler_params=_params(1),
    )(me_arr, gathered, gathered, gathered, gathered, *flat)
    return [tuple(outs[4 * i:4 * i + 4]) for i in range(len(params))], outs[-1][0, 0]


def kernel(x, meta, g_pre_mix, w_in, b_gates, conf_dw_w, conf_dw_b, conf_ln_g, conf_ln_b, conf_w_pw, short_dw_w, short_w_out, w_o, g_post_mix, g_pre_mlp, w_up, w_down, g_post_mlp, loss_target, m_meta, m_g_pre_mix, m_w_in, m_b_gates, m_conf_dw_w, m_conf_dw_b, m_conf_ln_g, m_conf_ln_b, m_conf_w_pw, m_short_dw_w, m_short_w_out, m_w_o, m_g_post_mix, m_g_pre_mlp, m_w_up, m_w_down, m_g_post_mlp, v_meta, v_g_pre_mix, v_w_in, v_b_gates, v_conf_dw_w, v_conf_dw_b, v_conf_ln_g, v_conf_ln_b, v_conf_w_pw, v_short_dw_w, v_short_w_out, v_w_o, v_g_post_mix, v_g_pre_mlp, v_w_up, v_w_down, v_g_post_mlp):
    seq, d = x.shape[1], x.shape[2]
    dc = conf_w_pw.shape[1]
    t_real = N_META + seq
    t = -(-t_real // ROW_TILE) * ROW_TILE
    tm = t // 2
    assert tm % 16 == 0 and d % 1024 == 0 and dc % 1024 == 0
    x_idx, y_idx, c_idx = _position()
    me_arr = jnp.reshape(4 * x_idx + 2 * y_idx + c_idx, (1,)).astype(jnp.int32)

    big = [w_in[0], conf_w_pw[0], short_w_out[0], w_o[0], w_up[0], w_down[0]]
    big_names = ["w_in", "conf_w_pw", "short_w_out", "w_o", "w_up", "w_down"]
    groups = [[0], [1, 2, 3], [4], [5]]
    slots, deps = [], []
    for g, idxs in enumerate(groups):
        slots.append([_cast_into_slot("cast_" + big_names[i], big[i], me_arr, deps=deps) for i in idxs])
        if g == 0:
            direct0 = _remote_start("gather0_direct_start", "gather_direct", slots[0])
            deps = [direct0[3]]
    casts = [sl for group in slots[1:] for sl in group]
    meta_g, cw_g, sw_g = _all_gather("gather_small_params", [meta, conf_dw_w[0], short_dw_w[0]], deps=casts)

    def start_direct(g, deps):
        send, recv, bufs, tok = _remote_start("gather%d_direct_start" % g, "gather_direct", slots[g], deps=deps)
        return (send, recv, bufs), tok

    def relay(g, state, after):
        send, recv, bufs, tok = _remote_pass_on("gather%d_relay" % g, "gather_direct", *state, after, "gather_relay")
        return (send, recv, bufs), tok

    def gathered(g, state, after):
        send, recv, bufs, tok = _remote_pass_on("gather%d_diag" % g, "gather_relay", *state, after, "gather_diag")
        return _remote_wait("gather%d_diag_wait" % g, "gather_diag", send, recv, bufs, len(bufs), [tok])

    unshard =lambda g: jnp.transpose(g, (1, 0, 2)).reshape(g.shape[1], -1)
    meta_full, cw_full, sw_full = unshard(meta_g), unshard(cw_g), unshard(sw_g)

    relay0, tok = relay(0, direct0[:3], [meta_g])
    zrows = jnp.zeros((t - t_real, d), F32) + tok[0, 0] * 0.0
    h0 = jnp.concatenate([meta_full, x[0], zrows], axis=0)
    tgt = jnp.concatenate([jnp.zeros((N_META, d), F32), loss_target[0], zrows], axis=0)
    n = _pre_norm(h0, g_pre_mix)
    direct1, tok = start_direct(1, [tok])
    direct2, tok = start_direct(2, [tok])
    win_g, = gathered(0, relay0, [tok, n])
    proj = _mm_cols_pairs("proj", n, win_g, tm=tm // 2)
    relay1, tok = relay(1, direct1, [proj])
    a1, s = _conv_forward(proj, cw_full, conf_dw_b, sw_full, dc, deps=[tok])
    relay2, tok = relay(2, direct2, [a1])
    direct3, tok = start_direct(3, [tok])
    a3 = _layer_norm_silu(a1, conf_ln_g, conf_ln_b, deps=[tok])
    wpw_g, wso_g, wo_g = gathered(1, relay1, [a3])
    wo_full = wo_g.reshape(d, d)
    ya, yb, gate_a, gate_b, m_mix = _branch_merge(a3, s, wpw_g, wso_g, proj, b_gates, d)
    mix, h1, n2 = _mix_post(m_mix, wo_full, h0, g_post_mix, g_pre_mlp)
    wup_g, = gathered(2, relay2, [n2])

    def up_epilogue(acc):
        r = jnp.maximum(acc, 0.0)
        return r * r, r

    half_up = dict(tm=tm, epilogue=up_epilogue, out_dtypes=(BF16, BF16))
    f, relu_up = _mm_cols("mlp_up0", n2, wup_g, blocks=(0, N_DEV // 2), **half_up)
    relay3, tok = relay(3, direct3, [f])
    f, relu_up = _mm_cols("mlp_up1", n2, wup_g, blocks=(N_DEV // 2, N_DEV), into=(f, relu_up), deps=[tok], **half_up)
    wdn_g, = gathered(3, relay3, [f])
    wdn_full = wdn_g.reshape(-1, d)
    fo = _mm_rows("mlp_down", f, wdn_full, tm=tm, tn=512)
    dfo, dh2, dg_post_mlp, loss_blk = _loss_head(fo, h1, tgt, g_post_mlp, t_real)

    def reduce_start(tag, fulls, deps):
        lands = [lax.empty((4,) + g.shape[1:], BF16) for g in fulls]
        send, recv, bufs, tok = _remote_start("reduce_%s_d2d_start" % tag, "reduce_d2d", fulls, lands, deps=deps)
        return (send, recv, bufs), tok

    def reduce_middle(tag, state, owns, after):
        send, recv, bufs = state
        k = len(owns)
        bufs = _remote_wait("reduce_%s_d2d_wait" % tag, "reduce_d2d", send, recv, bufs, k, after)
        from_sibling = bufs[k:]
        sums = [_chip_sum("chip_sum_%s%d" % (tag, i), bufs[i], from_sibling[i], me_arr) for i in range(k)]
        lands = [lax.empty(sm.shape, BF16) for sm in sums]
        send, recv, bufs, tok = _remote_start("reduce_%s_ici_start" % tag, "reduce_ici", sums, lands)
        return (send, recv, bufs, list(zip(owns, from_sibling))), tok

    def reduce_finish(tag, state, after):
        send, recv, bufs, local = state
        k = len(local)
        bufs = _remote_wait("reduce_%s_ici_wait" % tag, "reduce_ici", send, recv, bufs, k, after)
        return [(own, sib, landed) for (own, sib), landed in zip(local, bufs[k:])]

    dup = _mm_nt_blocks("d_up", dfo, wdn_full, tm=tm, tkb=1024, extra=(relu_up,),
                        epilogue=lambda acc, r: (acc * (2.0 * r.astype(F32)),), out_dtypes=(BF16,))[0]
    gw_down, gw_down_own = _mm_tn("dw_down", f, dfo, me_arr, m=f.shape[1], n=d, tma=512, tn=d, sharded="rows")
    red_down, tok = reduce_start("down", [gw_down], ())
    dn2 = _mm_nt_acc_parts("d_n2", dup, wup_g, tm=tm, tn=512, deps=[tok])
    gw_up, gw_up_own = _mm_tn("dw_up", n2, dup, me_arr, m=d, n=dup.shape[1], tma=512, tn=2048, sharded="cols")
    red_down, tok = reduce_middle("down", red_down, [gw_down_own], [dn2])
    red_up, tok = reduce_start("up", [gw_up], [tok])
    dh1, dmix, dg_pre_mlp, dg_post_mix = _mid_norm_bwd(dn2, h1, dh2, mix, g_pre_mlp, g_post_mix, deps=[tok])
    dya, dyb, dproj, db_a, db_b = _gate_backward(dmix, wo_full, gate_a, gate_b, ya, yb, proj.shape[1], tm // 2)
    db_gates = jnp.concatenate([db_a, db_b], axis=1)
    red_up, tok = reduce_middle("up", red_up, [gw_up_own], [dya])
    gw_o, gw_o_own = _mm_tn("dw_o", m_mix, dmix, me_arr, m=d, n=d, tma=d // N_DEV, tn=d, sharded="rows", deps=[tok])
    da3 = _mm_nt_acc("d_a3", dya, wpw_g, tm=tm, tn=512)
    gw_pw, gw_pw_own = _mm_tn("dw_pw", a3, dya, me_arr, m=dc, n=d, tma=512, tn=d, sharded="cols")
    dsb = _mm_nt_acc("d_s", dyb, wso_g, tm=tm, tn=512)
    gw_so, gw_so_own = _mm_tn("dw_so", s, dyb, me_arr, m=dc, n=d, tma=512, tn=d, sharded="cols")
    red_mix, tok = reduce_start("mix", [gw_pw, gw_so, gw_o], ())
    da1, dln_g, dln_b = _layer_norm_silu_bwd(da3, a1, conf_ln_g, conf_ln_b, deps=[tok])
    dproj, dcw, dcb, dsw = _conv_backward(dproj, proj, da1, dsb, cw_full, sw_full, dc)
    red_mix, tok = reduce_middle("mix", red_mix, [gw_pw_own, gw_so_own, gw_o_own], [dcb])
    in_cb = w_in.shape[2]
    half = d // 2
    red_in = []
    for part in range(2):
        gw, own = _mm_tn("dw_in%d" % part, n, dproj, me_arr, m=half, n=proj.shape[1], tma=512, tn=2 * in_cb,
                         sharded="cols", a_off=part * (half // 512), deps=[tok])
        state, tok = reduce_start("in%d" % part, [gw], ())
        red_in.append((state, own))
    for part in range(2):
        state, own = red_in[part]
        red_in[part], tok = reduce_middle("in%d" % part, state, [own], [tok])
    dn = _mm_nt_acc("d_n", dproj, win_g, tm=tm // 2, tn=512, deps=[tok])
    dh0, dg_pre_mix = _pre_norm_bwd(dn, h0, dh1, g_pre_mix)
    grad_x = dh0[N_META:t_real][None]

    vec_parts = [dg_pre_mix, db_gates, dcb, dln_g, dln_b, dg_post_mix, dg_pre_mlp, dg_post_mlp]
    packed = _pack_small(vec_parts, dh0[:N_META], dcw, dsw, loss_blk, me_arr)
    send, recv, bufs, tok = _remote_start("small_grads_ici_start", "gather_ici", [packed])
    vec_names = ["g_pre_mix", "b_gates", "conf_dw_b", "conf_ln_g", "conf_ln_b", "g_post_mix", "g_pre_mlp", "g_post_mlp"]
    env = locals()
    results = {}

    def update(nm, parts, deps=()):
        res = _adamw_shard("adamw_" + nm, env[nm][0], env["m_" + nm][0], env["v_" + nm][0], parts, me_arr, deps=deps)
        results[nm] = tuple(r[None] for r in res)
        return res[0]

    done = [update("w_down", reduce_finish("down", red_down, [tok]), deps=[tok])]
    done.append(update("w_up", reduce_finish("up", red_up, done)))
    bufs = _remote_wait("small_grads_ici_wait", "gather_ici", send, recv, bufs, 1, done)
    send, recv, bufs, tok = _remote_start("small_grads_d2d_start", "gather_d2d", bufs)
    for nm, pair in zip(["conf_w_pw", "short_w_out", "w_o"], reduce_finish("mix", red_mix, [tok])):
        done.append(update(nm, [pair], deps=[tok]))
    small_g, = _remote_wait("small_grads_d2d_wait", "gather_d2d", send, recv, bufs, 1, done)
    triple = lambda nm, sq: tuple(env[p + nm][0] if sq else env[p + nm] for p in ("", "m_", "v_"))
    small, loss = _small_update(small_g, me_arr, [triple(nm, False) for nm in vec_names],
                                triple("meta", False), triple("conf_dw_w", True), triple("short_dw_w", True))
    for nm, res in zip(vec_names + ["meta"], small[:len(vec_names) + 1]):
        results[nm] = res
    results["conf_dw_w"] = tuple(r[None] for r in small[-2])
    results["short_dw_w"] = tuple(r[None] for r in small[-1])
    update("w_in", [reduce_finish("in%d" % part, red_in[part], [small[0][0]])[0] for part in range(2)])

    order = ["meta", "g_pre_mix", "w_in", "b_gates", "conf_dw_w", "conf_dw_b", "conf_ln_g", "conf_ln_b", "conf_w_pw",
             "short_dw_w", "short_w_out", "w_o", "g_post_mix", "g_pre_mlp", "w_up", "w_down", "g_post_mlp"]
    return (loss, grad_x, *[results[nm][0] for nm in order], *[results[nm][1] for nm in order],
            *[results[nm][2] for nm in order], *[results[nm][3] for nm in order])
```

```python
import jax
import jax.numpy as jnp
from jax import lax
from jax.experimental import pallas as pl
from jax.experimental.pallas import tpu as pltpu

N_DEV = 8
N_META = 16
CONF_K = 31
SHORT_K = 3
RMS_EPS = 1e-6
LN_EPS = 1e-5
ADAM_LR = 0.001
ADAM_B1 = 0.9
ADAM_B2 = 0.999
ADAM_EPS = 1e-08
ADAM_WD = 0.01
ADAM_STEP = 10

LANE = 128
SUB = 8
ROW_TILE = 128
CONV_PAD = 32
CONV_CHUNK = 128
VMEM_LIMIT = 56 * 1024 * 1024

F32 = jnp.float32
BF16 = jnp.bfloat16
MESH = pl.DeviceIdType.MESH
ANY = pl.BlockSpec(memory_space=pl.ANY)
HBM_SPEC = pl.BlockSpec(memory_space=pltpu.HBM)
SEM_SPEC = pl.BlockSpec(memory_space=pltpu.SEMAPHORE)
EFFECT = pltpu.SideEffectType.DATAFLOW_SIDE_EFFECTING


def _params(n_axes):
    return pltpu.CompilerParams(dimension_semantics=("arbitrary",) * n_axes, vmem_limit_bytes=VMEM_LIMIT)


def _sigmoid(z):
    return 1.0 / (1.0 + jnp.exp(-z))


def _colsum8(v):
    r, c = v.shape
    return jnp.sum(v.reshape(r // SUB, SUB, c), axis=0)


def _position():
    x, y, c = lax.axis_index("x"), lax.axis_index("y"), lax.axis_index("c")
    return x, y, c


def _flat(p):
    return 4 * p[0] + 2 * p[1] + p[2]


def _all_gather(name, shards, deps=()):
    n, nd = len(shards), len(deps)

    def body(*refs):
        ins, outs = refs[:n], refs[n + nd:2 * n + nd]
        send_sems, recv_sems, local_sems = refs[2 * n + nd:]
        x, y, c = _position()
        me, sibling = (x, y, c), (x, y, 1 - c)
        chips = [(1 - x, y), (x, 1 - y), (1 - x, 1 - y)]

        def copy(q, k, block, to, src=None):
            dst = outs[q].at[_flat(block)]
            return pltpu.make_async_remote_copy(
                src_ref=dst if src is None else src, dst_ref=dst,
                send_sem=send_sems.at[q, k], recv_sem=recv_sems.at[q, k],
                device_id=to, device_id_type=MESH)

        mine = [pltpu.make_async_copy(ins[q], outs[q].at[_flat(me)], local_sems.at[q]) for q in range(n)]
        for cp in mine:
            cp.start()
        first = []
        for q in range(n):
            first.append(copy(q, 0, me, sibling, src=ins[q]))
            for j, chip in enumerate(chips):
                first.append(copy(q, 1 + j, me, (*chip, c), src=ins[q]))
        for cp in first:
            cp.start()
        passed = []
        for q in range(n):
            for j, chip in enumerate(chips):
                copy(q, 1 + j, (*chip, c), me).wait_recv()
                fwd = copy(q, 4 + j, (*chip, c), sibling)
                fwd.start()
                passed.append(fwd)
        for q in range(n):
            copy(q, 0, sibling, me).wait_recv()
            for j, chip in enumerate(chips):
                copy(q, 4 + j, (*chip, 1 - c), me).wait_recv()
        for cp in first + passed:
            cp.wait_send()
        for cp in mine:
            cp.wait()

    return pl.pallas_call(
        body, name=name,
        in_specs=[ANY] * (n + nd), out_specs=[ANY] * n,
        out_shape=[jax.ShapeDtypeStruct((N_DEV,) + s.shape, s.dtype) for s in shards],
        scratch_shapes=[pltpu.SemaphoreType.DMA((n, 7)), pltpu.SemaphoreType.DMA((n, 7)),
                        pltpu.SemaphoreType.DMA((n,))],
    )(*shards, *deps)


N_COPIES = {"gather_ici": 4, "gather_d2d": 3, "gather_direct": 3, "gather_relay": 3, "gather_diag": 1,
            "reduce_d2d": 4, "reduce_ici": 3}


def _copy_plan(kind):
    x, y, c = _position()
    me, sibling = (x, y, c), (x, y, 1 - c)
    chips = [(1 - x, y), (x, 1 - y), (1 - x, 1 - y)]
    if kind == "gather_ici":
        return [(_flat(me), _flat(me), sibling)] + [(_flat(me), _flat(me), (*ch, c)) for ch in chips]
    if kind == "gather_d2d":
        return [(_flat((*ch, c)), _flat((*ch, c)), sibling) for ch in chips]
    if kind == "gather_direct":
        return [(_flat(me), _flat(me), sibling)] + [(_flat(me), _flat(me), (*ch, c)) for ch in chips[:2]]
    if kind == "gather_relay":
        held, to = (x ^ (1 - c), y ^ c, c), (x ^ c, y ^ (1 - c), c)
        return [(_flat(held), _flat(held), to)] + [(_flat((*ch, c)), _flat((*ch, c)), sibling) for ch in chips[:2]]
    if kind == "gather_diag":
        return [(_flat((*chips[2], c)), _flat((*chips[2], c)), sibling)]
    if kind == "reduce_d2d":
        return [(2 * chip + (1 - c), chip, sibling) for chip in range(4)]
    return [(2 * ch[0] + ch[1], 2 * x + y, (*ch, c)) for ch in chips]


def _planned_copies(kind, srcs, dsts, send_sems, recv_sems):
    plan = _copy_plan(kind)
    return [pltpu.make_async_remote_copy(
        src_ref=src.at[s_slot], dst_ref=dst.at[d_slot],
        send_sem=send_sems.at[q * len(plan) + k], recv_sem=recv_sems.at[q * len(plan) + k],
        device_id=to, device_id_type=MESH)
        for q, (src, dst) in enumerate(zip(srcs, dsts)) for k, (s_slot, d_slot, to) in enumerate(plan)]


def _remote_start(name, kind, srcs, lands=None, deps=()):
    n = len(srcs)
    bufs = list(srcs) + ([] if lands is None else list(lands))
    nb, nd = len(bufs), len(deps)
    nsem = n * N_COPIES[kind]

    def body(*refs):
        ins = refs[:nb]
        send_sems, recv_sems = refs[nb + nd], refs[nb + nd + 1]
        token = refs[-1]
        for cp in _planned_copies(kind, ins[:n], ins[:n] if lands is None else ins[n:], send_sems, recv_sems):
            cp.start()
        token[...] = jnp.zeros_like(token)

    outs = pl.pallas_call(
        body, name=name,
        out_shape=(pltpu.SemaphoreType.DMA((nsem,)), pltpu.SemaphoreType.DMA((nsem,)),
                   *[pltpu.HBM(b.shape, b.dtype) for b in bufs], jax.ShapeDtypeStruct((SUB, LANE), F32)),
        in_specs=[HBM_SPEC] * nb + [ANY] * nd,
        out_specs=(SEM_SPEC, SEM_SPEC, *[HBM_SPEC] * nb, pl.BlockSpec(memory_space=pltpu.VMEM)),
        input_output_aliases={i: 2 + i for i in range(nb)},
        compiler_params=pltpu.CompilerParams(has_side_effects=EFFECT),
    )(*[pltpu.with_memory_space_constraint(b, pltpu.HBM) for b in bufs], *deps)
    return outs[0], outs[1], list(outs[2:2 + nb]), outs[-1]


def _remote_wait(name, kind, send_sems, recv_sems, bufs, n, after):
    nb, na = len(bufs), len(after)
    same = nb == n

    def body(*refs):
        ins = refs[:nb]
        sends, recvs = refs[nb], refs[nb + 1]
        for cp in _planned_copies(kind, ins[:n], ins[:n] if same else ins[n:], sends, recvs):
            cp.wait_send()
            cp.wait_recv()

    outs = pl.pallas_call(
        body, name=name,
        out_shape=[pltpu.HBM(b.shape, b.dtype) for b in bufs],
        in_specs=[HBM_SPEC] * nb + [SEM_SPEC, SEM_SPEC] + [ANY] * na,
        out_specs=[HBM_SPEC] * nb,
        input_output_aliases={i: i for i in range(nb)},
        compiler_params=pltpu.CompilerParams(has_side_effects=EFFECT),
    )(*bufs, send_sems, recv_sems, *after)
    return list(outs)


def _remote_pass_on(name, done, send_sems, recv_sems, bufs, after, nxt):
    nb, na = len(bufs), len(after)
    nsem = nb * N_COPIES[nxt]

    def body(*refs):
        ins = refs[:nb]
        new_sends, new_recvs = refs[nb + 2 + na], refs[nb + 3 + na]
        token = refs[-1]
        for cp in _planned_copies(done, ins, ins, refs[nb], refs[nb + 1]):
            cp.wait_send()
            cp.wait_recv()
        for cp in _planned_copies(nxt, ins, ins, new_sends, new_recvs):
            cp.start()
        token[...] = jnp.zeros_like(token)

    outs = pl.pallas_call(
        body, name=name,
        out_shape=(pltpu.SemaphoreType.DMA((nsem,)), pltpu.SemaphoreType.DMA((nsem,)),
                   *[pltpu.HBM(b.shape, b.dtype) for b in bufs], jax.ShapeDtypeStruct((SUB, LANE), F32)),
        in_specs=[HBM_SPEC] * nb + [SEM_SPEC, SEM_SPEC] + [ANY] * na,
        out_specs=(SEM_SPEC, SEM_SPEC, *[HBM_SPEC] * nb, pl.BlockSpec(memory_space=pltpu.VMEM)),
        input_output_aliases={i: 2 + i for i in range(nb)},
        compiler_params=pltpu.CompilerParams(has_side_effects=EFFECT),
    )(*bufs, send_sems, recv_sems, *after)
    return outs[0], outs[1], list(outs[2:2 + nb]), outs[-1]


def _mm_cols(name, a, w, *, tm, blocks, epilogue, out_dtypes, into=(), deps=()):
    t, k = a.shape
    nblk, _, cb = w.shape
    j0, j1 = blocks
    no = len(out_dtypes)

    def body(a_ref, w_ref, *rest):
        acc = jnp.dot(a_ref[...], w_ref[0], preferred_element_type=F32)
        for o_ref, o in zip(rest[len(into) + len(deps):], epilogue(acc)):
            o_ref[...] = o.astype(o_ref.dtype)

    return pl.pallas_call(
        body, name=name, grid=(j1 - j0, t // tm),
        in_specs=[pl.BlockSpec((tm, k), lambda j, i: (i, 0)),
                  pl.BlockSpec((1, k, cb), lambda j, i: (j0 + j, 0, 0))] + [ANY] * (len(into) + len(deps)),
        out_specs=[pl.BlockSpec((tm, cb), lambda j, i: (i, j0 + j)) for _ in range(no)],
        out_shape=[jax.ShapeDtypeStruct((t, nblk * cb), dt) for dt in out_dtypes],
        input_output_aliases={2 + idx: idx for idx in range(len(into))},
        compiler_params=_params(2),
    )(a, w, *into, *deps)


MXU_WIDTH = 256


def _mm_cols_pairs(name, a, w, *, tm):
    t, k = a.shape
    nblk, _, cb = w.shape
    main = cb // MXU_WIDTH * MXU_WIDTH
    tail = cb - main
    assert 2 * tail == MXU_WIDTH and nblk % 2 == 0

    def body(a_ref, w_ref, o_ref):
        av = a_ref[...]
        for b in range(2):
            o_ref[:, b * cb:b * cb + main] = jnp.dot(av, w_ref[b, :, 0:main], preferred_element_type=F32)
        tails = jnp.dot(av, jnp.concatenate([w_ref[0, :, main:cb], w_ref[1, :, main:cb]], axis=1),
                        preferred_element_type=F32)
        for b in range(2):
            o_ref[:, b * cb + main:(b + 1) * cb] = tails[:, b * tail:(b + 1) * tail]

    return pl.pallas_call(
        body, name=name, grid=(nblk // 2, t // tm),
        in_specs=[pl.BlockSpec((tm, k), lambda j, i: (i, 0)),
                  pl.BlockSpec((2, k, cb), lambda j, i: (j, 0, 0))],
        out_specs=pl.BlockSpec((tm, 2 * cb), lambda j, i: (i, j)),
        out_shape=jax.ShapeDtypeStruct((t, nblk * cb), F32),
        compiler_params=_params(2),
    )(a, w)


def _add_columns(o_ref, j, tn, acc, first):
    for jj in range(o_ref.shape[1] // tn):
        cols = slice(jj * tn, (jj + 1) * tn)

        @pl.when(jnp.logical_and(j == jj, first))
        def _(cols=cols):
            o_ref[:, cols] = acc

        @pl.when(jnp.logical_and(j == jj, jnp.logical_not(first)))
        def _(cols=cols):
            o_ref[:, cols] += acc


def _mm_rows(name, a, w2d, *, tm, tn, kparts=2):
    t, kf = a.shape
    n = w2d.shape[1]
    kp = kf // kparts

    def body(a_ref, w_ref, o_ref):
        acc = jnp.dot(a_ref[...], w_ref[...], preferred_element_type=F32)
        _add_columns(o_ref, pl.program_id(2), tn, acc, pl.program_id(1) == 0)

    return pl.pallas_call(
        body, name=name, grid=(t // tm, kparts, n // tn),
        in_specs=[pl.BlockSpec((tm, kp), lambda i, kh, j: (i, kh)),
                  pl.BlockSpec((kp, tn), lambda i, kh, j: (kh, j))],
        out_specs=pl.BlockSpec((tm, n), lambda i, kh, j: (i, 0)),
        out_shape=jax.ShapeDtypeStruct((t, n), F32),
        compiler_params=_params(3),
    )(a, w2d)


def _mm_nt_acc_parts(name, dy, w, *, tm, tn, kparts=2, deps=()):
    t = dy.shape[0]
    nblk, k, cb = w.shape
    per = nblk // kparts
    assert cb % MXU_WIDTH == 0

    def body(dy_ref, w_ref, *rest):
        acc = None
        for b in range(per):
            d = lax.dot_general(dy_ref[:, b * cb:(b + 1) * cb], w_ref[b], (((1,), (1,)), ((), ())),
                                preferred_element_type=F32)
            acc = d if acc is None else acc + d
        _add_columns(rest[-1], pl.program_id(2), tn, acc, pl.program_id(1) == 0)

    return pl.pallas_call(
        body, name=name, grid=(t // tm, kparts, k // tn),
        in_specs=[pl.BlockSpec((tm, per * cb), lambda i, kh, j: (i, kh)),
                  pl.BlockSpec((per, tn, cb), lambda i, kh, j: (kh, j, 0))] + [ANY] * len(deps),
        out_specs=pl.BlockSpec((tm, k), lambda i, kh, j: (i, 0)),
        out_shape=jax.ShapeDtypeStruct((t, k), F32),
        compiler_params=_params(3),
    )(dy, w, *deps)


def _mm_nt_acc(name, dy, w, *, tm, tn, col_off=0, deps=()):
    t = dy.shape[0]
    nblk, k, cb = w.shape

    main = cb // MXU_WIDTH * MXU_WIDTH

    def body(dy_ref, w_ref, *rest):
        nt = (((1,), (1,)), ((), ()))
        acc = None
        for b in range(nblk):
            d = lax.dot_general(dy_ref[:, b * cb:b * cb + main], w_ref[b, :, 0:main], nt, preferred_element_type=F32)
            acc = d if acc is None else acc + d
        if main < cb:
            dy_tails = jnp.concatenate([dy_ref[:, b * cb + main:(b + 1) * cb] for b in range(nblk)], axis=1)
            w_tails = jnp.concatenate([w_ref[b, :, main:cb] for b in range(nblk)], axis=1)
            acc = acc + lax.dot_general(dy_tails, w_tails, nt, preferred_element_type=F32)
        rest[-1][...] = acc

    return pl.pallas_call(
        body, name=name, grid=(t // tm, k // tn),
        in_specs=[pl.BlockSpec((tm, nblk * cb), lambda i, j: (i, col_off)),
                  pl.BlockSpec((nblk, tn, cb), lambda i, j: (0, j, 0))] + [ANY] * len(deps),
        out_specs=pl.BlockSpec((tm, tn), lambda i, j: (i, j)),
        out_shape=jax.ShapeDtypeStruct((t, k), F32),
        compiler_params=_params(2),
    )(dy, w, *deps)


def _mm_nt_blocks(name, dy, w2d, *, tm, tkb, extra=(), epilogue=None, out_dtypes=(F32,)):
    t, n = dy.shape
    kf = w2d.shape[0]
    ne = len(extra)

    def body(dy_ref, w_ref, *rest):
        acc = lax.dot_general(dy_ref[...], w_ref[...], (((1,), (1,)), ((), ())), preferred_element_type=F32)
        outs = (acc,) if epilogue is None else epilogue(acc, *[e[...] for e in rest[:ne]])
        for o_ref, o in zip(rest[ne:], outs):
            o_ref[...] = o.astype(o_ref.dtype)

    return pl.pallas_call(
        body, name=name, grid=(kf // tkb, t // tm),
        in_specs=[pl.BlockSpec((tm, n), lambda kb, i: (i, 0)),
                  pl.BlockSpec((tkb, n), lambda kb, i: (kb, 0))]
                 + [pl.BlockSpec((tm, tkb), lambda kb, i: (i, kb)) for _ in extra],
        out_specs=[pl.BlockSpec((tm, tkb), lambda kb, i: (i, kb)) for _ in out_dtypes],
        out_shape=[jax.ShapeDtypeStruct((t, kf), dt) for dt in out_dtypes],
        compiler_params=_params(2),
    )(dy, w2d, *extra)


def _mm_tn(name, a, b, me_arr, *, m, n, tma, tn, sharded, a_off=0, b_off=0, deps=()):
    t = a.shape[0]
    if sharded == "cols":
        cb = n // N_DEV
        nb, q = max(tn // cb, 1), max(cb // tn, 1)
        tw = tn // nb
        full_shape, own_shape = (N_DEV, m, cb), (m, cb)
        full_spec = pl.BlockSpec((nb, tma, tw), lambda i, j, me: (j // q, i, j % q))
    else:
        kb = m // N_DEV
        p = kb // tma
        nb, tw = 1, tn
        full_shape, own_shape = (m, n), (kb, n)
        full_spec = pl.BlockSpec((tma, tn), lambda i, j, me: (i, j))

    def body(me_ref, a_ref, b_ref, *rest):
        full_ref, own_ref, stage, sem, pending = rest[len(deps):]
        i, j = pl.program_id(0), pl.program_id(1)

        def own_copy(r0, c0):
            return pltpu.make_async_copy(
                stage, own_ref.at[pl.ds(pl.multiple_of(r0, tma), tma), pl.ds(pl.multiple_of(c0, tw), tw)], sem)

        def drain():
            @pl.when(pending[0] == 1)
            def _():
                own_copy(0, 0).wait()
                pending[0] = 0

        @pl.when(jnp.logical_and(i == 0, j == 0))
        def _():
            pending[0] = 0

        acc = lax.dot_general(a_ref[...], b_ref[...], (((0,), (0,)), ((), ())), preferred_element_type=F32)
        for blk in range(nb):
            part = acc[:, blk * tw:(blk + 1) * tw]
            if sharded == "cols":
                full_ref[blk] = part.astype(BF16)
                owner, r0, c0 = (j // q) * nb + blk, i * tma, (j % q) * tw
            else:
                full_ref[...] = part.astype(BF16)
                owner, r0, c0 = i // p, (i % p) * tma, j * tn

            @pl.when(owner == me_ref[0])
            def _():
                drain()
                stage[...] = part
                own_copy(r0, c0).start()
                pending[0] = 1

        @pl.when(jnp.logical_and(i == pl.num_programs(0) - 1, j == pl.num_programs(1) - 1))
        def _():
            drain()

    full, own = pl.pallas_call(
        body, name=name,
        grid_spec=pltpu.PrefetchScalarGridSpec(
            num_scalar_prefetch=1, grid=(m // tma, n // tn),
            in_specs=[pl.BlockSpec((t, tma), lambda i, j, me: (0, a_off + i)),
                      pl.BlockSpec((t, tn), lambda i, j, me: (0, b_off + j))] + [ANY] * len(deps),
            out_specs=[full_spec, ANY],
            scratch_shapes=[pltpu.VMEM((tma, tw), F32), pltpu.SemaphoreType.DMA(()), pltpu.SMEM((1,), jnp.int32)]),
        out_shape=[jax.ShapeDtypeStruct(full_shape, BF16), jax.ShapeDtypeStruct(own_shape, F32)],
        compiler_params=_params(2),
    )(me_arr, a, b, *deps)
    if sharded == "rows":
        full = full.reshape(N_DEV, m // N_DEV, n)
    return full, own


def _row_tile(t):
    return t // 8 if (t // 8) % 16 == 0 else ROW_TILE


def _row_call(name, body, t, row_ins, full_ins, row_outs, acc_outs, scratch=(), deps=()):
    tm = _row_tile(t)
    nin = len(row_ins) + len(full_ins)

    def without_deps(*refs):
        body(*refs[:nin], *refs[nin + len(deps):])

    return pl.pallas_call(
        without_deps, name=name, grid=(t // tm,),
        in_specs=[pl.BlockSpec((tm, a.shape[1]), lambda i: (i, 0)) for a in row_ins]
                 + [pl.BlockSpec(a.shape, lambda i: (0, 0)) for a in full_ins] + [ANY] * len(deps),
        out_specs=[pl.BlockSpec((tm, c), lambda i: (i, 0)) for c, _ in row_outs]
                  + [pl.BlockSpec((r, c), lambda i: (0, 0)) for r, c in acc_outs],
        out_shape=[jax.ShapeDtypeStruct((t, c), dt) for c, dt in row_outs]
                  + [jax.ShapeDtypeStruct((r, c), F32) for r, c in acc_outs],
        scratch_shapes=list(scratch),
        compiler_params=_params(1),
    )(*row_ins, *full_ins, *deps)


def _accumulate(ref, v):
    @pl.when(pl.program_id(0) == 0)
    def _():
        ref[...] = v

    @pl.when(pl.program_id(0) > 0)
    def _():
        ref[...] += v


def _rms(v):
    return lax.rsqrt(jnp.mean(v * v, axis=-1, keepdims=True) + RMS_EPS)


def _rms_bwd(dout, u, r, g):
    du = dout * g
    dx = r * (du - u * jnp.mean(du * u, axis=-1, keepdims=True))
    return dx, _colsum8(dout * u)


def _pre_norm(h0, g):
    t, d = h0.shape

    def body(h_ref, g_ref, n_ref):
        h = h_ref[...]
        n_ref[...] = (h * _rms(h) * g_ref[...]).astype(BF16)

    return _row_call("pre_norm", body, t, [h0], [g], [(d, BF16)], [])[0]


def _mix_post(m_mix, wo_full, h0, g_post, g_pre, deps=()):
    t, d = h0.shape
    tm = _row_tile(t)

    def body(m_ref, wo_ref, h0_ref, gp_ref, gq_ref, *rest):
        mix_ref, h1_ref, n2_ref = rest[len(deps):]
        mix_v = jnp.dot(m_ref[...], wo_ref[...], preferred_element_type=F32)
        mix_ref[...] = mix_v
        h1 = h0_ref[...] + mix_v * _rms(mix_v) * gp_ref[...]
        h1_ref[...] = h1
        n2_ref[...] = (h1 * _rms(h1) * gq_ref[...]).astype(BF16)

    tile = pl.BlockSpec((tm, d), lambda i: (i, 0))
    gain = pl.BlockSpec((1, d), lambda i: (0, 0))
    return pl.pallas_call(
        body, name="mix_post", grid=(t // tm,),
        in_specs=[tile, pl.BlockSpec((d, d), lambda i: (0, 0)), tile, gain, gain] + [ANY] * len(deps),
        out_specs=[tile, tile, tile],
        out_shape=[jax.ShapeDtypeStruct((t, d), F32), jax.ShapeDtypeStruct((t, d), F32),
                   jax.ShapeDtypeStruct((t, d), BF16)],
        compiler_params=_params(1),
    )(m_mix, wo_full, h0, g_post, g_pre, *deps)


def _loss_head(fo, h1, tgt, g_post_mlp, t_real):
    t, d = h1.shape
    tile = _row_tile(t)

    def body(fo_ref, h1_ref, tgt_ref, g_ref, dfo_ref, dh2_ref, dg_ref, loss_ref, lacc):
        i = pl.program_id(0)
        fo_v = fo_ref[...]
        g = g_ref[...]
        r = _rms(fo_v)
        u = fo_v * r
        h2 = h1_ref[...] + u * g
        row = i * tile + lax.broadcasted_iota(jnp.int32, (tile, 1), 0)
        valid = jnp.logical_and(row >= N_META, row < t_real)
        diff = jnp.where(valid, h2 - tgt_ref[...], 0.0)
        dh2 = diff * (1.0 / d)
        dh2_ref[...] = dh2
        dfo, dg = _rms_bwd(dh2, u, r, g)
        dfo_ref[...] = dfo.astype(BF16)
        _accumulate(dg_ref, dg)
        _accumulate(lacc, _colsum8(diff * diff))

        @pl.when(i == pl.num_programs(0) - 1)
        def _():
            loss_ref[...] = jnp.full((SUB, LANE), (0.5 / d) * jnp.sum(lacc[...]), F32)

    return _row_call("loss_head", body, t, [fo, h1, tgt], [g_post_mlp],
                     [(d, BF16), (d, F32)], [(SUB, d), (SUB, LANE)], scratch=[pltpu.VMEM((SUB, d), F32)])


def _mid_norm_bwd(dn2, h1, dh2, mix, g_pre_mlp, g_post_mix, deps=()):
    t, d = h1.shape

    def body(dn2_ref, h1_ref, dh2_ref, mix_ref, gq_ref, gp_ref, dh1_ref, dmix_ref, dgq_ref, dgp_ref):
        h1 = h1_ref[...]
        r3 = _rms(h1)
        dx, dgq = _rms_bwd(dn2_ref[...], h1 * r3, r3, gq_ref[...])
        dh1 = dh2_ref[...] + dx
        dh1_ref[...] = dh1
        mix_v = mix_ref[...]
        r2 = _rms(mix_v)
        dmix, dgp = _rms_bwd(dh1, mix_v * r2, r2, gp_ref[...])
        dmix_ref[...] = dmix.astype(BF16)
        _accumulate(dgq_ref, dgq)
        _accumulate(dgp_ref, dgp)

    return _row_call("mid_norm_bwd", body, t, [dn2, h1, dh2, mix], [g_pre_mlp, g_post_mix],
                     [(d, F32), (d, BF16)], [(SUB, d), (SUB, d)], deps=deps)


def _pre_norm_bwd(dn, h0, dh1, g_pre_mix, t_real):
    t, d = h0.shape
    tm = _row_tile(t)
    nt = t // tm
    seq = t_real - N_META
    tail = t_real - (nt - 1) * tm
    assert tm > N_META and N_META % SUB == 0 and 0 < tail <= tm and tail % SUB == 0

    def body(dn_ref, h0_ref, dh1_ref, g_ref, gx_ref, dmeta_ref, dg_ref, stage, sem):
        i = pl.program_id(0)
        h0 = h0_ref[...]
        r = _rms(h0)
        dx, dg = _rms_bwd(dn_ref[...], h0 * r, r, g_ref[...])
        dh0 = dh1_ref[...] + dx
        _accumulate(dg_ref, dg)

        def copy(rows, src0, dst0):
            return pltpu.make_async_copy(stage.at[pl.ds(src0, rows), :], gx_ref.at[pl.ds(dst0, rows), :], sem)

        @pl.when(i == 1)
        def _():
            copy(tm - N_META, N_META, 0).wait()

        @pl.when(i > 1)
        def _():
            copy(tm, 0, 0).wait()

        stage[...] = dh0

        @pl.when(i == 0)
        def _():
            dmeta_ref[...] = dh0[:N_META]
            copy(tm - N_META, N_META, 0).start()

        @pl.when(jnp.logical_and(i > 0, i < nt - 1))
        def _():
            copy(tm, 0, pl.multiple_of(i * tm - N_META, SUB)).start()

        @pl.when(i == nt - 1)
        def _():
            last = copy(tail, 0, (nt - 1) * tm - N_META)
            last.start()
            last.wait()

    tile = pl.BlockSpec((tm, d), lambda i: (i, 0))
    return pl.pallas_call(
        body, name="pre_norm_bwd", grid=(nt,),
        in_specs=[tile, tile, tile, pl.BlockSpec((1, d), lambda i: (0, 0))],
        out_specs=[ANY, pl.BlockSpec((N_META, d), lambda i: (0, 0)), pl.BlockSpec((SUB, d), lambda i: (0, 0))],
        out_shape=[jax.ShapeDtypeStruct((seq, d), F32), jax.ShapeDtypeStruct((N_META, d), F32),
                   jax.ShapeDtypeStruct((SUB, d), F32)],
        scratch_shapes=[pltpu.VMEM((tm, d), F32), pltpu.SemaphoreType.DMA(())],
        compiler_params=_params(1),
    )(dn, h0, dh1, g_pre_mix)


def _layer_norm_silu(a1, ln_g, ln_b, deps=()):
    t, c = a1.shape

    def body(a1_ref, g_ref, b_ref, a3_ref):
        a = a1_ref[...]
        mu = jnp.mean(a, axis=-1, keepdims=True)
        xc = a - mu
        rstd = lax.rsqrt(jnp.mean(xc * xc, axis=-1, keepdims=True) + LN_EPS)
        z = xc * rstd * g_ref[...] + b_ref[...]
        a3_ref[...] = (z * _sigmoid(z)).astype(BF16)

    return _row_call("layer_norm_silu", body, t, [a1], [ln_g, ln_b], [(c, BF16)], [], deps=deps)[0]


def _layer_norm_silu_bwd(da3, a1, ln_g, ln_b, deps=()):
    t, c = a1.shape

    def body(da3_ref, a1_ref, g_ref, b_ref, da1_ref, dg_ref, db_ref):
        a = a1_ref[...]
        g = g_ref[...]
        mu = jnp.mean(a, axis=-1, keepdims=True)
        xc = a - mu
        rstd = lax.rsqrt(jnp.mean(xc * xc, axis=-1, keepdims=True) + LN_EPS)
        xhat = xc * rstd
        z = xhat * g + b_ref[...]
        sg = _sigmoid(z)
        dz = da3_ref[...] * (sg * (1.0 + z * (1.0 - sg)))
        dxhat = dz * g
        da1_ref[...] = rstd * (dxhat - jnp.mean(dxhat, axis=-1, keepdims=True)
                               - xhat * jnp.mean(dxhat * xhat, axis=-1, keepdims=True))
        _accumulate(dg_ref, _colsum8(dz * xhat))
        _accumulate(db_ref, _colsum8(dz))

    return _row_call("layer_norm_silu_bwd", body, t, [da3, a1], [ln_g, ln_b], [(c, F32)], [(SUB, c), (SUB, c)], deps=deps)


def _branch_merge(a3, s, wpw, wso, proj, b_gates, d, deps=()):
    t, cols = proj.shape
    nblk, k, cb = wpw.shape
    w = 1024
    nh = d // w
    per = w // cb
    ga0 = (cols - 2 * d) // w
    tm = _row_tile(t)

    def body(a3_ref, s_ref, wpw_ref, wso_ref, *rest):
        pa_refs, pb_refs, bg_ref = rest[:nh], rest[nh:2 * nh], rest[2 * nh]
        ya_ref, yb_ref, ga_ref, gb_ref, m_ref = rest[2 * nh + 1 + len(deps):]
        a3v, sv = a3_ref[...], s_ref[...]
        for b in range(nblk):
            here = slice(b * cb, (b + 1) * cb)
            local = slice((b % per) * cb, (b % per + 1) * cb)
            ya = jnp.dot(a3v, wpw_ref[b], preferred_element_type=F32)
            yb = jnp.dot(sv, wso_ref[b], preferred_element_type=F32)
            ga = _sigmoid(pa_refs[b // per][:, local] + bg_ref[:, here])
            gb = _sigmoid(pb_refs[b // per][:, local] + bg_ref[:, d + b * cb:d + (b + 1) * cb])
            ya_ref[:, here] = ya.astype(BF16)
            yb_ref[:, here] = yb.astype(BF16)
            ga_ref[:, here] = ga.astype(BF16)
            gb_ref[:, here] = gb.astype(BF16)
            m_ref[:, here] = (ga * ya + gb * yb).astype(BF16)

    tile = pl.BlockSpec((tm, d), lambda i: (i, 0))
    return pl.pallas_call(
        body, name="branch_merge", grid=(t // tm,),
        in_specs=[pl.BlockSpec((tm, k), lambda i: (i, 0)), pl.BlockSpec((tm, k), lambda i: (i, 0)),
                  pl.BlockSpec((nblk, k, cb), lambda i: (0, 0, 0)), pl.BlockSpec((nblk, k, cb), lambda i: (0, 0, 0))]
                 + [pl.BlockSpec((tm, w), lambda i, h=h: (i, ga0 + h)) for h in range(2 * nh)]
                 + [pl.BlockSpec((1, 2 * d), lambda i: (0, 0))] + [ANY] * len(deps),
        out_specs=[tile] * 5,
        out_shape=[jax.ShapeDtypeStruct((t, d), BF16)] * 5,
        compiler_params=_params(1),
    )(a3, s, wpw, wso, *([proj] * (2 * nh)), b_gates, *deps)


def _gate_backward(dmix, wo_full, ga, gb, ya, yb, cols, tm, deps=()):
    t, d = ya.shape
    w = 1024
    nh = d // w
    ga0 = (cols - 2 * d) // w

    def body(dmix_ref, wo_ref, ga_ref, gb_ref, ya_ref, yb_ref, *rest):
        dya_ref, dyb_ref, dp_ref, dba_ref, dbb_ref, stage, sems = rest[len(deps):]
        h, i = pl.program_id(0), pl.program_id(1)
        dm = lax.dot_general(dmix_ref[...], wo_ref[...], (((1,), (1,)), ((), ())), preferred_element_type=F32)
        ga = ga_ref[...].astype(F32)
        gb = gb_ref[...].astype(F32)
        dya_ref[...] = (dm * ga).astype(BF16)
        dyb_ref[...] = (dm * gb).astype(BF16)
        dpa = dm * ya_ref[...].astype(F32) * ga * (1.0 - ga)
        dpb = dm * yb_ref[...].astype(F32) * gb * (1.0 - gb)

        def copies(row0, colblk):
            return [pltpu.make_async_copy(
                stage.at[g], dp_ref.at[pl.ds(pl.multiple_of(row0, tm), tm),
                                       pl.ds(pl.multiple_of((ga0 + g * nh + colblk) * w, w), w)], sems.at[g])
                for g in range(2)]

        @pl.when(jnp.logical_or(h > 0, i > 0))
        def _():
            for cp in copies(0, 0):
                cp.wait()

        stage[0] = dpa.astype(BF16)
        stage[1] = dpb.astype(BF16)
        for cp in copies(i * tm, h):
            cp.start()

        @pl.when(i == 0)
        def _():
            dba_ref[...] = _colsum8(dpa)
            dbb_ref[...] = _colsum8(dpb)

        @pl.when(i > 0)
        def _():
            dba_ref[...] += _colsum8(dpa)
            dbb_ref[...] += _colsum8(dpb)

        @pl.when(jnp.logical_and(h == pl.num_programs(0) - 1, i == pl.num_programs(1) - 1))
        def _():
            for cp in copies(0, 0):
                cp.wait()

    tile = pl.BlockSpec((tm, w), lambda h, i: (i, h))
    return pl.pallas_call(
        body, name="gate_backward", grid=(nh, t // tm),
        in_specs=[pl.BlockSpec((tm, d), lambda h, i: (i, 0)),
                  pl.BlockSpec((w, d), lambda h, i: (h, 0)),
                  tile, tile, tile, tile] + [ANY] * len(deps),
        out_specs=[tile, tile, ANY,
                   pl.BlockSpec((SUB, w), lambda h, i: (0, h)),
                   pl.BlockSpec((SUB, w), lambda h, i: (0, h))],
        out_shape=[jax.ShapeDtypeStruct((t, d), BF16), jax.ShapeDtypeStruct((t, d), BF16),
                   jax.ShapeDtypeStruct((t, cols), BF16),
                   jax.ShapeDtypeStruct((SUB, d), F32), jax.ShapeDtypeStruct((SUB, d), F32)],
        scratch_shapes=[pltpu.VMEM((2, tm, w), BF16), pltpu.SemaphoreType.DMA((2,))],
        compiler_params=_params(2),
    )(dmix, wo_full, ga, gb, ya, yb, *deps)


def _shifted_views(win, offsets):
    n = win.shape[0]
    rotated = {}
    views = {}
    for o in offsets:
        q, r = divmod(o, SUB)
        if r not in rotated:
            rotated[r] = win if r == 0 else pltpu.roll(win, n - r, 0)
        views[o] = rotated[r][q * SUB:q * SUB + CONV_CHUNK]
    return views


def _causal_views(xp_ref, ntap, r0):
    win = xp_ref[pl.ds(r0, CONV_CHUNK + CONV_PAD), :]
    views = _shifted_views(win, [CONV_PAD - (ntap - 1 - k) for k in range(ntap)])
    return [views[CONV_PAD - (ntap - 1 - k)] for k in range(ntap)]


def _causal_conv(xp_ref, w_ref, ntap, r0):
    acc = None
    for k, shifted in enumerate(_causal_views(xp_ref, ntap, r0)):
        term = w_ref[k:k + 1, :] * shifted
        acc = term if acc is None else acc + term
    return acc


def _anticausal_conv(xp_ref, w_ref, ntap, r0):
    win = xp_ref[pl.ds(pl.multiple_of(CONV_PAD + r0, CONV_PAD), CONV_CHUNK + CONV_PAD), :]
    views = _shifted_views(win, [ntap - 1 - k for k in range(ntap)])
    acc = None
    for k in range(ntap):
        term = w_ref[k:k + 1, :] * views[ntap - 1 - k]
        acc = term if acc is None else acc + term
    return acc


def _conv_weight_grad(dw_ref, d_chunk, xp_ref, ntap, r0):
    for k, shifted in enumerate(_causal_views(xp_ref, ntap, r0)):
        dw_ref[k * SUB:(k + 1) * SUB, :] += _colsum8(d_chunk * shifted)


def _zero_pads(ref, t):
    ref[0:CONV_PAD, :] = jnp.zeros((CONV_PAD, LANE), F32)
    ref[CONV_PAD + t:CONV_PAD + t + CONV_PAD, :] = jnp.zeros((CONV_PAD, LANE), F32)


def _for_chunks(t, fn):
    def step(idx, carry):
        fn(pl.multiple_of(idx * CONV_CHUNK, CONV_CHUNK))
        return carry

    lax.fori_loop(0, t // CONV_CHUNK, step, 0)


def _conv_forward(proj, conf_w, conf_b, short_w, dc, deps=()):
    t = proj.shape[0]
    nc = dc // LANE

    def body(av_ref, ag_ref, bg_ref, cg_ref, v_ref, cw_ref, cb_ref, sw_ref, *rest):
        a1_ref, s_ref, xa, xb = rest[len(deps):]
        _zero_pads(xa, t)
        _zero_pads(xb, t)
        xa[CONV_PAD:CONV_PAD + t, :] = av_ref[...] * _sigmoid(ag_ref[...])
        xb[CONV_PAD:CONV_PAD + t, :] = cg_ref[...] * v_ref[...]

        def chunk(r0):
            rs = pl.ds(r0, CONV_CHUNK)
            a1_ref[rs, :] = _causal_conv(xa, cw_ref, CONF_K, r0) + cb_ref[...]
            s_ref[rs, :] = (bg_ref[rs, :] * _causal_conv(xb, sw_ref, SHORT_K, r0)).astype(BF16)

        _for_chunks(t, chunk)

    col = lambda g: pl.BlockSpec((t, LANE), lambda c, g=g: (0, g * nc + c))
    return pl.pallas_call(
        body, name="conv_forward", grid=(nc,),
        in_specs=[col(0), col(1), col(2), col(3), col(4),
                  pl.BlockSpec((CONF_K, LANE), lambda c: (0, c)),
                  pl.BlockSpec((1, LANE), lambda c: (0, c)),
                  pl.BlockSpec((SHORT_K, LANE), lambda c: (0, c))] + [ANY] * len(deps),
        out_specs=[pl.BlockSpec((t, LANE), lambda c: (0, c)), pl.BlockSpec((t, LANE), lambda c: (0, c))],
        out_shape=[jax.ShapeDtypeStruct((t, dc), F32), jax.ShapeDtypeStruct((t, dc), BF16)],
        scratch_shapes=[pltpu.VMEM((t + 2 * CONV_PAD, LANE), F32), pltpu.VMEM((t + 2 * CONV_PAD, LANE), F32)],
        compiler_params=_params(1),
    )(proj, proj, proj, proj, proj, conf_w, conf_b, short_w, *deps)


def _conv_backward(dproj, proj, da1, ds, conf_w, short_w, dc):
    t = proj.shape[0]
    nc = dc // LANE

    def body(dp_in, av_ref, ag_ref, bg_ref, cg_ref, v_ref, da1_ref, ds_ref, cw_ref, sw_ref,
             dp_ref, dcw_ref, dcb_ref, dsw_ref, xa, xb, da, db, stage, sems):
        del dp_in
        c = pl.program_id(0)
        for ref in (xa, xb, da, db):
            _zero_pads(ref, t)
        xa[CONV_PAD:CONV_PAD + t, :] = av_ref[...] * _sigmoid(ag_ref[...])
        xb[CONV_PAD:CONV_PAD + t, :] = cg_ref[...] * v_ref[...]
        da[CONV_PAD:CONV_PAD + t, :] = da1_ref[...]
        dcw_ref[...] = jnp.zeros(dcw_ref.shape, F32)
        dsw_ref[...] = jnp.zeros(dsw_ref.shape, F32)
        dcb_ref[...] = jnp.zeros(dcb_ref.shape, F32)

        def copies(colblk):
            return [pltpu.make_async_copy(
                stage.at[g], dp_ref.at[:, pl.ds(pl.multiple_of((g * nc + colblk) * LANE, LANE), LANE)], sems.at[g])
                for g in range(5)]

        @pl.when(c > 0)
        def _():
            for cp in copies(0):
                cp.wait()

        def through_gate(r0):
            rs = pl.ds(r0, CONV_CHUNK)
            ds_c = ds_ref[rs, :]
            stage[2, rs, :] = (ds_c * _causal_conv(xb, sw_ref, SHORT_K, r0)).astype(BF16)
            db[pl.ds(pl.multiple_of(CONV_PAD + r0, CONV_PAD), CONV_CHUNK), :] = ds_c * bg_ref[rs, :]

        _for_chunks(t, through_gate)

        def through_convs(r0):
            rs = pl.ds(r0, CONV_CHUNK)
            da0 = _anticausal_conv(da, cw_ref, CONF_K, r0)
            sg = _sigmoid(ag_ref[rs, :])
            stage[0, rs, :] = (da0 * sg).astype(BF16)
            stage[1, rs, :] = (da0 * av_ref[rs, :] * sg * (1.0 - sg)).astype(BF16)
            dcv = _anticausal_conv(db, sw_ref, SHORT_K, r0)
            stage[3, rs, :] = (dcv * v_ref[rs, :]).astype(BF16)
            stage[4, rs, :] = (dcv * cg_ref[rs, :]).astype(BF16)
            da1_c = da1_ref[rs, :]
            _conv_weight_grad(dcw_ref, da1_c, xa, CONF_K, r0)
            _conv_weight_grad(dsw_ref, ds_ref[rs, :] * bg_ref[rs, :], xb, SHORT_K, r0)
            dcb_ref[...] += _colsum8(da1_c)

        _for_chunks(t, through_convs)
        for cp in copies(c):
            cp.start()

        @pl.when(c == pl.num_programs(0) - 1)
        def _():
            for cp in copies(0):
                cp.wait()

    col = lambda g: pl.BlockSpec((t, LANE), lambda c, g=g: (0, g * nc + c))
    blk = pl.BlockSpec((t, LANE), lambda c: (0, c))
    return pl.pallas_call(
        body, name="conv_backward", grid=(nc,),
        in_specs=[ANY, col(0), col(1), col(2), col(3), col(4), blk, blk,
                  pl.BlockSpec((CONF_K, LANE), lambda c: (0, c)),
                  pl.BlockSpec((SHORT_K, LANE), lambda c: (0, c))],
        out_specs=[ANY,
                   pl.BlockSpec((CONF_K * SUB, LANE), lambda c: (0, c)),
                   pl.BlockSpec((SUB, LANE), lambda c: (0, c)),
                   pl.BlockSpec((SHORT_K * SUB, LANE), lambda c: (0, c))],
        out_shape=[jax.ShapeDtypeStruct(dproj.shape, dproj.dtype),
                   jax.ShapeDtypeStruct((CONF_K * SUB, dc), F32),
                   jax.ShapeDtypeStruct((SUB, dc), F32),
                   jax.ShapeDtypeStruct((SHORT_K * SUB, dc), F32)],
        scratch_shapes=[pltpu.VMEM((t + 2 * CONV_PAD, LANE), F32)] * 4
                       + [pltpu.VMEM((5, t, LANE), BF16), pltpu.SemaphoreType.DMA((5,))],
        input_output_aliases={0: 0},
        compiler_params=_params(1),
    )(dproj, proj, proj, proj, proj, proj, da1, ds, conf_w, short_w)


def _adamw_math(w, g, m, v):
    m = ADAM_B1 * m + (1.0 - ADAM_B1) * g
    v = ADAM_B2 * v + (1.0 - ADAM_B2) * (g * g)
    m_hat = m / (1.0 - ADAM_B1 ** ADAM_STEP)
    v_hat = v / (1.0 - ADAM_B2 ** ADAM_STEP)
    delta = -ADAM_LR * (m_hat / (jnp.sqrt(v_hat) + ADAM_EPS) + ADAM_WD * w)
    return delta, m, v


def _cast_into_slot(name, w, me_arr, deps=()):
    r, c = w.shape
    tr = 256

    def body(me_ref, w_ref, *rest):
        del me_ref
        rest[-1][0] = w_ref[...].astype(BF16)

    return pl.pallas_call(
        body, name=name,
        grid_spec=pltpu.PrefetchScalarGridSpec(
            num_scalar_prefetch=1, grid=(r // tr,),
            in_specs=[pl.BlockSpec((tr, c), lambda i, me: (i, 0))] + [ANY] * len(deps),
            out_specs=pl.BlockSpec((1, tr, c), lambda i, me: (me[0], i, 0))),
        out_shape=jax.ShapeDtypeStruct((N_DEV, r, c), BF16),
        compiler_params=_params(1),
    )(me_arr, w, *deps)


def _chip_sum(name, full, from_sibling, me_arr):
    _, r, c = full.shape
    tr = min(r, 1024)

    def body(me_ref, full_ref, sib_ref, sums_ref):
        del me_ref
        sums_ref[0] = (full_ref[0].astype(F32) + sib_ref[0].astype(F32)).astype(BF16)

    other = lambda k, me: (me[0] // 2 + 1 + k) % 4
    return pl.pallas_call(
        body, name=name,
        grid_spec=pltpu.PrefetchScalarGridSpec(
            num_scalar_prefetch=1, grid=(r // tr, 3),
            in_specs=[pl.BlockSpec((1, tr, c), lambda i, k, me: (2 * other(k, me) + me[0] % 2, i, 0)),
                      pl.BlockSpec((1, tr, c), lambda i, k, me: (other(k, me), i, 0))],
            out_specs=pl.BlockSpec((1, tr, c), lambda i, k, me: (other(k, me), i, 0))),
        out_shape=jax.ShapeDtypeStruct((4, r, c), BF16),
        compiler_params=_params(2),
    )(me_arr, full, from_sibling)


def _adamw_shard(name, w, m, v, parts, me_arr, deps=()):
    r, c = w.shape
    tr = min(256, r // len(parts))
    np_ = len(parts)
    per = r // np_ // tr

    def body(me_ref, w_ref, m_ref, v_ref, *rest):
        g_out, d_out, m_out, v_out = rest[5 * np_ + len(deps):]
        g = None
        for p in range(np_):
            gp = rest[5 * p][...]
            for l_ref in rest[5 * p + 1:5 * p + 5]:
                gp = gp + l_ref[0].astype(F32)
            g = gp if g is None else jnp.where(pl.program_id(0) // per == p, gp, g)
        delta, m_new, v_new = _adamw_math(w_ref[...], g, m_ref[...], v_ref[...])
        g_out[...] = g
        d_out[...] = delta
        m_out[...] = m_new
        v_out[...] = v_new

    tile = pl.BlockSpec((tr, c), lambda i, me: (i, 0))
    part_specs, part_args = [], []
    for p, (g_own, from_sibling, landed) in enumerate(parts):
        row = lambda i, p=p: jnp.clip(i - p * per, 0, per - 1)
        part_specs.append(pl.BlockSpec((tr, c), lambda i, me, row=row: (row(i), 0)))
        part_specs += [pl.BlockSpec((1, tr, c), lambda i, me, k=k, row=row: ((me[0] // 2 + k) % 4, row(i), 0))
                       for k in range(4)]
        part_args += [g_own, from_sibling, landed, landed, landed]
    return pl.pallas_call(
        body, name=name,
        grid_spec=pltpu.PrefetchScalarGridSpec(
            num_scalar_prefetch=1, grid=(r // tr,),
            in_specs=[tile] * 3 + part_specs + [ANY] * len(deps), out_specs=[tile] * 4),
        out_shape=[jax.ShapeDtypeStruct((r, c), F32)] * 4,
        compiler_params=_params(1),
    )(me_arr, w, m, v, *part_args, *deps)


SMALL_W = 1024
VEC_ROWS = 16
LOSS_ROW = 15
META_ROW0 = 16
CONF_ROW0 = 64
SHORT_ROW0 = 96
SMALL_ROWS = 104


def _pack_small(vec_parts, dmeta, dcw, dsw, loss_blk, me_arr):
    widths = [p.shape[1] for p in vec_parts]
    nv = len(vec_parts)

    def body(me_ref, *refs):
        del me_ref
        parts, (dmeta_ref, dcw_ref, dsw_ref, loss_ref, out_ref) = refs[:nv], refs[nv:]
        out_ref[0] = jnp.zeros((SMALL_ROWS, SMALL_W), F32)
        out_ref[0, LOSS_ROW:LOSS_ROW + 1, 0:LANE] = loss_ref[0:1, :]
        row = 0
        for p_ref, wd in zip(parts, widths):
            s = jnp.sum(p_ref[...], axis=0, keepdims=True)
            for h in range(wd // SMALL_W):
                out_ref[0, row:row + 1, :] = s[:, h * SMALL_W:(h + 1) * SMALL_W]
                row += 1
        for h in range(dmeta_ref.shape[1] // SMALL_W):
            out_ref[0, META_ROW0 + h * N_META:META_ROW0 + (h + 1) * N_META, :] = dmeta_ref[:, h * SMALL_W:(h + 1) * SMALL_W]
        for k in range(CONF_K):
            out_ref[0, CONF_ROW0 + k:CONF_ROW0 + k + 1, :] = jnp.sum(dcw_ref[k * SUB:(k + 1) * SUB, :], axis=0, keepdims=True)
        for k in range(SHORT_K):
            out_ref[0, SHORT_ROW0 + k:SHORT_ROW0 + k + 1, :] = jnp.sum(dsw_ref[k * SUB:(k + 1) * SUB, :], axis=0, keepdims=True)

    ins = [*vec_parts, dmeta, dcw, dsw, loss_blk]
    return pl.pallas_call(
        body, name="pack_small",
        grid_spec=pltpu.PrefetchScalarGridSpec(
            num_scalar_prefetch=1, grid=(1,),
            in_specs=[pl.BlockSpec(a.shape, lambda i, me: (0, 0)) for a in ins],
            out_specs=pl.BlockSpec((1, SMALL_ROWS, SMALL_W), lambda i, me: (me[0], 0, 0))),
        out_shape=jax.ShapeDtypeStruct((N_DEV, SMALL_ROWS, SMALL_W), F32),
        compiler_params=_params(1),
    )(me_arr, *ins)


def _small_update(gathered, me_arr, vec_params, meta_p, conf_p, short_p):
    widths = [p[0].shape[1] for p in vec_params]
    nv = len(vec_params)
    mcols = meta_p[0].shape[1]
    per_row = SMALL_W // mcols

    def body(me_ref, gv_ref, gm_ref, gc_ref, gs_ref, *rest):
        del me_ref
        ins, outs = rest[:3 * (nv + 3)], rest[3 * (nv + 3):]

        def total(ref, r0, rows):
            s = ref[0, r0:r0 + rows, :]
            for dev in range(1, N_DEV):
                s = s + ref[dev, r0:r0 + rows, :]
            return s

        grads = []
        row = 0
        for wd in widths:
            pieces = [total(gv_ref, row + h, 1) for h in range(wd // SMALL_W)]
            grads.append(pieces[0] if len(pieces) == 1 else jnp.concatenate(pieces, axis=1))
            row += len(pieces)
        grads.append(total(gm_ref, 0, N_META))
        grads.append(total(gc_ref, 0, CONF_K))
        grads.append(total(gs_ref, 0, SHORT_K))
        loss = gv_ref[0, LOSS_ROW:LOSS_ROW + 1, 0:LANE]
        for dev in range(1, N_DEV):
            loss = loss + gv_ref[dev, LOSS_ROW:LOSS_ROW + 1, 0:LANE]
        outs[-1][...] = loss
        for idx, g in enumerate(grads):
            w_ref, m_ref, v_ref = ins[3 * idx:3 * idx + 3]
            delta, m_new, v_new = _adamw_math(w_ref[...], g, m_ref[...], v_ref[...])
            g_out, d_out, m_out, v_out = outs[4 * idx:4 * idx + 4]
            g_out[...] = g
            d_out[...] = delta
            m_out[...] = m_new
            v_out[...] = v_new

    params = list(vec_params) + [meta_p, conf_p, short_p]
    flat = [a for p in params for a in p]
    whole = lambda a: pl.BlockSpec(a.shape, lambda i, me: (0,) * a.ndim)
    outs = pl.pallas_call(
        body, name="small_update",
        grid_spec=pltpu.PrefetchScalarGridSpec(
            num_scalar_prefetch=1, grid=(1,),
            in_specs=[pl.BlockSpec((N_DEV, VEC_ROWS, SMALL_W), lambda i, me: (0, 0, 0)),
                      pl.BlockSpec((N_DEV, N_META, mcols),
                                   lambda i, me: (0, META_ROW0 // N_META + me[0] // per_row, me[0] % per_row)),
                      pl.BlockSpec((N_DEV, 32, LANE), lambda i, me: (0, CONF_ROW0 // 32, me[0])),
                      pl.BlockSpec((N_DEV, SUB, LANE), lambda i, me: (0, SHORT_ROW0 // SUB, me[0]))]
                     + [whole(a) for a in flat],
            out_specs=[whole(p[0]) for p in params for _ in range(4)]
                      + [pl.BlockSpec((1, LANE), lambda i, me: (0, 0))]),
        out_shape=[jax.ShapeDtypeStruct(p[0].shape, F32) for p in params for _ in range(4)]
                  + [jax.ShapeDtypeStruct((1, LANE), F32)],
        compiler_params=_params(1),
    )(me_arr, gathered, gathered, gathered, gathered, *flat)
    return [tuple(outs[4 * i:4 * i + 4]) for i in range(len(params))], outs[-1][0, 0]


def kernel(x, meta, g_pre_mix, w_in, b_gates, conf_dw_w, conf_dw_b, conf_ln_g, conf_ln_b, conf_w_pw, short_dw_w, short_w_out, w_o, g_post_mix, g_pre_mlp, w_up, w_down, g_post_mlp, loss_target, m_meta, m_g_pre_mix, m_w_in, m_b_gates, m_conf_dw_w, m_conf_dw_b, m_conf_ln_g, m_conf_ln_b, m_conf_w_pw, m_short_dw_w, m_short_w_out, m_w_o, m_g_post_mix, m_g_pre_mlp, m_w_up, m_w_down, m_g_post_mlp, v_meta, v_g_pre_mix, v_w_in, v_b_gates, v_conf_dw_w, v_conf_dw_b, v_conf_ln_g, v_conf_ln_b, v_conf_w_pw, v_short_dw_w, v_short_w_out, v_w_o, v_g_post_mix, v_g_pre_mlp, v_w_up, v_w_down, v_g_post_mlp):
    seq, d = x.shape[1], x.shape[2]
    dc = conf_w_pw.shape[1]
    t_real = N_META + seq
    t = -(-t_real // ROW_TILE) * ROW_TILE
    tm = t // 2
    assert tm % 16 == 0 and d % 1024 == 0 and dc % 1024 == 0
    x_idx, y_idx, c_idx = _position()
    me_arr = jnp.reshape(4 * x_idx + 2 * y_idx + c_idx, (1,)).astype(jnp.int32)

    big = [w_in[0], conf_w_pw[0], short_w_out[0], w_o[0], w_up[0], w_down[0]]
    big_names = ["w_in", "conf_w_pw", "short_w_out", "w_o", "w_up", "w_down"]
    groups = [[0], [1, 2, 3], [4], [5]]
    slots, deps = [], []
    for g, idxs in enumerate(groups):
        slots.append([_cast_into_slot("cast_" + big_names[i], big[i], me_arr, deps=deps) for i in idxs])
        if g == 0:
            direct0 = _remote_start("gather0_direct_start", "gather_direct", slots[0])
            deps = [direct0[3]]
    casts = [sl for group in slots[1:] for sl in group]
    meta_g, cw_g, sw_g = _all_gather("gather_small_params", [meta, conf_dw_w[0], short_dw_w[0]], deps=casts)

    def start_direct(g, deps):
        send, recv, bufs, tok = _remote_start("gather%d_direct_start" % g, "gather_direct", slots[g], deps=deps)
        return (send, recv, bufs), tok

    def relay(g, state, after):
        send, recv, bufs, tok = _remote_pass_on("gather%d_relay" % g, "gather_direct", *state, after, "gather_relay")
        return (send, recv, bufs), tok

    def gathered(g, state, after):
        send, recv, bufs, tok = _remote_pass_on("gather%d_diag" % g, "gather_relay", *state, after, "gather_diag")
        return _remote_wait("gather%d_diag_wait" % g, "gather_diag", send, recv, bufs, len(bufs), [tok])

    unshard =lambda g: jnp.transpose(g, (1, 0, 2)).reshape(g.shape[1], -1)
    meta_full, cw_full, sw_full = unshard(meta_g), unshard(cw_g), unshard(sw_g)

    relay0, tok = relay(0, direct0[:3], [meta_g])
    zrows = jnp.zeros((t - t_real, d), F32) + tok[0, 0] * 0.0
    h0 = jnp.concatenate([meta_full, x[0], zrows], axis=0)
    tgt = jnp.concatenate([jnp.zeros((N_META, d), F32), loss_target[0], zrows], axis=0)
    n = _pre_norm(h0, g_pre_mix)
    direct1, tok = start_direct(1, [tok])
    direct2, tok = start_direct(2, [tok])
    win_g, = gathered(0, relay0, [tok, n])
    proj = _mm_cols_pairs("proj", n, win_g, tm=tm // 2)
    relay1, tok = relay(1, direct1, [proj])
    a1, s = _conv_forward(proj, cw_full, conf_dw_b, sw_full, dc, deps=[tok])
    relay2, tok = relay(2, direct2, [a1])
    direct3, tok = start_direct(3, [tok])
    a3 = _layer_norm_silu(a1, conf_ln_g, conf_ln_b, deps=[tok])
    wpw_g, wso_g, wo_g = gathered(1, relay1, [a3])
    wo_full = wo_g.reshape(d, d)
    ya, yb, gate_a, gate_b, m_mix = _branch_merge(a3, s, wpw_g, wso_g, proj, b_gates, d)
    mix, h1, n2 = _mix_post(m_mix, wo_full, h0, g_post_mix, g_pre_mlp)
    wup_g, = gathered(2, relay2, [n2])

    def up_epilogue(acc):
        r = jnp.maximum(acc, 0.0)
        return r * r, r

    half_up = dict(tm=tm, epilogue=up_epilogue, out_dtypes=(BF16, BF16))
    f, relu_up = _mm_cols("mlp_up0", n2, wup_g, blocks=(0, N_DEV // 2), **half_up)
    relay3, tok = relay(3, direct3, [f])
    f, relu_up = _mm_cols("mlp_up1", n2, wup_g, blocks=(N_DEV // 2, N_DEV), into=(f, relu_up), deps=[tok], **half_up)
    wdn_g, = gathered(3, relay3, [f])
    wdn_full = wdn_g.reshape(-1, d)
    fo = _mm_rows("mlp_down", f, wdn_full, tm=tm, tn=512)
    dfo, dh2, dg_post_mlp, loss_blk = _loss_head(fo, h1, tgt, g_post_mlp, t_real)

    def reduce_start(tag, fulls, deps):
        lands = [lax.empty((4,) + g.shape[1:], BF16) for g in fulls]
        send, recv, bufs, tok = _remote_start("reduce_%s_d2d_start" % tag, "reduce_d2d", fulls, lands, deps=deps)
        return (send, recv, bufs), tok

    def reduce_middle(tag, state, owns, after):
        send, recv, bufs = state
        k = len(owns)
        bufs = _remote_wait("reduce_%s_d2d_wait" % tag, "reduce_d2d", send, recv, bufs, k, after)
        from_sibling = bufs[k:]
        sums = [_chip_sum("chip_sum_%s%d" % (tag, i), bufs[i], from_sibling[i], me_arr) for i in range(k)]
        lands = [lax.empty(sm.shape, BF16) for sm in sums]
        send, recv, bufs, tok = _remote_start("reduce_%s_ici_start" % tag, "reduce_ici", sums, lands)
        return (send, recv, bufs, list(zip(owns, from_sibling))), tok

    def reduce_finish(tag, state, after):
        send, recv, bufs, local = state
        k = len(local)
        bufs = _remote_wait("reduce_%s_ici_wait" % tag, "reduce_ici", send, recv, bufs, k, after)
        return [(own, sib, landed) for (own, sib), landed in zip(local, bufs[k:])]

    dup = _mm_nt_blocks("d_up", dfo, wdn_full, tm=tm, tkb=1024, extra=(relu_up,),
                        epilogue=lambda acc, r: (acc * (2.0 * r.astype(F32)),), out_dtypes=(BF16,))[0]
    gw_down, gw_down_own = _mm_tn("dw_down", f, dfo, me_arr, m=f.shape[1], n=d, tma=512, tn=d, sharded="rows")
    red_down, tok = reduce_start("down", [gw_down], ())
    dn2 = _mm_nt_acc_parts("d_n2", dup, wup_g, tm=tm, tn=512, deps=[tok])
    gw_up, gw_up_own = _mm_tn("dw_up", n2, dup, me_arr, m=d, n=dup.shape[1], tma=512, tn=2048, sharded="cols")
    red_down, tok = reduce_middle("down", red_down, [gw_down_own], [dn2])
    red_up, tok = reduce_start("up", [gw_up], [tok])
    dh1, dmix, dg_pre_mlp, dg_post_mix = _mid_norm_bwd(dn2, h1, dh2, mix, g_pre_mlp, g_post_mix, deps=[tok])
    dya, dyb, dproj, db_a, db_b = _gate_backward(dmix, wo_full, gate_a, gate_b, ya, yb, proj.shape[1], tm // 2)
    db_gates = jnp.concatenate([db_a, db_b], axis=1)
    red_up, tok = reduce_middle("up", red_up, [gw_up_own], [dya])
    gw_o, gw_o_own = _mm_tn("dw_o", m_mix, dmix, me_arr, m=d, n=d, tma=d // N_DEV, tn=d, sharded="rows", deps=[tok])
    da3 = _mm_nt_acc("d_a3", dya, wpw_g, tm=tm, tn=512)
    gw_pw, gw_pw_own = _mm_tn("dw_pw", a3, dya, me_arr, m=dc, n=d, tma=512, tn=d, sharded="cols")
    dsb = _mm_nt_acc("d_s", dyb, wso_g, tm=tm, tn=512)
    gw_so, gw_so_own = _mm_tn("dw_so", s, dyb, me_arr, m=dc, n=d, tma=512, tn=d, sharded="cols")
    red_mix, tok = reduce_start("mix", [gw_pw, gw_so, gw_o], ())
    da1, dln_g, dln_b = _layer_norm_silu_bwd(da3, a1, conf_ln_g, conf_ln_b, deps=[tok])
    dproj, dcw, dcb, dsw = _conv_backward(dproj, proj, da1, dsb, cw_full, sw_full, dc)
    red_mix, tok = reduce_middle("mix", red_mix, [gw_pw_own, gw_so_own, gw_o_own], [dcb])
    in_cb = w_in.shape[2]
    half = d // 2
    red_in = []
    for part in range(2):
        gw, own = _mm_tn("dw_in%d" % part, n, dproj, me_arr, m=half, n=proj.shape[1], tma=512, tn=2 * in_cb,
                         sharded="cols", a_off=part * (half // 512), deps=[tok])
        state, tok = reduce_start("in%d" % part, [gw], ())
        red_in.append((state, own))
    for part in range(2):
        state, own = red_in[part]
        red_in[part], tok = reduce_middle("in%d" % part, state, [own], [tok])
    dn = _mm_nt_acc("d_n", dproj, win_g, tm=tm // 2, tn=512, deps=[tok])
    grad_x, dmeta, dg_pre_mix = _pre_norm_bwd(dn, h0, dh1, g_pre_mix, t_real)
    grad_x = grad_x[None]

    vec_parts = [dg_pre_mix, db_gates, dcb, dln_g, dln_b, dg_post_mix, dg_pre_mlp, dg_post_mlp]
    packed = _pack_small(vec_parts, dmeta, dcw, dsw, loss_blk, me_arr)
    send, recv, bufs, tok = _remote_start("small_grads_ici_start", "gather_ici", [packed])
    vec_names = ["g_pre_mix", "b_gates", "conf_dw_b", "conf_ln_g", "conf_ln_b", "g_post_mix", "g_pre_mlp", "g_post_mlp"]
    env = locals()
    results = {}

    def update(nm, parts, deps=()):
        res = _adamw_shard("adamw_" + nm, env[nm][0], env["m_" + nm][0], env["v_" + nm][0], parts, me_arr, deps=deps)
        results[nm] = tuple(r[None] for r in res)
        return res[0]

    done = [update("w_down", reduce_finish("down", red_down, [tok]), deps=[tok])]
    done.append(update("w_up", reduce_finish("up", red_up, done)))
    bufs = _remote_wait("small_grads_ici_wait", "gather_ici", send, recv, bufs, 1, done)
    send, recv, bufs, tok = _remote_start("small_grads_d2d_start", "gather_d2d", bufs)
    for nm, pair in zip(["conf_w_pw", "short_w_out", "w_o"], reduce_finish("mix", red_mix, [tok])):
        done.append(update(nm, [pair], deps=[tok]))
    small_g, = _remote_wait("small_grads_d2d_wait", "gather_d2d", send, recv, bufs, 1, done)
    triple = lambda nm, sq: tuple(env[p + nm][0] if sq else env[p + nm] for p in ("", "m_", "v_"))
    small, loss = _small_update(small_g, me_arr, [triple(nm, False) for nm in vec_names],
                                triple("meta", False), triple("conf_dw_w", True), triple("short_dw_w", True))
    for nm, res in zip(vec_names + ["meta"], small[:len(vec_names) + 1]):
        results[nm] = res
    results["conf_dw_w"] = tuple(r[None] for r in small[-2])
    results["short_dw_w"] = tuple(r[None] for r in small[-1])
    update("w_in", [reduce_finish("in%d" % part, red_in[part], [small[0][0]])[0] for part in range(2)])

    order = ["meta", "g_pre_mix", "w_in", "b_gates", "conf_dw_w", "conf_dw_b", "conf_ln_g", "conf_ln_b", "conf_w_pw",
             "short_dw_w", "short_w_out", "w_o", "g_post_mix", "g_pre_mlp", "w_up", "w_down", "g_post_mlp"]
    return (loss, grad_x, *[results[nm][0] for nm in order], *[results[nm][1] for nm in order],
            *[results[nm][2] for nm in order], *[results[nm][3] for nm in order])
```

```python
import jax
import jax.numpy as jnp
from jax import lax
from jax.experimental import pallas as pl
from jax.experimental.pallas import tpu as pltpu

N_DEV = 8
N_META = 16
CONF_K = 31
SHORT_K = 3
RMS_EPS = 1e-6
LN_EPS = 1e-5
ADAM_LR = 0.001
ADAM_B1 = 0.9
ADAM_B2 = 0.999
ADAM_EPS = 1e-08
ADAM_WD = 0.01
ADAM_STEP = 10

LANE = 128
SUB = 8
ROW_TILE = 128
CONV_PAD = 32
CONV_CHUNK = 128
VMEM_LIMIT = 56 * 1024 * 1024

F32 = jnp.float32
BF16 = jnp.bfloat16
MESH = pl.DeviceIdType.MESH
ANY = pl.BlockSpec(memory_space=pl.ANY)
HBM_SPEC = pl.BlockSpec(memory_space=pltpu.HBM)
SEM_SPEC = pl.BlockSpec(memory_space=pltpu.SEMAPHORE)
EFFECT = pltpu.SideEffectType.DATAFLOW_SIDE_EFFECTING


def _params(n_axes):
    return pltpu.CompilerParams(dimension_semantics=("arbitrary",) * n_axes, vmem_limit_bytes=VMEM_LIMIT)


def _sigmoid(z):
    return 1.0 / (1.0 + jnp.exp(-z))


def _colsum8(v):
    r, c = v.shape
    return jnp.sum(v.reshape(r // SUB, SUB, c), axis=0)


def _position():
    x, y, c = lax.axis_index("x"), lax.axis_index("y"), lax.axis_index("c")
    return x, y, c


def _flat(p):
    return 4 * p[0] + 2 * p[1] + p[2]


def _all_gather(name, shards, deps=()):
    n, nd = len(shards), len(deps)

    def body(*refs):
        ins, outs = refs[:n], refs[n + nd:2 * n + nd]
        send_sems, recv_sems, local_sems = refs[2 * n + nd:]
        x, y, c = _position()
        me, sibling = (x, y, c), (x, y, 1 - c)
        chips = [(1 - x, y), (x, 1 - y), (1 - x, 1 - y)]

        def copy(q, k, block, to, src=None):
            dst = outs[q].at[_flat(block)]
            return pltpu.make_async_remote_copy(
                src_ref=dst if src is None else src, dst_ref=dst,
                send_sem=send_sems.at[q, k], recv_sem=recv_sems.at[q, k],
                device_id=to, device_id_type=MESH)

        mine = [pltpu.make_async_copy(ins[q], outs[q].at[_flat(me)], local_sems.at[q]) for q in range(n)]
        for cp in mine:
            cp.start()
        first = []
        for q in range(n):
            first.append(copy(q, 0, me, sibling, src=ins[q]))
            for j, chip in enumerate(chips):
                first.append(copy(q, 1 + j, me, (*chip, c), src=ins[q]))
        for cp in first:
            cp.start()
        passed = []
        for q in range(n):
            for j, chip in enumerate(chips):
                copy(q, 1 + j, (*chip, c), me).wait_recv()
                fwd = copy(q, 4 + j, (*chip, c), sibling)
                fwd.start()
                passed.append(fwd)
        for q in range(n):
            copy(q, 0, sibling, me).wait_recv()
            for j, chip in enumerate(chips):
                copy(q, 4 + j, (*chip, 1 - c), me).wait_recv()
        for cp in first + passed:
            cp.wait_send()
        for cp in mine:
            cp.wait()

    return pl.pallas_call(
        body, name=name,
        in_specs=[ANY] * (n + nd), out_specs=[ANY] * n,
        out_shape=[jax.ShapeDtypeStruct((N_DEV,) + s.shape, s.dtype) for s in shards],
        scratch_shapes=[pltpu.SemaphoreType.DMA((n, 7)), pltpu.SemaphoreType.DMA((n, 7)),
                        pltpu.SemaphoreType.DMA((n,))],
    )(*shards, *deps)


N_COPIES = {"gather_ici": 4, "gather_d2d": 3, "gather_direct": 3, "gather_relay": 3, "gather_diag": 1,
            "reduce_d2d": 4, "reduce_ici": 3}


def _copy_plan(kind):
    x, y, c = _position()
    me, sibling = (x, y, c), (x, y, 1 - c)
    chips = [(1 - x, y), (x, 1 - y), (1 - x, 1 - y)]
    if kind == "gather_ici":
        return [(_flat(me), _flat(me), sibling)] + [(_flat(me), _flat(me), (*ch, c)) for ch in chips]
    if kind == "gather_d2d":
        return [(_flat((*ch, c)), _flat((*ch, c)), sibling) for ch in chips]
    if kind == "gather_direct":
        return [(_flat(me), _flat(me), sibling)] + [(_flat(me), _flat(me), (*ch, c)) for ch in chips[:2]]
    if kind == "gather_relay":
        held, to = (x ^ (1 - c), y ^ c, c), (x ^ c, y ^ (1 - c), c)
        return [(_flat(held), _flat(held), to)] + [(_flat((*ch, c)), _flat((*ch, c)), sibling) for ch in chips[:2]]
    if kind == "gather_diag":
        return [(_flat((*chips[2], c)), _flat((*chips[2], c)), sibling)]
    if kind == "reduce_d2d":
        return [(2 * chip + (1 - c), chip, sibling) for chip in range(4)]
    return [(2 * ch[0] + ch[1], 2 * x + y, (*ch, c)) for ch in chips]


def _planned_copies(kind, srcs, dsts, send_sems, recv_sems):
    plan = _copy_plan(kind)
    return [pltpu.make_async_remote_copy(
        src_ref=src.at[s_slot], dst_ref=dst.at[d_slot],
        send_sem=send_sems.at[q * len(plan) + k], recv_sem=recv_sems.at[q * len(plan) + k],
        device_id=to, device_id_type=MESH)
        for q, (src, dst) in enumerate(zip(srcs, dsts)) for k, (s_slot, d_slot, to) in enumerate(plan)]


def _remote_start(name, kind, srcs, lands=None, deps=()):
    n = len(srcs)
    bufs = list(srcs) + ([] if lands is None else list(lands))
    nb, nd = len(bufs), len(deps)
    nsem = n * N_COPIES[kind]

    def body(*refs):
        ins = refs[:nb]
        send_sems, recv_sems = refs[nb + nd], refs[nb + nd + 1]
        token = refs[-1]
        for cp in _planned_copies(kind, ins[:n], ins[:n] if lands is None else ins[n:], send_sems, recv_sems):
            cp.start()
        token[...] = jnp.zeros_like(token)

    outs = pl.pallas_call(
        body, name=name,
        out_shape=(pltpu.SemaphoreType.DMA((nsem,)), pltpu.SemaphoreType.DMA((nsem,)),
                   *[pltpu.HBM(b.shape, b.dtype) for b in bufs], jax.ShapeDtypeStruct((SUB, LANE), F32)),
        in_specs=[HBM_SPEC] * nb + [ANY] * nd,
        out_specs=(SEM_SPEC, SEM_SPEC, *[HBM_SPEC] * nb, pl.BlockSpec(memory_space=pltpu.VMEM)),
        input_output_aliases={i: 2 + i for i in range(nb)},
        compiler_params=pltpu.CompilerParams(has_side_effects=EFFECT),
    )(*[pltpu.with_memory_space_constraint(b, pltpu.HBM) for b in bufs], *deps)
    return outs[0], outs[1], list(outs[2:2 + nb]), outs[-1]


def _remote_wait(name, kind, send_sems, recv_sems, bufs, n, after):
    nb, na = len(bufs), len(after)
    same = nb == n

    def body(*refs):
        ins = refs[:nb]
        sends, recvs = refs[nb], refs[nb + 1]
        for cp in _planned_copies(kind, ins[:n], ins[:n] if same else ins[n:], sends, recvs):
            cp.wait_send()
            cp.wait_recv()

    outs = pl.pallas_call(
        body, name=name,
        out_shape=[pltpu.HBM(b.shape, b.dtype) for b in bufs],
        in_specs=[HBM_SPEC] * nb + [SEM_SPEC, SEM_SPEC] + [ANY] * na,
        out_specs=[HBM_SPEC] * nb,
        input_output_aliases={i: i for i in range(nb)},
        compiler_params=pltpu.CompilerParams(has_side_effects=EFFECT),
    )(*bufs, send_sems, recv_sems, *after)
    return list(outs)


def _remote_pass_on(name, done, send_sems, recv_sems, bufs, after, nxt):
    nb, na = len(bufs), len(after)
    nsem = nb * N_COPIES[nxt]

    def body(*refs):
        ins = refs[:nb]
        new_sends, new_recvs = refs[nb + 2 + na], refs[nb + 3 + na]
        token = refs[-1]
        for cp in _planned_copies(done, ins, ins, refs[nb], refs[nb + 1]):
            cp.wait_send()
            cp.wait_recv()
        for cp in _planned_copies(nxt, ins, ins, new_sends, new_recvs):
            cp.start()
        token[...] = jnp.zeros_like(token)

    outs = pl.pallas_call(
        body, name=name,
        out_shape=(pltpu.SemaphoreType.DMA((nsem,)), pltpu.SemaphoreType.DMA((nsem,)),
                   *[pltpu.HBM(b.shape, b.dtype) for b in bufs], jax.ShapeDtypeStruct((SUB, LANE), F32)),
        in_specs=[HBM_SPEC] * nb + [SEM_SPEC, SEM_SPEC] + [ANY] * na,
        out_specs=(SEM_SPEC, SEM_SPEC, *[HBM_SPEC] * nb, pl.BlockSpec(memory_space=pltpu.VMEM)),
        input_output_aliases={i: 2 + i for i in range(nb)},
        compiler_params=pltpu.CompilerParams(has_side_effects=EFFECT),
    )(*bufs, send_sems, recv_sems, *after)
    return outs[0], outs[1], list(outs[2:2 + nb]), outs[-1]


def _mm_cols(name, a, w, *, tm, blocks, epilogue, out_dtypes, into=(), deps=()):
    t, k = a.shape
    nblk, _, cb = w.shape
    j0, j1 = blocks
    no = len(out_dtypes)

    def body(a_ref, w_ref, *rest):
        acc = jnp.dot(a_ref[...], w_ref[0], preferred_element_type=F32)
        for o_ref, o in zip(rest[len(into) + len(deps):], epilogue(acc)):
            o_ref[...] = o.astype(o_ref.dtype)

    return pl.pallas_call(
        body, name=name, grid=(j1 - j0, t // tm),
        in_specs=[pl.BlockSpec((tm, k), lambda j, i: (i, 0)),
                  pl.BlockSpec((1, k, cb), lambda j, i: (j0 + j, 0, 0))] + [ANY] * (len(into) + len(deps)),
        out_specs=[pl.BlockSpec((tm, cb), lambda j, i: (i, j0 + j)) for _ in range(no)],
        out_shape=[jax.ShapeDtypeStruct((t, nblk * cb), dt) for dt in out_dtypes],
        input_output_aliases={2 + idx: idx for idx in range(len(into))},
        compiler_params=_params(2),
    )(a, w, *into, *deps)


MXU_WIDTH = 256


def _mm_cols_pairs(name, a, w, *, tm):
    t, k = a.shape
    nblk, _, cb = w.shape
    main = cb // MXU_WIDTH * MXU_WIDTH
    tail = cb - main
    assert 2 * tail == MXU_WIDTH and nblk % 2 == 0

    def body(a_ref, w_ref, o_ref):
        av = a_ref[...]
        for b in range(2):
            o_ref[:, b * cb:b * cb + main] = jnp.dot(av, w_ref[b, :, 0:main], preferred_element_type=F32)
        tails = jnp.dot(av, jnp.concatenate([w_ref[0, :, main:cb], w_ref[1, :, main:cb]], axis=1),
                        preferred_element_type=F32)
        for b in range(2):
            o_ref[:, b * cb + main:(b + 1) * cb] = tails[:, b * tail:(b + 1) * tail]

    return pl.pallas_call(
        body, name=name, grid=(nblk // 2, t // tm),
        in_specs=[pl.BlockSpec((tm, k), lambda j, i: (i, 0)),
                  pl.BlockSpec((2, k, cb), lambda j, i: (j, 0, 0))],
        out_specs=pl.BlockSpec((tm, 2 * cb), lambda j, i: (i, j)),
        out_shape=jax.ShapeDtypeStruct((t, nblk * cb), F32),
        compiler_params=_params(2),
    )(a, w)


def _add_columns(o_ref, j, tn, acc, first):
    for jj in range(o_ref.shape[1] // tn):
        cols = slice(jj * tn, (jj + 1) * tn)

        @pl.when(jnp.logical_and(j == jj, first))
        def _(cols=cols):
            o_ref[:, cols] = acc

        @pl.when(jnp.logical_and(j == jj, jnp.logical_not(first)))
        def _(cols=cols):
            o_ref[:, cols] += acc


def _mm_rows(name, a, w2d, *, tm, tn, kparts=2):
    t, kf = a.shape
    n = w2d.shape[1]
    kp = kf // kparts

    def body(a_ref, w_ref, o_ref):
        acc = jnp.dot(a_ref[...], w_ref[...], preferred_element_type=F32)
        _add_columns(o_ref, pl.program_id(2), tn, acc, pl.program_id(1) == 0)

    return pl.pallas_call(
        body, name=name, grid=(t // tm, kparts, n // tn),
        in_specs=[pl.BlockSpec((tm, kp), lambda i, kh, j: (i, kh)),
                  pl.BlockSpec((kp, tn), lambda i, kh, j: (kh, j))],
        out_specs=pl.BlockSpec((tm, n), lambda i, kh, j: (i, 0)),
        out_shape=jax.ShapeDtypeStruct((t, n), F32),
        compiler_params=_params(3),
    )(a, w2d)


def _mm_nt_acc_parts(name, dy, w, *, tm, tn, kparts=2, deps=()):
    t = dy.shape[0]
    nblk, k, cb = w.shape
    per = nblk // kparts
    assert cb % MXU_WIDTH == 0

    def body(dy_ref, w_ref, *rest):
        acc = None
        for b in range(per):
            d = lax.dot_general(dy_ref[:, b * cb:(b + 1) * cb], w_ref[b], (((1,), (1,)), ((), ())),
                                preferred_element_type=F32)
            acc = d if acc is None else acc + d
        _add_columns(rest[-1], pl.program_id(2), tn, acc, pl.program_id(1) == 0)

    return pl.pallas_call(
        body, name=name, grid=(t // tm, kparts, k // tn),
        in_specs=[pl.BlockSpec((tm, per * cb), lambda i, kh, j: (i, kh)),
                  pl.BlockSpec((per, tn, cb), lambda i, kh, j: (kh, j, 0))] + [ANY] * len(deps),
        out_specs=pl.BlockSpec((tm, k), lambda i, kh, j: (i, 0)),
        out_shape=jax.ShapeDtypeStruct((t, k), F32),
        compiler_params=_params(3),
    )(dy, w, *deps)


def _mm_nt_acc(name, dy, w, *, tm, tn, col_off=0, deps=()):
    t = dy.shape[0]
    nblk, k, cb = w.shape

    main = cb // MXU_WIDTH * MXU_WIDTH

    def body(dy_ref, w_ref, *rest):
        nt = (((1,), (1,)), ((), ()))
        acc = None
        for b in range(nblk):
            d = lax.dot_general(dy_ref[:, b * cb:b * cb + main], w_ref[b, :, 0:main], nt, preferred_element_type=F32)
            acc = d if acc is None else acc + d
        if main < cb:
            dy_tails = jnp.concatenate([dy_ref[:, b * cb + main:(b + 1) * cb] for b in range(nblk)], axis=1)
            w_tails = jnp.concatenate([w_ref[b, :, main:cb] for b in range(nblk)], axis=1)
            acc = acc + lax.dot_general(dy_tails, w_tails, nt, preferred_element_type=F32)
        rest[-1][...] = acc

    return pl.pallas_call(
        body, name=name, grid=(t // tm, k // tn),
        in_specs=[pl.BlockSpec((tm, nblk * cb), lambda i, j: (i, col_off)),
                  pl.BlockSpec((nblk, tn, cb), lambda i, j: (0, j, 0))] + [ANY] * len(deps),
        out_specs=pl.BlockSpec((tm, tn), lambda i, j: (i, j)),
        out_shape=jax.ShapeDtypeStruct((t, k), F32),
        compiler_params=_params(2),
    )(dy, w, *deps)


def _mm_nt_blocks(name, dy, w2d, *, tm, tkb, extra=(), epilogue=None, out_dtypes=(F32,)):
    t, n = dy.shape
    kf = w2d.shape[0]
    ne = len(extra)

    def body(dy_ref, w_ref, *rest):
        acc = lax.dot_general(dy_ref[...], w_ref[...], (((1,), (1,)), ((), ())), preferred_element_type=F32)
        outs = (acc,) if epilogue is None else epilogue(acc, *[e[...] for e in rest[:ne]])
        for o_ref, o in zip(rest[ne:], outs):
            o_ref[...] = o.astype(o_ref.dtype)

    return pl.pallas_call(
        body, name=name, grid=(kf // tkb, t // tm),
        in_specs=[pl.BlockSpec((tm, n), lambda kb, i: (i, 0)),
                  pl.BlockSpec((tkb, n), lambda kb, i: (kb, 0))]
                 + [pl.BlockSpec((tm, tkb), lambda kb, i: (i, kb)) for _ in extra],
        out_specs=[pl.BlockSpec((tm, tkb), lambda kb, i: (i, kb)) for _ in out_dtypes],
        out_shape=[jax.ShapeDtypeStruct((t, kf), dt) for dt in out_dtypes],
        compiler_params=_params(2),
    )(dy, w2d, *extra)


def _mm_tn(name, a, b, me_arr, *, m, n, tma, tn, sharded, a_off=0, b_off=0, deps=()):
    t = a.shape[0]
    if sharded == "cols":
        cb = n // N_DEV
        nb, q = max(tn // cb, 1), max(cb // tn, 1)
        tw = tn // nb
        full_shape, own_shape = (N_DEV, m, cb), (m, cb)
        full_spec = pl.BlockSpec((nb, tma, tw), lambda i, j, me: (j // q, i, j % q))
    else:
        kb = m // N_DEV
        p = kb // tma
        nb, tw = 1, tn
        full_shape, own_shape = (m, n), (kb, n)
        full_spec = pl.BlockSpec((tma, tn), lambda i, j, me: (i, j))

    def body(me_ref, a_ref, b_ref, *rest):
        full_ref, own_ref, stage, sem, pending = rest[len(deps):]
        i, j = pl.program_id(0), pl.program_id(1)

        def own_copy(r0, c0):
            return pltpu.make_async_copy(
                stage, own_ref.at[pl.ds(pl.multiple_of(r0, tma), tma), pl.ds(pl.multiple_of(c0, tw), tw)], sem)

        def drain():
            @pl.when(pending[0] == 1)
            def _():
                own_copy(0, 0).wait()
                pending[0] = 0

        @pl.when(jnp.logical_and(i == 0, j == 0))
        def _():
            pending[0] = 0

        acc = lax.dot_general(a_ref[...], b_ref[...], (((0,), (0,)), ((), ())), preferred_element_type=F32)
        for blk in range(nb):
            part = acc[:, blk * tw:(blk + 1) * tw]
            if sharded == "cols":
                full_ref[blk] = part.astype(BF16)
                owner, r0, c0 = (j // q) * nb + blk, i * tma, (j % q) * tw
            else:
                full_ref[...] = part.astype(BF16)
                owner, r0, c0 = i // p, (i % p) * tma, j * tn

            @pl.when(owner == me_ref[0])
            def _():
                drain()
                stage[...] = part
                own_copy(r0, c0).start()
                pending[0] = 1

        @pl.when(jnp.logical_and(i == pl.num_programs(0) - 1, j == pl.num_programs(1) - 1))
        def _():
            drain()

    full, own = pl.pallas_call(
        body, name=name,
        grid_spec=pltpu.PrefetchScalarGridSpec(
            num_scalar_prefetch=1, grid=(m // tma, n // tn),
            in_specs=[pl.BlockSpec((t, tma), lambda i, j, me: (0, a_off + i)),
                      pl.BlockSpec((t, tn), lambda i, j, me: (0, b_off + j))] + [ANY] * len(deps),
            out_specs=[full_spec, ANY],
            scratch_shapes=[pltpu.VMEM((tma, tw), F32), pltpu.SemaphoreType.DMA(()), pltpu.SMEM((1,), jnp.int32)]),
        out_shape=[jax.ShapeDtypeStruct(full_shape, BF16), jax.ShapeDtypeStruct(own_shape, F32)],
        compiler_params=_params(2),
    )(me_arr, a, b, *deps)
    if sharded == "rows":
        full = full.reshape(N_DEV, m // N_DEV, n)
    return full, own


def _row_tile(t):
    return t // 8 if (t // 8) % 16 == 0 else ROW_TILE


def _row_call(name, body, t, row_ins, full_ins, row_outs, acc_outs, scratch=(), deps=()):
    tm = _row_tile(t)
    nin = len(row_ins) + len(full_ins)

    def without_deps(*refs):
        body(*refs[:nin], *refs[nin + len(deps):])

    return pl.pallas_call(
        without_deps, name=name, grid=(t // tm,),
        in_specs=[pl.BlockSpec((tm, a.shape[1]), lambda i: (i, 0)) for a in row_ins]
                 + [pl.BlockSpec(a.shape, lambda i: (0, 0)) for a in full_ins] + [ANY] * len(deps),
        out_specs=[pl.BlockSpec((tm, c), lambda i: (i, 0)) for c, _ in row_outs]
                  + [pl.BlockSpec((r, c), lambda i: (0, 0)) for r, c in acc_outs],
        out_shape=[jax.ShapeDtypeStruct((t, c), dt) for c, dt in row_outs]
                  + [jax.ShapeDtypeStruct((r, c), F32) for r, c in acc_outs],
        scratch_shapes=list(scratch),
        compiler_params=_params(1),
    )(*row_ins, *full_ins, *deps)


def _accumulate(ref, v):
    @pl.when(pl.program_id(0) == 0)
    def _():
        ref[...] = v

    @pl.when(pl.program_id(0) > 0)
    def _():
        ref[...] += v


def _rms(v):
    return lax.rsqrt(jnp.mean(v * v, axis=-1, keepdims=True) + RMS_EPS)


def _rms_bwd(dout, u, r, g):
    du = dout * g
    dx = r * (du - u * jnp.mean(du * u, axis=-1, keepdims=True))
    return dx, _colsum8(dout * u)


def _pre_norm(h0, g):
    t, d = h0.shape

    def body(h_ref, g_ref, n_ref):
        h = h_ref[...]
        n_ref[...] = (h * _rms(h) * g_ref[...]).astype(BF16)

    return _row_call("pre_norm", body, t, [h0], [g], [(d, BF16)], [])[0]


def _mix_post(m_mix, wo_full, h0, g_post, g_pre, deps=()):
    t, d = h0.shape
    tm = _row_tile(t)

    def body(m_ref, wo_ref, h0_ref, gp_ref, gq_ref, *rest):
        mix_ref, h1_ref, n2_ref = rest[len(deps):]
        mix_v = jnp.dot(m_ref[...], wo_ref[...], preferred_element_type=F32)
        mix_ref[...] = mix_v
        h1 = h0_ref[...] + mix_v * _rms(mix_v) * gp_ref[...]
        h1_ref[...] = h1
        n2_ref[...] = (h1 * _rms(h1) * gq_ref[...]).astype(BF16)

    tile = pl.BlockSpec((tm, d), lambda i: (i, 0))
    gain = pl.BlockSpec((1, d), lambda i: (0, 0))
    return pl.pallas_call(
        body, name="mix_post", grid=(t // tm,),
        in_specs=[tile, pl.BlockSpec((d, d), lambda i: (0, 0)), tile, gain, gain] + [ANY] * len(deps),
        out_specs=[tile, tile, tile],
        out_shape=[jax.ShapeDtypeStruct((t, d), F32), jax.ShapeDtypeStruct((t, d), F32),
                   jax.ShapeDtypeStruct((t, d), BF16)],
        compiler_params=_params(1),
    )(m_mix, wo_full, h0, g_post, g_pre, *deps)


def _loss_head(fo, h1, tgt, g_post_mlp, t_real):
    t, d = h1.shape
    tile = _row_tile(t)

    def body(fo_ref, h1_ref, tgt_ref, g_ref, dfo_ref, dh2_ref, dg_ref, loss_ref, lacc):
        i = pl.program_id(0)
        fo_v = fo_ref[...]
        g = g_ref[...]
        r = _rms(fo_v)
        u = fo_v * r
        h2 = h1_ref[...] + u * g
        row = i * tile + lax.broadcasted_iota(jnp.int32, (tile, 1), 0)
        valid = jnp.logical_and(row >= N_META, row < t_real)
        diff = jnp.where(valid, h2 - tgt_ref[...], 0.0)
        dh2 = diff * (1.0 / d)
        dh2_ref[...] = dh2
        dfo, dg = _rms_bwd(dh2, u, r, g)
        dfo_ref[...] = dfo.astype(BF16)
        _accumulate(dg_ref, dg)
        _accumulate(lacc, _colsum8(diff * diff))

        @pl.when(i == pl.num_programs(0) - 1)
        def _():
            loss_ref[...] = jnp.full((SUB, LANE), (0.5 / d) * jnp.sum(lacc[...]), F32)

    return _row_call("loss_head", body, t, [fo, h1, tgt], [g_post_mlp],
                     [(d, BF16), (d, F32)], [(SUB, d), (SUB, LANE)], scratch=[pltpu.VMEM((SUB, d), F32)])


def _mid_norm_bwd(dn2, h1, dh2, mix, g_pre_mlp, g_post_mix, deps=()):
    t, d = h1.shape

    def body(dn2_ref, h1_ref, dh2_ref, mix_ref, gq_ref, gp_ref, dh1_ref, dmix_ref, dgq_ref, dgp_ref):
        h1 = h1_ref[...]
        r3 = _rms(h1)
        dx, dgq = _rms_bwd(dn2_ref[...], h1 * r3, r3, gq_ref[...])
        dh1 = dh2_ref[...] + dx
        dh1_ref[...] = dh1
        mix_v = mix_ref[...]
        r2 = _rms(mix_v)
        dmix, dgp = _rms_bwd(dh1, mix_v * r2, r2, gp_ref[...])
        dmix_ref[...] = dmix.astype(BF16)
        _accumulate(dgq_ref, dgq)
        _accumulate(dgp_ref, dgp)

    return _row_call("mid_norm_bwd", body, t, [dn2, h1, dh2, mix], [g_pre_mlp, g_post_mix],
                     [(d, F32), (d, BF16)], [(SUB, d), (SUB, d)], deps=deps)


def _pre_norm_bwd(dn, h0, dh1, g_pre_mix, t_real):
    t, d = h0.shape
    tm = _row_tile(t)
    nt = t // tm
    seq = t_real - N_META
    tail = t_real - (nt - 1) * tm
    assert tm > N_META and N_META % SUB == 0 and 0 < tail <= tm and tail % SUB == 0

    def body(dn_ref, h0_ref, dh1_ref, g_ref, gx_ref, dmeta_ref, dg_ref, stage, sem):
        i = pl.program_id(0)
        h0 = h0_ref[...]
        r = _rms(h0)
        dx, dg = _rms_bwd(dn_ref[...], h0 * r, r, g_ref[...])
        dh0 = dh1_ref[...] + dx
        _accumulate(dg_ref, dg)

        def copy(rows, src0, dst0):
            return pltpu.make_async_copy(stage.at[pl.ds(src0, rows), :], gx_ref.at[pl.ds(dst0, rows), :], sem)

        @pl.when(i == 1)
        def _():
            copy(tm - N_META, N_META, 0).wait()

        @pl.when(i > 1)
        def _():
            copy(tm, 0, 0).wait()

        stage[...] = dh0

        @pl.when(i == 0)
        def _():
            dmeta_ref[...] = dh0[:N_META]
            copy(tm - N_META, N_META, 0).start()

        @pl.when(jnp.logical_and(i > 0, i < nt - 1))
        def _():
            copy(tm, 0, pl.multiple_of(i * tm - N_META, SUB)).start()

        @pl.when(i == nt - 1)
        def _():
            last = copy(tail, 0, (nt - 1) * tm - N_META)
            last.start()
            last.wait()

    tile = pl.BlockSpec((tm, d), lambda i: (i, 0))
    return pl.pallas_call(
        body, name="pre_norm_bwd", grid=(nt,),
        in_specs=[tile, tile, tile, pl.BlockSpec((1, d), lambda i: (0, 0))],
        out_specs=[ANY, pl.BlockSpec((N_META, d), lambda i: (0, 0)), pl.BlockSpec((SUB, d), lambda i: (0, 0))],
        out_shape=[jax.ShapeDtypeStruct((seq, d), F32), jax.ShapeDtypeStruct((N_META, d), F32),
                   jax.ShapeDtypeStruct((SUB, d), F32)],
        scratch_shapes=[pltpu.VMEM((tm, d), F32), pltpu.SemaphoreType.DMA(())],
        compiler_params=_params(1),
    )(dn, h0, dh1, g_pre_mix)


def _layer_norm_silu(a1, ln_g, ln_b, deps=()):
    t, c = a1.shape

    def body(a1_ref, g_ref, b_ref, a3_ref):
        a = a1_ref[...]
        mu = jnp.mean(a, axis=-1, keepdims=True)
        xc = a - mu
        rstd = lax.rsqrt(jnp.mean(xc * xc, axis=-1, keepdims=True) + LN_EPS)
        z = xc * rstd * g_ref[...] + b_ref[...]
        a3_ref[...] = (z * _sigmoid(z)).astype(BF16)

    return _row_call("layer_norm_silu", body, t, [a1], [ln_g, ln_b], [(c, BF16)], [], deps=deps)[0]


def _layer_norm_silu_bwd(da3, a1, ln_g, ln_b, deps=()):
    t, c = a1.shape

    def body(da3_ref, a1_ref, g_ref, b_ref, da1_ref, dg_ref, db_ref):
        a = a1_ref[...]
        g = g_ref[...]
        mu = jnp.mean(a, axis=-1, keepdims=True)
        xc = a - mu
        rstd = lax.rsqrt(jnp.mean(xc * xc, axis=-1, keepdims=True) + LN_EPS)
        xhat = xc * rstd
        z = xhat * g + b_ref[...]
        sg = _sigmoid(z)
        dz = da3_ref[...] * (sg * (1.0 + z * (1.0 - sg)))
        dxhat = dz * g
        da1_ref[...] = rstd * (dxhat - jnp.mean(dxhat, axis=-1, keepdims=True)
                               - xhat * jnp.mean(dxhat * xhat, axis=-1, keepdims=True))
        _accumulate(dg_ref, _colsum8(dz * xhat))
        _accumulate(db_ref, _colsum8(dz))

    return _row_call("layer_norm_silu_bwd", body, t, [da3, a1], [ln_g, ln_b], [(c, F32)], [(SUB, c), (SUB, c)], deps=deps)


def _branch_merge(a3, s, wpw, wso, proj, b_gates, d, deps=()):
    t, cols = proj.shape
    nblk, k, cb = wpw.shape
    w = 1024
    nh = d // w
    per = w // cb
    ga0 = (cols - 2 * d) // w
    tm = _row_tile(t)

    def body(a3_ref, s_ref, wpw_ref, wso_ref, *rest):
        pa_refs, pb_refs, bg_ref = rest[:nh], rest[nh:2 * nh], rest[2 * nh]
        ya_ref, yb_ref, ga_ref, gb_ref, m_ref = rest[2 * nh + 1 + len(deps):]
        a3v, sv = a3_ref[...], s_ref[...]
        for b in range(nblk):
            here = slice(b * cb, (b + 1) * cb)
            local = slice((b % per) * cb, (b % per + 1) * cb)
            ya = jnp.dot(a3v, wpw_ref[b], preferred_element_type=F32)
            yb = jnp.dot(sv, wso_ref[b], preferred_element_type=F32)
            ga = _sigmoid(pa_refs[b // per][:, local] + bg_ref[:, here])
            gb = _sigmoid(pb_refs[b // per][:, local] + bg_ref[:, d + b * cb:d + (b + 1) * cb])
            ya_ref[:, here] = ya.astype(BF16)
            yb_ref[:, here] = yb.astype(BF16)
            ga_ref[:, here] = ga.astype(BF16)
            gb_ref[:, here] = gb.astype(BF16)
            m_ref[:, here] = (ga * ya + gb * yb).astype(BF16)

    tile = pl.BlockSpec((tm, d), lambda i: (i, 0))
    return pl.pallas_call(
        body, name="branch_merge", grid=(t // tm,),
        in_specs=[pl.BlockSpec((tm, k), lambda i: (i, 0)), pl.BlockSpec((tm, k), lambda i: (i, 0)),
                  pl.BlockSpec((nblk, k, cb), lambda i: (0, 0, 0)), pl.BlockSpec((nblk, k, cb), lambda i: (0, 0, 0))]
                 + [pl.BlockSpec((tm, w), lambda i, h=h: (i, ga0 + h)) for h in range(2 * nh)]
                 + [pl.BlockSpec((1, 2 * d), lambda i: (0, 0))] + [ANY] * len(deps),
        out_specs=[tile] * 5,
        out_shape=[jax.ShapeDtypeStruct((t, d), BF16)] * 5,
        compiler_params=_params(1),
    )(a3, s, wpw, wso, *([proj] * (2 * nh)), b_gates, *deps)


def _gate_backward(dmix, wo_full, ga, gb, ya, yb, cols, tm, deps=()):
    t, d = ya.shape
    w = 1024
    nh = d // w
    ga0 = (cols - 2 * d) // w

    def body(dmix_ref, wo_ref, ga_ref, gb_ref, ya_ref, yb_ref, *rest):
        dya_ref, dyb_ref, dp_ref, dba_ref, dbb_ref, stage, sems = rest[len(deps):]
        h, i = pl.program_id(0), pl.program_id(1)
        dm = lax.dot_general(dmix_ref[...], wo_ref[...], (((1,), (1,)), ((), ())), preferred_element_type=F32)
        ga = ga_ref[...].astype(F32)
        gb = gb_ref[...].astype(F32)
        dya_ref[...] = (dm * ga).astype(BF16)
        dyb_ref[...] = (dm * gb).astype(BF16)
        dpa = dm * ya_ref[...].astype(F32) * ga * (1.0 - ga)
        dpb = dm * yb_ref[...].astype(F32) * gb * (1.0 - gb)

        def copies(row0, colblk):
            return [pltpu.make_async_copy(
                stage.at[g], dp_ref.at[pl.ds(pl.multiple_of(row0, tm), tm),
                                       pl.ds(pl.multiple_of((ga0 + g * nh + colblk) * w, w), w)], sems.at[g])
                for g in range(2)]

        @pl.when(jnp.logical_or(h > 0, i > 0))
        def _():
            for cp in copies(0, 0):
                cp.wait()

        stage[0] = dpa.astype(BF16)
        stage[1] = dpb.astype(BF16)
        for cp in copies(i * tm, h):
            cp.start()

        @pl.when(i == 0)
        def _():
            dba_ref[...] = _colsum8(dpa)
            dbb_ref[...] = _colsum8(dpb)

        @pl.when(i > 0)
        def _():
            dba_ref[...] += _colsum8(dpa)
            dbb_ref[...] += _colsum8(dpb)

        @pl.when(jnp.logical_and(h == pl.num_programs(0) - 1, i == pl.num_programs(1) - 1))
        def _():
            for cp in copies(0, 0):
                cp.wait()

    tile = pl.BlockSpec((tm, w), lambda h, i: (i, h))
    return pl.pallas_call(
        body, name="gate_backward", grid=(nh, t // tm),
        in_specs=[pl.BlockSpec((tm, d), lambda h, i: (i, 0)),
                  pl.BlockSpec((w, d), lambda h, i: (h, 0)),
                  tile, tile, tile, tile] + [ANY] * len(deps),
        out_specs=[tile, tile, ANY,
                   pl.BlockSpec((SUB, w), lambda h, i: (0, h)),
                   pl.BlockSpec((SUB, w), lambda h, i: (0, h))],
        out_shape=[jax.ShapeDtypeStruct((t, d), BF16), jax.ShapeDtypeStruct((t, d), BF16),
                   jax.ShapeDtypeStruct((t, cols), BF16),
                   jax.ShapeDtypeStruct((SUB, d), F32), jax.ShapeDtypeStruct((SUB, d), F32)],
        scratch_shapes=[pltpu.VMEM((2, tm, w), BF16), pltpu.SemaphoreType.DMA((2,))],
        compiler_params=_params(2),
    )(dmix, wo_full, ga, gb, ya, yb, *deps)


def _shifted_views(win, offsets):
    n = win.shape[0]
    rotated = {}
    views = {}
    for o in offsets:
        q, r = divmod(o, SUB)
        if r not in rotated:
            rotated[r] = win if r == 0 else pltpu.roll(win, n - r, 0)
        views[o] = rotated[r][q * SUB:q * SUB + CONV_CHUNK]
    return views


def _causal_views(xp_ref, ntap, r0):
    win = xp_ref[pl.ds(r0, CONV_CHUNK + CONV_PAD), :]
    views = _shifted_views(win, [CONV_PAD - (ntap - 1 - k) for k in range(ntap)])
    return [views[CONV_PAD - (ntap - 1 - k)] for k in range(ntap)]


def _causal_conv(xp_ref, w_ref, ntap, r0):
    acc = None
    for k, shifted in enumerate(_causal_views(xp_ref, ntap, r0)):
        term = w_ref[k:k + 1, :] * shifted
        acc = term if acc is None else acc + term
    return acc


def _anticausal_conv(xp_ref, w_ref, ntap, r0):
    win = xp_ref[pl.ds(pl.multiple_of(CONV_PAD + r0, CONV_PAD), CONV_CHUNK + CONV_PAD), :]
    views = _shifted_views(win, [ntap - 1 - k for k in range(ntap)])
    acc = None
    for k in range(ntap):
        term = w_ref[k:k + 1, :] * views[ntap - 1 - k]
        acc = term if acc is None else acc + term
    return acc


def _conv_weight_grad(dw_ref, d_chunk, xp_ref, ntap, r0):
    for k, shifted in enumerate(_causal_views(xp_ref, ntap, r0)):
        dw_ref[k * SUB:(k + 1) * SUB, :] += _colsum8(d_chunk * shifted)


def _zero_pads(ref, t):
    ref[0:CONV_PAD, :] = jnp.zeros((CONV_PAD, LANE), F32)
    ref[CONV_PAD + t:CONV_PAD + t + CONV_PAD, :] = jnp.zeros((CONV_PAD, LANE), F32)


def _for_chunks(t, fn):
    def step(idx, carry):
        fn(pl.multiple_of(idx * CONV_CHUNK, CONV_CHUNK))
        return carry

    lax.fori_loop(0, t // CONV_CHUNK, step, 0)


def _conv_forward(proj, conf_w, conf_b, short_w, dc, deps=()):
    t = proj.shape[0]
    nc = dc // LANE

    def body(av_ref, ag_ref, bg_ref, cg_ref, v_ref, cw_ref, cb_ref, sw_ref, *rest):
        a1_ref, s_ref, xa, xb = rest[len(deps):]
        _zero_pads(xa, t)
        _zero_pads(xb, t)
        xa[CONV_PAD:CONV_PAD + t, :] = av_ref[...] * _sigmoid(ag_ref[...])
        xb[CONV_PAD:CONV_PAD + t, :] = cg_ref[...] * v_ref[...]

        def chunk(r0):
            rs = pl.ds(r0, CONV_CHUNK)
            a1_ref[rs, :] = _causal_conv(xa, cw_ref, CONF_K, r0) + cb_ref[...]
            s_ref[rs, :] = (bg_ref[rs, :] * _causal_conv(xb, sw_ref, SHORT_K, r0)).astype(BF16)

        _for_chunks(t, chunk)

    col = lambda g: pl.BlockSpec((t, LANE), lambda c, g=g: (0, g * nc + c))
    return pl.pallas_call(
        body, name="conv_forward", grid=(nc,),
        in_specs=[col(0), col(1), col(2), col(3), col(4),
                  pl.BlockSpec((CONF_K, LANE), lambda c: (0, c)),
                  pl.BlockSpec((1, LANE), lambda c: (0, c)),
                  pl.BlockSpec((SHORT_K, LANE), lambda c: (0, c))] + [ANY] * len(deps),
        out_specs=[pl.BlockSpec((t, LANE), lambda c: (0, c)), pl.BlockSpec((t, LANE), lambda c: (0, c))],
        out_shape=[jax.ShapeDtypeStruct((t, dc), F32), jax.ShapeDtypeStruct((t, dc), BF16)],
        scratch_shapes=[pltpu.VMEM((t + 2 * CONV_PAD, LANE), F32), pltpu.VMEM((t + 2 * CONV_PAD, LANE), F32)],
        compiler_params=_params(1),
    )(proj, proj, proj, proj, proj, conf_w, conf_b, short_w, *deps)


def _conv_backward(dproj, proj, da1, ds, conf_w, short_w, dc):
    t = proj.shape[0]
    nc = dc // LANE

    def body(dp_in, av_ref, ag_ref, bg_ref, cg_ref, v_ref, da1_ref, ds_ref, cw_ref, sw_ref,
             dp_ref, dcw_ref, dcb_ref, dsw_ref, xa, xb, da, db, stage, sems):
        del dp_in
        c = pl.program_id(0)
        for ref in (xa, xb, da, db):
            _zero_pads(ref, t)
        xa[CONV_PAD:CONV_PAD + t, :] = av_ref[...] * _sigmoid(ag_ref[...])
        xb[CONV_PAD:CONV_PAD + t, :] = cg_ref[...] * v_ref[...]
        da[CONV_PAD:CONV_PAD + t, :] = da1_ref[...]
        dcw_ref[...] = jnp.zeros(dcw_ref.shape, F32)
        dsw_ref[...] = jnp.zeros(dsw_ref.shape, F32)
        dcb_ref[...] = jnp.zeros(dcb_ref.shape, F32)

        def copies(colblk):
            return [pltpu.make_async_copy(
                stage.at[g], dp_ref.at[:, pl.ds(pl.multiple_of((g * nc + colblk) * LANE, LANE), LANE)], sems.at[g])
                for g in range(5)]

        @pl.when(c > 0)
        def _():
            for cp in copies(0):
                cp.wait()

        def through_gate(r0):
            rs = pl.ds(r0, CONV_CHUNK)
            ds_c = ds_ref[rs, :]
            stage[2, rs, :] = (ds_c * _causal_conv(xb, sw_ref, SHORT_K, r0)).astype(BF16)
            db[pl.ds(pl.multiple_of(CONV_PAD + r0, CONV_PAD), CONV_CHUNK), :] = ds_c * bg_ref[rs, :]

        _for_chunks(t, through_gate)

        def through_convs(r0):
            rs = pl.ds(r0, CONV_CHUNK)
            da0 = _anticausal_conv(da, cw_ref, CONF_K, r0)
            sg = _sigmoid(ag_ref[rs, :])
            stage[0, rs, :] = (da0 * sg).astype(BF16)
            stage[1, rs, :] = (da0 * av_ref[rs, :] * sg * (1.0 - sg)).astype(BF16)
            dcv = _anticausal_conv(db, sw_ref, SHORT_K, r0)
            stage[3, rs, :] = (dcv * v_ref[rs, :]).astype(BF16)
            stage[4, rs, :] = (dcv * cg_ref[rs, :]).astype(BF16)
            da1_c = da1_ref[rs, :]
            _conv_weight_grad(dcw_ref, da1_c, xa, CONF_K, r0)
            _conv_weight_grad(dsw_ref, ds_ref[rs, :] * bg_ref[rs, :], xb, SHORT_K, r0)
            dcb_ref[...] += _colsum8(da1_c)

        _for_chunks(t, through_convs)
        for cp in copies(c):
            cp.start()

        @pl.when(c == pl.num_programs(0) - 1)
        def _():
            for cp in copies(0):
                cp.wait()

    col = lambda g: pl.BlockSpec((t, LANE), lambda c, g=g: (0, g * nc + c))
    blk = pl.BlockSpec((t, LANE), lambda c: (0, c))
    return pl.pallas_call(
        body, name="conv_backward", grid=(nc,),
        in_specs=[ANY, col(0), col(1), col(2), col(3), col(4), blk, blk,
                  pl.BlockSpec((CONF_K, LANE), lambda c: (0, c)),
                  pl.BlockSpec((SHORT_K, LANE), lambda c: (0, c))],
        out_specs=[ANY,
                   pl.BlockSpec((CONF_K * SUB, LANE), lambda c: (0, c)),
                   pl.BlockSpec((SUB, LANE), lambda c: (0, c)),
                   pl.BlockSpec((SHORT_K * SUB, LANE), lambda c: (0, c))],
        out_shape=[jax.ShapeDtypeStruct(dproj.shape, dproj.dtype),
                   jax.ShapeDtypeStruct((CONF_K * SUB, dc), F32),
                   jax.ShapeDtypeStruct((SUB, dc), F32),
                   jax.ShapeDtypeStruct((SHORT_K * SUB, dc), F32)],
        scratch_shapes=[pltpu.VMEM((t + 2 * CONV_PAD, LANE), F32)] * 4
                       + [pltpu.VMEM((5, t, LANE), BF16), pltpu.SemaphoreType.DMA((5,))],
        input_output_aliases={0: 0},
        compiler_params=_params(1),
    )(dproj, proj, proj, proj, proj, proj, da1, ds, conf_w, short_w)


def _adamw_math(w, g, m, v):
    m = ADAM_B1 * m + (1.0 - ADAM_B1) * g
    v = ADAM_B2 * v + (1.0 - ADAM_B2) * (g * g)
    m_hat = m / (1.0 - ADAM_B1 ** ADAM_STEP)
    v_hat = v / (1.0 - ADAM_B2 ** ADAM_STEP)
    delta = -ADAM_LR * (m_hat / (jnp.sqrt(v_hat) + ADAM_EPS) + ADAM_WD * w)
    return delta, m, v


def _cast_into_slot(name, w, me_arr, deps=()):
    r, c = w.shape
    tr = 256

    def body(me_ref, w_ref, *rest):
        del me_ref
        rest[-1][0] = w_ref[...].astype(BF16)

    return pl.pallas_call(
        body, name=name,
        grid_spec=pltpu.PrefetchScalarGridSpec(
            num_scalar_prefetch=1, grid=(r // tr,),
            in_specs=[pl.BlockSpec((tr, c), lambda i, me: (i, 0))] + [ANY] * len(deps),
            out_specs=pl.BlockSpec((1, tr, c), lambda i, me: (me[0], i, 0))),
        out_shape=jax.ShapeDtypeStruct((N_DEV, r, c), BF16),
        compiler_params=_params(1),
    )(me_arr, w, *deps)


def _chip_sum(name, full, from_sibling, me_arr):
    _, r, c = full.shape
    tr = min(r, 1024)

    def body(me_ref, full_ref, sib_ref, sums_ref):
        del me_ref
        sums_ref[0] = (full_ref[0].astype(F32) + sib_ref[0].astype(F32)).astype(BF16)

    other = lambda k, me: (me[0] // 2 + 1 + k) % 4
    return pl.pallas_call(
        body, name=name,
        grid_spec=pltpu.PrefetchScalarGridSpec(
            num_scalar_prefetch=1, grid=(r // tr, 3),
            in_specs=[pl.BlockSpec((1, tr, c), lambda i, k, me: (2 * other(k, me) + me[0] % 2, i, 0)),
                      pl.BlockSpec((1, tr, c), lambda i, k, me: (other(k, me), i, 0))],
            out_specs=pl.BlockSpec((1, tr, c), lambda i, k, me: (other(k, me), i, 0))),
        out_shape=jax.ShapeDtypeStruct((4, r, c), BF16),
        compiler_params=_params(2),
    )(me_arr, full, from_sibling)


def _adamw_shard(name, w, m, v, parts, me_arr, deps=()):
    r, c = w.shape
    tr = min(r // len(parts), 512 if c <= 1024 else 256)
    np_ = len(parts)
    per = r // np_ // tr

    def body(me_ref, w_ref, m_ref, v_ref, *rest):
        g_out, d_out, m_out, v_out = rest[5 * np_ + len(deps):]
        g = None
        for p in range(np_):
            gp = rest[5 * p][...]
            for l_ref in rest[5 * p + 1:5 * p + 5]:
                gp = gp + l_ref[0].astype(F32)
            g = gp if g is None else jnp.where(pl.program_id(0) // per == p, gp, g)
        delta, m_new, v_new = _adamw_math(w_ref[...], g, m_ref[...], v_ref[...])
        g_out[...] = g
        d_out[...] = delta
        m_out[...] = m_new
        v_out[...] = v_new

    tile = pl.BlockSpec((tr, c), lambda i, me: (i, 0))
    part_specs, part_args = [], []
    for p, (g_own, from_sibling, landed) in enumerate(parts):
        row = lambda i, p=p: jnp.clip(i - p * per, 0, per - 1)
        part_specs.append(pl.BlockSpec((tr, c), lambda i, me, row=row: (row(i), 0)))
        part_specs += [pl.BlockSpec((1, tr, c), lambda i, me, k=k, row=row: ((me[0] // 2 + k) % 4, row(i), 0))
                       for k in range(4)]
        part_args += [g_own, from_sibling, landed, landed, landed]
    return pl.pallas_call(
        body, name=name,
        grid_spec=pltpu.PrefetchScalarGridSpec(
            num_scalar_prefetch=1, grid=(r // tr,),
            in_specs=[tile] * 3 + part_specs + [ANY] * len(deps), out_specs=[tile] * 4),
        out_shape=[jax.ShapeDtypeStruct((r, c), F32)] * 4,
        compiler_params=_params(1),
    )(me_arr, w, m, v, *part_args, *deps)


SMALL_W = 1024
VEC_ROWS = 16
LOSS_ROW = 15
META_ROW0 = 16
CONF_ROW0 = 64
SHORT_ROW0 = 96
SMALL_ROWS = 104


def _pack_small(vec_parts, dmeta, dcw, dsw, loss_blk, me_arr):
    widths = [p.shape[1] for p in vec_parts]
    nv = len(vec_parts)

    def body(me_ref, *refs):
        del me_ref
        parts, (dmeta_ref, dcw_ref, dsw_ref, loss_ref, out_ref) = refs[:nv], refs[nv:]
        out_ref[0] = jnp.zeros((SMALL_ROWS, SMALL_W), F32)
        out_ref[0, LOSS_ROW:LOSS_ROW + 1, 0:LANE] = loss_ref[0:1, :]
        row = 0
        for p_ref, wd in zip(parts, widths):
            s = jnp.sum(p_ref[...], axis=0, keepdims=True)
            for h in range(wd // SMALL_W):
                out_ref[0, row:row + 1, :] = s[:, h * SMALL_W:(h + 1) * SMALL_W]
                row += 1
        for h in range(dmeta_ref.shape[1] // SMALL_W):
            out_ref[0, META_ROW0 + h * N_META:META_ROW0 + (h + 1) * N_META, :] = dmeta_ref[:, h * SMALL_W:(h + 1) * SMALL_W]
        for k in range(CONF_K):
            out_ref[0, CONF_ROW0 + k:CONF_ROW0 + k + 1, :] = jnp.sum(dcw_ref[k * SUB:(k + 1) * SUB, :], axis=0, keepdims=True)
        for k in range(SHORT_K):
            out_ref[0, SHORT_ROW0 + k:SHORT_ROW0 + k + 1, :] = jnp.sum(dsw_ref[k * SUB:(k + 1) * SUB, :], axis=0, keepdims=True)

    ins = [*vec_parts, dmeta, dcw, dsw, loss_blk]
    return pl.pallas_call(
        body, name="pack_small",
        grid_spec=pltpu.PrefetchScalarGridSpec(
            num_scalar_prefetch=1, grid=(1,),
            in_specs=[pl.BlockSpec(a.shape, lambda i, me: (0, 0)) for a in ins],
            out_specs=pl.BlockSpec((1, SMALL_ROWS, SMALL_W), lambda i, me: (me[0], 0, 0))),
        out_shape=jax.ShapeDtypeStruct((N_DEV, SMALL_ROWS, SMALL_W), F32),
        compiler_params=_params(1),
    )(me_arr, *ins)


def _small_update(gathered, me_arr, vec_params, meta_p, conf_p, short_p):
    widths = [p[0].shape[1] for p in vec_params]
    nv = len(vec_params)
    mcols = meta_p[0].shape[1]
    per_row = SMALL_W // mcols

    def body(me_ref, gv_ref, gm_ref, gc_ref, gs_ref, *rest):
        del me_ref
        ins, outs = rest[:3 * (nv + 3)], rest[3 * (nv + 3):]

        def total(ref, r0, rows):
            s = ref[0, r0:r0 + rows, :]
            for dev in range(1, N_DEV):
                s = s + ref[dev, r0:r0 + rows, :]
            return s

        grads = []
        row = 0
        for wd in widths:
            pieces = [total(gv_ref, row + h, 1) for h in range(wd // SMALL_W)]
            grads.append(pieces[0] if len(pieces) == 1 else jnp.concatenate(pieces, axis=1))
            row += len(pieces)
        grads.append(total(gm_ref, 0, N_META))
        grads.append(total(gc_ref, 0, CONF_K))
        grads.append(total(gs_ref, 0, SHORT_K))
        loss = gv_ref[0, LOSS_ROW:LOSS_ROW + 1, 0:LANE]
        for dev in range(1, N_DEV):
            loss = loss + gv_ref[dev, LOSS_ROW:LOSS_ROW + 1, 0:LANE]
        outs[-1][...] = loss
        for idx, g in enumerate(grads):
            w_ref, m_ref, v_ref = ins[3 * idx:3 * idx + 3]
            delta, m_new, v_new = _adamw_math(w_ref[...], g, m_ref[...], v_ref[...])
            g_out, d_out, m_out, v_out = outs[4 * idx:4 * idx + 4]
            g_out[...] = g
            d_out[...] = delta
            m_out[...] = m_new
            v_out[...] = v_new

    params = list(vec_params) + [meta_p, conf_p, short_p]
    flat = [a for p in params for a in p]
    whole = lambda a: pl.BlockSpec(a.shape, lambda i, me: (0,) * a.ndim)
    outs = pl.pallas_call(
        body, name="small_update",
        grid_spec=pltpu.PrefetchScalarGridSpec(
            num_scalar_prefetch=1, grid=(1,),
            in_specs=[pl.BlockSpec((N_DEV, VEC_ROWS, SMALL_W), lambda i, me: (0, 0, 0)),
                      pl.BlockSpec((N_DEV, N_META, mcols),
                                   lambda i, me: (0, META_ROW0 // N_META + me[0] // per_row, me[0] % per_row)),
                      pl.BlockSpec((N_DEV, 32, LANE), lambda i, me: (0, CONF_ROW0 // 32, me[0])),
                      pl.BlockSpec((N_DEV, SUB, LANE), lambda i, me: (0, SHORT_ROW0 // SUB, me[0]))]
                     + [whole(a) for a in flat],
            out_specs=[whole(p[0]) for p in params for _ in range(4)]
                      + [pl.BlockSpec((1, LANE), lambda i, me: (0, 0))]),
        out_shape=[jax.ShapeDtypeStruct(p[0].shape, F32) for p in params for _ in range(4)]
                  + [jax.ShapeDtypeStruct((1, LANE), F32)],
        compiler_params=_params(1),
    )(me_arr, gathered, gathered, gathered, gathered, *flat)
    return [tuple(outs[4 * i:4 * i + 4]) for i in range(len(params))], outs[-1][0, 0]


def kernel(x, meta, g_pre_mix, w_in, b_gates, conf_dw_w, conf_dw_b, conf_ln_g, conf_ln_b, conf_w_pw, short_dw_w, short_w_out, w_o, g_post_mix, g_pre_mlp, w_up, w_down, g_post_mlp, loss_target, m_meta, m_g_pre_mix, m_w_in, m_b_gates, m_conf_dw_w, m_conf_dw_b, m_conf_ln_g, m_conf_ln_b, m_conf_w_pw, m_short_dw_w, m_short_w_out, m_w_o, m_g_post_mix, m_g_pre_mlp, m_w_up, m_w_down, m_g_post_mlp, v_meta, v_g_pre_mix, v_w_in, v_b_gates, v_conf_dw_w, v_conf_dw_b, v_conf_ln_g, v_conf_ln_b, v_conf_w_pw, v_short_dw_w, v_short_w_out, v_w_o, v_g_post_mix, v_g_pre_mlp, v_w_up, v_w_down, v_g_post_mlp):
    seq, d = x.shape[1], x.shape[2]
    dc = conf_w_pw.shape[1]
    t_real = N_META + seq
    t = -(-t_real // ROW_TILE) * ROW_TILE
    tm = t // 2
    assert tm % 16 == 0 and d % 1024 == 0 and dc % 1024 == 0
    x_idx, y_idx, c_idx = _position()
    me_arr = jnp.reshape(4 * x_idx + 2 * y_idx + c_idx, (1,)).astype(jnp.int32)

    big = [w_in[0], conf_w_pw[0], short_w_out[0], w_o[0], w_up[0], w_down[0]]
    big_names = ["w_in", "conf_w_pw", "short_w_out", "w_o", "w_up", "w_down"]
    groups = [[0], [1, 2, 3], [4], [5]]
    slots, deps = [], []
    for g, idxs in enumerate(groups):
        slots.append([_cast_into_slot("cast_" + big_names[i], big[i], me_arr, deps=deps) for i in idxs])
        if g == 0:
            direct0 = _remote_start("gather0_direct_start", "gather_direct", slots[0])
            deps = [direct0[3]]
    casts = [sl for group in slots[1:] for sl in group]
    meta_g, cw_g, sw_g = _all_gather("gather_small_params", [meta, conf_dw_w[0], short_dw_w[0]], deps=casts)

    def start_direct(g, deps):
        send, recv, bufs, tok = _remote_start("gather%d_direct_start" % g, "gather_direct", slots[g], deps=deps)
        return (send, recv, bufs), tok

    def relay(g, state, after):
        send, recv, bufs, tok = _remote_pass_on("gather%d_relay" % g, "gather_direct", *state, after, "gather_relay")
        return (send, recv, bufs), tok

    def gathered(g, state, after):
        send, recv, bufs, tok = _remote_pass_on("gather%d_diag" % g, "gather_relay", *state, after, "gather_diag")
        return _remote_wait("gather%d_diag_wait" % g, "gather_diag", send, recv, bufs, len(bufs), [tok])

    unshard =lambda g: jnp.transpose(g, (1, 0, 2)).reshape(g.shape[1], -1)
    meta_full, cw_full, sw_full = unshard(meta_g), unshard(cw_g), unshard(sw_g)

    relay0, tok = relay(0, direct0[:3], [meta_g])
    zrows = jnp.zeros((t - t_real, d), F32) + tok[0, 0] * 0.0
    h0 = jnp.concatenate([meta_full, x[0], zrows], axis=0)
    tgt = jnp.concatenate([jnp.zeros((N_META, d), F32), loss_target[0], zrows], axis=0)
    n = _pre_norm(h0, g_pre_mix)
    direct1, tok = start_direct(1, [tok])
    direct2, tok = start_direct(2, [tok])
    win_g, = gathered(0, relay0, [tok, n])
    proj = _mm_cols_pairs("proj", n, win_g, tm=tm // 2)
    relay1, tok = relay(1, direct1, [proj])
    a1, s = _conv_forward(proj, cw_full, conf_dw_b, sw_full, dc, deps=[tok])
    relay2, tok = relay(2, direct2, [a1])
    direct3, tok = start_direct(3, [tok])
    a3 = _layer_norm_silu(a1, conf_ln_g, conf_ln_b, deps=[tok])
    wpw_g, wso_g, wo_g = gathered(1, relay1, [a3])
    wo_full = wo_g.reshape(d, d)
    ya, yb, gate_a, gate_b, m_mix = _branch_merge(a3, s, wpw_g, wso_g, proj, b_gates, d)
    mix, h1, n2 = _mix_post(m_mix, wo_full, h0, g_post_mix, g_pre_mlp)
    wup_g, = gathered(2, relay2, [n2])

    def up_epilogue(acc):
        r = jnp.maximum(acc, 0.0)
        return r * r, r

    half_up = dict(tm=tm, epilogue=up_epilogue, out_dtypes=(BF16, BF16))
    f, relu_up = _mm_cols("mlp_up0", n2, wup_g, blocks=(0, N_DEV // 2), **half_up)
    relay3, tok = relay(3, direct3, [f])
    f, relu_up = _mm_cols("mlp_up1", n2, wup_g, blocks=(N_DEV // 2, N_DEV), into=(f, relu_up), deps=[tok], **half_up)
    wdn_g, = gathered(3, relay3, [f])
    wdn_full = wdn_g.reshape(-1, d)
    fo = _mm_rows("mlp_down", f, wdn_full, tm=tm, tn=512)
    dfo, dh2, dg_post_mlp, loss_blk = _loss_head(fo, h1, tgt, g_post_mlp, t_real)

    def reduce_start(tag, fulls, deps):
        lands = [lax.empty((4,) + g.shape[1:], BF16) for g in fulls]
        send, recv, bufs, tok = _remote_start("reduce_%s_d2d_start" % tag, "reduce_d2d", fulls, lands, deps=deps)
        return (send, recv, bufs), tok

    def reduce_middle(tag, state, owns, after):
        send, recv, bufs = state
        k = len(owns)
        bufs = _remote_wait("reduce_%s_d2d_wait" % tag, "reduce_d2d", send, recv, bufs, k, after)
        from_sibling = bufs[k:]
        sums = [_chip_sum("chip_sum_%s%d" % (tag, i), bufs[i], from_sibling[i], me_arr) for i in range(k)]
        lands = [lax.empty(sm.shape, BF16) for sm in sums]
        send, recv, bufs, tok = _remote_start("reduce_%s_ici_start" % tag, "reduce_ici", sums, lands)
        return (send, recv, bufs, list(zip(owns, from_sibling))), tok

    def reduce_finish(tag, state, after):
        send, recv, bufs, local = state
        k = len(local)
        bufs = _remote_wait("reduce_%s_ici_wait" % tag, "reduce_ici", send, recv, bufs, k, after)
        return [(own, sib, landed) for (own, sib), landed in zip(local, bufs[k:])]

    dup = _mm_nt_blocks("d_up", dfo, wdn_full, tm=tm, tkb=1024, extra=(relu_up,),
                        epilogue=lambda acc, r: (acc * (2.0 * r.astype(F32)),), out_dtypes=(BF16,))[0]
    gw_down, gw_down_own = _mm_tn("dw_down", f, dfo, me_arr, m=f.shape[1], n=d, tma=512, tn=d, sharded="rows")
    red_down, tok = reduce_start("down", [gw_down], ())
    dn2 = _mm_nt_acc_parts("d_n2", dup, wup_g, tm=tm, tn=512, deps=[tok])
    gw_up, gw_up_own = _mm_tn("dw_up", n2, dup, me_arr, m=d, n=dup.shape[1], tma=512, tn=2048, sharded="cols")
    red_down, tok = reduce_middle("down", red_down, [gw_down_own], [dn2])
    red_up, tok = reduce_start("up", [gw_up], [tok])
    dh1, dmix, dg_pre_mlp, dg_post_mix = _mid_norm_bwd(dn2, h1, dh2, mix, g_pre_mlp, g_post_mix, deps=[tok])
    dya, dyb, dproj, db_a, db_b = _gate_backward(dmix, wo_full, gate_a, gate_b, ya, yb, proj.shape[1], tm // 2)
    db_gates = jnp.concatenate([db_a, db_b], axis=1)
    red_up, tok = reduce_middle("up", red_up, [gw_up_own], [dya])
    gw_o, gw_o_own = _mm_tn("dw_o", m_mix, dmix, me_arr, m=d, n=d, tma=d // N_DEV, tn=d, sharded="rows", deps=[tok])
    da3 = _mm_nt_acc("d_a3", dya, wpw_g, tm=tm, tn=512)
    gw_pw, gw_pw_own = _mm_tn("dw_pw", a3, dya, me_arr, m=dc, n=d, tma=512, tn=d, sharded="cols")
    dsb = _mm_nt_acc("d_s", dyb, wso_g, tm=tm, tn=512)
    gw_so, gw_so_own = _mm_tn("dw_so", s, dyb, me_arr, m=dc, n=d, tma=512, tn=d, sharded="cols")
    red_mix, tok = reduce_start("mix", [gw_pw, gw_so, gw_o], ())
    da1, dln_g, dln_b = _layer_norm_silu_bwd(da3, a1, conf_ln_g, conf_ln_b, deps=[tok])
    dproj, dcw, dcb, dsw = _conv_backward(dproj, proj, da1, dsb, cw_full, sw_full, dc)
    red_mix, tok = reduce_middle("mix", red_mix, [gw_pw_own, gw_so_own, gw_o_own], [dcb])
    in_cb = w_in.shape[2]
    half = d // 2
    red_in = []
    for part in range(2):
        gw, own = _mm_tn("dw_in%d" % part, n, dproj, me_arr, m=half, n=proj.shape[1], tma=512, tn=2 * in_cb,
                         sharded="cols", a_off=part * (half // 512), deps=[tok])
        state, tok = reduce_start("in%d" % part, [gw], ())
        red_in.append((state, own))
    for part in range(2):
        state, own = red_in[part]
        red_in[part], tok = reduce_middle("in%d" % part, state, [own], [tok])
    dn = _mm_nt_acc("d_n", dproj, win_g, tm=tm // 2, tn=512, deps=[tok])
    grad_x, dmeta, dg_pre_mix = _pre_norm_bwd(dn, h0, dh1, g_pre_mix, t_real)
    grad_x = grad_x[None]

    vec_parts = [dg_pre_mix, db_gates, dcb, dln_g, dln_b, dg_post_mix, dg_pre_mlp, dg_post_mlp]
    packed = _pack_small(vec_parts, dmeta, dcw, dsw, loss_blk, me_arr)
    send, recv, bufs, tok = _remote_start("small_grads_ici_start", "gather_ici", [packed])
    vec_names = ["g_pre_mix", "b_gates", "conf_dw_b", "conf_ln_g", "conf_ln_b", "g_post_mix", "g_pre_mlp", "g_post_mlp"]
    env = locals()
    results = {}

    def update(nm, parts, deps=()):
        res = _adamw_shard("adamw_" + nm, env[nm][0], env["m_" + nm][0], env["v_" + nm][0], parts, me_arr, deps=deps)
        results[nm] = tuple(r[None] for r in res)
        return res[0]

    done = [update("w_down", reduce_finish("down", red_down, [tok]), deps=[tok])]
    done.append(update("w_up", reduce_finish("up", red_up, done)))
    bufs = _remote_wait("small_grads_ici_wait", "gather_ici", send, recv, bufs, 1, done)
    send, recv, bufs, tok = _remote_start("small_grads_d2d_start", "gather_d2d", bufs)
    for nm, pair in zip(["conf_w_pw", "short_w_out", "w_o"], reduce_finish("mix", red_mix, [tok])):
        done.append(update(nm, [pair], deps=[tok]))
    small_g, = _remote_wait("small_grads_d2d_wait", "gather_d2d", send, recv, bufs, 1, done)
    triple = lambda nm, sq: tuple(env[p + nm][0] if sq else env[p + nm] for p in ("", "m_", "v_"))
    small, loss = _small_update(small_g, me_arr, [triple(nm, False) for nm in vec_names],
                                triple("meta", False), triple("conf_dw_w", True), triple("short_dw_w", True))
    for nm, res in zip(vec_names + ["meta"], small[:len(vec_names) + 1]):
        results[nm] = res
    results["conf_dw_w"] = tuple(r[None] for r in small[-2])
    results["short_dw_w"] = tuple(r[None] for r in small[-1])
    update("w_in", [reduce_finish("in%d" % part, red_in[part], [small[0][0]])[0] for part in range(2)])

    order = ["meta", "g_pre_mix", "w_in", "b_gates", "conf_dw_w", "conf_dw_b", "conf_ln_g", "conf_ln_b", "conf_w_pw",
             "short_dw_w", "short_w_out", "w_o", "g_post_mix", "g_pre_mlp", "w_up", "w_down", "g_post_mlp"]
    return (loss, grad_x, *[results[nm][0] for nm in order], *[results[nm][1] for nm in order],
            *[results[nm][2] for nm in order], *[results[nm][3] for nm in order])
```

```python
import jax
import jax.numpy as jnp
from jax import lax
from jax.experimental import pallas as pl
from jax.experimental.pallas import tpu as pltpu

N_DEV = 8
N_META = 16
CONF_K = 31
SHORT_K = 3
RMS_EPS = 1e-6
LN_EPS = 1e-5
ADAM_LR = 0.001
ADAM_B1 = 0.9
ADAM_B2 = 0.999
ADAM_EPS = 1e-08
ADAM_WD = 0.01
ADAM_STEP = 10

LANE = 128
SUB = 8
ROW_TILE = 128
CONV_PAD = 32
CONV_CHUNK = 128
VMEM_LIMIT = 56 * 1024 * 1024

F32 = jnp.float32
BF16 = jnp.bfloat16
MESH = pl.DeviceIdType.MESH
ANY = pl.BlockSpec(memory_space=pl.ANY)
HBM_SPEC = pl.BlockSpec(memory_space=pltpu.HBM)
SEM_SPEC = pl.BlockSpec(memory_space=pltpu.SEMAPHORE)
EFFECT = pltpu.SideEffectType.DATAFLOW_SIDE_EFFECTING


def _params(n_axes):
    return pltpu.CompilerParams(dimension_semantics=("arbitrary",) * n_axes, vmem_limit_bytes=VMEM_LIMIT)


def _sigmoid(z):
    return 1.0 / (1.0 + jnp.exp(-z))


def _colsum8(v):
    r, c = v.shape
    return jnp.sum(v.reshape(r // SUB, SUB, c), axis=0)


def _position():
    x, y, c = lax.axis_index("x"), lax.axis_index("y"), lax.axis_index("c")
    return x, y, c


def _flat(p):
    return 4 * p[0] + 2 * p[1] + p[2]


def _all_gather(name, shards, deps=()):
    n, nd = len(shards), len(deps)

    def body(*refs):
        ins, outs = refs[:n], refs[n + nd:2 * n + nd]
        send_sems, recv_sems, local_sems = refs[2 * n + nd:]
        x, y, c = _position()
        me, sibling = (x, y, c), (x, y, 1 - c)
        chips = [(1 - x, y), (x, 1 - y), (1 - x, 1 - y)]

        def copy(q, k, block, to, src=None):
            dst = outs[q].at[_flat(block)]
            return pltpu.make_async_remote_copy(
                src_ref=dst if src is None else src, dst_ref=dst,
                send_sem=send_sems.at[q, k], recv_sem=recv_sems.at[q, k],
                device_id=to, device_id_type=MESH)

        mine = [pltpu.make_async_copy(ins[q], outs[q].at[_flat(me)], local_sems.at[q]) for q in range(n)]
        for cp in mine:
            cp.start()
        first = []
        for q in range(n):
            first.append(copy(q, 0, me, sibling, src=ins[q]))
            for j, chip in enumerate(chips):
                first.append(copy(q, 1 + j, me, (*chip, c), src=ins[q]))
        for cp in first:
            cp.start()
        passed = []
        for q in range(n):
            for j, chip in enumerate(chips):
                copy(q, 1 + j, (*chip, c), me).wait_recv()
                fwd = copy(q, 4 + j, (*chip, c), sibling)
                fwd.start()
                passed.append(fwd)
        for q in range(n):
            copy(q, 0, sibling, me).wait_recv()
            for j, chip in enumerate(chips):
                copy(q, 4 + j, (*chip, 1 - c), me).wait_recv()
        for cp in first + passed:
            cp.wait_send()
        for cp in mine:
            cp.wait()

    return pl.pallas_call(
        body, name=name,
        in_specs=[ANY] * (n + nd), out_specs=[ANY] * n,
        out_shape=[jax.ShapeDtypeStruct((N_DEV,) + s.shape, s.dtype) for s in shards],
        scratch_shapes=[pltpu.SemaphoreType.DMA((n, 7)), pltpu.SemaphoreType.DMA((n, 7)),
                        pltpu.SemaphoreType.DMA((n,))],
    )(*shards, *deps)


N_COPIES = {"gather_ici": 4, "gather_d2d": 3, "gather_direct": 3, "gather_relay": 3, "gather_diag": 1,
            "reduce_d2d": 4, "reduce_ici": 3}


def _copy_plan(kind):
    x, y, c = _position()
    me, sibling = (x, y, c), (x, y, 1 - c)
    chips = [(1 - x, y), (x, 1 - y), (1 - x, 1 - y)]
    if kind == "gather_ici":
        return [(_flat(me), _flat(me), sibling)] + [(_flat(me), _flat(me), (*ch, c)) for ch in chips]
    if kind == "gather_d2d":
        return [(_flat((*ch, c)), _flat((*ch, c)), sibling) for ch in chips]
    if kind == "gather_direct":
        return [(_flat(me), _flat(me), sibling)] + [(_flat(me), _flat(me), (*ch, c)) for ch in chips[:2]]
    if kind == "gather_relay":
        held, to = (x ^ (1 - c), y ^ c, c), (x ^ c, y ^ (1 - c), c)
        return [(_flat(held), _flat(held), to)] + [(_flat((*ch, c)), _flat((*ch, c)), sibling) for ch in chips[:2]]
    if kind == "gather_diag":
        return [(_flat((*chips[2], c)), _flat((*chips[2], c)), sibling)]
    if kind == "reduce_d2d":
        return [(2 * chip + (1 - c), chip, sibling) for chip in range(4)]
    return [(2 * ch[0] + ch[1], 2 * x + y, (*ch, c)) for ch in chips]


def _planned_copies(kind, srcs, dsts, send_sems, recv_sems):
    plan = _copy_plan(kind)
    return [pltpu.make_async_remote_copy(
        src_ref=src.at[s_slot], dst_ref=dst.at[d_slot],
        send_sem=send_sems.at[q * len(plan) + k], recv_sem=recv_sems.at[q * len(plan) + k],
        device_id=to, device_id_type=MESH)
        for q, (src, dst) in enumerate(zip(srcs, dsts)) for k, (s_slot, d_slot, to) in enumerate(plan)]


def _remote_start(name, kind, srcs, lands=None, deps=()):
    n = len(srcs)
    bufs = list(srcs) + ([] if lands is None else list(lands))
    nb, nd = len(bufs), len(deps)
    nsem = n * N_COPIES[kind]

    def body(*refs):
        ins = refs[:nb]
        send_sems, recv_sems = refs[nb + nd], refs[nb + nd + 1]
        token = refs[-1]
        for cp in _planned_copies(kind, ins[:n], ins[:n] if lands is None else ins[n:], send_sems, recv_sems):
            cp.start()
        token[...] = jnp.zeros_like(token)

    outs = pl.pallas_call(
        body, name=name,
        out_shape=(pltpu.SemaphoreType.DMA((nsem,)), pltpu.SemaphoreType.DMA((nsem,)),
                   *[pltpu.HBM(b.shape, b.dtype) for b in bufs], jax.ShapeDtypeStruct((SUB, LANE), F32)),
        in_specs=[HBM_SPEC] * nb + [ANY] * nd,
        out_specs=(SEM_SPEC, SEM_SPEC, *[HBM_SPEC] * nb, pl.BlockSpec(memory_space=pltpu.VMEM)),
        input_output_aliases={i: 2 + i for i in range(nb)},
        compiler_params=pltpu.CompilerParams(has_side_effects=EFFECT),
    )(*[pltpu.with_memory_space_constraint(b, pltpu.HBM) for b in bufs], *deps)
    return outs[0], outs[1], list(outs[2:2 + nb]), outs[-1]


def _remote_wait(name, kind, send_sems, recv_sems, bufs, n, after):
    nb, na = len(bufs), len(after)
    same = nb == n

    def body(*refs):
        ins = refs[:nb]
        sends, recvs = refs[nb], refs[nb + 1]
        for cp in _planned_copies(kind, ins[:n], ins[:n] if same else ins[n:], sends, recvs):
            cp.wait_send()
            cp.wait_recv()

    outs = pl.pallas_call(
        body, name=name,
        out_shape=[pltpu.HBM(b.shape, b.dtype) for b in bufs],
        in_specs=[HBM_SPEC] * nb + [SEM_SPEC, SEM_SPEC] + [ANY] * na,
        out_specs=[HBM_SPEC] * nb,
        input_output_aliases={i: i for i in range(nb)},
        compiler_params=pltpu.CompilerParams(has_side_effects=EFFECT),
    )(*bufs, send_sems, recv_sems, *after)
    return list(outs)


def _remote_pass_on(name, done, send_sems, recv_sems, bufs, after, nxt):
    nb, na = len(bufs), len(after)
    nsem = nb * N_COPIES[nxt]

    def body(*refs):
        ins = refs[:nb]
        new_sends, new_recvs = refs[nb + 2 + na], refs[nb + 3 + na]
        token = refs[-1]
        for cp in _planned_copies(done, ins, ins, refs[nb], refs[nb + 1]):
            cp.wait_send()
            cp.wait_recv()
        for cp in _planned_copies(nxt, ins, ins, new_sends, new_recvs):
            cp.start()
        token[...] = jnp.zeros_like(token)

    outs = pl.pallas_call(
        body, name=name,
        out_shape=(pltpu.SemaphoreType.DMA((nsem,)), pltpu.SemaphoreType.DMA((nsem,)),
                   *[pltpu.HBM(b.shape, b.dtype) for b in bufs], jax.ShapeDtypeStruct((SUB, LANE), F32)),
        in_specs=[HBM_SPEC] * nb + [SEM_SPEC, SEM_SPEC] + [ANY] * na,
        out_specs=(SEM_SPEC, SEM_SPEC, *[HBM_SPEC] * nb, pl.BlockSpec(memory_space=pltpu.VMEM)),
        input_output_aliases={i: 2 + i for i in range(nb)},
        compiler_params=pltpu.CompilerParams(has_side_effects=EFFECT),
    )(*bufs, send_sems, recv_sems, *after)
    return outs[0], outs[1], list(outs[2:2 + nb]), outs[-1]


def _mm_cols(name, a, w, *, tm, blocks, epilogue, out_dtypes, into=(), deps=()):
    t, k = a.shape
    nblk, _, cb = w.shape
    j0, j1 = blocks
    no = len(out_dtypes)

    def body(a_ref, w_ref, *rest):
        acc = jnp.dot(a_ref[...], w_ref[0], preferred_element_type=F32)
        for o_ref, o in zip(rest[len(into) + len(deps):], epilogue(acc)):
            o_ref[...] = o.astype(o_ref.dtype)

    return pl.pallas_call(
        body, name=name, grid=(j1 - j0, t // tm),
        in_specs=[pl.BlockSpec((tm, k), lambda j, i: (i, 0)),
                  pl.BlockSpec((1, k, cb), lambda j, i: (j0 + j, 0, 0))] + [ANY] * (len(into) + len(deps)),
        out_specs=[pl.BlockSpec((tm, cb), lambda j, i: (i, j0 + j)) for _ in range(no)],
        out_shape=[jax.ShapeDtypeStruct((t, nblk * cb), dt) for dt in out_dtypes],
        input_output_aliases={2 + idx: idx for idx in range(len(into))},
        compiler_params=_params(2),
    )(a, w, *into, *deps)


MXU_WIDTH = 256


def _mm_cols_pairs(name, a, w, *, tm):
    t, k = a.shape
    nblk, _, cb = w.shape
    main = cb // MXU_WIDTH * MXU_WIDTH
    tail = cb - main
    assert 2 * tail == MXU_WIDTH and nblk % 2 == 0

    def body(a_ref, w_ref, o_ref):
        av = a_ref[...]
        for b in range(2):
            o_ref[:, b * cb:b * cb + main] = jnp.dot(av, w_ref[b, :, 0:main], preferred_element_type=F32)
        tails = jnp.dot(av, jnp.concatenate([w_ref[0, :, main:cb], w_ref[1, :, main:cb]], axis=1),
                        preferred_element_type=F32)
        for b in range(2):
            o_ref[:, b * cb + main:(b + 1) * cb] = tails[:, b * tail:(b + 1) * tail]

    return pl.pallas_call(
        body, name=name, grid=(nblk // 2, t // tm),
        in_specs=[pl.BlockSpec((tm, k), lambda j, i: (i, 0)),
                  pl.BlockSpec((2, k, cb), lambda j, i: (j, 0, 0))],
        out_specs=pl.BlockSpec((tm, 2 * cb), lambda j, i: (i, j)),
        out_shape=jax.ShapeDtypeStruct((t, nblk * cb), F32),
        compiler_params=_params(2),
    )(a, w)


def _add_columns(o_ref, j, tn, acc, first):
    for jj in range(o_ref.shape[1] // tn):
        cols = slice(jj * tn, (jj + 1) * tn)

        @pl.when(jnp.logical_and(j == jj, first))
        def _(cols=cols):
            o_ref[:, cols] = acc

        @pl.when(jnp.logical_and(j == jj, jnp.logical_not(first)))
        def _(cols=cols):
            o_ref[:, cols] += acc


def _mm_rows(name, a, w2d, *, tm, tn, kparts=2):
    t, kf = a.shape
    n = w2d.shape[1]
    kp = kf // kparts

    def body(a_ref, w_ref, o_ref):
        acc = jnp.dot(a_ref[...], w_ref[...], preferred_element_type=F32)
        _add_columns(o_ref, pl.program_id(2), tn, acc, pl.program_id(1) == 0)

    return pl.pallas_call(
        body, name=name, grid=(t // tm, kparts, n // tn),
        in_specs=[pl.BlockSpec((tm, kp), lambda i, kh, j: (i, kh)),
                  pl.BlockSpec((kp, tn), lambda i, kh, j: (kh, j))],
        out_specs=pl.BlockSpec((tm, n), lambda i, kh, j: (i, 0)),
        out_shape=jax.ShapeDtypeStruct((t, n), F32),
        compiler_params=_params(3),
    )(a, w2d)


def _mm_nt_acc_parts(name, dy, w, *, tm, tn, kparts=2, deps=()):
    t = dy.shape[0]
    nblk, k, cb = w.shape
    per = nblk // kparts
    assert cb % MXU_WIDTH == 0

    def body(dy_ref, w_ref, *rest):
        acc = None
        for b in range(per):
            d = lax.dot_general(dy_ref[:, b * cb:(b + 1) * cb], w_ref[b], (((1,), (1,)), ((), ())),
                                preferred_element_type=F32)
            acc = d if acc is None else acc + d
        _add_columns(rest[-1], pl.program_id(2), tn, acc, pl.program_id(1) == 0)

    return pl.pallas_call(
        body, name=name, grid=(t // tm, kparts, k // tn),
        in_specs=[pl.BlockSpec((tm, per * cb), lambda i, kh, j: (i, kh)),
                  pl.BlockSpec((per, tn, cb), lambda i, kh, j: (kh, j, 0))] + [ANY] * len(deps),
        out_specs=pl.BlockSpec((tm, k), lambda i, kh, j: (i, 0)),
        out_shape=jax.ShapeDtypeStruct((t, k), F32),
        compiler_params=_params(3),
    )(dy, w, *deps)


def _mm_nt_acc(name, dy, w, *, tm, tn, col_off=0, deps=()):
    t = dy.shape[0]
    nblk, k, cb = w.shape

    main = cb // MXU_WIDTH * MXU_WIDTH

    def body(dy_ref, w_ref, *rest):
        nt = (((1,), (1,)), ((), ()))
        acc = None
        for b in range(nblk):
            d = lax.dot_general(dy_ref[:, b * cb:b * cb + main], w_ref[b, :, 0:main], nt, preferred_element_type=F32)
            acc = d if acc is None else acc + d
        if main < cb:
            dy_tails = jnp.concatenate([dy_ref[:, b * cb + main:(b + 1) * cb] for b in range(nblk)], axis=1)
            w_tails = jnp.concatenate([w_ref[b, :, main:cb] for b in range(nblk)], axis=1)
            acc = acc + lax.dot_general(dy_tails, w_tails, nt, preferred_element_type=F32)
        rest[-1][...] = acc

    return pl.pallas_call(
        body, name=name, grid=(t // tm, k // tn),
        in_specs=[pl.BlockSpec((tm, nblk * cb), lambda i, j: (i, col_off)),
                  pl.BlockSpec((nblk, tn, cb), lambda i, j: (0, j, 0))] + [ANY] * len(deps),
        out_specs=pl.BlockSpec((tm, tn), lambda i, j: (i, j)),
        out_shape=jax.ShapeDtypeStruct((t, k), F32),
        compiler_params=_params(2),
    )(dy, w, *deps)


def _mm_nt_blocks(name, dy, w2d, *, tm, tkb, extra=(), epilogue=None, out_dtypes=(F32,)):
    t, n = dy.shape
    kf = w2d.shape[0]
    ne = len(extra)

    def body(dy_ref, w_ref, *rest):
        acc = lax.dot_general(dy_ref[...], w_ref[...], (((1,), (1,)), ((), ())), preferred_element_type=F32)
        outs = (acc,) if epilogue is None else epilogue(acc, *[e[...] for e in rest[:ne]])
        for o_ref, o in zip(rest[ne:], outs):
            o_ref[...] = o.astype(o_ref.dtype)

    return pl.pallas_call(
        body, name=name, grid=(kf // tkb, t // tm),
        in_specs=[pl.BlockSpec((tm, n), lambda kb, i: (i, 0)),
                  pl.BlockSpec((tkb, n), lambda kb, i: (kb, 0))]
                 + [pl.BlockSpec((tm, tkb), lambda kb, i: (i, kb)) for _ in extra],
        out_specs=[pl.BlockSpec((tm, tkb), lambda kb, i: (i, kb)) for _ in out_dtypes],
        out_shape=[jax.ShapeDtypeStruct((t, kf), dt) for dt in out_dtypes],
        compiler_params=_params(2),
    )(dy, w2d, *extra)


def _mm_tn(name, a, b, me_arr, *, m, n, tma, tn, sharded, a_off=0, b_off=0, deps=()):
    t = a.shape[0]
    if sharded == "cols":
        cb = n // N_DEV
        nb, q = max(tn // cb, 1), max(cb // tn, 1)
        tw = tn // nb
        full_shape, own_shape = (N_DEV, m, cb), (m, cb)
        full_spec = pl.BlockSpec((nb, tma, tw), lambda i, j, me: (j // q, i, j % q))
    else:
        kb = m // N_DEV
        p = kb // tma
        nb, tw = 1, tn
        full_shape, own_shape = (m, n), (kb, n)
        full_spec = pl.BlockSpec((tma, tn), lambda i, j, me: (i, j))

    def body(me_ref, a_ref, b_ref, *rest):
        full_ref, own_ref, stage, sem, pending = rest[len(deps):]
        i, j = pl.program_id(0), pl.program_id(1)

        def own_copy(r0, c0):
            return pltpu.make_async_copy(
                stage, own_ref.at[pl.ds(pl.multiple_of(r0, tma), tma), pl.ds(pl.multiple_of(c0, tw), tw)], sem)

        def drain():
            @pl.when(pending[0] == 1)
            def _():
                own_copy(0, 0).wait()
                pending[0] = 0

        @pl.when(jnp.logical_and(i == 0, j == 0))
        def _():
            pending[0] = 0

        acc = lax.dot_general(a_ref[...], b_ref[...], (((0,), (0,)), ((), ())), preferred_element_type=F32)
        for blk in range(nb):
            part = acc[:, blk * tw:(blk + 1) * tw]
            if sharded == "cols":
                full_ref[blk] = part.astype(BF16)
                owner, r0, c0 = (j // q) * nb + blk, i * tma, (j % q) * tw
            else:
                full_ref[...] = part.astype(BF16)
                owner, r0, c0 = i // p, (i % p) * tma, j * tn

            @pl.when(owner == me_ref[0])
            def _():
                drain()
                stage[...] = part
                own_copy(r0, c0).start()
                pending[0] = 1

        @pl.when(jnp.logical_and(i == pl.num_programs(0) - 1, j == pl.num_programs(1) - 1))
        def _():
            drain()

    full, own = pl.pallas_call(
        body, name=name,
        grid_spec=pltpu.PrefetchScalarGridSpec(
            num_scalar_prefetch=1, grid=(m // tma, n // tn),
            in_specs=[pl.BlockSpec((t, tma), lambda i, j, me: (0, a_off + i)),
                      pl.BlockSpec((t, tn), lambda i, j, me: (0, b_off + j))] + [ANY] * len(deps),
            out_specs=[full_spec, ANY],
            scratch_shapes=[pltpu.VMEM((tma, tw), F32), pltpu.SemaphoreType.DMA(()), pltpu.SMEM((1,), jnp.int32)]),
        out_shape=[jax.ShapeDtypeStruct(full_shape, BF16), jax.ShapeDtypeStruct(own_shape, F32)],
        compiler_params=_params(2),
    )(me_arr, a, b, *deps)
    if sharded == "rows":
        full = full.reshape(N_DEV, m // N_DEV, n)
    return full, own


def _row_tile(t):
    return t // 8 if (t // 8) % 16 == 0 else ROW_TILE


def _row_call(name, body, t, row_ins, full_ins, row_outs, acc_outs, scratch=(), deps=()):
    tm = _row_tile(t)
    nin = len(row_ins) + len(full_ins)

    def without_deps(*refs):
        body(*refs[:nin], *refs[nin + len(deps):])

    return pl.pallas_call(
        without_deps, name=name, grid=(t // tm,),
        in_specs=[pl.BlockSpec((tm, a.shape[1]), lambda i: (i, 0)) for a in row_ins]
                 + [pl.BlockSpec(a.shape, lambda i: (0, 0)) for a in full_ins] + [ANY] * len(deps),
        out_specs=[pl.BlockSpec((tm, c), lambda i: (i, 0)) for c, _ in row_outs]
                  + [pl.BlockSpec((r, c), lambda i: (0, 0)) for r, c in acc_outs],
        out_shape=[jax.ShapeDtypeStruct((t, c), dt) for c, dt in row_outs]
                  + [jax.ShapeDtypeStruct((r, c), F32) for r, c in acc_outs],
        scratch_shapes=list(scratch),
        compiler_params=_params(1),
    )(*row_ins, *full_ins, *deps)


RING_SLOTS = 3


def _row_call_ring(name, body, t, row_ins, full_ins, row_outs, acc_outs, scratch=(), deps=()):
    tm = _row_tile(t)
    nt = t // tm
    nrow, nfull, nd, nsc = len(row_ins), len(full_ins), len(deps), len(scratch)
    nout = len(row_outs) + len(acc_outs)

    def ringed(*refs):
        hbm = refs[:nrow]
        full = refs[nrow:nrow + nfull]
        outs = refs[nrow + nfull + nd:nrow + nfull + nd + nout + nsc]
        rings = refs[nrow + nfull + nd + nout + nsc:-1]
        sems = refs[-1]
        i = pl.program_id(0)

        def fetch(step, slot):
            rows = pl.ds(pl.multiple_of(step * tm, tm), tm)
            return [pltpu.make_async_copy(hbm[q].at[rows, :], rings[q].at[slot], sems.at[slot, q])
                    for q in range(nrow)]

        @pl.when(i == 0)
        def _():
            for ahead in range(min(RING_SLOTS - 1, nt)):
                for cp in fetch(ahead, ahead):
                    cp.start()

        @pl.when(i + RING_SLOTS - 1 < nt)
        def _():
            for cp in fetch(i + RING_SLOTS - 1, (i + RING_SLOTS - 1) % RING_SLOTS):
                cp.start()

        slot = i % RING_SLOTS
        for cp in fetch(i, slot):
            cp.wait()
        body(*[ring.at[slot] for ring in rings], *full, *outs)

    return pl.pallas_call(
        ringed, name=name, grid=(nt,),
        in_specs=[ANY] * nrow + [pl.BlockSpec(a.shape, lambda i: (0, 0)) for a in full_ins] + [ANY] * nd,
        out_specs=[pl.BlockSpec((tm, c), lambda i: (i, 0)) for c, _ in row_outs]
                  + [pl.BlockSpec((r, c), lambda i: (0, 0)) for r, c in acc_outs],
        out_shape=[jax.ShapeDtypeStruct((t, c), dt) for c, dt in row_outs]
                  + [jax.ShapeDtypeStruct((r, c), F32) for r, c in acc_outs],
        scratch_shapes=list(scratch) + [pltpu.VMEM((RING_SLOTS, tm, a.shape[1]), a.dtype) for a in row_ins]
                       + [pltpu.SemaphoreType.DMA((RING_SLOTS, nrow))],
        compiler_params=_params(1),
    )(*row_ins, *full_ins, *deps)


def _accumulate(ref, v):
    @pl.when(pl.program_id(0) == 0)
    def _():
        ref[...] = v

    @pl.when(pl.program_id(0) > 0)
    def _():
        ref[...] += v


def _rms(v):
    return lax.rsqrt(jnp.mean(v * v, axis=-1, keepdims=True) + RMS_EPS)


def _rms_bwd(dout, u, r, g):
    du = dout * g
    dx = r * (du - u * jnp.mean(du * u, axis=-1, keepdims=True))
    return dx, _colsum8(dout * u)


def _pre_norm(h0, g):
    t, d = h0.shape

    def body(h_ref, g_ref, n_ref):
        h = h_ref[...]
        n_ref[...] = (h * _rms(h) * g_ref[...]).astype(BF16)

    return _row_call("pre_norm", body, t, [h0], [g], [(d, BF16)], [])[0]


def _mix_post(m_mix, wo_full, h0, g_post, g_pre, deps=()):
    t, d = h0.shape
    tm = _row_tile(t)

    def body(m_ref, wo_ref, h0_ref, gp_ref, gq_ref, *rest):
        mix_ref, h1_ref, n2_ref = rest[len(deps):]
        mix_v = jnp.dot(m_ref[...], wo_ref[...], preferred_element_type=F32)
        mix_ref[...] = mix_v
        h1 = h0_ref[...] + mix_v * _rms(mix_v) * gp_ref[...]
        h1_ref[...] = h1
        n2_ref[...] = (h1 * _rms(h1) * gq_ref[...]).astype(BF16)

    tile = pl.BlockSpec((tm, d), lambda i: (i, 0))
    gain = pl.BlockSpec((1, d), lambda i: (0, 0))
    return pl.pallas_call(
        body, name="mix_post", grid=(t // tm,),
        in_specs=[tile, pl.BlockSpec((d, d), lambda i: (0, 0)), tile, gain, gain] + [ANY] * len(deps),
        out_specs=[tile, tile, tile],
        out_shape=[jax.ShapeDtypeStruct((t, d), F32), jax.ShapeDtypeStruct((t, d), F32),
                   jax.ShapeDtypeStruct((t, d), BF16)],
        compiler_params=_params(1),
    )(m_mix, wo_full, h0, g_post, g_pre, *deps)


def _loss_head(fo, h1, tgt, g_post_mlp, t_real):
    t, d = h1.shape
    tile = _row_tile(t)

    def body(fo_ref, h1_ref, tgt_ref, g_ref, dfo_ref, dh2_ref, dg_ref, loss_ref, lacc):
        i = pl.program_id(0)
        fo_v = fo_ref[...]
        g = g_ref[...]
        r = _rms(fo_v)
        u = fo_v * r
        h2 = h1_ref[...] + u * g
        row = i * tile + lax.broadcasted_iota(jnp.int32, (tile, 1), 0)
        valid = jnp.logical_and(row >= N_META, row < t_real)
        diff = jnp.where(valid, h2 - tgt_ref[...], 0.0)
        dh2 = diff * (1.0 / d)
        dh2_ref[...] = dh2
        dfo, dg = _rms_bwd(dh2, u, r, g)
        dfo_ref[...] = dfo.astype(BF16)
        _accumulate(dg_ref, dg)
        _accumulate(lacc, _colsum8(diff * diff))

        @pl.when(i == pl.num_programs(0) - 1)
        def _():
            loss_ref[...] = jnp.full((SUB, LANE), (0.5 / d) * jnp.sum(lacc[...]), F32)

    return _row_call_ring("loss_head", body, t, [fo, h1, tgt], [g_post_mlp],
                     [(d, BF16), (d, F32)], [(SUB, d), (SUB, LANE)], scratch=[pltpu.VMEM((SUB, d), F32)])


def _mid_norm_bwd(dn2, h1, dh2, mix, g_pre_mlp, g_post_mix, deps=()):
    t, d = h1.shape

    def body(dn2_ref, h1_ref, dh2_ref, mix_ref, gq_ref, gp_ref, dh1_ref, dmix_ref, dgq_ref, dgp_ref):
        h1 = h1_ref[...]
        r3 = _rms(h1)
        dx, dgq = _rms_bwd(dn2_ref[...], h1 * r3, r3, gq_ref[...])
        dh1 = dh2_ref[...] + dx
        dh1_ref[...] = dh1
        mix_v = mix_ref[...]
        r2 = _rms(mix_v)
        dmix, dgp = _rms_bwd(dh1, mix_v * r2, r2, gp_ref[...])
        dmix_ref[...] = dmix.astype(BF16)
        _accumulate(dgq_ref, dgq)
        _accumulate(dgp_ref, dgp)

    return _row_call_ring("mid_norm_bwd", body, t, [dn2, h1, dh2, mix], [g_pre_mlp, g_post_mix],
                     [(d, F32), (d, BF16)], [(SUB, d), (SUB, d)], deps=deps)


def _pre_norm_bwd(dn, h0, dh1, g_pre_mix, t_real):
    t, d = h0.shape
    tm = _row_tile(t)
    nt = t // tm
    seq = t_real - N_META
    tail = t_real - (nt - 1) * tm
    assert tm > N_META and N_META % SUB == 0 and 0 < tail <= tm and tail % SUB == 0

    def body(dn_ref, h0_ref, dh1_ref, g_ref, gx_ref, dmeta_ref, dg_ref, stage, sem):
        i = pl.program_id(0)
        h0 = h0_ref[...]
        r = _rms(h0)
        dx, dg = _rms_bwd(dn_ref[...], h0 * r, r, g_ref[...])
        dh0 = dh1_ref[...] + dx
        _accumulate(dg_ref, dg)

        def copy(rows, src0, dst0):
            return pltpu.make_async_copy(stage.at[pl.ds(src0, rows), :], gx_ref.at[pl.ds(dst0, rows), :], sem)

        @pl.when(i == 1)
        def _():
            copy(tm - N_META, N_META, 0).wait()

        @pl.when(i > 1)
        def _():
            copy(tm, 0, 0).wait()

        stage[...] = dh0

        @pl.when(i == 0)
        def _():
            dmeta_ref[...] = dh0[:N_META]
            copy(tm - N_META, N_META, 0).start()

        @pl.when(jnp.logical_and(i > 0, i < nt - 1))
        def _():
            copy(tm, 0, pl.multiple_of(i * tm - N_META, SUB)).start()

        @pl.when(i == nt - 1)
        def _():
            last = copy(tail, 0, (nt - 1) * tm - N_META)
            last.start()
            last.wait()

    tile = pl.BlockSpec((tm, d), lambda i: (i, 0))
    return pl.pallas_call(
        body, name="pre_norm_bwd", grid=(nt,),
        in_specs=[tile, tile, tile, pl.BlockSpec((1, d), lambda i: (0, 0))],
        out_specs=[ANY, pl.BlockSpec((N_META, d), lambda i: (0, 0)), pl.BlockSpec((SUB, d), lambda i: (0, 0))],
        out_shape=[jax.ShapeDtypeStruct((seq, d), F32), jax.ShapeDtypeStruct((N_META, d), F32),
                   jax.ShapeDtypeStruct((SUB, d), F32)],
        scratch_shapes=[pltpu.VMEM((tm, d), F32), pltpu.SemaphoreType.DMA(())],
        compiler_params=_params(1),
    )(dn, h0, dh1, g_pre_mix)


def _layer_norm_silu(a1, ln_g, ln_b, deps=()):
    t, c = a1.shape

    def body(a1_ref, g_ref, b_ref, a3_ref):
        a = a1_ref[...]
        mu = jnp.mean(a, axis=-1, keepdims=True)
        xc = a - mu
        rstd = lax.rsqrt(jnp.mean(xc * xc, axis=-1, keepdims=True) + LN_EPS)
        z = xc * rstd * g_ref[...] + b_ref[...]
        a3_ref[...] = (z * _sigmoid(z)).astype(BF16)

    return _row_call("layer_norm_silu", body, t, [a1], [ln_g, ln_b], [(c, BF16)], [], deps=deps)[0]


def _layer_norm_silu_bwd(da3, a1, ln_g, ln_b, deps=()):
    t, c = a1.shape

    def body(da3_ref, a1_ref, g_ref, b_ref, da1_ref, dg_ref, db_ref):
        a = a1_ref[...]
        g = g_ref[...]
        mu = jnp.mean(a, axis=-1, keepdims=True)
        xc = a - mu
        rstd = lax.rsqrt(jnp.mean(xc * xc, axis=-1, keepdims=True) + LN_EPS)
        xhat = xc * rstd
        z = xhat * g + b_ref[...]
        sg = _sigmoid(z)
        dz = da3_ref[...] * (sg * (1.0 + z * (1.0 - sg)))
        dxhat = dz * g
        da1_ref[...] = rstd * (dxhat - jnp.mean(dxhat, axis=-1, keepdims=True)
                               - xhat * jnp.mean(dxhat * xhat, axis=-1, keepdims=True))
        _accumulate(dg_ref, _colsum8(dz * xhat))
        _accumulate(db_ref, _colsum8(dz))

    return _row_call("layer_norm_silu_bwd", body, t, [da3, a1], [ln_g, ln_b], [(c, F32)], [(SUB, c), (SUB, c)], deps=deps)


def _branch_merge(a3, s, wpw, wso, proj, b_gates, d, deps=()):
    t, cols = proj.shape
    nblk, k, cb = wpw.shape
    w = 1024
    nh = d // w
    per = w // cb
    ga0 = (cols - 2 * d) // w
    tm = _row_tile(t)

    def body(a3_ref, s_ref, wpw_ref, wso_ref, *rest):
        pa_refs, pb_refs, bg_ref = rest[:nh], rest[nh:2 * nh], rest[2 * nh]
        ya_ref, yb_ref, ga_ref, gb_ref, m_ref = rest[2 * nh + 1 + len(deps):]
        a3v, sv = a3_ref[...], s_ref[...]
        for b in range(nblk):
            here = slice(b * cb, (b + 1) * cb)
            local = slice((b % per) * cb, (b % per + 1) * cb)
            ya = jnp.dot(a3v, wpw_ref[b], preferred_element_type=F32)
            yb = jnp.dot(sv, wso_ref[b], preferred_element_type=F32)
            ga = _sigmoid(pa_refs[b // per][:, local] + bg_ref[:, here])
            gb = _sigmoid(pb_refs[b // per][:, local] + bg_ref[:, d + b * cb:d + (b + 1) * cb])
            ya_ref[:, here] = ya.astype(BF16)
            yb_ref[:, here] = yb.astype(BF16)
            ga_ref[:, here] = ga.astype(BF16)
            gb_ref[:, here] = gb.astype(BF16)
            m_ref[:, here] = (ga * ya + gb * yb).astype(BF16)

    tile = pl.BlockSpec((tm, d), lambda i: (i, 0))
    return pl.pallas_call(
        body, name="branch_merge", grid=(t // tm,),
        in_specs=[pl.BlockSpec((tm, k), lambda i: (i, 0)), pl.BlockSpec((tm, k), lambda i: (i, 0)),
                  pl.BlockSpec((nblk, k, cb), lambda i: (0, 0, 0)), pl.BlockSpec((nblk, k, cb), lambda i: (0, 0, 0))]
                 + [pl.BlockSpec((tm, w), lambda i, h=h: (i, ga0 + h)) for h in range(2 * nh)]
                 + [pl.BlockSpec((1, 2 * d), lambda i: (0, 0))] + [ANY] * len(deps),
        out_specs=[tile] * 5,
        out_shape=[jax.ShapeDtypeStruct((t, d), BF16)] * 5,
        compiler_params=_params(1),
    )(a3, s, wpw, wso, *([proj] * (2 * nh)), b_gates, *deps)


def _gate_backward(dmix, wo_full, ga, gb, ya, yb, cols, tm, deps=()):
    t, d = ya.shape
    w = 1024
    nh = d // w
    ga0 = (cols - 2 * d) // w

    def body(dmix_ref, wo_ref, ga_ref, gb_ref, ya_ref, yb_ref, *rest):
        dya_ref, dyb_ref, dp_ref, dba_ref, dbb_ref, stage, sems = rest[len(deps):]
        h, i = pl.program_id(0), pl.program_id(1)
        dm = lax.dot_general(dmix_ref[...], wo_ref[...], (((1,), (1,)), ((), ())), preferred_element_type=F32)
        ga = ga_ref[...].astype(F32)
        gb = gb_ref[...].astype(F32)
        dya_ref[...] = (dm * ga).astype(BF16)
        dyb_ref[...] = (dm * gb).astype(BF16)
        dpa = dm * ya_ref[...].astype(F32) * ga * (1.0 - ga)
        dpb = dm * yb_ref[...].astype(F32) * gb * (1.0 - gb)

        def copies(row0, colblk):
            return [pltpu.make_async_copy(
                stage.at[g], dp_ref.at[pl.ds(pl.multiple_of(row0, tm), tm),
                                       pl.ds(pl.multiple_of((ga0 + g * nh + colblk) * w, w), w)], sems.at[g])
                for g in range(2)]

        @pl.when(jnp.logical_or(h > 0, i > 0))
        def _():
            for cp in copies(0, 0):
                cp.wait()

        stage[0] = dpa.astype(BF16)
        stage[1] = dpb.astype(BF16)
        for cp in copies(i * tm, h):
            cp.start()

        @pl.when(i == 0)
        def _():
            dba_ref[...] = _colsum8(dpa)
            dbb_ref[...] = _colsum8(dpb)

        @pl.when(i > 0)
        def _():
            dba_ref[...] += _colsum8(dpa)
            dbb_ref[...] += _colsum8(dpb)

        @pl.when(jnp.logical_and(h == pl.num_programs(0) - 1, i == pl.num_programs(1) - 1))
        def _():
            for cp in copies(0, 0):
                cp.wait()

    tile = pl.BlockSpec((tm, w), lambda h, i: (i, h))
    return pl.pallas_call(
        body, name="gate_backward", grid=(nh, t // tm),
        in_specs=[pl.BlockSpec((tm, d), lambda h, i: (i, 0)),
                  pl.BlockSpec((w, d), lambda h, i: (h, 0)),
                  tile, tile, tile, tile] + [ANY] * len(deps),
        out_specs=[tile, tile, ANY,
                   pl.BlockSpec((SUB, w), lambda h, i: (0, h)),
                   pl.BlockSpec((SUB, w), lambda h, i: (0, h))],
        out_shape=[jax.ShapeDtypeStruct((t, d), BF16), jax.ShapeDtypeStruct((t, d), BF16),
                   jax.ShapeDtypeStruct((t, cols), BF16),
                   jax.ShapeDtypeStruct((SUB, d), F32), jax.ShapeDtypeStruct((SUB, d), F32)],
        scratch_shapes=[pltpu.VMEM((2, tm, w), BF16), pltpu.SemaphoreType.DMA((2,))],
        compiler_params=_params(2),
    )(dmix, wo_full, ga, gb, ya, yb, *deps)


def _shifted_views(win, offsets):
    n = win.shape[0]
    rotated = {}
    views = {}
    for o in offsets:
        q, r = divmod(o, SUB)
        if r not in rotated:
            rotated[r] = win if r == 0 else pltpu.roll(win, n - r, 0)
        views[o] = rotated[r][q * SUB:q * SUB + CONV_CHUNK]
    return views


def _causal_views(xp_ref, ntap, r0):
    win = xp_ref[pl.ds(r0, CONV_CHUNK + CONV_PAD), :]
    views = _shifted_views(win, [CONV_PAD - (ntap - 1 - k) for k in range(ntap)])
    return [views[CONV_PAD - (ntap - 1 - k)] for k in range(ntap)]


def _causal_conv(xp_ref, w_ref, ntap, r0):
    acc = None
    for k, shifted in enumerate(_causal_views(xp_ref, ntap, r0)):
        term = w_ref[k:k + 1, :] * shifted
        acc = term if acc is None else acc + term
    return acc


def _anticausal_conv(xp_ref, w_ref, ntap, r0):
    win = xp_ref[pl.ds(pl.multiple_of(CONV_PAD + r0, CONV_PAD), CONV_CHUNK + CONV_PAD), :]
    views = _shifted_views(win, [ntap - 1 - k for k in range(ntap)])
    acc = None
    for k in range(ntap):
        term = w_ref[k:k + 1, :] * views[ntap - 1 - k]
        acc = term if acc is None else acc + term
    return acc


def _conv_weight_grad(dw_ref, d_chunk, xp_ref, ntap, r0):
    for k, shifted in enumerate(_causal_views(xp_ref, ntap, r0)):
        dw_ref[k * SUB:(k + 1) * SUB, :] += _colsum8(d_chunk * shifted)


def _zero_pads(ref, t):
    ref[0:CONV_PAD, :] = jnp.zeros((CONV_PAD, LANE), F32)
    ref[CONV_PAD + t:CONV_PAD + t + CONV_PAD, :] = jnp.zeros((CONV_PAD, LANE), F32)


def _for_chunks(t, fn):
    def step(idx, carry):
        fn(pl.multiple_of(idx * CONV_CHUNK, CONV_CHUNK))
        return carry

    lax.fori_loop(0, t // CONV_CHUNK, step, 0)


def _conv_forward(proj, conf_w, conf_b, short_w, dc, deps=()):
    t = proj.shape[0]
    nc = dc // LANE

    def body(av_ref, ag_ref, bg_ref, cg_ref, v_ref, cw_ref, cb_ref, sw_ref, *rest):
        a1_ref, s_ref, xa, xb = rest[len(deps):]
        _zero_pads(xa, t)
        _zero_pads(xb, t)
        xa[CONV_PAD:CONV_PAD + t, :] = av_ref[...] * _sigmoid(ag_ref[...])
        xb[CONV_PAD:CONV_PAD + t, :] = cg_ref[...] * v_ref[...]

        def chunk(r0):
            rs = pl.ds(r0, CONV_CHUNK)
            a1_ref[rs, :] = _causal_conv(xa, cw_ref, CONF_K, r0) + cb_ref[...]
            s_ref[rs, :] = (bg_ref[rs, :] * _causal_conv(xb, sw_ref, SHORT_K, r0)).astype(BF16)

        _for_chunks(t, chunk)

    col = lambda g: pl.BlockSpec((t, LANE), lambda c, g=g: (0, g * nc + c))
    return pl.pallas_call(
        body, name="conv_forward", grid=(nc,),
        in_specs=[col(0), col(1), col(2), col(3), col(4),
                  pl.BlockSpec((CONF_K, LANE), lambda c: (0, c)),
                  pl.BlockSpec((1, LANE), lambda c: (0, c)),
                  pl.BlockSpec((SHORT_K, LANE), lambda c: (0, c))] + [ANY] * len(deps),
        out_specs=[pl.BlockSpec((t, LANE), lambda c: (0, c)), pl.BlockSpec((t, LANE), lambda c: (0, c))],
        out_shape=[jax.ShapeDtypeStruct((t, dc), F32), jax.ShapeDtypeStruct((t, dc), BF16)],
        scratch_shapes=[pltpu.VMEM((t + 2 * CONV_PAD, LANE), F32), pltpu.VMEM((t + 2 * CONV_PAD, LANE), F32)],
        compiler_params=_params(1),
    )(proj, proj, proj, proj, proj, conf_w, conf_b, short_w, *deps)


def _conv_backward(dproj, proj, da1, ds, conf_w, short_w, dc):
    t = proj.shape[0]
    nc = dc // LANE

    def body(dp_in, av_ref, ag_ref, bg_ref, cg_ref, v_ref, da1_ref, ds_ref, cw_ref, sw_ref,
             dp_ref, dcw_ref, dcb_ref, dsw_ref, xa, xb, da, db, stage, sems):
        del dp_in
        c = pl.program_id(0)
        for ref in (xa, xb, da, db):
            _zero_pads(ref, t)
        xa[CONV_PAD:CONV_PAD + t, :] = av_ref[...] * _sigmoid(ag_ref[...])
        xb[CONV_PAD:CONV_PAD + t, :] = cg_ref[...] * v_ref[...]
        da[CONV_PAD:CONV_PAD + t, :] = da1_ref[...]
        dcw_ref[...] = jnp.zeros(dcw_ref.shape, F32)
        dsw_ref[...] = jnp.zeros(dsw_ref.shape, F32)
        dcb_ref[...] = jnp.zeros(dcb_ref.shape, F32)

        def copies(colblk):
            return [pltpu.make_async_copy(
                stage.at[g], dp_ref.at[:, pl.ds(pl.multiple_of((g * nc + colblk) * LANE, LANE), LANE)], sems.at[g])
                for g in range(5)]

        @pl.when(c > 0)
        def _():
            for cp in copies(0):
                cp.wait()

        def through_gate(r0):
            rs = pl.ds(r0, CONV_CHUNK)
            ds_c = ds_ref[rs, :]
            stage[2, rs, :] = (ds_c * _causal_conv(xb, sw_ref, SHORT_K, r0)).astype(BF16)
            db[pl.ds(pl.multiple_of(CONV_PAD + r0, CONV_PAD), CONV_CHUNK), :] = ds_c * bg_ref[rs, :]

        _for_chunks(t, through_gate)

        def through_convs(r0):
            rs = pl.ds(r0, CONV_CHUNK)
            da0 = _anticausal_conv(da, cw_ref, CONF_K, r0)
            sg = _sigmoid(ag_ref[rs, :])
            stage[0, rs, :] = (da0 * sg).astype(BF16)
            stage[1, rs, :] = (da0 * av_ref[rs, :] * sg * (1.0 - sg)).astype(BF16)
            dcv = _anticausal_conv(db, sw_ref, SHORT_K, r0)
            stage[3, rs, :] = (dcv * v_ref[rs, :]).astype(BF16)
            stage[4, rs, :] = (dcv * cg_ref[rs, :]).astype(BF16)
            da1_c = da1_ref[rs, :]
            _conv_weight_grad(dcw_ref, da1_c, xa, CONF_K, r0)
            _conv_weight_grad(dsw_ref, ds_ref[rs, :] * bg_ref[rs, :], xb, SHORT_K, r0)
            dcb_ref[...] += _colsum8(da1_c)

        _for_chunks(t, through_convs)
        for cp in copies(c):
            cp.start()

        @pl.when(c == pl.num_programs(0) - 1)
        def _():
            for cp in copies(0):
                cp.wait()

    col = lambda g: pl.BlockSpec((t, LANE), lambda c, g=g: (0, g * nc + c))
    blk = pl.BlockSpec((t, LANE), lambda c: (0, c))
    return pl.pallas_call(
        body, name="conv_backward", grid=(nc,),
        in_specs=[ANY, col(0), col(1), col(2), col(3), col(4), blk, blk,
                  pl.BlockSpec((CONF_K, LANE), lambda c: (0, c)),
                  pl.BlockSpec((SHORT_K, LANE), lambda c: (0, c))],
        out_specs=[ANY,
                   pl.BlockSpec((CONF_K * SUB, LANE), lambda c: (0, c)),
                   pl.BlockSpec((SUB, LANE), lambda c: (0, c)),
                   pl.BlockSpec((SHORT_K * SUB, LANE), lambda c: (0, c))],
        out_shape=[jax.ShapeDtypeStruct(dproj.shape, dproj.dtype),
                   jax.ShapeDtypeStruct((CONF_K * SUB, dc), F32),
                   jax.ShapeDtypeStruct((SUB, dc), F32),
                   jax.ShapeDtypeStruct((SHORT_K * SUB, dc), F32)],
        scratch_shapes=[pltpu.VMEM((t + 2 * CONV_PAD, LANE), F32)] * 4
                       + [pltpu.VMEM((5, t, LANE), BF16), pltpu.SemaphoreType.DMA((5,))],
        input_output_aliases={0: 0},
        compiler_params=_params(1),
    )(dproj, proj, proj, proj, proj, proj, da1, ds, conf_w, short_w)


def _adamw_math(w, g, m, v):
    m = ADAM_B1 * m + (1.0 - ADAM_B1) * g
    v = ADAM_B2 * v + (1.0 - ADAM_B2) * (g * g)
    m_hat = m / (1.0 - ADAM_B1 ** ADAM_STEP)
    v_hat = v / (1.0 - ADAM_B2 ** ADAM_STEP)
    delta = -ADAM_LR * (m_hat / (jnp.sqrt(v_hat) + ADAM_EPS) + ADAM_WD * w)
    return delta, m, v


def _cast_into_slot(name, w, me_arr, deps=()):
    r, c = w.shape
    tr = 256

    def body(me_ref, w_ref, *rest):
        del me_ref
        rest[-1][0] = w_ref[...].astype(BF16)

    return pl.pallas_call(
        body, name=name,
        grid_spec=pltpu.PrefetchScalarGridSpec(
            num_scalar_prefetch=1, grid=(r // tr,),
            in_specs=[pl.BlockSpec((tr, c), lambda i, me: (i, 0))] + [ANY] * len(deps),
            out_specs=pl.BlockSpec((1, tr, c), lambda i, me: (me[0], i, 0))),
        out_shape=jax.ShapeDtypeStruct((N_DEV, r, c), BF16),
        compiler_params=_params(1),
    )(me_arr, w, *deps)


def _chip_sum(name, full, from_sibling, me_arr):
    _, r, c = full.shape
    tr = min(r, 1024)

    def body(me_ref, full_ref, sib_ref, sums_ref):
        del me_ref
        sums_ref[0] = (full_ref[0].astype(F32) + sib_ref[0].astype(F32)).astype(BF16)

    other = lambda k, me: (me[0] // 2 + 1 + k) % 4
    return pl.pallas_call(
        body, name=name,
        grid_spec=pltpu.PrefetchScalarGridSpec(
            num_scalar_prefetch=1, grid=(r // tr, 3),
            in_specs=[pl.BlockSpec((1, tr, c), lambda i, k, me: (2 * other(k, me) + me[0] % 2, i, 0)),
                      pl.BlockSpec((1, tr, c), lambda i, k, me: (other(k, me), i, 0))],
            out_specs=pl.BlockSpec((1, tr, c), lambda i, k, me: (other(k, me), i, 0))),
        out_shape=jax.ShapeDtypeStruct((4, r, c), BF16),
        compiler_params=_params(2),
    )(me_arr, full, from_sibling)


def _adamw_shard(name, w, m, v, parts, me_arr, deps=()):
    r, c = w.shape
    tr = min(256, r // len(parts))
    np_ = len(parts)
    per = r // np_ // tr

    def body(me_ref, w_ref, m_ref, v_ref, *rest):
        g_out, d_out, m_out, v_out = rest[5 * np_ + len(deps):]
        g = None
        for p in range(np_):
            gp = rest[5 * p][...]
            for l_ref in rest[5 * p + 1:5 * p + 5]:
                gp = gp + l_ref[0].astype(F32)
            g = gp if g is None else jnp.where(pl.program_id(0) // per == p, gp, g)
        delta, m_new, v_new = _adamw_math(w_ref[...], g, m_ref[...], v_ref[...])
        g_out[...] = g
        d_out[...] = delta
        m_out[...] = m_new
        v_out[...] = v_new

    tile = pl.BlockSpec((tr, c), lambda i, me: (i, 0))
    part_specs, part_args = [], []
    for p, (g_own, from_sibling, landed) in enumerate(parts):
        row = lambda i, p=p: jnp.clip(i - p * per, 0, per - 1)
        part_specs.append(pl.BlockSpec((tr, c), lambda i, me, row=row: (row(i), 0)))
        part_specs += [pl.BlockSpec((1, tr, c), lambda i, me, k=k, row=row: ((me[0] // 2 + k) % 4, row(i), 0))
                       for k in range(4)]
        part_args += [g_own, from_sibling, landed, landed, landed]
    return pl.pallas_call(
        body, name=name,
        grid_spec=pltpu.PrefetchScalarGridSpec(
            num_scalar_prefetch=1, grid=(r // tr,),
            in_specs=[tile] * 3 + part_specs + [ANY] * len(deps), out_specs=[tile] * 4),
        out_shape=[jax.ShapeDtypeStruct((r, c), F32)] * 4,
        compiler_params=_params(1),
    )(me_arr, w, m, v, *part_args, *deps)


SMALL_W = 1024
VEC_ROWS = 16
LOSS_ROW = 15
META_ROW0 = 16
CONF_ROW0 = 64
SHORT_ROW0 = 96
SMALL_ROWS = 104


def _pack_small(vec_parts, dmeta, dcw, dsw, loss_blk, me_arr):
    widths = [p.shape[1] for p in vec_parts]
    nv = len(vec_parts)

    def body(me_ref, *refs):
        del me_ref
        parts, (dmeta_ref, dcw_ref, dsw_ref, loss_ref, out_ref) = refs[:nv], refs[nv:]
        out_ref[0] = jnp.zeros((SMALL_ROWS, SMALL_W), F32)
        out_ref[0, LOSS_ROW:LOSS_ROW + 1, 0:LANE] = loss_ref[0:1, :]
        row = 0
        for p_ref, wd in zip(parts, widths):
            s = jnp.sum(p_ref[...], axis=0, keepdims=True)
            for h in range(wd // SMALL_W):
                out_ref[0, row:row + 1, :] = s[:, h * SMALL_W:(h + 1) * SMALL_W]
                row += 1
        for h in range(dmeta_ref.shape[1] // SMALL_W):
            out_ref[0, META_ROW0 + h * N_META:META_ROW0 + (h + 1) * N_META, :] = dmeta_ref[:, h * SMALL_W:(h + 1) * SMALL_W]
        for k in range(CONF_K):
            out_ref[0, CONF_ROW0 + k:CONF_ROW0 + k + 1, :] = jnp.sum(dcw_ref[k * SUB:(k + 1) * SUB, :], axis=0, keepdims=True)
        for k in range(SHORT_K):
            out_ref[0, SHORT_ROW0 + k:SHORT_ROW0 + k + 1, :] = jnp.sum(dsw_ref[k * SUB:(k + 1) * SUB, :], axis=0, keepdims=True)

    ins = [*vec_parts, dmeta, dcw, dsw, loss_blk]
    return pl.pallas_call(
        body, name="pack_small",
        grid_spec=pltpu.PrefetchScalarGridSpec(
            num_scalar_prefetch=1, grid=(1,),
            in_specs=[pl.BlockSpec(a.shape, lambda i, me: (0, 0)) for a in ins],
            out_specs=pl.BlockSpec((1, SMALL_ROWS, SMALL_W), lambda i, me: (me[0], 0, 0))),
        out_shape=jax.ShapeDtypeStruct((N_DEV, SMALL_ROWS, SMALL_W), F32),
        compiler_params=_params(1),
    )(me_arr, *ins)


def _small_update(gathered, me_arr, vec_params, meta_p, conf_p, short_p):
    widths = [p[0].shape[1] for p in vec_params]
    nv = len(vec_params)
    mcols = meta_p[0].shape[1]
    per_row = SMALL_W // mcols

    def body(me_ref, gv_ref, gm_ref, gc_ref, gs_ref, *rest):
        del me_ref
        ins, outs = rest[:3 * (nv + 3)], rest[3 * (nv + 3):]

        def total(ref, r0, rows):
            s = ref[0, r0:r0 + rows, :]
            for dev in range(1, N_DEV):
                s = s + ref[dev, r0:r0 + rows, :]
            return s

        grads = []
        row = 0
        for wd in widths:
            pieces = [total(gv_ref, row + h, 1) for h in range(wd // SMALL_W)]
            grads.append(pieces[0] if len(pieces) == 1 else jnp.concatenate(pieces, axis=1))
            row += len(pieces)
        grads.append(total(gm_ref, 0, N_META))
        grads.append(total(gc_ref, 0, CONF_K))
        grads.append(total(gs_ref, 0, SHORT_K))
        loss = gv_ref[0, LOSS_ROW:LOSS_ROW + 1, 0:LANE]
        for dev in range(1, N_DEV):
            loss = loss + gv_ref[dev, LOSS_ROW:LOSS_ROW + 1, 0:LANE]
        outs[-1][...] = loss
        for idx, g in enumerate(grads):
            w_ref, m_ref, v_ref = ins[3 * idx:3 * idx + 3]
            delta, m_new, v_new = _adamw_math(w_ref[...], g, m_ref[...], v_ref[...])
            g_out, d_out, m_out, v_out = outs[4 * idx:4 * idx + 4]
            g_out[...] = g
            d_out[...] = delta
            m_out[...] = m_new
            v_out[...] = v_new

    params = list(vec_params) + [meta_p, conf_p, short_p]
    flat = [a for p in params for a in p]
    whole = lambda a: pl.BlockSpec(a.shape, lambda i, me: (0,) * a.ndim)
    outs = pl.pallas_call(
        body, name="small_update",
        grid_spec=pltpu.PrefetchScalarGridSpec(
            num_scalar_prefetch=1, grid=(1,),
            in_specs=[pl.BlockSpec((N_DEV, VEC_ROWS, SMALL_W), lambda i, me: (0, 0, 0)),
                      pl.BlockSpec((N_DEV, N_META, mcols),
                                   lambda i, me: (0, META_ROW0 // N_META + me[0] // per_row, me[0] % per_row)),
                      pl.BlockSpec((N_DEV, 32, LANE), lambda i, me: (0, CONF_ROW0 // 32, me[0])),
                      pl.BlockSpec((N_DEV, SUB, LANE), lambda i, me: (0, SHORT_ROW0 // SUB, me[0]))]
                     + [whole(a) for a in flat],
            out_specs=[whole(p[0]) for p in params for _ in range(4)]
                      + [pl.BlockSpec((1, LANE), lambda i, me: (0, 0))]),
        out_shape=[jax.ShapeDtypeStruct(p[0].shape, F32) for p in params for _ in range(4)]
                  + [jax.ShapeDtypeStruct((1, LANE), F32)],
        compiler_params=_params(1),
    )(me_arr, gathered, gathered, gathered, gathered, *flat)
    return [tuple(outs[4 * i:4 * i + 4]) for i in range(len(params))], outs[-1][0, 0]


def kernel(x, meta, g_pre_mix, w_in, b_gates, conf_dw_w, conf_dw_b, conf_ln_g, conf_ln_b, conf_w_pw, short_dw_w, short_w_out, w_o, g_post_mix, g_pre_mlp, w_up, w_down, g_post_mlp, loss_target, m_meta, m_g_pre_mix, m_w_in, m_b_gates, m_conf_dw_w, m_conf_dw_b, m_conf_ln_g, m_conf_ln_b, m_conf_w_pw, m_short_dw_w, m_short_w_out, m_w_o, m_g_post_mix, m_g_pre_mlp, m_w_up, m_w_down, m_g_post_mlp, v_meta, v_g_pre_mix, v_w_in, v_b_gates, v_conf_dw_w, v_conf_dw_b, v_conf_ln_g, v_conf_ln_b, v_conf_w_pw, v_short_dw_w, v_short_w_out, v_w_o, v_g_post_mix, v_g_pre_mlp, v_w_up, v_w_down, v_g_post_mlp):
    seq, d = x.shape[1], x.shape[2]
    dc = conf_w_pw.shape[1]
    t_real = N_META + seq
    t = -(-t_real // ROW_TILE) * ROW_TILE
    tm = t // 2
    assert tm % 16 == 0 and d % 1024 == 0 and dc % 1024 == 0
    x_idx, y_idx, c_idx = _position()
    me_arr = jnp.reshape(4 * x_idx + 2 * y_idx + c_idx, (1,)).astype(jnp.int32)

    big = [w_in[0], conf_w_pw[0], short_w_out[0], w_o[0], w_up[0], w_down[0]]
    big_names = ["w_in", "conf_w_pw", "short_w_out", "w_o", "w_up", "w_down"]
    groups = [[0], [1, 2, 3], [4], [5]]
    slots, deps = [], []
    for g, idxs in enumerate(groups):
        slots.append([_cast_into_slot("cast_" + big_names[i], big[i], me_arr, deps=deps) for i in idxs])
        if g == 0:
            direct0 = _remote_start("gather0_direct_start", "gather_direct", slots[0])
            deps = [direct0[3]]
    casts = [sl for group in slots[1:] for sl in group]
    meta_g, cw_g, sw_g = _all_gather("gather_small_params", [meta, conf_dw_w[0], short_dw_w[0]], deps=casts)

    def start_direct(g, deps):
        send, recv, bufs, tok = _remote_start("gather%d_direct_start" % g, "gather_direct", slots[g], deps=deps)
        return (send, recv, bufs), tok

    def relay(g, state, after):
        send, recv, bufs, tok = _remote_pass_on("gather%d_relay" % g, "gather_direct", *state, after, "gather_relay")
        return (send, recv, bufs), tok

    def gathered(g, state, after):
        send, recv, bufs, tok = _remote_pass_on("gather%d_diag" % g, "gather_relay", *state, after, "gather_diag")
        return _remote_wait("gather%d_diag_wait" % g, "gather_diag", send, recv, bufs, len(bufs), [tok])

    unshard =lambda g: jnp.transpose(g, (1, 0, 2)).reshape(g.shape[1], -1)
    meta_full, cw_full, sw_full = unshard(meta_g), unshard(cw_g), unshard(sw_g)

    relay0, tok = relay(0, direct0[:3], [meta_g])
    zrows = jnp.zeros((t - t_real, d), F32) + tok[0, 0] * 0.0
    h0 = jnp.concatenate([meta_full, x[0], zrows], axis=0)
    tgt = jnp.concatenate([jnp.zeros((N_META, d), F32), loss_target[0], zrows], axis=0)
    n = _pre_norm(h0, g_pre_mix)
    direct1, tok = start_direct(1, [tok])
    direct2, tok = start_direct(2, [tok])
    win_g, = gathered(0, relay0, [tok, n])
    proj = _mm_cols_pairs("proj", n, win_g, tm=tm // 2)
    relay1, tok = relay(1, direct1, [proj])
    a1, s = _conv_forward(proj, cw_full, conf_dw_b, sw_full, dc, deps=[tok])
    relay2, tok = relay(2, direct2, [a1])
    direct3, tok = start_direct(3, [tok])
    a3 = _layer_norm_silu(a1, conf_ln_g, conf_ln_b, deps=[tok])
    wpw_g, wso_g, wo_g = gathered(1, relay1, [a3])
    wo_full = wo_g.reshape(d, d)
    ya, yb, gate_a, gate_b, m_mix = _branch_merge(a3, s, wpw_g, wso_g, proj, b_gates, d)
    mix, h1, n2 = _mix_post(m_mix, wo_full, h0, g_post_mix, g_pre_mlp)
    wup_g, = gathered(2, relay2, [n2])

    def up_epilogue(acc):
        r = jnp.maximum(acc, 0.0)
        return r * r, r

    half_up = dict(tm=tm, epilogue=up_epilogue, out_dtypes=(BF16, BF16))
    f, relu_up = _mm_cols("mlp_up0", n2, wup_g, blocks=(0, N_DEV // 2), **half_up)
    relay3, tok = relay(3, direct3, [f])
    f, relu_up = _mm_cols("mlp_up1", n2, wup_g, blocks=(N_DEV // 2, N_DEV), into=(f, relu_up), deps=[tok], **half_up)
    wdn_g, = gathered(3, relay3, [f])
    wdn_full = wdn_g.reshape(-1, d)
    fo = _mm_rows("mlp_down", f, wdn_full, tm=tm, tn=512)
    dfo, dh2, dg_post_mlp, loss_blk = _loss_head(fo, h1, tgt, g_post_mlp, t_real)

    def reduce_start(tag, fulls, deps):
        lands = [lax.empty((4,) + g.shape[1:], BF16) for g in fulls]
        send, recv, bufs, tok = _remote_start("reduce_%s_d2d_start" % tag, "reduce_d2d", fulls, lands, deps=deps)
        return (send, recv, bufs), tok

    def reduce_middle(tag, state, owns, after):
        send, recv, bufs = state
        k = len(owns)
        bufs = _remote_wait("reduce_%s_d2d_wait" % tag, "reduce_d2d", send, recv, bufs, k, after)
        from_sibling = bufs[k:]
        sums = [_chip_sum("chip_sum_%s%d" % (tag, i), bufs[i], from_sibling[i], me_arr) for i in range(k)]
        lands = [lax.empty(sm.shape, BF16) for sm in sums]
        send, recv, bufs, tok = _remote_start("reduce_%s_ici_start" % tag, "reduce_ici", sums, lands)
        return (send, recv, bufs, list(zip(owns, from_sibling))), tok

    def reduce_finish(tag, state, after):
        send, recv, bufs, local = state
        k = len(local)
        bufs = _remote_wait("reduce_%s_ici_wait" % tag, "reduce_ici", send, recv, bufs, k, after)
        return [(own, sib, landed) for (own, sib), landed in zip(local, bufs[k:])]

    dup = _mm_nt_blocks("d_up", dfo, wdn_full, tm=tm, tkb=1024, extra=(relu_up,),
                        epilogue=lambda acc, r: (acc * (2.0 * r.astype(F32)),), out_dtypes=(BF16,))[0]
    gw_down, gw_down_own = _mm_tn("dw_down", f, dfo, me_arr, m=f.shape[1], n=d, tma=512, tn=d, sharded="rows")
    red_down, tok = reduce_start("down", [gw_down], ())
    dn2 = _mm_nt_acc_parts("d_n2", dup, wup_g, tm=tm, tn=512, deps=[tok])
    gw_up, gw_up_own = _mm_tn("dw_up", n2, dup, me_arr, m=d, n=dup.shape[1], tma=512, tn=2048, sharded="cols")
    red_down, tok = reduce_middle("down", red_down, [gw_down_own], [dn2])
    red_up, tok = reduce_start("up", [gw_up], [tok])
    dh1, dmix, dg_pre_mlp, dg_post_mix = _mid_norm_bwd(dn2, h1, dh2, mix, g_pre_mlp, g_post_mix, deps=[tok])
    dya, dyb, dproj, db_a, db_b = _gate_backward(dmix, wo_full, gate_a, gate_b, ya, yb, proj.shape[1], tm // 2)
    db_gates = jnp.concatenate([db_a, db_b], axis=1)
    red_up, tok = reduce_middle("up", red_up, [gw_up_own], [dya])
    gw_o, gw_o_own = _mm_tn("dw_o", m_mix, dmix, me_arr, m=d, n=d, tma=d // N_DEV, tn=d, sharded="rows", deps=[tok])
    da3 = _mm_nt_acc("d_a3", dya, wpw_g, tm=tm, tn=512)
    gw_pw, gw_pw_own = _mm_tn("dw_pw", a3, dya, me_arr, m=dc, n=d, tma=512, tn=d, sharded="cols")
    dsb = _mm_nt_acc("d_s", dyb, wso_g, tm=tm, tn=512)
    gw_so, gw_so_own = _mm_tn("dw_so", s, dyb, me_arr, m=dc, n=d, tma=512, tn=d, sharded="cols")
    red_mix, tok = reduce_start("mix", [gw_pw, gw_so, gw_o], ())
    da1, dln_g, dln_b = _layer_norm_silu_bwd(da3, a1, conf_ln_g, conf_ln_b, deps=[tok])
    dproj, dcw, dcb, dsw = _conv_backward(dproj, proj, da1, dsb, cw_full, sw_full, dc)
    red_mix, tok = reduce_middle("mix", red_mix, [gw_pw_own, gw_so_own, gw_o_own], [dcb])
    in_cb = w_in.shape[2]
    half = d // 2
    red_in = []
    for part in range(2):
        gw, own = _mm_tn("dw_in%d" % part, n, dproj, me_arr, m=half, n=proj.shape[1], tma=512, tn=2 * in_cb,
                         sharded="cols", a_off=part * (half // 512), deps=[tok])
        state, tok = reduce_start("in%d" % part, [gw], ())
        red_in.append((state, own))
    for part in range(2):
        state, own = red_in[part]
        red_in[part], tok = reduce_middle("in%d" % part, state, [own], [tok])
    dn = _mm_nt_acc("d_n", dproj, win_g, tm=tm // 2, tn=512, deps=[tok])
    grad_x, dmeta, dg_pre_mix = _pre_norm_bwd(dn, h0, dh1, g_pre_mix, t_real)
    grad_x = grad_x[None]

    vec_parts = [dg_pre_mix, db_gates, dcb, dln_g, dln_b, dg_post_mix, dg_pre_mlp, dg_post_mlp]
    packed = _pack_small(vec_parts, dmeta, dcw, dsw, loss_blk, me_arr)
    send, recv, bufs, tok = _remote_start("small_grads_ici_start", "gather_ici", [packed])
    vec_names = ["g_pre_mix", "b_gates", "conf_dw_b", "conf_ln_g", "conf_ln_b", "g_post_mix", "g_pre_mlp", "g_post_mlp"]
    env = locals()
    results = {}

    def update(nm, parts, deps=()):
        res = _adamw_shard("adamw_" + nm, env[nm][0], env["m_" + nm][0], env["v_" + nm][0], parts, me_arr, deps=deps)
        results[nm] = tuple(r[None] for r in res)
        return res[0]

    done = [update("w_down", reduce_finish("down", red_down, [tok]), deps=[tok])]
    done.append(update("w_up", reduce_finish("up", red_up, done)))
    bufs = _remote_wait("small_grads_ici_wait", "gather_ici", send, recv, bufs, 1, done)
    send, recv, bufs, tok = _remote_start("small_grads_d2d_start", "gather_d2d", bufs)
    for nm, pair in zip(["conf_w_pw", "short_w_out", "w_o"], reduce_finish("mix", red_mix, [tok])):
        done.append(update(nm, [pair], deps=[tok]))
    small_g, = _remote_wait("small_grads_d2d_wait", "gather_d2d", send, recv, bufs, 1, done)
    triple = lambda nm, sq: tuple(env[p + nm][0] if sq else env[p + nm] for p in ("", "m_", "v_"))
    small, loss = _small_update(small_g, me_arr, [triple(nm, False) for nm in vec_names],
                                triple("meta", False), triple("conf_dw_w", True), triple("short_dw_w", True))
    for nm, res in zip(vec_names + ["meta"], small[:len(vec_names) + 1]):
        results[nm] = res
    results["conf_dw_w"] = tuple(r[None] for r in small[-2])
    results["short_dw_w"] = tuple(r[None] for r in small[-1])
    update("w_in", [reduce_finish("in%d" % part, red_in[part], [small[0][0]])[0] for part in range(2)])

    order = ["meta", "g_pre_mix", "w_in", "b_gates", "conf_dw_w", "conf_dw_b", "conf_ln_g", "conf_ln_b", "conf_w_pw",
             "short_dw_w", "short_w_out", "w_o", "g_post_mix", "g_pre_mlp", "w_up", "w_down", "g_post_mlp"]
    return (loss, grad_x, *[results[nm][0] for nm in order], *[results[nm][1] for nm in order],
            *[results[nm][2] for nm in order], *[results[nm][3] for nm in order])
```

```python
import jax
import jax.numpy as jnp
from jax import lax
from jax.experimental import pallas as pl
from jax.experimental.pallas import tpu as pltpu

N_DEV = 8
N_META = 16
CONF_K = 31
SHORT_K = 3
RMS_EPS = 1e-6
LN_EPS = 1e-5
ADAM_LR = 0.001
ADAM_B1 = 0.9
ADAM_B2 = 0.999
ADAM_EPS = 1e-08
ADAM_WD = 0.01
ADAM_STEP = 10

LANE = 128
SUB = 8
ROW_TILE = 128
CONV_PAD = 32
CONV_CHUNK = 128
VMEM_LIMIT = 56 * 1024 * 1024

F32 = jnp.float32
BF16 = jnp.bfloat16
MESH = pl.DeviceIdType.MESH
ANY = pl.BlockSpec(memory_space=pl.ANY)
HBM_SPEC = pl.BlockSpec(memory_space=pltpu.HBM)
SEM_SPEC = pl.BlockSpec(memory_space=pltpu.SEMAPHORE)
EFFECT = pltpu.SideEffectType.DATAFLOW_SIDE_EFFECTING


def _params(n_axes):
    return pltpu.CompilerParams(dimension_semantics=("arbitrary",) * n_axes, vmem_limit_bytes=VMEM_LIMIT)


def _sigmoid(z):
    return 1.0 / (1.0 + jnp.exp(-z))


def _colsum8(v):
    r, c = v.shape
    return jnp.sum(v.reshape(r // SUB, SUB, c), axis=0)


def _position():
    x, y, c = lax.axis_index("x"), lax.axis_index("y"), lax.axis_index("c")
    return x, y, c


def _flat(p):
    return 4 * p[0] + 2 * p[1] + p[2]


def _all_gather(name, shards, deps=()):
    n, nd = len(shards), len(deps)

    def body(*refs):
        ins, outs = refs[:n], refs[n + nd:2 * n + nd]
        send_sems, recv_sems, local_sems = refs[2 * n + nd:]
        x, y, c = _position()
        me, sibling = (x, y, c), (x, y, 1 - c)
        chips = [(1 - x, y), (x, 1 - y), (1 - x, 1 - y)]

        def copy(q, k, block, to, src=None):
            dst = outs[q].at[_flat(block)]
            return pltpu.make_async_remote_copy(
                src_ref=dst if src is None else src, dst_ref=dst,
                send_sem=send_sems.at[q, k], recv_sem=recv_sems.at[q, k],
                device_id=to, device_id_type=MESH)

        mine = [pltpu.make_async_copy(ins[q], outs[q].at[_flat(me)], local_sems.at[q]) for q in range(n)]
        for cp in mine:
            cp.start()
        first = []
        for q in range(n):
            first.append(copy(q, 0, me, sibling, src=ins[q]))
            for j, chip in enumerate(chips):
                first.append(copy(q, 1 + j, me, (*chip, c), src=ins[q]))
        for cp in first:
            cp.start()
        passed = []
        for q in range(n):
            for j, chip in enumerate(chips):
                copy(q, 1 + j, (*chip, c), me).wait_recv()
                fwd = copy(q, 4 + j, (*chip, c), sibling)
                fwd.start()
                passed.append(fwd)
        for q in range(n):
            copy(q, 0, sibling, me).wait_recv()
            for j, chip in enumerate(chips):
                copy(q, 4 + j, (*chip, 1 - c), me).wait_recv()
        for cp in first + passed:
            cp.wait_send()
        for cp in mine:
            cp.wait()

    return pl.pallas_call(
        body, name=name,
        in_specs=[ANY] * (n + nd), out_specs=[ANY] * n,
        out_shape=[jax.ShapeDtypeStruct((N_DEV,) + s.shape, s.dtype) for s in shards],
        scratch_shapes=[pltpu.SemaphoreType.DMA((n, 7)), pltpu.SemaphoreType.DMA((n, 7)),
                        pltpu.SemaphoreType.DMA((n,))],
    )(*shards, *deps)


N_COPIES = {"gather_ici": 4, "gather_d2d": 3, "gather_direct": 3, "gather_relay": 3, "gather_diag": 1,
            "reduce_d2d": 4, "reduce_ici": 3}


def _copy_plan(kind):
    x, y, c = _position()
    me, sibling = (x, y, c), (x, y, 1 - c)
    chips = [(1 - x, y), (x, 1 - y), (1 - x, 1 - y)]
    if kind == "gather_ici":
        return [(_flat(me), _flat(me), sibling)] + [(_flat(me), _flat(me), (*ch, c)) for ch in chips]
    if kind == "gather_d2d":
        return [(_flat((*ch, c)), _flat((*ch, c)), sibling) for ch in chips]
    if kind == "gather_direct":
        return [(_flat(me), _flat(me), sibling)] + [(_flat(me), _flat(me), (*ch, c)) for ch in chips[:2]]
    if kind == "gather_relay":
        held, to = (x ^ (1 - c), y ^ c, c), (x ^ c, y ^ (1 - c), c)
        return [(_flat(held), _flat(held), to)] + [(_flat((*ch, c)), _flat((*ch, c)), sibling) for ch in chips[:2]]
    if kind == "gather_diag":
        return [(_flat((*chips[2], c)), _flat((*chips[2], c)), sibling)]
    if kind == "reduce_d2d":
        return [(2 * chip + (1 - c), chip, sibling) for chip in range(4)]
    return [(2 * ch[0] + ch[1], 2 * x + y, (*ch, c)) for ch in chips]


def _planned_copies(kind, srcs, dsts, send_sems, recv_sems):
    plan = _copy_plan(kind)
    return [pltpu.make_async_remote_copy(
        src_ref=src.at[s_slot], dst_ref=dst.at[d_slot],
        send_sem=send_sems.at[q * len(plan) + k], recv_sem=recv_sems.at[q * len(plan) + k],
        device_id=to, device_id_type=MESH)
        for q, (src, dst) in enumerate(zip(srcs, dsts)) for k, (s_slot, d_slot, to) in enumerate(plan)]


def _remote_start(name, kind, srcs, lands=None, deps=()):
    n = len(srcs)
    bufs = list(srcs) + ([] if lands is None else list(lands))
    nb, nd = len(bufs), len(deps)
    nsem = n * N_COPIES[kind]

    def body(*refs):
        ins = refs[:nb]
        send_sems, recv_sems = refs[nb + nd], refs[nb + nd + 1]
        token = refs[-1]
        for cp in _planned_copies(kind, ins[:n], ins[:n] if lands is None else ins[n:], send_sems, recv_sems):
            cp.start()
        token[...] = jnp.zeros_like(token)

    outs = pl.pallas_call(
        body, name=name,
        out_shape=(pltpu.SemaphoreType.DMA((nsem,)), pltpu.SemaphoreType.DMA((nsem,)),
                   *[pltpu.HBM(b.shape, b.dtype) for b in bufs], jax.ShapeDtypeStruct((SUB, LANE), F32)),
        in_specs=[HBM_SPEC] * nb + [ANY] * nd,
        out_specs=(SEM_SPEC, SEM_SPEC, *[HBM_SPEC] * nb, pl.BlockSpec(memory_space=pltpu.VMEM)),
        input_output_aliases={i: 2 + i for i in range(nb)},
        compiler_params=pltpu.CompilerParams(has_side_effects=EFFECT),
    )(*[pltpu.with_memory_space_constraint(b, pltpu.HBM) for b in bufs], *deps)
    return outs[0], outs[1], list(outs[2:2 + nb]), outs[-1]


def _remote_wait(name, kind, send_sems, recv_sems, bufs, n, after):
    nb, na = len(bufs), len(after)
    same = nb == n

    def body(*refs):
        ins = refs[:nb]
        sends, recvs = refs[nb], refs[nb + 1]
        for cp in _planned_copies(kind, ins[:n], ins[:n] if same else ins[n:], sends, recvs):
            cp.wait_send()
            cp.wait_recv()

    outs = pl.pallas_call(
        body, name=name,
        out_shape=[pltpu.HBM(b.shape, b.dtype) for b in bufs],
        in_specs=[HBM_SPEC] * nb + [SEM_SPEC, SEM_SPEC] + [ANY] * na,
        out_specs=[HBM_SPEC] * nb,
        input_output_aliases={i: i for i in range(nb)},
        compiler_params=pltpu.CompilerParams(has_side_effects=EFFECT),
    )(*bufs, send_sems, recv_sems, *after)
    return list(outs)


def _remote_pass_on(name, done, send_sems, recv_sems, bufs, after, nxt):
    nb, na = len(bufs), len(after)
    nsem = nb * N_COPIES[nxt]

    def body(*refs):
        ins = refs[:nb]
        new_sends, new_recvs = refs[nb + 2 + na], refs[nb + 3 + na]
        token = refs[-1]
        for cp in _planned_copies(done, ins, ins, refs[nb], refs[nb + 1]):
            cp.wait_send()
            cp.wait_recv()
        for cp in _planned_copies(nxt, ins, ins, new_sends, new_recvs):
            cp.start()
        token[...] = jnp.zeros_like(token)

    outs = pl.pallas_call(
        body, name=name,
        out_shape=(pltpu.SemaphoreType.DMA((nsem,)), pltpu.SemaphoreType.DMA((nsem,)),
                   *[pltpu.HBM(b.shape, b.dtype) for b in bufs], jax.ShapeDtypeStruct((SUB, LANE), F32)),
        in_specs=[HBM_SPEC] * nb + [SEM_SPEC, SEM_SPEC] + [ANY] * na,
        out_specs=(SEM_SPEC, SEM_SPEC, *[HBM_SPEC] * nb, pl.BlockSpec(memory_space=pltpu.VMEM)),
        input_output_aliases={i: 2 + i for i in range(nb)},
        compiler_params=pltpu.CompilerParams(has_side_effects=EFFECT),
    )(*bufs, send_sems, recv_sems, *after)
    return outs[0], outs[1], list(outs[2:2 + nb]), outs[-1]


def _mm_cols(name, a, w, *, tm, blocks, epilogue, out_dtypes, into=(), deps=()):
    t, k = a.shape
    nblk, _, cb = w.shape
    j0, j1 = blocks
    no = len(out_dtypes)

    def body(a_ref, w_ref, *rest):
        acc = jnp.dot(a_ref[...], w_ref[0], preferred_element_type=F32)
        for o_ref, o in zip(rest[len(into) + len(deps):], epilogue(acc)):
            o_ref[...] = o.astype(o_ref.dtype)

    return pl.pallas_call(
        body, name=name, grid=(j1 - j0, t // tm),
        in_specs=[pl.BlockSpec((tm, k), lambda j, i: (i, 0)),
                  pl.BlockSpec((1, k, cb), lambda j, i: (j0 + j, 0, 0))] + [ANY] * (len(into) + len(deps)),
        out_specs=[pl.BlockSpec((tm, cb), lambda j, i: (i, j0 + j)) for _ in range(no)],
        out_shape=[jax.ShapeDtypeStruct((t, nblk * cb), dt) for dt in out_dtypes],
        input_output_aliases={2 + idx: idx for idx in range(len(into))},
        compiler_params=_params(2),
    )(a, w, *into, *deps)


MXU_WIDTH = 256


def _mm_cols_pairs(name, a, w, *, tm):
    t, k = a.shape
    nblk, _, cb = w.shape
    main = cb // MXU_WIDTH * MXU_WIDTH
    tail = cb - main
    assert 2 * tail == MXU_WIDTH and nblk % 2 == 0

    def body(a_ref, w_ref, o_ref):
        av = a_ref[...]
        for b in range(2):
            o_ref[:, b * cb:b * cb + main] = jnp.dot(av, w_ref[b, :, 0:main], preferred_element_type=F32)
        tails = jnp.dot(av, jnp.concatenate([w_ref[0, :, main:cb], w_ref[1, :, main:cb]], axis=1),
                        preferred_element_type=F32)
        for b in range(2):
            o_ref[:, b * cb + main:(b + 1) * cb] = tails[:, b * tail:(b + 1) * tail]

    return pl.pallas_call(
        body, name=name, grid=(nblk // 2, t // tm),
        in_specs=[pl.BlockSpec((tm, k), lambda j, i: (i, 0)),
                  pl.BlockSpec((2, k, cb), lambda j, i: (j, 0, 0))],
        out_specs=pl.BlockSpec((tm, 2 * cb), lambda j, i: (i, j)),
        out_shape=jax.ShapeDtypeStruct((t, nblk * cb), F32),
        compiler_params=_params(2),
    )(a, w)


def _add_columns(o_ref, j, tn, acc, first):
    for jj in range(o_ref.shape[1] // tn):
        cols = slice(jj * tn, (jj + 1) * tn)

        @pl.when(jnp.logical_and(j == jj, first))
        def _(cols=cols):
            o_ref[:, cols] = acc

        @pl.when(jnp.logical_and(j == jj, jnp.logical_not(first)))
        def _(cols=cols):
            o_ref[:, cols] += acc


def _mm_rows(name, a, w2d, *, tm, tn, kparts=2):
    t, kf = a.shape
    n = w2d.shape[1]
    kp = kf // kparts

    def body(a_ref, w_ref, o_ref):
        acc = jnp.dot(a_ref[...], w_ref[...], preferred_element_type=F32)
        _add_columns(o_ref, pl.program_id(2), tn, acc, pl.program_id(1) == 0)

    return pl.pallas_call(
        body, name=name, grid=(t // tm, kparts, n // tn),
        in_specs=[pl.BlockSpec((tm, kp), lambda i, kh, j: (i, kh)),
                  pl.BlockSpec((kp, tn), lambda i, kh, j: (kh, j))],
        out_specs=pl.BlockSpec((tm, n), lambda i, kh, j: (i, 0)),
        out_shape=jax.ShapeDtypeStruct((t, n), F32),
        compiler_params=_params(3),
    )(a, w2d)


def _mm_nt_acc_parts(name, dy, w, *, tm, tn, kparts=2, deps=()):
    t = dy.shape[0]
    nblk, k, cb = w.shape
    per = nblk // kparts
    assert cb % MXU_WIDTH == 0

    def body(dy_ref, w_ref, *rest):
        acc = None
        for b in range(per):
            d = lax.dot_general(dy_ref[:, b * cb:(b + 1) * cb], w_ref[b], (((1,), (1,)), ((), ())),
                                preferred_element_type=F32)
            acc = d if acc is None else acc + d
        _add_columns(rest[-1], pl.program_id(2), tn, acc, pl.program_id(1) == 0)

    return pl.pallas_call(
        body, name=name, grid=(t // tm, kparts, k // tn),
        in_specs=[pl.BlockSpec((tm, per * cb), lambda i, kh, j: (i, kh)),
                  pl.BlockSpec((per, tn, cb), lambda i, kh, j: (kh, j, 0))] + [ANY] * len(deps),
        out_specs=pl.BlockSpec((tm, k), lambda i, kh, j: (i, 0)),
        out_shape=jax.ShapeDtypeStruct((t, k), F32),
        compiler_params=_params(3),
    )(dy, w, *deps)


def _mm_nt_acc(name, dy, w, *, tm, tn, col_off=0, deps=()):
    t = dy.shape[0]
    nblk, k, cb = w.shape

    main = cb // MXU_WIDTH * MXU_WIDTH

    def body(dy_ref, w_ref, *rest):
        nt = (((1,), (1,)), ((), ()))
        acc = None
        for b in range(nblk):
            d = lax.dot_general(dy_ref[:, b * cb:b * cb + main], w_ref[b, :, 0:main], nt, preferred_element_type=F32)
            acc = d if acc is None else acc + d
        if main < cb:
            dy_tails = jnp.concatenate([dy_ref[:, b * cb + main:(b + 1) * cb] for b in range(nblk)], axis=1)
            w_tails = jnp.concatenate([w_ref[b, :, main:cb] for b in range(nblk)], axis=1)
            acc = acc + lax.dot_general(dy_tails, w_tails, nt, preferred_element_type=F32)
        rest[-1][...] = acc

    return pl.pallas_call(
        body, name=name, grid=(t // tm, k // tn),
        in_specs=[pl.BlockSpec((tm, nblk * cb), lambda i, j: (i, col_off)),
                  pl.BlockSpec((nblk, tn, cb), lambda i, j: (0, j, 0))] + [ANY] * len(deps),
        out_specs=pl.BlockSpec((tm, tn), lambda i, j: (i, j)),
        out_shape=jax.ShapeDtypeStruct((t, k), F32),
        compiler_params=_params(2),
    )(dy, w, *deps)


def _mm_nt_blocks(name, dy, w2d, *, tm, tkb, extra=(), epilogue=None, out_dtypes=(F32,)):
    t, n = dy.shape
    kf = w2d.shape[0]
    ne = len(extra)

    def body(dy_ref, w_ref, *rest):
        acc = lax.dot_general(dy_ref[...], w_ref[...], (((1,), (1,)), ((), ())), preferred_element_type=F32)
        outs = (acc,) if epilogue is None else epilogue(acc, *[e[...] for e in rest[:ne]])
        for o_ref, o in zip(rest[ne:], outs):
            o_ref[...] = o.astype(o_ref.dtype)

    return pl.pallas_call(
        body, name=name, grid=(kf // tkb, t // tm),
        in_specs=[pl.BlockSpec((tm, n), lambda kb, i: (i, 0)),
                  pl.BlockSpec((tkb, n), lambda kb, i: (kb, 0))]
                 + [pl.BlockSpec((tm, tkb), lambda kb, i: (i, kb)) for _ in extra],
        out_specs=[pl.BlockSpec((tm, tkb), lambda kb, i: (i, kb)) for _ in out_dtypes],
        out_shape=[jax.ShapeDtypeStruct((t, kf), dt) for dt in out_dtypes],
        compiler_params=_params(2),
    )(dy, w2d, *extra)


def _mm_tn(name, a, b, me_arr, *, m, n, tma, tn, sharded, a_off=0, b_off=0, deps=()):
    t = a.shape[0]
    if sharded == "cols":
        cb = n // N_DEV
        nb, q = max(tn // cb, 1), max(cb // tn, 1)
        tw = tn // nb
        full_shape, own_shape = (N_DEV, m, cb), (m, cb)
        full_spec = pl.BlockSpec((nb, tma, tw), lambda i, j, me: (j // q, i, j % q))
    else:
        kb = m // N_DEV
        p = kb // tma
        nb, tw = 1, tn
        full_shape, own_shape = (m, n), (kb, n)
        full_spec = pl.BlockSpec((tma, tn), lambda i, j, me: (i, j))

    def body(me_ref, a_ref, b_ref, *rest):
        full_ref, own_ref, stage, sem, pending = rest[len(deps):]
        i, j = pl.program_id(0), pl.program_id(1)

        def own_copy(r0, c0):
            return pltpu.make_async_copy(
                stage, own_ref.at[pl.ds(pl.multiple_of(r0, tma), tma), pl.ds(pl.multiple_of(c0, tw), tw)], sem)

        def drain():
            @pl.when(pending[0] == 1)
            def _():
                own_copy(0, 0).wait()
                pending[0] = 0

        @pl.when(jnp.logical_and(i == 0, j == 0))
        def _():
            pending[0] = 0

        acc = lax.dot_general(a_ref[...], b_ref[...], (((0,), (0,)), ((), ())), preferred_element_type=F32)
        for blk in range(nb):
            part = acc[:, blk * tw:(blk + 1) * tw]
            if sharded == "cols":
                full_ref[blk] = part.astype(BF16)
                owner, r0, c0 = (j // q) * nb + blk, i * tma, (j % q) * tw
            else:
                full_ref[...] = part.astype(BF16)
                owner, r0, c0 = i // p, (i % p) * tma, j * tn

            @pl.when(owner == me_ref[0])
            def _():
                drain()
                stage[...] = part
                own_copy(r0, c0).start()
                pending[0] = 1

        @pl.when(jnp.logical_and(i == pl.num_programs(0) - 1, j == pl.num_programs(1) - 1))
        def _():
            drain()

    full, own = pl.pallas_call(
        body, name=name,
        grid_spec=pltpu.PrefetchScalarGridSpec(
            num_scalar_prefetch=1, grid=(m // tma, n // tn),
            in_specs=[pl.BlockSpec((t, tma), lambda i, j, me: (0, a_off + i)),
                      pl.BlockSpec((t, tn), lambda i, j, me: (0, b_off + j))] + [ANY] * len(deps),
            out_specs=[full_spec, ANY],
            scratch_shapes=[pltpu.VMEM((tma, tw), F32), pltpu.SemaphoreType.DMA(()), pltpu.SMEM((1,), jnp.int32)]),
        out_shape=[jax.ShapeDtypeStruct(full_shape, BF16), jax.ShapeDtypeStruct(own_shape, F32)],
        compiler_params=_params(2),
    )(me_arr, a, b, *deps)
    if sharded == "rows":
        full = full.reshape(N_DEV, m // N_DEV, n)
    return full, own


def _row_tile(t):
    return t // 8 if (t // 8) % 16 == 0 else ROW_TILE


RING_SLOTS = 3


def _row_call_ring(name, body, t, row_ins, full_ins, row_outs, acc_outs, scratch=(), deps=()):
    tm = _row_tile(t)
    nt = t // tm
    nrow, nfull, nd, nsc = len(row_ins), len(full_ins), len(deps), len(scratch)
    nout = len(row_outs) + len(acc_outs)

    def ringed(*refs):
        hbm = refs[:nrow]
        full = refs[nrow:nrow + nfull]
        outs = refs[nrow + nfull + nd:nrow + nfull + nd + nout + nsc]
        rings = refs[nrow + nfull + nd + nout + nsc:-1]
        sems = refs[-1]
        i = pl.program_id(0)

        def fetch(step, slot):
            rows = pl.ds(pl.multiple_of(step * tm, tm), tm)
            return [pltpu.make_async_copy(hbm[q].at[rows, :], rings[q].at[slot], sems.at[slot, q])
                    for q in range(nrow)]

        @pl.when(i == 0)
        def _():
            for ahead in range(min(RING_SLOTS - 1, nt)):
                for cp in fetch(ahead, ahead):
                    cp.start()

        @pl.when(i + RING_SLOTS - 1 < nt)
        def _():
            for cp in fetch(i + RING_SLOTS - 1, (i + RING_SLOTS - 1) % RING_SLOTS):
                cp.start()

        slot = i % RING_SLOTS
        for cp in fetch(i, slot):
            cp.wait()
        body(*[ring.at[slot] for ring in rings], *full, *outs)

    return pl.pallas_call(
        ringed, name=name, grid=(nt,),
        in_specs=[ANY] * nrow + [pl.BlockSpec(a.shape, lambda i: (0, 0)) for a in full_ins] + [ANY] * nd,
        out_specs=[pl.BlockSpec((tm, c), lambda i: (i, 0)) for c, _ in row_outs]
                  + [pl.BlockSpec((r, c), lambda i: (0, 0)) for r, c in acc_outs],
        out_shape=[jax.ShapeDtypeStruct((t, c), dt) for c, dt in row_outs]
                  + [jax.ShapeDtypeStruct((r, c), F32) for r, c in acc_outs],
        scratch_shapes=list(scratch) + [pltpu.VMEM((RING_SLOTS, tm, a.shape[1]), a.dtype) for a in row_ins]
                       + [pltpu.SemaphoreType.DMA((RING_SLOTS, nrow))],
        compiler_params=_params(1),
    )(*row_ins, *full_ins, *deps)


def _accumulate(ref, v):
    @pl.when(pl.program_id(0) == 0)
    def _():
        ref[...] = v

    @pl.when(pl.program_id(0) > 0)
    def _():
        ref[...] += v


def _rms(v):
    return lax.rsqrt(jnp.mean(v * v, axis=-1, keepdims=True) + RMS_EPS)


def _rms_bwd(dout, u, r, g):
    du = dout * g
    dx = r * (du - u * jnp.mean(du * u, axis=-1, keepdims=True))
    return dx, _colsum8(dout * u)


def _pre_norm(h0, g):
    t, d = h0.shape

    def body(h_ref, g_ref, n_ref):
        h = h_ref[...]
        n_ref[...] = (h * _rms(h) * g_ref[...]).astype(BF16)

    return _row_call_ring("pre_norm", body, t, [h0], [g], [(d, BF16)], [])[0]


def _mix_post(m_mix, wo_full, h0, g_post, g_pre, deps=()):
    t, d = h0.shape
    tm = _row_tile(t)

    def body(m_ref, wo_ref, h0_ref, gp_ref, gq_ref, *rest):
        mix_ref, h1_ref, n2_ref = rest[len(deps):]
        mix_v = jnp.dot(m_ref[...], wo_ref[...], preferred_element_type=F32)
        mix_ref[...] = mix_v
        h1 = h0_ref[...] + mix_v * _rms(mix_v) * gp_ref[...]
        h1_ref[...] = h1
        n2_ref[...] = (h1 * _rms(h1) * gq_ref[...]).astype(BF16)

    tile = pl.BlockSpec((tm, d), lambda i: (i, 0))
    gain = pl.BlockSpec((1, d), lambda i: (0, 0))
    return pl.pallas_call(
        body, name="mix_post", grid=(t // tm,),
        in_specs=[tile, pl.BlockSpec((d, d), lambda i: (0, 0)), tile, gain, gain] + [ANY] * len(deps),
        out_specs=[tile, tile, tile],
        out_shape=[jax.ShapeDtypeStruct((t, d), F32), jax.ShapeDtypeStruct((t, d), F32),
                   jax.ShapeDtypeStruct((t, d), BF16)],
        compiler_params=_params(1),
    )(m_mix, wo_full, h0, g_post, g_pre, *deps)


def _loss_head(fo, h1, tgt, g_post_mlp, t_real):
    t, d = h1.shape
    tile = _row_tile(t)

    def body(fo_ref, h1_ref, tgt_ref, g_ref, dfo_ref, dh2_ref, dg_ref, loss_ref, lacc):
        i = pl.program_id(0)
        fo_v = fo_ref[...]
        g = g_ref[...]
        r = _rms(fo_v)
        u = fo_v * r
        h2 = h1_ref[...] + u * g
        row = i * tile + lax.broadcasted_iota(jnp.int32, (tile, 1), 0)
        valid = jnp.logical_and(row >= N_META, row < t_real)
        diff = jnp.where(valid, h2 - tgt_ref[...], 0.0)
        dh2 = diff * (1.0 / d)
        dh2_ref[...] = dh2
        dfo, dg = _rms_bwd(dh2, u, r, g)
        dfo_ref[...] = dfo.astype(BF16)
        _accumulate(dg_ref, dg)
        _accumulate(lacc, _colsum8(diff * diff))

        @pl.when(i == pl.num_programs(0) - 1)
        def _():
            loss_ref[...] = jnp.full((SUB, LANE), (0.5 / d) * jnp.sum(lacc[...]), F32)

    return _row_call_ring("loss_head", body, t, [fo, h1, tgt], [g_post_mlp],
                     [(d, BF16), (d, F32)], [(SUB, d), (SUB, LANE)], scratch=[pltpu.VMEM((SUB, d), F32)])


def _mid_norm_bwd(dn2, h1, dh2, mix, g_pre_mlp, g_post_mix, deps=()):
    t, d = h1.shape

    def body(dn2_ref, h1_ref, dh2_ref, mix_ref, gq_ref, gp_ref, dh1_ref, dmix_ref, dgq_ref, dgp_ref):
        h1 = h1_ref[...]
        r3 = _rms(h1)
        dx, dgq = _rms_bwd(dn2_ref[...], h1 * r3, r3, gq_ref[...])
        dh1 = dh2_ref[...] + dx
        dh1_ref[...] = dh1
        mix_v = mix_ref[...]
        r2 = _rms(mix_v)
        dmix, dgp = _rms_bwd(dh1, mix_v * r2, r2, gp_ref[...])
        dmix_ref[...] = dmix.astype(BF16)
        _accumulate(dgq_ref, dgq)
        _accumulate(dgp_ref, dgp)

    return _row_call_ring("mid_norm_bwd", body, t, [dn2, h1, dh2, mix], [g_pre_mlp, g_post_mix],
                     [(d, F32), (d, BF16)], [(SUB, d), (SUB, d)], deps=deps)


def _pre_norm_bwd(dn, h0, dh1, g_pre_mix, t_real):
    t, d = h0.shape
    tm = _row_tile(t)
    nt = t // tm
    seq = t_real - N_META
    tail = t_real - (nt - 1) * tm
    assert tm > N_META and N_META % SUB == 0 and 0 < tail <= tm and tail % SUB == 0

    def body(dn_ref, h0_ref, dh1_ref, g_ref, gx_ref, dmeta_ref, dg_ref, stage, sem):
        i = pl.program_id(0)
        h0 = h0_ref[...]
        r = _rms(h0)
        dx, dg = _rms_bwd(dn_ref[...], h0 * r, r, g_ref[...])
        dh0 = dh1_ref[...] + dx
        _accumulate(dg_ref, dg)

        def copy(rows, src0, dst0):
            return pltpu.make_async_copy(stage.at[pl.ds(src0, rows), :], gx_ref.at[pl.ds(dst0, rows), :], sem)

        @pl.when(i == 1)
        def _():
            copy(tm - N_META, N_META, 0).wait()

        @pl.when(i > 1)
        def _():
            copy(tm, 0, 0).wait()

        stage[...] = dh0

        @pl.when(i == 0)
        def _():
            dmeta_ref[...] = dh0[:N_META]
            copy(tm - N_META, N_META, 0).start()

        @pl.when(jnp.logical_and(i > 0, i < nt - 1))
        def _():
            copy(tm, 0, pl.multiple_of(i * tm - N_META, SUB)).start()

        @pl.when(i == nt - 1)
        def _():
            last = copy(tail, 0, (nt - 1) * tm - N_META)
            last.start()
            last.wait()

    tile = pl.BlockSpec((tm, d), lambda i: (i, 0))
    return pl.pallas_call(
        body, name="pre_norm_bwd", grid=(nt,),
        in_specs=[tile, tile, tile, pl.BlockSpec((1, d), lambda i: (0, 0))],
        out_specs=[ANY, pl.BlockSpec((N_META, d), lambda i: (0, 0)), pl.BlockSpec((SUB, d), lambda i: (0, 0))],
        out_shape=[jax.ShapeDtypeStruct((seq, d), F32), jax.ShapeDtypeStruct((N_META, d), F32),
                   jax.ShapeDtypeStruct((SUB, d), F32)],
        scratch_shapes=[pltpu.VMEM((tm, d), F32), pltpu.SemaphoreType.DMA(())],
        compiler_params=_params(1),
    )(dn, h0, dh1, g_pre_mix)


def _layer_norm_silu(a1, ln_g, ln_b, deps=()):
    t, c = a1.shape

    def body(a1_ref, g_ref, b_ref, a3_ref):
        a = a1_ref[...]
        mu = jnp.mean(a, axis=-1, keepdims=True)
        xc = a - mu
        rstd = lax.rsqrt(jnp.mean(xc * xc, axis=-1, keepdims=True) + LN_EPS)
        z = xc * rstd * g_ref[...] + b_ref[...]
        a3_ref[...] = (z * _sigmoid(z)).astype(BF16)

    return _row_call_ring("layer_norm_silu", body, t, [a1], [ln_g, ln_b], [(c, BF16)], [], deps=deps)[0]


def _layer_norm_silu_bwd(da3, a1, ln_g, ln_b, deps=()):
    t, c = a1.shape

    def body(da3_ref, a1_ref, g_ref, b_ref, da1_ref, dg_ref, db_ref):
        a = a1_ref[...]
        g = g_ref[...]
        mu = jnp.mean(a, axis=-1, keepdims=True)
        xc = a - mu
        rstd = lax.rsqrt(jnp.mean(xc * xc, axis=-1, keepdims=True) + LN_EPS)
        xhat = xc * rstd
        z = xhat * g + b_ref[...]
        sg = _sigmoid(z)
        dz = da3_ref[...] * (sg * (1.0 + z * (1.0 - sg)))
        dxhat = dz * g
        da1_ref[...] = rstd * (dxhat - jnp.mean(dxhat, axis=-1, keepdims=True)
                               - xhat * jnp.mean(dxhat * xhat, axis=-1, keepdims=True))
        _accumulate(dg_ref, _colsum8(dz * xhat))
        _accumulate(db_ref, _colsum8(dz))

    return _row_call_ring("layer_norm_silu_bwd", body, t, [da3, a1], [ln_g, ln_b], [(c, F32)], [(SUB, c), (SUB, c)], deps=deps)


def _branch_merge(a3, s, wpw, wso, proj, b_gates, d, deps=()):
    t, cols = proj.shape
    nblk, k, cb = wpw.shape
    w = 1024
    nh = d // w
    per = w // cb
    ga0 = (cols - 2 * d) // w
    tm = _row_tile(t)

    def body(a3_ref, s_ref, wpw_ref, wso_ref, *rest):
        pa_refs, pb_refs, bg_ref = rest[:nh], rest[nh:2 * nh], rest[2 * nh]
        ya_ref, yb_ref, ga_ref, gb_ref, m_ref = rest[2 * nh + 1 + len(deps):]
        a3v, sv = a3_ref[...], s_ref[...]
        for b in range(nblk):
            here = slice(b * cb, (b + 1) * cb)
            local = slice((b % per) * cb, (b % per + 1) * cb)
            ya = jnp.dot(a3v, wpw_ref[b], preferred_element_type=F32)
            yb = jnp.dot(sv, wso_ref[b], preferred_element_type=F32)
            ga = _sigmoid(pa_refs[b // per][:, local] + bg_ref[:, here])
            gb = _sigmoid(pb_refs[b // per][:, local] + bg_ref[:, d + b * cb:d + (b + 1) * cb])
            ya_ref[:, here] = ya.astype(BF16)
            yb_ref[:, here] = yb.astype(BF16)
            ga_ref[:, here] = ga.astype(BF16)
            gb_ref[:, here] = gb.astype(BF16)
            m_ref[:, here] = (ga * ya + gb * yb).astype(BF16)

    tile = pl.BlockSpec((tm, d), lambda i: (i, 0))
    return pl.pallas_call(
        body, name="branch_merge", grid=(t // tm,),
        in_specs=[pl.BlockSpec((tm, k), lambda i: (i, 0)), pl.BlockSpec((tm, k), lambda i: (i, 0)),
                  pl.BlockSpec((nblk, k, cb), lambda i: (0, 0, 0)), pl.BlockSpec((nblk, k, cb), lambda i: (0, 0, 0))]
                 + [pl.BlockSpec((tm, w), lambda i, h=h: (i, ga0 + h)) for h in range(2 * nh)]
                 + [pl.BlockSpec((1, 2 * d), lambda i: (0, 0))] + [ANY] * len(deps),
        out_specs=[tile] * 5,
        out_shape=[jax.ShapeDtypeStruct((t, d), BF16)] * 5,
        compiler_params=_params(1),
    )(a3, s, wpw, wso, *([proj] * (2 * nh)), b_gates, *deps)


def _gate_backward(dmix, wo_full, ga, gb, ya, yb, cols, tm, deps=()):
    t, d = ya.shape
    w = 1024
    nh = d // w
    ga0 = (cols - 2 * d) // w

    def body(dmix_ref, wo_ref, ga_ref, gb_ref, ya_ref, yb_ref, *rest):
        dya_ref, dyb_ref, dp_ref, dba_ref, dbb_ref, stage, sems = rest[len(deps):]
        h, i = pl.program_id(0), pl.program_id(1)
        dm = lax.dot_general(dmix_ref[...], wo_ref[...], (((1,), (1,)), ((), ())), preferred_element_type=F32)
        ga = ga_ref[...].astype(F32)
        gb = gb_ref[...].astype(F32)
        dya_ref[...] = (dm * ga).astype(BF16)
        dyb_ref[...] = (dm * gb).astype(BF16)
        dpa = dm * ya_ref[...].astype(F32) * ga * (1.0 - ga)
        dpb = dm * yb_ref[...].astype(F32) * gb * (1.0 - gb)

        def copies(row0, colblk):
            return [pltpu.make_async_copy(
                stage.at[g], dp_ref.at[pl.ds(pl.multiple_of(row0, tm), tm),
                                       pl.ds(pl.multiple_of((ga0 + g * nh + colblk) * w, w), w)], sems.at[g])
                for g in range(2)]

        @pl.when(jnp.logical_or(h > 0, i > 0))
        def _():
            for cp in copies(0, 0):
                cp.wait()

        stage[0] = dpa.astype(BF16)
        stage[1] = dpb.astype(BF16)
        for cp in copies(i * tm, h):
            cp.start()

        @pl.when(i == 0)
        def _():
            dba_ref[...] = _colsum8(dpa)
            dbb_ref[...] = _colsum8(dpb)

        @pl.when(i > 0)
        def _():
            dba_ref[...] += _colsum8(dpa)
            dbb_ref[...] += _colsum8(dpb)

        @pl.when(jnp.logical_and(h == pl.num_programs(0) - 1, i == pl.num_programs(1) - 1))
        def _():
            for cp in copies(0, 0):
                cp.wait()

    tile = pl.BlockSpec((tm, w), lambda h, i: (i, h))
    return pl.pallas_call(
        body, name="gate_backward", grid=(nh, t // tm),
        in_specs=[pl.BlockSpec((tm, d), lambda h, i: (i, 0)),
                  pl.BlockSpec((w, d), lambda h, i: (h, 0)),
                  tile, tile, tile, tile] + [ANY] * len(deps),
        out_specs=[tile, tile, ANY,
                   pl.BlockSpec((SUB, w), lambda h, i: (0, h)),
                   pl.BlockSpec((SUB, w), lambda h, i: (0, h))],
        out_shape=[jax.ShapeDtypeStruct((t, d), BF16), jax.ShapeDtypeStruct((t, d), BF16),
                   jax.ShapeDtypeStruct((t, cols), BF16),
                   jax.ShapeDtypeStruct((SUB, d), F32), jax.ShapeDtypeStruct((SUB, d), F32)],
        scratch_shapes=[pltpu.VMEM((2, tm, w), BF16), pltpu.SemaphoreType.DMA((2,))],
        compiler_params=_params(2),
    )(dmix, wo_full, ga, gb, ya, yb, *deps)


def _shifted_views(win, offsets):
    n = win.shape[0]
    rotated = {}
    views = {}
    for o in offsets:
        q, r = divmod(o, SUB)
        if r not in rotated:
            rotated[r] = win if r == 0 else pltpu.roll(win, n - r, 0)
        views[o] = rotated[r][q * SUB:q * SUB + CONV_CHUNK]
    return views


def _causal_views(xp_ref, ntap, r0):
    win = xp_ref[pl.ds(r0, CONV_CHUNK + CONV_PAD), :]
    views = _shifted_views(win, [CONV_PAD - (ntap - 1 - k) for k in range(ntap)])
    return [views[CONV_PAD - (ntap - 1 - k)] for k in range(ntap)]


def _causal_conv(xp_ref, w_ref, ntap, r0):
    acc = None
    for k, shifted in enumerate(_causal_views(xp_ref, ntap, r0)):
        term = w_ref[k:k + 1, :] * shifted
        acc = term if acc is None else acc + term
    return acc


def _anticausal_conv(xp_ref, w_ref, ntap, r0):
    win = xp_ref[pl.ds(pl.multiple_of(CONV_PAD + r0, CONV_PAD), CONV_CHUNK + CONV_PAD), :]
    views = _shifted_views(win, [ntap - 1 - k for k in range(ntap)])
    acc = None
    for k in range(ntap):
        term = w_ref[k:k + 1, :] * views[ntap - 1 - k]
        acc = term if acc is None else acc + term
    return acc


def _conv_weight_grad(dw_ref, d_chunk, xp_ref, ntap, r0):
    for k, shifted in enumerate(_causal_views(xp_ref, ntap, r0)):
        dw_ref[k * SUB:(k + 1) * SUB, :] += _colsum8(d_chunk * shifted)


def _zero_pads(ref, t):
    ref[0:CONV_PAD, :] = jnp.zeros((CONV_PAD, LANE), F32)
    ref[CONV_PAD + t:CONV_PAD + t + CONV_PAD, :] = jnp.zeros((CONV_PAD, LANE), F32)


def _for_chunks(t, fn):
    def step(idx, carry):
        fn(pl.multiple_of(idx * CONV_CHUNK, CONV_CHUNK))
        return carry

    lax.fori_loop(0, t // CONV_CHUNK, step, 0)


def _conv_forward(proj, conf_w, conf_b, short_w, dc, deps=()):
    t = proj.shape[0]
    nc = dc // LANE

    def body(av_ref, ag_ref, bg_ref, cg_ref, v_ref, cw_ref, cb_ref, sw_ref, *rest):
        a1_ref, s_ref, xa, xb = rest[len(deps):]
        _zero_pads(xa, t)
        _zero_pads(xb, t)
        xa[CONV_PAD:CONV_PAD + t, :] = av_ref[...] * _sigmoid(ag_ref[...])
        xb[CONV_PAD:CONV_PAD + t, :] = cg_ref[...] * v_ref[...]

        def chunk(r0):
            rs = pl.ds(r0, CONV_CHUNK)
            a1_ref[rs, :] = _causal_conv(xa, cw_ref, CONF_K, r0) + cb_ref[...]
            s_ref[rs, :] = (bg_ref[rs, :] * _causal_conv(xb, sw_ref, SHORT_K, r0)).astype(BF16)

        _for_chunks(t, chunk)

    col = lambda g: pl.BlockSpec((t, LANE), lambda c, g=g: (0, g * nc + c))
    return pl.pallas_call(
        body, name="conv_forward", grid=(nc,),
        in_specs=[col(0), col(1), col(2), col(3), col(4),
                  pl.BlockSpec((CONF_K, LANE), lambda c: (0, c)),
                  pl.BlockSpec((1, LANE), lambda c: (0, c)),
                  pl.BlockSpec((SHORT_K, LANE), lambda c: (0, c))] + [ANY] * len(deps),
        out_specs=[pl.BlockSpec((t, LANE), lambda c: (0, c)), pl.BlockSpec((t, LANE), lambda c: (0, c))],
        out_shape=[jax.ShapeDtypeStruct((t, dc), F32), jax.ShapeDtypeStruct((t, dc), BF16)],
        scratch_shapes=[pltpu.VMEM((t + 2 * CONV_PAD, LANE), F32), pltpu.VMEM((t + 2 * CONV_PAD, LANE), F32)],
        compiler_params=_params(1),
    )(proj, proj, proj, proj, proj, conf_w, conf_b, short_w, *deps)


def _conv_backward(dproj, proj, da1, ds, conf_w, short_w, dc):
    t = proj.shape[0]
    nc = dc // LANE

    def body(dp_in, av_ref, ag_ref, bg_ref, cg_ref, v_ref, da1_ref, ds_ref, cw_ref, sw_ref,
             dp_ref, dcw_ref, dcb_ref, dsw_ref, xa, xb, da, db, stage, sems):
        del dp_in
        c = pl.program_id(0)
        for ref in (xa, xb, da, db):
            _zero_pads(ref, t)
        xa[CONV_PAD:CONV_PAD + t, :] = av_ref[...] * _sigmoid(ag_ref[...])
        xb[CONV_PAD:CONV_PAD + t, :] = cg_ref[...] * v_ref[...]
        da[CONV_PAD:CONV_PAD + t, :] = da1_ref[...]
        dcw_ref[...] = jnp.zeros(dcw_ref.shape, F32)
        dsw_ref[...] = jnp.zeros(dsw_ref.shape, F32)
        dcb_ref[...] = jnp.zeros(dcb_ref.shape, F32)

        def copies(colblk):
            return [pltpu.make_async_copy(
                stage.at[g], dp_ref.at[:, pl.ds(pl.multiple_of((g * nc + colblk) * LANE, LANE), LANE)], sems.at[g])
                for g in range(5)]

        @pl.when(c > 0)
        def _():
            for cp in copies(0):
                cp.wait()

        def through_gate(r0):
            rs = pl.ds(r0, CONV_CHUNK)
            ds_c = ds_ref[rs, :]
            stage[2, rs, :] = (ds_c * _causal_conv(xb, sw_ref, SHORT_K, r0)).astype(BF16)
            db[pl.ds(pl.multiple_of(CONV_PAD + r0, CONV_PAD), CONV_CHUNK), :] = ds_c * bg_ref[rs, :]

        _for_chunks(t, through_gate)

        def through_convs(r0):
            rs = pl.ds(r0, CONV_CHUNK)
            da0 = _anticausal_conv(da, cw_ref, CONF_K, r0)
            sg = _sigmoid(ag_ref[rs, :])
            stage[0, rs, :] = (da0 * sg).astype(BF16)
            stage[1, rs, :] = (da0 * av_ref[rs, :] * sg * (1.0 - sg)).astype(BF16)
            dcv = _anticausal_conv(db, sw_ref, SHORT_K, r0)
            stage[3, rs, :] = (dcv * v_ref[rs, :]).astype(BF16)
            stage[4, rs, :] = (dcv * cg_ref[rs, :]).astype(BF16)
            da1_c = da1_ref[rs, :]
            _conv_weight_grad(dcw_ref, da1_c, xa, CONF_K, r0)
            _conv_weight_grad(dsw_ref, ds_ref[rs, :] * bg_ref[rs, :], xb, SHORT_K, r0)
            dcb_ref[...] += _colsum8(da1_c)

        _for_chunks(t, through_convs)
        for cp in copies(c):
            cp.start()

        @pl.when(c == pl.num_programs(0) - 1)
        def _():
            for cp in copies(0):
                cp.wait()

    col = lambda g: pl.BlockSpec((t, LANE), lambda c, g=g: (0, g * nc + c))
    blk = pl.BlockSpec((t, LANE), lambda c: (0, c))
    return pl.pallas_call(
        body, name="conv_backward", grid=(nc,),
        in_specs=[ANY, col(0), col(1), col(2), col(3), col(4), blk, blk,
                  pl.BlockSpec((CONF_K, LANE), lambda c: (0, c)),
                  pl.BlockSpec((SHORT_K, LANE), lambda c: (0, c))],
        out_specs=[ANY,
                   pl.BlockSpec((CONF_K * SUB, LANE), lambda c: (0, c)),
                   pl.BlockSpec((SUB, LANE), lambda c: (0, c)),
                   pl.BlockSpec((SHORT_K * SUB, LANE), lambda c: (0, c))],
        out_shape=[jax.ShapeDtypeStruct(dproj.shape, dproj.dtype),
                   jax.ShapeDtypeStruct((CONF_K * SUB, dc), F32),
                   jax.ShapeDtypeStruct((SUB, dc), F32),
                   jax.ShapeDtypeStruct((SHORT_K * SUB, dc), F32)],
        scratch_shapes=[pltpu.VMEM((t + 2 * CONV_PAD, LANE), F32)] * 4
                       + [pltpu.VMEM((5, t, LANE), BF16), pltpu.SemaphoreType.DMA((5,))],
        input_output_aliases={0: 0},
        compiler_params=_params(1),
    )(dproj, proj, proj, proj, proj, proj, da1, ds, conf_w, short_w)


def _adamw_math(w, g, m, v):
    m = ADAM_B1 * m + (1.0 - ADAM_B1) * g
    v = ADAM_B2 * v + (1.0 - ADAM_B2) * (g * g)
    m_hat = m / (1.0 - ADAM_B1 ** ADAM_STEP)
    v_hat = v / (1.0 - ADAM_B2 ** ADAM_STEP)
    delta = -ADAM_LR * (m_hat / (jnp.sqrt(v_hat) + ADAM_EPS) + ADAM_WD * w)
    return delta, m, v


def _cast_into_slot(name, w, me_arr, deps=()):
    r, c = w.shape
    tr = 256

    def body(me_ref, w_ref, *rest):
        del me_ref
        rest[-1][0] = w_ref[...].astype(BF16)

    return pl.pallas_call(
        body, name=name,
        grid_spec=pltpu.PrefetchScalarGridSpec(
            num_scalar_prefetch=1, grid=(r // tr,),
            in_specs=[pl.BlockSpec((tr, c), lambda i, me: (i, 0))] + [ANY] * len(deps),
            out_specs=pl.BlockSpec((1, tr, c), lambda i, me: (me[0], i, 0))),
        out_shape=jax.ShapeDtypeStruct((N_DEV, r, c), BF16),
        compiler_params=_params(1),
    )(me_arr, w, *deps)


def _chip_sum(name, full, from_sibling, me_arr):
    _, r, c = full.shape
    tr = min(r, 1024)

    def body(me_ref, full_ref, sib_ref, sums_ref):
        del me_ref
        sums_ref[0] = (full_ref[0].astype(F32) + sib_ref[0].astype(F32)).astype(BF16)

    other = lambda k, me: (me[0] // 2 + 1 + k) % 4
    return pl.pallas_call(
        body, name=name,
        grid_spec=pltpu.PrefetchScalarGridSpec(
            num_scalar_prefetch=1, grid=(r // tr, 3),
            in_specs=[pl.BlockSpec((1, tr, c), lambda i, k, me: (2 * other(k, me) + me[0] % 2, i, 0)),
                      pl.BlockSpec((1, tr, c), lambda i, k, me: (other(k, me), i, 0))],
            out_specs=pl.BlockSpec((1, tr, c), lambda i, k, me: (other(k, me), i, 0))),
        out_shape=jax.ShapeDtypeStruct((4, r, c), BF16),
        compiler_params=_params(2),
    )(me_arr, full, from_sibling)


def _adamw_shard(name, w, m, v, parts, me_arr, deps=()):
    r, c = w.shape
    tr = min(256, r // len(parts))
    np_ = len(parts)
    per = r // np_ // tr

    def body(me_ref, w_ref, m_ref, v_ref, *rest):
        g_out, d_out, m_out, v_out = rest[5 * np_ + len(deps):]
        g = None
        for p in range(np_):
            gp = rest[5 * p][...]
            for l_ref in rest[5 * p + 1:5 * p + 5]:
                gp = gp + l_ref[0].astype(F32)
            g = gp if g is None else jnp.where(pl.program_id(0) // per == p, gp, g)
        delta, m_new, v_new = _adamw_math(w_ref[...], g, m_ref[...], v_ref[...])
        g_out[...] = g
        d_out[...] = delta
        m_out[...] = m_new
        v_out[...] = v_new

    tile = pl.BlockSpec((tr, c), lambda i, me: (i, 0))
    part_specs, part_args = [], []
    for p, (g_own, from_sibling, landed) in enumerate(parts):
        row = lambda i, p=p: jnp.clip(i - p * per, 0, per - 1)
        part_specs.append(pl.BlockSpec((tr, c), lambda i, me, row=row: (row(i), 0)))
        part_specs += [pl.BlockSpec((1, tr, c), lambda i, me, k=k, row=row: ((me[0] // 2 + k) % 4, row(i), 0))
                       for k in range(4)]
        part_args += [g_own, from_sibling, landed, landed, landed]
    return pl.pallas_call(
        body, name=name,
        grid_spec=pltpu.PrefetchScalarGridSpec(
            num_scalar_prefetch=1, grid=(r // tr,),
            in_specs=[tile] * 3 + part_specs + [ANY] * len(deps), out_specs=[tile] * 4),
        out_shape=[jax.ShapeDtypeStruct((r, c), F32)] * 4,
        compiler_params=_params(1),
    )(me_arr, w, m, v, *part_args, *deps)


SMALL_W = 1024
VEC_ROWS = 16
LOSS_ROW = 15
META_ROW0 = 16
CONF_ROW0 = 64
SHORT_ROW0 = 96
SMALL_ROWS = 104


def _pack_small(vec_parts, dmeta, dcw, dsw, loss_blk, me_arr):
    widths = [p.shape[1] for p in vec_parts]
    nv = len(vec_parts)

    def body(me_ref, *refs):
        del me_ref
        parts, (dmeta_ref, dcw_ref, dsw_ref, loss_ref, out_ref) = refs[:nv], refs[nv:]
        out_ref[0] = jnp.zeros((SMALL_ROWS, SMALL_W), F32)
        out_ref[0, LOSS_ROW:LOSS_ROW + 1, 0:LANE] = loss_ref[0:1, :]
        row = 0
        for p_ref, wd in zip(parts, widths):
            s = jnp.sum(p_ref[...], axis=0, keepdims=True)
            for h in range(wd // SMALL_W):
                out_ref[0, row:row + 1, :] = s[:, h * SMALL_W:(h + 1) * SMALL_W]
                row += 1
        for h in range(dmeta_ref.shape[1] // SMALL_W):
            out_ref[0, META_ROW0 + h * N_META:META_ROW0 + (h + 1) * N_META, :] = dmeta_ref[:, h * SMALL_W:(h + 1) * SMALL_W]
        for k in range(CONF_K):
            out_ref[0, CONF_ROW0 + k:CONF_ROW0 + k + 1, :] = jnp.sum(dcw_ref[k * SUB:(k + 1) * SUB, :], axis=0, keepdims=True)
        for k in range(SHORT_K):
            out_ref[0, SHORT_ROW0 + k:SHORT_ROW0 + k + 1, :] = jnp.sum(dsw_ref[k * SUB:(k + 1) * SUB, :], axis=0, keepdims=True)

    ins = [*vec_parts, dmeta, dcw, dsw, loss_blk]
    return pl.pallas_call(
        body, name="pack_small",
        grid_spec=pltpu.PrefetchScalarGridSpec(
            num_scalar_prefetch=1, grid=(1,),
            in_specs=[pl.BlockSpec(a.shape, lambda i, me: (0, 0)) for a in ins],
            out_specs=pl.BlockSpec((1, SMALL_ROWS, SMALL_W), lambda i, me: (me[0], 0, 0))),
        out_shape=jax.ShapeDtypeStruct((N_DEV, SMALL_ROWS, SMALL_W), F32),
        compiler_params=_params(1),
    )(me_arr, *ins)


def _small_update(gathered, me_arr, vec_params, meta_p, conf_p, short_p):
    widths = [p[0].shape[1] for p in vec_params]
    nv = len(vec_params)
    mcols = meta_p[0].shape[1]
    per_row = SMALL_W // mcols

    def body(me_ref, gv_ref, gm_ref, gc_ref, gs_ref, *rest):
        del me_ref
        ins, outs = rest[:3 * (nv + 3)], rest[3 * (nv + 3):]

        def total(ref, r0, rows):
            s = ref[0, r0:r0 + rows, :]
            for dev in range(1, N_DEV):
                s = s + ref[dev, r0:r0 + rows, :]
            return s

        grads = []
        row = 0
        for wd in widths:
            pieces = [total(gv_ref, row + h, 1) for h in range(wd // SMALL_W)]
            grads.append(pieces[0] if len(pieces) == 1 else jnp.concatenate(pieces, axis=1))
            row += len(pieces)
        grads.append(total(gm_ref, 0, N_META))
        grads.append(total(gc_ref, 0, CONF_K))
        grads.append(total(gs_ref, 0, SHORT_K))
        loss = gv_ref[0, LOSS_ROW:LOSS_ROW + 1, 0:LANE]
        for dev in range(1, N_DEV):
            loss = loss + gv_ref[dev, LOSS_ROW:LOSS_ROW + 1, 0:LANE]
        outs[-1][...] = loss
        for idx, g in enumerate(grads):
            w_ref, m_ref, v_ref = ins[3 * idx:3 * idx + 3]
            delta, m_new, v_new = _adamw_math(w_ref[...], g, m_ref[...], v_ref[...])
            g_out, d_out, m_out, v_out = outs[4 * idx:4 * idx + 4]
            g_out[...] = g
            d_out[...] = delta
            m_out[...] = m_new
            v_out[...] = v_new

    params = list(vec_params) + [meta_p, conf_p, short_p]
    flat = [a for p in params for a in p]
    whole = lambda a: pl.BlockSpec(a.shape, lambda i, me: (0,) * a.ndim)
    outs = pl.pallas_call(
        body, name="small_update",
        grid_spec=pltpu.PrefetchScalarGridSpec(
            num_scalar_prefetch=1, grid=(1,),
            in_specs=[pl.BlockSpec((N_DEV, VEC_ROWS, SMALL_W), lambda i, me: (0, 0, 0)),
                      pl.BlockSpec((N_DEV, N_META, mcols),
                                   lambda i, me: (0, META_ROW0 // N_META + me[0] // per_row, me[0] % per_row)),
                      pl.BlockSpec((N_DEV, 32, LANE), lambda i, me: (0, CONF_ROW0 // 32, me[0])),
                      pl.BlockSpec((N_DEV, SUB, LANE), lambda i, me: (0, SHORT_ROW0 // SUB, me[0]))]
                     + [whole(a) for a in flat],
            out_specs=[whole(p[0]) for p in params for _ in range(4)]
                      + [pl.BlockSpec((1, LANE), lambda i, me: (0, 0))]),
        out_shape=[jax.ShapeDtypeStruct(p[0].shape, F32) for p in params for _ in range(4)]
                  + [jax.ShapeDtypeStruct((1, LANE), F32)],
        compiler_params=_params(1),
    )(me_arr, gathered, gathered, gathered, gathered, *flat)
    return [tuple(outs[4 * i:4 * i + 4]) for i in range(len(params))], outs[-1][0, 0]


def kernel(x, meta, g_pre_mix, w_in, b_gates, conf_dw_w, conf_dw_b, conf_ln_g, conf_ln_b, conf_w_pw, short_dw_w, short_w_out, w_o, g_post_mix, g_pre_mlp, w_up, w_down, g_post_mlp, loss_target, m_meta, m_g_pre_mix, m_w_in, m_b_gates, m_conf_dw_w, m_conf_dw_b, m_conf_ln_g, m_conf_ln_b, m_conf_w_pw, m_short_dw_w, m_short_w_out, m_w_o, m_g_post_mix, m_g_pre_mlp, m_w_up, m_w_down, m_g_post_mlp, v_meta, v_g_pre_mix, v_w_in, v_b_gates, v_conf_dw_w, v_conf_dw_b, v_conf_ln_g, v_conf_ln_b, v_conf_w_pw, v_short_dw_w, v_short_w_out, v_w_o, v_g_post_mix, v_g_pre_mlp, v_w_up, v_w_down, v_g_post_mlp):
    seq, d = x.shape[1], x.shape[2]
    dc = conf_w_pw.shape[1]
    t_real = N_META + seq
    t = -(-t_real // ROW_TILE) * ROW_TILE
    tm = t // 2
    assert tm % 16 == 0 and d % 1024 == 0 and dc % 1024 == 0
    x_idx, y_idx, c_idx = _position()
    me_arr = jnp.reshape(4 * x_idx + 2 * y_idx + c_idx, (1,)).astype(jnp.int32)

    big = [w_in[0], conf_w_pw[0], short_w_out[0], w_o[0], w_up[0], w_down[0]]
    big_names = ["w_in", "conf_w_pw", "short_w_out", "w_o", "w_up", "w_down"]
    groups = [[0], [1, 2, 3], [4], [5]]
    slots, deps = [], []
    for g, idxs in enumerate(groups):
        slots.append([_cast_into_slot("cast_" + big_names[i], big[i], me_arr, deps=deps) for i in idxs])
        if g == 0:
            direct0 = _remote_start("gather0_direct_start", "gather_direct", slots[0])
            deps = [direct0[3]]
    casts = [sl for group in slots[1:] for sl in group]
    meta_g, cw_g, sw_g = _all_gather("gather_small_params", [meta, conf_dw_w[0], short_dw_w[0]], deps=casts)

    def start_direct(g, deps):
        send, recv, bufs, tok = _remote_start("gather%d_direct_start" % g, "gather_direct", slots[g], deps=deps)
        return (send, recv, bufs), tok

    def relay(g, state, after):
        send, recv, bufs, tok = _remote_pass_on("gather%d_relay" % g, "gather_direct", *state, after, "gather_relay")
        return (send, recv, bufs), tok

    def gathered(g, state, after):
        send, recv, bufs, tok = _remote_pass_on("gather%d_diag" % g, "gather_relay", *state, after, "gather_diag")
        return _remote_wait("gather%d_diag_wait" % g, "gather_diag", send, recv, bufs, len(bufs), [tok])

    unshard =lambda g: jnp.transpose(g, (1, 0, 2)).reshape(g.shape[1], -1)
    meta_full, cw_full, sw_full = unshard(meta_g), unshard(cw_g), unshard(sw_g)

    relay0, tok = relay(0, direct0[:3], [meta_g])
    zrows = jnp.zeros((t - t_real, d), F32) + tok[0, 0] * 0.0
    h0 = jnp.concatenate([meta_full, x[0], zrows], axis=0)
    tgt = jnp.concatenate([jnp.zeros((N_META, d), F32), loss_target[0], zrows], axis=0)
    n = _pre_norm(h0, g_pre_mix)
    direct1, tok = start_direct(1, [tok])
    direct2, tok = start_direct(2, [tok])
    win_g, = gathered(0, relay0, [tok, n])
    proj = _mm_cols_pairs("proj", n, win_g, tm=tm // 2)
    relay1, tok = relay(1, direct1, [proj])
    a1, s = _conv_forward(proj, cw_full, conf_dw_b, sw_full, dc, deps=[tok])
    relay2, tok = relay(2, direct2, [a1])
    direct3, tok = start_direct(3, [tok])
    a3 = _layer_norm_silu(a1, conf_ln_g, conf_ln_b, deps=[tok])
    wpw_g, wso_g, wo_g = gathered(1, relay1, [a3])
    wo_full = wo_g.reshape(d, d)
    ya, yb, gate_a, gate_b, m_mix = _branch_merge(a3, s, wpw_g, wso_g, proj, b_gates, d)
    mix, h1, n2 = _mix_post(m_mix, wo_full, h0, g_post_mix, g_pre_mlp)
    wup_g, = gathered(2, relay2, [n2])

    def up_epilogue(acc):
        r = jnp.maximum(acc, 0.0)
        return r * r, r

    half_up = dict(tm=tm, epilogue=up_epilogue, out_dtypes=(BF16, BF16))
    f, relu_up = _mm_cols("mlp_up0", n2, wup_g, blocks=(0, N_DEV // 2), **half_up)
    relay3, tok = relay(3, direct3, [f])
    f, relu_up = _mm_cols("mlp_up1", n2, wup_g, blocks=(N_DEV // 2, N_DEV), into=(f, relu_up), deps=[tok], **half_up)
    wdn_g, = gathered(3, relay3, [f])
    wdn_full = wdn_g.reshape(-1, d)
    fo = _mm_rows("mlp_down", f, wdn_full, tm=tm, tn=512)
    dfo, dh2, dg_post_mlp, loss_blk = _loss_head(fo, h1, tgt, g_post_mlp, t_real)

    def reduce_start(tag, fulls, deps):
        lands = [lax.empty((4,) + g.shape[1:], BF16) for g in fulls]
        send, recv, bufs, tok = _remote_start("reduce_%s_d2d_start" % tag, "reduce_d2d", fulls, lands, deps=deps)
        return (send, recv, bufs), tok

    def reduce_middle(tag, state, owns, after):
        send, recv, bufs = state
        k = len(owns)
        bufs = _remote_wait("reduce_%s_d2d_wait" % tag, "reduce_d2d", send, recv, bufs, k, after)
        from_sibling = bufs[k:]
        sums = [_chip_sum("chip_sum_%s%d" % (tag, i), bufs[i], from_sibling[i], me_arr) for i in range(k)]
        lands = [lax.empty(sm.shape, BF16) for sm in sums]
        send, recv, bufs, tok = _remote_start("reduce_%s_ici_start" % tag, "reduce_ici", sums, lands)
        return (send, recv, bufs, list(zip(owns, from_sibling))), tok

    def reduce_finish(tag, state, after):
        send, recv, bufs, local = state
        k = len(local)
        bufs = _remote_wait("reduce_%s_ici_wait" % tag, "reduce_ici", send, recv, bufs, k, after)
        return [(own, sib, landed) for (own, sib), landed in zip(local, bufs[k:])]

    dup = _mm_nt_blocks("d_up", dfo, wdn_full, tm=tm, tkb=1024, extra=(relu_up,),
                        epilogue=lambda acc, r: (acc * (2.0 * r.astype(F32)),), out_dtypes=(BF16,))[0]
    gw_down, gw_down_own = _mm_tn("dw_down", f, dfo, me_arr, m=f.shape[1], n=d, tma=512, tn=d, sharded="rows")
    red_down, tok = reduce_start("down", [gw_down], ())
    dn2 = _mm_nt_acc_parts("d_n2", dup, wup_g, tm=tm, tn=512, deps=[tok])
    gw_up, gw_up_own = _mm_tn("dw_up", n2, dup, me_arr, m=d, n=dup.shape[1], tma=512, tn=2048, sharded="cols")
    red_down, tok = reduce_middle("down", red_down, [gw_down_own], [dn2])
    red_up, tok = reduce_start("up", [gw_up], [tok])
    dh1, dmix, dg_pre_mlp, dg_post_mix = _mid_norm_bwd(dn2, h1, dh2, mix, g_pre_mlp, g_post_mix, deps=[tok])
    dya, dyb, dproj, db_a, db_b = _gate_backward(dmix, wo_full, gate_a, gate_b, ya, yb, proj.shape[1], tm // 2)
    db_gates = jnp.concatenate([db_a, db_b], axis=1)
    red_up, tok = reduce_middle("up", red_up, [gw_up_own], [dya])
    gw_o, gw_o_own = _mm_tn("dw_o", m_mix, dmix, me_arr, m=d, n=d, tma=d // N_DEV, tn=d, sharded="rows", deps=[tok])
    da3 = _mm_nt_acc("d_a3", dya, wpw_g, tm=tm, tn=512)
    gw_pw, gw_pw_own = _mm_tn("dw_pw", a3, dya, me_arr, m=dc, n=d, tma=512, tn=d, sharded="cols")
    dsb = _mm_nt_acc("d_s", dyb, wso_g, tm=tm, tn=512)
    gw_so, gw_so_own = _mm_tn("dw_so", s, dyb, me_arr, m=dc, n=d, tma=512, tn=d, sharded="cols")
    red_mix, tok = reduce_start("mix", [gw_pw, gw_so, gw_o], ())
    da1, dln_g, dln_b = _layer_norm_silu_bwd(da3, a1, conf_ln_g, conf_ln_b, deps=[tok])
    dproj, dcw, dcb, dsw = _conv_backward(dproj, proj, da1, dsb, cw_full, sw_full, dc)
    red_mix, tok = reduce_middle("mix", red_mix, [gw_pw_own, gw_so_own, gw_o_own], [dcb])
    in_cb = w_in.shape[2]
    half = d // 2
    red_in = []
    for part in range(2):
        gw, own = _mm_tn("dw_in%d" % part, n, dproj, me_arr, m=half, n=proj.shape[1], tma=512, tn=2 * in_cb,
                         sharded="cols", a_off=part * (half // 512), deps=[tok])
        state, tok = reduce_start("in%d" % part, [gw], ())
        red_in.append((state, own))
    for part in range(2):
        state, own = red_in[part]
        red_in[part], tok = reduce_middle("in%d" % part, state, [own], [tok])
    dn = _mm_nt_acc("d_n", dproj, win_g, tm=tm // 2, tn=512, deps=[tok])
    grad_x, dmeta, dg_pre_mix = _pre_norm_bwd(dn, h0, dh1, g_pre_mix, t_real)
    grad_x = grad_x[None]

    vec_parts = [dg_pre_mix, db_gates, dcb, dln_g, dln_b, dg_post_mix, dg_pre_mlp, dg_post_mlp]
    packed = _pack_small(vec_parts, dmeta, dcw, dsw, loss_blk, me_arr)
    send, recv, bufs, tok = _remote_start("small_grads_ici_start", "gather_ici", [packed])
    vec_names = ["g_pre_mix", "b_gates", "conf_dw_b", "conf_ln_g", "conf_ln_b", "g_post_mix", "g_pre_mlp", "g_post_mlp"]
    env = locals()
    results = {}

    def update(nm, parts, deps=()):
        res = _adamw_shard("adamw_" + nm, env[nm][0], env["m_" + nm][0], env["v_" + nm][0], parts, me_arr, deps=deps)
        results[nm] = tuple(r[None] for r in res)
        return res[0]

    done = [update("w_down", reduce_finish("down", red_down, [tok]), deps=[tok])]
    done.append(update("w_up", reduce_finish("up", red_up, done)))
    bufs = _remote_wait("small_grads_ici_wait", "gather_ici", send, recv, bufs, 1, done)
    send, recv, bufs, tok = _remote_start("small_grads_d2d_start", "gather_d2d", bufs)
    for nm, pair in zip(["conf_w_pw", "short_w_out", "w_o"], reduce_finish("mix", red_mix, [tok])):
        done.append(update(nm, [pair], deps=[tok]))
    small_g, = _remote_wait("small_grads_d2d_wait", "gather_d2d", send, recv, bufs, 1, done)
    triple = lambda nm, sq: tuple(env[p + nm][0] if sq else env[p + nm] for p in ("", "m_", "v_"))
    small, loss = _small_update(small_g, me_arr, [triple(nm, False) for nm in vec_names],
                                triple("meta", False), triple("conf_dw_w", True), triple("short_dw_w", True))
    for nm, res in zip(vec_names + ["meta"], small[:len(vec_names) + 1]):
        results[nm] = res
    results["conf_dw_w"] = tuple(r[None] for r in small[-2])
    results["short_dw_w"] = tuple(r[None] for r in small[-1])
    update("w_in", [reduce_finish("in%d" % part, red_in[part], [small[0][0]])[0] for part in range(2)])

    order = ["meta", "g_pre_mix", "w_in", "b_gates", "conf_dw_w", "conf_dw_b", "conf_ln_g", "conf_ln_b", "conf_w_pw",
             "short_dw_w", "short_w_out", "w_o", "g_post_mix", "g_pre_mlp", "w_up", "w_down", "g_post_mlp"]
    return (loss, grad_x, *[results[nm][0] for nm in order], *[results[nm][1] for nm in order],
            *[results[nm][2] for nm in order], *[results[nm][3] for nm in order])
```

```python
import jax
import jax.numpy as jnp
from jax import lax
from jax.experimental import pallas as pl
from jax.experimental.pallas import tpu as pltpu

N_DEV = 8
N_META = 16
CONF_K = 31
SHORT_K = 3
RMS_EPS = 1e-6
LN_EPS = 1e-5
ADAM_LR = 0.001
ADAM_B1 = 0.9
ADAM_B2 = 0.999
ADAM_EPS = 1e-08
ADAM_WD = 0.01
ADAM_STEP = 10

LANE = 128
SUB = 8
ROW_TILE = 128
CONV_PAD = 32
CONV_CHUNK = 128
VMEM_LIMIT = 56 * 1024 * 1024

F32 = jnp.float32
BF16 = jnp.bfloat16
MESH = pl.DeviceIdType.MESH
ANY = pl.BlockSpec(memory_space=pl.ANY)
HBM_SPEC = pl.BlockSpec(memory_space=pltpu.HBM)
SEM_SPEC = pl.BlockSpec(memory_space=pltpu.SEMAPHORE)
EFFECT = pltpu.SideEffectType.DATAFLOW_SIDE_EFFECTING


def _params(n_axes):
    return pltpu.CompilerParams(dimension_semantics=("arbitrary",) * n_axes, vmem_limit_bytes=VMEM_LIMIT)


def _sigmoid(z):
    return 1.0 / (1.0 + jnp.exp(-z))


def _colsum8(v):
    r, c = v.shape
    return jnp.sum(v.reshape(r // SUB, SUB, c), axis=0)


def _position():
    x, y, c = lax.axis_index("x"), lax.axis_index("y"), lax.axis_index("c")
    return x, y, c


def _flat(p):
    return 4 * p[0] + 2 * p[1] + p[2]


def _all_gather(name, shards, deps=()):
    n, nd = len(shards), len(deps)

    def body(*refs):
        ins, outs = refs[:n], refs[n + nd:2 * n + nd]
        send_sems, recv_sems, local_sems = refs[2 * n + nd:]
        x, y, c = _position()
        me, sibling = (x, y, c), (x, y, 1 - c)
        chips = [(1 - x, y), (x, 1 - y), (1 - x, 1 - y)]

        def copy(q, k, block, to, src=None):
            dst = outs[q].at[_flat(block)]
            return pltpu.make_async_remote_copy(
                src_ref=dst if src is None else src, dst_ref=dst,
                send_sem=send_sems.at[q, k], recv_sem=recv_sems.at[q, k],
                device_id=to, device_id_type=MESH)

        mine = [pltpu.make_async_copy(ins[q], outs[q].at[_flat(me)], local_sems.at[q]) for q in range(n)]
        for cp in mine:
            cp.start()
        first = []
        for q in range(n):
            first.append(copy(q, 0, me, sibling, src=ins[q]))
            for j, chip in enumerate(chips):
                first.append(copy(q, 1 + j, me, (*chip, c), src=ins[q]))
        for cp in first:
            cp.start()
        passed = []
        for q in range(n):
            for j, chip in enumerate(chips):
                copy(q, 1 + j, (*chip, c), me).wait_recv()
                fwd = copy(q, 4 + j, (*chip, c), sibling)
                fwd.start()
                passed.append(fwd)
        for q in range(n):
            copy(q, 0, sibling, me).wait_recv()
            for j, chip in enumerate(chips):
                copy(q, 4 + j, (*chip, 1 - c), me).wait_recv()
        for cp in first + passed:
            cp.wait_send()
        for cp in mine:
            cp.wait()

    return pl.pallas_call(
        body, name=name,
        in_specs=[ANY] * (n + nd), out_specs=[ANY] * n,
        out_shape=[jax.ShapeDtypeStruct((N_DEV,) + s.shape, s.dtype) for s in shards],
        scratch_shapes=[pltpu.SemaphoreType.DMA((n, 7)), pltpu.SemaphoreType.DMA((n, 7)),
                        pltpu.SemaphoreType.DMA((n,))],
    )(*shards, *deps)


N_COPIES = {"gather_ici": 4, "gather_d2d": 3, "gather_direct": 3, "gather_relay": 3, "gather_diag": 1,
            "reduce_d2d": 4, "reduce_ici": 3}


def _copy_plan(kind):
    x, y, c = _position()
    me, sibling = (x, y, c), (x, y, 1 - c)
    chips = [(1 - x, y), (x, 1 - y), (1 - x, 1 - y)]
    if kind == "gather_ici":
        return [(_flat(me), _flat(me), sibling)] + [(_flat(me), _flat(me), (*ch, c)) for ch in chips]
    if kind == "gather_d2d":
        return [(_flat((*ch, c)), _flat((*ch, c)), sibling) for ch in chips]
    if kind == "gather_direct":
        return [(_flat(me), _flat(me), sibling)] + [(_flat(me), _flat(me), (*ch, c)) for ch in chips[:2]]
    if kind == "gather_relay":
        held, to = (x ^ (1 - c), y ^ c, c), (x ^ c, y ^ (1 - c), c)
        return [(_flat(held), _flat(held), to)] + [(_flat((*ch, c)), _flat((*ch, c)), sibling) for ch in chips[:2]]
    if kind == "gather_diag":
        return [(_flat((*chips[2], c)), _flat((*chips[2], c)), sibling)]
    if kind == "reduce_d2d":
        return [(2 * chip + (1 - c), chip, sibling) for chip in range(4)]
    return [(2 * ch[0] + ch[1], 2 * x + y, (*ch, c)) for ch in chips]


def _planned_copies(kind, srcs, dsts, send_sems, recv_sems):
    plan = _copy_plan(kind)
    return [pltpu.make_async_remote_copy(
        src_ref=src.at[s_slot], dst_ref=dst.at[d_slot],
        send_sem=send_sems.at[q * len(plan) + k], recv_sem=recv_sems.at[q * len(plan) + k],
        device_id=to, device_id_type=MESH)
        for q, (src, dst) in enumerate(zip(srcs, dsts)) for k, (s_slot, d_slot, to) in enumerate(plan)]


def _remote_start(name, kind, srcs, lands=None, deps=()):
    n = len(srcs)
    bufs = list(srcs) + ([] if lands is None else list(lands))
    nb, nd = len(bufs), len(deps)
    nsem = n * N_COPIES[kind]

    def body(*refs):
        ins = refs[:nb]
        send_sems, recv_sems = refs[nb + nd], refs[nb + nd + 1]
        token = refs[-1]
        for cp in _planned_copies(kind, ins[:n], ins[:n] if lands is None else ins[n:], send_sems, recv_sems):
            cp.start()
        token[...] = jnp.zeros_like(token)

    outs = pl.pallas_call(
        body, name=name,
        out_shape=(pltpu.SemaphoreType.DMA((nsem,)), pltpu.SemaphoreType.DMA((nsem,)),
                   *[pltpu.HBM(b.shape, b.dtype) for b in bufs], jax.ShapeDtypeStruct((SUB, LANE), F32)),
        in_specs=[HBM_SPEC] * nb + [ANY] * nd,
        out_specs=(SEM_SPEC, SEM_SPEC, *[HBM_SPEC] * nb, pl.BlockSpec(memory_space=pltpu.VMEM)),
        input_output_aliases={i: 2 + i for i in range(nb)},
        compiler_params=pltpu.CompilerParams(has_side_effects=EFFECT),
    )(*[pltpu.with_memory_space_constraint(b, pltpu.HBM) for b in bufs], *deps)
    return outs[0], outs[1], list(outs[2:2 + nb]), outs[-1]


def _remote_wait(name, kind, send_sems, recv_sems, bufs, n, after):
    nb, na = len(bufs), len(after)
    same = nb == n

    def body(*refs):
        ins = refs[:nb]
        sends, recvs = refs[nb], refs[nb + 1]
        for cp in _planned_copies(kind, ins[:n], ins[:n] if same else ins[n:], sends, recvs):
            cp.wait_send()
            cp.wait_recv()

    outs = pl.pallas_call(
        body, name=name,
        out_shape=[pltpu.HBM(b.shape, b.dtype) for b in bufs],
        in_specs=[HBM_SPEC] * nb + [SEM_SPEC, SEM_SPEC] + [ANY] * na,
        out_specs=[HBM_SPEC] * nb,
        input_output_aliases={i: i for i in range(nb)},
        compiler_params=pltpu.CompilerParams(has_side_effects=EFFECT),
    )(*bufs, send_sems, recv_sems, *after)
    return list(outs)


def _remote_pass_on(name, done, send_sems, recv_sems, bufs, after, nxt):
    nb, na = len(bufs), len(after)
    nsem = nb * N_COPIES[nxt]

    def body(*refs):
        ins = refs[:nb]
        new_sends, new_recvs = refs[nb + 2 + na], refs[nb + 3 + na]
        token = refs[-1]
        for cp in _planned_copies(done, ins, ins, refs[nb], refs[nb + 1]):
            cp.wait_send()
            cp.wait_recv()
        for cp in _planned_copies(nxt, ins, ins, new_sends, new_recvs):
            cp.start()
        token[...] = jnp.zeros_like(token)

    outs = pl.pallas_call(
        body, name=name,
        out_shape=(pltpu.SemaphoreType.DMA((nsem,)), pltpu.SemaphoreType.DMA((nsem,)),
                   *[pltpu.HBM(b.shape, b.dtype) for b in bufs], jax.ShapeDtypeStruct((SUB, LANE), F32)),
        in_specs=[HBM_SPEC] * nb + [SEM_SPEC, SEM_SPEC] + [ANY] * na,
        out_specs=(SEM_SPEC, SEM_SPEC, *[HBM_SPEC] * nb, pl.BlockSpec(memory_space=pltpu.VMEM)),
        input_output_aliases={i: 2 + i for i in range(nb)},
        compiler_params=pltpu.CompilerParams(has_side_effects=EFFECT),
    )(*bufs, send_sems, recv_sems, *after)
    return outs[0], outs[1], list(outs[2:2 + nb]), outs[-1]


def _mm_cols(name, a, w, *, tm, blocks, epilogue, out_dtypes, into=(), deps=()):
    t, k = a.shape
    nblk, _, cb = w.shape
    j0, j1 = blocks
    no = len(out_dtypes)

    def body(a_ref, w_ref, *rest):
        acc = jnp.dot(a_ref[...], w_ref[0], preferred_element_type=F32)
        for o_ref, o in zip(rest[len(into) + len(deps):], epilogue(acc)):
            o_ref[...] = o.astype(o_ref.dtype)

    return pl.pallas_call(
        body, name=name, grid=(j1 - j0, t // tm),
        in_specs=[pl.BlockSpec((tm, k), lambda j, i: (i, 0)),
                  pl.BlockSpec((1, k, cb), lambda j, i: (j0 + j, 0, 0))] + [ANY] * (len(into) + len(deps)),
        out_specs=[pl.BlockSpec((tm, cb), lambda j, i: (i, j0 + j)) for _ in range(no)],
        out_shape=[jax.ShapeDtypeStruct((t, nblk * cb), dt) for dt in out_dtypes],
        input_output_aliases={2 + idx: idx for idx in range(len(into))},
        compiler_params=_params(2),
    )(a, w, *into, *deps)


MXU_WIDTH = 256


def _mm_cols_pairs(name, a, w, *, tm):
    t, k = a.shape
    nblk, _, cb = w.shape
    main = cb // MXU_WIDTH * MXU_WIDTH
    tail = cb - main
    assert 2 * tail == MXU_WIDTH and nblk % 2 == 0

    def body(a_ref, w_ref, o_ref):
        av = a_ref[...]
        for b in range(2):
            o_ref[:, b * cb:b * cb + main] = jnp.dot(av, w_ref[b, :, 0:main], preferred_element_type=F32)
        tails = jnp.dot(av, jnp.concatenate([w_ref[0, :, main:cb], w_ref[1, :, main:cb]], axis=1),
                        preferred_element_type=F32)
        for b in range(2):
            o_ref[:, b * cb + main:(b + 1) * cb] = tails[:, b * tail:(b + 1) * tail]

    return pl.pallas_call(
        body, name=name, grid=(nblk // 2, t // tm),
        in_specs=[pl.BlockSpec((tm, k), lambda j, i: (i, 0)),
                  pl.BlockSpec((2, k, cb), lambda j, i: (j, 0, 0))],
        out_specs=pl.BlockSpec((tm, 2 * cb), lambda j, i: (i, j)),
        out_shape=jax.ShapeDtypeStruct((t, nblk * cb), F32),
        compiler_params=_params(2),
    )(a, w)


def _add_columns(o_ref, j, tn, acc, first):
    for jj in range(o_ref.shape[1] // tn):
        cols = slice(jj * tn, (jj + 1) * tn)

        @pl.when(jnp.logical_and(j == jj, first))
        def _(cols=cols):
            o_ref[:, cols] = acc

        @pl.when(jnp.logical_and(j == jj, jnp.logical_not(first)))
        def _(cols=cols):
            o_ref[:, cols] += acc


def _mm_rows(name, a, w2d, *, tm, tn, kparts=2):
    t, kf = a.shape
    n = w2d.shape[1]
    kp = kf // kparts

    def body(a_ref, w_ref, o_ref):
        acc = jnp.dot(a_ref[...], w_ref[...], preferred_element_type=F32)
        _add_columns(o_ref, pl.program_id(2), tn, acc, pl.program_id(1) == 0)

    return pl.pallas_call(
        body, name=name, grid=(t // tm, kparts, n // tn),
        in_specs=[pl.BlockSpec((tm, kp), lambda i, kh, j: (i, kh)),
                  pl.BlockSpec((kp, tn), lambda i, kh, j: (kh, j))],
        out_specs=pl.BlockSpec((tm, n), lambda i, kh, j: (i, 0)),
        out_shape=jax.ShapeDtypeStruct((t, n), F32),
        compiler_params=_params(3),
    )(a, w2d)


def _mm_nt_acc_parts(name, dy, w, *, tm, tn, kparts=2, deps=()):
    t = dy.shape[0]
    nblk, k, cb = w.shape
    per = nblk // kparts
    assert cb % MXU_WIDTH == 0

    def body(dy_ref, w_ref, *rest):
        acc = None
        for b in range(per):
            d = lax.dot_general(dy_ref[:, b * cb:(b + 1) * cb], w_ref[b], (((1,), (1,)), ((), ())),
                                preferred_element_type=F32)
            acc = d if acc is None else acc + d
        _add_columns(rest[-1], pl.program_id(2), tn, acc, pl.program_id(1) == 0)

    return pl.pallas_call(
        body, name=name, grid=(t // tm, kparts, k // tn),
        in_specs=[pl.BlockSpec((tm, per * cb), lambda i, kh, j: (i, kh)),
                  pl.BlockSpec((per, tn, cb), lambda i, kh, j: (kh, j, 0))] + [ANY] * len(deps),
        out_specs=pl.BlockSpec((tm, k), lambda i, kh, j: (i, 0)),
        out_shape=jax.ShapeDtypeStruct((t, k), F32),
        compiler_params=_params(3),
    )(dy, w, *deps)


def _mm_nt_acc(name, dy, w, *, tm, tn, col_off=0, deps=()):
    t = dy.shape[0]
    nblk, k, cb = w.shape

    main = cb // MXU_WIDTH * MXU_WIDTH

    def body(dy_ref, w_ref, *rest):
        nt = (((1,), (1,)), ((), ()))
        acc = None
        for b in range(nblk):
            d = lax.dot_general(dy_ref[:, b * cb:b * cb + main], w_ref[b, :, 0:main], nt, preferred_element_type=F32)
            acc = d if acc is None else acc + d
        if main < cb:
            dy_tails = jnp.concatenate([dy_ref[:, b * cb + main:(b + 1) * cb] for b in range(nblk)], axis=1)
            w_tails = jnp.concatenate([w_ref[b, :, main:cb] for b in range(nblk)], axis=1)
            acc = acc + lax.dot_general(dy_tails, w_tails, nt, preferred_element_type=F32)
        rest[-1][...] = acc

    return pl.pallas_call(
        body, name=name, grid=(t // tm, k // tn),
        in_specs=[pl.BlockSpec((tm, nblk * cb), lambda i, j: (i, col_off)),
                  pl.BlockSpec((nblk, tn, cb), lambda i, j: (0, j, 0))] + [ANY] * len(deps),
        out_specs=pl.BlockSpec((tm, tn), lambda i, j: (i, j)),
        out_shape=jax.ShapeDtypeStruct((t, k), F32),
        compiler_params=_params(2),
    )(dy, w, *deps)


def _mm_nt_blocks(name, dy, w2d, *, tm, tkb, extra=(), epilogue=None, out_dtypes=(F32,)):
    t, n = dy.shape
    kf = w2d.shape[0]
    ne = len(extra)

    def body(dy_ref, w_ref, *rest):
        acc = lax.dot_general(dy_ref[...], w_ref[...], (((1,), (1,)), ((), ())), preferred_element_type=F32)
        outs = (acc,) if epilogue is None else epilogue(acc, *[e[...] for e in rest[:ne]])
        for o_ref, o in zip(rest[ne:], outs):
            o_ref[...] = o.astype(o_ref.dtype)

    return pl.pallas_call(
        body, name=name, grid=(kf // tkb, t // tm),
        in_specs=[pl.BlockSpec((tm, n), lambda kb, i: (i, 0)),
                  pl.BlockSpec((tkb, n), lambda kb, i: (kb, 0))]
                 + [pl.BlockSpec((tm, tkb), lambda kb, i: (i, kb)) for _ in extra],
        out_specs=[pl.BlockSpec((tm, tkb), lambda kb, i: (i, kb)) for _ in out_dtypes],
        out_shape=[jax.ShapeDtypeStruct((t, kf), dt) for dt in out_dtypes],
        compiler_params=_params(2),
    )(dy, w2d, *extra)


def _mm_tn(name, a, b, me_arr, *, m, n, tma, tn, sharded, a_off=0, b_off=0, deps=()):
    t = a.shape[0]
    if sharded == "cols":
        cb = n // N_DEV
        nb, q = max(tn // cb, 1), max(cb // tn, 1)
        tw = tn // nb
        full_shape, own_shape = (N_DEV, m, cb), (m, cb)
        full_spec = pl.BlockSpec((nb, tma, tw), lambda i, j, me: (j // q, i, j % q))
    else:
        kb = m // N_DEV
        p = kb // tma
        nb, tw = 1, tn
        full_shape, own_shape = (m, n), (kb, n)
        full_spec = pl.BlockSpec((tma, tn), lambda i, j, me: (i, j))

    def body(me_ref, a_ref, b_ref, *rest):
        full_ref, own_ref, stage, sem, pending = rest[len(deps):]
        i, j = pl.program_id(0), pl.program_id(1)

        def own_copy(r0, c0):
            return pltpu.make_async_copy(
                stage, own_ref.at[pl.ds(pl.multiple_of(r0, tma), tma), pl.ds(pl.multiple_of(c0, tw), tw)], sem)

        def drain():
            @pl.when(pending[0] == 1)
            def _():
                own_copy(0, 0).wait()
                pending[0] = 0

        @pl.when(jnp.logical_and(i == 0, j == 0))
        def _():
            pending[0] = 0

        acc = lax.dot_general(a_ref[...], b_ref[...], (((0,), (0,)), ((), ())), preferred_element_type=F32)
        for blk in range(nb):
            part = acc[:, blk * tw:(blk + 1) * tw]
            if sharded == "cols":
                full_ref[blk] = part.astype(BF16)
                owner, r0, c0 = (j // q) * nb + blk, i * tma, (j % q) * tw
            else:
                full_ref[...] = part.astype(BF16)
                owner, r0, c0 = i // p, (i % p) * tma, j * tn

            @pl.when(owner == me_ref[0])
            def _():
                drain()
                stage[...] = part
                own_copy(r0, c0).start()
                pending[0] = 1

        @pl.when(jnp.logical_and(i == pl.num_programs(0) - 1, j == pl.num_programs(1) - 1))
        def _():
            drain()

    full, own = pl.pallas_call(
        body, name=name,
        grid_spec=pltpu.PrefetchScalarGridSpec(
            num_scalar_prefetch=1, grid=(m // tma, n // tn),
            in_specs=[pl.BlockSpec((t, tma), lambda i, j, me: (0, a_off + i)),
                      pl.BlockSpec((t, tn), lambda i, j, me: (0, b_off + j))] + [ANY] * len(deps),
            out_specs=[full_spec, ANY],
            scratch_shapes=[pltpu.VMEM((tma, tw), F32), pltpu.SemaphoreType.DMA(()), pltpu.SMEM((1,), jnp.int32)]),
        out_shape=[jax.ShapeDtypeStruct(full_shape, BF16), jax.ShapeDtypeStruct(own_shape, F32)],
        compiler_params=_params(2),
    )(me_arr, a, b, *deps)
    if sharded == "rows":
        full = full.reshape(N_DEV, m // N_DEV, n)
    return full, own


def _row_tile(t):
    return t // 8 if (t // 8) % 16 == 0 else ROW_TILE


RING_SLOTS = 4


def _row_call_ring(name, body, t, row_ins, full_ins, row_outs, acc_outs, scratch=(), deps=()):
    tm = _row_tile(t)
    nt = t // tm
    nrow, nfull, nd, nsc = len(row_ins), len(full_ins), len(deps), len(scratch)
    nout = len(row_outs) + len(acc_outs)

    def ringed(*refs):
        hbm = refs[:nrow]
        full = refs[nrow:nrow + nfull]
        outs = refs[nrow + nfull + nd:nrow + nfull + nd + nout + nsc]
        rings = refs[nrow + nfull + nd + nout + nsc:-1]
        sems = refs[-1]
        i = pl.program_id(0)

        def fetch(step, slot):
            rows = pl.ds(pl.multiple_of(step * tm, tm), tm)
            return [pltpu.make_async_copy(hbm[q].at[rows, :], rings[q].at[slot], sems.at[slot, q])
                    for q in range(nrow)]

        @pl.when(i == 0)
        def _():
            for ahead in range(min(RING_SLOTS - 1, nt)):
                for cp in fetch(ahead, ahead):
                    cp.start()

        @pl.when(i + RING_SLOTS - 1 < nt)
        def _():
            for cp in fetch(i + RING_SLOTS - 1, (i + RING_SLOTS - 1) % RING_SLOTS):
                cp.start()

        slot = i % RING_SLOTS
        for cp in fetch(i, slot):
            cp.wait()
        body(*[ring.at[slot] for ring in rings], *full, *outs)

    return pl.pallas_call(
        ringed, name=name, grid=(nt,),
        in_specs=[ANY] * nrow + [pl.BlockSpec(a.shape, lambda i: (0, 0)) for a in full_ins] + [ANY] * nd,
        out_specs=[pl.BlockSpec((tm, c), lambda i: (i, 0)) for c, _ in row_outs]
                  + [pl.BlockSpec((r, c), lambda i: (0, 0)) for r, c in acc_outs],
        out_shape=[jax.ShapeDtypeStruct((t, c), dt) for c, dt in row_outs]
                  + [jax.ShapeDtypeStruct((r, c), F32) for r, c in acc_outs],
        scratch_shapes=list(scratch) + [pltpu.VMEM((RING_SLOTS, tm, a.shape[1]), a.dtype) for a in row_ins]
                       + [pltpu.SemaphoreType.DMA((RING_SLOTS, nrow))],
        compiler_params=_params(1),
    )(*row_ins, *full_ins, *deps)


def _accumulate(ref, v):
    @pl.when(pl.program_id(0) == 0)
    def _():
        ref[...] = v

    @pl.when(pl.program_id(0) > 0)
    def _():
        ref[...] += v


def _rms(v):
    return lax.rsqrt(jnp.mean(v * v, axis=-1, keepdims=True) + RMS_EPS)


def _rms_bwd(dout, u, r, g):
    du = dout * g
    dx = r * (du - u * jnp.mean(du * u, axis=-1, keepdims=True))
    return dx, _colsum8(dout * u)


def _pre_norm(h0, g):
    t, d = h0.shape

    def body(h_ref, g_ref, n_ref):
        h = h_ref[...]
        n_ref[...] = (h * _rms(h) * g_ref[...]).astype(BF16)

    return _row_call_ring("pre_norm", body, t, [h0], [g], [(d, BF16)], [])[0]


def _mix_post(m_mix, wo_full, h0, g_post, g_pre, deps=()):
    t, d = h0.shape
    tm = _row_tile(t)

    def body(m_ref, wo_ref, h0_ref, gp_ref, gq_ref, *rest):
        mix_ref, h1_ref, n2_ref = rest[len(deps):]
        mix_v = jnp.dot(m_ref[...], wo_ref[...], preferred_element_type=F32)
        mix_ref[...] = mix_v
        h1 = h0_ref[...] + mix_v * _rms(mix_v) * gp_ref[...]
        h1_ref[...] = h1
        n2_ref[...] = (h1 * _rms(h1) * gq_ref[...]).astype(BF16)

    tile = pl.BlockSpec((tm, d), lambda i: (i, 0))
    gain = pl.BlockSpec((1, d), lambda i: (0, 0))
    return pl.pallas_call(
        body, name="mix_post", grid=(t // tm,),
        in_specs=[tile, pl.BlockSpec((d, d), lambda i: (0, 0)), tile, gain, gain] + [ANY] * len(deps),
        out_specs=[tile, tile, tile],
        out_shape=[jax.ShapeDtypeStruct((t, d), F32), jax.ShapeDtypeStruct((t, d), F32),
                   jax.ShapeDtypeStruct((t, d), BF16)],
        compiler_params=_params(1),
    )(m_mix, wo_full, h0, g_post, g_pre, *deps)


def _loss_head(fo, h1, tgt, g_post_mlp, t_real):
    t, d = h1.shape
    tile = _row_tile(t)

    def body(fo_ref, h1_ref, tgt_ref, g_ref, dfo_ref, dh2_ref, dg_ref, loss_ref, lacc):
        i = pl.program_id(0)
        fo_v = fo_ref[...]
        g = g_ref[...]
        r = _rms(fo_v)
        u = fo_v * r
        h2 = h1_ref[...] + u * g
        row = i * tile + lax.broadcasted_iota(jnp.int32, (tile, 1), 0)
        valid = jnp.logical_and(row >= N_META, row < t_real)
        diff = jnp.where(valid, h2 - tgt_ref[...], 0.0)
        dh2 = diff * (1.0 / d)
        dh2_ref[...] = dh2
        dfo, dg = _rms_bwd(dh2, u, r, g)
        dfo_ref[...] = dfo.astype(BF16)
        _accumulate(dg_ref, dg)
        _accumulate(lacc, _colsum8(diff * diff))

        @pl.when(i == pl.num_programs(0) - 1)
        def _():
            loss_ref[...] = jnp.full((SUB, LANE), (0.5 / d) * jnp.sum(lacc[...]), F32)

    return _row_call_ring("loss_head", body, t, [fo, h1, tgt], [g_post_mlp],
                     [(d, BF16), (d, F32)], [(SUB, d), (SUB, LANE)], scratch=[pltpu.VMEM((SUB, d), F32)])


def _mid_norm_bwd(dn2, h1, dh2, mix, g_pre_mlp, g_post_mix, deps=()):
    t, d = h1.shape

    def body(dn2_ref, h1_ref, dh2_ref, mix_ref, gq_ref, gp_ref, dh1_ref, dmix_ref, dgq_ref, dgp_ref):
        h1 = h1_ref[...]
        r3 = _rms(h1)
        dx, dgq = _rms_bwd(dn2_ref[...], h1 * r3, r3, gq_ref[...])
        dh1 = dh2_ref[...] + dx
        dh1_ref[...] = dh1
        mix_v = mix_ref[...]
        r2 = _rms(mix_v)
        dmix, dgp = _rms_bwd(dh1, mix_v * r2, r2, gp_ref[...])
        dmix_ref[...] = dmix.astype(BF16)
        _accumulate(dgq_ref, dgq)
        _accumulate(dgp_ref, dgp)

    return _row_call_ring("mid_norm_bwd", body, t, [dn2, h1, dh2, mix], [g_pre_mlp, g_post_mix],
                     [(d, F32), (d, BF16)], [(SUB, d), (SUB, d)], deps=deps)


def _pre_norm_bwd(dn, h0, dh1, g_pre_mix, t_real):
    t, d = h0.shape
    tm = _row_tile(t)
    nt = t // tm
    seq = t_real - N_META
    tail = t_real - (nt - 1) * tm
    assert tm > N_META and N_META % SUB == 0 and 0 < tail <= tm and tail % SUB == 0

    def body(dn_ref, h0_ref, dh1_ref, g_ref, gx_ref, dmeta_ref, dg_ref, stage, sem):
        i = pl.program_id(0)
        h0 = h0_ref[...]
        r = _rms(h0)
        dx, dg = _rms_bwd(dn_ref[...], h0 * r, r, g_ref[...])
        dh0 = dh1_ref[...] + dx
        _accumulate(dg_ref, dg)

        def copy(rows, src0, dst0):
            return pltpu.make_async_copy(stage.at[pl.ds(src0, rows), :], gx_ref.at[pl.ds(dst0, rows), :], sem)

        @pl.when(i == 1)
        def _():
            copy(tm - N_META, N_META, 0).wait()

        @pl.when(i > 1)
        def _():
            copy(tm, 0, 0).wait()

        stage[...] = dh0

        @pl.when(i == 0)
        def _():
            dmeta_ref[...] = dh0[:N_META]
            copy(tm - N_META, N_META, 0).start()

        @pl.when(jnp.logical_and(i > 0, i < nt - 1))
        def _():
            copy(tm, 0, pl.multiple_of(i * tm - N_META, SUB)).start()

        @pl.when(i == nt - 1)
        def _():
            last = copy(tail, 0, (nt - 1) * tm - N_META)
            last.start()
            last.wait()

    tile = pl.BlockSpec((tm, d), lambda i: (i, 0))
    return pl.pallas_call(
        body, name="pre_norm_bwd", grid=(nt,),
        in_specs=[tile, tile, tile, pl.BlockSpec((1, d), lambda i: (0, 0))],
        out_specs=[ANY, pl.BlockSpec((N_META, d), lambda i: (0, 0)), pl.BlockSpec((SUB, d), lambda i: (0, 0))],
        out_shape=[jax.ShapeDtypeStruct((seq, d), F32), jax.ShapeDtypeStruct((N_META, d), F32),
                   jax.ShapeDtypeStruct((SUB, d), F32)],
        scratch_shapes=[pltpu.VMEM((tm, d), F32), pltpu.SemaphoreType.DMA(())],
        compiler_params=_params(1),
    )(dn, h0, dh1, g_pre_mix)


def _layer_norm_silu(a1, ln_g, ln_b, deps=()):
    t, c = a1.shape

    def body(a1_ref, g_ref, b_ref, a3_ref):
        a = a1_ref[...]
        mu = jnp.mean(a, axis=-1, keepdims=True)
        xc = a - mu
        rstd = lax.rsqrt(jnp.mean(xc * xc, axis=-1, keepdims=True) + LN_EPS)
        z = xc * rstd * g_ref[...] + b_ref[...]
        a3_ref[...] = (z * _sigmoid(z)).astype(BF16)

    return _row_call_ring("layer_norm_silu", body, t, [a1], [ln_g, ln_b], [(c, BF16)], [], deps=deps)[0]


def _layer_norm_silu_bwd(da3, a1, ln_g, ln_b, deps=()):
    t, c = a1.shape

    def body(da3_ref, a1_ref, g_ref, b_ref, da1_ref, dg_ref, db_ref):
        a = a1_ref[...]
        g = g_ref[...]
        mu = jnp.mean(a, axis=-1, keepdims=True)
        xc = a - mu
        rstd = lax.rsqrt(jnp.mean(xc * xc, axis=-1, keepdims=True) + LN_EPS)
        xhat = xc * rstd
        z = xhat * g + b_ref[...]
        sg = _sigmoid(z)
        dz = da3_ref[...] * (sg * (1.0 + z * (1.0 - sg)))
        dxhat = dz * g
        da1_ref[...] = rstd * (dxhat - jnp.mean(dxhat, axis=-1, keepdims=True)
                               - xhat * jnp.mean(dxhat * xhat, axis=-1, keepdims=True))
        _accumulate(dg_ref, _colsum8(dz * xhat))
        _accumulate(db_ref, _colsum8(dz))

    return _row_call_ring("layer_norm_silu_bwd", body, t, [da3, a1], [ln_g, ln_b], [(c, F32)], [(SUB, c), (SUB, c)], deps=deps)


def _branch_merge(a3, s, wpw, wso, proj, b_gates, d, deps=()):
    t, cols = proj.shape
    nblk, k, cb = wpw.shape
    w = 1024
    nh = d // w
    per = w // cb
    ga0 = (cols - 2 * d) // w
    tm = _row_tile(t)

    def body(a3_ref, s_ref, wpw_ref, wso_ref, *rest):
        pa_refs, pb_refs, bg_ref = rest[:nh], rest[nh:2 * nh], rest[2 * nh]
        ya_ref, yb_ref, ga_ref, gb_ref, m_ref = rest[2 * nh + 1 + len(deps):]
        a3v, sv = a3_ref[...], s_ref[...]
        for b in range(nblk):
            here = slice(b * cb, (b + 1) * cb)
            local = slice((b % per) * cb, (b % per + 1) * cb)
            ya = jnp.dot(a3v, wpw_ref[b], preferred_element_type=F32)
            yb = jnp.dot(sv, wso_ref[b], preferred_element_type=F32)
            ga = _sigmoid(pa_refs[b // per][:, local] + bg_ref[:, here])
            gb = _sigmoid(pb_refs[b // per][:, local] + bg_ref[:, d + b * cb:d + (b + 1) * cb])
            ya_ref[:, here] = ya.astype(BF16)
            yb_ref[:, here] = yb.astype(BF16)
            ga_ref[:, here] = ga.astype(BF16)
            gb_ref[:, here] = gb.astype(BF16)
            m_ref[:, here] = (ga * ya + gb * yb).astype(BF16)

    tile = pl.BlockSpec((tm, d), lambda i: (i, 0))
    return pl.pallas_call(
        body, name="branch_merge", grid=(t // tm,),
        in_specs=[pl.BlockSpec((tm, k), lambda i: (i, 0)), pl.BlockSpec((tm, k), lambda i: (i, 0)),
                  pl.BlockSpec((nblk, k, cb), lambda i: (0, 0, 0)), pl.BlockSpec((nblk, k, cb), lambda i: (0, 0, 0))]
                 + [pl.BlockSpec((tm, w), lambda i, h=h: (i, ga0 + h)) for h in range(2 * nh)]
                 + [pl.BlockSpec((1, 2 * d), lambda i: (0, 0))] + [ANY] * len(deps),
        out_specs=[tile] * 5,
        out_shape=[jax.ShapeDtypeStruct((t, d), BF16)] * 5,
        compiler_params=_params(1),
    )(a3, s, wpw, wso, *([proj] * (2 * nh)), b_gates, *deps)


def _gate_backward(dmix, wo_full, ga, gb, ya, yb, cols, tm, deps=()):
    t, d = ya.shape
    w = 1024
    nh = d // w
    ga0 = (cols - 2 * d) // w

    def body(dmix_ref, wo_ref, ga_ref, gb_ref, ya_ref, yb_ref, *rest):
        dya_ref, dyb_ref, dp_ref, dba_ref, dbb_ref, stage, sems = rest[len(deps):]
        h, i = pl.program_id(0), pl.program_id(1)
        dm = lax.dot_general(dmix_ref[...], wo_ref[...], (((1,), (1,)), ((), ())), preferred_element_type=F32)
        ga = ga_ref[...].astype(F32)
        gb = gb_ref[...].astype(F32)
        dya_ref[...] = (dm * ga).astype(BF16)
        dyb_ref[...] = (dm * gb).astype(BF16)
        dpa = dm * ya_ref[...].astype(F32) * ga * (1.0 - ga)
        dpb = dm * yb_ref[...].astype(F32) * gb * (1.0 - gb)

        def copies(row0, colblk):
            return [pltpu.make_async_copy(
                stage.at[g], dp_ref.at[pl.ds(pl.multiple_of(row0, tm), tm),
                                       pl.ds(pl.multiple_of((ga0 + g * nh + colblk) * w, w), w)], sems.at[g])
                for g in range(2)]

        @pl.when(jnp.logical_or(h > 0, i > 0))
        def _():
            for cp in copies(0, 0):
                cp.wait()

        stage[0] = dpa.astype(BF16)
        stage[1] = dpb.astype(BF16)
        for cp in copies(i * tm, h):
            cp.start()

        @pl.when(i == 0)
        def _():
            dba_ref[...] = _colsum8(dpa)
            dbb_ref[...] = _colsum8(dpb)

        @pl.when(i > 0)
        def _():
            dba_ref[...] += _colsum8(dpa)
            dbb_ref[...] += _colsum8(dpb)

        @pl.when(jnp.logical_and(h == pl.num_programs(0) - 1, i == pl.num_programs(1) - 1))
        def _():
            for cp in copies(0, 0):
                cp.wait()

    tile = pl.BlockSpec((tm, w), lambda h, i: (i, h))
    return pl.pallas_call(
        body, name="gate_backward", grid=(nh, t // tm),
        in_specs=[pl.BlockSpec((tm, d), lambda h, i: (i, 0)),
                  pl.BlockSpec((w, d), lambda h, i: (h, 0)),
                  tile, tile, tile, tile] + [ANY] * len(deps),
        out_specs=[tile, tile, ANY,
                   pl.BlockSpec((SUB, w), lambda h, i: (0, h)),
                   pl.BlockSpec((SUB, w), lambda h, i: (0, h))],
        out_shape=[jax.ShapeDtypeStruct((t, d), BF16), jax.ShapeDtypeStruct((t, d), BF16),
                   jax.ShapeDtypeStruct((t, cols), BF16),
                   jax.ShapeDtypeStruct((SUB, d), F32), jax.ShapeDtypeStruct((SUB, d), F32)],
        scratch_shapes=[pltpu.VMEM((2, tm, w), BF16), pltpu.SemaphoreType.DMA((2,))],
        compiler_params=_params(2),
    )(dmix, wo_full, ga, gb, ya, yb, *deps)


def _shifted_views(win, offsets):
    n = win.shape[0]
    rotated = {}
    views = {}
    for o in offsets:
        q, r = divmod(o, SUB)
        if r not in rotated:
            rotated[r] = win if r == 0 else pltpu.roll(win, n - r, 0)
        views[o] = rotated[r][q * SUB:q * SUB + CONV_CHUNK]
    return views


def _causal_views(xp_ref, ntap, r0):
    win = xp_ref[pl.ds(r0, CONV_CHUNK + CONV_PAD), :]
    views = _shifted_views(win, [CONV_PAD - (ntap - 1 - k) for k in range(ntap)])
    return [views[CONV_PAD - (ntap - 1 - k)] for k in range(ntap)]


def _causal_conv(xp_ref, w_ref, ntap, r0):
    acc = None
    for k, shifted in enumerate(_causal_views(xp_ref, ntap, r0)):
        term = w_ref[k:k + 1, :] * shifted
        acc = term if acc is None else acc + term
    return acc


def _anticausal_conv(xp_ref, w_ref, ntap, r0):
    win = xp_ref[pl.ds(pl.multiple_of(CONV_PAD + r0, CONV_PAD), CONV_CHUNK + CONV_PAD), :]
    views = _shifted_views(win, [ntap - 1 - k for k in range(ntap)])
    acc = None
    for k in range(ntap):
        term = w_ref[k:k + 1, :] * views[ntap - 1 - k]
        acc = term if acc is None else acc + term
    return acc


def _conv_weight_grad(dw_ref, d_chunk, xp_ref, ntap, r0):
    for k, shifted in enumerate(_causal_views(xp_ref, ntap, r0)):
        dw_ref[k * SUB:(k + 1) * SUB, :] += _colsum8(d_chunk * shifted)


def _zero_pads(ref, t):
    ref[0:CONV_PAD, :] = jnp.zeros((CONV_PAD, LANE), F32)
    ref[CONV_PAD + t:CONV_PAD + t + CONV_PAD, :] = jnp.zeros((CONV_PAD, LANE), F32)


def _for_chunks(t, fn):
    def step(idx, carry):
        fn(pl.multiple_of(idx * CONV_CHUNK, CONV_CHUNK))
        return carry

    lax.fori_loop(0, t // CONV_CHUNK, step, 0)


def _conv_forward(proj, conf_w, conf_b, short_w, dc, deps=()):
    t = proj.shape[0]
    nc = dc // LANE

    def body(av_ref, ag_ref, bg_ref, cg_ref, v_ref, cw_ref, cb_ref, sw_ref, *rest):
        a1_ref, s_ref, xa, xb = rest[len(deps):]
        _zero_pads(xa, t)
        _zero_pads(xb, t)
        xa[CONV_PAD:CONV_PAD + t, :] = av_ref[...] * _sigmoid(ag_ref[...])
        xb[CONV_PAD:CONV_PAD + t, :] = cg_ref[...] * v_ref[...]

        def chunk(r0):
            rs = pl.ds(r0, CONV_CHUNK)
            a1_ref[rs, :] = _causal_conv(xa, cw_ref, CONF_K, r0) + cb_ref[...]
            s_ref[rs, :] = (bg_ref[rs, :] * _causal_conv(xb, sw_ref, SHORT_K, r0)).astype(BF16)

        _for_chunks(t, chunk)

    col = lambda g: pl.BlockSpec((t, LANE), lambda c, g=g: (0, g * nc + c))
    return pl.pallas_call(
        body, name="conv_forward", grid=(nc,),
        in_specs=[col(0), col(1), col(2), col(3), col(4),
                  pl.BlockSpec((CONF_K, LANE), lambda c: (0, c)),
                  pl.BlockSpec((1, LANE), lambda c: (0, c)),
                  pl.BlockSpec((SHORT_K, LANE), lambda c: (0, c))] + [ANY] * len(deps),
        out_specs=[pl.BlockSpec((t, LANE), lambda c: (0, c)), pl.BlockSpec((t, LANE), lambda c: (0, c))],
        out_shape=[jax.ShapeDtypeStruct((t, dc), F32), jax.ShapeDtypeStruct((t, dc), BF16)],
        scratch_shapes=[pltpu.VMEM((t + 2 * CONV_PAD, LANE), F32), pltpu.VMEM((t + 2 * CONV_PAD, LANE), F32)],
        compiler_params=_params(1),
    )(proj, proj, proj, proj, proj, conf_w, conf_b, short_w, *deps)


def _conv_backward(dproj, proj, da1, ds, conf_w, short_w, dc):
    t = proj.shape[0]
    nc = dc // LANE

    def body(dp_in, av_ref, ag_ref, bg_ref, cg_ref, v_ref, da1_ref, ds_ref, cw_ref, sw_ref,
             dp_ref, dcw_ref, dcb_ref, dsw_ref, xa, xb, da, db, stage, sems):
        del dp_in
        c = pl.program_id(0)
        for ref in (xa, xb, da, db):
            _zero_pads(ref, t)
        xa[CONV_PAD:CONV_PAD + t, :] = av_ref[...] * _sigmoid(ag_ref[...])
        xb[CONV_PAD:CONV_PAD + t, :] = cg_ref[...] * v_ref[...]
        da[CONV_PAD:CONV_PAD + t, :] = da1_ref[...]
        dcw_ref[...] = jnp.zeros(dcw_ref.shape, F32)
        dsw_ref[...] = jnp.zeros(dsw_ref.shape, F32)
        dcb_ref[...] = jnp.zeros(dcb_ref.shape, F32)

        def copies(colblk):
            return [pltpu.make_async_copy(
                stage.at[g], dp_ref.at[:, pl.ds(pl.multiple_of((g * nc + colblk) * LANE, LANE), LANE)], sems.at[g])
                for g in range(5)]

        @pl.when(c > 0)
        def _():
            for cp in copies(0):
                cp.wait()

        def through_gate(r0):
            rs = pl.ds(r0, CONV_CHUNK)
            ds_c = ds_ref[rs, :]
            stage[2, rs, :] = (ds_c * _causal_conv(xb, sw_ref, SHORT_K, r0)).astype(BF16)
            db[pl.ds(pl.multiple_of(CONV_PAD + r0, CONV_PAD), CONV_CHUNK), :] = ds_c * bg_ref[rs, :]

        _for_chunks(t, through_gate)

        def through_convs(r0):
            rs = pl.ds(r0, CONV_CHUNK)
            da0 = _anticausal_conv(da, cw_ref, CONF_K, r0)
            sg = _sigmoid(ag_ref[rs, :])
            stage[0, rs, :] = (da0 * sg).astype(BF16)
            stage[1, rs, :] = (da0 * av_ref[rs, :] * sg * (1.0 - sg)).astype(BF16)
            dcv = _anticausal_conv(db, sw_ref, SHORT_K, r0)
            stage[3, rs, :] = (dcv * v_ref[rs, :]).astype(BF16)
            stage[4, rs, :] = (dcv * cg_ref[rs, :]).astype(BF16)
            da1_c = da1_ref[rs, :]
            _conv_weight_grad(dcw_ref, da1_c, xa, CONF_K, r0)
            _conv_weight_grad(dsw_ref, ds_ref[rs, :] * bg_ref[rs, :], xb, SHORT_K, r0)
            dcb_ref[...] += _colsum8(da1_c)

        _for_chunks(t, through_convs)
        for cp in copies(c):
            cp.start()

        @pl.when(c == pl.num_programs(0) - 1)
        def _():
            for cp in copies(0):
                cp.wait()

    col = lambda g: pl.BlockSpec((t, LANE), lambda c, g=g: (0, g * nc + c))
    blk = pl.BlockSpec((t, LANE), lambda c: (0, c))
    return pl.pallas_call(
        body, name="conv_backward", grid=(nc,),
        in_specs=[ANY, col(0), col(1), col(2), col(3), col(4), blk, blk,
                  pl.BlockSpec((CONF_K, LANE), lambda c: (0, c)),
                  pl.BlockSpec((SHORT_K, LANE), lambda c: (0, c))],
        out_specs=[ANY,
                   pl.BlockSpec((CONF_K * SUB, LANE), lambda c: (0, c)),
                   pl.BlockSpec((SUB, LANE), lambda c: (0, c)),
                   pl.BlockSpec((SHORT_K * SUB, LANE), lambda c: (0, c))],
        out_shape=[jax.ShapeDtypeStruct(dproj.shape, dproj.dtype),
                   jax.ShapeDtypeStruct((CONF_K * SUB, dc), F32),
                   jax.ShapeDtypeStruct((SUB, dc), F32),
                   jax.ShapeDtypeStruct((SHORT_K * SUB, dc), F32)],
        scratch_shapes=[pltpu.VMEM((t + 2 * CONV_PAD, LANE), F32)] * 4
                       + [pltpu.VMEM((5, t, LANE), BF16), pltpu.SemaphoreType.DMA((5,))],
        input_output_aliases={0: 0},
        compiler_params=_params(1),
    )(dproj, proj, proj, proj, proj, proj, da1, ds, conf_w, short_w)


def _adamw_math(w, g, m, v):
    m = ADAM_B1 * m + (1.0 - ADAM_B1) * g
    v = ADAM_B2 * v + (1.0 - ADAM_B2) * (g * g)
    m_hat = m / (1.0 - ADAM_B1 ** ADAM_STEP)
    v_hat = v / (1.0 - ADAM_B2 ** ADAM_STEP)
    delta = -ADAM_LR * (m_hat / (jnp.sqrt(v_hat) + ADAM_EPS) + ADAM_WD * w)
    return delta, m, v


def _cast_into_slot(name, w, me_arr, deps=()):
    r, c = w.shape
    tr = 256

    def body(me_ref, w_ref, *rest):
        del me_ref
        rest[-1][0] = w_ref[...].astype(BF16)

    return pl.pallas_call(
        body, name=name,
        grid_spec=pltpu.PrefetchScalarGridSpec(
            num_scalar_prefetch=1, grid=(r // tr,),
            in_specs=[pl.BlockSpec((tr, c), lambda i, me: (i, 0))] + [ANY] * len(deps),
            out_specs=pl.BlockSpec((1, tr, c), lambda i, me: (me[0], i, 0))),
        out_shape=jax.ShapeDtypeStruct((N_DEV, r, c), BF16),
        compiler_params=_params(1),
    )(me_arr, w, *deps)


def _chip_sum(name, full, from_sibling, me_arr):
    _, r, c = full.shape
    tr = min(r, 1024)

    def body(me_ref, full_ref, sib_ref, sums_ref):
        del me_ref
        sums_ref[0] = (full_ref[0].astype(F32) + sib_ref[0].astype(F32)).astype(BF16)

    other = lambda k, me: (me[0] // 2 + 1 + k) % 4
    return pl.pallas_call(
        body, name=name,
        grid_spec=pltpu.PrefetchScalarGridSpec(
            num_scalar_prefetch=1, grid=(r // tr, 3),
            in_specs=[pl.BlockSpec((1, tr, c), lambda i, k, me: (2 * other(k, me) + me[0] % 2, i, 0)),
                      pl.BlockSpec((1, tr, c), lambda i, k, me: (other(k, me), i, 0))],
            out_specs=pl.BlockSpec((1, tr, c), lambda i, k, me: (other(k, me), i, 0))),
        out_shape=jax.ShapeDtypeStruct((4, r, c), BF16),
        compiler_params=_params(2),
    )(me_arr, full, from_sibling)


def _adamw_shard(name, w, m, v, parts, me_arr, deps=()):
    r, c = w.shape
    tr = min(256, r // len(parts))
    np_ = len(parts)
    per = r // np_ // tr

    def body(me_ref, w_ref, m_ref, v_ref, *rest):
        g_out, d_out, m_out, v_out = rest[5 * np_ + len(deps):]
        g = None
        for p in range(np_):
            gp = rest[5 * p][...]
            for l_ref in rest[5 * p + 1:5 * p + 5]:
                gp = gp + l_ref[0].astype(F32)
            g = gp if g is None else jnp.where(pl.program_id(0) // per == p, gp, g)
        delta, m_new, v_new = _adamw_math(w_ref[...], g, m_ref[...], v_ref[...])
        g_out[...] = g
        d_out[...] = delta
        m_out[...] = m_new
        v_out[...] = v_new

    tile = pl.BlockSpec((tr, c), lambda i, me: (i, 0))
    part_specs, part_args = [], []
    for p, (g_own, from_sibling, landed) in enumerate(parts):
        row = lambda i, p=p: jnp.clip(i - p * per, 0, per - 1)
        part_specs.append(pl.BlockSpec((tr, c), lambda i, me, row=row: (row(i), 0)))
        part_specs += [pl.BlockSpec((1, tr, c), lambda i, me, k=k, row=row: ((me[0] // 2 + k) % 4, row(i), 0))
                       for k in range(4)]
        part_args += [g_own, from_sibling, landed, landed, landed]
    return pl.pallas_call(
        body, name=name,
        grid_spec=pltpu.PrefetchScalarGridSpec(
            num_scalar_prefetch=1, grid=(r // tr,),
            in_specs=[tile] * 3 + part_specs + [ANY] * len(deps), out_specs=[tile] * 4),
        out_shape=[jax.ShapeDtypeStruct((r, c), F32)] * 4,
        compiler_params=_params(1),
    )(me_arr, w, m, v, *part_args, *deps)


SMALL_W = 1024
VEC_ROWS = 16
LOSS_ROW = 15
META_ROW0 = 16
CONF_ROW0 = 64
SHORT_ROW0 = 96
SMALL_ROWS = 104


def _pack_small(vec_parts, dmeta, dcw, dsw, loss_blk, me_arr):
    widths = [p.shape[1] for p in vec_parts]
    nv = len(vec_parts)

    def body(me_ref, *refs):
        del me_ref
        parts, (dmeta_ref, dcw_ref, dsw_ref, loss_ref, out_ref) = refs[:nv], refs[nv:]
        out_ref[0] = jnp.zeros((SMALL_ROWS, SMALL_W), F32)
        out_ref[0, LOSS_ROW:LOSS_ROW + 1, 0:LANE] = loss_ref[0:1, :]
        row = 0
        for p_ref, wd in zip(parts, widths):
            s = jnp.sum(p_ref[...], axis=0, keepdims=True)
            for h in range(wd // SMALL_W):
                out_ref[0, row:row + 1, :] = s[:, h * SMALL_W:(h + 1) * SMALL_W]
                row += 1
        for h in range(dmeta_ref.shape[1] // SMALL_W):
            out_ref[0, META_ROW0 + h * N_META:META_ROW0 + (h + 1) * N_META, :] = dmeta_ref[:, h * SMALL_W:(h + 1) * SMALL_W]
        for k in range(CONF_K):
            out_ref[0, CONF_ROW0 + k:CONF_ROW0 + k + 1, :] = jnp.sum(dcw_ref[k * SUB:(k + 1) * SUB, :], axis=0, keepdims=True)
        for k in range(SHORT_K):
            out_ref[0, SHORT_ROW0 + k:SHORT_ROW0 + k + 1, :] = jnp.sum(dsw_ref[k * SUB:(k + 1) * SUB, :], axis=0, keepdims=True)

    ins = [*vec_parts, dmeta, dcw, dsw, loss_blk]
    return pl.pallas_call(
        body, name="pack_small",
        grid_spec=pltpu.PrefetchScalarGridSpec(
            num_scalar_prefetch=1, grid=(1,),
            in_specs=[pl.BlockSpec(a.shape, lambda i, me: (0, 0)) for a in ins],
            out_specs=pl.BlockSpec((1, SMALL_ROWS, SMALL_W), lambda i, me: (me[0], 0, 0))),
        out_shape=jax.ShapeDtypeStruct((N_DEV, SMALL_ROWS, SMALL_W), F32),
        compiler_params=_params(1),
    )(me_arr, *ins)


def _small_update(gathered, me_arr, vec_params, meta_p, conf_p, short_p):
    widths = [p[0].shape[1] for p in vec_params]
    nv = len(vec_params)
    mcols = meta_p[0].shape[1]
    per_row = SMALL_W // mcols

    def body(me_ref, gv_ref, gm_ref, gc_ref, gs_ref, *rest):
        del me_ref
        ins, outs = rest[:3 * (nv + 3)], rest[3 * (nv + 3):]

        def total(ref, r0, rows):
            s = ref[0, r0:r0 + rows, :]
            for dev in range(1, N_DEV):
                s = s + ref[dev, r0:r0 + rows, :]
            return s

        grads = []
        row = 0
        for wd in widths:
            pieces = [total(gv_ref, row + h, 1) for h in range(wd // SMALL_W)]
            grads.append(pieces[0] if len(pieces) == 1 else jnp.concatenate(pieces, axis=1))
            row += len(pieces)
        grads.append(total(gm_ref, 0, N_META))
        grads.append(total(gc_ref, 0, CONF_K))
        grads.append(total(gs_ref, 0, SHORT_K))
        loss = gv_ref[0, LOSS_ROW:LOSS_ROW + 1, 0:LANE]
        for dev in range(1, N_DEV):
            loss = loss + gv_ref[dev, LOSS_ROW:LOSS_ROW + 1, 0:LANE]
        outs[-1][...] = loss
        for idx, g in enumerate(grads):
            w_ref, m_ref, v_ref = ins[3 * idx:3 * idx + 3]
            delta, m_new, v_new = _adamw_math(w_ref[...], g, m_ref[...], v_ref[...])
            g_out, d_out, m_out, v_out = outs[4 * idx:4 * idx + 4]
            g_out[...] = g
            d_out[...] = delta
            m_out[...] = m_new
            v_out[...] = v_new

    params = list(vec_params) + [meta_p, conf_p, short_p]
    flat = [a for p in params for a in p]
    whole = lambda a: pl.BlockSpec(a.shape, lambda i, me: (0,) * a.ndim)
    outs = pl.pallas_call(
        body, name="small_update",
        grid_spec=pltpu.PrefetchScalarGridSpec(
            num_scalar_prefetch=1, grid=(1,),
            in_specs=[pl.BlockSpec((N_DEV, VEC_ROWS, SMALL_W), lambda i, me: (0, 0, 0)),
                      pl.BlockSpec((N_DEV, N_META, mcols),
                                   lambda i, me: (0, META_ROW0 // N_META + me[0] // per_row, me[0] % per_row)),
                      pl.BlockSpec((N_DEV, 32, LANE), lambda i, me: (0, CONF_ROW0 // 32, me[0])),
                      pl.BlockSpec((N_DEV, SUB, LANE), lambda i, me: (0, SHORT_ROW0 // SUB, me[0]))]
                     + [whole(a) for a in flat],
            out_specs=[whole(p[0]) for p in params for _ in range(4)]
                      + [pl.BlockSpec((1, LANE), lambda i, me: (0, 0))]),
        out_shape=[jax.ShapeDtypeStruct(p[0].shape, F32) for p in params for _ in range(4)]
                  + [jax.ShapeDtypeStruct((1, LANE), F32)],
        compiler_params=_params(1),
    )(me_arr, gathered, gathered, gathered, gathered, *flat)
    return [tuple(outs[4 * i:4 * i + 4]) for i in range(len(params))], outs[-1][0, 0]


def kernel(x, meta, g_pre_mix, w_in, b_gates, conf_dw_w, conf_dw_b, conf_ln_g, conf_ln_b, conf_w_pw, short_dw_w, short_w_out, w_o, g_post_mix, g_pre_mlp, w_up, w_down, g_post_mlp, loss_target, m_meta, m_g_pre_mix, m_w_in, m_b_gates, m_conf_dw_w, m_conf_dw_b, m_conf_ln_g, m_conf_ln_b, m_conf_w_pw, m_short_dw_w, m_short_w_out, m_w_o, m_g_post_mix, m_g_pre_mlp, m_w_up, m_w_down, m_g_post_mlp, v_meta, v_g_pre_mix, v_w_in, v_b_gates, v_conf_dw_w, v_conf_dw_b, v_conf_ln_g, v_conf_ln_b, v_conf_w_pw, v_short_dw_w, v_short_w_out, v_w_o, v_g_post_mix, v_g_pre_mlp, v_w_up, v_w_down, v_g_post_mlp):
    seq, d = x.shape[1], x.shape[2]
    dc = conf_w_pw.shape[1]
    t_real = N_META + seq
    t = -(-t_real // ROW_TILE) * ROW_TILE
    tm = t // 2
    assert tm % 16 == 0 and d % 1024 == 0 and dc % 1024 == 0
    x_idx, y_idx, c_idx = _position()
    me_arr = jnp.reshape(4 * x_idx + 2 * y_idx + c_idx, (1,)).astype(jnp.int32)

    big = [w_in[0], conf_w_pw[0], short_w_out[0], w_o[0], w_up[0], w_down[0]]
    big_names = ["w_in", "conf_w_pw", "short_w_out", "w_o", "w_up", "w_down"]
    groups = [[0], [1, 2, 3], [4], [5]]
    slots, deps = [], []
    for g, idxs in enumerate(groups):
        slots.append([_cast_into_slot("cast_" + big_names[i], big[i], me_arr, deps=deps) for i in idxs])
        if g == 0:
            direct0 = _remote_start("gather0_direct_start", "gather_direct", slots[0])
            deps = [direct0[3]]
    casts = [sl for group in slots[1:] for sl in group]
    meta_g, cw_g, sw_g = _all_gather("gather_small_params", [meta, conf_dw_w[0], short_dw_w[0]], deps=casts)

    def start_direct(g, deps):
        send, recv, bufs, tok = _remote_start("gather%d_direct_start" % g, "gather_direct", slots[g], deps=deps)
        return (send, recv, bufs), tok

    def relay(g, state, after):
        send, recv, bufs, tok = _remote_pass_on("gather%d_relay" % g, "gather_direct", *state, after, "gather_relay")
        return (send, recv, bufs), tok

    def gathered(g, state, after):
        send, recv, bufs, tok = _remote_pass_on("gather%d_diag" % g, "gather_relay", *state, after, "gather_diag")
        return _remote_wait("gather%d_diag_wait" % g, "gather_diag", send, recv, bufs, len(bufs), [tok])

    unshard =lambda g: jnp.transpose(g, (1, 0, 2)).reshape(g.shape[1], -1)
    meta_full, cw_full, sw_full = unshard(meta_g), unshard(cw_g), unshard(sw_g)

    relay0, tok = relay(0, direct0[:3], [meta_g])
    zrows = jnp.zeros((t - t_real, d), F32) + tok[0, 0] * 0.0
    h0 = jnp.concatenate([meta_full, x[0], zrows], axis=0)
    tgt = jnp.concatenate([jnp.zeros((N_META, d), F32), loss_target[0], zrows], axis=0)
    n = _pre_norm(h0, g_pre_mix)
    direct1, tok = start_direct(1, [tok])
    direct2, tok = start_direct(2, [tok])
    win_g, = gathered(0, relay0, [tok, n])
    proj = _mm_cols_pairs("proj", n, win_g, tm=tm // 2)
    relay1, tok = relay(1, direct1, [proj])
    a1, s = _conv_forward(proj, cw_full, conf_dw_b, sw_full, dc, deps=[tok])
    relay2, tok = relay(2, direct2, [a1])
    direct3, tok = start_direct(3, [tok])
    a3 = _layer_norm_silu(a1, conf_ln_g, conf_ln_b, deps=[tok])
    wpw_g, wso_g, wo_g = gathered(1, relay1, [a3])
    wo_full = wo_g.reshape(d, d)
    ya, yb, gate_a, gate_b, m_mix = _branch_merge(a3, s, wpw_g, wso_g, proj, b_gates, d)
    mix, h1, n2 = _mix_post(m_mix, wo_full, h0, g_post_mix, g_pre_mlp)
    wup_g, = gathered(2, relay2, [n2])

    def up_epilogue(acc):
        r = jnp.maximum(acc, 0.0)
        return r * r, r

    half_up = dict(tm=tm, epilogue=up_epilogue, out_dtypes=(BF16, BF16))
    f, relu_up = _mm_cols("mlp_up0", n2, wup_g, blocks=(0, N_DEV // 2), **half_up)
    relay3, tok = relay(3, direct3, [f])
    f, relu_up = _mm_cols("mlp_up1", n2, wup_g, blocks=(N_DEV // 2, N_DEV), into=(f, relu_up), deps=[tok], **half_up)
    wdn_g, = gathered(3, relay3, [f])
    wdn_full = wdn_g.reshape(-1, d)
    fo = _mm_rows("mlp_down", f, wdn_full, tm=tm, tn=512)
    dfo, dh2, dg_post_mlp, loss_blk = _loss_head(fo, h1, tgt, g_post_mlp, t_real)

    def reduce_start(tag, fulls, deps):
        lands = [lax.empty((4,) + g.shape[1:], BF16) for g in fulls]
        send, recv, bufs, tok = _remote_start("reduce_%s_d2d_start" % tag, "reduce_d2d", fulls, lands, deps=deps)
        return (send, recv, bufs), tok

    def reduce_middle(tag, state, owns, after):
        send, recv, bufs = state
        k = len(owns)
        bufs = _remote_wait("reduce_%s_d2d_wait" % tag, "reduce_d2d", send, recv, bufs, k, after)
        from_sibling = bufs[k:]
        sums = [_chip_sum("chip_sum_%s%d" % (tag, i), bufs[i], from_sibling[i], me_arr) for i in range(k)]
        lands = [lax.empty(sm.shape, BF16) for sm in sums]
        send, recv, bufs, tok = _remote_start("reduce_%s_ici_start" % tag, "reduce_ici", sums, lands)
        return (send, recv, bufs, list(zip(owns, from_sibling))), tok

    def reduce_finish(tag, state, after):
        send, recv, bufs, local = state
        k = len(local)
        bufs = _remote_wait("reduce_%s_ici_wait" % tag, "reduce_ici", send, recv, bufs, k, after)
        return [(own, sib, landed) for (own, sib), landed in zip(local, bufs[k:])]

    dup = _mm_nt_blocks("d_up", dfo, wdn_full, tm=tm, tkb=1024, extra=(relu_up,),
                        epilogue=lambda acc, r: (acc * (2.0 * r.astype(F32)),), out_dtypes=(BF16,))[0]
    gw_down, gw_down_own = _mm_tn("dw_down", f, dfo, me_arr, m=f.shape[1], n=d, tma=512, tn=d, sharded="rows")
    red_down, tok = reduce_start("down", [gw_down], ())
    dn2 = _mm_nt_acc_parts("d_n2", dup, wup_g, tm=tm, tn=512, deps=[tok])
    gw_up, gw_up_own = _mm_tn("dw_up", n2, dup, me_arr, m=d, n=dup.shape[1], tma=512, tn=2048, sharded="cols")
    red_down, tok = reduce_middle("down", red_down, [gw_down_own], [dn2])
    red_up, tok = reduce_start("up", [gw_up], [tok])
    dh1, dmix, dg_pre_mlp, dg_post_mix = _mid_norm_bwd(dn2, h1, dh2, mix, g_pre_mlp, g_post_mix, deps=[tok])
    dya, dyb, dproj, db_a, db_b = _gate_backward(dmix, wo_full, gate_a, gate_b, ya, yb, proj.shape[1], tm // 2)
    db_gates = jnp.concatenate([db_a, db_b], axis=1)
    red_up, tok = reduce_middle("up", red_up, [gw_up_own], [dya])
    gw_o, gw_o_own = _mm_tn("dw_o", m_mix, dmix, me_arr, m=d, n=d, tma=d // N_DEV, tn=d, sharded="rows", deps=[tok])
    da3 = _mm_nt_acc("d_a3", dya, wpw_g, tm=tm, tn=512)
    gw_pw, gw_pw_own = _mm_tn("dw_pw", a3, dya, me_arr, m=dc, n=d, tma=512, tn=d, sharded="cols")
    dsb = _mm_nt_acc("d_s", dyb, wso_g, tm=tm, tn=512)
    gw_so, gw_so_own = _mm_tn("dw_so", s, dyb, me_arr, m=dc, n=d, tma=512, tn=d, sharded="cols")
    red_mix, tok = reduce_start("mix", [gw_pw, gw_so, gw_o], ())
    da1, dln_g, dln_b = _layer_norm_silu_bwd(da3, a1, conf_ln_g, conf_ln_b, deps=[tok])
    dproj, dcw, dcb, dsw = _conv_backward(dproj, proj, da1, dsb, cw_full, sw_full, dc)
    red_mix, tok = reduce_middle("mix", red_mix, [gw_pw_own, gw_so_own, gw_o_own], [dcb])
    in_cb = w_in.shape[2]
    half = d // 2
    red_in = []
    for part in range(2):
        gw, own = _mm_tn("dw_in%d" % part, n, dproj, me_arr, m=half, n=proj.shape[1], tma=512, tn=2 * in_cb,
                         sharded="cols", a_off=part * (half // 512), deps=[tok])
        state, tok = reduce_start("in%d" % part, [gw], ())
        red_in.append((state, own))
    for part in range(2):
        state, own = red_in[part]
        red_in[part], tok = reduce_middle("in%d" % part, state, [own], [tok])
    dn = _mm_nt_acc("d_n", dproj, win_g, tm=tm // 2, tn=512, deps=[tok])
    grad_x, dmeta, dg_pre_mix = _pre_norm_bwd(dn, h0, dh1, g_pre_mix, t_real)
    grad_x = grad_x[None]

    vec_parts = [dg_pre_mix, db_gates, dcb, dln_g, dln_b, dg_post_mix, dg_pre_mlp, dg_post_mlp]
    packed = _pack_small(vec_parts, dmeta, dcw, dsw, loss_blk, me_arr)
    send, recv, bufs, tok = _remote_start("small_grads_ici_start", "gather_ici", [packed])
    vec_names = ["g_pre_mix", "b_gates", "conf_dw_b", "conf_ln_g", "conf_ln_b", "g_post_mix", "g_pre_mlp", "g_post_mlp"]
    env = locals()
    results = {}

    def update(nm, parts, deps=()):
        res = _adamw_shard("adamw_" + nm, env[nm][0], env["m_" + nm][0], env["v_" + nm][0], parts, me_arr, deps=deps)
        results[nm] = tuple(r[None] for r in res)
        return res[0]

    done = [update("w_down", reduce_finish("down", red_down, [tok]), deps=[tok])]
    done.append(update("w_up", reduce_finish("up", red_up, done)))
    bufs = _remote_wait("small_grads_ici_wait", "gather_ici", send, recv, bufs, 1, done)
    send, recv, bufs, tok = _remote_start("small_grads_d2d_start", "gather_d2d", bufs)
    for nm, pair in zip(["conf_w_pw", "short_w_out", "w_o"], reduce_finish("mix", red_mix, [tok])):
        done.append(update(nm, [pair], deps=[tok]))
    small_g, = _remote_wait("small_grads_d2d_wait", "gather_d2d", send, recv, bufs, 1, done)
    triple = lambda nm, sq: tuple(env[p + nm][0] if sq else env[p + nm] for p in ("", "m_", "v_"))
    small, loss = _small_update(small_g, me_arr, [triple(nm, False) for nm in vec_names],
                                triple("meta", False), triple("conf_dw_w", True), triple("short_dw_w", True))
    for nm, res in zip(vec_names + ["meta"], small[:len(vec_names) + 1]):
        results[nm] = res
    results["conf_dw_w"] = tuple(r[None] for r in small[-2])
    results["short_dw_w"] = tuple(r[None] for r in small[-1])
    update("w_in", [reduce_finish("in%d" % part, red_in[part], [small[0][0]])[0] for part in range(2)])

    order = ["meta", "g_pre_mix", "w_in", "b_gates", "conf_dw_w", "conf_dw_b", "conf_ln_g", "conf_ln_b", "conf_w_pw",
             "short_dw_w", "short_w_out", "w_o", "g_post_mix", "g_pre_mlp", "w_up", "w_down", "g_post_mlp"]
    return (loss, grad_x, *[results[nm][0] for nm in order], *[results[nm][1] for nm in order],
            *[results[nm][2] for nm in order], *[results[nm][3] for nm in order])
```
